```python
import jax, jax.numpy as jnp
from jax import lax
import numpy as np

D_MODEL = 1024
BATCH = 8
SEQ = 8192
DEPTH = 1

HEAD_DIM = 64
N_HEADS = D_MODEL // HEAD_DIM
N_FOX_HEADS = N_HEADS // 2
N_SB_HEADS = N_HEADS - N_FOX_HEADS
FOX_WIDTH = N_FOX_HEADS * HEAD_DIM
SB_WIDTH = N_SB_HEADS * HEAD_DIM
MIX_WIDTH = FOX_WIDTH + SB_WIDTH
IN_COLS = 3 * FOX_WIDTH + N_FOX_HEADS + 3 * SB_WIDTH
D_FF = ((8 * D_MODEL // 3 + 127) // 128) * 128
CONV_WIDTH = 3
BLOCK_Q = 128
EPS = 1e-6

kernel_name = "hybrid_fox_stickbreaking_convffn"


def rmsnorm(x, g):
    xf = x.astype(jnp.float32)
    y = xf * lax.rsqrt(jnp.mean(xf * xf, axis=-1, keepdims=True) + EPS)
    return (y * g.astype(jnp.float32)).astype(x.dtype)


def split_heads(t, n_heads):
    b, s, _ = t.shape
    return t.reshape(b, s, n_heads, HEAD_DIM).transpose(0, 2, 1, 3)


def merge_heads(t):
    b, h, s, d = t.shape
    return t.transpose(0, 2, 1, 3).reshape(b, s, h * d)


def to_blocks(t):
    b, h, s = t.shape[:3]
    t = t.reshape(b, h, s // BLOCK_Q, BLOCK_Q, *t.shape[3:])
    return jnp.moveaxis(t, 2, 0)


def from_blocks(t):
    nb, b, h, bq, d = t.shape
    return jnp.moveaxis(t, 0, 2).reshape(b, h, nb * bq, d)


def fox_attention(q, k, v, log_f):
    s_len = q.shape[2]
    scale = HEAD_DIM ** -0.5
    kf = k.astype(jnp.float32)
    vf = v.astype(jnp.float32)
    F = jnp.cumsum(log_f, axis=-1)
    kpos = jnp.arange(s_len)

    def block(args):
        i, qi, Fi = args
        qpos = i * BLOCK_Q + jnp.arange(BLOCK_Q)
        logits = (jnp.einsum('bhqd,bhkd->bhqk', qi.astype(jnp.float32), kf) * scale
                  + Fi[..., :, None] - F[:, :, None, :])
        logits = jnp.where(kpos[None, :] <= qpos[:, None], logits, -jnp.inf)
        p = jax.nn.softmax(logits, axis=-1)
        return jnp.einsum('bhqk,bhkd->bhqd', p, vf)

    out = lax.map(block, (jnp.arange(s_len // BLOCK_Q), to_blocks(q), to_blocks(F)))
    return from_blocks(out).astype(v.dtype)


def stick_breaking_attention(q, k, v):
    s_len = q.shape[2]
    scale = HEAD_DIM ** -0.5
    kf = k.astype(jnp.float32)
    vf = v.astype(jnp.float32)
    kpos = jnp.arange(s_len)

    def block(args):
        i, qi = args
        qpos = i * BLOCK_Q + jnp.arange(BLOCK_Q)
        mask = kpos[None, :] < qpos[:, None]
        z = jnp.einsum('bhqd,bhkd->bhqk', qi.astype(jnp.float32), kf) * scale
        log_beta = jax.nn.log_sigmoid(z)
        log_one_minus = jnp.where(mask, jax.nn.log_sigmoid(-z), 0.0)
        after = lax.cumsum(log_one_minus, axis=3, reverse=True) - log_one_minus
        weights = jnp.where(mask, jnp.exp(log_beta + after), 0.0)
        return jnp.einsum('bhqk,bhkd->bhqd', weights, vf)

    out = lax.map(block, (jnp.arange(s_len // BLOCK_Q), to_blocks(q)))
    return from_blocks(out).astype(v.dtype)


def causal_dwconv(u, w, b):
    s_len = u.shape[1]
    up = jnp.pad(u, ((0, 0), (CONV_WIDTH - 1, 0), (0, 0)))
    out = b + w[0] * up[:, 0:s_len]
    for kk in range(1, CONV_WIDTH):
        out = out + w[kk] * up[:, kk:kk + s_len]
    return out


def _fwd_setup_inputs(seed: int = 0) -> dict:
    key = jax.random.key(seed)
    ks = jax.random.split(key, 13)
    f32 = jnp.float32
    x = jax.random.normal(ks[0], (BATCH, SEQ, D_MODEL), f32)
    attn_norm_g = 1.0 + 0.05 * jax.random.normal(ks[1], (DEPTH, D_MODEL), f32)
    w_in = jax.random.normal(ks[2], (DEPTH, D_MODEL, IN_COLS), f32) * D_MODEL ** -0.5
    forget_bias = (jnp.linspace(1.0, 6.0, N_FOX_HEADS, dtype=f32)[None, :]
                   + 0.1 * jax.random.normal(ks[3], (DEPTH, N_FOX_HEADS), f32))
    fox_out_g = 1.0 + 0.05 * jax.random.normal(ks[4], (DEPTH, FOX_WIDTH), f32)
    sb_out_g = 1.0 + 0.05 * jax.random.normal(ks[5], (DEPTH, SB_WIDTH), f32)
    w_out = jax.random.normal(ks[6], (DEPTH, MIX_WIDTH, D_MODEL), f32) * MIX_WIDTH ** -0.5
    ffn_norm_g = 1.0 + 0.05 * jax.random.normal(ks[7], (DEPTH, D_MODEL), f32)
    w_up = jax.random.normal(ks[8], (DEPTH, D_MODEL, 2 * D_FF), f32) * D_MODEL ** -0.5
    conv_w = jax.random.normal(ks[9], (DEPTH, CONV_WIDTH, 2 * D_FF), f32) * CONV_WIDTH ** -0.5
    conv_b = 0.02 * jax.random.normal(ks[10], (DEPTH, 2 * D_FF), f32)
    w_down = jax.random.normal(ks[11], (DEPTH, D_FF, D_MODEL), f32) * D_FF ** -0.5
    final_norm_g = 1.0 + 0.05 * jax.random.normal(ks[12], (D_MODEL,), f32)
    return {"x": x, "attn_norm_g": attn_norm_g, "w_in": w_in, "forget_bias": forget_bias,
            "fox_out_g": fox_out_g, "sb_out_g": sb_out_g, "w_out": w_out,
            "ffn_norm_g": ffn_norm_g, "w_up": w_up, "conv_w": conv_w, "conv_b": conv_b,
            "w_down": w_down, "final_norm_g": final_norm_g}


def _fwd_reference(x, attn_norm_g, w_in, forget_bias, fox_out_g, sb_out_g, w_out,
              ffn_norm_g, w_up, conv_w, conv_b, w_down, final_norm_g):
    b, s_len, _ = x.shape
    splits = [FOX_WIDTH, 2 * FOX_WIDTH, 3 * FOX_WIDTH, 3 * FOX_WIDTH + N_FOX_HEADS,
              3 * FOX_WIDTH + N_FOX_HEADS + SB_WIDTH,
              3 * FOX_WIDTH + N_FOX_HEADS + 2 * SB_WIDTH]
    for l in range(DEPTH):
        h = rmsnorm(x, attn_norm_g[l])
        proj = h @ w_in[l]
        fq, fk, fv, f_logit, sq, sk, sv = jnp.split(proj, splits, axis=-1)
        log_f = jax.nn.log_sigmoid(
            (f_logit + forget_bias[l]).astype(jnp.float32)).transpose(0, 2, 1)
        o_fox = fox_attention(split_heads(fq, N_FOX_HEADS), split_heads(fk, N_FOX_HEADS),
                              split_heads(fv, N_FOX_HEADS), log_f)
        o_sb = stick_breaking_attention(split_heads(sq, N_SB_HEADS), split_heads(sk, N_SB_HEADS),
                                        split_heads(sv, N_SB_HEADS))
        o = jnp.concatenate([rmsnorm(merge_heads(o_fox), fox_out_g[l]),
                             rmsnorm(merge_heads(o_sb), sb_out_g[l])], axis=-1)
        x = x + o @ w_out[l]
        h = rmsnorm(x, ffn_norm_g[l])
        u = causal_dwconv(h @ w_up[l], conv_w[l], conv_b[l])
        gate, val = jnp.split(u, 2, axis=-1)
        x = x + (jax.nn.silu(gate) * val) @ w_down[l]
    return rmsnorm(x, final_norm_g)


import jax as _jax
import jax.numpy as _jnp

TWIN_FORMAT = 'train_step'
FWD_PARAMS = ['x', 'attn_norm_g', 'w_in', 'forget_bias', 'fox_out_g', 'sb_out_g', 'w_out', 'ffn_norm_g', 'w_up', 'conv_w', 'conv_b', 'w_down', 'final_norm_g']
TWIN_WEIGHTS = ['attn_norm_g', 'w_in', 'forget_bias', 'fox_out_g', 'sb_out_g', 'w_out', 'ffn_norm_g', 'w_up', 'conv_w', 'conv_b', 'w_down', 'final_norm_g']
TWIN_DIFF_INPUT = 'x'
TWIN_INPUTS = ['x', 'attn_norm_g', 'w_in', 'forget_bias', 'fox_out_g', 'sb_out_g', 'w_out', 'ffn_norm_g', 'w_up', 'conv_w', 'conv_b', 'w_down', 'final_norm_g', 'loss_target', 'm_attn_norm_g', 'm_w_in', 'm_forget_bias', 'm_fox_out_g', 'm_sb_out_g', 'm_w_out', 'm_ffn_norm_g', 'm_w_up', 'm_conv_w', 'm_conv_b', 'm_w_down', 'm_final_norm_g', 'v_attn_norm_g', 'v_w_in', 'v_forget_bias', 'v_fox_out_g', 'v_sb_out_g', 'v_w_out', 'v_ffn_norm_g', 'v_w_up', 'v_conv_w', 'v_conv_b', 'v_w_down', 'v_final_norm_g']
TWIN_OUTPUTS = ['loss', 'grad_x', 'grad_attn_norm_g', 'grad_w_in', 'grad_forget_bias', 'grad_fox_out_g', 'grad_sb_out_g', 'grad_w_out', 'grad_ffn_norm_g', 'grad_w_up', 'grad_conv_w', 'grad_conv_b', 'grad_w_down', 'grad_final_norm_g', 'delta_attn_norm_g', 'delta_w_in', 'delta_forget_bias', 'delta_fox_out_g', 'delta_sb_out_g', 'delta_w_out', 'delta_ffn_norm_g', 'delta_w_up', 'delta_conv_w', 'delta_conv_b', 'delta_w_down', 'delta_final_norm_g', 'new_m_attn_norm_g', 'new_m_w_in', 'new_m_forget_bias', 'new_m_fox_out_g', 'new_m_sb_out_g', 'new_m_w_out', 'new_m_ffn_norm_g', 'new_m_w_up', 'new_m_conv_w', 'new_m_conv_b', 'new_m_w_down', 'new_m_final_norm_g', 'new_v_attn_norm_g', 'new_v_w_in', 'new_v_forget_bias', 'new_v_fox_out_g', 'new_v_sb_out_g', 'new_v_w_out', 'new_v_ffn_norm_g', 'new_v_w_up', 'new_v_conv_w', 'new_v_conv_b', 'new_v_w_down', 'new_v_final_norm_g']
TWIN_LEAF_KINDS = {'loss': 'loss', 'grad_x': 'grad_x', 'grad_attn_norm_g': 'grad_w', 'grad_w_in': 'grad_w', 'grad_forget_bias': 'grad_w', 'grad_fox_out_g': 'grad_w', 'grad_sb_out_g': 'grad_w', 'grad_w_out': 'grad_w', 'grad_ffn_norm_g': 'grad_w', 'grad_w_up': 'grad_w', 'grad_conv_w': 'grad_w', 'grad_conv_b': 'grad_w', 'grad_w_down': 'grad_w', 'grad_final_norm_g': 'grad_w', 'delta_attn_norm_g': 'delta_w', 'delta_w_in': 'delta_w', 'delta_forget_bias': 'delta_w', 'delta_fox_out_g': 'delta_w', 'delta_sb_out_g': 'delta_w', 'delta_w_out': 'delta_w', 'delta_ffn_norm_g': 'delta_w', 'delta_w_up': 'delta_w', 'delta_conv_w': 'delta_w', 'delta_conv_b': 'delta_w', 'delta_w_down': 'delta_w', 'delta_final_norm_g': 'delta_w', 'new_m_attn_norm_g': 'new_m', 'new_m_w_in': 'new_m', 'new_m_forget_bias': 'new_m', 'new_m_fox_out_g': 'new_m', 'new_m_sb_out_g': 'new_m', 'new_m_w_out': 'new_m', 'new_m_ffn_norm_g': 'new_m', 'new_m_w_up': 'new_m', 'new_m_conv_w': 'new_m', 'new_m_conv_b': 'new_m', 'new_m_w_down': 'new_m', 'new_m_final_norm_g': 'new_m', 'new_v_attn_norm_g': 'new_v', 'new_v_w_in': 'new_v', 'new_v_forget_bias': 'new_v', 'new_v_fox_out_g': 'new_v', 'new_v_sb_out_g': 'new_v', 'new_v_w_out': 'new_v', 'new_v_ffn_norm_g': 'new_v', 'new_v_w_up': 'new_v', 'new_v_conv_w': 'new_v', 'new_v_conv_b': 'new_v', 'new_v_w_down': 'new_v', 'new_v_final_norm_g': 'new_v'}


def _forward(args):
    return _fwd_reference(*[args[k] for k in FWD_PARAMS])


def _output_shape():
    def fwd():
        inp = _fwd_setup_inputs(0)
        return _fwd_reference(*[inp[k] for k in FWD_PARAMS])
    out = _jax.eval_shape(fwd)
    return out.shape, out.dtype

N_MICROBATCH = 1
ADAM_LR = 0.001
ADAM_B1 = 0.9
ADAM_B2 = 0.999
ADAM_EPS = 1e-08
ADAM_WD = 0.01
ADAM_STEP = 10
PER_EXAMPLE_BATCH_AXIS = {'x': 0, 'loss_target': 0}
SHARED_INPUTS = []
_WEIGHT_DTYPES = {'attn_norm_g': _jnp.float32, 'w_in': _jnp.float32, 'forget_bias': _jnp.float32, 'fox_out_g': _jnp.float32, 'sb_out_g': _jnp.float32, 'w_out': _jnp.float32, 'ffn_norm_g': _jnp.float32, 'w_up': _jnp.float32, 'conv_w': _jnp.float32, 'conv_b': _jnp.float32, 'w_down': _jnp.float32, 'final_norm_g': _jnp.float32}
MOMENT_SCALE = {'attn_norm_g': 2.596198e-01, 'w_in': 1.517602e-01, 'forget_bias': 8.952599e-01, 'fox_out_g': 2.333795e-01, 'sb_out_g': 2.143277e-01, 'w_out': 2.031653e-01, 'ffn_norm_g': 1.470242e-01, 'w_up': 6.202939e-02, 'conv_w': 6.183440e-02, 'conv_b': 6.067437e-02, 'w_down': 1.030006e-01, 'final_norm_g': 6.425646e+01}


def _to_microbatches(a, axis):
    t = _jnp.moveaxis(a, axis, 0)
    t = t.reshape((N_MICROBATCH, t.shape[0] // N_MICROBATCH) + t.shape[1:])
    return _jnp.moveaxis(t, 1, axis + 1)


def setup_inputs(seed: int = 0) -> dict:
    inp = _fwd_setup_inputs(seed)
    key = _jax.random.fold_in(_jax.random.key(seed), 7919)
    shape, _ = _output_shape()
    out = dict(inp)
    out["loss_target"] = _jax.random.normal(_jax.random.fold_in(key, 0), shape, _jnp.float32)
    for i, name in enumerate(TWIN_WEIGHTS):
        w = inp[name].astype(_jnp.float32)
        if MOMENT_SCALE is None:
            s = _jnp.sqrt(_jnp.mean(_jnp.square(w)) + 1e-30)
        else:
            s = MOMENT_SCALE[name]
        km, kv = _jax.random.split(_jax.random.fold_in(key, i + 1))
        out[name] = w
        out["m_" + name] = s * _jax.random.normal(km, w.shape, _jnp.float32)
        out["v_" + name] = (s * s) * _jax.random.uniform(kv, w.shape, _jnp.float32, 0.5, 1.5)
    if N_MICROBATCH > 1:
        for name, axis in PER_EXAMPLE_BATCH_AXIS.items():
            out[name] = _to_microbatches(out[name], axis)
    return {'x': out['x'], 'attn_norm_g': out['attn_norm_g'], 'w_in': out['w_in'], 'forget_bias': out['forget_bias'], 'fox_out_g': out['fox_out_g'], 'sb_out_g': out['sb_out_g'], 'w_out': out['w_out'], 'ffn_norm_g': out['ffn_norm_g'], 'w_up': out['w_up'], 'conv_w': out['conv_w'], 'conv_b': out['conv_b'], 'w_down': out['w_down'], 'final_norm_g': out['final_norm_g'], 'loss_target': out['loss_target'], 'm_attn_norm_g': out['m_attn_norm_g'], 'm_w_in': out['m_w_in'], 'm_forget_bias': out['m_forget_bias'], 'm_fox_out_g': out['m_fox_out_g'], 'm_sb_out_g': out['m_sb_out_g'], 'm_w_out': out['m_w_out'], 'm_ffn_norm_g': out['m_ffn_norm_g'], 'm_w_up': out['m_w_up'], 'm_conv_w': out['m_conv_w'], 'm_conv_b': out['m_conv_b'], 'm_w_down': out['m_w_down'], 'm_final_norm_g': out['m_final_norm_g'], 'v_attn_norm_g': out['v_attn_norm_g'], 'v_w_in': out['v_w_in'], 'v_forget_bias': out['v_forget_bias'], 'v_fox_out_g': out['v_fox_out_g'], 'v_sb_out_g': out['v_sb_out_g'], 'v_w_out': out['v_w_out'], 'v_ffn_norm_g': out['v_ffn_norm_g'], 'v_w_up': out['v_w_up'], 'v_conv_w': out['v_conv_w'], 'v_conv_b': out['v_conv_b'], 'v_w_down': out['v_w_down'], 'v_final_norm_g': out['v_final_norm_g']}


def _loss(weights, diff, rest, loss_target):
    with _jax.named_scope("forward"):
        args = {**rest, TWIN_DIFF_INPUT: diff, **{k: w.astype(_WEIGHT_DTYPES[k]) for k, w in weights.items()}}
        y = _forward(args)
    with _jax.named_scope("loss_head"):
        err = _jnp.square(y.astype(_jnp.float32) - loss_target)
        return 0.5 * _jnp.sum(_jnp.mean(err, axis=-1)) if err.ndim else 0.5 * err


def _adamw(w, g, m, v):
    m = ADAM_B1 * m + (1.0 - ADAM_B1) * g
    v = ADAM_B2 * v + (1.0 - ADAM_B2) * _jnp.square(g)
    m_hat = m / (1.0 - ADAM_B1 ** ADAM_STEP)
    v_hat = v / (1.0 - ADAM_B2 ** ADAM_STEP)
    delta = -ADAM_LR * (m_hat / (_jnp.sqrt(v_hat) + ADAM_EPS) + ADAM_WD * w)
    return delta, m, v


def reference(x, attn_norm_g, w_in, forget_bias, fox_out_g, sb_out_g, w_out, ffn_norm_g, w_up, conv_w, conv_b, w_down, final_norm_g, loss_target, m_attn_norm_g, m_w_in, m_forget_bias, m_fox_out_g, m_sb_out_g, m_w_out, m_ffn_norm_g, m_w_up, m_conv_w, m_conv_b, m_w_down, m_final_norm_g, v_attn_norm_g, v_w_in, v_forget_bias, v_fox_out_g, v_sb_out_g, v_w_out, v_ffn_norm_g, v_w_up, v_conv_w, v_conv_b, v_w_down, v_final_norm_g):
    given = dict(x=x, attn_norm_g=attn_norm_g, w_in=w_in, forget_bias=forget_bias, fox_out_g=fox_out_g, sb_out_g=sb_out_g, w_out=w_out, ffn_norm_g=ffn_norm_g, w_up=w_up, conv_w=conv_w, conv_b=conv_b, w_down=w_down, final_norm_g=final_norm_g, loss_target=loss_target, m_attn_norm_g=m_attn_norm_g, m_w_in=m_w_in, m_forget_bias=m_forget_bias, m_fox_out_g=m_fox_out_g, m_sb_out_g=m_sb_out_g, m_w_out=m_w_out, m_ffn_norm_g=m_ffn_norm_g, m_w_up=m_w_up, m_conv_w=m_conv_w, m_conv_b=m_conv_b, m_w_down=m_w_down, m_final_norm_g=m_final_norm_g, v_attn_norm_g=v_attn_norm_g, v_w_in=v_w_in, v_forget_bias=v_forget_bias, v_fox_out_g=v_fox_out_g, v_sb_out_g=v_sb_out_g, v_w_out=v_w_out, v_ffn_norm_g=v_ffn_norm_g, v_w_up=v_w_up, v_conv_w=v_conv_w, v_conv_b=v_conv_b, v_w_down=v_w_down, v_final_norm_g=v_final_norm_g)
    weights = {n: given[n] for n in TWIN_WEIGHTS}
    shared = {n: given[n] for n in SHARED_INPUTS}
    per_example = {n: given[n] for n in ['x']}
    grad_fn = _jax.value_and_grad(_loss, argnums=(0, 1))

    def one_microbatch(ex, loss_target):
        ex = dict(ex)
        diff = ex.pop(TWIN_DIFF_INPUT)
        return grad_fn(weights, diff, {**shared, **ex}, loss_target)

    if N_MICROBATCH == 1:
        loss, (grad_w, grad_x) = one_microbatch(per_example, given["loss_target"])
    else:
        def body(carry, xs):
            loss_sum, grad_sum = carry
            l_k, (gw_k, gx_k) = one_microbatch(xs[0], xs[1])
            with _jax.named_scope("update"):
                return (loss_sum + l_k, _jax.tree.map(_jnp.add, grad_sum, gw_k)), gx_k

        init = (_jnp.zeros((), _jnp.float32), _jax.tree.map(_jnp.zeros_like, weights))
        (loss, grad_w), grad_x = _jax.lax.scan(body, init, (per_example, given["loss_target"]))
    with _jax.named_scope("update"):
        delta_w, new_m, new_v = {}, {}, {}
        for n in TWIN_WEIGHTS:
            delta_w[n], new_m[n], new_v[n] = _adamw(weights[n], grad_w[n], given["m_" + n], given["v_" + n])
    return (loss, grad_x, *[grad_w[n] for n in TWIN_WEIGHTS], *[delta_w[n] for n in TWIN_WEIGHTS],
            *[new_m[n] for n in TWIN_WEIGHTS], *[new_v[n] for n in TWIN_WEIGHTS])
```

```python
import functools

import jax
import jax.numpy as jnp
from jax import lax
from jax.experimental import pallas as pl
from jax.experimental.pallas import tpu as pltpu

F32 = jnp.float32
BF16 = jnp.bfloat16

D_MODEL = 1024
HEAD_DIM = 64
N_GROUP_HEADS = 8
GROUP_W = N_GROUP_HEADS * HEAD_DIM
QKV_W = 3 * GROUP_W
IN_COLS = 2 * QKV_W + N_GROUP_HEADS
GATE_PAD = 128
IN_COLS_PAD = 2 * QKV_W + GATE_PAD
D_FF = 2816
N_DEV = 8
EPS = 1e-6
Q_SCALE = HEAD_DIM ** -0.5

ADAM_LR = 0.001
ADAM_B1 = 0.9
ADAM_B2 = 0.999
ADAM_EPS = 1e-08
ADAM_WD = 0.01
ADAM_STEP = 10

LANES = 128
PACK_ROWS = 12800
SMALL_ROWS = 80
VMEM_LIMIT = 56 * 1024 * 1024
NEG_BIG = -1e30
ATTN_TQ = 512
FOX_TK = 512
SB_TK = 256
MESH = pl.DeviceIdType.MESH


def _cparams(sem=None, **kw):
    return pltpu.CompilerParams(dimension_semantics=sem, vmem_limit_bytes=VMEM_LIMIT, **kw)


def _tile(n, target, mult=LANES):
    if n <= target:
        return n
    t = (target // mult) * mult
    while t >= mult:
        if n % t == 0:
            return t
        t -= mult
    return n


def _seg_len(shape):
    n = 1
    for s in shape:
        n *= s
    return -(-n // LANES) * LANES


def _pack(arrs, rows, dtype):
    parts = []
    for a in arrs:
        f = a.reshape(-1).astype(dtype)
        parts.append(jnp.pad(f, (0, _seg_len(a.shape) - f.shape[0])))
    flat = jnp.concatenate(parts)
    flat = jnp.pad(flat, (0, rows * LANES - flat.shape[0]))
    return flat.reshape(rows, LANES)


def _unpack(p, shapes, lead=()):
    flat = p.reshape(lead + (-1,))
    out, off = [], 0
    for shp in shapes:
        n = 1
        for s in shp:
            n *= s
        out.append(flat[..., off:off + n].reshape(lead + tuple(shp)))
        off += _seg_len(shp)
    return out


def _my_pos():
    return lax.axis_index("x"), lax.axis_index("y"), lax.axis_index("c")


def _all_gather(block):
    rows, lanes = block.shape

    def body(x_ref, out_ref, send_sems, recv_sems, local_sem):
        x, y, c = _my_pos()
        me, sibling = (x, y, c), (x, y, 1 - c)
        chips = [(1 - x, y), (x, 1 - y), (1 - x, 1 - y)]

        def slot(px, py, pc):
            return out_ref.at[4 * px + 2 * py + pc]

        def copy(k, blk, to, src=None):
            return pltpu.make_async_remote_copy(
                src_ref=slot(*blk) if src is None else src, dst_ref=slot(*blk),
                send_sem=send_sems.at[k], recv_sem=recv_sems.at[k],
                device_id=to, device_id_type=MESH)

        mine = pltpu.make_async_copy(x_ref, slot(*me), local_sem)
        mine.start()
        first = [copy(0, me, sibling, src=x_ref)]
        first += [copy(1 + j, me, (*chip, c), src=x_ref) for j, chip in enumerate(chips)]
        for cp in first:
            cp.start()
        passed = [copy(4 + j, (*chip, c), sibling) for j, chip in enumerate(chips)]
        for j, chip in enumerate(chips):
            copy(1 + j, (*chip, c), me).wait_recv()
            passed[j].start()
        copy(0, sibling, me).wait_recv()
        for j, chip in enumerate(chips):
            copy(4 + j, (*chip, 1 - c), me).wait_recv()
        for cp in first + passed:
            cp.wait_send()
        mine.wait()

    return pl.pallas_call(
        body, name="weights_all_gather",
        out_shape=jax.ShapeDtypeStruct((N_DEV, rows, lanes), block.dtype),
        in_specs=[pl.BlockSpec(memory_space=pl.ANY)],
        out_specs=pl.BlockSpec(memory_space=pl.ANY),
        scratch_shapes=[pltpu.SemaphoreType.DMA((7,)), pltpu.SemaphoreType.DMA((7,)),
                        pltpu.SemaphoreType.DMA],
    )(block)


def _grad_exchange(gpack, spack):
    _, rows, lanes = gpack.shape
    srows = spack.shape[0]

    def body(g_ref, s_ref, grecv_ref, srecv_ref, send_sems, recv_sems, local_sems):
        x, y, c = _my_pos()
        my_id = 4 * x + 2 * y + c
        own_g = pltpu.make_async_copy(g_ref.at[my_id], grecv_ref.at[my_id], local_sems.at[0])
        own_s = pltpu.make_async_copy(s_ref, srecv_ref.at[my_id], local_sems.at[1])
        own_g.start()
        own_s.start()
        sends, arrivals = [], []
        for k in range(1, N_DEV):
            px, py, pc = x ^ (k >> 2), y ^ ((k >> 1) & 1), c ^ (k & 1)
            peer_id = 4 * px + 2 * py + pc
            for a, (src, dst) in enumerate(((g_ref.at[peer_id], grecv_ref), (s_ref, srecv_ref))):
                sends.append(pltpu.make_async_remote_copy(
                    src_ref=src, dst_ref=dst.at[my_id],
                    send_sem=send_sems.at[a, k - 1], recv_sem=recv_sems.at[a, k - 1],
                    device_id=(px, py, pc), device_id_type=MESH))
                arrivals.append(pltpu.make_async_remote_copy(
                    src_ref=src, dst_ref=dst.at[peer_id],
                    send_sem=send_sems.at[a, k - 1], recv_sem=recv_sems.at[a, k - 1],
                    device_id=(px, py, pc), device_id_type=MESH))
        for cp in sends:
            cp.start()
        for cp in arrivals:
            cp.wait_recv()
        for cp in sends:
            cp.wait_send()
        own_g.wait()
        own_s.wait()

    return pl.pallas_call(
        body, name="grad_exchange",
        out_shape=(jax.ShapeDtypeStruct((N_DEV, rows, lanes), gpack.dtype),
                   jax.ShapeDtypeStruct((N_DEV, srows, lanes), spack.dtype)),
        in_specs=[pl.BlockSpec(memory_space=pl.ANY), pl.BlockSpec(memory_space=pl.ANY)],
        out_specs=(pl.BlockSpec(memory_space=pl.ANY), pl.BlockSpec(memory_space=pl.ANY)),
        scratch_shapes=[pltpu.SemaphoreType.DMA((2, 7)), pltpu.SemaphoreType.DMA((2, 7)),
                        pltpu.SemaphoreType.DMA((2,))],
    )(gpack, spack)


_DIMS = {"nn": (((1,), (0,)), ((), ())), "nt": (((1,), (1,)), ((), ())), "tn": (((0,), (0,)), ((), ()))}


def _matmul(a, b, *, mode, grid, a_block, a_map, b_block, b_map, o_block, o_map, out_shape, name,
            resid=None):
    nk = grid[2]
    dims = _DIMS[mode]

    def body(*refs):
        if resid is None:
            a_ref, b_ref, o_ref, acc_ref = refs
            r_ref = None
        else:
            a_ref, b_ref, r_ref, o_ref, acc_ref = refs
        k = pl.program_id(2)

        @pl.when(k == 0)
        def _():
            acc_ref[...] = jnp.zeros_like(acc_ref)

        acc_ref[...] += lax.dot_general(a_ref[...], b_ref[...], dims, preferred_element_type=F32)

        @pl.when(k == nk - 1)
        def _():
            res = acc_ref[...]
            if r_ref is not None:
                res = r_ref[...] + res
            o_ref[...] = res.astype(o_ref.dtype)

    in_specs = [pl.BlockSpec(a_block, a_map), pl.BlockSpec(b_block, b_map)]
    args = [a, b]
    if resid is not None:
        in_specs.append(pl.BlockSpec(o_block, o_map))
        args.append(resid)
    acc_shape = tuple(d for d in o_block if d is not None)
    return pl.pallas_call(
        body, name=name, grid=grid, in_specs=in_specs,
        out_specs=pl.BlockSpec(o_block, o_map), out_shape=out_shape,
        scratch_shapes=[pltpu.VMEM(acc_shape, F32)],
        compiler_params=_cparams(("parallel", "parallel", "arbitrary")),
    )(*args)


def _mm_nn(a, b, out_dtype, name, resid=None, tm=512, tn=512, tk=1024):
    m, kk = a.shape
    n = b.shape[1]
    tm, tn, tk = _tile(m, tm, 8), _tile(n, tn), _tile(kk, tk)
    return _matmul(a, b, mode="nn", grid=(m // tm, n // tn, kk // tk),
                   a_block=(tm, tk), a_map=lambda i, j, k: (i, k),
                   b_block=(tk, tn), b_map=lambda i, j, k: (k, j),
                   o_block=(tm, tn), o_map=lambda i, j, k: (i, j),
                   out_shape=jax.ShapeDtypeStruct((m, n), out_dtype), name=name, resid=resid)


def _mm_nt(a, b, out_dtype, name, tm=512, tn=512, tk=1024):
    m, kk = a.shape
    n = b.shape[0]
    tm, tn, tk = _tile(m, tm, 8), _tile(n, tn), _tile(kk, tk)
    return _matmul(a, b, mode="nt", grid=(m // tm, n // tn, kk // tk),
                   a_block=(tm, tk), a_map=lambda i, j, k: (i, k),
                   b_block=(tn, tk), b_map=lambda i, j, k: (j, k),
                   o_block=(tm, tn), o_map=lambda i, j, k: (i, j),
                   out_shape=jax.ShapeDtypeStruct((m, n), out_dtype), name=name)


def _mm_tn(a, b, name, tm=512, tn=512, tk=1024):
    kk, m = a.shape
    n = b.shape[1]
    tm, tn, tk = _tile(m, tm), _tile(n, tn), _tile(kk, tk, 8)
    return _matmul(a, b, mode="tn", grid=(m // tm, n // tn, kk // tk),
                   a_block=(tk, tm), a_map=lambda i, j, k: (k, i),
                   b_block=(tk, tn), b_map=lambda i, j, k: (k, j),
                   o_block=(tm, tn), o_map=lambda i, j, k: (i, j),
                   out_shape=jax.ShapeDtypeStruct((m, n), F32), name=name)


def _mm_up(h, w_up, tm=512, tn=256):
    s = h.shape[0]
    tm = _tile(s, tm, 8)
    nh = D_FF // tn
    return _matmul(h, w_up, mode="nn", grid=(s // tm, 2 * nh, 1),
                   a_block=(tm, D_MODEL), a_map=lambda i, j, k: (i, 0),
                   b_block=(D_MODEL, tn), b_map=lambda i, j, k: (0, j),
                   o_block=(None, tm, tn), o_map=lambda i, j, k: (j // nh, i, j % nh),
                   out_shape=jax.ShapeDtypeStruct((2, s, D_FF), F32), name="up_proj")


def _mm_dup_nt(dup, w_up, tm=512, tk=1408):
    s = dup.shape[1]
    tm = _tile(s, tm, 8)
    nh = D_FF // tk
    return _matmul(dup, w_up, mode="nt", grid=(s // tm, 1, 2 * nh),
                   a_block=(None, tm, tk), a_map=lambda i, j, k: (k // nh, i, k % nh),
                   b_block=(D_MODEL, tk), b_map=lambda i, j, k: (0, k),
                   o_block=(tm, D_MODEL), o_map=lambda i, j, k: (i, 0),
                   out_shape=jax.ShapeDtypeStruct((s, D_MODEL), F32), name="d_h2")


def _mm_dwup_tn(h, dup, tn=256, tk=1024):
    s = h.shape[0]
    tk = _tile(s, tk, 8)
    nh = D_FF // tn
    return _matmul(h, dup, mode="tn", grid=(1, 2 * nh, s // tk),
                   a_block=(tk, D_MODEL), a_map=lambda i, j, k: (k, 0),
                   b_block=(None, tk, tn), b_map=lambda i, j, k: (j // nh, k, j % nh),
                   o_block=(D_MODEL, tn), o_map=lambda i, j, k: (0, j),
                   out_shape=jax.ShapeDtypeStruct((D_MODEL, 2 * D_FF), F32), name="d_w_up")


def _rms_fwd(x, g, tr=256):
    s, d = x.shape
    tr = _tile(s, tr, 8)

    def body(x_ref, g_ref, o_ref):
        xv = x_ref[...]
        r = lax.rsqrt(jnp.mean(xv * xv, axis=-1, keepdims=True) + EPS)
        o_ref[...] = (xv * r * g_ref[...]).astype(o_ref.dtype)

    return pl.pallas_call(
        body, name="rms_fwd", grid=(s // tr,),
        in_specs=[pl.BlockSpec((tr, d), lambda i: (i, 0)), pl.BlockSpec((1, d), lambda i: (0, 0))],
        out_specs=pl.BlockSpec((tr, d), lambda i: (i, 0)),
        out_shape=jax.ShapeDtypeStruct((s, d), BF16),
        compiler_params=_cparams(("parallel",)),
    )(x, g)


def _group_rms_fwd(o_fox, o_sb, g_fox, g_sb, tr=256):
    s, d = o_fox.shape
    tr = _tile(s, tr, 8)

    def body(a_ref, b_ref, ga_ref, gb_ref, o_ref):
        for src, g_ref, lo in ((a_ref, ga_ref, 0), (b_ref, gb_ref, d)):
            xv = src[...]
            r = lax.rsqrt(jnp.mean(xv * xv, axis=-1, keepdims=True) + EPS)
            o_ref[:, lo:lo + d] = (xv * r * g_ref[...]).astype(o_ref.dtype)

    row = pl.BlockSpec((tr, d), lambda i: (i, 0))
    gain = pl.BlockSpec((1, d), lambda i: (0, 0))
    return pl.pallas_call(
        body, name="group_rms_fwd", grid=(s // tr,),
        in_specs=[row, row, gain, gain],
        out_specs=pl.BlockSpec((tr, 2 * d), lambda i: (i, 0)),
        out_shape=jax.ShapeDtypeStruct((s, 2 * d), BF16),
        compiler_params=_cparams(("parallel",)),
    )(o_fox, o_sb, g_fox, g_sb)


def _rms_bwd(x, dy, g, resid, *, dy_col, name, want_bf16, tr=256):
    s, d = x.shape
    tr = _tile(s, tr, 8)
    has_resid = resid is not None

    def body(*refs):
        refs = list(refs)
        x_ref, dy_ref, g_ref = refs[:3]
        r_ref = refs[3] if has_resid else None
        outs = refs[4:] if has_resid else refs[3:]
        dx_ref = outs[0]
        dxb_ref = outs[1] if want_bf16 else None
        dg_ref = outs[-1]

        @pl.when(pl.program_id(0) == 0)
        def _():
            dg_ref[...] = jnp.zeros_like(dg_ref)

        xv = x_ref[...]
        dyv = dy_ref[...]
        r = lax.rsqrt(jnp.mean(xv * xv, axis=-1, keepdims=True) + EPS)
        xh = xv * r
        gy = dyv * g_ref[...]
        dx = r * (gy - xh * jnp.mean(xh * gy, axis=-1, keepdims=True))
        if r_ref is not None:
            dx = r_ref[...] + dx
        dx_ref[...] = dx
        if dxb_ref is not None:
            dxb_ref[...] = dx.astype(BF16)
        dg_ref[...] += jnp.sum(dyv * xh, axis=0, keepdims=True)

    row = pl.BlockSpec((tr, d), lambda i: (i, 0))
    in_specs = [row, pl.BlockSpec((tr, d), lambda i: (i, dy_col)), pl.BlockSpec((1, d), lambda i: (0, 0))]
    args = [x, dy, g]
    if has_resid:
        in_specs.append(row)
        args.append(resid)
    out_specs = [row]
    out_shape = [jax.ShapeDtypeStruct((s, d), F32)]
    if want_bf16:
        out_specs.append(row)
        out_shape.append(jax.ShapeDtypeStruct((s, d), BF16))
    out_specs.append(pl.BlockSpec((1, d), lambda i: (0, 0)))
    out_shape.append(jax.ShapeDtypeStruct((1, d), F32))
    return pl.pallas_call(
        body, name=name, grid=(s // tr,), in_specs=in_specs, out_specs=out_specs, out_shape=out_shape,
        compiler_params=_cparams(("arbitrary",)),
    )(*args)


def _loss_head(x2, target, g, tr=256):
    s, d = x2.shape
    tr = _tile(s, tr, 8)

    def body(x_ref, t_ref, g_ref, dx_ref, dxb_ref, dg_ref, loss_ref):
        @pl.when(pl.program_id(0) == 0)
        def _():
            dg_ref[...] = jnp.zeros_like(dg_ref)
            loss_ref[...] = jnp.zeros_like(loss_ref)

        xv = x_ref[...]
        gv = g_ref[...]
        r = lax.rsqrt(jnp.mean(xv * xv, axis=-1, keepdims=True) + EPS)
        xh = xv * r
        err = xh * gv - t_ref[...]
        loss_ref[...] += jnp.sum(jnp.mean(err * err, axis=-1, keepdims=True), axis=0, keepdims=True) * 0.5
        dyv = err * (1.0 / d)
        gy = dyv * gv
        dx = r * (gy - xh * jnp.mean(xh * gy, axis=-1, keepdims=True))
        dx_ref[...] = dx
        dxb_ref[...] = dx.astype(BF16)
        dg_ref[...] += jnp.sum(dyv * xh, axis=0, keepdims=True)

    row = pl.BlockSpec((tr, d), lambda i: (i, 0))
    return pl.pallas_call(
        body, name="loss_head", grid=(s // tr,),
        in_specs=[row, row, pl.BlockSpec((1, d), lambda i: (0, 0))],
        out_specs=[row, row, pl.BlockSpec((1, d), lambda i: (0, 0)), pl.BlockSpec((1, LANES), lambda i: (0, 0))],
        out_shape=[jax.ShapeDtypeStruct((s, d), F32), jax.ShapeDtypeStruct((s, d), BF16),
                   jax.ShapeDtypeStruct((1, d), F32), jax.ShapeDtypeStruct((1, LANES), F32)],
        compiler_params=_cparams(("arbitrary",)),
    )(x2, target, g)


def _conv_taps(cur, prev8, w, b, first):
    prev8 = jnp.where(first, 0.0, prev8)
    ext = jnp.concatenate([prev8, cur], axis=0)
    x1 = pltpu.roll(ext, 1, 0)[8:]
    x2 = pltpu.roll(ext, 2, 0)[8:]
    u = b + w[0:1] * x2
    u = u + w[1:2] * x1
    u = u + w[2:3] * cur
    return u, x1, x2


def _conv_gate_fwd(up, conv_w, conv_b, tm=512, tn=256):
    s = up.shape[1]
    tm = _tile(s, tm, 8)
    nrb = s // tm
    rb8 = tm // 8

    def body(g_ref, v_ref, gp_ref, vp_ref, wg_ref, wv_ref, bg_ref, bv_ref, o_ref):
        first = pl.program_id(1) == 0
        ug, _, _ = _conv_taps(g_ref[...], gp_ref[...], wg_ref[...], bg_ref[...], first)
        uv, _, _ = _conv_taps(v_ref[...], vp_ref[...], wv_ref[...], bv_ref[...], first)
        sg = 1.0 / (1.0 + jnp.exp(-ug))
        o_ref[...] = (ug * sg * uv).astype(o_ref.dtype)

    def cur(h):
        return pl.BlockSpec((None, tm, tn), lambda j, i: (h, i, j))

    def prev(h):
        return pl.BlockSpec((None, 8, tn), lambda j, i: (h, jnp.maximum(i * rb8 - 1, 0), j))

    def par(h, r):
        return pl.BlockSpec((None, r, tn), lambda j, i: (h, 0, j))

    return pl.pallas_call(
        body, name="conv_gate_fwd", grid=(D_FF // tn, nrb),
        in_specs=[cur(0), cur(1), prev(0), prev(1), par(0, 3), par(1, 3), par(0, 1), par(1, 1)],
        out_specs=pl.BlockSpec((tm, tn), lambda j, i: (i, j)),
        out_shape=jax.ShapeDtypeStruct((s, D_FF), BF16),
        compiler_params=_cparams(("parallel", "parallel")),
    )(up, up, up, up, conv_w, conv_w, conv_b, conv_b)


def _conv_gate_bwd(up, dact, conv_w, conv_b, tm=512, tn=256):
    s = up.shape[1]
    tm = _tile(s, tm, 8)
    nrb = s // tm
    rb8 = tm // 8

    def body(g_ref, v_ref, gp_ref, vp_ref, da_ref, wg_ref, wv_ref, bg_ref, bv_ref,
             dup_ref, dcw_ref, dcb_ref, carry_ref):
        i = pl.program_id(1)
        first = i == nrb - 1

        @pl.when(i == 0)
        def _():
            carry_ref[...] = jnp.zeros_like(carry_ref)
            dcw_ref[...] = jnp.zeros_like(dcw_ref)
            dcb_ref[...] = jnp.zeros_like(dcb_ref)

        curs = (g_ref[...], v_ref[...])
        ws = (wg_ref[...], wv_ref[...])
        ug, g1, g2 = _conv_taps(curs[0], gp_ref[...], ws[0], bg_ref[...], first)
        uv, v1, v2 = _conv_taps(curs[1], vp_ref[...], ws[1], bv_ref[...], first)
        sg = 1.0 / (1.0 + jnp.exp(-ug))
        da = da_ref[...].astype(F32)
        d_v = da * (ug * sg)
        d_g = da * uv * (sg * (1.0 + ug * (1.0 - sg)))
        for h, (du, x0, x1, x2) in enumerate(((d_g, curs[0], g1, g2), (d_v, curs[1], v1, v2))):
            dcb_ref[h] += jnp.sum(du, axis=0, keepdims=True)
            dcw_ref[h, 0:1, :] += jnp.sum(du * x2, axis=0, keepdims=True)
            dcw_ref[h, 1:2, :] += jnp.sum(du * x1, axis=0, keepdims=True)
            dcw_ref[h, 2:3, :] += jnp.sum(du * x0, axis=0, keepdims=True)
            ext = jnp.concatenate([du, carry_ref[h]], axis=0)
            n1 = pltpu.roll(ext, tm + 7, 0)[:tm]
            n2 = pltpu.roll(ext, tm + 6, 0)[:tm]
            w = ws[h]
            dup_ref[h] = (w[2:3] * du + w[1:2] * n1 + w[0:1] * n2).astype(dup_ref.dtype)
            carry_ref[h] = du[:8]

    def cur(h):
        return pl.BlockSpec((None, tm, tn), lambda j, i: (h, nrb - 1 - i, j))

    def prev(h):
        return pl.BlockSpec((None, 8, tn), lambda j, i: (h, jnp.maximum((nrb - 1 - i) * rb8 - 1, 0), j))

    def par(h, r):
        return pl.BlockSpec((None, r, tn), lambda j, i: (h, 0, j))

    return pl.pallas_call(
        body, name="conv_gate_bwd", grid=(D_FF // tn, nrb),
        in_specs=[cur(0), cur(1), prev(0), prev(1),
                  pl.BlockSpec((tm, tn), lambda j, i: (nrb - 1 - i, j)),
                  par(0, 3), par(1, 3), par(0, 1), par(1, 1)],
        out_specs=[pl.BlockSpec((2, tm, tn), lambda j, i: (0, nrb - 1 - i, j)),
                   pl.BlockSpec((2, 3, tn), lambda j, i: (0, 0, j)),
                   pl.BlockSpec((2, 1, tn), lambda j, i: (0, 0, j))],
        out_shape=[jax.ShapeDtypeStruct((2, s, D_FF), BF16),
                   jax.ShapeDtypeStruct((2, 3, D_FF), F32),
                   jax.ShapeDtypeStruct((2, 1, D_FF), F32)],
        scratch_shapes=[pltpu.VMEM((2, 8, tn), F32)],
        compiler_params=_cparams(("parallel", "arbitrary")),
    )(up, up, up, up, dact, conv_w, conv_w, conv_b, conv_b)


def _split_dot(x, tri, terms):
    piece = x.astype(BF16)
    out = jnp.dot(piece, tri, preferred_element_type=F32)
    rest = x
    for _ in range(terms - 1):
        rest = rest - piece.astype(F32)
        piece = rest.astype(BF16)
        out = out + jnp.dot(piece, tri, preferred_element_type=F32)
    return out


def _split_dot_rhs(tri, x, terms):
    piece = x.astype(BF16)
    out = jnp.dot(tri, piece, preferred_element_type=F32)
    rest = x
    for _ in range(terms - 1):
        rest = rest - piece.astype(F32)
        piece = rest.astype(BF16)
        out = out + jnp.dot(tri, piece, preferred_element_type=F32)
    return out


def _tri(n, kind):
    r = lax.broadcasted_iota(jnp.int32, (n, n), 0)
    c = lax.broadcasted_iota(jnp.int32, (n, n), 1)
    cond = {"le": r <= c, "ge": r >= c, "lt": r < c, "gt": r > c}[kind]
    return jnp.where(cond, 1.0, 0.0).astype(BF16)


def _log_sigmoid(x):
    return jnp.minimum(x, 0.0) - jnp.log(1.0 + jnp.exp(-jnp.abs(x)))


def _forget_fwd(f_logit, bias):
    h, r, _ = f_logit.shape

    def body(x_ref, b_ref, o_ref):
        lf = _log_sigmoid(x_ref[...] + b_ref[...])
        within = _split_dot(lf, _tri(LANES, "le"), 3)
        row_tot = jnp.broadcast_to(within[:, LANES - 1:LANES], (r, LANES))
        before = _split_dot_rhs(_tri(r, "gt"), row_tot, 3)
        o_ref[...] = within + before

    blk = pl.BlockSpec((None, r, LANES), lambda i: (i, 0, 0))
    return pl.pallas_call(
        body, name="forget_cumsum_fwd", grid=(h,),
        in_specs=[blk, pl.BlockSpec((None, 1, LANES), lambda i: (i, 0, 0))],
        out_specs=blk, out_shape=jax.ShapeDtypeStruct((h, r, LANES), F32),
        compiler_params=_cparams(("parallel",)),
    )(f_logit, bias)


def _forget_bwd(f_logit, bias, ksum, qsum):
    h, r, _ = f_logit.shape

    def body(x_ref, b_ref, k_ref, q_ref, dx_ref, db_ref):
        d_f = q_ref[...] - k_ref[...]
        within = _split_dot(d_f, _tri(LANES, "ge"), 3)
        row_tot = jnp.broadcast_to(within[:, 0:1], (r, LANES))
        after = _split_dot_rhs(_tri(r, "lt"), row_tot, 3)
        xv = x_ref[...] + b_ref[...]
        dx = (within + after) * jnp.exp(_log_sigmoid(-xv))
        dx_ref[...] = dx
        db_ref[...] = jnp.broadcast_to(jnp.sum(dx), (1, LANES))

    blk = pl.BlockSpec((None, r, LANES), lambda i: (i, 0, 0))
    one = pl.BlockSpec((None, 1, LANES), lambda i: (i, 0, 0))
    return pl.pallas_call(
        body, name="forget_cumsum_bwd", grid=(h,),
        in_specs=[blk, one, blk, blk], out_specs=[blk, one],
        out_shape=[jax.ShapeDtypeStruct((h, r, LANES), F32), jax.ShapeDtypeStruct((h, 1, LANES), F32)],
        compiler_params=_cparams(("parallel",)),
    )(f_logit, bias, ksum, qsum)


def _head_specs(s, tq):
    qblk = pl.BlockSpec((None, tq, HEAD_DIM), lambda h, i: (h, i, 0))
    full = pl.BlockSpec((None, s, HEAD_DIM), lambda h, i: (h, 0, 0))
    col = pl.BlockSpec((None, tq, 1), lambda h, i: (h, i, 0))
    return qblk, full, col


def _positions(i, tq, tk):
    row = i * tq + lax.broadcasted_iota(jnp.int32, (tq, tk), 0)
    col = lax.broadcasted_iota(jnp.int32, (tq, tk), 1)
    return row, col


def _scaled(q_ref):
    return (q_ref[...].astype(F32) * Q_SCALE).astype(BF16)


_NT = (((1,), (1,)), ((), ()))
_TN = (((0,), (0,)), ((), ()))


def _fox_fwd(q, k, v, f_col, f_row, tq, tk):
    h, s, _ = q.shape
    nk = s // tk

    def body(q_ref, k_ref, v_ref, fc_ref, fr_ref, o_ref, lse_ref, m_ref, l_ref, acc_ref):
        i = pl.program_id(1)
        qs = _scaled(q_ref)
        fq = fc_ref[...]
        row, col = _positions(i, tq, tk)
        m_ref[...] = jnp.full_like(m_ref, NEG_BIG)
        l_ref[...] = jnp.zeros_like(l_ref)
        acc_ref[...] = jnp.zeros_like(acc_ref)

        def step(j, carry):
            ks = k_ref[pl.ds(pl.multiple_of(j * tk, tk), tk), :]
            vs = v_ref[pl.ds(pl.multiple_of(j * tk, tk), tk), :]
            sc = lax.dot_general(qs, ks, _NT, preferred_element_type=F32) + fq - fr_ref[j]
            sc = jnp.where(j * tk + col <= row, sc, NEG_BIG)
            m_old = m_ref[...]
            m_new = jnp.maximum(m_old, jnp.max(sc, axis=-1, keepdims=True))
            alpha = jnp.exp(m_old - m_new)
            p = jnp.exp(sc - m_new)
            l_ref[...] = alpha * l_ref[...] + jnp.sum(p, axis=-1, keepdims=True)
            acc_ref[...] = alpha * acc_ref[...] + jnp.dot(p.astype(BF16), vs, preferred_element_type=F32)
            m_ref[...] = m_new
            return carry

        lax.fori_loop(0, ((i + 1) * tq + tk - 1) // tk, step, 0)
        o_ref[...] = acc_ref[...] / l_ref[...]
        lse_ref[...] = m_ref[...] + jnp.log(l_ref[...])

    qblk, full, colspec = _head_specs(s, tq)
    return pl.pallas_call(
        body, name="fox_fwd", grid=(h, s // tq),
        in_specs=[qblk, full, full, colspec, pl.BlockSpec((None, nk, 1, tk), lambda hh, i: (hh, 0, 0, 0))],
        out_specs=[qblk, colspec],
        out_shape=[jax.ShapeDtypeStruct((h, s, HEAD_DIM), F32), jax.ShapeDtypeStruct((h, s, 1), F32)],
        scratch_shapes=[pltpu.VMEM((tq, 1), F32), pltpu.VMEM((tq, 1), F32), pltpu.VMEM((tq, HEAD_DIM), F32)],
        compiler_params=_cparams(("parallel", "parallel")),
    )(q, k, v, f_col, f_row)


def _fox_bwd(q, k, v, f_col, f_row, o, lse, d_o, tq, tk):
    h, s, _ = q.shape
    nk = s // tk

    def body(q_ref, k_ref, v_ref, fc_ref, fr_ref, o_ref, lse_ref, do_ref,
             dq_ref, dk_ref, dv_ref, ks_ref, qs_ref, dq_acc, qsum_acc):
        i = pl.program_id(1)

        @pl.when(i == 0)
        def _():
            dk_ref[...] = jnp.zeros_like(dk_ref)
            dv_ref[...] = jnp.zeros_like(dv_ref)
            ks_ref[...] = jnp.zeros_like(ks_ref)

        qs = _scaled(q_ref)
        fq = fc_ref[...]
        lse_v = lse_ref[...]
        dob = do_ref[...].astype(BF16)
        delta = jnp.sum(dob.astype(F32) * o_ref[...], axis=-1, keepdims=True)
        row, col = _positions(i, tq, tk)
        dq_acc[...] = jnp.zeros_like(dq_acc)
        qsum_acc[...] = jnp.zeros_like(qsum_acc)

        def step(j, carry):
            at = pl.ds(pl.multiple_of(j * tk, tk), tk)
            ks = k_ref[at, :]
            vs = v_ref[at, :]
            sc = lax.dot_general(qs, ks, _NT, preferred_element_type=F32) + fq - fr_ref[j]
            sc = jnp.where(j * tk + col <= row, sc, NEG_BIG)
            p = jnp.exp(sc - lse_v)
            dp = lax.dot_general(dob, vs, _NT, preferred_element_type=F32)
            ds = p * (dp - delta)
            dsb = ds.astype(BF16)
            dq_acc[...] += jnp.dot(dsb, ks, preferred_element_type=F32)
            dk_ref[at, :] += lax.dot_general(dsb, qs, _TN, preferred_element_type=F32)
            dv_ref[at, :] += lax.dot_general(p.astype(BF16), dob, _TN, preferred_element_type=F32)
            ks_ref[j] += jnp.sum(ds.reshape(tq // 8, 8, tk), axis=0)
            qsum_acc[...] += jnp.sum(ds, axis=-1, keepdims=True)
            return carry

        lax.fori_loop(0, ((i + 1) * tq + tk - 1) // tk, step, 0)
        dq_ref[...] = dq_acc[...] * Q_SCALE
        qs_ref[...] = qsum_acc[...]

    qblk, full, colspec = _head_specs(s, tq)
    frow = pl.BlockSpec((None, nk, 1, tk), lambda hh, i: (hh, 0, 0, 0))
    return pl.pallas_call(
        body, name="fox_bwd", grid=(h, s // tq),
        in_specs=[qblk, full, full, colspec, frow, qblk, colspec, qblk],
        out_specs=[qblk, full, full, pl.BlockSpec((None, nk, 8, tk), lambda hh, i: (hh, 0, 0, 0)), colspec],
        out_shape=[jax.ShapeDtypeStruct((h, s, HEAD_DIM), F32)] * 3
        + [jax.ShapeDtypeStruct((h, nk, 8, tk), F32), jax.ShapeDtypeStruct((h, s, 1), F32)],
        scratch_shapes=[pltpu.VMEM((tq, HEAD_DIM), F32), pltpu.VMEM((tq, 1), F32)],
        compiler_params=_cparams(("parallel", "arbitrary")),
    )(q, k, v, f_col, f_row, o, lse, d_o)


SB_TERMS = 2


def _sb_logits(qs, ks, mask):
    z = lax.dot_general(qs, ks, _NT, preferred_element_type=F32)
    lb = _log_sigmoid(z)
    lom = jnp.where(mask, lb - z, 0.0)
    return lb, lom


def _sb_fwd(q, k, v, tq, tk):
    h, s, _ = q.shape

    def body(q_ref, k_ref, v_ref, o_ref, tot_ref, acc_ref, run_ref):
        i = pl.program_id(1)
        qs = _scaled(q_ref)
        row, col = _positions(i, tq, tk)
        tri = _tri(tk, "ge")
        acc_ref[...] = jnp.zeros_like(acc_ref)
        run_ref[...] = jnp.zeros_like(run_ref)
        nkv = ((i + 1) * tq + tk - 1) // tk

        def step(jj, carry):
            j = nkv - 1 - jj
            at = pl.ds(pl.multiple_of(j * tk, tk), tk)
            mask = j * tk + col < row
            lb, lom = _sb_logits(qs, k_ref[at, :], mask)
            inc = _split_dot(lom, tri, SB_TERMS)
            after = run_ref[...] + inc - lom
            w = jnp.where(mask, jnp.exp(lb + after), 0.0)
            acc_ref[...] += jnp.dot(w.astype(BF16), v_ref[at, :], preferred_element_type=F32)
            run_ref[...] += inc[:, 0:1]
            return carry

        lax.fori_loop(0, nkv, step, 0)
        o_ref[...] = acc_ref[...]
        tot_ref[...] = run_ref[...]

    qblk, full, colspec = _head_specs(s, tq)
    return pl.pallas_call(
        body, name="sb_fwd", grid=(h, s // tq),
        in_specs=[qblk, full, full], out_specs=[qblk, colspec],
        out_shape=[jax.ShapeDtypeStruct((h, s, HEAD_DIM), F32), jax.ShapeDtypeStruct((h, s, 1), F32)],
        scratch_shapes=[pltpu.VMEM((tq, HEAD_DIM), F32), pltpu.VMEM((tq, 1), F32)],
        compiler_params=_cparams(("parallel", "parallel")),
    )(q, k, v)


def _sb_bwd(q, k, v, tot, d_o, tq, tk):
    h, s, _ = q.shape

    def body(q_ref, k_ref, v_ref, tot_ref, do_ref, dq_ref, dk_ref, dv_ref, dq_acc, lrun_ref, grun_ref):
        i = pl.program_id(1)

        @pl.when(i == 0)
        def _():
            dk_ref[...] = jnp.zeros_like(dk_ref)
            dv_ref[...] = jnp.zeros_like(dv_ref)

        qs = _scaled(q_ref)
        dob = do_ref[...].astype(BF16)
        tot_v = tot_ref[...]
        row, col = _positions(i, tq, tk)
        tri = _tri(tk, "le")
        dq_acc[...] = jnp.zeros_like(dq_acc)
        lrun_ref[...] = jnp.zeros_like(lrun_ref)
        grun_ref[...] = jnp.zeros_like(grun_ref)

        def step(j, carry):
            at = pl.ds(pl.multiple_of(j * tk, tk), tk)
            ks = k_ref[at, :]
            vs = v_ref[at, :]
            mask = j * tk + col < row
            lb, lom = _sb_logits(qs, ks, mask)
            linc = _split_dot(lom, tri, SB_TERMS)
            after = tot_v - (lrun_ref[...] + linc)
            w = jnp.where(mask, jnp.exp(lb + after), 0.0)
            g = w * lax.dot_general(dob, vs, _NT, preferred_element_type=F32)
            ginc = _split_dot(g, tri, SB_TERMS)
            dz = jnp.where(mask, g - jnp.exp(lb) * (grun_ref[...] + ginc), 0.0)
            dzb = dz.astype(BF16)
            dq_acc[...] += jnp.dot(dzb, ks, preferred_element_type=F32)
            dk_ref[at, :] += lax.dot_general(dzb, qs, _TN, preferred_element_type=F32)
            dv_ref[at, :] += lax.dot_general(w.astype(BF16), dob, _TN, preferred_element_type=F32)
            lrun_ref[...] += linc[:, tk - 1:tk]
            grun_ref[...] += ginc[:, tk - 1:tk]
            return carry

        lax.fori_loop(0, ((i + 1) * tq + tk - 1) // tk, step, 0)
        dq_ref[...] = dq_acc[...] * Q_SCALE

    qblk, full, colspec = _head_specs(s, tq)
    return pl.pallas_call(
        body, name="sb_bwd", grid=(h, s // tq),
        in_specs=[qblk, full, full, colspec, qblk], out_specs=[qblk, full, full],
        out_shape=[jax.ShapeDtypeStruct((h, s, HEAD_DIM), F32)] * 3,
        scratch_shapes=[pltpu.VMEM((tq, HEAD_DIM), F32), pltpu.VMEM((tq, 1), F32), pltpu.VMEM((tq, 1), F32)],
        compiler_params=_cparams(("parallel", "arbitrary")),
    )(q, k, v, tot, d_o)


def _sum_adamw(parts, w, m, v, name, tr=512):
    _, rows, lanes = parts.shape
    tr = _tile(rows, tr, 16)
    c_m = 1.0 - ADAM_B1 ** ADAM_STEP
    c_v = 1.0 - ADAM_B2 ** ADAM_STEP

    def body(p_ref, w_ref, m_ref, v_ref, g_ref, d_ref, nm_ref, nv_ref):
        g = p_ref[0].astype(F32)
        for j in range(1, N_DEV):
            g = g + p_ref[j].astype(F32)
        nm = ADAM_B1 * m_ref[...] + (1.0 - ADAM_B1) * g
        nv = ADAM_B2 * v_ref[...] + (1.0 - ADAM_B2) * (g * g)
        m_hat = nm / c_m
        v_hat = nv / c_v
        g_ref[...] = g
        d_ref[...] = -ADAM_LR * (m_hat / (jnp.sqrt(v_hat) + ADAM_EPS) + ADAM_WD * w_ref[...])
        nm_ref[...] = nm
        nv_ref[...] = nv

    blk = pl.BlockSpec((tr, lanes), lambda i: (i, 0))
    return pl.pallas_call(
        body, name=name, grid=(rows // tr,),
        in_specs=[pl.BlockSpec((N_DEV, tr, lanes), lambda i: (0, i, 0)), blk, blk, blk],
        out_specs=[blk] * 4, out_shape=[jax.ShapeDtypeStruct((rows, lanes), F32)] * 4,
        compiler_params=_cparams(("parallel",)),
    )(parts, w, m, v)


def _to_heads(t):
    s = t.shape[0]
    return t.reshape(s, N_GROUP_HEADS, HEAD_DIM).transpose(1, 0, 2)


def _from_heads(t):
    s = t.shape[1]
    return t.transpose(1, 0, 2).reshape(s, GROUP_W)


def kernel(x, attn_norm_g, w_in, forget_bias, fox_out_g, sb_out_g, w_out, ffn_norm_g, w_up, conv_w, conv_b, w_down, final_norm_g, loss_target, m_attn_norm_g, m_w_in, m_forget_bias, m_fox_out_g, m_sb_out_g, m_w_out, m_ffn_norm_g, m_w_up, m_conv_w, m_conv_b, m_w_down, m_final_norm_g, v_attn_norm_g, v_w_in, v_forget_bias, v_fox_out_g, v_sb_out_g, v_w_out, v_ffn_norm_g, v_w_up, v_conv_w, v_conv_b, v_w_down, v_final_norm_g):
    s = x.shape[1]
    xs = x[0]
    tq = min(ATTN_TQ, s)
    tk_fox = min(FOX_TK, s)
    tk_sb = min(SB_TK, s)
    in_shard, up_shard, out_shard, down_shard = IN_COLS // N_DEV, 2 * D_FF // N_DEV, D_MODEL // N_DEV, D_FF // N_DEV

    cw = conv_w[0]
    cw_hi = cw.astype(BF16)
    cw_lo = (cw - cw_hi.astype(F32)).astype(BF16)
    wpack = _pack([w_in[0], w_out[0], w_up[0], w_down[0], cw_hi, cw_lo], PACK_ROWS, BF16)
    gathered = _all_gather(wpack)
    g_in, g_out, g_up, g_down, g_chi, g_clo = _unpack(
        gathered, [(D_MODEL, in_shard), (out_shard, D_MODEL), (D_MODEL, up_shard), (down_shard, D_MODEL),
                   (3, up_shard), (3, up_shard)], lead=(N_DEV,))
    w_in_f = g_in.transpose(1, 0, 2).reshape(D_MODEL, IN_COLS)
    n_gate = QKV_W + N_GROUP_HEADS
    w_in_p = jnp.concatenate([w_in_f[:, :n_gate], jnp.zeros((D_MODEL, GATE_PAD - N_GROUP_HEADS), BF16),
                              w_in_f[:, n_gate:]], axis=1)
    w_out_f = g_out.reshape(D_MODEL, D_MODEL)
    w_up_f = g_up.transpose(1, 0, 2).reshape(D_MODEL, 2 * D_FF)
    w_down_f = g_down.reshape(D_FF, D_MODEL)
    conv_w_f = (g_chi.astype(F32) + g_clo.astype(F32)).transpose(1, 0, 2).reshape(3, 2 * D_FF)
    conv_w2 = conv_w_f.reshape(3, 2, D_FF).transpose(1, 0, 2)
    conv_b2 = conv_b.reshape(2, 1, D_FF)

    h1 = _rms_fwd(xs, attn_norm_g)
    proj = _mm_nn(h1, w_in_p, F32, "in_proj", tn=640)
    fox_qkv = proj[:, :QKV_W].astype(BF16)
    f_logit = proj[:, QKV_W:QKV_W + N_GROUP_HEADS]
    sb_qkv = proj[:, QKV_W + GATE_PAD:].astype(BF16)
    fq, fk, fv = (_to_heads(fox_qkv[:, j * GROUP_W:(j + 1) * GROUP_W]) for j in range(3))
    sq, sk, sv = (_to_heads(sb_qkv[:, j * GROUP_W:(j + 1) * GROUP_W]) for j in range(3))

    f_logit_h = f_logit.T.reshape(N_GROUP_HEADS, s // LANES, LANES)
    bias_h = jnp.broadcast_to(forget_bias.reshape(N_GROUP_HEADS, 1, 1), (N_GROUP_HEADS, 1, LANES))
    big_f = _forget_fwd(f_logit_h, bias_h)
    f_col = big_f.reshape(N_GROUP_HEADS, s, 1)
    f_row = big_f.reshape(N_GROUP_HEADS, s // tk_fox, 1, tk_fox)

    o_fox_h, lse = _fox_fwd(fq, fk, fv, f_col, f_row, tq, tk_fox)
    o_sb_h, sb_tot = _sb_fwd(sq, sk, sv, tq, tk_sb)
    o_fox = _from_heads(o_fox_h)
    o_sb = _from_heads(o_sb_h)
    o_n = _group_rms_fwd(o_fox, o_sb, fox_out_g, sb_out_g)
    x1 = _mm_nn(o_n, w_out_f, F32, "out_proj", resid=xs)
    h2 = _rms_fwd(x1, ffn_norm_g)
    up = _mm_up(h2, w_up_f)
    act = _conv_gate_fwd(up, conv_w2, conv_b2)
    x2 = _mm_nn(act, w_down_f, F32, "down_proj", resid=x1, tk=1408)

    d_x2, d_x2b, dg_final, loss_part = _loss_head(x2, loss_target[0], final_norm_g.reshape(1, D_MODEL))
    d_act = _mm_nt(d_x2b, w_down_f, BF16, "d_act", tn=1408)
    dw_down = _mm_tn(act, d_x2b, "d_w_down", tm=1408)
    d_up, dcw2, dcb2 = _conv_gate_bwd(up, d_act, conv_w2, conv_b2)
    d_h2 = _mm_dup_nt(d_up, w_up_f)
    dw_up = _mm_dwup_tn(h2, d_up)
    d_x1, d_x1b, dg_ffn = _rms_bwd(x1, d_h2, ffn_norm_g, d_x2, dy_col=0, name="ffn_norm_bwd", want_bf16=True)
    d_on = _mm_nt(d_x1b, w_out_f, F32, "d_o_normed")
    dw_out = _mm_tn(o_n, d_x1b, "d_w_out")
    d_o_fox, dg_fox = _rms_bwd(o_fox, d_on, fox_out_g, None, dy_col=0, name="fox_norm_bwd", want_bf16=False)
    d_o_sb, dg_sb = _rms_bwd(o_sb, d_on, sb_out_g, None, dy_col=1, name="sb_norm_bwd", want_bf16=False)

    dfq, dfk, dfv, ksum8, qsum = _fox_bwd(fq, fk, fv, f_col, f_row, o_fox_h, lse, _to_heads(d_o_fox), tq, tk_fox)
    dsq, dsk, dsv = _sb_bwd(sq, sk, sv, sb_tot, _to_heads(d_o_sb), tq, tk_sb)
    ksum = jnp.sum(ksum8, axis=2).reshape(N_GROUP_HEADS, s // LANES, LANES)
    d_f_logit_h, d_bias_h = _forget_bwd(f_logit_h, bias_h, ksum,
                                        qsum.reshape(N_GROUP_HEADS, s // LANES, LANES))
    d_f_logit = d_f_logit_h.reshape(N_GROUP_HEADS, s).T

    d_proj = jnp.concatenate(
        [_from_heads(t).astype(BF16) for t in (dfq, dfk, dfv)]
        + [d_f_logit.astype(BF16), jnp.zeros((s, GATE_PAD - N_GROUP_HEADS), BF16)]
        + [_from_heads(t).astype(BF16) for t in (dsq, dsk, dsv)], axis=1)
    d_h1 = _mm_nt(d_proj, w_in_p, F32, "d_h1", tk=640)
    dw_in_p = _mm_tn(h1, d_proj, "d_w_in", tn=640)
    dw_in = jnp.concatenate([dw_in_p[:, :n_gate], dw_in_p[:, QKV_W + GATE_PAD:]], axis=1)
    grad_x, dg_attn = _rms_bwd(xs, d_h1, attn_norm_g, d_x1, dy_col=0, name="attn_norm_bwd", want_bf16=False)

    dconv_w = dcw2.transpose(1, 0, 2).reshape(3, 2 * D_FF)
    dconv_b = dcb2.reshape(1, 2 * D_FF)
    gpack = jnp.stack([
        _pack([dw_in[:, j * in_shard:(j + 1) * in_shard], dw_out[j * out_shard:(j + 1) * out_shard],
               dw_up[:, j * up_shard:(j + 1) * up_shard], dw_down[j * down_shard:(j + 1) * down_shard],
               dconv_w[:, j * up_shard:(j + 1) * up_shard]], PACK_ROWS, BF16)
        for j in range(N_DEV)])
    small_shapes = [(1, D_MODEL), (1, N_GROUP_HEADS), (1, GROUP_W), (1, GROUP_W), (1, D_MODEL),
                    (1, 2 * D_FF), (D_MODEL,), (1,)]
    spack = _pack([dg_attn, d_bias_h[:, 0, 0], dg_fox, dg_sb, dg_ffn, dconv_b, dg_final, loss_part[0, 0:1]],
                  SMALL_ROWS, F32)
    grecv, srecv = _grad_exchange(gpack, spack)

    shard_shapes = [(D_MODEL, in_shard), (out_shard, D_MODEL), (D_MODEL, up_shard), (down_shard, D_MODEL),
                    (3, up_shard)]

    def shard_pack(a_in, a_out, a_up, a_down, a_conv):
        return _pack([a_in[0], a_out[0], a_up[0], a_down[0], a_conv[0]], PACK_ROWS, F32)

    big = _sum_adamw(grecv, shard_pack(w_in, w_out, w_up, w_down, conv_w),
                     shard_pack(m_w_in, m_w_out, m_w_up, m_w_down, m_conv_w),
                     shard_pack(v_w_in, v_w_out, v_w_up, v_w_down, v_conv_w), "adamw_sharded")

    def small_pack(a_attn, a_bias, a_fox, a_sb, a_ffn, a_cb, a_fin):
        return _pack([a_attn, a_bias, a_fox, a_sb, a_ffn, a_cb, a_fin, jnp.zeros((1,), F32)], SMALL_ROWS, F32)

    small = _sum_adamw(srecv, small_pack(attn_norm_g, forget_bias, fox_out_g, sb_out_g, ffn_norm_g, conv_b, final_norm_g),
                       small_pack(m_attn_norm_g, m_forget_bias, m_fox_out_g, m_sb_out_g, m_ffn_norm_g, m_conv_b, m_final_norm_g),
                       small_pack(v_attn_norm_g, v_forget_bias, v_fox_out_g, v_sb_out_g, v_ffn_norm_g, v_conv_b, v_final_norm_g),
                       "adamw_replicated", tr=SMALL_ROWS)

    outs = []
    loss = None
    for kind in range(4):
        b_in, b_out, b_up, b_down, b_conv = _unpack(big[kind], shard_shapes)
        s_attn, s_bias, s_fox, s_sb, s_ffn, s_cb, s_fin, s_loss = _unpack(small[kind], small_shapes)
        if kind == 0:
            loss = s_loss[0]
        outs += [s_attn, b_in[None], s_bias, s_fox, s_sb, b_out[None], s_ffn, b_up[None], b_conv[None], s_cb,
                 b_down[None], s_fin]
    return (loss, grad_x[None], *outs)
```

```python
import jax
import jax.numpy as jnp
from jax import lax
from jax.experimental import pallas as pl
from jax.experimental.pallas import tpu as pltpu

F32 = jnp.float32
BF16 = jnp.bfloat16

D_MODEL = 1024
HEAD_DIM = 64
N_GROUP_HEADS = 8
GROUP_W = N_GROUP_HEADS * HEAD_DIM
QKV_W = 3 * GROUP_W
IN_COLS = 2 * QKV_W + N_GROUP_HEADS
GATE_PAD = 128
IN_COLS_PAD = 2 * QKV_W + GATE_PAD
D_FF = 2816
N_DEV = 8
EPS = 1e-6
Q_SCALE = HEAD_DIM ** -0.5

ADAM_LR = 0.001
ADAM_B1 = 0.9
ADAM_B2 = 0.999
ADAM_EPS = 1e-08
ADAM_WD = 0.01
ADAM_STEP = 10

LANES = 128
PACK_ROWS = 12800
SMALL_ROWS = 80
VMEM_LIMIT = 56 * 1024 * 1024
NEG_BIG = -1e30
ATTN_TQ = 512
FOX_TK = 512
SB_TK = 256
MESH = pl.DeviceIdType.MESH


def _cparams(sem=None, **kw):
    return pltpu.CompilerParams(dimension_semantics=sem, vmem_limit_bytes=VMEM_LIMIT, **kw)


def _tile(n, target, mult=LANES):
    if n <= target:
        return n
    t = (target // mult) * mult
    while t >= mult:
        if n % t == 0:
            return t
        t -= mult
    return n


def _seg_len(shape):
    n = 1
    for s in shape:
        n *= s
    return -(-n // LANES) * LANES


def _pack(arrs, rows, dtype):
    parts = []
    for a in arrs:
        f = a.reshape(-1).astype(dtype)
        parts.append(jnp.pad(f, (0, _seg_len(a.shape) - f.shape[0])))
    flat = jnp.concatenate(parts)
    flat = jnp.pad(flat, (0, rows * LANES - flat.shape[0]))
    return flat.reshape(rows, LANES)


def _unpack(p, shapes, lead=()):
    flat = p.reshape(lead + (-1,))
    out, off = [], 0
    for shp in shapes:
        n = 1
        for s in shp:
            n *= s
        out.append(flat[..., off:off + n].reshape(lead + tuple(shp)))
        off += _seg_len(shp)
    return out


def _my_pos():
    return lax.axis_index("x"), lax.axis_index("y"), lax.axis_index("c")


def _all_gather(block):
    rows, lanes = block.shape

    def body(x_ref, out_ref, send_sems, recv_sems, local_sem):
        x, y, c = _my_pos()
        me, sibling = (x, y, c), (x, y, 1 - c)
        chips = [(1 - x, y), (x, 1 - y), (1 - x, 1 - y)]

        def slot(px, py, pc):
            return out_ref.at[4 * px + 2 * py + pc]

        def copy(k, blk, to, src=None):
            return pltpu.make_async_remote_copy(
                src_ref=slot(*blk) if src is None else src, dst_ref=slot(*blk),
                send_sem=send_sems.at[k], recv_sem=recv_sems.at[k],
                device_id=to, device_id_type=MESH)

        mine = pltpu.make_async_copy(x_ref, slot(*me), local_sem)
        mine.start()
        first = [copy(0, me, sibling, src=x_ref)]
        first += [copy(1 + j, me, (*chip, c), src=x_ref) for j, chip in enumerate(chips)]
        for cp in first:
            cp.start()
        passed = [copy(4 + j, (*chip, c), sibling) for j, chip in enumerate(chips)]
        for j, chip in enumerate(chips):
            copy(1 + j, (*chip, c), me).wait_recv()
            passed[j].start()
        copy(0, sibling, me).wait_recv()
        for j, chip in enumerate(chips):
            copy(4 + j, (*chip, 1 - c), me).wait_recv()
        for cp in first + passed:
            cp.wait_send()
        mine.wait()

    return pl.pallas_call(
        body, name="weights_all_gather",
        out_shape=jax.ShapeDtypeStruct((N_DEV, rows, lanes), block.dtype),
        in_specs=[pl.BlockSpec(memory_space=pl.ANY)],
        out_specs=pl.BlockSpec(memory_space=pl.ANY),
        scratch_shapes=[pltpu.SemaphoreType.DMA((7,)), pltpu.SemaphoreType.DMA((7,)),
                        pltpu.SemaphoreType.DMA],
    )(block)


def _grad_exchange(gpack, spack):
    _, rows, lanes = gpack.shape
    srows = spack.shape[0]

    def body(g_ref, s_ref, grecv_ref, srecv_ref, send_sems, recv_sems, local_sems):
        x, y, c = _my_pos()
        my_id = 4 * x + 2 * y + c
        own_g = pltpu.make_async_copy(g_ref.at[my_id], grecv_ref.at[my_id], local_sems.at[0])
        own_s = pltpu.make_async_copy(s_ref, srecv_ref.at[my_id], local_sems.at[1])
        own_g.start()
        own_s.start()
        sends, arrivals = [], []
        for k in range(1, N_DEV):
            px, py, pc = x ^ (k >> 2), y ^ ((k >> 1) & 1), c ^ (k & 1)
            peer_id = 4 * px + 2 * py + pc
            for a, (src, dst) in enumerate(((g_ref.at[peer_id], grecv_ref), (s_ref, srecv_ref))):
                sends.append(pltpu.make_async_remote_copy(
                    src_ref=src, dst_ref=dst.at[my_id],
                    send_sem=send_sems.at[a, k - 1], recv_sem=recv_sems.at[a, k - 1],
                    device_id=(px, py, pc), device_id_type=MESH))
                arrivals.append(pltpu.make_async_remote_copy(
                    src_ref=src, dst_ref=dst.at[peer_id],
                    send_sem=send_sems.at[a, k - 1], recv_sem=recv_sems.at[a, k - 1],
                    device_id=(px, py, pc), device_id_type=MESH))
        for cp in sends:
            cp.start()
        for cp in arrivals:
            cp.wait_recv()
        for cp in sends:
            cp.wait_send()
        own_g.wait()
        own_s.wait()

    return pl.pallas_call(
        body, name="grad_exchange",
        out_shape=(jax.ShapeDtypeStruct((N_DEV, rows, lanes), gpack.dtype),
                   jax.ShapeDtypeStruct((N_DEV, srows, lanes), spack.dtype)),
        in_specs=[pl.BlockSpec(memory_space=pl.ANY), pl.BlockSpec(memory_space=pl.ANY)],
        out_specs=(pl.BlockSpec(memory_space=pl.ANY), pl.BlockSpec(memory_space=pl.ANY)),
        scratch_shapes=[pltpu.SemaphoreType.DMA((2, 7)), pltpu.SemaphoreType.DMA((2, 7)),
                        pltpu.SemaphoreType.DMA((2,))],
    )(gpack, spack)


_DIMS = {"nn": (((1,), (0,)), ((), ())), "nt": (((1,), (1,)), ((), ())), "tn": (((0,), (0,)), ((), ()))}


def _matmul(a, b, *, mode, grid, a_block, a_map, b_block, b_map, o_block, o_map, out_shape, name,
            resid=None):
    nk = grid[2]
    dims = _DIMS[mode]

    def body(*refs):
        if resid is None:
            a_ref, b_ref, o_ref, acc_ref = refs
            r_ref = None
        else:
            a_ref, b_ref, r_ref, o_ref, acc_ref = refs
        k = pl.program_id(2)

        @pl.when(k == 0)
        def _():
            acc_ref[...] = jnp.zeros_like(acc_ref)

        acc_ref[...] += lax.dot_general(a_ref[...], b_ref[...], dims, preferred_element_type=F32)

        @pl.when(k == nk - 1)
        def _():
            res = acc_ref[...]
            if r_ref is not None:
                res = r_ref[...] + res
            o_ref[...] = res.astype(o_ref.dtype)

    in_specs = [pl.BlockSpec(a_block, a_map), pl.BlockSpec(b_block, b_map)]
    args = [a, b]
    if resid is not None:
        in_specs.append(pl.BlockSpec(o_block, o_map))
        args.append(resid)
    acc_shape = tuple(d for d in o_block if d is not None)
    return pl.pallas_call(
        body, name=name, grid=grid, in_specs=in_specs,
        out_specs=pl.BlockSpec(o_block, o_map), out_shape=out_shape,
        scratch_shapes=[pltpu.VMEM(acc_shape, F32)],
        compiler_params=_cparams(("parallel", "parallel", "arbitrary")),
    )(*args)


def _mm_nn(a, b, out_dtype, name, resid=None, tm=512, tn=512, tk=1024):
    m, kk = a.shape
    n = b.shape[1]
    tm, tn, tk = _tile(m, tm, 8), _tile(n, tn), _tile(kk, tk)
    return _matmul(a, b, mode="nn", grid=(m // tm, n // tn, kk // tk),
                   a_block=(tm, tk), a_map=lambda i, j, k: (i, k),
                   b_block=(tk, tn), b_map=lambda i, j, k: (k, j),
                   o_block=(tm, tn), o_map=lambda i, j, k: (i, j),
                   out_shape=jax.ShapeDtypeStruct((m, n), out_dtype), name=name, resid=resid)


def _mm_nt(a, b, out_dtype, name, tm=512, tn=512, tk=1024):
    m, kk = a.shape
    n = b.shape[0]
    tm, tn, tk = _tile(m, tm, 8), _tile(n, tn), _tile(kk, tk)
    return _matmul(a, b, mode="nt", grid=(m // tm, n // tn, kk // tk),
                   a_block=(tm, tk), a_map=lambda i, j, k: (i, k),
                   b_block=(tn, tk), b_map=lambda i, j, k: (j, k),
                   o_block=(tm, tn), o_map=lambda i, j, k: (i, j),
                   out_shape=jax.ShapeDtypeStruct((m, n), out_dtype), name=name)


def _mm_tn(a, b, name, tm=512, tn=512, tk=1024):
    kk, m = a.shape
    n = b.shape[1]
    tm, tn, tk = _tile(m, tm), _tile(n, tn), _tile(kk, tk, 8)
    return _matmul(a, b, mode="tn", grid=(m // tm, n // tn, kk // tk),
                   a_block=(tk, tm), a_map=lambda i, j, k: (k, i),
                   b_block=(tk, tn), b_map=lambda i, j, k: (k, j),
                   o_block=(tm, tn), o_map=lambda i, j, k: (i, j),
                   out_shape=jax.ShapeDtypeStruct((m, n), F32), name=name)


def _mm_up(h, w_up, tm=512, tn=256):
    s = h.shape[0]
    tm = _tile(s, tm, 8)
    nh = D_FF // tn
    return _matmul(h, w_up, mode="nn", grid=(s // tm, 2 * nh, 1),
                   a_block=(tm, D_MODEL), a_map=lambda i, j, k: (i, 0),
                   b_block=(D_MODEL, tn), b_map=lambda i, j, k: (0, j),
                   o_block=(None, tm, tn), o_map=lambda i, j, k: (j // nh, i, j % nh),
                   out_shape=jax.ShapeDtypeStruct((2, s, D_FF), F32), name="up_proj")


def _mm_dup_nt(dup, w_up, tm=512, tk=1408):
    s = dup.shape[1]
    tm = _tile(s, tm, 8)
    nh = D_FF // tk
    return _matmul(dup, w_up, mode="nt", grid=(s // tm, 1, 2 * nh),
                   a_block=(None, tm, tk), a_map=lambda i, j, k: (k // nh, i, k % nh),
                   b_block=(D_MODEL, tk), b_map=lambda i, j, k: (0, k),
                   o_block=(tm, D_MODEL), o_map=lambda i, j, k: (i, 0),
                   out_shape=jax.ShapeDtypeStruct((s, D_MODEL), F32), name="d_h2")


def _mm_dwup_tn(h, dup, tn=256, tk=1024):
    s = h.shape[0]
    tk = _tile(s, tk, 8)
    nh = D_FF // tn
    return _matmul(h, dup, mode="tn", grid=(1, 2 * nh, s // tk),
                   a_block=(tk, D_MODEL), a_map=lambda i, j, k: (k, 0),
                   b_block=(None, tk, tn), b_map=lambda i, j, k: (j // nh, k, j % nh),
                   o_block=(D_MODEL, tn), o_map=lambda i, j, k: (0, j),
                   out_shape=jax.ShapeDtypeStruct((D_MODEL, 2 * D_FF), F32), name="d_w_up")


def _rms_fwd(x, g, tr=256):
    s, d = x.shape
    tr = _tile(s, tr, 8)

    def body(x_ref, g_ref, o_ref):
        xv = x_ref[...]
        r = lax.rsqrt(jnp.mean(xv * xv, axis=-1, keepdims=True) + EPS)
        o_ref[...] = (xv * r * g_ref[...]).astype(o_ref.dtype)

    return pl.pallas_call(
        body, name="rms_fwd", grid=(s // tr,),
        in_specs=[pl.BlockSpec((tr, d), lambda i: (i, 0)), pl.BlockSpec((1, d), lambda i: (0, 0))],
        out_specs=pl.BlockSpec((tr, d), lambda i: (i, 0)),
        out_shape=jax.ShapeDtypeStruct((s, d), BF16),
        compiler_params=_cparams(("parallel",)),
    )(x, g)


def _group_rms_fwd(o_fox, o_sb, g_fox, g_sb, tr=256):
    s, d = o_fox.shape
    tr = _tile(s, tr, 8)

    def body(a_ref, b_ref, ga_ref, gb_ref, o_ref):
        for src, g_ref, lo in ((a_ref, ga_ref, 0), (b_ref, gb_ref, d)):
            xv = src[...]
            r = lax.rsqrt(jnp.mean(xv * xv, axis=-1, keepdims=True) + EPS)
            o_ref[:, lo:lo + d] = (xv * r * g_ref[...]).astype(o_ref.dtype)

    row = pl.BlockSpec((tr, d), lambda i: (i, 0))
    gain = pl.BlockSpec((1, d), lambda i: (0, 0))
    return pl.pallas_call(
        body, name="group_rms_fwd", grid=(s // tr,),
        in_specs=[row, row, gain, gain],
        out_specs=pl.BlockSpec((tr, 2 * d), lambda i: (i, 0)),
        out_shape=jax.ShapeDtypeStruct((s, 2 * d), BF16),
        compiler_params=_cparams(("parallel",)),
    )(o_fox, o_sb, g_fox, g_sb)


def _rms_bwd(x, dy, g, resid, *, dy_col, name, want_bf16, tr=256):
    s, d = x.shape
    tr = _tile(s, tr, 8)
    has_resid = resid is not None

    def body(*refs):
        refs = list(refs)
        x_ref, dy_ref, g_ref = refs[:3]
        r_ref = refs[3] if has_resid else None
        outs = refs[4:] if has_resid else refs[3:]
        dx_ref = outs[0]
        dxb_ref = outs[1] if want_bf16 else None
        dg_ref = outs[-1]

        @pl.when(pl.program_id(0) == 0)
        def _():
            dg_ref[...] = jnp.zeros_like(dg_ref)

        xv = x_ref[...]
        dyv = dy_ref[...]
        r = lax.rsqrt(jnp.mean(xv * xv, axis=-1, keepdims=True) + EPS)
        xh = xv * r
        gy = dyv * g_ref[...]
        dx = r * (gy - xh * jnp.mean(xh * gy, axis=-1, keepdims=True))
        if r_ref is not None:
            dx = r_ref[...] + dx
        dx_ref[...] = dx
        if dxb_ref is not None:
            dxb_ref[...] = dx.astype(BF16)
        dg_ref[...] += jnp.sum(dyv * xh, axis=0, keepdims=True)

    row = pl.BlockSpec((tr, d), lambda i: (i, 0))
    in_specs = [row, pl.BlockSpec((tr, d), lambda i: (i, dy_col)), pl.BlockSpec((1, d), lambda i: (0, 0))]
    args = [x, dy, g]
    if has_resid:
        in_specs.append(row)
        args.append(resid)
    out_specs = [row]
    out_shape = [jax.ShapeDtypeStruct((s, d), F32)]
    if want_bf16:
        out_specs.append(row)
        out_shape.append(jax.ShapeDtypeStruct((s, d), BF16))
    out_specs.append(pl.BlockSpec((1, d), lambda i: (0, 0)))
    out_shape.append(jax.ShapeDtypeStruct((1, d), F32))
    return pl.pallas_call(
        body, name=name, grid=(s // tr,), in_specs=in_specs, out_specs=out_specs, out_shape=out_shape,
        compiler_params=_cparams(("arbitrary",)),
    )(*args)


def _loss_head(x2, target, g, tr=256):
    s, d = x2.shape
    tr = _tile(s, tr, 8)

    def body(x_ref, t_ref, g_ref, dx_ref, dxb_ref, dg_ref, loss_ref):
        @pl.when(pl.program_id(0) == 0)
        def _():
            dg_ref[...] = jnp.zeros_like(dg_ref)
            loss_ref[...] = jnp.zeros_like(loss_ref)

        xv = x_ref[...]
        gv = g_ref[...]
        r = lax.rsqrt(jnp.mean(xv * xv, axis=-1, keepdims=True) + EPS)
        xh = xv * r
        err = xh * gv - t_ref[...]
        loss_ref[...] += jnp.sum(jnp.mean(err * err, axis=-1, keepdims=True), axis=0, keepdims=True) * 0.5
        dyv = err * (1.0 / d)
        gy = dyv * gv
        dx = r * (gy - xh * jnp.mean(xh * gy, axis=-1, keepdims=True))
        dx_ref[...] = dx
        dxb_ref[...] = dx.astype(BF16)
        dg_ref[...] += jnp.sum(dyv * xh, axis=0, keepdims=True)

    row = pl.BlockSpec((tr, d), lambda i: (i, 0))
    return pl.pallas_call(
        body, name="loss_head", grid=(s // tr,),
        in_specs=[row, row, pl.BlockSpec((1, d), lambda i: (0, 0))],
        out_specs=[row, row, pl.BlockSpec((1, d), lambda i: (0, 0)), pl.BlockSpec((1, LANES), lambda i: (0, 0))],
        out_shape=[jax.ShapeDtypeStruct((s, d), F32), jax.ShapeDtypeStruct((s, d), BF16),
                   jax.ShapeDtypeStruct((1, d), F32), jax.ShapeDtypeStruct((1, LANES), F32)],
        compiler_params=_cparams(("arbitrary",)),
    )(x2, target, g)


def _conv_taps(cur, prev8, w, b, first):
    prev8 = jnp.where(first, 0.0, prev8)
    ext = jnp.concatenate([prev8, cur], axis=0)
    x1 = pltpu.roll(ext, 1, 0)[8:]
    x2 = pltpu.roll(ext, 2, 0)[8:]
    u = b + w[0:1] * x2
    u = u + w[1:2] * x1
    u = u + w[2:3] * cur
    return u, x1, x2


def _conv_gate_fwd(up, conv_w, conv_b, tm=512, tn=256):
    s = up.shape[1]
    tm = _tile(s, tm, 8)
    nrb = s // tm
    rb8 = tm // 8

    def body(g_ref, v_ref, gp_ref, vp_ref, wg_ref, wv_ref, bg_ref, bv_ref, o_ref):
        first = pl.program_id(1) == 0
        ug, _, _ = _conv_taps(g_ref[...], gp_ref[...], wg_ref[...], bg_ref[...], first)
        uv, _, _ = _conv_taps(v_ref[...], vp_ref[...], wv_ref[...], bv_ref[...], first)
        sg = 1.0 / (1.0 + jnp.exp(-ug))
        o_ref[...] = (ug * sg * uv).astype(o_ref.dtype)

    def cur(h):
        return pl.BlockSpec((None, tm, tn), lambda j, i: (h, i, j))

    def prev(h):
        return pl.BlockSpec((None, 8, tn), lambda j, i: (h, jnp.maximum(i * rb8 - 1, 0), j))

    def par(h, r):
        return pl.BlockSpec((None, r, tn), lambda j, i: (h, 0, j))

    return pl.pallas_call(
        body, name="conv_gate_fwd", grid=(D_FF // tn, nrb),
        in_specs=[cur(0), cur(1), prev(0), prev(1), par(0, 3), par(1, 3), par(0, 1), par(1, 1)],
        out_specs=pl.BlockSpec((tm, tn), lambda j, i: (i, j)),
        out_shape=jax.ShapeDtypeStruct((s, D_FF), BF16),
        compiler_params=_cparams(("parallel", "parallel")),
    )(up, up, up, up, conv_w, conv_w, conv_b, conv_b)


def _conv_gate_bwd(up, dact, conv_w, conv_b, tm=512, tn=256):
    s = up.shape[1]
    tm = _tile(s, tm, 8)
    nrb = s // tm
    rb8 = tm // 8

    def body(g_ref, v_ref, gp_ref, vp_ref, da_ref, wg_ref, wv_ref, bg_ref, bv_ref,
             dup_ref, dcw_ref, dcb_ref, carry_ref):
        i = pl.program_id(1)
        first = i == nrb - 1

        @pl.when(i == 0)
        def _():
            carry_ref[...] = jnp.zeros_like(carry_ref)
            dcw_ref[...] = jnp.zeros_like(dcw_ref)
            dcb_ref[...] = jnp.zeros_like(dcb_ref)

        curs = (g_ref[...], v_ref[...])
        ws = (wg_ref[...], wv_ref[...])
        ug, g1, g2 = _conv_taps(curs[0], gp_ref[...], ws[0], bg_ref[...], first)
        uv, v1, v2 = _conv_taps(curs[1], vp_ref[...], ws[1], bv_ref[...], first)
        sg = 1.0 / (1.0 + jnp.exp(-ug))
        da = da_ref[...].astype(F32)
        d_v = da * (ug * sg)
        d_g = da * uv * (sg * (1.0 + ug * (1.0 - sg)))
        for h, (du, x0, x1, x2) in enumerate(((d_g, curs[0], g1, g2), (d_v, curs[1], v1, v2))):
            dcb_ref[h] += jnp.sum(du, axis=0, keepdims=True)
            dcw_ref[h, 0:1, :] += jnp.sum(du * x2, axis=0, keepdims=True)
            dcw_ref[h, 1:2, :] += jnp.sum(du * x1, axis=0, keepdims=True)
            dcw_ref[h, 2:3, :] += jnp.sum(du * x0, axis=0, keepdims=True)
            ext = jnp.concatenate([du, carry_ref[h]], axis=0)
            n1 = pltpu.roll(ext, tm + 7, 0)[:tm]
            n2 = pltpu.roll(ext, tm + 6, 0)[:tm]
            w = ws[h]
            dup_ref[h] = (w[2:3] * du + w[1:2] * n1 + w[0:1] * n2).astype(dup_ref.dtype)
            carry_ref[h] = du[:8]

    def cur(h):
        return pl.BlockSpec((None, tm, tn), lambda j, i: (h, nrb - 1 - i, j))

    def prev(h):
        return pl.BlockSpec((None, 8, tn), lambda j, i: (h, jnp.maximum((nrb - 1 - i) * rb8 - 1, 0), j))

    def par(h, r):
        return pl.BlockSpec((None, r, tn), lambda j, i: (h, 0, j))

    return pl.pallas_call(
        body, name="conv_gate_bwd", grid=(D_FF // tn, nrb),
        in_specs=[cur(0), cur(1), prev(0), prev(1),
                  pl.BlockSpec((tm, tn), lambda j, i: (nrb - 1 - i, j)),
                  par(0, 3), par(1, 3), par(0, 1), par(1, 1)],
        out_specs=[pl.BlockSpec((2, tm, tn), lambda j, i: (0, nrb - 1 - i, j)),
                   pl.BlockSpec((2, 3, tn), lambda j, i: (0, 0, j)),
                   pl.BlockSpec((2, 1, tn), lambda j, i: (0, 0, j))],
        out_shape=[jax.ShapeDtypeStruct((2, s, D_FF), BF16),
                   jax.ShapeDtypeStruct((2, 3, D_FF), F32),
                   jax.ShapeDtypeStruct((2, 1, D_FF), F32)],
        scratch_shapes=[pltpu.VMEM((2, 8, tn), F32)],
        compiler_params=_cparams(("parallel", "arbitrary")),
    )(up, up, up, up, dact, conv_w, conv_w, conv_b, conv_b)


def _split_dot(x, tri, terms):
    piece = x.astype(BF16)
    out = jnp.dot(piece, tri, preferred_element_type=F32)
    rest = x
    for _ in range(terms - 1):
        rest = rest - piece.astype(F32)
        piece = rest.astype(BF16)
        out = out + jnp.dot(piece, tri, preferred_element_type=F32)
    return out


def _split_dot_rhs(tri, x, terms):
    piece = x.astype(BF16)
    out = jnp.dot(tri, piece, preferred_element_type=F32)
    rest = x
    for _ in range(terms - 1):
        rest = rest - piece.astype(F32)
        piece = rest.astype(BF16)
        out = out + jnp.dot(tri, piece, preferred_element_type=F32)
    return out


def _tri(n, kind):
    r = lax.broadcasted_iota(jnp.int32, (n, n), 0)
    c = lax.broadcasted_iota(jnp.int32, (n, n), 1)
    cond = {"le": r <= c, "ge": r >= c, "lt": r < c, "gt": r > c}[kind]
    return jnp.where(cond, 1.0, 0.0).astype(BF16)


def _log_sigmoid(x):
    return jnp.minimum(x, 0.0) - jnp.log(1.0 + jnp.exp(-jnp.abs(x)))


def _forget_fwd(f_logit, bias):
    h, r, _ = f_logit.shape

    def body(x_ref, b_ref, o_ref):
        lf = _log_sigmoid(x_ref[...] + b_ref[...])
        within = _split_dot(lf, _tri(LANES, "le"), 3)
        row_tot = jnp.broadcast_to(within[:, LANES - 1:LANES], (r, LANES))
        before = _split_dot_rhs(_tri(r, "gt"), row_tot, 3)
        o_ref[...] = within + before

    blk = pl.BlockSpec((None, r, LANES), lambda i: (i, 0, 0))
    return pl.pallas_call(
        body, name="forget_cumsum_fwd", grid=(h,),
        in_specs=[blk, pl.BlockSpec((None, 1, LANES), lambda i: (i, 0, 0))],
        out_specs=blk, out_shape=jax.ShapeDtypeStruct((h, r, LANES), F32),
        compiler_params=_cparams(("parallel",)),
    )(f_logit, bias)


def _forget_bwd(f_logit, bias, ksum, qsum):
    h, r, _ = f_logit.shape

    def body(x_ref, b_ref, k_ref, q_ref, dx_ref, db_ref):
        d_f = q_ref[...] - k_ref[...]
        within = _split_dot(d_f, _tri(LANES, "ge"), 3)
        row_tot = jnp.broadcast_to(within[:, 0:1], (r, LANES))
        after = _split_dot_rhs(_tri(r, "lt"), row_tot, 3)
        xv = x_ref[...] + b_ref[...]
        dx = (within + after) * jnp.exp(_log_sigmoid(-xv))
        dx_ref[...] = dx
        db_ref[...] = jnp.broadcast_to(jnp.sum(dx), (1, LANES))

    blk = pl.BlockSpec((None, r, LANES), lambda i: (i, 0, 0))
    one = pl.BlockSpec((None, 1, LANES), lambda i: (i, 0, 0))
    return pl.pallas_call(
        body, name="forget_cumsum_bwd", grid=(h,),
        in_specs=[blk, one, blk, blk], out_specs=[blk, one],
        out_shape=[jax.ShapeDtypeStruct((h, r, LANES), F32), jax.ShapeDtypeStruct((h, 1, LANES), F32)],
        compiler_params=_cparams(("parallel",)),
    )(f_logit, bias, ksum, qsum)


def _head_specs(s, tq):
    qblk = pl.BlockSpec((None, tq, HEAD_DIM), lambda h, i: (h, i, 0))
    full = pl.BlockSpec((None, s, HEAD_DIM), lambda h, i: (h, 0, 0))
    col = pl.BlockSpec((None, tq, 1), lambda h, i: (h, i, 0))
    return qblk, full, col


def _scaled(q_ref):
    return (q_ref[...].astype(F32) * Q_SCALE).astype(BF16)


_NT = (((1,), (1,)), ((), ()))
_TN = (((0,), (0,)), ((), ()))


def _fox_fwd(q, k, v_ones, f_col, f_row, tq, tk):
    h, s, _ = q.shape
    nk = s // tk
    assert tq == tk

    def body(q_ref, k_ref, v_ref, fc_ref, fr_ref, o_ref, lse_ref, m_ref, acc_ref, z0, z1):
        i = pl.program_id(1)
        qs = _scaled(q_ref)
        fq = fc_ref[...]
        m_ref[...] = jnp.full_like(m_ref, NEG_BIG)
        acc_ref[...] = jnp.zeros_like(acc_ref)

        def keys_of(j):
            return pl.ds(pl.multiple_of(j * tk, tk), tk)

        def logits(j):
            return lax.dot_general(qs, k_ref[keys_of(j), :], _NT, preferred_element_type=F32)

        def soft(j, raw, mask):
            sc = raw + fq - fr_ref[j]
            if mask is not None:
                sc = jnp.where(mask, sc, NEG_BIG)
            m_old = m_ref[...]
            m_new = jnp.maximum(m_old, jnp.max(sc, axis=-1, keepdims=True))
            p = jnp.exp(sc - m_new)
            acc_ref[...] = jnp.exp(m_old - m_new) * acc_ref[...] + jnp.dot(
                p.astype(BF16), v_ref[keys_of(j), :], preferred_element_type=F32)
            m_ref[...] = m_new

        @pl.when(i >= 2)
        def _():
            z0[...] = logits(0)

            def step(p, carry):
                j = 2 * p
                z1[...] = logits(j + 1)
                soft(j, z0[...], None)
                z0[...] = logits(jnp.minimum(j + 2, i - 1))
                soft(j + 1, z1[...], None)
                return carry

            lax.fori_loop(0, i // 2, step, 0)

        @pl.when(i % 2 == 1)
        def _():
            soft(i - 1, logits(i - 1), None)

        soft(i, logits(i), _band_mask(tq, tk, 0, 0, strict=False))
        l = acc_ref[:, HEAD_DIM:HEAD_DIM + 1]
        o_ref[...] = acc_ref[:, :HEAD_DIM] / l
        lse_ref[...] = m_ref[...] + jnp.log(l)

    qblk, full, colspec = _head_specs(s, tq)
    return pl.pallas_call(
        body, name="fox_fwd", grid=(h, s // tq),
        in_specs=[qblk, full, pl.BlockSpec((None, s, 2 * HEAD_DIM), lambda hh, i: (hh, 0, 0)), colspec,
                  pl.BlockSpec((None, nk, 1, tk), lambda hh, i: (hh, 0, 0, 0))],
        out_specs=[qblk, colspec],
        out_shape=[jax.ShapeDtypeStruct((h, s, HEAD_DIM), F32), jax.ShapeDtypeStruct((h, s, 1), F32)],
        scratch_shapes=[pltpu.VMEM((tq, 1), F32), pltpu.VMEM((tq, 2 * HEAD_DIM), F32),
                        pltpu.VMEM((tq, tk), F32), pltpu.VMEM((tq, tk), F32)],
        compiler_params=_cparams(("parallel", "parallel")),
    )(q, k, v_ones, f_col, f_row)


def _fox_bwd(q, k, v, f_col, f_row, o, lse, d_o, tq, tk):
    h, s, _ = q.shape
    nk = s // tk
    assert tq == tk

    def body(q_ref, k_ref, v_ref, fc_ref, fr_ref, o_ref, lse_ref, do_ref,
             dq_ref, dk_ref, dv_ref, ks_ref, qs_ref, dq_acc, qsum_acc, z0, z1, p0, p1):
        i = pl.program_id(1)

        @pl.when(i == 0)
        def _():
            dk_ref[...] = jnp.zeros_like(dk_ref)
            dv_ref[...] = jnp.zeros_like(dv_ref)
            ks_ref[...] = jnp.zeros_like(ks_ref)

        qs = _scaled(q_ref)
        fq = fc_ref[...]
        lse_v = lse_ref[...]
        dob = do_ref[...].astype(BF16)
        delta = jnp.sum(dob.astype(F32) * o_ref[...], axis=-1, keepdims=True)
        dq_acc[...] = jnp.zeros_like(dq_acc)
        qsum_acc[...] = jnp.zeros_like(qsum_acc)

        def keys_of(j):
            return pl.ds(pl.multiple_of(j * tk, tk), tk)

        def products(j):
            at = keys_of(j)
            return (lax.dot_general(qs, k_ref[at, :], _NT, preferred_element_type=F32),
                    lax.dot_general(dob, v_ref[at, :], _NT, preferred_element_type=F32))

        def grads(j, raw, dp, mask):
            at = keys_of(j)
            sc = raw + fq - fr_ref[j]
            if mask is not None:
                sc = jnp.where(mask, sc, NEG_BIG)
            p = jnp.exp(sc - lse_v)
            ds = p * (dp - delta)
            dsb = ds.astype(BF16)
            dq_acc[...] += jnp.dot(dsb, k_ref[at, :], preferred_element_type=F32)
            dk_ref[at, :] += lax.dot_general(dsb, qs, _TN, preferred_element_type=F32)
            dv_ref[at, :] += lax.dot_general(p.astype(BF16), dob, _TN, preferred_element_type=F32)
            ks_ref[j] += jnp.sum(ds.reshape(tq // 8, 8, tk), axis=0)
            qsum_acc[...] += jnp.sum(ds, axis=-1, keepdims=True)

        @pl.when(i >= 2)
        def _():
            z0[...], p0[...] = products(0)

            def step(pp, carry):
                j = 2 * pp
                z1[...], p1[...] = products(j + 1)
                grads(j, z0[...], p0[...], None)
                z0[...], p0[...] = products(jnp.minimum(j + 2, i - 1))
                grads(j + 1, z1[...], p1[...], None)
                return carry

            lax.fori_loop(0, i // 2, step, 0)

        @pl.when(i % 2 == 1)
        def _():
            grads(i - 1, *products(i - 1), None)

        grads(i, *products(i), _band_mask(tq, tk, 0, 0, strict=False))
        dq_ref[...] = dq_acc[...] * Q_SCALE
        qs_ref[...] = qsum_acc[...]

    qblk, full, colspec = _head_specs(s, tq)
    frow = pl.BlockSpec((None, nk, 1, tk), lambda hh, i: (hh, 0, 0, 0))
    big = pltpu.VMEM((tq, tk), F32)
    return pl.pallas_call(
        body, name="fox_bwd", grid=(h, s // tq),
        in_specs=[qblk, full, full, colspec, frow, qblk, colspec, qblk],
        out_specs=[qblk, full, full, pl.BlockSpec((None, nk, 8, tk), lambda hh, i: (hh, 0, 0, 0)), colspec],
        out_shape=[jax.ShapeDtypeStruct((h, s, HEAD_DIM), F32)] * 3
        + [jax.ShapeDtypeStruct((h, nk, 8, tk), F32), jax.ShapeDtypeStruct((h, s, 1), F32)],
        scratch_shapes=[pltpu.VMEM((tq, HEAD_DIM), F32), pltpu.VMEM((tq, 1), F32), big, big, big, big],
        compiler_params=_cparams(("parallel", "arbitrary")),
    )(q, k, v, f_col, f_row, o, lse, d_o)


SB_TERMS = 2
ROW_CHUNK = 256
LOG2E = 1.4426950408889634
LN2 = 0.6931471805599453


def _softplus2(z2):
    return jnp.maximum(z2, 0.0) + jnp.log2(1.0 + jnp.exp2(-jnp.abs(z2)))


def _band_mask(rows, cols, row0, col0, strict=True):
    r = row0 + lax.broadcasted_iota(jnp.int32, (rows, cols), 0)
    c = col0 + lax.broadcasted_iota(jnp.int32, (rows, cols), 1)
    return c < r if strict else c <= r


def _sb_fwd(q, k, v, tq, tk):
    h, s, _ = q.shape

    rc = min(ROW_CHUNK, tq)
    assert tq % tk == 0 and tq % rc == 0

    assert (tq // tk) % 2 == 0

    def body(q_ref, k_ref, v_ref, o_ref, tot_ref, acc_ref, run_ref, z0, z1, d0, d1, t0, t1):
        z_refs, d_refs, t_refs = (z0, z1), (d0, d1), (t0, t1)
        i = pl.program_id(1)
        qs = _scaled(q_ref)
        tri = _tri(tk, "ge")
        acc_ref[...] = jnp.zeros_like(acc_ref)
        run_ref[...] = jnp.zeros_like(run_ref)
        n_full = (i * tq) // tk

        def block(j, rel):
            at = pl.ds(pl.multiple_of(j * tk, tk), tk)
            ks = k_ref[at, :]
            vs = v_ref[at, :]
            run = run_ref[...]
            live = [r for r in range(tq // rc) if rel is None or rel * tk < (r + 1) * rc - 1]
            z2, inc, mask, outs = {}, {}, {}, {}

            def logits(r):
                rows = slice(r * rc, (r + 1) * rc)
                z2[r] = lax.dot_general(qs[rows], ks, _NT, preferred_element_type=F32) * LOG2E

            def sums(r):
                sp = _softplus2(z2[r])
                visible = rel is None or rel * tk + tk - 1 < r * rc
                mask[r] = None if visible else _band_mask(rc, tk, r * rc, rel * tk)
                if mask[r] is not None:
                    sp = jnp.where(mask[r], sp, 0.0)
                inc[r] = _split_dot(sp, tri, SB_TERMS)

            def weigh(r):
                rows = slice(r * rc, (r + 1) * rc)
                w = jnp.exp2(z2[r] - (run[rows] + inc[r]))
                if mask[r] is not None:
                    w = jnp.where(mask[r], w, 0.0)
                outs[r] = jnp.dot(w.astype(BF16), vs, preferred_element_type=F32)

            for t in range(len(live) + 2):
                if t < len(live):
                    logits(live[t])
                if 0 <= t - 1 < len(live):
                    sums(live[t - 1])
                if 0 <= t - 2 < len(live):
                    weigh(live[t - 2])
            for r in live:
                rows = slice(r * rc, (r + 1) * rc)
                acc_ref[rows, :] += outs[r]
                run_ref[rows, :] += inc[r][:, 0:1]

        for rel in reversed(range(tq // tk)):
            block(n_full + rel, rel)

        def keys_of(b):
            j = n_full - 1 - jnp.minimum(b, n_full - 1)
            return pl.ds(pl.multiple_of(j * tk, tk), tk)

        def logits(b, slot):
            z_refs[slot][...] = lax.dot_general(qs, k_ref[keys_of(b), :], _NT,
                                                preferred_element_type=F32) * LOG2E

        def sums(slot):
            z2 = z_refs[slot][...]
            inc = _split_dot(_softplus2(z2), tri, SB_TERMS)
            d_refs[slot][...] = z2 - inc
            t_refs[slot][...] = inc[:, 0:1]

        def weigh(b, slot):
            w = jnp.exp2(d_refs[slot][...] - run_ref[...])
            acc_ref[...] += jnp.dot(w.astype(BF16), v_ref[keys_of(b), :], preferred_element_type=F32)
            run_ref[...] += t_refs[slot][...]

        @pl.when(n_full > 0)
        def _():
            logits(0, 0)
            logits(1, 1)
            sums(0)

            def step(p, carry):
                b = 2 * p
                logits(b + 2, 0)
                sums(1)
                weigh(b, 0)
                logits(b + 3, 1)
                sums(0)
                weigh(b + 1, 1)
                return carry

            lax.fori_loop(0, n_full // 2, step, 0)

        o_ref[...] = acc_ref[...]
        tot_ref[...] = run_ref[...] * (-LN2)

    qblk, full, colspec = _head_specs(s, tq)
    return pl.pallas_call(
        body, name="sb_fwd", grid=(h, s // tq),
        in_specs=[qblk, full, full], out_specs=[qblk, colspec],
        out_shape=[jax.ShapeDtypeStruct((h, s, HEAD_DIM), F32), jax.ShapeDtypeStruct((h, s, 1), F32)],
        scratch_shapes=[pltpu.VMEM((tq, HEAD_DIM), F32), pltpu.VMEM((tq, 1), F32),
                        pltpu.VMEM((tq, tk), F32), pltpu.VMEM((tq, tk), F32),
                        pltpu.VMEM((tq, tk), F32), pltpu.VMEM((tq, tk), F32),
                        pltpu.VMEM((tq, 1), F32), pltpu.VMEM((tq, 1), F32)],
        compiler_params=_cparams(("parallel", "parallel")),
    )(q, k, v)


def _sb_bwd(q, k, v, tot, d_o, tq, tk):
    h, s, _ = q.shape

    assert tq % tk == 0 and (tq // tk) % 2 == 0

    def body(q_ref, k_ref, v_ref, tot_ref, do_ref, dq_ref, dk_ref, dv_ref, dq_acc, off_ref, grun_ref,
             z0, z1, p0, p1, u0, u1, b0, b1, t0, t1):
        z_refs, p_refs, u_refs, b_refs, t_refs = (z0, z1), (p0, p1), (u0, u1), (b0, b1), (t0, t1)
        i = pl.program_id(1)

        @pl.when(i == 0)
        def _():
            dk_ref[...] = jnp.zeros_like(dk_ref)
            dv_ref[...] = jnp.zeros_like(dv_ref)

        qs = _scaled(q_ref)
        dob = do_ref[...].astype(BF16)
        tri = _tri(tk, "le")
        dq_acc[...] = jnp.zeros_like(dq_acc)
        off_ref[...] = tot_ref[...] * LOG2E
        grun_ref[...] = jnp.zeros_like(grun_ref)
        n_full = (i * tq) // tk

        def keys_of(b):
            return pl.ds(pl.multiple_of(jnp.minimum(b, n_full - 1) * tk, tk), tk)

        def finish(at, w, g, ginc, beta, t_blk, mask):
            dz = g - beta * (grun_ref[...] + ginc)
            if mask is not None:
                dz = jnp.where(mask, dz, 0.0)
            dzb = dz.astype(BF16)
            dq_acc[...] += jnp.dot(dzb, k_ref[at, :], preferred_element_type=F32)
            dk_ref[at, :] += lax.dot_general(dzb, qs, _TN, preferred_element_type=F32)
            dv_ref[at, :] += lax.dot_general(w.astype(BF16), dob, _TN, preferred_element_type=F32)
            off_ref[...] += t_blk
            grun_ref[...] += ginc[:, tk - 1:tk]

        def logits(b, slot):
            z_refs[slot][...] = lax.dot_general(qs, k_ref[keys_of(b), :], _NT,
                                                preferred_element_type=F32) * LOG2E

        def sums(b, slot):
            z2 = z_refs[slot][...]
            sp = _softplus2(z2)
            lb2 = z2 - sp
            linc = _split_dot(sp, tri, SB_TERMS)
            p_refs[slot][...] = lax.dot_general(dob, v_ref[keys_of(b), :], _NT, preferred_element_type=F32)
            u_refs[slot][...] = lb2 + linc
            b_refs[slot][...] = jnp.exp2(lb2)
            t_refs[slot][...] = linc[:, tk - 1:tk]

        def weigh(slot):
            w = jnp.exp2(u_refs[slot][...] + off_ref[...])
            g = w * p_refs[slot][...]
            return w, g, _split_dot(g, tri, SB_TERMS)

        @pl.when(n_full > 0)
        def _():
            logits(0, 0)
            logits(1, 1)
            sums(0, 0)

            def step(p, carry):
                for slot in (0, 1):
                    b = 2 * p + slot
                    w, g, ginc = weigh(slot)
                    logits(b + 2, slot)
                    sums(b + 1, 1 - slot)
                    finish(keys_of(b), w, g, ginc, b_refs[slot][...], t_refs[slot][...], None)
                return carry

            lax.fori_loop(0, n_full // 2, step, 0)

        for rel in range(tq // tk):
            at = pl.ds(pl.multiple_of((n_full + rel) * tk, tk), tk)
            mask = _band_mask(tq, tk, 0, rel * tk)
            z2 = lax.dot_general(qs, k_ref[at, :], _NT, preferred_element_type=F32) * LOG2E
            sp = _softplus2(z2)
            linc = _split_dot(jnp.where(mask, sp, 0.0), tri, SB_TERMS)
            w = jnp.where(mask, jnp.exp2(z2 - sp + linc + off_ref[...]), 0.0)
            g = w * lax.dot_general(dob, v_ref[at, :], _NT, preferred_element_type=F32)
            ginc = _split_dot(g, tri, SB_TERMS)
            finish(at, w, g, ginc, jnp.exp2(z2 - sp), linc[:, tk - 1:tk], mask)

        dq_ref[...] = dq_acc[...] * Q_SCALE

    qblk, full, colspec = _head_specs(s, tq)
    big = pltpu.VMEM((tq, tk), F32)
    return pl.pallas_call(
        body, name="sb_bwd", grid=(h, s // tq),
        in_specs=[qblk, full, full, colspec, qblk], out_specs=[qblk, full, full],
        out_shape=[jax.ShapeDtypeStruct((h, s, HEAD_DIM), F32)] * 3,
        scratch_shapes=[pltpu.VMEM((tq, HEAD_DIM), F32), pltpu.VMEM((tq, 1), F32), pltpu.VMEM((tq, 1), F32)]
        + [big] * 8 + [pltpu.VMEM((tq, 1), F32)] * 2,
        compiler_params=_cparams(("parallel", "arbitrary")),
    )(q, k, v, tot, d_o)


def _sum_adamw(parts, w, m, v, name, tr=512):
    _, rows, lanes = parts.shape
    tr = _tile(rows, tr, 16)
    c_m = 1.0 - ADAM_B1 ** ADAM_STEP
    c_v = 1.0 - ADAM_B2 ** ADAM_STEP

    def body(p_ref, w_ref, m_ref, v_ref, g_ref, d_ref, nm_ref, nv_ref):
        g = p_ref[0].astype(F32)
        for j in range(1, N_DEV):
            g = g + p_ref[j].astype(F32)
        nm = ADAM_B1 * m_ref[...] + (1.0 - ADAM_B1) * g
        nv = ADAM_B2 * v_ref[...] + (1.0 - ADAM_B2) * (g * g)
        m_hat = nm / c_m
        v_hat = nv / c_v
        g_ref[...] = g
        d_ref[...] = -ADAM_LR * (m_hat / (jnp.sqrt(v_hat) + ADAM_EPS) + ADAM_WD * w_ref[...])
        nm_ref[...] = nm
        nv_ref[...] = nv

    blk = pl.BlockSpec((tr, lanes), lambda i: (i, 0))
    return pl.pallas_call(
        body, name=name, grid=(rows // tr,),
        in_specs=[pl.BlockSpec((N_DEV, tr, lanes), lambda i: (0, i, 0)), blk, blk, blk],
        out_specs=[blk] * 4, out_shape=[jax.ShapeDtypeStruct((rows, lanes), F32)] * 4,
        compiler_params=_cparams(("parallel",)),
    )(parts, w, m, v)


def _to_heads(t):
    s = t.shape[0]
    return t.reshape(s, N_GROUP_HEADS, HEAD_DIM).transpose(1, 0, 2)


def _from_heads(t):
    s = t.shape[1]
    return t.transpose(1, 0, 2).reshape(s, GROUP_W)


def kernel(x, attn_norm_g, w_in, forget_bias, fox_out_g, sb_out_g, w_out, ffn_norm_g, w_up, conv_w, conv_b, w_down, final_norm_g, loss_target, m_attn_norm_g, m_w_in, m_forget_bias, m_fox_out_g, m_sb_out_g, m_w_out, m_ffn_norm_g, m_w_up, m_conv_w, m_conv_b, m_w_down, m_final_norm_g, v_attn_norm_g, v_w_in, v_forget_bias, v_fox_out_g, v_sb_out_g, v_w_out, v_ffn_norm_g, v_w_up, v_conv_w, v_conv_b, v_w_down, v_final_norm_g):
    s = x.shape[1]
    xs = x[0]
    tq = min(ATTN_TQ, s)
    tk_fox = min(FOX_TK, s)
    tk_sb = min(SB_TK, s)
    in_shard, up_shard, out_shard, down_shard = IN_COLS // N_DEV, 2 * D_FF // N_DEV, D_MODEL // N_DEV, D_FF // N_DEV

    cw = conv_w[0]
    cw_hi = cw.astype(BF16)
    cw_lo = (cw - cw_hi.astype(F32)).astype(BF16)
    wpack = _pack([w_in[0], w_out[0], w_up[0], w_down[0], cw_hi, cw_lo], PACK_ROWS, BF16)
    gathered = _all_gather(wpack)
    g_in, g_out, g_up, g_down, g_chi, g_clo = _unpack(
        gathered, [(D_MODEL, in_shard), (out_shard, D_MODEL), (D_MODEL, up_shard), (down_shard, D_MODEL),
                   (3, up_shard), (3, up_shard)], lead=(N_DEV,))
    w_in_f = g_in.transpose(1, 0, 2).reshape(D_MODEL, IN_COLS)
    n_gate = QKV_W + N_GROUP_HEADS
    w_in_p = jnp.concatenate([w_in_f[:, :n_gate], jnp.zeros((D_MODEL, GATE_PAD - N_GROUP_HEADS), BF16),
                              w_in_f[:, n_gate:]], axis=1)
    w_out_f = g_out.reshape(D_MODEL, D_MODEL)
    w_up_f = g_up.transpose(1, 0, 2).reshape(D_MODEL, 2 * D_FF)
    w_down_f = g_down.reshape(D_FF, D_MODEL)
    conv_w_f = (g_chi.astype(F32) + g_clo.astype(F32)).transpose(1, 0, 2).reshape(3, 2 * D_FF)
    conv_w2 = conv_w_f.reshape(3, 2, D_FF).transpose(1, 0, 2)
    conv_b2 = conv_b.reshape(2, 1, D_FF)

    h1 = _rms_fwd(xs, attn_norm_g)
    proj = _mm_nn(h1, w_in_p, F32, "in_proj", tn=640)
    fox_qkv = proj[:, :QKV_W].astype(BF16)
    f_logit = proj[:, QKV_W:QKV_W + N_GROUP_HEADS]
    sb_qkv = proj[:, QKV_W + GATE_PAD:].astype(BF16)
    fq, fk, fv = (_to_heads(fox_qkv[:, j * GROUP_W:(j + 1) * GROUP_W]) for j in range(3))
    sq, sk, sv = (_to_heads(sb_qkv[:, j * GROUP_W:(j + 1) * GROUP_W]) for j in range(3))

    f_logit_h = f_logit.T.reshape(N_GROUP_HEADS, s // LANES, LANES)
    bias_h = jnp.broadcast_to(forget_bias.reshape(N_GROUP_HEADS, 1, 1), (N_GROUP_HEADS, 1, LANES))
    big_f = _forget_fwd(f_logit_h, bias_h)
    f_col = big_f.reshape(N_GROUP_HEADS, s, 1)
    f_row = big_f.reshape(N_GROUP_HEADS, s // tk_fox, 1, tk_fox)

    fv_ones = jnp.concatenate([fv, jnp.ones_like(fv)], axis=-1)
    o_fox_h, lse = _fox_fwd(fq, fk, fv_ones, f_col, f_row, tq, tk_fox)
    o_sb_h, sb_tot = _sb_fwd(sq, sk, sv, tq, tk_sb)
    o_fox = _from_heads(o_fox_h)
    o_sb = _from_heads(o_sb_h)
    o_n = _group_rms_fwd(o_fox, o_sb, fox_out_g, sb_out_g)
    x1 = _mm_nn(o_n, w_out_f, F32, "out_proj", resid=xs)
    h2 = _rms_fwd(x1, ffn_norm_g)
    up = _mm_up(h2, w_up_f)
    act = _conv_gate_fwd(up, conv_w2, conv_b2)
    x2 = _mm_nn(act, w_down_f, F32, "down_proj", resid=x1, tk=1408)

    d_x2, d_x2b, dg_final, loss_part = _loss_head(x2, loss_target[0], final_norm_g.reshape(1, D_MODEL))
    d_act = _mm_nt(d_x2b, w_down_f, BF16, "d_act", tn=1408)
    dw_down = _mm_tn(act, d_x2b, "d_w_down", tm=1408)
    d_up, dcw2, dcb2 = _conv_gate_bwd(up, d_act, conv_w2, conv_b2)
    d_h2 = _mm_dup_nt(d_up, w_up_f)
    dw_up = _mm_dwup_tn(h2, d_up)
    d_x1, d_x1b, dg_ffn = _rms_bwd(x1, d_h2, ffn_norm_g, d_x2, dy_col=0, name="ffn_norm_bwd", want_bf16=True)
    d_on = _mm_nt(d_x1b, w_out_f, F32, "d_o_normed")
    dw_out = _mm_tn(o_n, d_x1b, "d_w_out")
    d_o_fox, dg_fox = _rms_bwd(o_fox, d_on, fox_out_g, None, dy_col=0, name="fox_norm_bwd", want_bf16=False)
    d_o_sb, dg_sb = _rms_bwd(o_sb, d_on, sb_out_g, None, dy_col=1, name="sb_norm_bwd", want_bf16=False)

    dfq, dfk, dfv, ksum8, qsum = _fox_bwd(fq, fk, fv, f_col, f_row, o_fox_h, lse, _to_heads(d_o_fox), tq, tk_fox)
    dsq, dsk, dsv = _sb_bwd(sq, sk, sv, sb_tot, _to_heads(d_o_sb), tq, tk_sb)
    ksum = jnp.sum(ksum8, axis=2).reshape(N_GROUP_HEADS, s // LANES, LANES)
    d_f_logit_h, d_bias_h = _forget_bwd(f_logit_h, bias_h, ksum,
                                        qsum.reshape(N_GROUP_HEADS, s // LANES, LANES))
    d_f_logit = d_f_logit_h.reshape(N_GROUP_HEADS, s).T

    d_proj = jnp.concatenate(
        [_from_heads(t).astype(BF16) for t in (dfq, dfk, dfv)]
        + [d_f_logit.astype(BF16), jnp.zeros((s, GATE_PAD - N_GROUP_HEADS), BF16)]
        + [_from_heads(t).astype(BF16) for t in (dsq, dsk, dsv)], axis=1)
    d_h1 = _mm_nt(d_proj, w_in_p, F32, "d_h1", tk=640)
    dw_in_p = _mm_tn(h1, d_proj, "d_w_in", tn=640)
    dw_in = jnp.concatenate([dw_in_p[:, :n_gate], dw_in_p[:, QKV_W + GATE_PAD:]], axis=1)
    grad_x, dg_attn = _rms_bwd(xs, d_h1, attn_norm_g, d_x1, dy_col=0, name="attn_norm_bwd", want_bf16=False)

    dconv_w = dcw2.transpose(1, 0, 2).reshape(3, 2 * D_FF)
    dconv_b = dcb2.reshape(1, 2 * D_FF)
    gpack = jnp.stack([
        _pack([dw_in[:, j * in_shard:(j + 1) * in_shard], dw_out[j * out_shard:(j + 1) * out_shard],
               dw_up[:, j * up_shard:(j + 1) * up_shard], dw_down[j * down_shard:(j + 1) * down_shard],
               dconv_w[:, j * up_shard:(j + 1) * up_shard]], PACK_ROWS, BF16)
        for j in range(N_DEV)])
    small_shapes = [(1, D_MODEL), (1, N_GROUP_HEADS), (1, GROUP_W), (1, GROUP_W), (1, D_MODEL),
                    (1, 2 * D_FF), (D_MODEL,), (1,)]
    spack = _pack([dg_attn, d_bias_h[:, 0, 0], dg_fox, dg_sb, dg_ffn, dconv_b, dg_final, loss_part[0, 0:1]],
                  SMALL_ROWS, F32)
    grecv, srecv = _grad_exchange(gpack, spack)

    shard_shapes = [(D_MODEL, in_shard), (out_shard, D_MODEL), (D_MODEL, up_shard), (down_shard, D_MODEL),
                    (3, up_shard)]

    def shard_pack(a_in, a_out, a_up, a_down, a_conv):
        return _pack([a_in[0], a_out[0], a_up[0], a_down[0], a_conv[0]], PACK_ROWS, F32)

    big = _sum_adamw(grecv, shard_pack(w_in, w_out, w_up, w_down, conv_w),
                     shard_pack(m_w_in, m_w_out, m_w_up, m_w_down, m_conv_w),
                     shard_pack(v_w_in, v_w_out, v_w_up, v_w_down, v_conv_w), "adamw_sharded")

    def small_pack(a_attn, a_bias, a_fox, a_sb, a_ffn, a_cb, a_fin):
        return _pack([a_attn, a_bias, a_fox, a_sb, a_ffn, a_cb, a_fin, jnp.zeros((1,), F32)], SMALL_ROWS, F32)

    small = _sum_adamw(srecv, small_pack(attn_norm_g, forget_bias, fox_out_g, sb_out_g, ffn_norm_g, conv_b, final_norm_g),
                       small_pack(m_attn_norm_g, m_forget_bias, m_fox_out_g, m_sb_out_g, m_ffn_norm_g, m_conv_b, m_final_norm_g),
                       small_pack(v_attn_norm_g, v_forget_bias, v_fox_out_g, v_sb_out_g, v_ffn_norm_g, v_conv_b, v_final_norm_g),
                       "adamw_replicated", tr=SMALL_ROWS)

    outs = []
    loss = None
    for kind in range(4):
        b_in, b_out, b_up, b_down, b_conv = _unpack(big[kind], shard_shapes)
        s_attn, s_bias, s_fox, s_sb, s_ffn, s_cb, s_fin, s_loss = _unpack(small[kind], small_shapes)
        if kind == 0:
            loss = s_loss[0]
        outs += [s_attn, b_in[None], s_bias, s_fox, s_sb, b_out[None], s_ffn, b_up[None], b_conv[None], s_cb,
                 b_down[None], s_fin]
    return (loss, grad_x[None], *outs)
```

```python
import jax
import jax.numpy as jnp
from jax import lax
from jax.experimental import pallas as pl
from jax.experimental.pallas import tpu as pltpu

F32 = jnp.float32
BF16 = jnp.bfloat16

D_MODEL = 1024
HEAD_DIM = 64
N_GROUP_HEADS = 8
GROUP_W = N_GROUP_HEADS * HEAD_DIM
QKV_W = 3 * GROUP_W
IN_COLS = 2 * QKV_W + N_GROUP_HEADS
GATE_PAD = 128
IN_COLS_PAD = 2 * QKV_W + GATE_PAD
D_FF = 2816
N_DEV = 8
EPS = 1e-6
Q_SCALE = HEAD_DIM ** -0.5

ADAM_LR = 0.001
ADAM_B1 = 0.9
ADAM_B2 = 0.999
ADAM_EPS = 1e-08
ADAM_WD = 0.01
ADAM_STEP = 10

LANES = 128
PACK_ROWS = 12800
SMALL_ROWS = 80
VMEM_LIMIT = 56 * 1024 * 1024
NEG_BIG = -1e30
ATTN_TQ = 512
FOX_TK = 512
SB_TK = 256
MESH = pl.DeviceIdType.MESH


def _cparams(sem=None, **kw):
    return pltpu.CompilerParams(dimension_semantics=sem, vmem_limit_bytes=VMEM_LIMIT, **kw)


def _tile(n, target, mult=LANES):
    if n <= target:
        return n
    t = (target // mult) * mult
    while t >= mult:
        if n % t == 0:
            return t
        t -= mult
    return n


def _seg_len(shape):
    n = 1
    for s in shape:
        n *= s
    return -(-n // LANES) * LANES


def _pack(arrs, rows, dtype):
    parts = []
    for a in arrs:
        f = a.reshape(-1).astype(dtype)
        parts.append(jnp.pad(f, (0, _seg_len(a.shape) - f.shape[0])))
    flat = jnp.concatenate(parts)
    flat = jnp.pad(flat, (0, rows * LANES - flat.shape[0]))
    return flat.reshape(rows, LANES)


def _unpack(p, shapes, lead=()):
    flat = p.reshape(lead + (-1,))
    out, off = [], 0
    for shp in shapes:
        n = 1
        for s in shp:
            n *= s
        out.append(flat[..., off:off + n].reshape(lead + tuple(shp)))
        off += _seg_len(shp)
    return out


def _my_pos():
    return lax.axis_index("x"), lax.axis_index("y"), lax.axis_index("c")


def _all_gather(block):
    rows, lanes = block.shape

    def body(x_ref, out_ref, send_sems, recv_sems, local_sem):
        x, y, c = _my_pos()
        me, sibling = (x, y, c), (x, y, 1 - c)
        chips = [(1 - x, y), (x, 1 - y), (1 - x, 1 - y)]

        def slot(px, py, pc):
            return out_ref.at[4 * px + 2 * py + pc]

        def copy(k, blk, to, src=None):
            return pltpu.make_async_remote_copy(
                src_ref=slot(*blk) if src is None else src, dst_ref=slot(*blk),
                send_sem=send_sems.at[k], recv_sem=recv_sems.at[k],
                device_id=to, device_id_type=MESH)

        mine = pltpu.make_async_copy(x_ref, slot(*me), local_sem)
        mine.start()
        first = [copy(0, me, sibling, src=x_ref)]
        first += [copy(1 + j, me, (*chip, c), src=x_ref) for j, chip in enumerate(chips)]
        for cp in first:
            cp.start()
        passed = [copy(4 + j, (*chip, c), sibling) for j, chip in enumerate(chips)]
        for j, chip in enumerate(chips):
            copy(1 + j, (*chip, c), me).wait_recv()
            passed[j].start()
        copy(0, sibling, me).wait_recv()
        for j, chip in enumerate(chips):
            copy(4 + j, (*chip, 1 - c), me).wait_recv()
        for cp in first + passed:
            cp.wait_send()
        mine.wait()

    return pl.pallas_call(
        body, name="weights_all_gather",
        out_shape=jax.ShapeDtypeStruct((N_DEV, rows, lanes), block.dtype),
        in_specs=[pl.BlockSpec(memory_space=pl.ANY)],
        out_specs=pl.BlockSpec(memory_space=pl.ANY),
        scratch_shapes=[pltpu.SemaphoreType.DMA((7,)), pltpu.SemaphoreType.DMA((7,)),
                        pltpu.SemaphoreType.DMA],
    )(block)


def _grad_exchange(gpack, spack):
    _, rows, lanes = gpack.shape
    srows = spack.shape[0]

    def body(g_ref, s_ref, grecv_ref, srecv_ref, send_sems, recv_sems, local_sems):
        x, y, c = _my_pos()
        my_id = 4 * x + 2 * y + c
        own_g = pltpu.make_async_copy(g_ref.at[my_id], grecv_ref.at[my_id], local_sems.at[0])
        own_s = pltpu.make_async_copy(s_ref, srecv_ref.at[my_id], local_sems.at[1])
        own_g.start()
        own_s.start()
        sends, arrivals = [], []
        for k in range(1, N_DEV):
            px, py, pc = x ^ (k >> 2), y ^ ((k >> 1) & 1), c ^ (k & 1)
            peer_id = 4 * px + 2 * py + pc
            for a, (src, dst) in enumerate(((g_ref.at[peer_id], grecv_ref), (s_ref, srecv_ref))):
                sends.append(pltpu.make_async_remote_copy(
                    src_ref=src, dst_ref=dst.at[my_id],
                    send_sem=send_sems.at[a, k - 1], recv_sem=recv_sems.at[a, k - 1],
                    device_id=(px, py, pc), device_id_type=MESH))
                arrivals.append(pltpu.make_async_remote_copy(
                    src_ref=src, dst_ref=dst.at[peer_id],
                    send_sem=send_sems.at[a, k - 1], recv_sem=recv_sems.at[a, k - 1],
                    device_id=(px, py, pc), device_id_type=MESH))
        for cp in sends:
            cp.start()
        for cp in arrivals:
            cp.wait_recv()
        for cp in sends:
            cp.wait_send()
        own_g.wait()
        own_s.wait()

    return pl.pallas_call(
        body, name="grad_exchange",
        out_shape=(jax.ShapeDtypeStruct((N_DEV, rows, lanes), gpack.dtype),
                   jax.ShapeDtypeStruct((N_DEV, srows, lanes), spack.dtype)),
        in_specs=[pl.BlockSpec(memory_space=pl.ANY), pl.BlockSpec(memory_space=pl.ANY)],
        out_specs=(pl.BlockSpec(memory_space=pl.ANY), pl.BlockSpec(memory_space=pl.ANY)),
        scratch_shapes=[pltpu.SemaphoreType.DMA((2, 7)), pltpu.SemaphoreType.DMA((2, 7)),
                        pltpu.SemaphoreType.DMA((2,))],
    )(gpack, spack)


_DIMS = {"nn": (((1,), (0,)), ((), ())), "nt": (((1,), (1,)), ((), ())), "tn": (((0,), (0,)), ((), ()))}


def _matmul(a, b, *, mode, grid, a_block, a_map, b_block, b_map, o_block, o_map, out_shape, name,
            resid=None):
    nk = grid[2]
    dims = _DIMS[mode]

    def body(*refs):
        if resid is None:
            a_ref, b_ref, o_ref, acc_ref = refs
            r_ref = None
        else:
            a_ref, b_ref, r_ref, o_ref, acc_ref = refs
        k = pl.program_id(2)

        @pl.when(k == 0)
        def _():
            acc_ref[...] = jnp.zeros_like(acc_ref)

        acc_ref[...] += lax.dot_general(a_ref[...], b_ref[...], dims, preferred_element_type=F32)

        @pl.when(k == nk - 1)
        def _():
            res = acc_ref[...]
            if r_ref is not None:
                res = r_ref[...] + res
            o_ref[...] = res.astype(o_ref.dtype)

    in_specs = [pl.BlockSpec(a_block, a_map), pl.BlockSpec(b_block, b_map)]
    args = [a, b]
    if resid is not None:
        in_specs.append(pl.BlockSpec(o_block, o_map))
        args.append(resid)
    acc_shape = tuple(d for d in o_block if d is not None)
    return pl.pallas_call(
        body, name=name, grid=grid, in_specs=in_specs,
        out_specs=pl.BlockSpec(o_block, o_map), out_shape=out_shape,
        scratch_shapes=[pltpu.VMEM(acc_shape, F32)],
        compiler_params=_cparams(("parallel", "parallel", "arbitrary")),
    )(*args)


def _mm_nn(a, b, out_dtype, name, resid=None, tm=1024, tn=1024, tk=1024):
    m, kk = a.shape
    n = b.shape[1]
    tm, tn, tk = _tile(m, tm, 8), _tile(n, tn), _tile(kk, tk)
    return _matmul(a, b, mode="nn", grid=(m // tm, n // tn, kk // tk),
                   a_block=(tm, tk), a_map=lambda i, j, k: (i, k),
                   b_block=(tk, tn), b_map=lambda i, j, k: (k, j),
                   o_block=(tm, tn), o_map=lambda i, j, k: (i, j),
                   out_shape=jax.ShapeDtypeStruct((m, n), out_dtype), name=name, resid=resid)


def _mm_nt(a, b, out_dtype, name, tm=1024, tn=1024, tk=1024):
    m, kk = a.shape
    n = b.shape[0]
    tm, tn, tk = _tile(m, tm, 8), _tile(n, tn), _tile(kk, tk)
    return _matmul(a, b, mode="nt", grid=(m // tm, n // tn, kk // tk),
                   a_block=(tm, tk), a_map=lambda i, j, k: (i, k),
                   b_block=(tn, tk), b_map=lambda i, j, k: (j, k),
                   o_block=(tm, tn), o_map=lambda i, j, k: (i, j),
                   out_shape=jax.ShapeDtypeStruct((m, n), out_dtype), name=name)


def _mm_tn(a, b, name, tm=1024, tn=1024, tk=1024):
    kk, m = a.shape
    n = b.shape[1]
    tm, tn, tk = _tile(m, tm), _tile(n, tn), _tile(kk, tk, 8)
    return _matmul(a, b, mode="tn", grid=(m // tm, n // tn, kk // tk),
                   a_block=(tk, tm), a_map=lambda i, j, k: (k, i),
                   b_block=(tk, tn), b_map=lambda i, j, k: (k, j),
                   o_block=(tm, tn), o_map=lambda i, j, k: (i, j),
                   out_shape=jax.ShapeDtypeStruct((m, n), F32), name=name)


def _mm_up(h, w_up, tm=2048, tn=256):
    s = h.shape[0]
    tm = _tile(s, tm, 8)
    nh = D_FF // tn
    return _matmul(h, w_up, mode="nn", grid=(s // tm, 2 * nh, 1),
                   a_block=(tm, D_MODEL), a_map=lambda i, j, k: (i, 0),
                   b_block=(D_MODEL, tn), b_map=lambda i, j, k: (0, j),
                   o_block=(None, tm, tn), o_map=lambda i, j, k: (j // nh, i, j % nh),
                   out_shape=jax.ShapeDtypeStruct((2, s, D_FF), F32), name="up_proj")


def _mm_dup_nt(dup, w_up, tm=1024, tk=1408):
    s = dup.shape[1]
    tm = _tile(s, tm, 8)
    nh = D_FF // tk
    return _matmul(dup, w_up, mode="nt", grid=(s // tm, 1, 2 * nh),
                   a_block=(None, tm, tk), a_map=lambda i, j, k: (k // nh, i, k % nh),
                   b_block=(D_MODEL, tk), b_map=lambda i, j, k: (0, k),
                   o_block=(tm, D_MODEL), o_map=lambda i, j, k: (i, 0),
                   out_shape=jax.ShapeDtypeStruct((s, D_MODEL), F32), name="d_h2")


def _mm_dwup_tn(h, dup, tn=1408, tk=1024):
    s = h.shape[0]
    tk = _tile(s, tk, 8)
    nh = D_FF // tn
    return _matmul(h, dup, mode="tn", grid=(1, 2 * nh, s // tk),
                   a_block=(tk, D_MODEL), a_map=lambda i, j, k: (k, 0),
                   b_block=(None, tk, tn), b_map=lambda i, j, k: (j // nh, k, j % nh),
                   o_block=(D_MODEL, tn), o_map=lambda i, j, k: (0, j),
                   out_shape=jax.ShapeDtypeStruct((D_MODEL, 2 * D_FF), F32), name="d_w_up")


def _rms_fwd(x, g, tr=256):
    s, d = x.shape
    tr = _tile(s, tr, 8)

    def body(x_ref, g_ref, o_ref):
        xv = x_ref[...]
        r = lax.rsqrt(jnp.mean(xv * xv, axis=-1, keepdims=True) + EPS)
        o_ref[...] = (xv * r * g_ref[...]).astype(o_ref.dtype)

    return pl.pallas_call(
        body, name="rms_fwd", grid=(s // tr,),
        in_specs=[pl.BlockSpec((tr, d), lambda i: (i, 0)), pl.BlockSpec((1, d), lambda i: (0, 0))],
        out_specs=pl.BlockSpec((tr, d), lambda i: (i, 0)),
        out_shape=jax.ShapeDtypeStruct((s, d), BF16),
        compiler_params=_cparams(("parallel",)),
    )(x, g)


def _group_rms_fwd(o_fox, o_sb, g_fox, g_sb, tr=256):
    s, d = o_fox.shape
    tr = _tile(s, tr, 8)

    def body(a_ref, b_ref, ga_ref, gb_ref, o_ref):
        for src, g_ref, lo in ((a_ref, ga_ref, 0), (b_ref, gb_ref, d)):
            xv = src[...]
            r = lax.rsqrt(jnp.mean(xv * xv, axis=-1, keepdims=True) + EPS)
            o_ref[:, lo:lo + d] = (xv * r * g_ref[...]).astype(o_ref.dtype)

    row = pl.BlockSpec((tr, d), lambda i: (i, 0))
    gain = pl.BlockSpec((1, d), lambda i: (0, 0))
    return pl.pallas_call(
        body, name="group_rms_fwd", grid=(s // tr,),
        in_specs=[row, row, gain, gain],
        out_specs=pl.BlockSpec((tr, 2 * d), lambda i: (i, 0)),
        out_shape=jax.ShapeDtypeStruct((s, 2 * d), BF16),
        compiler_params=_cparams(("parallel",)),
    )(o_fox, o_sb, g_fox, g_sb)


def _rms_bwd(x, dy, g, resid, *, dy_col, name, want_bf16, tr=256):
    s, d = x.shape
    tr = _tile(s, tr, 8)
    has_resid = resid is not None

    def body(*refs):
        refs = list(refs)
        x_ref, dy_ref, g_ref = refs[:3]
        r_ref = refs[3] if has_resid else None
        outs = refs[4:] if has_resid else refs[3:]
        dx_ref = outs[0]
        dxb_ref = outs[1] if want_bf16 else None
        dg_ref = outs[-1]

        @pl.when(pl.program_id(0) == 0)
        def _():
            dg_ref[...] = jnp.zeros_like(dg_ref)

        xv = x_ref[...]
        dyv = dy_ref[...]
        r = lax.rsqrt(jnp.mean(xv * xv, axis=-1, keepdims=True) + EPS)
        xh = xv * r
        gy = dyv * g_ref[...]
        dx = r * (gy - xh * jnp.mean(xh * gy, axis=-1, keepdims=True))
        if r_ref is not None:
            dx = r_ref[...] + dx
        dx_ref[...] = dx
        if dxb_ref is not None:
            dxb_ref[...] = dx.astype(BF16)
        dg_ref[...] += jnp.sum(dyv * xh, axis=0, keepdims=True)

    row = pl.BlockSpec((tr, d), lambda i: (i, 0))
    in_specs = [row, pl.BlockSpec((tr, d), lambda i: (i, dy_col)), pl.BlockSpec((1, d), lambda i: (0, 0))]
    args = [x, dy, g]
    if has_resid:
        in_specs.append(row)
        args.append(resid)
    out_specs = [row]
    out_shape = [jax.ShapeDtypeStruct((s, d), F32)]
    if want_bf16:
        out_specs.append(row)
        out_shape.append(jax.ShapeDtypeStruct((s, d), BF16))
    out_specs.append(pl.BlockSpec((1, d), lambda i: (0, 0)))
    out_shape.append(jax.ShapeDtypeStruct((1, d), F32))
    return pl.pallas_call(
        body, name=name, grid=(s // tr,), in_specs=in_specs, out_specs=out_specs, out_shape=out_shape,
        compiler_params=_cparams(("arbitrary",)),
    )(*args)


def _loss_head(x2, target, g, tr=256):
    s, d = x2.shape
    tr = _tile(s, tr, 8)

    def body(x_ref, t_ref, g_ref, dx_ref, dxb_ref, dg_ref, loss_ref):
        @pl.when(pl.program_id(0) == 0)
        def _():
            dg_ref[...] = jnp.zeros_like(dg_ref)
            loss_ref[...] = jnp.zeros_like(loss_ref)

        xv = x_ref[...]
        gv = g_ref[...]
        r = lax.rsqrt(jnp.mean(xv * xv, axis=-1, keepdims=True) + EPS)
        xh = xv * r
        err = xh * gv - t_ref[...]
        loss_ref[...] += jnp.sum(jnp.mean(err * err, axis=-1, keepdims=True), axis=0, keepdims=True) * 0.5
        dyv = err * (1.0 / d)
        gy = dyv * gv
        dx = r * (gy - xh * jnp.mean(xh * gy, axis=-1, keepdims=True))
        dx_ref[...] = dx
        dxb_ref[...] = dx.astype(BF16)
        dg_ref[...] += jnp.sum(dyv * xh, axis=0, keepdims=True)

    row = pl.BlockSpec((tr, d), lambda i: (i, 0))
    return pl.pallas_call(
        body, name="loss_head", grid=(s // tr,),
        in_specs=[row, row, pl.BlockSpec((1, d), lambda i: (0, 0))],
        out_specs=[row, row, pl.BlockSpec((1, d), lambda i: (0, 0)), pl.BlockSpec((1, LANES), lambda i: (0, 0))],
        out_shape=[jax.ShapeDtypeStruct((s, d), F32), jax.ShapeDtypeStruct((s, d), BF16),
                   jax.ShapeDtypeStruct((1, d), F32), jax.ShapeDtypeStruct((1, LANES), F32)],
        compiler_params=_cparams(("arbitrary",)),
    )(x2, target, g)


def _conv_taps(cur, prev8, w, b, first):
    prev8 = jnp.where(first, 0.0, prev8)
    ext = jnp.concatenate([prev8, cur], axis=0)
    x1 = pltpu.roll(ext, 1, 0)[8:]
    x2 = pltpu.roll(ext, 2, 0)[8:]
    u = b + w[0:1] * x2
    u = u + w[1:2] * x1
    u = u + w[2:3] * cur
    return u, x1, x2


def _conv_gate_fwd(up, conv_w, conv_b, tm=512, tn=256):
    s = up.shape[1]
    tm = _tile(s, tm, 8)
    nrb = s // tm
    rb8 = tm // 8

    def body(g_ref, v_ref, gp_ref, vp_ref, wg_ref, wv_ref, bg_ref, bv_ref, o_ref):
        first = pl.program_id(1) == 0
        ug, _, _ = _conv_taps(g_ref[...], gp_ref[...], wg_ref[...], bg_ref[...], first)
        uv, _, _ = _conv_taps(v_ref[...], vp_ref[...], wv_ref[...], bv_ref[...], first)
        sg = 1.0 / (1.0 + jnp.exp(-ug))
        o_ref[...] = (ug * sg * uv).astype(o_ref.dtype)

    def cur(h):
        return pl.BlockSpec((None, tm, tn), lambda j, i: (h, i, j))

    def prev(h):
        return pl.BlockSpec((None, 8, tn), lambda j, i: (h, jnp.maximum(i * rb8 - 1, 0), j))

    def par(h, r):
        return pl.BlockSpec((None, r, tn), lambda j, i: (h, 0, j))

    return pl.pallas_call(
        body, name="conv_gate_fwd", grid=(D_FF // tn, nrb),
        in_specs=[cur(0), cur(1), prev(0), prev(1), par(0, 3), par(1, 3), par(0, 1), par(1, 1)],
        out_specs=pl.BlockSpec((tm, tn), lambda j, i: (i, j)),
        out_shape=jax.ShapeDtypeStruct((s, D_FF), BF16),
        compiler_params=_cparams(("parallel", "parallel")),
    )(up, up, up, up, conv_w, conv_w, conv_b, conv_b)


def _conv_gate_bwd(up, dact, conv_w, conv_b, tm=512, tn=256):
    s = up.shape[1]
    tm = _tile(s, tm, 8)
    nrb = s // tm
    rb8 = tm // 8

    def body(g_ref, v_ref, gp_ref, vp_ref, da_ref, wg_ref, wv_ref, bg_ref, bv_ref,
             dup_ref, dcw_ref, dcb_ref, carry_ref):
        i = pl.program_id(1)
        first = i == nrb - 1

        @pl.when(i == 0)
        def _():
            carry_ref[...] = jnp.zeros_like(carry_ref)
            dcw_ref[...] = jnp.zeros_like(dcw_ref)
            dcb_ref[...] = jnp.zeros_like(dcb_ref)

        curs = (g_ref[...], v_ref[...])
        ws = (wg_ref[...], wv_ref[...])
        ug, g1, g2 = _conv_taps(curs[0], gp_ref[...], ws[0], bg_ref[...], first)
        uv, v1, v2 = _conv_taps(curs[1], vp_ref[...], ws[1], bv_ref[...], first)
        sg = 1.0 / (1.0 + jnp.exp(-ug))
        da = da_ref[...].astype(F32)
        d_v = da * (ug * sg)
        d_g = da * uv * (sg * (1.0 + ug * (1.0 - sg)))
        for h, (du, x0, x1, x2) in enumerate(((d_g, curs[0], g1, g2), (d_v, curs[1], v1, v2))):
            dcb_ref[h] += jnp.sum(du, axis=0, keepdims=True)
            dcw_ref[h, 0:1, :] += jnp.sum(du * x2, axis=0, keepdims=True)
            dcw_ref[h, 1:2, :] += jnp.sum(du * x1, axis=0, keepdims=True)
            dcw_ref[h, 2:3, :] += jnp.sum(du * x0, axis=0, keepdims=True)
            ext = jnp.concatenate([du, carry_ref[h]], axis=0)
            n1 = pltpu.roll(ext, tm + 7, 0)[:tm]
            n2 = pltpu.roll(ext, tm + 6, 0)[:tm]
            w = ws[h]
            dup_ref[h] = (w[2:3] * du + w[1:2] * n1 + w[0:1] * n2).astype(dup_ref.dtype)
            carry_ref[h] = du[:8]

    def cur(h):
        return pl.BlockSpec((None, tm, tn), lambda j, i: (h, nrb - 1 - i, j))

    def prev(h):
        return pl.BlockSpec((None, 8, tn), lambda j, i: (h, jnp.maximum((nrb - 1 - i) * rb8 - 1, 0), j))

    def par(h, r):
        return pl.BlockSpec((None, r, tn), lambda j, i: (h, 0, j))

    return pl.pallas_call(
        body, name="conv_gate_bwd", grid=(D_FF // tn, nrb),
        in_specs=[cur(0), cur(1), prev(0), prev(1),
                  pl.BlockSpec((tm, tn), lambda j, i: (nrb - 1 - i, j)),
                  par(0, 3), par(1, 3), par(0, 1), par(1, 1)],
        out_specs=[pl.BlockSpec((2, tm, tn), lambda j, i: (0, nrb - 1 - i, j)),
                   pl.BlockSpec((2, 3, tn), lambda j, i: (0, 0, j)),
                   pl.BlockSpec((2, 1, tn), lambda j, i: (0, 0, j))],
        out_shape=[jax.ShapeDtypeStruct((2, s, D_FF), BF16),
                   jax.ShapeDtypeStruct((2, 3, D_FF), F32),
                   jax.ShapeDtypeStruct((2, 1, D_FF), F32)],
        scratch_shapes=[pltpu.VMEM((2, 8, tn), F32)],
        compiler_params=_cparams(("parallel", "arbitrary")),
    )(up, up, up, up, dact, conv_w, conv_w, conv_b, conv_b)


def _split_dot(x, tri, terms):
    piece = x.astype(BF16)
    out = jnp.dot(piece, tri, preferred_element_type=F32)
    rest = x
    for _ in range(terms - 1):
        rest = rest - piece.astype(F32)
        piece = rest.astype(BF16)
        out = out + jnp.dot(piece, tri, preferred_element_type=F32)
    return out


def _split_dot_rhs(tri, x, terms):
    piece = x.astype(BF16)
    out = jnp.dot(tri, piece, preferred_element_type=F32)
    rest = x
    for _ in range(terms - 1):
        rest = rest - piece.astype(F32)
        piece = rest.astype(BF16)
        out = out + jnp.dot(tri, piece, preferred_element_type=F32)
    return out


def _tri(n, kind):
    r = lax.broadcasted_iota(jnp.int32, (n, n), 0)
    c = lax.broadcasted_iota(jnp.int32, (n, n), 1)
    cond = {"le": r <= c, "ge": r >= c, "lt": r < c, "gt": r > c}[kind]
    return jnp.where(cond, 1.0, 0.0).astype(BF16)


def _log_sigmoid(x):
    return jnp.minimum(x, 0.0) - jnp.log(1.0 + jnp.exp(-jnp.abs(x)))


def _forget_fwd(f_logit, bias):
    h, r, _ = f_logit.shape

    def body(x_ref, b_ref, o_ref):
        lf = _log_sigmoid(x_ref[...] + b_ref[...])
        within = _split_dot(lf, _tri(LANES, "le"), 3)
        row_tot = jnp.broadcast_to(within[:, LANES - 1:LANES], (r, LANES))
        before = _split_dot_rhs(_tri(r, "gt"), row_tot, 3)
        o_ref[...] = within + before

    blk = pl.BlockSpec((None, r, LANES), lambda i: (i, 0, 0))
    return pl.pallas_call(
        body, name="forget_cumsum_fwd", grid=(h,),
        in_specs=[blk, pl.BlockSpec((None, 1, LANES), lambda i: (i, 0, 0))],
        out_specs=blk, out_shape=jax.ShapeDtypeStruct((h, r, LANES), F32),
        compiler_params=_cparams(("parallel",)),
    )(f_logit, bias)


def _forget_bwd(f_logit, bias, ksum, qsum):
    h, r, _ = f_logit.shape

    def body(x_ref, b_ref, k_ref, q_ref, dx_ref, db_ref):
        d_f = q_ref[...] - k_ref[...]
        within = _split_dot(d_f, _tri(LANES, "ge"), 3)
        row_tot = jnp.broadcast_to(within[:, 0:1], (r, LANES))
        after = _split_dot_rhs(_tri(r, "lt"), row_tot, 3)
        xv = x_ref[...] + b_ref[...]
        dx = (within + after) * jnp.exp(_log_sigmoid(-xv))
        dx_ref[...] = dx
        db_ref[...] = jnp.broadcast_to(jnp.sum(dx), (1, LANES))

    blk = pl.BlockSpec((None, r, LANES), lambda i: (i, 0, 0))
    one = pl.BlockSpec((None, 1, LANES), lambda i: (i, 0, 0))
    return pl.pallas_call(
        body, name="forget_cumsum_bwd", grid=(h,),
        in_specs=[blk, one, blk, blk], out_specs=[blk, one],
        out_shape=[jax.ShapeDtypeStruct((h, r, LANES), F32), jax.ShapeDtypeStruct((h, 1, LANES), F32)],
        compiler_params=_cparams(("parallel",)),
    )(f_logit, bias, ksum, qsum)


def _head_specs(s, tq):
    qblk = pl.BlockSpec((None, tq, HEAD_DIM), lambda h, i: (h, i, 0))
    full = pl.BlockSpec((None, s, HEAD_DIM), lambda h, i: (h, 0, 0))
    col = pl.BlockSpec((None, tq, 1), lambda h, i: (h, i, 0))
    return qblk, full, col


def _scaled(q_ref):
    return (q_ref[...].astype(F32) * Q_SCALE).astype(BF16)


_NT = (((1,), (1,)), ((), ()))
_TN = (((0,), (0,)), ((), ()))


def _fox_fwd(q, k, v_ones, f_col, f_row, tq, tk):
    h, s, _ = q.shape
    nk = s // tk
    assert tq == tk

    def body(q_ref, k_ref, v_ref, fc_ref, fr_ref, o_ref, lse_ref, m_ref, acc_ref, z0, z1):
        i = pl.program_id(1)
        qs = _scaled(q_ref)
        fq = fc_ref[...]
        m_ref[...] = jnp.full_like(m_ref, NEG_BIG)
        acc_ref[...] = jnp.zeros_like(acc_ref)

        def keys_of(j):
            return pl.ds(pl.multiple_of(j * tk, tk), tk)

        def logits(j):
            return lax.dot_general(qs, k_ref[keys_of(j), :], _NT, preferred_element_type=F32)

        def soft(j, raw, mask):
            sc = raw + fq - fr_ref[j]
            if mask is not None:
                sc = jnp.where(mask, sc, NEG_BIG)
            m_old = m_ref[...]
            m_new = jnp.maximum(m_old, jnp.max(sc, axis=-1, keepdims=True))
            p = jnp.exp(sc - m_new)
            acc_ref[...] = jnp.exp(m_old - m_new) * acc_ref[...] + jnp.dot(
                p.astype(BF16), v_ref[keys_of(j), :], preferred_element_type=F32)
            m_ref[...] = m_new

        @pl.when(i >= 2)
        def _():
            z0[...] = logits(0)

            def step(p, carry):
                j = 2 * p
                z1[...] = logits(j + 1)
                soft(j, z0[...], None)
                z0[...] = logits(jnp.minimum(j + 2, i - 1))
                soft(j + 1, z1[...], None)
                return carry

            lax.fori_loop(0, i // 2, step, 0)

        @pl.when(i % 2 == 1)
        def _():
            soft(i - 1, logits(i - 1), None)

        soft(i, logits(i), _band_mask(tq, tk, 0, 0, strict=False))
        l = acc_ref[:, HEAD_DIM:HEAD_DIM + 1]
        o_ref[...] = acc_ref[:, :HEAD_DIM] / l
        lse_ref[...] = m_ref[...] + jnp.log(l)

    qblk, full, colspec = _head_specs(s, tq)
    return pl.pallas_call(
        body, name="fox_fwd", grid=(h, s // tq),
        in_specs=[qblk, full, pl.BlockSpec((None, s, 2 * HEAD_DIM), lambda hh, i: (hh, 0, 0)), colspec,
                  pl.BlockSpec((None, nk, 1, tk), lambda hh, i: (hh, 0, 0, 0))],
        out_specs=[qblk, colspec],
        out_shape=[jax.ShapeDtypeStruct((h, s, HEAD_DIM), F32), jax.ShapeDtypeStruct((h, s, 1), F32)],
        scratch_shapes=[pltpu.VMEM((tq, 1), F32), pltpu.VMEM((tq, 2 * HEAD_DIM), F32),
                        pltpu.VMEM((tq, tk), F32), pltpu.VMEM((tq, tk), F32)],
        compiler_params=_cparams(("parallel", "parallel")),
    )(q, k, v_ones, f_col, f_row)


def _fox_bwd(q, k, v, f_col, f_row, o, lse, d_o, tq, tk):
    h, s, _ = q.shape
    nk = s // tk
    assert tq == tk

    def body(q_ref, k_ref, v_ref, fc_ref, fr_ref, o_ref, lse_ref, do_ref,
             dq_ref, dk_ref, dv_ref, ks_ref, qs_ref, dq_acc, qsum_acc, z0, z1, p0, p1):
        i = pl.program_id(1)

        @pl.when(i == 0)
        def _():
            dk_ref[...] = jnp.zeros_like(dk_ref)
            dv_ref[...] = jnp.zeros_like(dv_ref)
            ks_ref[...] = jnp.zeros_like(ks_ref)

        qs = _scaled(q_ref)
        fq = fc_ref[...]
        lse_v = lse_ref[...]
        dob = do_ref[...].astype(BF16)
        delta = jnp.sum(dob.astype(F32) * o_ref[...], axis=-1, keepdims=True)
        dq_acc[...] = jnp.zeros_like(dq_acc)
        qsum_acc[...] = jnp.zeros_like(qsum_acc)

        def keys_of(j):
            return pl.ds(pl.multiple_of(j * tk, tk), tk)

        def products(j):
            at = keys_of(j)
            return (lax.dot_general(qs, k_ref[at, :], _NT, preferred_element_type=F32),
                    lax.dot_general(dob, v_ref[at, :], _NT, preferred_element_type=F32))

        def grads(j, raw, dp, mask):
            at = keys_of(j)
            sc = raw + fq - fr_ref[j]
            if mask is not None:
                sc = jnp.where(mask, sc, NEG_BIG)
            p = jnp.exp(sc - lse_v)
            ds = p * (dp - delta)
            dsb = ds.astype(BF16)
            dq_acc[...] += jnp.dot(dsb, k_ref[at, :], preferred_element_type=F32)
            dk_ref[at, :] += lax.dot_general(dsb, qs, _TN, preferred_element_type=F32)
            dv_ref[at, :] += lax.dot_general(p.astype(BF16), dob, _TN, preferred_element_type=F32)
            ks_ref[j] += jnp.sum(ds.reshape(tq // 8, 8, tk), axis=0)
            qsum_acc[...] += jnp.sum(ds, axis=-1, keepdims=True)

        @pl.when(i >= 2)
        def _():
            z0[...], p0[...] = products(0)

            def step(pp, carry):
                j = 2 * pp
                z1[...], p1[...] = products(j + 1)
                grads(j, z0[...], p0[...], None)
                z0[...], p0[...] = products(jnp.minimum(j + 2, i - 1))
                grads(j + 1, z1[...], p1[...], None)
                return carry

            lax.fori_loop(0, i // 2, step, 0)

        @pl.when(i % 2 == 1)
        def _():
            grads(i - 1, *products(i - 1), None)

        grads(i, *products(i), _band_mask(tq, tk, 0, 0, strict=False))
        dq_ref[...] = dq_acc[...] * Q_SCALE
        qs_ref[...] = qsum_acc[...]

    qblk, full, colspec = _head_specs(s, tq)
    frow = pl.BlockSpec((None, nk, 1, tk), lambda hh, i: (hh, 0, 0, 0))
    big = pltpu.VMEM((tq, tk), F32)
    return pl.pallas_call(
        body, name="fox_bwd", grid=(h, s // tq),
        in_specs=[qblk, full, full, colspec, frow, qblk, colspec, qblk],
        out_specs=[qblk, full, full, pl.BlockSpec((None, nk, 8, tk), lambda hh, i: (hh, 0, 0, 0)), colspec],
        out_shape=[jax.ShapeDtypeStruct((h, s, HEAD_DIM), F32)] * 3
        + [jax.ShapeDtypeStruct((h, nk, 8, tk), F32), jax.ShapeDtypeStruct((h, s, 1), F32)],
        scratch_shapes=[pltpu.VMEM((tq, HEAD_DIM), F32), pltpu.VMEM((tq, 1), F32), big, big, big, big],
        compiler_params=_cparams(("parallel", "arbitrary")),
    )(q, k, v, f_col, f_row, o, lse, d_o)


SB_TERMS = 2
G_TERMS = 1
ROW_CHUNK = 256
LOG2E = 1.4426950408889634
LN2 = 0.6931471805599453


def _softplus2(z2):
    return jnp.maximum(z2, 0.0) + jnp.log2(1.0 + jnp.exp2(-jnp.abs(z2)))


def _band_mask(rows, cols, row0, col0, strict=True):
    r = row0 + lax.broadcasted_iota(jnp.int32, (rows, cols), 0)
    c = col0 + lax.broadcasted_iota(jnp.int32, (rows, cols), 1)
    return c < r if strict else c <= r


def _sb_fwd(q, k, v, tq, tk):
    h, s, _ = q.shape

    rc = min(ROW_CHUNK, tq)
    assert tq % tk == 0 and tq % rc == 0

    assert (tq // tk) % 2 == 0

    def body(q_ref, k_ref, v_ref, o_ref, tot_ref, acc_ref, run_ref, z0, z1, d0, d1, t0, t1):
        z_refs, d_refs, t_refs = (z0, z1), (d0, d1), (t0, t1)
        i = pl.program_id(1)
        qs = _scaled(q_ref)
        tri = _tri(tk, "ge")
        acc_ref[...] = jnp.zeros_like(acc_ref)
        run_ref[...] = jnp.zeros_like(run_ref)
        n_full = (i * tq) // tk

        def block(j, rel):
            at = pl.ds(pl.multiple_of(j * tk, tk), tk)
            ks = k_ref[at, :]
            vs = v_ref[at, :]
            run = run_ref[...]
            live = [r for r in range(tq // rc) if rel is None or rel * tk < (r + 1) * rc - 1]
            z2, inc, mask, outs = {}, {}, {}, {}

            def logits(r):
                rows = slice(r * rc, (r + 1) * rc)
                z2[r] = lax.dot_general(qs[rows], ks, _NT, preferred_element_type=F32) * LOG2E

            def sums(r):
                sp = _softplus2(z2[r])
                visible = rel is None or rel * tk + tk - 1 < r * rc
                mask[r] = None if visible else _band_mask(rc, tk, r * rc, rel * tk)
                if mask[r] is not None:
                    sp = jnp.where(mask[r], sp, 0.0)
                inc[r] = _split_dot(sp, tri, SB_TERMS)

            def weigh(r):
                rows = slice(r * rc, (r + 1) * rc)
                w = jnp.exp2(z2[r] - (run[rows] + inc[r]))
                if mask[r] is not None:
                    w = jnp.where(mask[r], w, 0.0)
                outs[r] = jnp.dot(w.astype(BF16), vs, preferred_element_type=F32)

            for t in range(len(live) + 2):
                if t < len(live):
                    logits(live[t])
                if 0 <= t - 1 < len(live):
                    sums(live[t - 1])
                if 0 <= t - 2 < len(live):
                    weigh(live[t - 2])
            for r in live:
                rows = slice(r * rc, (r + 1) * rc)
                acc_ref[rows, :] += outs[r]
                run_ref[rows, :] += inc[r][:, 0:1]

        for rel in reversed(range(tq // tk)):
            block(n_full + rel, rel)

        def keys_of(b):
            j = n_full - 1 - jnp.minimum(b, n_full - 1)
            return pl.ds(pl.multiple_of(j * tk, tk), tk)

        def logits(b, slot):
            z_refs[slot][...] = lax.dot_general(qs, k_ref[keys_of(b), :], _NT,
                                                preferred_element_type=F32) * LOG2E

        def sums(slot):
            z2 = z_refs[slot][...]
            inc = _split_dot(_softplus2(z2), tri, SB_TERMS)
            d_refs[slot][...] = z2 - inc
            t_refs[slot][...] = inc[:, 0:1]

        def weigh(b, slot):
            w = jnp.exp2(d_refs[slot][...] - run_ref[...])
            acc_ref[...] += jnp.dot(w.astype(BF16), v_ref[keys_of(b), :], preferred_element_type=F32)
            run_ref[...] += t_refs[slot][...]

        @pl.when(n_full > 0)
        def _():
            logits(0, 0)
            logits(1, 1)
            sums(0)

            def step(p, carry):
                b = 2 * p
                logits(b + 2, 0)
                sums(1)
                weigh(b, 0)
                logits(b + 3, 1)
                sums(0)
                weigh(b + 1, 1)
                return carry

            lax.fori_loop(0, n_full // 2, step, 0)

        o_ref[...] = acc_ref[...]
        tot_ref[...] = run_ref[...] * (-LN2)

    qblk, full, colspec = _head_specs(s, tq)
    return pl.pallas_call(
        body, name="sb_fwd", grid=(h, s // tq),
        in_specs=[qblk, full, full], out_specs=[qblk, colspec],
        out_shape=[jax.ShapeDtypeStruct((h, s, HEAD_DIM), F32), jax.ShapeDtypeStruct((h, s, 1), F32)],
        scratch_shapes=[pltpu.VMEM((tq, HEAD_DIM), F32), pltpu.VMEM((tq, 1), F32),
                        pltpu.VMEM((tq, tk), F32), pltpu.VMEM((tq, tk), F32),
                        pltpu.VMEM((tq, tk), F32), pltpu.VMEM((tq, tk), F32),
                        pltpu.VMEM((tq, 1), F32), pltpu.VMEM((tq, 1), F32)],
        compiler_params=_cparams(("parallel", "parallel")),
    )(q, k, v)


def _sb_bwd(q, k, v, tot, d_o, tq, tk):
    h, s, _ = q.shape

    assert tq % tk == 0 and (tq // tk) % 2 == 0

    def body(q_ref, k_ref, v_ref, tot_ref, do_ref, dq_ref, dk_ref, dv_ref, dq_acc, off_ref, grun_ref,
             z0, z1, p0, p1, u0, u1, b0, b1, t0, t1):
        z_refs, p_refs, u_refs, b_refs, t_refs = (z0, z1), (p0, p1), (u0, u1), (b0, b1), (t0, t1)
        i = pl.program_id(1)

        @pl.when(i == 0)
        def _():
            dk_ref[...] = jnp.zeros_like(dk_ref)
            dv_ref[...] = jnp.zeros_like(dv_ref)

        qs = _scaled(q_ref)
        dob = do_ref[...].astype(BF16)
        tri = _tri(tk, "le")
        dq_acc[...] = jnp.zeros_like(dq_acc)
        off_ref[...] = tot_ref[...] * LOG2E
        grun_ref[...] = jnp.zeros_like(grun_ref)
        n_full = (i * tq) // tk

        def keys_of(b):
            return pl.ds(pl.multiple_of(jnp.minimum(b, n_full - 1) * tk, tk), tk)

        def finish(at, w, g, ginc, beta, t_blk, mask):
            dz = g - beta * (grun_ref[...] + ginc)
            if mask is not None:
                dz = jnp.where(mask, dz, 0.0)
            dzb = dz.astype(BF16)
            dq_acc[...] += jnp.dot(dzb, k_ref[at, :], preferred_element_type=F32)
            dk_ref[at, :] += lax.dot_general(dzb, qs, _TN, preferred_element_type=F32)
            dv_ref[at, :] += lax.dot_general(w.astype(BF16), dob, _TN, preferred_element_type=F32)
            off_ref[...] += t_blk
            grun_ref[...] += ginc[:, tk - 1:tk]

        def logits(b, slot):
            z_refs[slot][...] = lax.dot_general(qs, k_ref[keys_of(b), :], _NT,
                                                preferred_element_type=F32) * LOG2E

        def sums(b, slot):
            z2 = z_refs[slot][...]
            sp = _softplus2(z2)
            lb2 = z2 - sp
            linc = _split_dot(sp, tri, SB_TERMS)
            p_refs[slot][...] = lax.dot_general(dob, v_ref[keys_of(b), :], _NT, preferred_element_type=F32)
            u_refs[slot][...] = lb2 + linc
            b_refs[slot][...] = jnp.exp2(lb2)
            t_refs[slot][...] = linc[:, tk - 1:tk]

        def weigh(slot):
            w = jnp.exp2(u_refs[slot][...] + off_ref[...])
            g = w * p_refs[slot][...]
            return w, g, _split_dot(g, tri, G_TERMS)

        @pl.when(n_full > 0)
        def _():
            logits(0, 0)
            logits(1, 1)
            sums(0, 0)

            def step(p, carry):
                for slot in (0, 1):
                    b = 2 * p + slot
                    w, g, ginc = weigh(slot)
                    logits(b + 2, slot)
                    sums(b + 1, 1 - slot)
                    finish(keys_of(b), w, g, ginc, b_refs[slot][...], t_refs[slot][...], None)
                return carry

            lax.fori_loop(0, n_full // 2, step, 0)

        for rel in range(tq // tk):
            at = pl.ds(pl.multiple_of((n_full + rel) * tk, tk), tk)
            mask = _band_mask(tq, tk, 0, rel * tk)
            z2 = lax.dot_general(qs, k_ref[at, :], _NT, preferred_element_type=F32) * LOG2E
            sp = _softplus2(z2)
            linc = _split_dot(jnp.where(mask, sp, 0.0), tri, SB_TERMS)
            w = jnp.where(mask, jnp.exp2(z2 - sp + linc + off_ref[...]), 0.0)
            g = w * lax.dot_general(dob, v_ref[at, :], _NT, preferred_element_type=F32)
            ginc = _split_dot(g, tri, G_TERMS)
            finish(at, w, g, ginc, jnp.exp2(z2 - sp), linc[:, tk - 1:tk], mask)

        dq_ref[...] = dq_acc[...] * Q_SCALE

    qblk, full, colspec = _head_specs(s, tq)
    big = pltpu.VMEM((tq, tk), F32)
    return pl.pallas_call(
        body, name="sb_bwd", grid=(h, s // tq),
        in_specs=[qblk, full, full, colspec, qblk], out_specs=[qblk, full, full],
        out_shape=[jax.ShapeDtypeStruct((h, s, HEAD_DIM), F32)] * 3,
        scratch_shapes=[pltpu.VMEM((tq, HEAD_DIM), F32), pltpu.VMEM((tq, 1), F32), pltpu.VMEM((tq, 1), F32)]
        + [big] * 8 + [pltpu.VMEM((tq, 1), F32)] * 2,
        compiler_params=_cparams(("parallel", "arbitrary")),
    )(q, k, v, tot, d_o)


def _sum_adamw(parts, w, m, v, name, tr=512):
    _, rows, lanes = parts.shape
    tr = _tile(rows, tr, 16)
    c_m = 1.0 - ADAM_B1 ** ADAM_STEP
    c_v = 1.0 - ADAM_B2 ** ADAM_STEP

    def body(p_ref, w_ref, m_ref, v_ref, g_ref, d_ref, nm_ref, nv_ref):
        g = p_ref[0].astype(F32)
        for j in range(1, N_DEV):
            g = g + p_ref[j].astype(F32)
        nm = ADAM_B1 * m_ref[...] + (1.0 - ADAM_B1) * g
        nv = ADAM_B2 * v_ref[...] + (1.0 - ADAM_B2) * (g * g)
        m_hat = nm / c_m
        v_hat = nv / c_v
        g_ref[...] = g
        d_ref[...] = -ADAM_LR * (m_hat / (jnp.sqrt(v_hat) + ADAM_EPS) + ADAM_WD * w_ref[...])
        nm_ref[...] = nm
        nv_ref[...] = nv

    blk = pl.BlockSpec((tr, lanes), lambda i: (i, 0))
    return pl.pallas_call(
        body, name=name, grid=(rows // tr,),
        in_specs=[pl.BlockSpec((N_DEV, tr, lanes), lambda i: (0, i, 0)), blk, blk, blk],
        out_specs=[blk] * 4, out_shape=[jax.ShapeDtypeStruct((rows, lanes), F32)] * 4,
        compiler_params=_cparams(("parallel",)),
    )(parts, w, m, v)


def _to_heads(t):
    s = t.shape[0]
    return t.reshape(s, N_GROUP_HEADS, HEAD_DIM).transpose(1, 0, 2)


def _from_heads(t):
    s = t.shape[1]
    return t.transpose(1, 0, 2).reshape(s, GROUP_W)


def kernel(x, attn_norm_g, w_in, forget_bias, fox_out_g, sb_out_g, w_out, ffn_norm_g, w_up, conv_w, conv_b, w_down, final_norm_g, loss_target, m_attn_norm_g, m_w_in, m_forget_bias, m_fox_out_g, m_sb_out_g, m_w_out, m_ffn_norm_g, m_w_up, m_conv_w, m_conv_b, m_w_down, m_final_norm_g, v_attn_norm_g, v_w_in, v_forget_bias, v_fox_out_g, v_sb_out_g, v_w_out, v_ffn_norm_g, v_w_up, v_conv_w, v_conv_b, v_w_down, v_final_norm_g):
    s = x.shape[1]
    xs = x[0]
    tq = min(ATTN_TQ, s)
    tk_fox = min(FOX_TK, s)
    tk_sb = min(SB_TK, s)
    in_shard, up_shard, out_shard, down_shard = IN_COLS // N_DEV, 2 * D_FF // N_DEV, D_MODEL // N_DEV, D_FF // N_DEV

    cw = conv_w[0]
    cw_hi = cw.astype(BF16)
    cw_lo = (cw - cw_hi.astype(F32)).astype(BF16)
    wpack = _pack([w_in[0], w_out[0], w_up[0], w_down[0], cw_hi, cw_lo], PACK_ROWS, BF16)
    gathered = _all_gather(wpack)
    g_in, g_out, g_up, g_down, g_chi, g_clo = _unpack(
        gathered, [(D_MODEL, in_shard), (out_shard, D_MODEL), (D_MODEL, up_shard), (down_shard, D_MODEL),
                   (3, up_shard), (3, up_shard)], lead=(N_DEV,))
    w_in_f = g_in.transpose(1, 0, 2).reshape(D_MODEL, IN_COLS)
    n_gate = QKV_W + N_GROUP_HEADS
    w_in_p = jnp.concatenate([w_in_f[:, :n_gate], jnp.zeros((D_MODEL, GATE_PAD - N_GROUP_HEADS), BF16),
                              w_in_f[:, n_gate:]], axis=1)
    w_out_f = g_out.reshape(D_MODEL, D_MODEL)
    w_up_f = g_up.transpose(1, 0, 2).reshape(D_MODEL, 2 * D_FF)
    w_down_f = g_down.reshape(D_FF, D_MODEL)
    conv_w_f = (g_chi.astype(F32) + g_clo.astype(F32)).transpose(1, 0, 2).reshape(3, 2 * D_FF)
    conv_w2 = conv_w_f.reshape(3, 2, D_FF).transpose(1, 0, 2)
    conv_b2 = conv_b.reshape(2, 1, D_FF)

    h1 = _rms_fwd(xs, attn_norm_g)
    proj = _mm_nn(h1, w_in_p, BF16, "in_proj", tn=640)
    fox_qkv = proj[:, :QKV_W]
    f_logit = _mm_nn(h1, w_in_p[:, QKV_W:QKV_W + GATE_PAD], F32, "gate_proj")[:, :N_GROUP_HEADS]
    sb_qkv = proj[:, QKV_W + GATE_PAD:]
    fq, fk, fv = (_to_heads(fox_qkv[:, j * GROUP_W:(j + 1) * GROUP_W]) for j in range(3))
    sq, sk, sv = (_to_heads(sb_qkv[:, j * GROUP_W:(j + 1) * GROUP_W]) for j in range(3))

    f_logit_h = f_logit.T.reshape(N_GROUP_HEADS, s // LANES, LANES)
    bias_h = jnp.broadcast_to(forget_bias.reshape(N_GROUP_HEADS, 1, 1), (N_GROUP_HEADS, 1, LANES))
    big_f = _forget_fwd(f_logit_h, bias_h)
    f_col = big_f.reshape(N_GROUP_HEADS, s, 1)
    f_row = big_f.reshape(N_GROUP_HEADS, s // tk_fox, 1, tk_fox)

    fv_ones = jnp.concatenate([fv, jnp.ones_like(fv)], axis=-1)
    o_fox_h, lse = _fox_fwd(fq, fk, fv_ones, f_col, f_row, tq, tk_fox)
    o_sb_h, sb_tot = _sb_fwd(sq, sk, sv, tq, tk_sb)
    o_fox = _from_heads(o_fox_h)
    o_sb = _from_heads(o_sb_h)
    o_n = _group_rms_fwd(o_fox, o_sb, fox_out_g, sb_out_g)
    x1 = _mm_nn(o_n, w_out_f, F32, "out_proj", resid=xs)
    h2 = _rms_fwd(x1, ffn_norm_g)
    up = _mm_up(h2, w_up_f)
    act = _conv_gate_fwd(up, conv_w2, conv_b2)
    x2 = _mm_nn(act, w_down_f, F32, "down_proj", resid=x1, tk=1408)

    d_x2, d_x2b, dg_final, loss_part = _loss_head(x2, loss_target[0], final_norm_g.reshape(1, D_MODEL))
    d_act = _mm_nt(d_x2b, w_down_f, BF16, "d_act", tn=1408)
    dw_down = _mm_tn(act, d_x2b, "d_w_down", tm=1408)
    d_up, dcw2, dcb2 = _conv_gate_bwd(up, d_act, conv_w2, conv_b2)
    d_h2 = _mm_dup_nt(d_up, w_up_f)
    dw_up = _mm_dwup_tn(h2, d_up)
    d_x1, d_x1b, dg_ffn = _rms_bwd(x1, d_h2, ffn_norm_g, d_x2, dy_col=0, name="ffn_norm_bwd", want_bf16=True)
    d_on = _mm_nt(d_x1b, w_out_f, F32, "d_o_normed")
    dw_out = _mm_tn(o_n, d_x1b, "d_w_out")
    d_o_fox, dg_fox = _rms_bwd(o_fox, d_on, fox_out_g, None, dy_col=0, name="fox_norm_bwd", want_bf16=False)
    d_o_sb, dg_sb = _rms_bwd(o_sb, d_on, sb_out_g, None, dy_col=1, name="sb_norm_bwd", want_bf16=False)

    dfq, dfk, dfv, ksum8, qsum = _fox_bwd(fq, fk, fv, f_col, f_row, o_fox_h, lse, _to_heads(d_o_fox), tq, tk_fox)
    dsq, dsk, dsv = _sb_bwd(sq, sk, sv, sb_tot, _to_heads(d_o_sb), tq, tk_sb)
    ksum = jnp.sum(ksum8, axis=2).reshape(N_GROUP_HEADS, s // LANES, LANES)
    d_f_logit_h, d_bias_h = _forget_bwd(f_logit_h, bias_h, ksum,
                                        qsum.reshape(N_GROUP_HEADS, s // LANES, LANES))
    d_f_logit = d_f_logit_h.reshape(N_GROUP_HEADS, s).T

    d_proj = jnp.concatenate(
        [_from_heads(t).astype(BF16) for t in (dfq, dfk, dfv)]
        + [d_f_logit.astype(BF16), jnp.zeros((s, GATE_PAD - N_GROUP_HEADS), BF16)]
        + [_from_heads(t).astype(BF16) for t in (dsq, dsk, dsv)], axis=1)
    d_h1 = _mm_nt(d_proj, w_in_p, F32, "d_h1", tk=640)
    dw_in_p = _mm_tn(h1, d_proj, "d_w_in", tn=640)
    dw_in = jnp.concatenate([dw_in_p[:, :n_gate], dw_in_p[:, QKV_W + GATE_PAD:]], axis=1)
    grad_x, dg_attn = _rms_bwd(xs, d_h1, attn_norm_g, d_x1, dy_col=0, name="attn_norm_bwd", want_bf16=False)

    dconv_w = dcw2.transpose(1, 0, 2).reshape(3, 2 * D_FF)
    dconv_b = dcb2.reshape(1, 2 * D_FF)
    def by_cols(a, width):
        return a.astype(BF16).reshape(a.shape[0], N_DEV, width).transpose(1, 0, 2).reshape(N_DEV, -1)

    def by_rows(a):
        return a.astype(BF16).reshape(N_DEV, -1)

    pieces = [by_cols(dw_in, in_shard), by_rows(dw_out), by_cols(dw_up, up_shard), by_rows(dw_down),
              by_cols(dconv_w, up_shard)]
    pieces = [jnp.pad(p, ((0, 0), (0, -p.shape[1] % LANES))) for p in pieces]
    flat = jnp.concatenate(pieces, axis=1)
    gpack = jnp.pad(flat, ((0, 0), (0, PACK_ROWS * LANES - flat.shape[1]))).reshape(N_DEV, PACK_ROWS, LANES)
    small_shapes = [(1, D_MODEL), (1, N_GROUP_HEADS), (1, GROUP_W), (1, GROUP_W), (1, D_MODEL),
                    (1, 2 * D_FF), (D_MODEL,), (1,)]
    spack = _pack([dg_attn, d_bias_h[:, 0, 0], dg_fox, dg_sb, dg_ffn, dconv_b, dg_final, loss_part[0, 0:1]],
                  SMALL_ROWS, F32)
    grecv, srecv = _grad_exchange(gpack, spack)

    shard_shapes = [(D_MODEL, in_shard), (out_shard, D_MODEL), (D_MODEL, up_shard), (down_shard, D_MODEL),
                    (3, up_shard)]

    def shard_pack(a_in, a_out, a_up, a_down, a_conv):
        return _pack([a_in[0], a_out[0], a_up[0], a_down[0], a_conv[0]], PACK_ROWS, F32)

    big = _sum_adamw(grecv, shard_pack(w_in, w_out, w_up, w_down, conv_w),
                     shard_pack(m_w_in, m_w_out, m_w_up, m_w_down, m_conv_w),
                     shard_pack(v_w_in, v_w_out, v_w_up, v_w_down, v_conv_w), "adamw_sharded")

    def small_pack(a_attn, a_bias, a_fox, a_sb, a_ffn, a_cb, a_fin):
        return _pack([a_attn, a_bias, a_fox, a_sb, a_ffn, a_cb, a_fin, jnp.zeros((1,), F32)], SMALL_ROWS, F32)

    small = _sum_adamw(srecv, small_pack(attn_norm_g, forget_bias, fox_out_g, sb_out_g, ffn_norm_g, conv_b, final_norm_g),
                       small_pack(m_attn_norm_g, m_forget_bias, m_fox_out_g, m_sb_out_g, m_ffn_norm_g, m_conv_b, m_final_norm_g),
                       small_pack(v_attn_norm_g, v_forget_bias, v_fox_out_g, v_sb_out_g, v_ffn_norm_g, v_conv_b, v_final_norm_g),
                       "adamw_replicated", tr=SMALL_ROWS)

    outs = []
    loss = None
    for kind in range(4):
        b_in, b_out, b_up, b_down, b_conv = _unpack(big[kind], shard_shapes)
        s_attn, s_bias, s_fox, s_sb, s_ffn, s_cb, s_fin, s_loss = _unpack(small[kind], small_shapes)
        if kind == 0:
            loss = s_loss[0]
        outs += [s_attn, b_in[None], s_bias, s_fox, s_sb, b_out[None], s_ffn, b_up[None], b_conv[None], s_cb,
                 b_down[None], s_fin]
    return (loss, grad_x[None], *outs)
```

```python
import jax
import jax.numpy as jnp
from jax import lax
from jax.experimental import pallas as pl
from jax.experimental.pallas import tpu as pltpu

F32 = jnp.float32
BF16 = jnp.bfloat16

D_MODEL = 1024
HEAD_DIM = 64
N_GROUP_HEADS = 8
GROUP_W = N_GROUP_HEADS * HEAD_DIM
QKV_W = 3 * GROUP_W
IN_COLS = 2 * QKV_W + N_GROUP_HEADS
GATE_PAD = 128
IN_COLS_PAD = 2 * QKV_W + GATE_PAD
D_FF = 2816
N_DEV = 8
EPS = 1e-6
Q_SCALE = HEAD_DIM ** -0.5

ADAM_LR = 0.001
ADAM_B1 = 0.9
ADAM_B2 = 0.999
ADAM_EPS = 1e-08
ADAM_WD = 0.01
ADAM_STEP = 10

LANES = 128
SMALL_ROWS = 80
VMEM_LIMIT = 56 * 1024 * 1024
NEG_BIG = -1e30
ATTN_TQ = 512
FOX_TK = 512
SB_TK = 256
MESH = pl.DeviceIdType.MESH


def _cparams(sem=None, **kw):
    return pltpu.CompilerParams(dimension_semantics=sem, vmem_limit_bytes=VMEM_LIMIT, **kw)


def _tile(n, target, mult=LANES):
    if n <= target:
        return n
    t = (target // mult) * mult
    while t >= mult:
        if n % t == 0:
            return t
        t -= mult
    return n


def _seg_len(shape):
    n = 1
    for s in shape:
        n *= s
    return -(-n // LANES) * LANES


def _pack(arrs, rows, dtype):
    parts = []
    for a in arrs:
        f = a.reshape(-1).astype(dtype)
        parts.append(jnp.pad(f, (0, _seg_len(a.shape) - f.shape[0])))
    flat = jnp.concatenate(parts)
    flat = jnp.pad(flat, (0, rows * LANES - flat.shape[0]))
    return flat.reshape(rows, LANES)


def _unpack(p, shapes, lead=()):
    flat = p.reshape(lead + (-1,))
    out, off = [], 0
    for shp in shapes:
        n = 1
        for s in shp:
            n *= s
        out.append(flat[..., off:off + n].reshape(lead + tuple(shp)))
        off += _seg_len(shp)
    return out


def _my_pos():
    return lax.axis_index("x"), lax.axis_index("y"), lax.axis_index("c")


def _all_gather(blocks):
    n = len(blocks)

    def body(*refs):
        x_refs, out_refs = refs[:n], refs[n:2 * n]
        send_sems, recv_sems, local_sems = refs[2 * n:]
        x, y, c = _my_pos()
        me, sibling = (x, y, c), (x, y, 1 - c)
        chips = [(1 - x, y), (x, 1 - y), (1 - x, 1 - y)]

        def copy(a, k, blk, to, own=False):
            px, py, pc = blk
            slot = out_refs[a].at[4 * px + 2 * py + pc]
            return pltpu.make_async_remote_copy(
                src_ref=x_refs[a] if own else slot, dst_ref=slot,
                send_sem=send_sems.at[a, k], recv_sem=recv_sems.at[a, k],
                device_id=to, device_id_type=MESH)

        mine = [pltpu.make_async_copy(x_refs[a], out_refs[a].at[4 * x + 2 * y + c], local_sems.at[a])
                for a in range(n)]
        for cp in mine:
            cp.start()
        first = []
        for a in range(n):
            first.append(copy(a, 0, me, sibling, own=True))
            first += [copy(a, 1 + j, me, (*chip, c), own=True) for j, chip in enumerate(chips)]
        for cp in first:
            cp.start()
        passed = []
        for j, chip in enumerate(chips):
            for a in range(n):
                copy(a, 1 + j, (*chip, c), me).wait_recv()
                passed.append(copy(a, 4 + j, (*chip, c), sibling))
                passed[-1].start()
        for a in range(n):
            copy(a, 0, sibling, me).wait_recv()
            for j, chip in enumerate(chips):
                copy(a, 4 + j, (*chip, 1 - c), me).wait_recv()
        for cp in first + passed:
            cp.wait_send()
        for cp in mine:
            cp.wait()

    hbm = pl.BlockSpec(memory_space=pl.ANY)
    return pl.pallas_call(
        body, name="weights_all_gather",
        out_shape=[jax.ShapeDtypeStruct((N_DEV,) + b.shape, b.dtype) for b in blocks],
        in_specs=[hbm] * n, out_specs=[hbm] * n,
        scratch_shapes=[pltpu.SemaphoreType.DMA((n, 7)), pltpu.SemaphoreType.DMA((n, 7)),
                        pltpu.SemaphoreType.DMA((n,))],
    )(*blocks)


def _grad_exchange(slabs, spack):
    n = len(slabs) + 1

    def body(*refs):
        in_refs, out_refs = refs[:n], refs[n:2 * n]
        send_sems, recv_sems, local_sems = refs[2 * n:]
        x, y, c = _my_pos()
        my_id = 4 * x + 2 * y + c

        def src_of(a, dev):
            return in_refs[a] if a == n - 1 else in_refs[a].at[dev]

        own = [pltpu.make_async_copy(src_of(a, my_id), out_refs[a].at[my_id], local_sems.at[a])
               for a in range(n)]
        for cp in own:
            cp.start()
        sends, arrivals = [], []
        for k in range(1, N_DEV):
            px, py, pc = x ^ (k >> 2), y ^ ((k >> 1) & 1), c ^ (k & 1)
            peer_id = 4 * px + 2 * py + pc
            for a in range(n):
                for dst_slot, bucket in ((my_id, sends), (peer_id, arrivals)):
                    bucket.append(pltpu.make_async_remote_copy(
                        src_ref=src_of(a, peer_id), dst_ref=out_refs[a].at[dst_slot],
                        send_sem=send_sems.at[a, k - 1], recv_sem=recv_sems.at[a, k - 1],
                        device_id=(px, py, pc), device_id_type=MESH))
        for cp in sends:
            cp.start()
        for cp in arrivals:
            cp.wait_recv()
        for cp in sends:
            cp.wait_send()
        for cp in own:
            cp.wait()

    hbm = pl.BlockSpec(memory_space=pl.ANY)
    return pl.pallas_call(
        body, name="grad_exchange",
        out_shape=[jax.ShapeDtypeStruct(g.shape, g.dtype) for g in slabs]
        + [jax.ShapeDtypeStruct((N_DEV,) + spack.shape, spack.dtype)],
        in_specs=[hbm] * n, out_specs=[hbm] * n,
        scratch_shapes=[pltpu.SemaphoreType.DMA((n, 7)), pltpu.SemaphoreType.DMA((n, 7)),
                        pltpu.SemaphoreType.DMA((n,))],
    )(*slabs, spack)


def _col_windows(n_shards, width, gap_at=None, gap=0):
    out = []
    for j in range(n_shards):
        g0, g1 = j * width, (j + 1) * width
        cuts = [g0, g1] if gap_at is None or not g0 < gap_at < g1 else [g0, gap_at, g1]
        for a, b in zip(cuts[:-1], cuts[1:]):
            out.append((j, a - g0, b - g0, a + (gap if gap_at is not None and a >= gap_at else 0)))
    return out


def _assemble_cols(parts, total, windows, name, tr=256):
    n, rows, w = parts.shape
    tr = _tile(rows, tr, 16)

    def body(p_ref, o_ref):
        o_ref[...] = jnp.zeros_like(o_ref)
        for j, lo, hi, dst in windows:
            o_ref[:, dst:dst + hi - lo] = p_ref[j, :, lo:hi]

    return pl.pallas_call(
        body, name=name, grid=(rows // tr,),
        in_specs=[pl.BlockSpec((n, tr, w), lambda i: (0, i, 0))],
        out_specs=pl.BlockSpec((tr, total), lambda i: (i, 0)),
        out_shape=jax.ShapeDtypeStruct((rows, total), parts.dtype),
        compiler_params=_cparams(("parallel",)),
    )(parts)


def _split_cols(full, n, w, windows, name, tr=256):
    rows, total = full.shape
    tr = _tile(rows, tr, 16)

    def body(f_ref, o_ref):
        for j, lo, hi, dst in windows:
            o_ref[j, :, lo:hi] = f_ref[:, dst:dst + hi - lo].astype(o_ref.dtype)

    return pl.pallas_call(
        body, name=name, grid=(rows // tr,),
        in_specs=[pl.BlockSpec((tr, total), lambda i: (i, 0))],
        out_specs=pl.BlockSpec((n, tr, w), lambda i: (0, i, 0)),
        out_shape=jax.ShapeDtypeStruct((n, rows, w), BF16),
        compiler_params=_cparams(("parallel",)),
    )(full)


_DIMS = {"nn": (((1,), (0,)), ((), ())), "nt": (((1,), (1,)), ((), ())), "tn": (((0,), (0,)), ((), ()))}


def _matmul(a, b, *, mode, grid, a_block, a_map, b_block, b_map, o_block, o_map, out_shape, name,
            resid=None):
    nk = grid[2]
    dims = _DIMS[mode]

    def body(*refs):
        if resid is None:
            a_ref, b_ref, o_ref, acc_ref = refs
            r_ref = None
        else:
            a_ref, b_ref, r_ref, o_ref, acc_ref = refs
        k = pl.program_id(2)

        @pl.when(k == 0)
        def _():
            acc_ref[...] = jnp.zeros_like(acc_ref)

        acc_ref[...] += lax.dot_general(a_ref[...], b_ref[...], dims, preferred_element_type=F32)

        @pl.when(k == nk - 1)
        def _():
            res = acc_ref[...]
            if r_ref is not None:
                res = r_ref[...] + res
            o_ref[...] = res.astype(o_ref.dtype)

    in_specs = [pl.BlockSpec(a_block, a_map), pl.BlockSpec(b_block, b_map)]
    args = [a, b]
    if resid is not None:
        in_specs.append(pl.BlockSpec(o_block, o_map))
        args.append(resid)
    acc_shape = tuple(d for d in o_block if d is not None)
    return pl.pallas_call(
        body, name=name, grid=grid, in_specs=in_specs,
        out_specs=pl.BlockSpec(o_block, o_map), out_shape=out_shape,
        scratch_shapes=[pltpu.VMEM(acc_shape, F32)],
        compiler_params=_cparams(("parallel", "parallel", "arbitrary")),
    )(*args)


def _mm_nn(a, b, out_dtype, name, resid=None, tm=1024, tn=1024, tk=1024):
    m, kk = a.shape
    n = b.shape[1]
    tm, tn, tk = _tile(m, tm, 8), _tile(n, tn), _tile(kk, tk)
    return _matmul(a, b, mode="nn", grid=(m // tm, n // tn, kk // tk),
                   a_block=(tm, tk), a_map=lambda i, j, k: (i, k),
                   b_block=(tk, tn), b_map=lambda i, j, k: (k, j),
                   o_block=(tm, tn), o_map=lambda i, j, k: (i, j),
                   out_shape=jax.ShapeDtypeStruct((m, n), out_dtype), name=name, resid=resid)


def _mm_nt(a, b, out_dtype, name, tm=1024, tn=1024, tk=1024):
    m, kk = a.shape
    n = b.shape[0]
    tm, tn, tk = _tile(m, tm, 8), _tile(n, tn), _tile(kk, tk)
    return _matmul(a, b, mode="nt", grid=(m // tm, n // tn, kk // tk),
                   a_block=(tm, tk), a_map=lambda i, j, k: (i, k),
                   b_block=(tn, tk), b_map=lambda i, j, k: (j, k),
                   o_block=(tm, tn), o_map=lambda i, j, k: (i, j),
                   out_shape=jax.ShapeDtypeStruct((m, n), out_dtype), name=name)


def _mm_tn(a, b, name, tm=1024, tn=1024, tk=1024):
    kk, m = a.shape
    n = b.shape[1]
    tm, tn, tk = _tile(m, tm), _tile(n, tn), _tile(kk, tk, 8)
    return _matmul(a, b, mode="tn", grid=(m // tm, n // tn, kk // tk),
                   a_block=(tk, tm), a_map=lambda i, j, k: (k, i),
                   b_block=(tk, tn), b_map=lambda i, j, k: (k, j),
                   o_block=(tm, tn), o_map=lambda i, j, k: (i, j),
                   out_shape=jax.ShapeDtypeStruct((m, n), F32), name=name)


def _mm_up(h, w_up, tm=2048, tn=256):
    s = h.shape[0]
    tm = _tile(s, tm, 8)
    nh = D_FF // tn
    return _matmul(h, w_up, mode="nn", grid=(s // tm, 2 * nh, 1),
                   a_block=(tm, D_MODEL), a_map=lambda i, j, k: (i, 0),
                   b_block=(D_MODEL, tn), b_map=lambda i, j, k: (0, j),
                   o_block=(None, tm, tn), o_map=lambda i, j, k: (j // nh, i, j % nh),
                   out_shape=jax.ShapeDtypeStruct((2, s, D_FF), F32), name="up_proj")


def _mm_dup_nt(dup, w_up, tm=1024, tk=1408):
    s = dup.shape[1]
    tm = _tile(s, tm, 8)
    nh = D_FF // tk
    return _matmul(dup, w_up, mode="nt", grid=(s // tm, 1, 2 * nh),
                   a_block=(None, tm, tk), a_map=lambda i, j, k: (k // nh, i, k % nh),
                   b_block=(D_MODEL, tk), b_map=lambda i, j, k: (0, k),
                   o_block=(tm, D_MODEL), o_map=lambda i, j, k: (i, 0),
                   out_shape=jax.ShapeDtypeStruct((s, D_MODEL), F32), name="d_h2")


def _mm_dwup_tn(h, dup, tn=1408, tk=1024):
    s = h.shape[0]
    tk = _tile(s, tk, 8)
    nh = D_FF // tn
    return _matmul(h, dup, mode="tn", grid=(1, 2 * nh, s // tk),
                   a_block=(tk, D_MODEL), a_map=lambda i, j, k: (k, 0),
                   b_block=(None, tk, tn), b_map=lambda i, j, k: (j // nh, k, j % nh),
                   o_block=(D_MODEL, tn), o_map=lambda i, j, k: (0, j),
                   out_shape=jax.ShapeDtypeStruct((D_MODEL, 2 * D_FF), F32), name="d_w_up")


def _rms_fwd(x, g, tr=256):
    s, d = x.shape
    tr = _tile(s, tr, 8)

    def body(x_ref, g_ref, o_ref):
        xv = x_ref[...]
        r = lax.rsqrt(jnp.mean(xv * xv, axis=-1, keepdims=True) + EPS)
        o_ref[...] = (xv * r * g_ref[...]).astype(o_ref.dtype)

    return pl.pallas_call(
        body, name="rms_fwd", grid=(s // tr,),
        in_specs=[pl.BlockSpec((tr, d), lambda i: (i, 0)), pl.BlockSpec((1, d), lambda i: (0, 0))],
        out_specs=pl.BlockSpec((tr, d), lambda i: (i, 0)),
        out_shape=jax.ShapeDtypeStruct((s, d), BF16),
        compiler_params=_cparams(("parallel",)),
    )(x, g)


def _group_rms_fwd(o_fox, o_sb, g_fox, g_sb, tr=256):
    s, d = o_fox.shape
    tr = _tile(s, tr, 8)

    def body(a_ref, b_ref, ga_ref, gb_ref, o_ref):
        for src, g_ref, lo in ((a_ref, ga_ref, 0), (b_ref, gb_ref, d)):
            xv = src[...]
            r = lax.rsqrt(jnp.mean(xv * xv, axis=-1, keepdims=True) + EPS)
            o_ref[:, lo:lo + d] = (xv * r * g_ref[...]).astype(o_ref.dtype)

    row = pl.BlockSpec((tr, d), lambda i: (i, 0))
    gain = pl.BlockSpec((1, d), lambda i: (0, 0))
    return pl.pallas_call(
        body, name="group_rms_fwd", grid=(s // tr,),
        in_specs=[row, row, gain, gain],
        out_specs=pl.BlockSpec((tr, 2 * d), lambda i: (i, 0)),
        out_shape=jax.ShapeDtypeStruct((s, 2 * d), BF16),
        compiler_params=_cparams(("parallel",)),
    )(o_fox, o_sb, g_fox, g_sb)


def _rms_bwd(x, dy, g, resid, *, dy_col, name, want_bf16, tr=256):
    s, d = x.shape
    tr = _tile(s, tr, 8)
    has_resid = resid is not None

    def body(*refs):
        refs = list(refs)
        x_ref, dy_ref, g_ref = refs[:3]
        r_ref = refs[3] if has_resid else None
        outs = refs[4:] if has_resid else refs[3:]
        dx_ref = outs[0]
        dxb_ref = outs[1] if want_bf16 else None
        dg_ref = outs[-1]

        @pl.when(pl.program_id(0) == 0)
        def _():
            dg_ref[...] = jnp.zeros_like(dg_ref)

        xv = x_ref[...]
        dyv = dy_ref[...]
        r = lax.rsqrt(jnp.mean(xv * xv, axis=-1, keepdims=True) + EPS)
        xh = xv * r
        gy = dyv * g_ref[...]
        dx = r * (gy - xh * jnp.mean(xh * gy, axis=-1, keepdims=True))
        if r_ref is not None:
            dx = r_ref[...] + dx
        dx_ref[...] = dx
        if dxb_ref is not None:
            dxb_ref[...] = dx.astype(BF16)
        dg_ref[...] += jnp.sum(dyv * xh, axis=0, keepdims=True)

    row = pl.BlockSpec((tr, d), lambda i: (i, 0))
    in_specs = [row, pl.BlockSpec((tr, d), lambda i: (i, dy_col)), pl.BlockSpec((1, d), lambda i: (0, 0))]
    args = [x, dy, g]
    if has_resid:
        in_specs.append(row)
        args.append(resid)
    out_specs = [row]
    out_shape = [jax.ShapeDtypeStruct((s, d), F32)]
    if want_bf16:
        out_specs.append(row)
        out_shape.append(jax.ShapeDtypeStruct((s, d), BF16))
    out_specs.append(pl.BlockSpec((1, d), lambda i: (0, 0)))
    out_shape.append(jax.ShapeDtypeStruct((1, d), F32))
    return pl.pallas_call(
        body, name=name, grid=(s // tr,), in_specs=in_specs, out_specs=out_specs, out_shape=out_shape,
        compiler_params=_cparams(("arbitrary",)),
    )(*args)


def _loss_head(x2, target, g, tr=256):
    s, d = x2.shape
    tr = _tile(s, tr, 8)

    def body(x_ref, t_ref, g_ref, dx_ref, dxb_ref, dg_ref, loss_ref):
        @pl.when(pl.program_id(0) == 0)
        def _():
            dg_ref[...] = jnp.zeros_like(dg_ref)
            loss_ref[...] = jnp.zeros_like(loss_ref)

        xv = x_ref[...]
        gv = g_ref[...]
        r = lax.rsqrt(jnp.mean(xv * xv, axis=-1, keepdims=True) + EPS)
        xh = xv * r
        err = xh * gv - t_ref[...]
        loss_ref[...] += jnp.sum(jnp.mean(err * err, axis=-1, keepdims=True), axis=0, keepdims=True) * 0.5
        dyv = err * (1.0 / d)
        gy = dyv * gv
        dx = r * (gy - xh * jnp.mean(xh * gy, axis=-1, keepdims=True))
        dx_ref[...] = dx
        dxb_ref[...] = dx.astype(BF16)
        dg_ref[...] += jnp.sum(dyv * xh, axis=0, keepdims=True)

    row = pl.BlockSpec((tr, d), lambda i: (i, 0))
    return pl.pallas_call(
        body, name="loss_head", grid=(s // tr,),
        in_specs=[row, row, pl.BlockSpec((1, d), lambda i: (0, 0))],
        out_specs=[row, row, pl.BlockSpec((1, d), lambda i: (0, 0)), pl.BlockSpec((1, LANES), lambda i: (0, 0))],
        out_shape=[jax.ShapeDtypeStruct((s, d), F32), jax.ShapeDtypeStruct((s, d), BF16),
                   jax.ShapeDtypeStruct((1, d), F32), jax.ShapeDtypeStruct((1, LANES), F32)],
        compiler_params=_cparams(("arbitrary",)),
    )(x2, target, g)


def _conv_taps(cur, prev8, w, b, first):
    prev8 = jnp.where(first, 0.0, prev8)
    ext = jnp.concatenate([prev8, cur], axis=0)
    x1 = pltpu.roll(ext, 1, 0)[8:]
    x2 = pltpu.roll(ext, 2, 0)[8:]
    u = b + w[0:1] * x2
    u = u + w[1:2] * x1
    u = u + w[2:3] * cur
    return u, x1, x2


def _conv_gate_fwd(up, conv_w, conv_b, tm=512, tn=256):
    s = up.shape[1]
    tm = _tile(s, tm, 8)
    nrb = s // tm
    rb8 = tm // 8

    def body(g_ref, v_ref, gp_ref, vp_ref, wg_ref, wv_ref, bg_ref, bv_ref, o_ref):
        first = pl.program_id(1) == 0
        ug, _, _ = _conv_taps(g_ref[...], gp_ref[...], wg_ref[...], bg_ref[...], first)
        uv, _, _ = _conv_taps(v_ref[...], vp_ref[...], wv_ref[...], bv_ref[...], first)
        sg = 1.0 / (1.0 + jnp.exp(-ug))
        o_ref[...] = (ug * sg * uv).astype(o_ref.dtype)

    def cur(h):
        return pl.BlockSpec((None, tm, tn), lambda j, i: (h, i, j))

    def prev(h):
        return pl.BlockSpec((None, 8, tn), lambda j, i: (h, jnp.maximum(i * rb8 - 1, 0), j))

    def par(h, r):
        return pl.BlockSpec((None, r, tn), lambda j, i: (h, 0, j))

    return pl.pallas_call(
        body, name="conv_gate_fwd", grid=(D_FF // tn, nrb),
        in_specs=[cur(0), cur(1), prev(0), prev(1), par(0, 3), par(1, 3), par(0, 1), par(1, 1)],
        out_specs=pl.BlockSpec((tm, tn), lambda j, i: (i, j)),
        out_shape=jax.ShapeDtypeStruct((s, D_FF), BF16),
        compiler_params=_cparams(("parallel", "parallel")),
    )(up, up, up, up, conv_w, conv_w, conv_b, conv_b)


def _conv_gate_bwd(up, dact, conv_w, conv_b, tm=512, tn=256):
    s = up.shape[1]
    tm = _tile(s, tm, 8)
    nrb = s // tm
    rb8 = tm // 8

    def body(g_ref, v_ref, gp_ref, vp_ref, da_ref, wg_ref, wv_ref, bg_ref, bv_ref,
             dup_ref, dcw_ref, dcb_ref, carry_ref):
        i = pl.program_id(1)
        first = i == nrb - 1

        @pl.when(i == 0)
        def _():
            carry_ref[...] = jnp.zeros_like(carry_ref)
            dcw_ref[...] = jnp.zeros_like(dcw_ref)
            dcb_ref[...] = jnp.zeros_like(dcb_ref)

        curs = (g_ref[...], v_ref[...])
        ws = (wg_ref[...], wv_ref[...])
        ug, g1, g2 = _conv_taps(curs[0], gp_ref[...], ws[0], bg_ref[...], first)
        uv, v1, v2 = _conv_taps(curs[1], vp_ref[...], ws[1], bv_ref[...], first)
        sg = 1.0 / (1.0 + jnp.exp(-ug))
        da = da_ref[...].astype(F32)
        d_v = da * (ug * sg)
        d_g = da * uv * (sg * (1.0 + ug * (1.0 - sg)))
        for h, (du, x0, x1, x2) in enumerate(((d_g, curs[0], g1, g2), (d_v, curs[1], v1, v2))):
            dcb_ref[h] += jnp.sum(du, axis=0, keepdims=True)
            dcw_ref[h, 0:1, :] += jnp.sum(du * x2, axis=0, keepdims=True)
            dcw_ref[h, 1:2, :] += jnp.sum(du * x1, axis=0, keepdims=True)
            dcw_ref[h, 2:3, :] += jnp.sum(du * x0, axis=0, keepdims=True)
            ext = jnp.concatenate([du, carry_ref[h]], axis=0)
            n1 = pltpu.roll(ext, tm + 7, 0)[:tm]
            n2 = pltpu.roll(ext, tm + 6, 0)[:tm]
            w = ws[h]
            dup_ref[h] = (w[2:3] * du + w[1:2] * n1 + w[0:1] * n2).astype(dup_ref.dtype)
            carry_ref[h] = du[:8]

    def cur(h):
        return pl.BlockSpec((None, tm, tn), lambda j, i: (h, nrb - 1 - i, j))

    def prev(h):
        return pl.BlockSpec((None, 8, tn), lambda j, i: (h, jnp.maximum((nrb - 1 - i) * rb8 - 1, 0), j))

    def par(h, r):
        return pl.BlockSpec((None, r, tn), lambda j, i: (h, 0, j))

    return pl.pallas_call(
        body, name="conv_gate_bwd", grid=(D_FF // tn, nrb),
        in_specs=[cur(0), cur(1), prev(0), prev(1),
                  pl.BlockSpec((tm, tn), lambda j, i: (nrb - 1 - i, j)),
                  par(0, 3), par(1, 3), par(0, 1), par(1, 1)],
        out_specs=[pl.BlockSpec((2, tm, tn), lambda j, i: (0, nrb - 1 - i, j)),
                   pl.BlockSpec((2, 3, tn), lambda j, i: (0, 0, j)),
                   pl.BlockSpec((2, 1, tn), lambda j, i: (0, 0, j))],
        out_shape=[jax.ShapeDtypeStruct((2, s, D_FF), BF16),
                   jax.ShapeDtypeStruct((2, 3, D_FF), F32),
                   jax.ShapeDtypeStruct((2, 1, D_FF), F32)],
        scratch_shapes=[pltpu.VMEM((2, 8, tn), F32)],
        compiler_params=_cparams(("parallel", "arbitrary")),
    )(up, up, up, up, dact, conv_w, conv_w, conv_b, conv_b)


def _split_dot(x, tri, terms):
    piece = x.astype(BF16)
    out = jnp.dot(piece, tri, preferred_element_type=F32)
    rest = x
    for _ in range(terms - 1):
        rest = rest - piece.astype(F32)
        piece = rest.astype(BF16)
        out = out + jnp.dot(piece, tri, preferred_element_type=F32)
    return out


def _split_dot_rhs(tri, x, terms):
    piece = x.astype(BF16)
    out = jnp.dot(tri, piece, preferred_element_type=F32)
    rest = x
    for _ in range(terms - 1):
        rest = rest - piece.astype(F32)
        piece = rest.astype(BF16)
        out = out + jnp.dot(tri, piece, preferred_element_type=F32)
    return out


def _tri(n, kind):
    r = lax.broadcasted_iota(jnp.int32, (n, n), 0)
    c = lax.broadcasted_iota(jnp.int32, (n, n), 1)
    cond = {"le": r <= c, "ge": r >= c, "lt": r < c, "gt": r > c}[kind]
    return jnp.where(cond, 1.0, 0.0).astype(BF16)


def _log_sigmoid(x):
    return jnp.minimum(x, 0.0) - jnp.log(1.0 + jnp.exp(-jnp.abs(x)))


def _forget_fwd(f_logit, bias):
    h, r, _ = f_logit.shape

    def body(x_ref, b_ref, o_ref):
        lf = _log_sigmoid(x_ref[...] + b_ref[...])
        within = _split_dot(lf, _tri(LANES, "le"), 3)
        row_tot = jnp.broadcast_to(within[:, LANES - 1:LANES], (r, LANES))
        before = _split_dot_rhs(_tri(r, "gt"), row_tot, 3)
        o_ref[...] = within + before

    blk = pl.BlockSpec((None, r, LANES), lambda i: (i, 0, 0))
    return pl.pallas_call(
        body, name="forget_cumsum_fwd", grid=(h,),
        in_specs=[blk, pl.BlockSpec((None, 1, LANES), lambda i: (i, 0, 0))],
        out_specs=blk, out_shape=jax.ShapeDtypeStruct((h, r, LANES), F32),
        compiler_params=_cparams(("parallel",)),
    )(f_logit, bias)


def _forget_bwd(f_logit, bias, ksum, qsum):
    h, r, _ = f_logit.shape

    def body(x_ref, b_ref, k_ref, q_ref, dx_ref, db_ref):
        d_f = q_ref[...] - k_ref[...]
        within = _split_dot(d_f, _tri(LANES, "ge"), 3)
        row_tot = jnp.broadcast_to(within[:, 0:1], (r, LANES))
        after = _split_dot_rhs(_tri(r, "lt"), row_tot, 3)
        xv = x_ref[...] + b_ref[...]
        dx = (within + after) * jnp.exp(_log_sigmoid(-xv))
        dx_ref[...] = dx
        db_ref[...] = jnp.broadcast_to(jnp.sum(dx), (1, LANES))

    blk = pl.BlockSpec((None, r, LANES), lambda i: (i, 0, 0))
    one = pl.BlockSpec((None, 1, LANES), lambda i: (i, 0, 0))
    return pl.pallas_call(
        body, name="forget_cumsum_bwd", grid=(h,),
        in_specs=[blk, one, blk, blk], out_specs=[blk, one],
        out_shape=[jax.ShapeDtypeStruct((h, r, LANES), F32), jax.ShapeDtypeStruct((h, 1, LANES), F32)],
        compiler_params=_cparams(("parallel",)),
    )(f_logit, bias, ksum, qsum)


def _head_specs(s, tq):
    qblk = pl.BlockSpec((None, tq, HEAD_DIM), lambda h, i: (h, i, 0))
    full = pl.BlockSpec((None, s, HEAD_DIM), lambda h, i: (h, 0, 0))
    col = pl.BlockSpec((None, tq, 1), lambda h, i: (h, i, 0))
    return qblk, full, col


def _scaled(q_ref):
    return (q_ref[...].astype(F32) * Q_SCALE).astype(BF16)


_NT = (((1,), (1,)), ((), ()))
_TN = (((0,), (0,)), ((), ()))


def _fox_fwd(q, k, v_ones, f_col, f_row, tq, tk):
    h, s, _ = q.shape
    nk = s // tk
    assert tq == tk

    def body(q_ref, k_ref, v_ref, fc_ref, fr_ref, o_ref, lse_ref, m_ref, acc_ref, z0, z1):
        i = pl.program_id(1)
        qs = _scaled(q_ref)
        fq = fc_ref[...]
        m_ref[...] = jnp.full_like(m_ref, NEG_BIG)
        acc_ref[...] = jnp.zeros_like(acc_ref)

        def keys_of(j):
            return pl.ds(pl.multiple_of(j * tk, tk), tk)

        def logits(j):
            return lax.dot_general(qs, k_ref[keys_of(j), :], _NT, preferred_element_type=F32)

        def soft(j, raw, mask):
            sc = raw + fq - fr_ref[j]
            if mask is not None:
                sc = jnp.where(mask, sc, NEG_BIG)
            m_old = m_ref[...]
            m_new = jnp.maximum(m_old, jnp.max(sc, axis=-1, keepdims=True))
            p = jnp.exp(sc - m_new)
            acc_ref[...] = jnp.exp(m_old - m_new) * acc_ref[...] + jnp.dot(
                p.astype(BF16), v_ref[keys_of(j), :], preferred_element_type=F32)
            m_ref[...] = m_new

        @pl.when(i >= 2)
        def _():
            z0[...] = logits(0)

            def step(p, carry):
                j = 2 * p
                z1[...] = logits(j + 1)
                soft(j, z0[...], None)
                z0[...] = logits(jnp.minimum(j + 2, i - 1))
                soft(j + 1, z1[...], None)
                return carry

            lax.fori_loop(0, i // 2, step, 0)

        @pl.when(i % 2 == 1)
        def _():
            soft(i - 1, logits(i - 1), None)

        soft(i, logits(i), _band_mask(tq, tk, 0, 0, strict=False))
        l = acc_ref[:, HEAD_DIM:HEAD_DIM + 1]
        o_ref[...] = acc_ref[:, :HEAD_DIM] / l
        lse_ref[...] = m_ref[...] + jnp.log(l)

    qblk, full, colspec = _head_specs(s, tq)
    return pl.pallas_call(
        body, name="fox_fwd", grid=(h, s // tq),
        in_specs=[qblk, full, pl.BlockSpec((None, s, 2 * HEAD_DIM), lambda hh, i: (hh, 0, 0)), colspec,
                  pl.BlockSpec((None, nk, 1, tk), lambda hh, i: (hh, 0, 0, 0))],
        out_specs=[qblk, colspec],
        out_shape=[jax.ShapeDtypeStruct((h, s, HEAD_DIM), F32), jax.ShapeDtypeStruct((h, s, 1), F32)],
        scratch_shapes=[pltpu.VMEM((tq, 1), F32), pltpu.VMEM((tq, 2 * HEAD_DIM), F32),
                        pltpu.VMEM((tq, tk), F32), pltpu.VMEM((tq, tk), F32)],
        compiler_params=_cparams(("parallel", "parallel")),
    )(q, k, v_ones, f_col, f_row)


def _fox_bwd(q, k, v, f_col, f_row, o, lse, d_o, tq, tk):
    h, s, _ = q.shape
    nk = s // tk
    assert tq == tk

    def body(q_ref, k_ref, v_ref, fc_ref, fr_ref, o_ref, lse_ref, do_ref,
             dq_ref, dk_ref, dv_ref, ks_ref, qs_ref, dq_acc, qsum_acc, z0, z1, p0, p1):
        i = pl.program_id(1)

        @pl.when(i == 0)
        def _():
            dk_ref[...] = jnp.zeros_like(dk_ref)
            dv_ref[...] = jnp.zeros_like(dv_ref)
            ks_ref[...] = jnp.zeros_like(ks_ref)

        qs = _scaled(q_ref)
        fq = fc_ref[...]
        lse_v = lse_ref[...]
        dob = do_ref[...].astype(BF16)
        delta = jnp.sum(dob.astype(F32) * o_ref[...], axis=-1, keepdims=True)
        dq_acc[...] = jnp.zeros_like(dq_acc)
        qsum_acc[...] = jnp.zeros_like(qsum_acc)

        def keys_of(j):
            return pl.ds(pl.multiple_of(j * tk, tk), tk)

        def products(j):
            at = keys_of(j)
            return (lax.dot_general(qs, k_ref[at, :], _NT, preferred_element_type=F32),
                    lax.dot_general(dob, v_ref[at, :], _NT, preferred_element_type=F32))

        def grads(j, raw, dp, mask):
            at = keys_of(j)
            sc = raw + fq - fr_ref[j]
            if mask is not None:
                sc = jnp.where(mask, sc, NEG_BIG)
            p = jnp.exp(sc - lse_v)
            ds = p * (dp - delta)
            dsb = ds.astype(BF16)
            dq_acc[...] += jnp.dot(dsb, k_ref[at, :], preferred_element_type=F32)
            dk_ref[at, :] += lax.dot_general(dsb, qs, _TN, preferred_element_type=F32)
            dv_ref[at, :] += lax.dot_general(p.astype(BF16), dob, _TN, preferred_element_type=F32)
            ks_ref[j] += jnp.sum(ds.reshape(tq // 8, 8, tk), axis=0)
            qsum_acc[...] += jnp.sum(ds, axis=-1, keepdims=True)

        @pl.when(i >= 2)
        def _():
            z0[...], p0[...] = products(0)

            def step(pp, carry):
                j = 2 * pp
                z1[...], p1[...] = products(j + 1)
                grads(j, z0[...], p0[...], None)
                z0[...], p0[...] = products(jnp.minimum(j + 2, i - 1))
                grads(j + 1, z1[...], p1[...], None)
                return carry

            lax.fori_loop(0, i // 2, step, 0)

        @pl.when(i % 2 == 1)
        def _():
            grads(i - 1, *products(i - 1), None)

        grads(i, *products(i), _band_mask(tq, tk, 0, 0, strict=False))
        dq_ref[...] = dq_acc[...] * Q_SCALE
        qs_ref[...] = qsum_acc[...]

    qblk, full, colspec = _head_specs(s, tq)
    frow = pl.BlockSpec((None, nk, 1, tk), lambda hh, i: (hh, 0, 0, 0))
    big = pltpu.VMEM((tq, tk), F32)
    return pl.pallas_call(
        body, name="fox_bwd", grid=(h, s // tq),
        in_specs=[qblk, full, full, colspec, frow, qblk, colspec, qblk],
        out_specs=[qblk, full, full, pl.BlockSpec((None, nk, 8, tk), lambda hh, i: (hh, 0, 0, 0)), colspec],
        out_shape=[jax.ShapeDtypeStruct((h, s, HEAD_DIM), F32)] * 3
        + [jax.ShapeDtypeStruct((h, nk, 8, tk), F32), jax.ShapeDtypeStruct((h, s, 1), F32)],
        scratch_shapes=[pltpu.VMEM((tq, HEAD_DIM), F32), pltpu.VMEM((tq, 1), F32), big, big, big, big],
        compiler_params=_cparams(("parallel", "arbitrary")),
    )(q, k, v, f_col, f_row, o, lse, d_o)


SB_TERMS = 2
G_TERMS = 1
ROW_CHUNK = 256
LOG2E = 1.4426950408889634
LN2 = 0.6931471805599453


def _softplus2(z2):
    return jnp.maximum(z2, 0.0) + jnp.log2(1.0 + jnp.exp2(-jnp.abs(z2)))


def _band_mask(rows, cols, row0, col0, strict=True):
    r = row0 + lax.broadcasted_iota(jnp.int32, (rows, cols), 0)
    c = col0 + lax.broadcasted_iota(jnp.int32, (rows, cols), 1)
    return c < r if strict else c <= r


def _sb_fwd(q, k, v, tq, tk):
    h, s, _ = q.shape

    rc = min(ROW_CHUNK, tq)
    assert tq % tk == 0 and tq % rc == 0

    assert (tq // tk) % 2 == 0

    def body(q_ref, k_ref, v_ref, o_ref, tot_ref, acc_ref, run_ref, z0, z1, d0, d1, t0, t1):
        z_refs, d_refs, t_refs = (z0, z1), (d0, d1), (t0, t1)
        i = pl.program_id(1)
        qs = _scaled(q_ref)
        tri = _tri(tk, "ge")
        acc_ref[...] = jnp.zeros_like(acc_ref)
        run_ref[...] = jnp.zeros_like(run_ref)
        n_full = (i * tq) // tk

        def block(j, rel):
            at = pl.ds(pl.multiple_of(j * tk, tk), tk)
            ks = k_ref[at, :]
            vs = v_ref[at, :]
            run = run_ref[...]
            live = [r for r in range(tq // rc) if rel is None or rel * tk < (r + 1) * rc - 1]
            z2, inc, mask, outs = {}, {}, {}, {}

            def logits(r):
                rows = slice(r * rc, (r + 1) * rc)
                z2[r] = lax.dot_general(qs[rows], ks, _NT, preferred_element_type=F32) * LOG2E

            def sums(r):
                sp = _softplus2(z2[r])
                visible = rel is None or rel * tk + tk - 1 < r * rc
                mask[r] = None if visible else _band_mask(rc, tk, r * rc, rel * tk)
                if mask[r] is not None:
                    sp = jnp.where(mask[r], sp, 0.0)
                inc[r] = _split_dot(sp, tri, SB_TERMS)

            def weigh(r):
                rows = slice(r * rc, (r + 1) * rc)
                w = jnp.exp2(z2[r] - (run[rows] + inc[r]))
                if mask[r] is not None:
                    w = jnp.where(mask[r], w, 0.0)
                outs[r] = jnp.dot(w.astype(BF16), vs, preferred_element_type=F32)

            for t in range(len(live) + 2):
                if t < len(live):
                    logits(live[t])
                if 0 <= t - 1 < len(live):
                    sums(live[t - 1])
                if 0 <= t - 2 < len(live):
                    weigh(live[t - 2])
            for r in live:
                rows = slice(r * rc, (r + 1) * rc)
                acc_ref[rows, :] += outs[r]
                run_ref[rows, :] += inc[r][:, 0:1]

        for rel in reversed(range(tq // tk)):
            block(n_full + rel, rel)

        def keys_of(b):
            j = n_full - 1 - jnp.minimum(b, n_full - 1)
            return pl.ds(pl.multiple_of(j * tk, tk), tk)

        def logits(b, slot):
            z_refs[slot][...] = lax.dot_general(qs, k_ref[keys_of(b), :], _NT,
                                                preferred_element_type=F32) * LOG2E

        def sums(slot):
            z2 = z_refs[slot][...]
            inc = _split_dot(_softplus2(z2), tri, SB_TERMS)
            d_refs[slot][...] = z2 - inc
            t_refs[slot][...] = inc[:, 0:1]

        def weigh(b, slot):
            w = jnp.exp2(d_refs[slot][...] - run_ref[...])
            acc_ref[...] += jnp.dot(w.astype(BF16), v_ref[keys_of(b), :], preferred_element_type=F32)
            run_ref[...] += t_refs[slot][...]

        @pl.when(n_full > 0)
        def _():
            logits(0, 0)
            logits(1, 1)
            sums(0)

            def step(p, carry):
                b = 2 * p
                logits(b + 2, 0)
                sums(1)
                weigh(b, 0)
                logits(b + 3, 1)
                sums(0)
                weigh(b + 1, 1)
                return carry

            lax.fori_loop(0, n_full // 2, step, 0)

        o_ref[...] = acc_ref[...]
        tot_ref[...] = run_ref[...] * (-LN2)

    qblk, full, colspec = _head_specs(s, tq)
    return pl.pallas_call(
        body, name="sb_fwd", grid=(h, s // tq),
        in_specs=[qblk, full, full], out_specs=[qblk, colspec],
        out_shape=[jax.ShapeDtypeStruct((h, s, HEAD_DIM), F32), jax.ShapeDtypeStruct((h, s, 1), F32)],
        scratch_shapes=[pltpu.VMEM((tq, HEAD_DIM), F32), pltpu.VMEM((tq, 1), F32),
                        pltpu.VMEM((tq, tk), F32), pltpu.VMEM((tq, tk), F32),
                        pltpu.VMEM((tq, tk), F32), pltpu.VMEM((tq, tk), F32),
                        pltpu.VMEM((tq, 1), F32), pltpu.VMEM((tq, 1), F32)],
        compiler_params=_cparams(("parallel", "parallel")),
    )(q, k, v)


def _sb_bwd(q, k, v, tot, d_o, tq, tk):
    h, s, _ = q.shape

    assert tq % tk == 0 and (tq // tk) % 2 == 0

    def body(q_ref, k_ref, v_ref, tot_ref, do_ref, dq_ref, dk_ref, dv_ref, dq_acc, off_ref, grun_ref,
             z0, z1, p0, p1, u0, u1, b0, b1, t0, t1):
        z_refs, p_refs, u_refs, b_refs, t_refs = (z0, z1), (p0, p1), (u0, u1), (b0, b1), (t0, t1)
        i = pl.program_id(1)

        @pl.when(i == 0)
        def _():
            dk_ref[...] = jnp.zeros_like(dk_ref)
            dv_ref[...] = jnp.zeros_like(dv_ref)

        qs = _scaled(q_ref)
        dob = do_ref[...].astype(BF16)
        tri = _tri(tk, "le")
        dq_acc[...] = jnp.zeros_like(dq_acc)
        off_ref[...] = tot_ref[...] * LOG2E
        grun_ref[...] = jnp.zeros_like(grun_ref)
        n_full = (i * tq) // tk

        def keys_of(b):
            return pl.ds(pl.multiple_of(jnp.minimum(b, n_full - 1) * tk, tk), tk)

        def finish(at, w, g, ginc, beta, t_blk, mask):
            dz = g - beta * (grun_ref[...] + ginc)
            if mask is not None:
                dz = jnp.where(mask, dz, 0.0)
            dzb = dz.astype(BF16)
            dq_acc[...] += jnp.dot(dzb, k_ref[at, :], preferred_element_type=F32)
            dk_ref[at, :] += lax.dot_general(dzb, qs, _TN, preferred_element_type=F32)
            dv_ref[at, :] += lax.dot_general(w.astype(BF16), dob, _TN, preferred_element_type=F32)
            off_ref[...] += t_blk
            grun_ref[...] += ginc[:, tk - 1:tk]

        def logits(b, slot):
            z_refs[slot][...] = lax.dot_general(qs, k_ref[keys_of(b), :], _NT,
                                                preferred_element_type=F32) * LOG2E

        def sums(b, slot):
            z2 = z_refs[slot][...]
            sp = _softplus2(z2)
            lb2 = z2 - sp
            linc = _split_dot(sp, tri, SB_TERMS)
            p_refs[slot][...] = lax.dot_general(dob, v_ref[keys_of(b), :], _NT, preferred_element_type=F32)
            u_refs[slot][...] = lb2 + linc
            b_refs[slot][...] = jnp.exp2(lb2)
            t_refs[slot][...] = linc[:, tk - 1:tk]

        def weigh(slot):
            w = jnp.exp2(u_refs[slot][...] + off_ref[...])
            g = w * p_refs[slot][...]
            return w, g, _split_dot(g, tri, G_TERMS)

        @pl.when(n_full > 0)
        def _():
            logits(0, 0)
            logits(1, 1)
            sums(0, 0)

            def step(p, carry):
                for slot in (0, 1):
                    b = 2 * p + slot
                    w, g, ginc = weigh(slot)
                    logits(b + 2, slot)
                    sums(b + 1, 1 - slot)
                    finish(keys_of(b), w, g, ginc, b_refs[slot][...], t_refs[slot][...], None)
                return carry

            lax.fori_loop(0, n_full // 2, step, 0)

        for rel in range(tq // tk):
            at = pl.ds(pl.multiple_of((n_full + rel) * tk, tk), tk)
            mask = _band_mask(tq, tk, 0, rel * tk)
            z2 = lax.dot_general(qs, k_ref[at, :], _NT, preferred_element_type=F32) * LOG2E
            sp = _softplus2(z2)
            linc = _split_dot(jnp.where(mask, sp, 0.0), tri, SB_TERMS)
            w = jnp.where(mask, jnp.exp2(z2 - sp + linc + off_ref[...]), 0.0)
            g = w * lax.dot_general(dob, v_ref[at, :], _NT, preferred_element_type=F32)
            ginc = _split_dot(g, tri, G_TERMS)
            finish(at, w, g, ginc, jnp.exp2(z2 - sp), linc[:, tk - 1:tk], mask)

        dq_ref[...] = dq_acc[...] * Q_SCALE

    qblk, full, colspec = _head_specs(s, tq)
    big = pltpu.VMEM((tq, tk), F32)
    return pl.pallas_call(
        body, name="sb_bwd", grid=(h, s // tq),
        in_specs=[qblk, full, full, colspec, qblk], out_specs=[qblk, full, full],
        out_shape=[jax.ShapeDtypeStruct((h, s, HEAD_DIM), F32)] * 3,
        scratch_shapes=[pltpu.VMEM((tq, HEAD_DIM), F32), pltpu.VMEM((tq, 1), F32), pltpu.VMEM((tq, 1), F32)]
        + [big] * 8 + [pltpu.VMEM((tq, 1), F32)] * 2,
        compiler_params=_cparams(("parallel", "arbitrary")),
    )(q, k, v, tot, d_o)


def _sum_adamw(parts, w, m, v, name, tr=256):
    _, rows, lanes = parts.shape
    tr = _tile(rows, tr, 16)
    c_m = 1.0 - ADAM_B1 ** ADAM_STEP
    c_v = 1.0 - ADAM_B2 ** ADAM_STEP

    def body(p_ref, w_ref, m_ref, v_ref, g_ref, d_ref, nm_ref, nv_ref):
        g = p_ref[0].astype(F32)
        for j in range(1, N_DEV):
            g = g + p_ref[j].astype(F32)
        nm = ADAM_B1 * m_ref[...] + (1.0 - ADAM_B1) * g
        nv = ADAM_B2 * v_ref[...] + (1.0 - ADAM_B2) * (g * g)
        m_hat = nm / c_m
        v_hat = nv / c_v
        g_ref[...] = g
        d_ref[...] = -ADAM_LR * (m_hat / (jnp.sqrt(v_hat) + ADAM_EPS) + ADAM_WD * w_ref[...])
        nm_ref[...] = nm
        nv_ref[...] = nv

    blk = pl.BlockSpec((tr, lanes), lambda i: (i, 0))
    return pl.pallas_call(
        body, name=name, grid=(rows // tr,),
        in_specs=[pl.BlockSpec((N_DEV, tr, lanes), lambda i: (0, i, 0)), blk, blk, blk],
        out_specs=[blk] * 4, out_shape=[jax.ShapeDtypeStruct((rows, lanes), F32)] * 4,
        compiler_params=_cparams(("parallel",)),
    )(parts, w, m, v)


def _to_heads(t):
    s = t.shape[0]
    return t.reshape(s, N_GROUP_HEADS, HEAD_DIM).transpose(1, 0, 2)


def _from_heads(t):
    s = t.shape[1]
    return t.transpose(1, 0, 2).reshape(s, GROUP_W)


def kernel(x, attn_norm_g, w_in, forget_bias, fox_out_g, sb_out_g, w_out, ffn_norm_g, w_up, conv_w, conv_b, w_down, final_norm_g, loss_target, m_attn_norm_g, m_w_in, m_forget_bias, m_fox_out_g, m_sb_out_g, m_w_out, m_ffn_norm_g, m_w_up, m_conv_w, m_conv_b, m_w_down, m_final_norm_g, v_attn_norm_g, v_w_in, v_forget_bias, v_fox_out_g, v_sb_out_g, v_w_out, v_ffn_norm_g, v_w_up, v_conv_w, v_conv_b, v_w_down, v_final_norm_g):
    s = x.shape[1]
    xs = x[0]
    tq = min(ATTN_TQ, s)
    tk_fox = min(FOX_TK, s)
    tk_sb = min(SB_TK, s)
    in_shard, up_shard, out_shard, down_shard = IN_COLS // N_DEV, 2 * D_FF // N_DEV, D_MODEL // N_DEV, D_FF // N_DEV

    cw = conv_w[0]
    cw_hi = cw.astype(BF16)
    cw_lo = (cw - cw_hi.astype(F32)).astype(BF16)
    g_in, g_out, g_up, g_down, g_conv = _all_gather(
        [w_in[0].astype(BF16), w_out[0].astype(BF16), w_up[0].astype(BF16), w_down[0].astype(BF16),
         jnp.stack([cw_hi, cw_lo])])
    g_chi, g_clo = g_conv[:, 0], g_conv[:, 1]
    n_gate = QKV_W + N_GROUP_HEADS
    in_windows = _col_windows(N_DEV, in_shard, gap_at=n_gate, gap=GATE_PAD - N_GROUP_HEADS)
    up_windows = _col_windows(N_DEV, up_shard)
    w_in_p = _assemble_cols(g_in, IN_COLS_PAD, in_windows, "assemble_w_in")
    w_out_f = g_out.reshape(D_MODEL, D_MODEL)
    w_up_f = _assemble_cols(g_up, 2 * D_FF, up_windows, "assemble_w_up")
    w_down_f = g_down.reshape(D_FF, D_MODEL)
    conv_w_f = (g_chi.astype(F32) + g_clo.astype(F32)).transpose(1, 0, 2).reshape(3, 2 * D_FF)
    conv_w2 = conv_w_f.reshape(3, 2, D_FF).transpose(1, 0, 2)
    conv_b2 = conv_b.reshape(2, 1, D_FF)

    h1 = _rms_fwd(xs, attn_norm_g)
    proj = _mm_nn(h1, w_in_p, BF16, "in_proj", tn=640)
    fox_qkv = proj[:, :QKV_W]
    f_logit = _mm_nn(h1, w_in_p[:, QKV_W:QKV_W + GATE_PAD], F32, "gate_proj")[:, :N_GROUP_HEADS]
    sb_qkv = proj[:, QKV_W + GATE_PAD:]
    fq, fk, fv = (_to_heads(fox_qkv[:, j * GROUP_W:(j + 1) * GROUP_W]) for j in range(3))
    sq, sk, sv = (_to_heads(sb_qkv[:, j * GROUP_W:(j + 1) * GROUP_W]) for j in range(3))

    f_logit_h = f_logit.T.reshape(N_GROUP_HEADS, s // LANES, LANES)
    bias_h = jnp.broadcast_to(forget_bias.reshape(N_GROUP_HEADS, 1, 1), (N_GROUP_HEADS, 1, LANES))
    big_f = _forget_fwd(f_logit_h, bias_h)
    f_col = big_f.reshape(N_GROUP_HEADS, s, 1)
    f_row = big_f.reshape(N_GROUP_HEADS, s // tk_fox, 1, tk_fox)

    fv_ones = jnp.concatenate([fv, jnp.ones_like(fv)], axis=-1)
    o_fox_h, lse = _fox_fwd(fq, fk, fv_ones, f_col, f_row, tq, tk_fox)
    o_sb_h, sb_tot = _sb_fwd(sq, sk, sv, tq, tk_sb)
    o_fox = _from_heads(o_fox_h)
    o_sb = _from_heads(o_sb_h)
    o_n = _group_rms_fwd(o_fox, o_sb, fox_out_g, sb_out_g)
    x1 = _mm_nn(o_n, w_out_f, F32, "out_proj", resid=xs)
    h2 = _rms_fwd(x1, ffn_norm_g)
    up = _mm_up(h2, w_up_f)
    act = _conv_gate_fwd(up, conv_w2, conv_b2)
    x2 = _mm_nn(act, w_down_f, F32, "down_proj", resid=x1, tk=1408)

    d_x2, d_x2b, dg_final, loss_part = _loss_head(x2, loss_target[0], final_norm_g.reshape(1, D_MODEL))
    d_act = _mm_nt(d_x2b, w_down_f, BF16, "d_act", tn=1408)
    dw_down = _mm_tn(act, d_x2b, "d_w_down", tm=1408)
    d_up, dcw2, dcb2 = _conv_gate_bwd(up, d_act, conv_w2, conv_b2)
    d_h2 = _mm_dup_nt(d_up, w_up_f)
    dw_up = _mm_dwup_tn(h2, d_up)
    d_x1, d_x1b, dg_ffn = _rms_bwd(x1, d_h2, ffn_norm_g, d_x2, dy_col=0, name="ffn_norm_bwd", want_bf16=True)
    d_on = _mm_nt(d_x1b, w_out_f, F32, "d_o_normed")
    dw_out = _mm_tn(o_n, d_x1b, "d_w_out")
    d_o_fox, dg_fox = _rms_bwd(o_fox, d_on, fox_out_g, None, dy_col=0, name="fox_norm_bwd", want_bf16=False)
    d_o_sb, dg_sb = _rms_bwd(o_sb, d_on, sb_out_g, None, dy_col=1, name="sb_norm_bwd", want_bf16=False)

    dfq, dfk, dfv, ksum8, qsum = _fox_bwd(fq, fk, fv, f_col, f_row, o_fox_h, lse, _to_heads(d_o_fox), tq, tk_fox)
    dsq, dsk, dsv = _sb_bwd(sq, sk, sv, sb_tot, _to_heads(d_o_sb), tq, tk_sb)
    ksum = jnp.sum(ksum8, axis=2).reshape(N_GROUP_HEADS, s // LANES, LANES)
    d_f_logit_h, d_bias_h = _forget_bwd(f_logit_h, bias_h, ksum,
                                        qsum.reshape(N_GROUP_HEADS, s // LANES, LANES))
    d_f_logit = d_f_logit_h.reshape(N_GROUP_HEADS, s).T

    d_proj = jnp.concatenate(
        [_from_heads(t).astype(BF16) for t in (dfq, dfk, dfv)]
        + [d_f_logit.astype(BF16), jnp.zeros((s, GATE_PAD - N_GROUP_HEADS), BF16)]
        + [_from_heads(t).astype(BF16) for t in (dsq, dsk, dsv)], axis=1)
    d_h1 = _mm_nt(d_proj, w_in_p, F32, "d_h1", tk=640)
    dw_in_p = _mm_tn(h1, d_proj, "d_w_in", tn=640)
    grad_x, dg_attn = _rms_bwd(xs, d_h1, attn_norm_g, d_x1, dy_col=0, name="attn_norm_bwd", want_bf16=False)

    dconv_w = dcw2.transpose(1, 0, 2).reshape(3, 2 * D_FF)
    dconv_b = dcb2.reshape(1, 2 * D_FF)
    slabs = [_split_cols(dw_in_p, N_DEV, in_shard, in_windows, "split_d_w_in"),
             dw_out.astype(BF16).reshape(N_DEV, out_shard, D_MODEL),
             _split_cols(dw_up, N_DEV, up_shard, up_windows, "split_d_w_up"),
             dw_down.astype(BF16).reshape(N_DEV, down_shard, D_MODEL),
             dconv_w.astype(BF16).reshape(3, N_DEV, up_shard).transpose(1, 0, 2)]
    small_shapes = [(1, D_MODEL), (1, N_GROUP_HEADS), (1, GROUP_W), (1, GROUP_W), (1, D_MODEL),
                    (1, 2 * D_FF), (D_MODEL,), (1,)]
    spack = _pack([dg_attn, d_bias_h[:, 0, 0], dg_fox, dg_sb, dg_ffn, dconv_b, dg_final, loss_part[0, 0:1]],
                  SMALL_ROWS, F32)
    *grecv, srecv = _grad_exchange(slabs, spack)

    big = [_sum_adamw(g, w_[0], m_[0], v_[0], "adamw_" + tag)
           for g, w_, m_, v_, tag in zip(
               grecv, (w_in, w_out, w_up, w_down, conv_w), (m_w_in, m_w_out, m_w_up, m_w_down, m_conv_w),
               (v_w_in, v_w_out, v_w_up, v_w_down, v_conv_w), ("w_in", "w_out", "w_up", "w_down", "conv_w"))]

    def small_pack(a_attn, a_bias, a_fox, a_sb, a_ffn, a_cb, a_fin):
        return _pack([a_attn, a_bias, a_fox, a_sb, a_ffn, a_cb, a_fin, jnp.zeros((1,), F32)], SMALL_ROWS, F32)

    small = _sum_adamw(srecv, small_pack(attn_norm_g, forget_bias, fox_out_g, sb_out_g, ffn_norm_g, conv_b, final_norm_g),
                       small_pack(m_attn_norm_g, m_forget_bias, m_fox_out_g, m_sb_out_g, m_ffn_norm_g, m_conv_b, m_final_norm_g),
                       small_pack(v_attn_norm_g, v_forget_bias, v_fox_out_g, v_sb_out_g, v_ffn_norm_g, v_conv_b, v_final_norm_g),
                       "adamw_replicated", tr=SMALL_ROWS)

    outs = []
    loss = None
    for kind in range(4):
        b_in, b_out, b_up, b_down, b_conv = (res[kind] for res in big)
        s_attn, s_bias, s_fox, s_sb, s_ffn, s_cb, s_fin, s_loss = _unpack(small[kind], small_shapes)
        if kind == 0:
            loss = s_loss[0]
        outs += [s_attn, b_in[None], s_bias, s_fox, s_sb, b_out[None], s_ffn, b_up[None], b_conv[None], s_cb,
                 b_down[None], s_fin]
    return (loss, grad_x[None], *outs)
```

```python
import jax
import jax.numpy as jnp
from jax import lax
from jax.experimental import pallas as pl
from jax.experimental.pallas import tpu as pltpu

F32 = jnp.float32
BF16 = jnp.bfloat16

D_MODEL = 1024
HEAD_DIM = 64
N_GROUP_HEADS = 8
GROUP_W = N_GROUP_HEADS * HEAD_DIM
QKV_W = 3 * GROUP_W
IN_COLS = 2 * QKV_W + N_GROUP_HEADS
GATE_PAD = 128
IN_COLS_PAD = 2 * QKV_W + GATE_PAD
D_FF = 2816
N_DEV = 8
EPS = 1e-6
Q_SCALE = HEAD_DIM ** -0.5

ADAM_LR = 0.001
ADAM_B1 = 0.9
ADAM_B2 = 0.999
ADAM_EPS = 1e-08
ADAM_WD = 0.01
ADAM_STEP = 10

LANES = 128
SMALL_ROWS = 80
VMEM_LIMIT = 56 * 1024 * 1024
NEG_BIG = -1e30
ATTN_TQ = 512
FOX_TK = 512
SB_TK = 256
MESH = pl.DeviceIdType.MESH


def _cparams(sem=None, **kw):
    return pltpu.CompilerParams(dimension_semantics=sem, vmem_limit_bytes=VMEM_LIMIT, **kw)


def _tile(n, target, mult=LANES):
    if n <= target:
        return n
    t = (target // mult) * mult
    while t >= mult:
        if n % t == 0:
            return t
        t -= mult
    return n


def _seg_len(shape):
    n = 1
    for s in shape:
        n *= s
    return -(-n // LANES) * LANES


def _pack(arrs, rows, dtype):
    parts = []
    for a in arrs:
        f = a.reshape(-1).astype(dtype)
        parts.append(jnp.pad(f, (0, _seg_len(a.shape) - f.shape[0])))
    flat = jnp.concatenate(parts)
    flat = jnp.pad(flat, (0, rows * LANES - flat.shape[0]))
    return flat.reshape(rows, LANES)


def _unpack(p, shapes, lead=()):
    flat = p.reshape(lead + (-1,))
    out, off = [], 0
    for shp in shapes:
        n = 1
        for s in shp:
            n *= s
        out.append(flat[..., off:off + n].reshape(lead + tuple(shp)))
        off += _seg_len(shp)
    return out


def _my_pos():
    return lax.axis_index("x"), lax.axis_index("y"), lax.axis_index("c")


def _all_gather(blocks):
    n = len(blocks)

    def body(*refs):
        x_refs, out_refs = refs[:n], refs[n:2 * n]
        send_sems, recv_sems, local_sems = refs[2 * n:]
        x, y, c = _my_pos()
        me, sibling = (x, y, c), (x, y, 1 - c)
        chips = [(1 - x, y), (x, 1 - y), (1 - x, 1 - y)]

        def copy(a, k, blk, to, own=False):
            px, py, pc = blk
            slot = out_refs[a].at[4 * px + 2 * py + pc]
            return pltpu.make_async_remote_copy(
                src_ref=x_refs[a] if own else slot, dst_ref=slot,
                send_sem=send_sems.at[a, k], recv_sem=recv_sems.at[a, k],
                device_id=to, device_id_type=MESH)

        mine = [pltpu.make_async_copy(x_refs[a], out_refs[a].at[4 * x + 2 * y + c], local_sems.at[a])
                for a in range(n)]
        for cp in mine:
            cp.start()
        first = []
        for a in range(n):
            first.append(copy(a, 0, me, sibling, own=True))
            first += [copy(a, 1 + j, me, (*chip, c), own=True) for j, chip in enumerate(chips)]
        for cp in first:
            cp.start()
        passed = []
        for j, chip in enumerate(chips):
            for a in range(n):
                copy(a, 1 + j, (*chip, c), me).wait_recv()
                passed.append(copy(a, 4 + j, (*chip, c), sibling))
                passed[-1].start()
        for a in range(n):
            copy(a, 0, sibling, me).wait_recv()
            for j, chip in enumerate(chips):
                copy(a, 4 + j, (*chip, 1 - c), me).wait_recv()
        for cp in first + passed:
            cp.wait_send()
        for cp in mine:
            cp.wait()

    hbm = pl.BlockSpec(memory_space=pl.ANY)
    return pl.pallas_call(
        body, name="weights_all_gather",
        out_shape=[jax.ShapeDtypeStruct((N_DEV,) + b.shape, b.dtype) for b in blocks],
        in_specs=[hbm] * n, out_specs=[hbm] * n,
        scratch_shapes=[pltpu.SemaphoreType.DMA((n, 7)), pltpu.SemaphoreType.DMA((n, 7)),
                        pltpu.SemaphoreType.DMA((n,))],
    )(*blocks)


def _grad_exchange(slabs, spack):
    n = len(slabs) + 1

    def body(*refs):
        in_refs, out_refs = refs[:n], refs[n:2 * n]
        send_sems, recv_sems, local_sems = refs[2 * n:]
        x, y, c = _my_pos()
        my_id = 4 * x + 2 * y + c

        def src_of(a, dev):
            return in_refs[a] if a == n - 1 else in_refs[a].at[dev]

        own = [pltpu.make_async_copy(src_of(a, my_id), out_refs[a].at[my_id], local_sems.at[a])
               for a in range(n)]
        for cp in own:
            cp.start()
        sends, arrivals = [], []
        for k in range(1, N_DEV):
            px, py, pc = x ^ (k >> 2), y ^ ((k >> 1) & 1), c ^ (k & 1)
            peer_id = 4 * px + 2 * py + pc
            for a in range(n):
                for dst_slot, bucket in ((my_id, sends), (peer_id, arrivals)):
                    bucket.append(pltpu.make_async_remote_copy(
                        src_ref=src_of(a, peer_id), dst_ref=out_refs[a].at[dst_slot],
                        send_sem=send_sems.at[a, k - 1], recv_sem=recv_sems.at[a, k - 1],
                        device_id=(px, py, pc), device_id_type=MESH))
        for cp in sends:
            cp.start()
        for cp in arrivals:
            cp.wait_recv()
        for cp in sends:
            cp.wait_send()
        for cp in own:
            cp.wait()

    hbm = pl.BlockSpec(memory_space=pl.ANY)
    return pl.pallas_call(
        body, name="grad_exchange",
        out_shape=[jax.ShapeDtypeStruct(g.shape, g.dtype) for g in slabs]
        + [jax.ShapeDtypeStruct((N_DEV,) + spack.shape, spack.dtype)],
        in_specs=[hbm] * n, out_specs=[hbm] * n,
        scratch_shapes=[pltpu.SemaphoreType.DMA((n, 7)), pltpu.SemaphoreType.DMA((n, 7)),
                        pltpu.SemaphoreType.DMA((n,))],
    )(*slabs, spack)


def _col_windows(n_shards, width, gap_at=None, gap=0):
    out = []
    for j in range(n_shards):
        g0, g1 = j * width, (j + 1) * width
        cuts = [g0, g1] if gap_at is None or not g0 < gap_at < g1 else [g0, gap_at, g1]
        for a, b in zip(cuts[:-1], cuts[1:]):
            out.append((j, a - g0, b - g0, a + (gap if gap_at is not None and a >= gap_at else 0)))
    return out


def _assemble_cols(parts, total, windows, name, tr=256):
    n, rows, w = parts.shape
    tr = _tile(rows, tr, 16)

    def body(p_ref, o_ref):
        o_ref[...] = jnp.zeros_like(o_ref)
        for j, lo, hi, dst in windows:
            o_ref[:, dst:dst + hi - lo] = p_ref[j, :, lo:hi]

    return pl.pallas_call(
        body, name=name, grid=(rows // tr,),
        in_specs=[pl.BlockSpec((n, tr, w), lambda i: (0, i, 0))],
        out_specs=pl.BlockSpec((tr, total), lambda i: (i, 0)),
        out_shape=jax.ShapeDtypeStruct((rows, total), parts.dtype),
        compiler_params=_cparams(("parallel",)),
    )(parts)


def _split_cols(full, n, w, windows, name, tr=256):
    rows, total = full.shape
    tr = _tile(rows, tr, 16)

    def body(f_ref, o_ref):
        for j, lo, hi, dst in windows:
            o_ref[j, :, lo:hi] = f_ref[:, dst:dst + hi - lo].astype(o_ref.dtype)

    return pl.pallas_call(
        body, name=name, grid=(rows // tr,),
        in_specs=[pl.BlockSpec((tr, total), lambda i: (i, 0))],
        out_specs=pl.BlockSpec((n, tr, w), lambda i: (0, i, 0)),
        out_shape=jax.ShapeDtypeStruct((n, rows, w), BF16),
        compiler_params=_cparams(("parallel",)),
    )(full)


_DIMS = {"nn": (((1,), (0,)), ((), ())), "nt": (((1,), (1,)), ((), ())), "tn": (((0,), (0,)), ((), ()))}


def _matmul(a, b, *, mode, grid, a_block, a_map, b_block, b_map, o_block, o_map, out_shape, name,
            resid=None):
    nk = grid[2]
    dims = _DIMS[mode]

    def body(*refs):
        if resid is None:
            a_ref, b_ref, o_ref, acc_ref = refs
            r_ref = None
        else:
            a_ref, b_ref, r_ref, o_ref, acc_ref = refs
        k = pl.program_id(2)

        @pl.when(k == 0)
        def _():
            acc_ref[...] = jnp.zeros_like(acc_ref)

        acc_ref[...] += lax.dot_general(a_ref[...], b_ref[...], dims, preferred_element_type=F32)

        @pl.when(k == nk - 1)
        def _():
            res = acc_ref[...]
            if r_ref is not None:
                res = r_ref[...] + res
            o_ref[...] = res.astype(o_ref.dtype)

    in_specs = [pl.BlockSpec(a_block, a_map), pl.BlockSpec(b_block, b_map)]
    args = [a, b]
    if resid is not None:
        in_specs.append(pl.BlockSpec(o_block, o_map))
        args.append(resid)
    acc_shape = tuple(d for d in o_block if d is not None)
    return pl.pallas_call(
        body, name=name, grid=grid, in_specs=in_specs,
        out_specs=pl.BlockSpec(o_block, o_map), out_shape=out_shape,
        scratch_shapes=[pltpu.VMEM(acc_shape, F32)],
        compiler_params=_cparams(("parallel", "parallel", "arbitrary")),
    )(*args)


def _mm_nn(a, b, out_dtype, name, resid=None, tm=1024, tn=1024, tk=1024):
    m, kk = a.shape
    n = b.shape[1]
    tm, tn, tk = _tile(m, tm, 8), _tile(n, tn), _tile(kk, tk)
    return _matmul(a, b, mode="nn", grid=(m // tm, n // tn, kk // tk),
                   a_block=(tm, tk), a_map=lambda i, j, k: (i, k),
                   b_block=(tk, tn), b_map=lambda i, j, k: (k, j),
                   o_block=(tm, tn), o_map=lambda i, j, k: (i, j),
                   out_shape=jax.ShapeDtypeStruct((m, n), out_dtype), name=name, resid=resid)


def _mm_nt(a, b, out_dtype, name, tm=1024, tn=1024, tk=1024):
    m, kk = a.shape
    n = b.shape[0]
    tm, tn, tk = _tile(m, tm, 8), _tile(n, tn), _tile(kk, tk)
    return _matmul(a, b, mode="nt", grid=(m // tm, n // tn, kk // tk),
                   a_block=(tm, tk), a_map=lambda i, j, k: (i, k),
                   b_block=(tn, tk), b_map=lambda i, j, k: (j, k),
                   o_block=(tm, tn), o_map=lambda i, j, k: (i, j),
                   out_shape=jax.ShapeDtypeStruct((m, n), out_dtype), name=name)


def _mm_tn(a, b, name, tm=1024, tn=1024, tk=1024):
    kk, m = a.shape
    n = b.shape[1]
    tm, tn, tk = _tile(m, tm), _tile(n, tn), _tile(kk, tk, 8)
    return _matmul(a, b, mode="tn", grid=(m // tm, n // tn, kk // tk),
                   a_block=(tk, tm), a_map=lambda i, j, k: (k, i),
                   b_block=(tk, tn), b_map=lambda i, j, k: (k, j),
                   o_block=(tm, tn), o_map=lambda i, j, k: (i, j),
                   out_shape=jax.ShapeDtypeStruct((m, n), F32), name=name)


def _mm_heads(a, b, name, tm=1024, tn=640):
    m, kk = a.shape
    n = b.shape[1]
    tm, tn = _tile(m, tm, 16), _tile(n, tn)
    per_tile = tn // HEAD_DIM

    def body(a_ref, b_ref, o_ref):
        res = jnp.dot(a_ref[...], b_ref[...], preferred_element_type=F32)
        for hh in range(per_tile):
            o_ref[hh] = res[:, hh * HEAD_DIM:(hh + 1) * HEAD_DIM].astype(o_ref.dtype)

    return pl.pallas_call(
        body, name=name, grid=(m // tm, n // tn),
        in_specs=[pl.BlockSpec((tm, kk), lambda i, j: (i, 0)), pl.BlockSpec((kk, tn), lambda i, j: (0, j))],
        out_specs=pl.BlockSpec((per_tile, tm, HEAD_DIM), lambda i, j: (j, i, 0)),
        out_shape=jax.ShapeDtypeStruct((n // HEAD_DIM, m, HEAD_DIM), BF16),
        compiler_params=_cparams(("parallel", "parallel")),
    )(a, b)


def _mm_up(h, w_up, tm=2048, tn=256):
    s = h.shape[0]
    tm = _tile(s, tm, 8)
    nh = D_FF // tn
    return _matmul(h, w_up, mode="nn", grid=(s // tm, 2 * nh, 1),
                   a_block=(tm, D_MODEL), a_map=lambda i, j, k: (i, 0),
                   b_block=(D_MODEL, tn), b_map=lambda i, j, k: (0, j),
                   o_block=(None, tm, tn), o_map=lambda i, j, k: (j // nh, i, j % nh),
                   out_shape=jax.ShapeDtypeStruct((2, s, D_FF), F32), name="up_proj")


def _mm_dup_nt(dup, w_up, tm=1024, tk=1408):
    s = dup.shape[1]
    tm = _tile(s, tm, 8)
    nh = D_FF // tk
    return _matmul(dup, w_up, mode="nt", grid=(s // tm, 1, 2 * nh),
                   a_block=(None, tm, tk), a_map=lambda i, j, k: (k // nh, i, k % nh),
                   b_block=(D_MODEL, tk), b_map=lambda i, j, k: (0, k),
                   o_block=(tm, D_MODEL), o_map=lambda i, j, k: (i, 0),
                   out_shape=jax.ShapeDtypeStruct((s, D_MODEL), F32), name="d_h2")


def _mm_dwup_tn(h, dup, tn=1408, tk=1024):
    s = h.shape[0]
    tk = _tile(s, tk, 8)
    nh = D_FF // tn
    return _matmul(h, dup, mode="tn", grid=(1, 2 * nh, s // tk),
                   a_block=(tk, D_MODEL), a_map=lambda i, j, k: (k, 0),
                   b_block=(None, tk, tn), b_map=lambda i, j, k: (j // nh, k, j % nh),
                   o_block=(D_MODEL, tn), o_map=lambda i, j, k: (0, j),
                   out_shape=jax.ShapeDtypeStruct((D_MODEL, 2 * D_FF), F32), name="d_w_up")


def _rms_fwd(x, g, tr=256):
    s, d = x.shape
    tr = _tile(s, tr, 8)

    def body(x_ref, g_ref, o_ref):
        xv = x_ref[...]
        r = lax.rsqrt(jnp.mean(xv * xv, axis=-1, keepdims=True) + EPS)
        o_ref[...] = (xv * r * g_ref[...]).astype(o_ref.dtype)

    return pl.pallas_call(
        body, name="rms_fwd", grid=(s // tr,),
        in_specs=[pl.BlockSpec((tr, d), lambda i: (i, 0)), pl.BlockSpec((1, d), lambda i: (0, 0))],
        out_specs=pl.BlockSpec((tr, d), lambda i: (i, 0)),
        out_shape=jax.ShapeDtypeStruct((s, d), BF16),
        compiler_params=_cparams(("parallel",)),
    )(x, g)


def _group_rms_fwd(o_fox, o_sb, g_fox, g_sb, tr=256):
    nh, s, dh = o_fox.shape
    tr = _tile(s, tr, 8)

    def body(a_ref, b_ref, ga_ref, gb_ref, o_ref):
        for src, g_ref, lo in ((a_ref, ga_ref, 0), (b_ref, gb_ref, nh * dh)):
            heads = [src[hh] for hh in range(nh)]
            ss = heads[0] * heads[0]
            for xv in heads[1:]:
                ss = ss + xv * xv
            r = lax.rsqrt(jnp.sum(ss, axis=-1, keepdims=True) * (1.0 / (nh * dh)) + EPS)
            for hh, xv in enumerate(heads):
                o_ref[:, lo + hh * dh:lo + (hh + 1) * dh] = (xv * r * g_ref[hh]).astype(o_ref.dtype)

    heads_blk = pl.BlockSpec((nh, tr, dh), lambda i: (0, i, 0))
    gain = pl.BlockSpec((nh, 1, dh), lambda i: (0, 0, 0))
    return pl.pallas_call(
        body, name="group_rms_fwd", grid=(s // tr,),
        in_specs=[heads_blk, heads_blk, gain, gain],
        out_specs=pl.BlockSpec((tr, 2 * nh * dh), lambda i: (i, 0)),
        out_shape=jax.ShapeDtypeStruct((s, 2 * nh * dh), BF16),
        compiler_params=_cparams(("parallel",)),
    )(o_fox, o_sb, g_fox, g_sb)


def _group_rms_bwd(x, dy, g, *, dy_col, name, tr=256):
    nh, s, dh = x.shape
    tr = _tile(s, tr, 8)
    d = nh * dh

    def body(x_ref, dy_ref, g_ref, dx_ref, dg_ref):
        @pl.when(pl.program_id(0) == 0)
        def _():
            dg_ref[...] = jnp.zeros_like(dg_ref)

        dyv = dy_ref[...]
        xs_ = [x_ref[hh] for hh in range(nh)]
        dys = [dyv[:, hh * dh:(hh + 1) * dh] for hh in range(nh)]
        ss = xs_[0] * xs_[0]
        for xv in xs_[1:]:
            ss = ss + xv * xv
        r = lax.rsqrt(jnp.sum(ss, axis=-1, keepdims=True) * (1.0 / d) + EPS)
        xh = [xv * r for xv in xs_]
        gy = [dys[hh] * g_ref[hh] for hh in range(nh)]
        dot = xh[0] * gy[0]
        for hh in range(1, nh):
            dot = dot + xh[hh] * gy[hh]
        mean_dot = jnp.sum(dot, axis=-1, keepdims=True) * (1.0 / d)
        for hh in range(nh):
            dx_ref[hh] = r * (gy[hh] - xh[hh] * mean_dot)
            dg_ref[hh] += jnp.sum(dys[hh] * xh[hh], axis=0, keepdims=True)

    heads_blk = pl.BlockSpec((nh, tr, dh), lambda i: (0, i, 0))
    gain = pl.BlockSpec((nh, 1, dh), lambda i: (0, 0, 0))
    return pl.pallas_call(
        body, name=name, grid=(s // tr,),
        in_specs=[heads_blk, pl.BlockSpec((tr, d), lambda i: (i, dy_col)), gain],
        out_specs=[heads_blk, gain],
        out_shape=[jax.ShapeDtypeStruct((nh, s, dh), F32), jax.ShapeDtypeStruct((nh, 1, dh), F32)],
        compiler_params=_cparams(("arbitrary",)),
    )(x, dy, g)


def _merge_dproj(parts_fox, d_gate, parts_sb, tr=256):
    nh, s, dh = parts_fox[0].shape
    tr = _tile(s, tr, 16)

    def body(*refs):
        o_ref = refs[-1]
        gate_ref = refs[3]
        col = 0
        for ref in refs[:3]:
            for hh in range(nh):
                o_ref[:, col:col + dh] = ref[hh].astype(o_ref.dtype)
                col += dh
        o_ref[:, col:col + GATE_PAD] = jnp.zeros((tr, GATE_PAD), o_ref.dtype)
        o_ref[:, col:col + N_GROUP_HEADS] = gate_ref[...].astype(o_ref.dtype)
        col += GATE_PAD
        for ref in refs[4:7]:
            for hh in range(nh):
                o_ref[:, col:col + dh] = ref[hh].astype(o_ref.dtype)
                col += dh

    heads_blk = pl.BlockSpec((nh, tr, dh), lambda i: (0, i, 0))
    return pl.pallas_call(
        body, name="merge_d_proj", grid=(s // tr,),
        in_specs=[heads_blk] * 3 + [pl.BlockSpec((tr, N_GROUP_HEADS), lambda i: (i, 0))] + [heads_blk] * 3,
        out_specs=pl.BlockSpec((tr, IN_COLS_PAD), lambda i: (i, 0)),
        out_shape=jax.ShapeDtypeStruct((s, IN_COLS_PAD), BF16),
        compiler_params=_cparams(("parallel",)),
    )(*parts_fox, d_gate, *parts_sb)


def _rms_bwd(x, dy, g, resid, *, dy_col, name, want_bf16, tr=256):
    s, d = x.shape
    tr = _tile(s, tr, 8)
    has_resid = resid is not None

    def body(*refs):
        refs = list(refs)
        x_ref, dy_ref, g_ref = refs[:3]
        r_ref = refs[3] if has_resid else None
        outs = refs[4:] if has_resid else refs[3:]
        dx_ref = outs[0]
        dxb_ref = outs[1] if want_bf16 else None
        dg_ref = outs[-1]

        @pl.when(pl.program_id(0) == 0)
        def _():
            dg_ref[...] = jnp.zeros_like(dg_ref)

        xv = x_ref[...]
        dyv = dy_ref[...]
        r = lax.rsqrt(jnp.mean(xv * xv, axis=-1, keepdims=True) + EPS)
        xh = xv * r
        gy = dyv * g_ref[...]
        dx = r * (gy - xh * jnp.mean(xh * gy, axis=-1, keepdims=True))
        if r_ref is not None:
            dx = r_ref[...] + dx
        dx_ref[...] = dx
        if dxb_ref is not None:
            dxb_ref[...] = dx.astype(BF16)
        dg_ref[...] += jnp.sum(dyv * xh, axis=0, keepdims=True)

    row = pl.BlockSpec((tr, d), lambda i: (i, 0))
    in_specs = [row, pl.BlockSpec((tr, d), lambda i: (i, dy_col)), pl.BlockSpec((1, d), lambda i: (0, 0))]
    args = [x, dy, g]
    if has_resid:
        in_specs.append(row)
        args.append(resid)
    out_specs = [row]
    out_shape = [jax.ShapeDtypeStruct((s, d), F32)]
    if want_bf16:
        out_specs.append(row)
        out_shape.append(jax.ShapeDtypeStruct((s, d), BF16))
    out_specs.append(pl.BlockSpec((1, d), lambda i: (0, 0)))
    out_shape.append(jax.ShapeDtypeStruct((1, d), F32))
    return pl.pallas_call(
        body, name=name, grid=(s // tr,), in_specs=in_specs, out_specs=out_specs, out_shape=out_shape,
        compiler_params=_cparams(("arbitrary",)),
    )(*args)


def _loss_head(x2, target, g, tr=256):
    s, d = x2.shape
    tr = _tile(s, tr, 8)

    def body(x_ref, t_ref, g_ref, dx_ref, dxb_ref, dg_ref, loss_ref):
        @pl.when(pl.program_id(0) == 0)
        def _():
            dg_ref[...] = jnp.zeros_like(dg_ref)
            loss_ref[...] = jnp.zeros_like(loss_ref)

        xv = x_ref[...]
        gv = g_ref[...]
        r = lax.rsqrt(jnp.mean(xv * xv, axis=-1, keepdims=True) + EPS)
        xh = xv * r
        err = xh * gv - t_ref[...]
        loss_ref[...] += jnp.sum(jnp.mean(err * err, axis=-1, keepdims=True), axis=0, keepdims=True) * 0.5
        dyv = err * (1.0 / d)
        gy = dyv * gv
        dx = r * (gy - xh * jnp.mean(xh * gy, axis=-1, keepdims=True))
        dx_ref[...] = dx
        dxb_ref[...] = dx.astype(BF16)
        dg_ref[...] += jnp.sum(dyv * xh, axis=0, keepdims=True)

    row = pl.BlockSpec((tr, d), lambda i: (i, 0))
    return pl.pallas_call(
        body, name="loss_head", grid=(s // tr,),
        in_specs=[row, row, pl.BlockSpec((1, d), lambda i: (0, 0))],
        out_specs=[row, row, pl.BlockSpec((1, d), lambda i: (0, 0)), pl.BlockSpec((1, LANES), lambda i: (0, 0))],
        out_shape=[jax.ShapeDtypeStruct((s, d), F32), jax.ShapeDtypeStruct((s, d), BF16),
                   jax.ShapeDtypeStruct((1, d), F32), jax.ShapeDtypeStruct((1, LANES), F32)],
        compiler_params=_cparams(("arbitrary",)),
    )(x2, target, g)


def _conv_taps(cur, prev8, w, b, first):
    prev8 = jnp.where(first, 0.0, prev8)
    ext = jnp.concatenate([prev8, cur], axis=0)
    x1 = pltpu.roll(ext, 1, 0)[8:]
    x2 = pltpu.roll(ext, 2, 0)[8:]
    u = b + w[0:1] * x2
    u = u + w[1:2] * x1
    u = u + w[2:3] * cur
    return u, x1, x2


def _conv_gate_fwd(up, conv_w, conv_b, tm=512, tn=256):
    s = up.shape[1]
    tm = _tile(s, tm, 8)
    nrb = s // tm
    rb8 = tm // 8

    def body(g_ref, v_ref, gp_ref, vp_ref, wg_ref, wv_ref, bg_ref, bv_ref, o_ref):
        first = pl.program_id(1) == 0
        ug, _, _ = _conv_taps(g_ref[...], gp_ref[...], wg_ref[...], bg_ref[...], first)
        uv, _, _ = _conv_taps(v_ref[...], vp_ref[...], wv_ref[...], bv_ref[...], first)
        sg = 1.0 / (1.0 + jnp.exp(-ug))
        o_ref[...] = (ug * sg * uv).astype(o_ref.dtype)

    def cur(h):
        return pl.BlockSpec((None, tm, tn), lambda j, i: (h, i, j))

    def prev(h):
        return pl.BlockSpec((None, 8, tn), lambda j, i: (h, jnp.maximum(i * rb8 - 1, 0), j))

    def par(h, r):
        return pl.BlockSpec((None, r, tn), lambda j, i: (h, 0, j))

    return pl.pallas_call(
        body, name="conv_gate_fwd", grid=(D_FF // tn, nrb),
        in_specs=[cur(0), cur(1), prev(0), prev(1), par(0, 3), par(1, 3), par(0, 1), par(1, 1)],
        out_specs=pl.BlockSpec((tm, tn), lambda j, i: (i, j)),
        out_shape=jax.ShapeDtypeStruct((s, D_FF), BF16),
        compiler_params=_cparams(("parallel", "parallel")),
    )(up, up, up, up, conv_w, conv_w, conv_b, conv_b)


def _conv_gate_bwd(up, dact, conv_w, conv_b, tm=512, tn=256):
    s = up.shape[1]
    tm = _tile(s, tm, 8)
    nrb = s // tm
    rb8 = tm // 8

    def body(g_ref, v_ref, gp_ref, vp_ref, da_ref, wg_ref, wv_ref, bg_ref, bv_ref,
             dup_ref, dcw_ref, dcb_ref, carry_ref):
        i = pl.program_id(1)
        first = i == nrb - 1

        @pl.when(i == 0)
        def _():
            carry_ref[...] = jnp.zeros_like(carry_ref)
            dcw_ref[...] = jnp.zeros_like(dcw_ref)
            dcb_ref[...] = jnp.zeros_like(dcb_ref)

        curs = (g_ref[...], v_ref[...])
        ws = (wg_ref[...], wv_ref[...])
        ug, g1, g2 = _conv_taps(curs[0], gp_ref[...], ws[0], bg_ref[...], first)
        uv, v1, v2 = _conv_taps(curs[1], vp_ref[...], ws[1], bv_ref[...], first)
        sg = 1.0 / (1.0 + jnp.exp(-ug))
        da = da_ref[...].astype(F32)
        d_v = da * (ug * sg)
        d_g = da * uv * (sg * (1.0 + ug * (1.0 - sg)))
        for h, (du, x0, x1, x2) in enumerate(((d_g, curs[0], g1, g2), (d_v, curs[1], v1, v2))):
            dcb_ref[h] += jnp.sum(du, axis=0, keepdims=True)
            dcw_ref[h, 0:1, :] += jnp.sum(du * x2, axis=0, keepdims=True)
            dcw_ref[h, 1:2, :] += jnp.sum(du * x1, axis=0, keepdims=True)
            dcw_ref[h, 2:3, :] += jnp.sum(du * x0, axis=0, keepdims=True)
            ext = jnp.concatenate([du, carry_ref[h]], axis=0)
            n1 = pltpu.roll(ext, tm + 7, 0)[:tm]
            n2 = pltpu.roll(ext, tm + 6, 0)[:tm]
            w = ws[h]
            dup_ref[h] = (w[2:3] * du + w[1:2] * n1 + w[0:1] * n2).astype(dup_ref.dtype)
            carry_ref[h] = du[:8]

    def cur(h):
        return pl.BlockSpec((None, tm, tn), lambda j, i: (h, nrb - 1 - i, j))

    def prev(h):
        return pl.BlockSpec((None, 8, tn), lambda j, i: (h, jnp.maximum((nrb - 1 - i) * rb8 - 1, 0), j))

    def par(h, r):
        return pl.BlockSpec((None, r, tn), lambda j, i: (h, 0, j))

    return pl.pallas_call(
        body, name="conv_gate_bwd", grid=(D_FF // tn, nrb),
        in_specs=[cur(0), cur(1), prev(0), prev(1),
                  pl.BlockSpec((tm, tn), lambda j, i: (nrb - 1 - i, j)),
                  par(0, 3), par(1, 3), par(0, 1), par(1, 1)],
        out_specs=[pl.BlockSpec((2, tm, tn), lambda j, i: (0, nrb - 1 - i, j)),
                   pl.BlockSpec((2, 3, tn), lambda j, i: (0, 0, j)),
                   pl.BlockSpec((2, 1, tn), lambda j, i: (0, 0, j))],
        out_shape=[jax.ShapeDtypeStruct((2, s, D_FF), BF16),
                   jax.ShapeDtypeStruct((2, 3, D_FF), F32),
                   jax.ShapeDtypeStruct((2, 1, D_FF), F32)],
        scratch_shapes=[pltpu.VMEM((2, 8, tn), F32)],
        compiler_params=_cparams(("parallel", "arbitrary")),
    )(up, up, up, up, dact, conv_w, conv_w, conv_b, conv_b)


def _split_dot(x, tri, terms):
    piece = x.astype(BF16)
    out = jnp.dot(piece, tri, preferred_element_type=F32)
    rest = x
    for _ in range(terms - 1):
        rest = rest - piece.astype(F32)
        piece = rest.astype(BF16)
        out = out + jnp.dot(piece, tri, preferred_element_type=F32)
    return out


def _split_dot_rhs(tri, x, terms):
    piece = x.astype(BF16)
    out = jnp.dot(tri, piece, preferred_element_type=F32)
    rest = x
    for _ in range(terms - 1):
        rest = rest - piece.astype(F32)
        piece = rest.astype(BF16)
        out = out + jnp.dot(tri, piece, preferred_element_type=F32)
    return out


def _tri(n, kind):
    r = lax.broadcasted_iota(jnp.int32, (n, n), 0)
    c = lax.broadcasted_iota(jnp.int32, (n, n), 1)
    cond = {"le": r <= c, "ge": r >= c, "lt": r < c, "gt": r > c}[kind]
    return jnp.where(cond, 1.0, 0.0).astype(BF16)


def _log_sigmoid(x):
    return jnp.minimum(x, 0.0) - jnp.log(1.0 + jnp.exp(-jnp.abs(x)))


def _forget_fwd(f_logit, bias):
    h, r, _ = f_logit.shape

    def body(x_ref, b_ref, o_ref):
        lf = _log_sigmoid(x_ref[...] + b_ref[...])
        within = _split_dot(lf, _tri(LANES, "le"), 3)
        row_tot = jnp.broadcast_to(within[:, LANES - 1:LANES], (r, LANES))
        before = _split_dot_rhs(_tri(r, "gt"), row_tot, 3)
        o_ref[...] = within + before

    blk = pl.BlockSpec((None, r, LANES), lambda i: (i, 0, 0))
    return pl.pallas_call(
        body, name="forget_cumsum_fwd", grid=(h,),
        in_specs=[blk, pl.BlockSpec((None, 1, LANES), lambda i: (i, 0, 0))],
        out_specs=blk, out_shape=jax.ShapeDtypeStruct((h, r, LANES), F32),
        compiler_params=_cparams(("parallel",)),
    )(f_logit, bias)


def _forget_bwd(f_logit, bias, ksum, qsum):
    h, r, _ = f_logit.shape

    def body(x_ref, b_ref, k_ref, q_ref, dx_ref, db_ref):
        d_f = q_ref[...] - k_ref[...]
        within = _split_dot(d_f, _tri(LANES, "ge"), 3)
        row_tot = jnp.broadcast_to(within[:, 0:1], (r, LANES))
        after = _split_dot_rhs(_tri(r, "lt"), row_tot, 3)
        xv = x_ref[...] + b_ref[...]
        dx = (within + after) * jnp.exp(_log_sigmoid(-xv))
        dx_ref[...] = dx
        db_ref[...] = jnp.broadcast_to(jnp.sum(dx), (1, LANES))

    blk = pl.BlockSpec((None, r, LANES), lambda i: (i, 0, 0))
    one = pl.BlockSpec((None, 1, LANES), lambda i: (i, 0, 0))
    return pl.pallas_call(
        body, name="forget_cumsum_bwd", grid=(h,),
        in_specs=[blk, one, blk, blk], out_specs=[blk, one],
        out_shape=[jax.ShapeDtypeStruct((h, r, LANES), F32), jax.ShapeDtypeStruct((h, 1, LANES), F32)],
        compiler_params=_cparams(("parallel",)),
    )(f_logit, bias, ksum, qsum)


def _head_specs(s, tq):
    qblk = pl.BlockSpec((None, tq, HEAD_DIM), lambda h, i: (h, i, 0))
    full = pl.BlockSpec((None, s, HEAD_DIM), lambda h, i: (h, 0, 0))
    col = pl.BlockSpec((None, tq, 1), lambda h, i: (h, i, 0))
    return qblk, full, col


def _qkv_specs(s, tq, offs):
    q_off, k_off, v_off = offs
    return (pl.BlockSpec((None, tq, HEAD_DIM), lambda h, i: (h + q_off, i, 0)),
            pl.BlockSpec((None, s, HEAD_DIM), lambda h, i: (h + k_off, 0, 0)),
            pl.BlockSpec((None, s, HEAD_DIM), lambda h, i: (h + v_off, 0, 0)))


def _scaled(q_ref):
    return (q_ref[...].astype(F32) * Q_SCALE).astype(BF16)


_NT = (((1,), (1,)), ((), ()))
_TN = (((0,), (0,)), ((), ()))


def _cols_minus_rows(rows, cols):
    return lax.broadcasted_iota(jnp.int32, (rows, cols), 1) - lax.broadcasted_iota(jnp.int32, (rows, cols), 0)


def _fox_fwd(qkv, offs, v_ones, f_col, f_row, tq, tk):
    h, s = N_GROUP_HEADS, qkv.shape[1]
    nk = s // tk
    assert tq == tk

    def body(q_ref, k_ref, v_ref, fc_ref, fr_ref, o_ref, lse_ref, m_ref, acc_ref, z0, z1):
        i = pl.program_id(1)
        qs = _scaled(q_ref)
        fq = fc_ref[...]
        m_ref[...] = jnp.full_like(m_ref, NEG_BIG)
        acc_ref[...] = jnp.zeros_like(acc_ref)

        ahead = _cols_minus_rows(tq, tk)

        def block_of(j):
            return jnp.minimum(j, nk - 1)

        def keys_of(j):
            return pl.ds(pl.multiple_of(block_of(j) * tk, tk), tk)

        def logits(j):
            return lax.dot_general(qs, k_ref[keys_of(j), :], _NT, preferred_element_type=F32)

        def soft(j, raw, masked):
            sc = raw + fq - fr_ref[block_of(j)]
            if masked:
                sc = jnp.where(ahead <= (i - j) * tk, sc, NEG_BIG)
            m_old = m_ref[...]
            m_new = jnp.maximum(m_old, jnp.max(sc, axis=-1, keepdims=True))
            p = jnp.exp(sc - m_new)
            acc_ref[...] = jnp.exp(m_old - m_new) * acc_ref[...] + jnp.dot(
                p.astype(BF16), v_ref[keys_of(j), :], preferred_element_type=F32)
            m_ref[...] = m_new

        z0[...] = logits(0)

        def trip(p, masked):
            j = 2 * p
            z1[...] = logits(j + 1)
            soft(j, z0[...], masked)
            z0[...] = logits(j + 2)
            soft(j + 1, z1[...], masked)

        def step(p, carry):
            trip(p, False)
            return carry

        lax.fori_loop(0, i // 2, step, 0)
        trip(i // 2, True)
        l = acc_ref[:, HEAD_DIM:HEAD_DIM + 1]
        o_ref[...] = acc_ref[:, :HEAD_DIM] / l
        lse_ref[...] = m_ref[...] + jnp.log(l)

    qblk, full, colspec = _head_specs(s, tq)
    q_in, k_in, _ = _qkv_specs(s, tq, offs)
    return pl.pallas_call(
        body, name="fox_fwd", grid=(h, s // tq),
        in_specs=[q_in, k_in, pl.BlockSpec((None, s, 2 * HEAD_DIM), lambda hh, i: (hh, 0, 0)), colspec,
                  pl.BlockSpec((None, nk, 1, tk), lambda hh, i: (hh, 0, 0, 0))],
        out_specs=[qblk, colspec],
        out_shape=[jax.ShapeDtypeStruct((h, s, HEAD_DIM), F32), jax.ShapeDtypeStruct((h, s, 1), F32)],
        scratch_shapes=[pltpu.VMEM((tq, 1), F32), pltpu.VMEM((tq, 2 * HEAD_DIM), F32),
                        pltpu.VMEM((tq, tk), F32), pltpu.VMEM((tq, tk), F32)],
        compiler_params=_cparams(("parallel", "parallel")),
    )(qkv, qkv, v_ones, f_col, f_row)


def _fox_bwd(qkv, offs, f_col, f_row, o, lse, d_o, tq, tk):
    h, s = N_GROUP_HEADS, qkv.shape[1]
    nk = s // tk
    assert tq == tk

    def body(q_ref, k_ref, v_ref, fc_ref, fr_ref, o_ref, lse_ref, do_ref,
             dq_ref, dk_ref, dv_ref, ks_ref, qs_ref, dq_acc, qsum_acc, z0, z1, p0, p1):
        i = pl.program_id(1)

        @pl.when(i == 0)
        def _():
            dk_ref[...] = jnp.zeros_like(dk_ref)
            dv_ref[...] = jnp.zeros_like(dv_ref)
            ks_ref[...] = jnp.zeros_like(ks_ref)

        qs = _scaled(q_ref)
        fq = fc_ref[...]
        lse_v = lse_ref[...]
        dob = do_ref[...].astype(BF16)
        delta = jnp.sum(dob.astype(F32) * o_ref[...], axis=-1, keepdims=True)
        dq_acc[...] = jnp.zeros_like(dq_acc)
        qsum_acc[...] = jnp.zeros_like(qsum_acc)

        ahead = _cols_minus_rows(tq, tk)

        def block_of(j):
            return jnp.minimum(j, nk - 1)

        def keys_of(j):
            return pl.ds(pl.multiple_of(block_of(j) * tk, tk), tk)

        def products(j):
            at = keys_of(j)
            return (lax.dot_general(qs, k_ref[at, :], _NT, preferred_element_type=F32),
                    lax.dot_general(dob, v_ref[at, :], _NT, preferred_element_type=F32))

        def grads(j, raw, dp, masked):
            at = keys_of(j)
            sc = raw + fq - fr_ref[block_of(j)]
            if masked:
                sc = jnp.where(ahead <= (i - j) * tk, sc, NEG_BIG)
            p = jnp.exp(sc - lse_v)
            ds = p * (dp - delta)
            dsb = ds.astype(BF16)
            dq_acc[...] += jnp.dot(dsb, k_ref[at, :], preferred_element_type=F32)
            dk_ref[at, :] += lax.dot_general(dsb, qs, _TN, preferred_element_type=F32)
            dv_ref[at, :] += lax.dot_general(p.astype(BF16), dob, _TN, preferred_element_type=F32)
            ks_ref[block_of(j)] += jnp.sum(ds.reshape(tq // 8, 8, tk), axis=0)
            qsum_acc[...] += jnp.sum(ds, axis=-1, keepdims=True)

        z0[...], p0[...] = products(0)

        def trip(pp, masked):
            j = 2 * pp
            z1[...], p1[...] = products(j + 1)
            grads(j, z0[...], p0[...], masked)
            z0[...], p0[...] = products(j + 2)
            grads(j + 1, z1[...], p1[...], masked)

        def step(pp, carry):
            trip(pp, False)
            return carry

        lax.fori_loop(0, i // 2, step, 0)
        trip(i // 2, True)
        dq_ref[...] = dq_acc[...] * Q_SCALE
        qs_ref[...] = qsum_acc[...]

    qblk, full, colspec = _head_specs(s, tq)
    frow = pl.BlockSpec((None, nk, 1, tk), lambda hh, i: (hh, 0, 0, 0))
    big = pltpu.VMEM((tq, tk), F32)
    return pl.pallas_call(
        body, name="fox_bwd", grid=(h, s // tq),
        in_specs=[*_qkv_specs(s, tq, offs), colspec, frow, qblk, colspec, qblk],
        out_specs=[qblk, full, full, pl.BlockSpec((None, nk, 8, tk), lambda hh, i: (hh, 0, 0, 0)), colspec],
        out_shape=[jax.ShapeDtypeStruct((h, s, HEAD_DIM), F32)] * 3
        + [jax.ShapeDtypeStruct((h, nk, 8, tk), F32), jax.ShapeDtypeStruct((h, s, 1), F32)],
        scratch_shapes=[pltpu.VMEM((tq, HEAD_DIM), F32), pltpu.VMEM((tq, 1), F32), big, big, big, big],
        compiler_params=_cparams(("parallel", "arbitrary")),
    )(qkv, qkv, qkv, f_col, f_row, o, lse, d_o)


SB_TERMS = 2
G_TERMS = 1
LOG2E = 1.4426950408889634
LN2 = 0.6931471805599453


def _softplus2(z2):
    return jnp.maximum(z2, 0.0) + jnp.log2(1.0 + jnp.exp2(-jnp.abs(z2)))


def _sb_fwd(qkv, offs, tq, tk):
    h, s = N_GROUP_HEADS, qkv.shape[1]

    assert tq == 2 * tk

    def body(q_ref, k_ref, v_ref, o_ref, tot_ref, acc_ref, run_ref, z0, z1, d0, d1, t0, t1):
        z_refs, d_refs, t_refs = (z0, z1), (d0, d1), (t0, t1)
        i = pl.program_id(1)
        qs = _scaled(q_ref)
        tri = _tri(tk, "ge")
        acc_ref[...] = jnp.zeros_like(acc_ref)
        run_ref[...] = jnp.zeros_like(run_ref)
        nb = (i + 1) * (tq // tk)
        ahead = _cols_minus_rows(tq, tk)

        def keys_of(b):
            j = nb - 1 - jnp.minimum(b, nb - 1)
            return pl.ds(pl.multiple_of(j * tk, tk), tk)

        def visible(b):
            return ahead < i * tq - (nb - 1 - b) * tk

        def logits(b, slot):
            z_refs[slot][...] = lax.dot_general(qs, k_ref[keys_of(b), :], _NT,
                                                preferred_element_type=F32) * LOG2E

        def sums(b, slot, masked):
            z2 = z_refs[slot][...]
            sp = _softplus2(z2)
            if masked:
                sp = jnp.where(visible(b), sp, 0.0)
            inc = _split_dot(sp, tri, SB_TERMS)
            d_refs[slot][...] = z2 - inc
            t_refs[slot][...] = inc[:, 0:1]

        def weigh(b, slot, masked):
            w = jnp.exp2(d_refs[slot][...] - run_ref[...])
            if masked:
                w = jnp.where(visible(b), w, 0.0)
            acc_ref[...] += jnp.dot(w.astype(BF16), v_ref[keys_of(b), :], preferred_element_type=F32)
            run_ref[...] += t_refs[slot][...]

        def trip(p, masked):
            b = 2 * p
            logits(b + 2, 0)
            sums(b + 1, 1, masked)
            weigh(b, 0, masked)
            logits(b + 3, 1)
            sums(b + 2, 0, masked)
            weigh(b + 1, 1, masked)

        logits(0, 0)
        logits(1, 1)
        sums(0, 0, True)
        trip(0, True)

        def step(p, carry):
            trip(p, False)
            return carry

        lax.fori_loop(1, nb // 2, step, 0)
        o_ref[...] = acc_ref[...]
        tot_ref[...] = run_ref[...] * (-LN2)

    qblk, full, colspec = _head_specs(s, tq)
    return pl.pallas_call(
        body, name="sb_fwd", grid=(h, s // tq),
        in_specs=[*_qkv_specs(s, tq, offs)], out_specs=[qblk, colspec],
        out_shape=[jax.ShapeDtypeStruct((h, s, HEAD_DIM), F32), jax.ShapeDtypeStruct((h, s, 1), F32)],
        scratch_shapes=[pltpu.VMEM((tq, HEAD_DIM), F32), pltpu.VMEM((tq, 1), F32),
                        pltpu.VMEM((tq, tk), F32), pltpu.VMEM((tq, tk), F32),
                        pltpu.VMEM((tq, tk), F32), pltpu.VMEM((tq, tk), F32),
                        pltpu.VMEM((tq, 1), F32), pltpu.VMEM((tq, 1), F32)],
        compiler_params=_cparams(("parallel", "parallel")),
    )(qkv, qkv, qkv)


def _sb_bwd(qkv, offs, tot, d_o, tq, tk):
    h, s = N_GROUP_HEADS, qkv.shape[1]

    assert tq == 2 * tk

    def body(q_ref, k_ref, v_ref, tot_ref, do_ref, dq_ref, dk_ref, dv_ref, dq_acc, off_ref, grun_ref,
             z0, z1, p0, p1, u0, u1, b0, b1, t0, t1):
        z_refs, p_refs, u_refs, b_refs, t_refs = (z0, z1), (p0, p1), (u0, u1), (b0, b1), (t0, t1)
        i = pl.program_id(1)

        @pl.when(i == 0)
        def _():
            dk_ref[...] = jnp.zeros_like(dk_ref)
            dv_ref[...] = jnp.zeros_like(dv_ref)

        qs = _scaled(q_ref)
        dob = do_ref[...].astype(BF16)
        tri = _tri(tk, "le")
        dq_acc[...] = jnp.zeros_like(dq_acc)
        off_ref[...] = tot_ref[...] * LOG2E
        grun_ref[...] = jnp.zeros_like(grun_ref)
        nb = (i + 1) * (tq // tk)
        ahead = _cols_minus_rows(tq, tk)

        def keys_of(b):
            return pl.ds(pl.multiple_of(jnp.minimum(b, nb - 1) * tk, tk), tk)

        def visible(b):
            return ahead < i * tq - b * tk

        def logits(b, slot):
            z_refs[slot][...] = lax.dot_general(qs, k_ref[keys_of(b), :], _NT,
                                                preferred_element_type=F32) * LOG2E

        def sums(b, slot, masked):
            z2 = z_refs[slot][...]
            sp = _softplus2(z2)
            lb2 = z2 - sp
            linc = _split_dot(jnp.where(visible(b), sp, 0.0) if masked else sp, tri, SB_TERMS)
            p_refs[slot][...] = lax.dot_general(dob, v_ref[keys_of(b), :], _NT, preferred_element_type=F32)
            u_refs[slot][...] = lb2 + linc
            b_refs[slot][...] = jnp.exp2(lb2)
            t_refs[slot][...] = linc[:, tk - 1:tk]

        def weigh(b, slot, masked):
            w = jnp.exp2(u_refs[slot][...] + off_ref[...])
            if masked:
                w = jnp.where(visible(b), w, 0.0)
            g = w * p_refs[slot][...]
            return w, g, _split_dot(g, tri, G_TERMS)

        def finish(b, slot, w, g, ginc, masked):
            at = keys_of(b)
            dz = g - b_refs[slot][...] * (grun_ref[...] + ginc)
            if masked:
                dz = jnp.where(visible(b), dz, 0.0)
            dzb = dz.astype(BF16)
            dq_acc[...] += jnp.dot(dzb, k_ref[at, :], preferred_element_type=F32)
            dk_ref[at, :] += lax.dot_general(dzb, qs, _TN, preferred_element_type=F32)
            dv_ref[at, :] += lax.dot_general(w.astype(BF16), dob, _TN, preferred_element_type=F32)
            off_ref[...] += t_refs[slot][...]
            grun_ref[...] += ginc[:, tk - 1:tk]

        def trip(p, masked):
            for slot in (0, 1):
                b = 2 * p + slot
                w, g, ginc = weigh(b, slot, masked)
                logits(b + 2, slot)
                sums(b + 1, 1 - slot, masked)
                finish(b, slot, w, g, ginc, masked)

        logits(0, 0)
        logits(1, 1)
        sums(0, 0, True)
        n_plain = jnp.maximum(nb // 2 - 2, 0)

        def plain(p, carry):
            trip(p, False)
            return carry

        def guarded(p, carry):
            trip(p, True)
            return carry

        lax.fori_loop(0, n_plain, plain, 0)
        lax.fori_loop(n_plain, nb // 2, guarded, 0)
        dq_ref[...] = dq_acc[...] * Q_SCALE

    qblk, full, colspec = _head_specs(s, tq)
    big = pltpu.VMEM((tq, tk), F32)
    return pl.pallas_call(
        body, name="sb_bwd", grid=(h, s // tq),
        in_specs=[*_qkv_specs(s, tq, offs), colspec, qblk], out_specs=[qblk, full, full],
        out_shape=[jax.ShapeDtypeStruct((h, s, HEAD_DIM), F32)] * 3,
        scratch_shapes=[pltpu.VMEM((tq, HEAD_DIM), F32), pltpu.VMEM((tq, 1), F32), pltpu.VMEM((tq, 1), F32)]
        + [big] * 8 + [pltpu.VMEM((tq, 1), F32)] * 2,
        compiler_params=_cparams(("parallel", "arbitrary")),
    )(qkv, qkv, qkv, tot, d_o)


def _sum_adamw(parts, w, m, v, name, tr=256):
    _, rows, lanes = parts.shape
    tr = _tile(rows, tr, 16)
    c_m = 1.0 - ADAM_B1 ** ADAM_STEP
    c_v = 1.0 - ADAM_B2 ** ADAM_STEP

    def body(p_ref, w_ref, m_ref, v_ref, g_ref, d_ref, nm_ref, nv_ref):
        g = p_ref[0].astype(F32)
        for j in range(1, N_DEV):
            g = g + p_ref[j].astype(F32)
        nm = ADAM_B1 * m_ref[...] + (1.0 - ADAM_B1) * g
        nv = ADAM_B2 * v_ref[...] + (1.0 - ADAM_B2) * (g * g)
        m_hat = nm / c_m
        v_hat = nv / c_v
        g_ref[...] = g
        d_ref[...] = -ADAM_LR * (m_hat / (jnp.sqrt(v_hat) + ADAM_EPS) + ADAM_WD * w_ref[...])
        nm_ref[...] = nm
        nv_ref[...] = nv

    blk = pl.BlockSpec((tr, lanes), lambda i: (i, 0))
    return pl.pallas_call(
        body, name=name, grid=(rows // tr,),
        in_specs=[pl.BlockSpec((N_DEV, tr, lanes), lambda i: (0, i, 0)), blk, blk, blk],
        out_specs=[blk] * 4, out_shape=[jax.ShapeDtypeStruct((rows, lanes), F32)] * 4,
        compiler_params=_cparams(("parallel",)),
    )(parts, w, m, v)


def kernel(x, attn_norm_g, w_in, forget_bias, fox_out_g, sb_out_g, w_out, ffn_norm_g, w_up, conv_w, conv_b, w_down, final_norm_g, loss_target, m_attn_norm_g, m_w_in, m_forget_bias, m_fox_out_g, m_sb_out_g, m_w_out, m_ffn_norm_g, m_w_up, m_conv_w, m_conv_b, m_w_down, m_final_norm_g, v_attn_norm_g, v_w_in, v_forget_bias, v_fox_out_g, v_sb_out_g, v_w_out, v_ffn_norm_g, v_w_up, v_conv_w, v_conv_b, v_w_down, v_final_norm_g):
    s = x.shape[1]
    xs = x[0]
    tq = min(ATTN_TQ, s)
    tk_fox = min(FOX_TK, s)
    tk_sb = min(SB_TK, s)
    in_shard, up_shard, out_shard, down_shard = IN_COLS // N_DEV, 2 * D_FF // N_DEV, D_MODEL // N_DEV, D_FF // N_DEV

    cw = conv_w[0]
    cw_hi = cw.astype(BF16)
    cw_lo = (cw - cw_hi.astype(F32)).astype(BF16)
    g_in, g_out, g_up, g_down, g_conv = _all_gather(
        [w_in[0].astype(BF16), w_out[0].astype(BF16), w_up[0].astype(BF16), w_down[0].astype(BF16),
         jnp.stack([cw_hi, cw_lo])])
    g_chi, g_clo = g_conv[:, 0], g_conv[:, 1]
    n_gate = QKV_W + N_GROUP_HEADS
    in_windows = _col_windows(N_DEV, in_shard, gap_at=n_gate, gap=GATE_PAD - N_GROUP_HEADS)
    up_windows = _col_windows(N_DEV, up_shard)
    w_in_p = _assemble_cols(g_in, IN_COLS_PAD, in_windows, "assemble_w_in")
    w_out_f = g_out.reshape(D_MODEL, D_MODEL)
    w_up_f = _assemble_cols(g_up, 2 * D_FF, up_windows, "assemble_w_up")
    w_down_f = g_down.reshape(D_FF, D_MODEL)
    conv_w_f = (g_chi.astype(F32) + g_clo.astype(F32)).transpose(1, 0, 2).reshape(3, 2 * D_FF)
    conv_w2 = conv_w_f.reshape(3, 2, D_FF).transpose(1, 0, 2)
    conv_b2 = conv_b.reshape(2, 1, D_FF)

    h1 = _rms_fwd(xs, attn_norm_g)
    proj_h = _mm_heads(h1, w_in_p, "in_proj")
    fox_offs = (0, N_GROUP_HEADS, 2 * N_GROUP_HEADS)
    sb_first = 3 * N_GROUP_HEADS + GATE_PAD // HEAD_DIM
    sb_offs = (sb_first, sb_first + N_GROUP_HEADS, sb_first + 2 * N_GROUP_HEADS)
    f_logit = _mm_nn(h1, w_in_p[:, QKV_W:QKV_W + GATE_PAD], F32, "gate_proj")[:, :N_GROUP_HEADS]
    fv = proj_h[2 * N_GROUP_HEADS:3 * N_GROUP_HEADS]

    f_logit_h = f_logit.T.reshape(N_GROUP_HEADS, s // LANES, LANES)
    bias_h = jnp.broadcast_to(forget_bias.reshape(N_GROUP_HEADS, 1, 1), (N_GROUP_HEADS, 1, LANES))
    big_f = _forget_fwd(f_logit_h, bias_h)
    f_col = big_f.reshape(N_GROUP_HEADS, s, 1)
    f_row = big_f.reshape(N_GROUP_HEADS, s // tk_fox, 1, tk_fox)

    fv_ones = jnp.concatenate([fv, jnp.ones_like(fv)], axis=-1)
    o_fox_h, lse = _fox_fwd(proj_h, fox_offs, fv_ones, f_col, f_row, tq, tk_fox)
    o_sb_h, sb_tot = _sb_fwd(proj_h, sb_offs, tq, tk_sb)
    g_fox_h = fox_out_g.reshape(N_GROUP_HEADS, 1, HEAD_DIM)
    g_sb_h = sb_out_g.reshape(N_GROUP_HEADS, 1, HEAD_DIM)
    o_n = _group_rms_fwd(o_fox_h, o_sb_h, g_fox_h, g_sb_h)
    x1 = _mm_nn(o_n, w_out_f, F32, "out_proj", resid=xs)
    h2 = _rms_fwd(x1, ffn_norm_g)
    up = _mm_up(h2, w_up_f)
    act = _conv_gate_fwd(up, conv_w2, conv_b2)
    x2 = _mm_nn(act, w_down_f, F32, "down_proj", resid=x1, tk=1408)

    d_x2, d_x2b, dg_final, loss_part = _loss_head(x2, loss_target[0], final_norm_g.reshape(1, D_MODEL))
    d_act = _mm_nt(d_x2b, w_down_f, BF16, "d_act", tn=1408)
    dw_down = _mm_tn(act, d_x2b, "d_w_down", tm=1408)
    d_up, dcw2, dcb2 = _conv_gate_bwd(up, d_act, conv_w2, conv_b2)
    d_h2 = _mm_dup_nt(d_up, w_up_f)
    dw_up = _mm_dwup_tn(h2, d_up)
    d_x1, d_x1b, dg_ffn = _rms_bwd(x1, d_h2, ffn_norm_g, d_x2, dy_col=0, name="ffn_norm_bwd", want_bf16=True)
    d_on = _mm_nt(d_x1b, w_out_f, F32, "d_o_normed")
    dw_out = _mm_tn(o_n, d_x1b, "d_w_out")
    d_o_fox_h, dg_fox = _group_rms_bwd(o_fox_h, d_on, g_fox_h, dy_col=0, name="fox_norm_bwd")
    d_o_sb_h, dg_sb = _group_rms_bwd(o_sb_h, d_on, g_sb_h, dy_col=1, name="sb_norm_bwd")

    dfq, dfk, dfv, ksum8, qsum = _fox_bwd(proj_h, fox_offs, f_col, f_row, o_fox_h, lse, d_o_fox_h, tq, tk_fox)
    dsq, dsk, dsv = _sb_bwd(proj_h, sb_offs, sb_tot, d_o_sb_h, tq, tk_sb)
    ksum = jnp.sum(ksum8, axis=2).reshape(N_GROUP_HEADS, s // LANES, LANES)
    d_f_logit_h, d_bias_h = _forget_bwd(f_logit_h, bias_h, ksum,
                                        qsum.reshape(N_GROUP_HEADS, s // LANES, LANES))
    d_f_logit = d_f_logit_h.reshape(N_GROUP_HEADS, s).T

    d_proj = _merge_dproj((dfq, dfk, dfv), d_f_logit, (dsq, dsk, dsv))
    d_h1 = _mm_nt(d_proj, w_in_p, F32, "d_h1", tk=640)
    dw_in_p = _mm_tn(h1, d_proj, "d_w_in", tn=640)
    grad_x, dg_attn = _rms_bwd(xs, d_h1, attn_norm_g, d_x1, dy_col=0, name="attn_norm_bwd", want_bf16=False)

    dconv_w = dcw2.transpose(1, 0, 2).reshape(3, 2 * D_FF)
    dconv_b = dcb2.reshape(1, 2 * D_FF)
    slabs = [_split_cols(dw_in_p, N_DEV, in_shard, in_windows, "split_d_w_in"),
             dw_out.astype(BF16).reshape(N_DEV, out_shard, D_MODEL),
             _split_cols(dw_up, N_DEV, up_shard, up_windows, "split_d_w_up"),
             dw_down.astype(BF16).reshape(N_DEV, down_shard, D_MODEL),
             dconv_w.astype(BF16).reshape(3, N_DEV, up_shard).transpose(1, 0, 2)]
    small_shapes = [(1, D_MODEL), (1, N_GROUP_HEADS), (1, GROUP_W), (1, GROUP_W), (1, D_MODEL),
                    (1, 2 * D_FF), (D_MODEL,), (1,)]
    spack = _pack([dg_attn, d_bias_h[:, 0, 0], dg_fox, dg_sb, dg_ffn, dconv_b, dg_final, loss_part[0, 0:1]],
                  SMALL_ROWS, F32)
    *grecv, srecv = _grad_exchange(slabs, spack)

    big = [_sum_adamw(g, w_[0], m_[0], v_[0], "adamw_" + tag)
           for g, w_, m_, v_, tag in zip(
               grecv, (w_in, w_out, w_up, w_down, conv_w), (m_w_in, m_w_out, m_w_up, m_w_down, m_conv_w),
               (v_w_in, v_w_out, v_w_up, v_w_down, v_conv_w), ("w_in", "w_out", "w_up", "w_down", "conv_w"))]

    def small_pack(a_attn, a_bias, a_fox, a_sb, a_ffn, a_cb, a_fin):
        return _pack([a_attn, a_bias, a_fox, a_sb, a_ffn, a_cb, a_fin, jnp.zeros((1,), F32)], SMALL_ROWS, F32)

    small = _sum_adamw(srecv, small_pack(attn_norm_g, forget_bias, fox_out_g, sb_out_g, ffn_norm_g, conv_b, final_norm_g),
                       small_pack(m_attn_norm_g, m_forget_bias, m_fox_out_g, m_sb_out_g, m_ffn_norm_g, m_conv_b, m_final_norm_g),
                       small_pack(v_attn_norm_g, v_forget_bias, v_fox_out_g, v_sb_out_g, v_ffn_norm_g, v_conv_b, v_final_norm_g),
                       "adamw_replicated", tr=SMALL_ROWS)

    outs = []
    loss = None
    for kind in range(4):
        b_in, b_out, b_up, b_down, b_conv = (res[kind] for res in big)
        s_attn, s_bias, s_fox, s_sb, s_ffn, s_cb, s_fin, s_loss = _unpack(small[kind], small_shapes)
        if kind == 0:
            loss = s_loss[0]
        outs += [s_attn, b_in[None], s_bias, s_fox, s_sb, b_out[None], s_ffn, b_up[None], b_conv[None], s_cb,
                 b_down[None], s_fin]
    return (loss, grad_x[None], *outs)
```

```python
import jax
import jax.numpy as jnp
from jax import lax
from jax.experimental import pallas as pl
from jax.experimental.pallas import tpu as pltpu

F32 = jnp.float32
BF16 = jnp.bfloat16

D_MODEL = 1024
HEAD_DIM = 64
N_GROUP_HEADS = 8
GROUP_W = N_GROUP_HEADS * HEAD_DIM
QKV_W = 3 * GROUP_W
IN_COLS = 2 * QKV_W + N_GROUP_HEADS
GATE_PAD = 128
IN_COLS_PAD = 2 * QKV_W + GATE_PAD
D_FF = 2816
N_DEV = 8
EPS = 1e-6
Q_SCALE = HEAD_DIM ** -0.5

ADAM_LR = 0.001
ADAM_B1 = 0.9
ADAM_B2 = 0.999
ADAM_EPS = 1e-08
ADAM_WD = 0.01
ADAM_STEP = 10

LANES = 128
SMALL_ROWS = 80
VMEM_LIMIT = 56 * 1024 * 1024
NEG_BIG = -1e30
ATTN_TQ = 512
FOX_TK = 512
SB_TK = 256
MESH = pl.DeviceIdType.MESH


def _cparams(sem=None, **kw):
    return pltpu.CompilerParams(dimension_semantics=sem, vmem_limit_bytes=VMEM_LIMIT, **kw)


def _tile(n, target, mult=LANES):
    if n <= target:
        return n
    t = (target // mult) * mult
    while t >= mult:
        if n % t == 0:
            return t
        t -= mult
    return n


def _seg_len(shape):
    n = 1
    for s in shape:
        n *= s
    return -(-n // LANES) * LANES


def _pack(arrs, rows, dtype):
    parts = []
    for a in arrs:
        f = a.reshape(-1).astype(dtype)
        parts.append(jnp.pad(f, (0, _seg_len(a.shape) - f.shape[0])))
    flat = jnp.concatenate(parts)
    flat = jnp.pad(flat, (0, rows * LANES - flat.shape[0]))
    return flat.reshape(rows, LANES)


def _unpack(p, shapes, lead=()):
    flat = p.reshape(lead + (-1,))
    out, off = [], 0
    for shp in shapes:
        n = 1
        for s in shp:
            n *= s
        out.append(flat[..., off:off + n].reshape(lead + tuple(shp)))
        off += _seg_len(shp)
    return out


def _my_pos():
    return lax.axis_index("x"), lax.axis_index("y"), lax.axis_index("c")


def _all_gather(blocks):
    n = len(blocks)

    def body(*refs):
        x_refs, out_refs = refs[:n], refs[n:2 * n]
        send_sems, recv_sems, local_sems = refs[2 * n:]
        x, y, c = _my_pos()
        me, sibling = (x, y, c), (x, y, 1 - c)
        chips = [(1 - x, y), (x, 1 - y), (1 - x, 1 - y)]

        def copy(a, k, blk, to, own=False):
            px, py, pc = blk
            slot = out_refs[a].at[4 * px + 2 * py + pc]
            return pltpu.make_async_remote_copy(
                src_ref=x_refs[a] if own else slot, dst_ref=slot,
                send_sem=send_sems.at[a, k], recv_sem=recv_sems.at[a, k],
                device_id=to, device_id_type=MESH)

        mine = [pltpu.make_async_copy(x_refs[a], out_refs[a].at[4 * x + 2 * y + c], local_sems.at[a])
                for a in range(n)]
        for cp in mine:
            cp.start()
        first = []
        for a in range(n):
            first.append(copy(a, 0, me, sibling, own=True))
            first += [copy(a, 1 + j, me, (*chip, c), own=True) for j, chip in enumerate(chips)]
        for cp in first:
            cp.start()
        passed = []
        for j, chip in enumerate(chips):
            for a in range(n):
                copy(a, 1 + j, (*chip, c), me).wait_recv()
                passed.append(copy(a, 4 + j, (*chip, c), sibling))
                passed[-1].start()
        for a in range(n):
            copy(a, 0, sibling, me).wait_recv()
            for j, chip in enumerate(chips):
                copy(a, 4 + j, (*chip, 1 - c), me).wait_recv()
        for cp in first + passed:
            cp.wait_send()
        for cp in mine:
            cp.wait()

    hbm = pl.BlockSpec(memory_space=pl.ANY)
    return pl.pallas_call(
        body, name="weights_all_gather",
        out_shape=[jax.ShapeDtypeStruct((N_DEV,) + b.shape, b.dtype) for b in blocks],
        in_specs=[hbm] * n, out_specs=[hbm] * n,
        scratch_shapes=[pltpu.SemaphoreType.DMA((n, 7)), pltpu.SemaphoreType.DMA((n, 7)),
                        pltpu.SemaphoreType.DMA((n,))],
    )(*blocks)


def _grad_exchange(slabs, spack):
    n = len(slabs) + 1

    def body(*refs):
        in_refs, out_refs = refs[:n], refs[n:2 * n]
        send_sems, recv_sems, local_sems = refs[2 * n:]
        x, y, c = _my_pos()
        my_id = 4 * x + 2 * y + c

        def src_of(a, dev):
            return in_refs[a] if a == n - 1 else in_refs[a].at[dev]

        own = [pltpu.make_async_copy(src_of(a, my_id), out_refs[a].at[my_id], local_sems.at[a])
               for a in range(n)]
        for cp in own:
            cp.start()
        sends, arrivals = [], []
        for k in range(1, N_DEV):
            px, py, pc = x ^ (k >> 2), y ^ ((k >> 1) & 1), c ^ (k & 1)
            peer_id = 4 * px + 2 * py + pc
            for a in range(n):
                for dst_slot, bucket in ((my_id, sends), (peer_id, arrivals)):
                    bucket.append(pltpu.make_async_remote_copy(
                        src_ref=src_of(a, peer_id), dst_ref=out_refs[a].at[dst_slot],
                        send_sem=send_sems.at[a, k - 1], recv_sem=recv_sems.at[a, k - 1],
                        device_id=(px, py, pc), device_id_type=MESH))
        for cp in sends:
            cp.start()
        for cp in arrivals:
            cp.wait_recv()
        for cp in sends:
            cp.wait_send()
        for cp in own:
            cp.wait()

    hbm = pl.BlockSpec(memory_space=pl.ANY)
    return pl.pallas_call(
        body, name="grad_exchange",
        out_shape=[jax.ShapeDtypeStruct(g.shape, g.dtype) for g in slabs]
        + [jax.ShapeDtypeStruct((N_DEV,) + spack.shape, spack.dtype)],
        in_specs=[hbm] * n, out_specs=[hbm] * n,
        scratch_shapes=[pltpu.SemaphoreType.DMA((n, 7)), pltpu.SemaphoreType.DMA((n, 7)),
                        pltpu.SemaphoreType.DMA((n,))],
    )(*slabs, spack)


_HBM = pl.BlockSpec(memory_space=pltpu.HBM)
_SEM = pl.BlockSpec(memory_space=pltpu.SEMAPHORE)
_EFFECT = pltpu.SideEffectType.DATAFLOW_SIDE_EFFECTING


def _my_id():
    x, y, c = _my_pos()
    return 4 * x + 2 * y + c


def _peer_copies(src_refs, land_refs, send_sems, recv_sems, per_peer):
    x, y, c = _my_pos()
    my_id = 4 * x + 2 * y + c
    copies = []
    for k in range(1, N_DEV):
        px, py, pc = x ^ (k >> 2), y ^ ((k >> 1) & 1), c ^ (k & 1)
        for a, (src, land) in enumerate(zip(src_refs, land_refs)):
            copies.append(pltpu.make_async_remote_copy(
                src_ref=src.at[4 * px + 2 * py + pc] if per_peer else src, dst_ref=land.at[my_id],
                send_sem=send_sems.at[a * (N_DEV - 1) + k - 1], recv_sem=recv_sems.at[a * (N_DEV - 1) + k - 1],
                device_id=(px, py, pc), device_id_type=MESH))
    return copies


def _exchange_start(srcs, per_peer, name):
    n = len(srcs)
    lands = [lax.empty(s.shape if per_peer else (N_DEV,) + s.shape, s.dtype) for s in srcs]

    def body(*refs):
        src_refs, land_refs = refs[:n], refs[n:2 * n]
        send_sems, recv_sems = refs[2 * n], refs[2 * n + 1]
        token = refs[-1]
        for cp in _peer_copies(src_refs, land_refs, send_sems, recv_sems, per_peer):
            cp.start()
        token[...] = jnp.zeros_like(token)

    outs = pl.pallas_call(
        body, name=name,
        out_shape=(pltpu.SemaphoreType.DMA((n * (N_DEV - 1),)), pltpu.SemaphoreType.DMA((n * (N_DEV - 1),)),
                   *[pltpu.HBM(a.shape, a.dtype) for a in srcs + lands],
                   jax.ShapeDtypeStruct((8, LANES), F32)),
        in_specs=[_HBM] * (2 * n),
        out_specs=(_SEM, _SEM, *[_HBM] * (2 * n), pl.BlockSpec(memory_space=pltpu.VMEM)),
        input_output_aliases={a: 2 + a for a in range(2 * n)},
        compiler_params=pltpu.CompilerParams(has_side_effects=_EFFECT),
    )(*[pltpu.with_memory_space_constraint(a, pltpu.HBM) for a in srcs + lands])
    return outs[0], outs[1], list(outs[2:2 + n]), list(outs[2 + n:2 + 2 * n]), outs[-1]


def _exchange_wait(handles, per_peer, after, name):
    send_sems, recv_sems, srcs, lands, _ = handles
    n = len(srcs)

    def body(*refs):
        src_refs, land_refs = refs[:n], refs[n:2 * n]
        for cp in _peer_copies(src_refs, land_refs, refs[2 * n], refs[2 * n + 1], per_peer):
            cp.wait_send()
            cp.wait_recv()

    outs = pl.pallas_call(
        body, name=name,
        out_shape=tuple(pltpu.HBM(a.shape, a.dtype) for a in srcs + lands),
        in_specs=[_HBM] * (2 * n) + [_SEM, _SEM, pl.BlockSpec(memory_space=pl.ANY)],
        out_specs=tuple([_HBM] * (2 * n)),
        input_output_aliases={a: a for a in range(2 * n)},
        compiler_params=pltpu.CompilerParams(has_side_effects=_EFFECT),
    )(*srcs, *lands, send_sems, recv_sems, after)
    me = _my_id()
    filled = []
    for src, land in zip(outs[:n], outs[n:]):
        own = lax.dynamic_index_in_dim(src, me, 0, keepdims=True) if per_peer else src[None]
        filled.append(lax.dynamic_update_slice_in_dim(land, own, me, 0))
    return filled


def _col_windows(n_shards, width, gap_at=None, gap=0):
    out = []
    for j in range(n_shards):
        g0, g1 = j * width, (j + 1) * width
        cuts = [g0, g1] if gap_at is None or not g0 < gap_at < g1 else [g0, gap_at, g1]
        for a, b in zip(cuts[:-1], cuts[1:]):
            out.append((j, a - g0, b - g0, a + (gap if gap_at is not None and a >= gap_at else 0)))
    return out


def _assemble_cols(parts, total, windows, name, tr=256):
    n, rows, w = parts.shape
    tr = _tile(rows, tr, 16)

    def body(p_ref, o_ref):
        o_ref[...] = jnp.zeros_like(o_ref)
        for j, lo, hi, dst in windows:
            o_ref[:, dst:dst + hi - lo] = p_ref[j, :, lo:hi]

    return pl.pallas_call(
        body, name=name, grid=(rows // tr,),
        in_specs=[pl.BlockSpec((n, tr, w), lambda i: (0, i, 0))],
        out_specs=pl.BlockSpec((tr, total), lambda i: (i, 0)),
        out_shape=jax.ShapeDtypeStruct((rows, total), parts.dtype),
        compiler_params=_cparams(("parallel",)),
    )(parts)


def _split_cols(full, n, w, windows, name, tr=256):
    rows, total = full.shape
    tr = _tile(rows, tr, 16)

    def body(f_ref, o_ref):
        for j, lo, hi, dst in windows:
            o_ref[j, :, lo:hi] = f_ref[:, dst:dst + hi - lo].astype(o_ref.dtype)

    return pl.pallas_call(
        body, name=name, grid=(rows // tr,),
        in_specs=[pl.BlockSpec((tr, total), lambda i: (i, 0))],
        out_specs=pl.BlockSpec((n, tr, w), lambda i: (0, i, 0)),
        out_shape=jax.ShapeDtypeStruct((n, rows, w), BF16),
        compiler_params=_cparams(("parallel",)),
    )(full)


_DIMS = {"nn": (((1,), (0,)), ((), ())), "nt": (((1,), (1,)), ((), ())), "tn": (((0,), (0,)), ((), ()))}


def _matmul(a, b, *, mode, grid, a_block, a_map, b_block, b_map, o_block, o_map, out_shape, name,
            resid=None):
    nk = grid[2]
    dims = _DIMS[mode]

    def body(*refs):
        if resid is None:
            a_ref, b_ref, o_ref, acc_ref = refs
            r_ref = None
        else:
            a_ref, b_ref, r_ref, o_ref, acc_ref = refs
        k = pl.program_id(2)

        @pl.when(k == 0)
        def _():
            acc_ref[...] = jnp.zeros_like(acc_ref)

        acc_ref[...] += lax.dot_general(a_ref[...], b_ref[...], dims, preferred_element_type=F32)

        @pl.when(k == nk - 1)
        def _():
            res = acc_ref[...]
            if r_ref is not None:
                res = r_ref[...] + res
            o_ref[...] = res.astype(o_ref.dtype)

    in_specs = [pl.BlockSpec(a_block, a_map), pl.BlockSpec(b_block, b_map)]
    args = [a, b]
    if resid is not None:
        in_specs.append(pl.BlockSpec(o_block, o_map))
        args.append(resid)
    acc_shape = tuple(d for d in o_block if d is not None)
    return pl.pallas_call(
        body, name=name, grid=grid, in_specs=in_specs,
        out_specs=pl.BlockSpec(o_block, o_map), out_shape=out_shape,
        scratch_shapes=[pltpu.VMEM(acc_shape, F32)],
        compiler_params=_cparams(("parallel", "parallel", "arbitrary")),
    )(*args)


def _mm_nn(a, b, out_dtype, name, resid=None, tm=1024, tn=1024, tk=1024):
    m, kk = a.shape
    n = b.shape[1]
    tm, tn, tk = _tile(m, tm, 8), _tile(n, tn), _tile(kk, tk)
    return _matmul(a, b, mode="nn", grid=(m // tm, n // tn, kk // tk),
                   a_block=(tm, tk), a_map=lambda i, j, k: (i, k),
                   b_block=(tk, tn), b_map=lambda i, j, k: (k, j),
                   o_block=(tm, tn), o_map=lambda i, j, k: (i, j),
                   out_shape=jax.ShapeDtypeStruct((m, n), out_dtype), name=name, resid=resid)


def _mm_nt(a, b, out_dtype, name, tm=1024, tn=1024, tk=1024):
    m, kk = a.shape
    n = b.shape[0]
    tm, tn, tk = _tile(m, tm, 8), _tile(n, tn), _tile(kk, tk)
    return _matmul(a, b, mode="nt", grid=(m // tm, n // tn, kk // tk),
                   a_block=(tm, tk), a_map=lambda i, j, k: (i, k),
                   b_block=(tn, tk), b_map=lambda i, j, k: (j, k),
                   o_block=(tm, tn), o_map=lambda i, j, k: (i, j),
                   out_shape=jax.ShapeDtypeStruct((m, n), out_dtype), name=name)


def _mm_tn(a, b, name, tm=1024, tn=1024, tk=1024):
    kk, m = a.shape
    n = b.shape[1]
    tm, tn, tk = _tile(m, tm), _tile(n, tn), _tile(kk, tk, 8)
    return _matmul(a, b, mode="tn", grid=(m // tm, n // tn, kk // tk),
                   a_block=(tk, tm), a_map=lambda i, j, k: (k, i),
                   b_block=(tk, tn), b_map=lambda i, j, k: (k, j),
                   o_block=(tm, tn), o_map=lambda i, j, k: (i, j),
                   out_shape=jax.ShapeDtypeStruct((m, n), F32), name=name)


def _mm_heads(a, b, name, tm=1024, tn=640):
    m, kk = a.shape
    n = b.shape[1]
    tm, tn = _tile(m, tm, 16), _tile(n, tn)
    per_tile = tn // HEAD_DIM

    def body(a_ref, b_ref, o_ref):
        res = jnp.dot(a_ref[...], b_ref[...], preferred_element_type=F32)
        for hh in range(per_tile):
            o_ref[hh] = res[:, hh * HEAD_DIM:(hh + 1) * HEAD_DIM].astype(o_ref.dtype)

    return pl.pallas_call(
        body, name=name, grid=(m // tm, n // tn),
        in_specs=[pl.BlockSpec((tm, kk), lambda i, j: (i, 0)), pl.BlockSpec((kk, tn), lambda i, j: (0, j))],
        out_specs=pl.BlockSpec((per_tile, tm, HEAD_DIM), lambda i, j: (j, i, 0)),
        out_shape=jax.ShapeDtypeStruct((n // HEAD_DIM, m, HEAD_DIM), BF16),
        compiler_params=_cparams(("parallel", "parallel")),
    )(a, b)


def _mm_up(h, w_up, tm=2048, tn=256):
    s = h.shape[0]
    tm = _tile(s, tm, 8)
    nh = D_FF // tn
    return _matmul(h, w_up, mode="nn", grid=(s // tm, 2 * nh, 1),
                   a_block=(tm, D_MODEL), a_map=lambda i, j, k: (i, 0),
                   b_block=(D_MODEL, tn), b_map=lambda i, j, k: (0, j),
                   o_block=(None, tm, tn), o_map=lambda i, j, k: (j // nh, i, j % nh),
                   out_shape=jax.ShapeDtypeStruct((2, s, D_FF), F32), name="up_proj")


def _mm_dup_nt(dup, w_up, tm=1024, tk=1408):
    s = dup.shape[1]
    tm = _tile(s, tm, 8)
    nh = D_FF // tk
    return _matmul(dup, w_up, mode="nt", grid=(s // tm, 1, 2 * nh),
                   a_block=(None, tm, tk), a_map=lambda i, j, k: (k // nh, i, k % nh),
                   b_block=(D_MODEL, tk), b_map=lambda i, j, k: (0, k),
                   o_block=(tm, D_MODEL), o_map=lambda i, j, k: (i, 0),
                   out_shape=jax.ShapeDtypeStruct((s, D_MODEL), F32), name="d_h2")


def _mm_dwup_tn(h, dup, tn=1408, tk=1024):
    s = h.shape[0]
    tk = _tile(s, tk, 8)
    nh = D_FF // tn
    return _matmul(h, dup, mode="tn", grid=(1, 2 * nh, s // tk),
                   a_block=(tk, D_MODEL), a_map=lambda i, j, k: (k, 0),
                   b_block=(None, tk, tn), b_map=lambda i, j, k: (j // nh, k, j % nh),
                   o_block=(D_MODEL, tn), o_map=lambda i, j, k: (0, j),
                   out_shape=jax.ShapeDtypeStruct((D_MODEL, 2 * D_FF), F32), name="d_w_up")


def _rms_fwd(x, g, tr=256):
    s, d = x.shape
    tr = _tile(s, tr, 8)

    def body(x_ref, g_ref, o_ref):
        xv = x_ref[...]
        r = lax.rsqrt(jnp.mean(xv * xv, axis=-1, keepdims=True) + EPS)
        o_ref[...] = (xv * r * g_ref[...]).astype(o_ref.dtype)

    return pl.pallas_call(
        body, name="rms_fwd", grid=(s // tr,),
        in_specs=[pl.BlockSpec((tr, d), lambda i: (i, 0)), pl.BlockSpec((1, d), lambda i: (0, 0))],
        out_specs=pl.BlockSpec((tr, d), lambda i: (i, 0)),
        out_shape=jax.ShapeDtypeStruct((s, d), BF16),
        compiler_params=_cparams(("parallel",)),
    )(x, g)


def _group_rms_fwd(o_fox, o_sb, g_fox, g_sb, tr=256):
    nh, s, dh = o_fox.shape
    tr = _tile(s, tr, 8)

    def body(a_ref, b_ref, ga_ref, gb_ref, o_ref):
        for src, g_ref, lo in ((a_ref, ga_ref, 0), (b_ref, gb_ref, nh * dh)):
            heads = [src[hh] for hh in range(nh)]
            ss = heads[0] * heads[0]
            for xv in heads[1:]:
                ss = ss + xv * xv
            r = lax.rsqrt(jnp.sum(ss, axis=-1, keepdims=True) * (1.0 / (nh * dh)) + EPS)
            for hh, xv in enumerate(heads):
                o_ref[:, lo + hh * dh:lo + (hh + 1) * dh] = (xv * r * g_ref[hh]).astype(o_ref.dtype)

    heads_blk = pl.BlockSpec((nh, tr, dh), lambda i: (0, i, 0))
    gain = pl.BlockSpec((nh, 1, dh), lambda i: (0, 0, 0))
    return pl.pallas_call(
        body, name="group_rms_fwd", grid=(s // tr,),
        in_specs=[heads_blk, heads_blk, gain, gain],
        out_specs=pl.BlockSpec((tr, 2 * nh * dh), lambda i: (i, 0)),
        out_shape=jax.ShapeDtypeStruct((s, 2 * nh * dh), BF16),
        compiler_params=_cparams(("parallel",)),
    )(o_fox, o_sb, g_fox, g_sb)


def _group_rms_bwd(x, dy, g, *, dy_col, name, tr=256):
    nh, s, dh = x.shape
    tr = _tile(s, tr, 8)
    d = nh * dh

    def body(x_ref, dy_ref, g_ref, dx_ref, dg_ref):
        @pl.when(pl.program_id(0) == 0)
        def _():
            dg_ref[...] = jnp.zeros_like(dg_ref)

        dyv = dy_ref[...]
        xs_ = [x_ref[hh] for hh in range(nh)]
        dys = [dyv[:, hh * dh:(hh + 1) * dh] for hh in range(nh)]
        ss = xs_[0] * xs_[0]
        for xv in xs_[1:]:
            ss = ss + xv * xv
        r = lax.rsqrt(jnp.sum(ss, axis=-1, keepdims=True) * (1.0 / d) + EPS)
        xh = [xv * r for xv in xs_]
        gy = [dys[hh] * g_ref[hh] for hh in range(nh)]
        dot = xh[0] * gy[0]
        for hh in range(1, nh):
            dot = dot + xh[hh] * gy[hh]
        mean_dot = jnp.sum(dot, axis=-1, keepdims=True) * (1.0 / d)
        for hh in range(nh):
            dx_ref[hh] = r * (gy[hh] - xh[hh] * mean_dot)
            dg_ref[hh] += jnp.sum(dys[hh] * xh[hh], axis=0, keepdims=True)

    heads_blk = pl.BlockSpec((nh, tr, dh), lambda i: (0, i, 0))
    gain = pl.BlockSpec((nh, 1, dh), lambda i: (0, 0, 0))
    return pl.pallas_call(
        body, name=name, grid=(s // tr,),
        in_specs=[heads_blk, pl.BlockSpec((tr, d), lambda i: (i, dy_col)), gain],
        out_specs=[heads_blk, gain],
        out_shape=[jax.ShapeDtypeStruct((nh, s, dh), F32), jax.ShapeDtypeStruct((nh, 1, dh), F32)],
        compiler_params=_cparams(("arbitrary",)),
    )(x, dy, g)


def _merge_dproj(parts_fox, d_gate, parts_sb, tr=256):
    nh, s, dh = parts_fox[0].shape
    tr = _tile(s, tr, 16)

    def body(*refs):
        o_ref = refs[-1]
        gate_ref = refs[3]
        col = 0
        for ref in refs[:3]:
            for hh in range(nh):
                o_ref[:, col:col + dh] = ref[hh].astype(o_ref.dtype)
                col += dh
        o_ref[:, col:col + GATE_PAD] = jnp.zeros((tr, GATE_PAD), o_ref.dtype)
        o_ref[:, col:col + N_GROUP_HEADS] = gate_ref[...].astype(o_ref.dtype)
        col += GATE_PAD
        for ref in refs[4:7]:
            for hh in range(nh):
                o_ref[:, col:col + dh] = ref[hh].astype(o_ref.dtype)
                col += dh

    heads_blk = pl.BlockSpec((nh, tr, dh), lambda i: (0, i, 0))
    return pl.pallas_call(
        body, name="merge_d_proj", grid=(s // tr,),
        in_specs=[heads_blk] * 3 + [pl.BlockSpec((tr, N_GROUP_HEADS), lambda i: (i, 0))] + [heads_blk] * 3,
        out_specs=pl.BlockSpec((tr, IN_COLS_PAD), lambda i: (i, 0)),
        out_shape=jax.ShapeDtypeStruct((s, IN_COLS_PAD), BF16),
        compiler_params=_cparams(("parallel",)),
    )(*parts_fox, d_gate, *parts_sb)


def _rms_bwd(x, dy, g, resid, *, dy_col, name, want_bf16, tr=256):
    s, d = x.shape
    tr = _tile(s, tr, 8)
    has_resid = resid is not None

    def body(*refs):
        refs = list(refs)
        x_ref, dy_ref, g_ref = refs[:3]
        r_ref = refs[3] if has_resid else None
        outs = refs[4:] if has_resid else refs[3:]
        dx_ref = outs[0]
        dxb_ref = outs[1] if want_bf16 else None
        dg_ref = outs[-1]

        @pl.when(pl.program_id(0) == 0)
        def _():
            dg_ref[...] = jnp.zeros_like(dg_ref)

        xv = x_ref[...]
        dyv = dy_ref[...]
        r = lax.rsqrt(jnp.mean(xv * xv, axis=-1, keepdims=True) + EPS)
        xh = xv * r
        gy = dyv * g_ref[...]
        dx = r * (gy - xh * jnp.mean(xh * gy, axis=-1, keepdims=True))
        if r_ref is not None:
            dx = r_ref[...] + dx
        dx_ref[...] = dx
        if dxb_ref is not None:
            dxb_ref[...] = dx.astype(BF16)
        dg_ref[...] += jnp.sum(dyv * xh, axis=0, keepdims=True)

    row = pl.BlockSpec((tr, d), lambda i: (i, 0))
    in_specs = [row, pl.BlockSpec((tr, d), lambda i: (i, dy_col)), pl.BlockSpec((1, d), lambda i: (0, 0))]
    args = [x, dy, g]
    if has_resid:
        in_specs.append(row)
        args.append(resid)
    out_specs = [row]
    out_shape = [jax.ShapeDtypeStruct((s, d), F32)]
    if want_bf16:
        out_specs.append(row)
        out_shape.append(jax.ShapeDtypeStruct((s, d), BF16))
    out_specs.append(pl.BlockSpec((1, d), lambda i: (0, 0)))
    out_shape.append(jax.ShapeDtypeStruct((1, d), F32))
    return pl.pallas_call(
        body, name=name, grid=(s // tr,), in_specs=in_specs, out_specs=out_specs, out_shape=out_shape,
        compiler_params=_cparams(("arbitrary",)),
    )(*args)


def _loss_head(x2, target, g, tr=256):
    s, d = x2.shape
    tr = _tile(s, tr, 8)

    def body(x_ref, t_ref, g_ref, dx_ref, dxb_ref, dg_ref, loss_ref):
        @pl.when(pl.program_id(0) == 0)
        def _():
            dg_ref[...] = jnp.zeros_like(dg_ref)
            loss_ref[...] = jnp.zeros_like(loss_ref)

        xv = x_ref[...]
        gv = g_ref[...]
        r = lax.rsqrt(jnp.mean(xv * xv, axis=-1, keepdims=True) + EPS)
        xh = xv * r
        err = xh * gv - t_ref[...]
        loss_ref[...] += jnp.sum(jnp.mean(err * err, axis=-1, keepdims=True), axis=0, keepdims=True) * 0.5
        dyv = err * (1.0 / d)
        gy = dyv * gv
        dx = r * (gy - xh * jnp.mean(xh * gy, axis=-1, keepdims=True))
        dx_ref[...] = dx
        dxb_ref[...] = dx.astype(BF16)
        dg_ref[...] += jnp.sum(dyv * xh, axis=0, keepdims=True)

    row = pl.BlockSpec((tr, d), lambda i: (i, 0))
    return pl.pallas_call(
        body, name="loss_head", grid=(s // tr,),
        in_specs=[row, row, pl.BlockSpec((1, d), lambda i: (0, 0))],
        out_specs=[row, row, pl.BlockSpec((1, d), lambda i: (0, 0)), pl.BlockSpec((1, LANES), lambda i: (0, 0))],
        out_shape=[jax.ShapeDtypeStruct((s, d), F32), jax.ShapeDtypeStruct((s, d), BF16),
                   jax.ShapeDtypeStruct((1, d), F32), jax.ShapeDtypeStruct((1, LANES), F32)],
        compiler_params=_cparams(("arbitrary",)),
    )(x2, target, g)


def _conv_taps(cur, prev8, w, b, first):
    prev8 = jnp.where(first, 0.0, prev8)
    ext = jnp.concatenate([prev8, cur], axis=0)
    x1 = pltpu.roll(ext, 1, 0)[8:]
    x2 = pltpu.roll(ext, 2, 0)[8:]
    u = b + w[0:1] * x2
    u = u + w[1:2] * x1
    u = u + w[2:3] * cur
    return u, x1, x2


def _conv_gate_fwd(up, conv_w, conv_b, tm=512, tn=256):
    s = up.shape[1]
    tm = _tile(s, tm, 8)
    nrb = s // tm
    rb8 = tm // 8

    def body(g_ref, v_ref, gp_ref, vp_ref, wg_ref, wv_ref, bg_ref, bv_ref, o_ref):
        first = pl.program_id(1) == 0
        ug, _, _ = _conv_taps(g_ref[...], gp_ref[...], wg_ref[...], bg_ref[...], first)
        uv, _, _ = _conv_taps(v_ref[...], vp_ref[...], wv_ref[...], bv_ref[...], first)
        sg = 1.0 / (1.0 + jnp.exp(-ug))
        o_ref[...] = (ug * sg * uv).astype(o_ref.dtype)

    def cur(h):
        return pl.BlockSpec((None, tm, tn), lambda j, i: (h, i, j))

    def prev(h):
        return pl.BlockSpec((None, 8, tn), lambda j, i: (h, jnp.maximum(i * rb8 - 1, 0), j))

    def par(h, r):
        return pl.BlockSpec((None, r, tn), lambda j, i: (h, 0, j))

    return pl.pallas_call(
        body, name="conv_gate_fwd", grid=(D_FF // tn, nrb),
        in_specs=[cur(0), cur(1), prev(0), prev(1), par(0, 3), par(1, 3), par(0, 1), par(1, 1)],
        out_specs=pl.BlockSpec((tm, tn), lambda j, i: (i, j)),
        out_shape=jax.ShapeDtypeStruct((s, D_FF), BF16),
        compiler_params=_cparams(("parallel", "parallel")),
    )(up, up, up, up, conv_w, conv_w, conv_b, conv_b)


def _conv_gate_bwd(up, dact, conv_w, conv_b, tm=512, tn=256):
    s = up.shape[1]
    tm = _tile(s, tm, 8)
    nrb = s // tm
    rb8 = tm // 8

    def body(g_ref, v_ref, gp_ref, vp_ref, da_ref, wg_ref, wv_ref, bg_ref, bv_ref,
             dup_ref, dcw_ref, dcb_ref, carry_ref):
        i = pl.program_id(1)
        first = i == nrb - 1

        @pl.when(i == 0)
        def _():
            carry_ref[...] = jnp.zeros_like(carry_ref)
            dcw_ref[...] = jnp.zeros_like(dcw_ref)
            dcb_ref[...] = jnp.zeros_like(dcb_ref)

        curs = (g_ref[...], v_ref[...])
        ws = (wg_ref[...], wv_ref[...])
        ug, g1, g2 = _conv_taps(curs[0], gp_ref[...], ws[0], bg_ref[...], first)
        uv, v1, v2 = _conv_taps(curs[1], vp_ref[...], ws[1], bv_ref[...], first)
        sg = 1.0 / (1.0 + jnp.exp(-ug))
        da = da_ref[...].astype(F32)
        d_v = da * (ug * sg)
        d_g = da * uv * (sg * (1.0 + ug * (1.0 - sg)))
        for h, (du, x0, x1, x2) in enumerate(((d_g, curs[0], g1, g2), (d_v, curs[1], v1, v2))):
            dcb_ref[h] += jnp.sum(du, axis=0, keepdims=True)
            dcw_ref[h, 0:1, :] += jnp.sum(du * x2, axis=0, keepdims=True)
            dcw_ref[h, 1:2, :] += jnp.sum(du * x1, axis=0, keepdims=True)
            dcw_ref[h, 2:3, :] += jnp.sum(du * x0, axis=0, keepdims=True)
            ext = jnp.concatenate([du, carry_ref[h]], axis=0)
            n1 = pltpu.roll(ext, tm + 7, 0)[:tm]
            n2 = pltpu.roll(ext, tm + 6, 0)[:tm]
            w = ws[h]
            dup_ref[h] = (w[2:3] * du + w[1:2] * n1 + w[0:1] * n2).astype(dup_ref.dtype)
            carry_ref[h] = du[:8]

    def cur(h):
        return pl.BlockSpec((None, tm, tn), lambda j, i: (h, nrb - 1 - i, j))

    def prev(h):
        return pl.BlockSpec((None, 8, tn), lambda j, i: (h, jnp.maximum((nrb - 1 - i) * rb8 - 1, 0), j))

    def par(h, r):
        return pl.BlockSpec((None, r, tn), lambda j, i: (h, 0, j))

    return pl.pallas_call(
        body, name="conv_gate_bwd", grid=(D_FF // tn, nrb),
        in_specs=[cur(0), cur(1), prev(0), prev(1),
                  pl.BlockSpec((tm, tn), lambda j, i: (nrb - 1 - i, j)),
                  par(0, 3), par(1, 3), par(0, 1), par(1, 1)],
        out_specs=[pl.BlockSpec((2, tm, tn), lambda j, i: (0, nrb - 1 - i, j)),
                   pl.BlockSpec((2, 3, tn), lambda j, i: (0, 0, j)),
                   pl.BlockSpec((2, 1, tn), lambda j, i: (0, 0, j))],
        out_shape=[jax.ShapeDtypeStruct((2, s, D_FF), BF16),
                   jax.ShapeDtypeStruct((2, 3, D_FF), F32),
                   jax.ShapeDtypeStruct((2, 1, D_FF), F32)],
        scratch_shapes=[pltpu.VMEM((2, 8, tn), F32)],
        compiler_params=_cparams(("parallel", "arbitrary")),
    )(up, up, up, up, dact, conv_w, conv_w, conv_b, conv_b)


def _split_dot(x, tri, terms):
    piece = x.astype(BF16)
    out = jnp.dot(piece, tri, preferred_element_type=F32)
    rest = x
    for _ in range(terms - 1):
        rest = rest - piece.astype(F32)
        piece = rest.astype(BF16)
        out = out + jnp.dot(piece, tri, preferred_element_type=F32)
    return out


def _split_dot_rhs(tri, x, terms):
    piece = x.astype(BF16)
    out = jnp.dot(tri, piece, preferred_element_type=F32)
    rest = x
    for _ in range(terms - 1):
        rest = rest - piece.astype(F32)
        piece = rest.astype(BF16)
        out = out + jnp.dot(tri, piece, preferred_element_type=F32)
    return out


def _tri(n, kind):
    r = lax.broadcasted_iota(jnp.int32, (n, n), 0)
    c = lax.broadcasted_iota(jnp.int32, (n, n), 1)
    cond = {"le": r <= c, "ge": r >= c, "lt": r < c, "gt": r > c}[kind]
    return jnp.where(cond, 1.0, 0.0).astype(BF16)


def _log_sigmoid(x):
    return jnp.minimum(x, 0.0) - jnp.log(1.0 + jnp.exp(-jnp.abs(x)))


def _forget_fwd(f_logit, bias):
    h, r, _ = f_logit.shape

    def body(x_ref, b_ref, o_ref):
        lf = _log_sigmoid(x_ref[...] + b_ref[...])
        within = _split_dot(lf, _tri(LANES, "le"), 3)
        row_tot = jnp.broadcast_to(within[:, LANES - 1:LANES], (r, LANES))
        before = _split_dot_rhs(_tri(r, "gt"), row_tot, 3)
        o_ref[...] = within + before

    blk = pl.BlockSpec((None, r, LANES), lambda i: (i, 0, 0))
    return pl.pallas_call(
        body, name="forget_cumsum_fwd", grid=(h,),
        in_specs=[blk, pl.BlockSpec((None, 1, LANES), lambda i: (i, 0, 0))],
        out_specs=blk, out_shape=jax.ShapeDtypeStruct((h, r, LANES), F32),
        compiler_params=_cparams(("parallel",)),
    )(f_logit, bias)


def _forget_bwd(f_logit, bias, ksum, qsum):
    h, r, _ = f_logit.shape

    def body(x_ref, b_ref, k_ref, q_ref, dx_ref, db_ref):
        d_f = q_ref[...] - k_ref[...]
        within = _split_dot(d_f, _tri(LANES, "ge"), 3)
        row_tot = jnp.broadcast_to(within[:, 0:1], (r, LANES))
        after = _split_dot_rhs(_tri(r, "lt"), row_tot, 3)
        xv = x_ref[...] + b_ref[...]
        dx = (within + after) * jnp.exp(_log_sigmoid(-xv))
        dx_ref[...] = dx
        db_ref[...] = jnp.broadcast_to(jnp.sum(dx), (1, LANES))

    blk = pl.BlockSpec((None, r, LANES), lambda i: (i, 0, 0))
    one = pl.BlockSpec((None, 1, LANES), lambda i: (i, 0, 0))
    return pl.pallas_call(
        body, name="forget_cumsum_bwd", grid=(h,),
        in_specs=[blk, one, blk, blk], out_specs=[blk, one],
        out_shape=[jax.ShapeDtypeStruct((h, r, LANES), F32), jax.ShapeDtypeStruct((h, 1, LANES), F32)],
        compiler_params=_cparams(("parallel",)),
    )(f_logit, bias, ksum, qsum)


def _head_specs(s, tq):
    qblk = pl.BlockSpec((None, tq, HEAD_DIM), lambda h, i: (h, i, 0))
    full = pl.BlockSpec((None, s, HEAD_DIM), lambda h, i: (h, 0, 0))
    col = pl.BlockSpec((None, tq, 1), lambda h, i: (h, i, 0))
    return qblk, full, col


def _qkv_specs(s, tq, offs):
    q_off, k_off, v_off = offs
    return (pl.BlockSpec((None, tq, HEAD_DIM), lambda h, i: (h + q_off, i, 0)),
            pl.BlockSpec((None, s, HEAD_DIM), lambda h, i: (h + k_off, 0, 0)),
            pl.BlockSpec((None, s, HEAD_DIM), lambda h, i: (h + v_off, 0, 0)))


def _scaled(q_ref):
    return (q_ref[...].astype(F32) * Q_SCALE).astype(BF16)


_NT = (((1,), (1,)), ((), ()))
_TN = (((0,), (0,)), ((), ()))


def _cols_minus_rows(rows, cols):
    return lax.broadcasted_iota(jnp.int32, (rows, cols), 1) - lax.broadcasted_iota(jnp.int32, (rows, cols), 0)


def _fox_fwd(qkv, offs, v_ones, f_col, f_row, tq, tk):
    h, s = N_GROUP_HEADS, qkv.shape[1]
    nk = s // tk
    assert tq == tk

    def body(q_ref, k_ref, v_ref, fc_ref, fr_ref, o_ref, lse_ref, m_ref, acc_ref, z0, z1):
        i = pl.program_id(1)
        qs = _scaled(q_ref)
        fq = fc_ref[...]
        m_ref[...] = jnp.full_like(m_ref, NEG_BIG)
        acc_ref[...] = jnp.zeros_like(acc_ref)

        ahead = _cols_minus_rows(tq, tk)

        def block_of(j):
            return jnp.minimum(j, nk - 1)

        def keys_of(j):
            return pl.ds(pl.multiple_of(block_of(j) * tk, tk), tk)

        def logits(j):
            return lax.dot_general(qs, k_ref[keys_of(j), :], _NT, preferred_element_type=F32)

        def soft(j, raw, masked):
            sc = raw + fq - fr_ref[block_of(j)]
            if masked:
                sc = jnp.where(ahead <= (i - j) * tk, sc, NEG_BIG)
            m_old = m_ref[...]
            m_new = jnp.maximum(m_old, jnp.max(sc, axis=-1, keepdims=True))
            p = jnp.exp(sc - m_new)
            acc_ref[...] = jnp.exp(m_old - m_new) * acc_ref[...] + jnp.dot(
                p.astype(BF16), v_ref[keys_of(j), :], preferred_element_type=F32)
            m_ref[...] = m_new

        z0[...] = logits(0)

        def trip(p, masked):
            j = 2 * p
            z1[...] = logits(j + 1)
            soft(j, z0[...], masked)
            z0[...] = logits(j + 2)
            soft(j + 1, z1[...], masked)

        def step(p, carry):
            trip(p, False)
            return carry

        lax.fori_loop(0, i // 2, step, 0)
        trip(i // 2, True)
        l = acc_ref[:, HEAD_DIM:HEAD_DIM + 1]
        o_ref[...] = acc_ref[:, :HEAD_DIM] / l
        lse_ref[...] = m_ref[...] + jnp.log(l)

    qblk, full, colspec = _head_specs(s, tq)
    q_in, k_in, _ = _qkv_specs(s, tq, offs)
    return pl.pallas_call(
        body, name="fox_fwd", grid=(h, s // tq),
        in_specs=[q_in, k_in, pl.BlockSpec((None, s, 2 * HEAD_DIM), lambda hh, i: (hh, 0, 0)), colspec,
                  pl.BlockSpec((None, nk, 1, tk), lambda hh, i: (hh, 0, 0, 0))],
        out_specs=[qblk, colspec],
        out_shape=[jax.ShapeDtypeStruct((h, s, HEAD_DIM), F32), jax.ShapeDtypeStruct((h, s, 1), F32)],
        scratch_shapes=[pltpu.VMEM((tq, 1), F32), pltpu.VMEM((tq, 2 * HEAD_DIM), F32),
                        pltpu.VMEM((tq, tk), F32), pltpu.VMEM((tq, tk), F32)],
        compiler_params=_cparams(("parallel", "parallel")),
    )(qkv, qkv, v_ones, f_col, f_row)


def _fox_bwd(qkv, offs, f_col, f_row, o, lse, d_o, tq, tk):
    h, s = N_GROUP_HEADS, qkv.shape[1]
    nk = s // tk
    assert tq == tk

    def body(q_ref, k_ref, v_ref, fc_ref, fr_ref, o_ref, lse_ref, do_ref,
             dq_ref, dk_ref, dv_ref, ks_ref, qs_ref, dq_acc, qsum_acc, z0, z1, p0, p1):
        i = pl.program_id(1)

        @pl.when(i == 0)
        def _():
            dk_ref[...] = jnp.zeros_like(dk_ref)
            dv_ref[...] = jnp.zeros_like(dv_ref)
            ks_ref[...] = jnp.zeros_like(ks_ref)

        qs = _scaled(q_ref)
        fq = fc_ref[...]
        lse_v = lse_ref[...]
        dob = do_ref[...].astype(BF16)
        delta = jnp.sum(dob.astype(F32) * o_ref[...], axis=-1, keepdims=True)
        dq_acc[...] = jnp.zeros_like(dq_acc)
        qsum_acc[...] = jnp.zeros_like(qsum_acc)

        ahead = _cols_minus_rows(tq, tk)

        def block_of(j):
            return jnp.minimum(j, nk - 1)

        def keys_of(j):
            return pl.ds(pl.multiple_of(block_of(j) * tk, tk), tk)

        def products(j):
            at = keys_of(j)
            return (lax.dot_general(qs, k_ref[at, :], _NT, preferred_element_type=F32),
                    lax.dot_general(dob, v_ref[at, :], _NT, preferred_element_type=F32))

        def grads(j, raw, dp, masked):
            at = keys_of(j)
            sc = raw + fq - fr_ref[block_of(j)]
            if masked:
                sc = jnp.where(ahead <= (i - j) * tk, sc, NEG_BIG)
            p = jnp.exp(sc - lse_v)
            ds = p * (dp - delta)
            dsb = ds.astype(BF16)
            dq_acc[...] += jnp.dot(dsb, k_ref[at, :], preferred_element_type=F32)
            dk_ref[at, :] += lax.dot_general(dsb, qs, _TN, preferred_element_type=F32)
            dv_ref[at, :] += lax.dot_general(p.astype(BF16), dob, _TN, preferred_element_type=F32)
            ks_ref[block_of(j)] += jnp.sum(ds.reshape(tq // 8, 8, tk), axis=0)
            qsum_acc[...] += jnp.sum(ds, axis=-1, keepdims=True)

        z0[...], p0[...] = products(0)

        def trip(pp, masked):
            j = 2 * pp
            z1[...], p1[...] = products(j + 1)
            grads(j, z0[...], p0[...], masked)
            z0[...], p0[...] = products(j + 2)
            grads(j + 1, z1[...], p1[...], masked)

        def step(pp, carry):
            trip(pp, False)
            return carry

        lax.fori_loop(0, i // 2, step, 0)
        trip(i // 2, True)
        dq_ref[...] = dq_acc[...] * Q_SCALE
        qs_ref[...] = qsum_acc[...]

    qblk, full, colspec = _head_specs(s, tq)
    frow = pl.BlockSpec((None, nk, 1, tk), lambda hh, i: (hh, 0, 0, 0))
    big = pltpu.VMEM((tq, tk), F32)
    return pl.pallas_call(
        body, name="fox_bwd", grid=(h, s // tq),
        in_specs=[*_qkv_specs(s, tq, offs), colspec, frow, qblk, colspec, qblk],
        out_specs=[qblk, full, full, pl.BlockSpec((None, nk, 8, tk), lambda hh, i: (hh, 0, 0, 0)), colspec],
        out_shape=[jax.ShapeDtypeStruct((h, s, HEAD_DIM), F32)] * 3
        + [jax.ShapeDtypeStruct((h, nk, 8, tk), F32), jax.ShapeDtypeStruct((h, s, 1), F32)],
        scratch_shapes=[pltpu.VMEM((tq, HEAD_DIM), F32), pltpu.VMEM((tq, 1), F32), big, big, big, big],
        compiler_params=_cparams(("parallel", "arbitrary")),
    )(qkv, qkv, qkv, f_col, f_row, o, lse, d_o)


SB_TERMS = 2
G_TERMS = 1
LOG2E = 1.4426950408889634
LN2 = 0.6931471805599453


def _softplus2(z2):
    return jnp.maximum(z2, 0.0) + jnp.log2(1.0 + jnp.exp2(-jnp.abs(z2)))


def _sb_fwd(qkv, offs, tq, tk):
    h, s = N_GROUP_HEADS, qkv.shape[1]

    assert tq == 2 * tk

    def body(q_ref, k_ref, v_ref, o_ref, tot_ref, acc_ref, run_ref, z0, z1, d0, d1, t0, t1):
        z_refs, d_refs, t_refs = (z0, z1), (d0, d1), (t0, t1)
        i = pl.program_id(1)
        qs = _scaled(q_ref)
        tri = _tri(tk, "ge")
        acc_ref[...] = jnp.zeros_like(acc_ref)
        run_ref[...] = jnp.zeros_like(run_ref)
        nb = (i + 1) * (tq // tk)
        ahead = _cols_minus_rows(tq, tk)

        def keys_of(b):
            j = nb - 1 - jnp.minimum(b, nb - 1)
            return pl.ds(pl.multiple_of(j * tk, tk), tk)

        def visible(b):
            return ahead < i * tq - (nb - 1 - b) * tk

        def logits(b, slot):
            z_refs[slot][...] = lax.dot_general(qs, k_ref[keys_of(b), :], _NT,
                                                preferred_element_type=F32) * LOG2E

        def sums(b, slot, masked):
            z2 = z_refs[slot][...]
            sp = _softplus2(z2)
            if masked:
                sp = jnp.where(visible(b), sp, 0.0)
            inc = _split_dot(sp, tri, SB_TERMS)
            d_refs[slot][...] = z2 - inc
            t_refs[slot][...] = inc[:, 0:1]

        def weigh(b, slot, masked):
            w = jnp.exp2(d_refs[slot][...] - run_ref[...])
            if masked:
                w = jnp.where(visible(b), w, 0.0)
            acc_ref[...] += jnp.dot(w.astype(BF16), v_ref[keys_of(b), :], preferred_element_type=F32)
            run_ref[...] += t_refs[slot][...]

        def trip(p, masked):
            b = 2 * p
            logits(b + 2, 0)
            sums(b + 1, 1, masked)
            weigh(b, 0, masked)
            logits(b + 3, 1)
            sums(b + 2, 0, masked)
            weigh(b + 1, 1, masked)

        logits(0, 0)
        logits(1, 1)
        sums(0, 0, True)
        trip(0, True)

        def step(p, carry):
            trip(p, False)
            return carry

        lax.fori_loop(1, nb // 2, step, 0)
        o_ref[...] = acc_ref[...]
        tot_ref[...] = run_ref[...] * (-LN2)

    qblk, full, colspec = _head_specs(s, tq)
    return pl.pallas_call(
        body, name="sb_fwd", grid=(h, s // tq),
        in_specs=[*_qkv_specs(s, tq, offs)], out_specs=[qblk, colspec],
        out_shape=[jax.ShapeDtypeStruct((h, s, HEAD_DIM), F32), jax.ShapeDtypeStruct((h, s, 1), F32)],
        scratch_shapes=[pltpu.VMEM((tq, HEAD_DIM), F32), pltpu.VMEM((tq, 1), F32),
                        pltpu.VMEM((tq, tk), F32), pltpu.VMEM((tq, tk), F32),
                        pltpu.VMEM((tq, tk), F32), pltpu.VMEM((tq, tk), F32),
                        pltpu.VMEM((tq, 1), F32), pltpu.VMEM((tq, 1), F32)],
        compiler_params=_cparams(("parallel", "parallel")),
    )(qkv, qkv, qkv)


def _sb_bwd(qkv, offs, tot, d_o, tq, tk):
    h, s = N_GROUP_HEADS, qkv.shape[1]

    assert tq == 2 * tk

    def body(q_ref, k_ref, v_ref, tot_ref, do_ref, dq_ref, dk_ref, dv_ref, dq_acc, off_ref, grun_ref,
             z0, z1, p0, p1, u0, u1, b0, b1, t0, t1):
        z_refs, p_refs, u_refs, b_refs, t_refs = (z0, z1), (p0, p1), (u0, u1), (b0, b1), (t0, t1)
        i = pl.program_id(1)

        @pl.when(i == 0)
        def _():
            dk_ref[...] = jnp.zeros_like(dk_ref)
            dv_ref[...] = jnp.zeros_like(dv_ref)

        qs = _scaled(q_ref)
        dob = do_ref[...].astype(BF16)
        tri = _tri(tk, "le")
        dq_acc[...] = jnp.zeros_like(dq_acc)
        off_ref[...] = tot_ref[...] * LOG2E
        grun_ref[...] = jnp.zeros_like(grun_ref)
        nb = (i + 1) * (tq // tk)
        ahead = _cols_minus_rows(tq, tk)

        def keys_of(b):
            return pl.ds(pl.multiple_of(jnp.minimum(b, nb - 1) * tk, tk), tk)

        def visible(b):
            return ahead < i * tq - b * tk

        def logits(b, slot):
            z_refs[slot][...] = lax.dot_general(qs, k_ref[keys_of(b), :], _NT,
                                                preferred_element_type=F32) * LOG2E

        def sums(b, slot, masked):
            z2 = z_refs[slot][...]
            sp = _softplus2(z2)
            lb2 = z2 - sp
            linc = _split_dot(jnp.where(visible(b), sp, 0.0) if masked else sp, tri, SB_TERMS)
            p_refs[slot][...] = lax.dot_general(dob, v_ref[keys_of(b), :], _NT, preferred_element_type=F32)
            u_refs[slot][...] = lb2 + linc
            b_refs[slot][...] = jnp.exp2(lb2)
            t_refs[slot][...] = linc[:, tk - 1:tk]

        def weigh(b, slot, masked):
            w = jnp.exp2(u_refs[slot][...] + off_ref[...])
            if masked:
                w = jnp.where(visible(b), w, 0.0)
            g = w * p_refs[slot][...]
            return w, g, _split_dot(g, tri, G_TERMS)

        def finish(b, slot, w, g, ginc, masked):
            at = keys_of(b)
            dz = g - b_refs[slot][...] * (grun_ref[...] + ginc)
            if masked:
                dz = jnp.where(visible(b), dz, 0.0)
            dzb = dz.astype(BF16)
            dq_acc[...] += jnp.dot(dzb, k_ref[at, :], preferred_element_type=F32)
            dk_ref[at, :] += lax.dot_general(dzb, qs, _TN, preferred_element_type=F32)
            dv_ref[at, :] += lax.dot_general(w.astype(BF16), dob, _TN, preferred_element_type=F32)
            off_ref[...] += t_refs[slot][...]
            grun_ref[...] += ginc[:, tk - 1:tk]

        def trip(p, masked):
            for slot in (0, 1):
                b = 2 * p + slot
                w, g, ginc = weigh(b, slot, masked)
                logits(b + 2, slot)
                sums(b + 1, 1 - slot, masked)
                finish(b, slot, w, g, ginc, masked)

        logits(0, 0)
        logits(1, 1)
        sums(0, 0, True)
        n_plain = jnp.maximum(nb // 2 - 2, 0)

        def plain(p, carry):
            trip(p, False)
            return carry

        def guarded(p, carry):
            trip(p, True)
            return carry

        lax.fori_loop(0, n_plain, plain, 0)
        lax.fori_loop(n_plain, nb // 2, guarded, 0)
        dq_ref[...] = dq_acc[...] * Q_SCALE

    qblk, full, colspec = _head_specs(s, tq)
    big = pltpu.VMEM((tq, tk), F32)
    return pl.pallas_call(
        body, name="sb_bwd", grid=(h, s // tq),
        in_specs=[*_qkv_specs(s, tq, offs), colspec, qblk], out_specs=[qblk, full, full],
        out_shape=[jax.ShapeDtypeStruct((h, s, HEAD_DIM), F32)] * 3,
        scratch_shapes=[pltpu.VMEM((tq, HEAD_DIM), F32), pltpu.VMEM((tq, 1), F32), pltpu.VMEM((tq, 1), F32)]
        + [big] * 8 + [pltpu.VMEM((tq, 1), F32)] * 2,
        compiler_params=_cparams(("parallel", "arbitrary")),
    )(qkv, qkv, qkv, tot, d_o)


def _sum_adamw(parts, w, m, v, name, tr=256):
    _, rows, lanes = parts.shape
    tr = _tile(rows, tr, 16)
    c_m = 1.0 - ADAM_B1 ** ADAM_STEP
    c_v = 1.0 - ADAM_B2 ** ADAM_STEP

    def body(p_ref, w_ref, m_ref, v_ref, g_ref, d_ref, nm_ref, nv_ref):
        g = p_ref[0].astype(F32)
        for j in range(1, N_DEV):
            g = g + p_ref[j].astype(F32)
        nm = ADAM_B1 * m_ref[...] + (1.0 - ADAM_B1) * g
        nv = ADAM_B2 * v_ref[...] + (1.0 - ADAM_B2) * (g * g)
        m_hat = nm / c_m
        v_hat = nv / c_v
        g_ref[...] = g
        d_ref[...] = -ADAM_LR * (m_hat / (jnp.sqrt(v_hat) + ADAM_EPS) + ADAM_WD * w_ref[...])
        nm_ref[...] = nm
        nv_ref[...] = nv

    blk = pl.BlockSpec((tr, lanes), lambda i: (i, 0))
    return pl.pallas_call(
        body, name=name, grid=(rows // tr,),
        in_specs=[pl.BlockSpec((N_DEV, tr, lanes), lambda i: (0, i, 0)), blk, blk, blk],
        out_specs=[blk] * 4, out_shape=[jax.ShapeDtypeStruct((rows, lanes), F32)] * 4,
        compiler_params=_cparams(("parallel",)),
    )(parts, w, m, v)


def kernel(x, attn_norm_g, w_in, forget_bias, fox_out_g, sb_out_g, w_out, ffn_norm_g, w_up, conv_w, conv_b, w_down, final_norm_g, loss_target, m_attn_norm_g, m_w_in, m_forget_bias, m_fox_out_g, m_sb_out_g, m_w_out, m_ffn_norm_g, m_w_up, m_conv_w, m_conv_b, m_w_down, m_final_norm_g, v_attn_norm_g, v_w_in, v_forget_bias, v_fox_out_g, v_sb_out_g, v_w_out, v_ffn_norm_g, v_w_up, v_conv_w, v_conv_b, v_w_down, v_final_norm_g):
    s = x.shape[1]
    xs = x[0]
    tq = min(ATTN_TQ, s)
    tk_fox = min(FOX_TK, s)
    tk_sb = min(SB_TK, s)
    in_shard, up_shard, out_shard, down_shard = IN_COLS // N_DEV, 2 * D_FF // N_DEV, D_MODEL // N_DEV, D_FF // N_DEV

    cw = conv_w[0]
    cw_hi = cw.astype(BF16)
    cw_lo = (cw - cw_hi.astype(F32)).astype(BF16)
    (g_in,) = _all_gather([w_in[0].astype(BF16)])
    rest = _exchange_start([w_out[0].astype(BF16), w_up[0].astype(BF16), w_down[0].astype(BF16),
                            jnp.stack([cw_hi, cw_lo])], False, "weights_rest_start")
    n_gate = QKV_W + N_GROUP_HEADS
    in_windows = _col_windows(N_DEV, in_shard, gap_at=n_gate, gap=GATE_PAD - N_GROUP_HEADS)
    up_windows = _col_windows(N_DEV, up_shard)
    w_in_p = _assemble_cols(g_in, IN_COLS_PAD, in_windows, "assemble_w_in")
    conv_b2 = conv_b.reshape(2, 1, D_FF)

    h1 = _rms_fwd(xs, attn_norm_g + rest[-1][0:1, 0:1])
    proj_h = _mm_heads(h1, w_in_p, "in_proj")
    fox_offs = (0, N_GROUP_HEADS, 2 * N_GROUP_HEADS)
    sb_first = 3 * N_GROUP_HEADS + GATE_PAD // HEAD_DIM
    sb_offs = (sb_first, sb_first + N_GROUP_HEADS, sb_first + 2 * N_GROUP_HEADS)
    f_logit = _mm_nn(h1, w_in_p[:, QKV_W:QKV_W + GATE_PAD], F32, "gate_proj")[:, :N_GROUP_HEADS]
    fv = proj_h[2 * N_GROUP_HEADS:3 * N_GROUP_HEADS]

    f_logit_h = f_logit.T.reshape(N_GROUP_HEADS, s // LANES, LANES)
    bias_h = jnp.broadcast_to(forget_bias.reshape(N_GROUP_HEADS, 1, 1), (N_GROUP_HEADS, 1, LANES))
    big_f = _forget_fwd(f_logit_h, bias_h)
    f_col = big_f.reshape(N_GROUP_HEADS, s, 1)
    f_row = big_f.reshape(N_GROUP_HEADS, s // tk_fox, 1, tk_fox)

    fv_ones = jnp.concatenate([fv, jnp.ones_like(fv)], axis=-1)
    o_fox_h, lse = _fox_fwd(proj_h, fox_offs, fv_ones, f_col, f_row, tq, tk_fox)
    o_sb_h, sb_tot = _sb_fwd(proj_h, sb_offs, tq, tk_sb)
    g_fox_h = fox_out_g.reshape(N_GROUP_HEADS, 1, HEAD_DIM)
    g_sb_h = sb_out_g.reshape(N_GROUP_HEADS, 1, HEAD_DIM)
    o_n = _group_rms_fwd(o_fox_h, o_sb_h, g_fox_h, g_sb_h)
    g_out, g_up, g_down, g_conv = _exchange_wait(rest, False, o_n, "weights_rest_wait")
    w_out_f = g_out.reshape(D_MODEL, D_MODEL)
    w_up_f = _assemble_cols(g_up, 2 * D_FF, up_windows, "assemble_w_up")
    w_down_f = g_down.reshape(D_FF, D_MODEL)
    conv_w_f = (g_conv[:, 0].astype(F32) + g_conv[:, 1].astype(F32)).transpose(1, 0, 2).reshape(3, 2 * D_FF)
    conv_w2 = conv_w_f.reshape(3, 2, D_FF).transpose(1, 0, 2)
    x1 = _mm_nn(o_n, w_out_f, F32, "out_proj", resid=xs)
    h2 = _rms_fwd(x1, ffn_norm_g)
    up = _mm_up(h2, w_up_f)
    act = _conv_gate_fwd(up, conv_w2, conv_b2)
    x2 = _mm_nn(act, w_down_f, F32, "down_proj", resid=x1, tk=1408)

    d_x2, d_x2b, dg_final, loss_part = _loss_head(x2, loss_target[0], final_norm_g.reshape(1, D_MODEL))
    d_act = _mm_nt(d_x2b, w_down_f, BF16, "d_act", tn=1408)
    dw_down = _mm_tn(act, d_x2b, "d_w_down", tm=1408)
    d_up, dcw2, dcb2 = _conv_gate_bwd(up, d_act, conv_w2, conv_b2)
    d_h2 = _mm_dup_nt(d_up, w_up_f)
    dw_up = _mm_dwup_tn(h2, d_up)
    d_x1, d_x1b, dg_ffn = _rms_bwd(x1, d_h2, ffn_norm_g, d_x2, dy_col=0, name="ffn_norm_bwd", want_bf16=True)
    d_on = _mm_nt(d_x1b, w_out_f, F32, "d_o_normed")
    dw_out = _mm_tn(o_n, d_x1b, "d_w_out")
    early = _exchange_start(
        [dw_out.astype(BF16).reshape(N_DEV, out_shard, D_MODEL),
         _split_cols(dw_up, N_DEV, up_shard, up_windows, "split_d_w_up"),
         dw_down.astype(BF16).reshape(N_DEV, down_shard, D_MODEL)], True, "grads_early_start")
    g_fox_t = g_fox_h + early[-1][0:1, 0:1]
    d_o_fox_h, dg_fox = _group_rms_bwd(o_fox_h, d_on, g_fox_t, dy_col=0, name="fox_norm_bwd")
    d_o_sb_h, dg_sb = _group_rms_bwd(o_sb_h, d_on, g_sb_h, dy_col=1, name="sb_norm_bwd")

    dfq, dfk, dfv, ksum8, qsum = _fox_bwd(proj_h, fox_offs, f_col, f_row, o_fox_h, lse, d_o_fox_h, tq, tk_fox)
    dsq, dsk, dsv = _sb_bwd(proj_h, sb_offs, sb_tot, d_o_sb_h, tq, tk_sb)
    ksum = jnp.sum(ksum8, axis=2).reshape(N_GROUP_HEADS, s // LANES, LANES)
    d_f_logit_h, d_bias_h = _forget_bwd(f_logit_h, bias_h, ksum,
                                        qsum.reshape(N_GROUP_HEADS, s // LANES, LANES))
    d_f_logit = d_f_logit_h.reshape(N_GROUP_HEADS, s).T

    d_proj = _merge_dproj((dfq, dfk, dfv), d_f_logit, (dsq, dsk, dsv))
    d_h1 = _mm_nt(d_proj, w_in_p, F32, "d_h1", tk=640)
    dw_in_p = _mm_tn(h1, d_proj, "d_w_in", tn=640)
    grad_x, dg_attn = _rms_bwd(xs, d_h1, attn_norm_g, d_x1, dy_col=0, name="attn_norm_bwd", want_bf16=False)

    dconv_w = dcw2.transpose(1, 0, 2).reshape(3, 2 * D_FF)
    dconv_b = dcb2.reshape(1, 2 * D_FF)
    slabs = [_split_cols(dw_in_p, N_DEV, in_shard, in_windows, "split_d_w_in"),
             dconv_w.astype(BF16).reshape(3, N_DEV, up_shard).transpose(1, 0, 2)]
    small_shapes = [(1, D_MODEL), (1, N_GROUP_HEADS), (1, GROUP_W), (1, GROUP_W), (1, D_MODEL),
                    (1, 2 * D_FF), (D_MODEL,), (1,)]
    spack = _pack([dg_attn, d_bias_h[:, 0, 0], dg_fox, dg_sb, dg_ffn, dconv_b, dg_final, loss_part[0, 0:1]],
                  SMALL_ROWS, F32)
    r_in, r_conv, srecv = _grad_exchange(slabs, spack)
    r_out, r_up, r_down = _exchange_wait(early, True, r_in, "grads_early_wait")

    big = [_sum_adamw(g, w_[0], m_[0], v_[0], "adamw_" + tag)
           for g, w_, m_, v_, tag in zip(
               (r_in, r_out, r_up, r_down, r_conv), (w_in, w_out, w_up, w_down, conv_w), (m_w_in, m_w_out, m_w_up, m_w_down, m_conv_w),
               (v_w_in, v_w_out, v_w_up, v_w_down, v_conv_w), ("w_in", "w_out", "w_up", "w_down", "conv_w"))]

    def small_pack(a_attn, a_bias, a_fox, a_sb, a_ffn, a_cb, a_fin):
        return _pack([a_attn, a_bias, a_fox, a_sb, a_ffn, a_cb, a_fin, jnp.zeros((1,), F32)], SMALL_ROWS, F32)

    small = _sum_adamw(srecv, small_pack(attn_norm_g, forget_bias, fox_out_g, sb_out_g, ffn_norm_g, conv_b, final_norm_g),
                       small_pack(m_attn_norm_g, m_forget_bias, m_fox_out_g, m_sb_out_g, m_ffn_norm_g, m_conv_b, m_final_norm_g),
                       small_pack(v_attn_norm_g, v_forget_bias, v_fox_out_g, v_sb_out_g, v_ffn_norm_g, v_conv_b, v_final_norm_g),
                       "adamw_replicated", tr=SMALL_ROWS)

    outs = []
    loss = None
    for kind in range(4):
        b_in, b_out, b_up, b_down, b_conv = (res[kind] for res in big)
        s_attn, s_bias, s_fox, s_sb, s_ffn, s_cb, s_fin, s_loss = _unpack(small[kind], small_shapes)
        if kind == 0:
            loss = s_loss[0]
        outs += [s_attn, b_in[None], s_bias, s_fox, s_sb, b_out[None], s_ffn, b_up[None], b_conv[None], s_cb,
                 b_down[None], s_fin]
    return (loss, grad_x[None], *outs)
```

```python
import jax
import jax.numpy as jnp
from jax import lax
from jax.experimental import pallas as pl
from jax.experimental.pallas import tpu as pltpu

F32 = jnp.float32
BF16 = jnp.bfloat16

D_MODEL = 1024
HEAD_DIM = 64
N_GROUP_HEADS = 8
GROUP_W = N_GROUP_HEADS * HEAD_DIM
QKV_W = 3 * GROUP_W
IN_COLS = 2 * QKV_W + N_GROUP_HEADS
GATE_PAD = 128
IN_COLS_PAD = 2 * QKV_W + GATE_PAD
D_FF = 2816
N_DEV = 8
EPS = 1e-6
Q_SCALE = HEAD_DIM ** -0.5

ADAM_LR = 0.001
ADAM_B1 = 0.9
ADAM_B2 = 0.999
ADAM_EPS = 1e-08
ADAM_WD = 0.01
ADAM_STEP = 10

LANES = 128
SMALL_ROWS = 80
VMEM_LIMIT = 56 * 1024 * 1024
NEG_BIG = -1e30
ATTN_TQ = 512
FOX_TK = 512
SB_TK = 256
MESH = pl.DeviceIdType.MESH


def _cparams(sem=None, **kw):
    return pltpu.CompilerParams(dimension_semantics=sem, vmem_limit_bytes=VMEM_LIMIT, **kw)


def _tile(n, target, mult=LANES):
    if n <= target:
        return n
    t = (target // mult) * mult
    while t >= mult:
        if n % t == 0:
            return t
        t -= mult
    return n


def _seg_len(shape):
    n = 1
    for s in shape:
        n *= s
    return -(-n // LANES) * LANES


def _pack(arrs, rows, dtype):
    parts = []
    for a in arrs:
        f = a.reshape(-1).astype(dtype)
        parts.append(jnp.pad(f, (0, _seg_len(a.shape) - f.shape[0])))
    flat = jnp.concatenate(parts)
    flat = jnp.pad(flat, (0, rows * LANES - flat.shape[0]))
    return flat.reshape(rows, LANES)


def _unpack(p, shapes, lead=()):
    flat = p.reshape(lead + (-1,))
    out, off = [], 0
    for shp in shapes:
        n = 1
        for s in shp:
            n *= s
        out.append(flat[..., off:off + n].reshape(lead + tuple(shp)))
        off += _seg_len(shp)
    return out


def _my_pos():
    return lax.axis_index("x"), lax.axis_index("y"), lax.axis_index("c")


def _all_gather(blocks):
    n = len(blocks)

    def body(*refs):
        x_refs, out_refs = refs[:n], refs[n:2 * n]
        send_sems, recv_sems, local_sems = refs[2 * n:]
        x, y, c = _my_pos()
        me, sibling = (x, y, c), (x, y, 1 - c)
        chips = [(1 - x, y), (x, 1 - y), (1 - x, 1 - y)]

        def copy(a, k, blk, to, own=False):
            px, py, pc = blk
            slot = out_refs[a].at[4 * px + 2 * py + pc]
            return pltpu.make_async_remote_copy(
                src_ref=x_refs[a] if own else slot, dst_ref=slot,
                send_sem=send_sems.at[a, k], recv_sem=recv_sems.at[a, k],
                device_id=to, device_id_type=MESH)

        mine = [pltpu.make_async_copy(x_refs[a], out_refs[a].at[4 * x + 2 * y + c], local_sems.at[a])
                for a in range(n)]
        for cp in mine:
            cp.start()
        first = []
        for a in range(n):
            first.append(copy(a, 0, me, sibling, own=True))
            first += [copy(a, 1 + j, me, (*chip, c), own=True) for j, chip in enumerate(chips)]
        for cp in first:
            cp.start()
        passed = []
        for j, chip in enumerate(chips):
            for a in range(n):
                copy(a, 1 + j, (*chip, c), me).wait_recv()
                passed.append(copy(a, 4 + j, (*chip, c), sibling))
                passed[-1].start()
        for a in range(n):
            copy(a, 0, sibling, me).wait_recv()
            for j, chip in enumerate(chips):
                copy(a, 4 + j, (*chip, 1 - c), me).wait_recv()
        for cp in first + passed:
            cp.wait_send()
        for cp in mine:
            cp.wait()

    hbm = pl.BlockSpec(memory_space=pl.ANY)
    return pl.pallas_call(
        body, name="weights_all_gather",
        out_shape=[jax.ShapeDtypeStruct((N_DEV,) + b.shape, b.dtype) for b in blocks],
        in_specs=[hbm] * n, out_specs=[hbm] * n,
        scratch_shapes=[pltpu.SemaphoreType.DMA((n, 7)), pltpu.SemaphoreType.DMA((n, 7)),
                        pltpu.SemaphoreType.DMA((n,))],
    )(*blocks)


def _grad_exchange(slabs, spack):
    n = len(slabs) + 1

    def body(*refs):
        in_refs, out_refs = refs[:n], refs[n:2 * n]
        send_sems, recv_sems, local_sems = refs[2 * n:]
        x, y, c = _my_pos()
        my_id = 4 * x + 2 * y + c

        def src_of(a, dev):
            return in_refs[a] if a == n - 1 else in_refs[a].at[dev]

        own = [pltpu.make_async_copy(src_of(a, my_id), out_refs[a].at[my_id], local_sems.at[a])
               for a in range(n)]
        for cp in own:
            cp.start()
        sends, arrivals = [], []
        for k in range(1, N_DEV):
            px, py, pc = x ^ (k >> 2), y ^ ((k >> 1) & 1), c ^ (k & 1)
            peer_id = 4 * px + 2 * py + pc
            for a in range(n):
                for dst_slot, bucket in ((my_id, sends), (peer_id, arrivals)):
                    bucket.append(pltpu.make_async_remote_copy(
                        src_ref=src_of(a, peer_id), dst_ref=out_refs[a].at[dst_slot],
                        send_sem=send_sems.at[a, k - 1], recv_sem=recv_sems.at[a, k - 1],
                        device_id=(px, py, pc), device_id_type=MESH))
        for cp in sends:
            cp.start()
        for cp in arrivals:
            cp.wait_recv()
        for cp in sends:
            cp.wait_send()
        for cp in own:
            cp.wait()

    hbm = pl.BlockSpec(memory_space=pl.ANY)
    return pl.pallas_call(
        body, name="grad_exchange",
        out_shape=[jax.ShapeDtypeStruct(g.shape, g.dtype) for g in slabs]
        + [jax.ShapeDtypeStruct((N_DEV,) + spack.shape, spack.dtype)],
        in_specs=[hbm] * n, out_specs=[hbm] * n,
        scratch_shapes=[pltpu.SemaphoreType.DMA((n, 7)), pltpu.SemaphoreType.DMA((n, 7)),
                        pltpu.SemaphoreType.DMA((n,))],
    )(*slabs, spack)


_HBM = pl.BlockSpec(memory_space=pltpu.HBM)
_SEM = pl.BlockSpec(memory_space=pltpu.SEMAPHORE)
_EFFECT = pltpu.SideEffectType.DATAFLOW_SIDE_EFFECTING


def _my_id():
    x, y, c = _my_pos()
    return 4 * x + 2 * y + c


def _peer_copies(src_refs, land_refs, send_sems, recv_sems, per_peer):
    x, y, c = _my_pos()
    my_id = 4 * x + 2 * y + c
    copies = []
    for k in range(1, N_DEV):
        px, py, pc = x ^ (k >> 2), y ^ ((k >> 1) & 1), c ^ (k & 1)
        for a, (src, land) in enumerate(zip(src_refs, land_refs)):
            copies.append(pltpu.make_async_remote_copy(
                src_ref=src.at[4 * px + 2 * py + pc] if per_peer else src, dst_ref=land.at[my_id],
                send_sem=send_sems.at[a * (N_DEV - 1) + k - 1], recv_sem=recv_sems.at[a * (N_DEV - 1) + k - 1],
                device_id=(px, py, pc), device_id_type=MESH))
    return copies


def _exchange_start(srcs, per_peer, name):
    n = len(srcs)
    lands = [lax.empty(s.shape if per_peer else (N_DEV,) + s.shape, s.dtype) for s in srcs]

    def body(*refs):
        src_refs, land_refs = refs[:n], refs[n:2 * n]
        send_sems, recv_sems = refs[2 * n], refs[2 * n + 1]
        token = refs[-1]
        for cp in _peer_copies(src_refs, land_refs, send_sems, recv_sems, per_peer):
            cp.start()
        token[...] = jnp.zeros_like(token)

    outs = pl.pallas_call(
        body, name=name,
        out_shape=(pltpu.SemaphoreType.DMA((n * (N_DEV - 1),)), pltpu.SemaphoreType.DMA((n * (N_DEV - 1),)),
                   *[pltpu.HBM(a.shape, a.dtype) for a in srcs + lands],
                   jax.ShapeDtypeStruct((8, LANES), F32)),
        in_specs=[_HBM] * (2 * n),
        out_specs=(_SEM, _SEM, *[_HBM] * (2 * n), pl.BlockSpec(memory_space=pltpu.VMEM)),
        input_output_aliases={a: 2 + a for a in range(2 * n)},
        compiler_params=pltpu.CompilerParams(has_side_effects=_EFFECT),
    )(*[pltpu.with_memory_space_constraint(a, pltpu.HBM) for a in srcs + lands])
    return outs[0], outs[1], list(outs[2:2 + n]), list(outs[2 + n:2 + 2 * n]), outs[-1]


def _exchange_wait(handles, per_peer, after, name):
    send_sems, recv_sems, srcs, lands, _ = handles
    n = len(srcs)

    def body(*refs):
        src_refs, land_refs = refs[:n], refs[n:2 * n]
        for cp in _peer_copies(src_refs, land_refs, refs[2 * n], refs[2 * n + 1], per_peer):
            cp.wait_send()
            cp.wait_recv()

    outs = pl.pallas_call(
        body, name=name,
        out_shape=tuple(pltpu.HBM(a.shape, a.dtype) for a in srcs + lands),
        in_specs=[_HBM] * (2 * n) + [_SEM, _SEM, pl.BlockSpec(memory_space=pl.ANY)],
        out_specs=tuple([_HBM] * (2 * n)),
        input_output_aliases={a: a for a in range(2 * n)},
        compiler_params=pltpu.CompilerParams(has_side_effects=_EFFECT),
    )(*srcs, *lands, send_sems, recv_sems, after)
    me = _my_id()
    filled = []
    for src, land in zip(outs[:n], outs[n:]):
        own = lax.dynamic_index_in_dim(src, me, 0, keepdims=True) if per_peer else src[None]
        filled.append(lax.dynamic_update_slice_in_dim(land, own, me, 0))
    return filled


def _col_windows(n_shards, width, gap_at=None, gap=0):
    out = []
    for j in range(n_shards):
        g0, g1 = j * width, (j + 1) * width
        cuts = [g0, g1] if gap_at is None or not g0 < gap_at < g1 else [g0, gap_at, g1]
        for a, b in zip(cuts[:-1], cuts[1:]):
            out.append((j, a - g0, b - g0, a + (gap if gap_at is not None and a >= gap_at else 0)))
    return out


def _assemble_cols(parts, total, windows, name, tr=256):
    n, rows, w = parts.shape
    tr = _tile(rows, tr, 16)

    def body(p_ref, o_ref):
        o_ref[...] = jnp.zeros_like(o_ref)
        for j, lo, hi, dst in windows:
            o_ref[:, dst:dst + hi - lo] = p_ref[j, :, lo:hi]

    return pl.pallas_call(
        body, name=name, grid=(rows // tr,),
        in_specs=[pl.BlockSpec((n, tr, w), lambda i: (0, i, 0))],
        out_specs=pl.BlockSpec((tr, total), lambda i: (i, 0)),
        out_shape=jax.ShapeDtypeStruct((rows, total), parts.dtype),
        compiler_params=_cparams(("parallel",)),
    )(parts)


def _split_cols(full, n, w, windows, name, tr=256):
    rows, total = full.shape
    tr = _tile(rows, tr, 16)

    def body(f_ref, o_ref):
        for j, lo, hi, dst in windows:
            o_ref[j, :, lo:hi] = f_ref[:, dst:dst + hi - lo].astype(o_ref.dtype)

    return pl.pallas_call(
        body, name=name, grid=(rows // tr,),
        in_specs=[pl.BlockSpec((tr, total), lambda i: (i, 0))],
        out_specs=pl.BlockSpec((n, tr, w), lambda i: (0, i, 0)),
        out_shape=jax.ShapeDtypeStruct((n, rows, w), BF16),
        compiler_params=_cparams(("parallel",)),
    )(full)


_DIMS = {"nn": (((1,), (0,)), ((), ())), "nt": (((1,), (1,)), ((), ())), "tn": (((0,), (0,)), ((), ()))}


def _matmul(a, b, *, mode, grid, a_block, a_map, b_block, b_map, o_block, o_map, out_shape, name,
            resid=None):
    nk = grid[2]
    dims = _DIMS[mode]

    def body(*refs):
        if resid is None:
            a_ref, b_ref, o_ref, acc_ref = refs
            r_ref = None
        else:
            a_ref, b_ref, r_ref, o_ref, acc_ref = refs
        k = pl.program_id(2)

        @pl.when(k == 0)
        def _():
            acc_ref[...] = jnp.zeros_like(acc_ref)

        acc_ref[...] += lax.dot_general(a_ref[...], b_ref[...], dims, preferred_element_type=F32)

        @pl.when(k == nk - 1)
        def _():
            res = acc_ref[...]
            if r_ref is not None:
                res = r_ref[...] + res
            o_ref[...] = res.astype(o_ref.dtype)

    in_specs = [pl.BlockSpec(a_block, a_map), pl.BlockSpec(b_block, b_map)]
    args = [a, b]
    if resid is not None:
        in_specs.append(pl.BlockSpec(o_block, o_map))
        args.append(resid)
    acc_shape = tuple(d for d in o_block if d is not None)
    return pl.pallas_call(
        body, name=name, grid=grid, in_specs=in_specs,
        out_specs=pl.BlockSpec(o_block, o_map), out_shape=out_shape,
        scratch_shapes=[pltpu.VMEM(acc_shape, F32)],
        compiler_params=_cparams(("parallel", "parallel", "arbitrary")),
    )(*args)


def _mm_nn(a, b, out_dtype, name, resid=None, tm=1024, tn=1024, tk=1024):
    m, kk = a.shape
    n = b.shape[1]
    tm, tn, tk = _tile(m, tm, 8), _tile(n, tn), _tile(kk, tk)
    return _matmul(a, b, mode="nn", grid=(m // tm, n // tn, kk // tk),
                   a_block=(tm, tk), a_map=lambda i, j, k: (i, k),
                   b_block=(tk, tn), b_map=lambda i, j, k: (k, j),
                   o_block=(tm, tn), o_map=lambda i, j, k: (i, j),
                   out_shape=jax.ShapeDtypeStruct((m, n), out_dtype), name=name, resid=resid)


def _mm_nt(a, b, out_dtype, name, tm=1024, tn=1024, tk=1024):
    m, kk = a.shape
    n = b.shape[0]
    tm, tn, tk = _tile(m, tm, 8), _tile(n, tn), _tile(kk, tk)
    return _matmul(a, b, mode="nt", grid=(m // tm, n // tn, kk // tk),
                   a_block=(tm, tk), a_map=lambda i, j, k: (i, k),
                   b_block=(tn, tk), b_map=lambda i, j, k: (j, k),
                   o_block=(tm, tn), o_map=lambda i, j, k: (i, j),
                   out_shape=jax.ShapeDtypeStruct((m, n), out_dtype), name=name)


def _mm_tn(a, b, name, tm=1024, tn=1024, tk=1024):
    kk, m = a.shape
    n = b.shape[1]
    tm, tn, tk = _tile(m, tm), _tile(n, tn), _tile(kk, tk, 8)
    return _matmul(a, b, mode="tn", grid=(m // tm, n // tn, kk // tk),
                   a_block=(tk, tm), a_map=lambda i, j, k: (k, i),
                   b_block=(tk, tn), b_map=lambda i, j, k: (k, j),
                   o_block=(tm, tn), o_map=lambda i, j, k: (i, j),
                   out_shape=jax.ShapeDtypeStruct((m, n), F32), name=name)


def _mm_heads(a, b, name, tm=1024, tn=640):
    m, kk = a.shape
    n = b.shape[1]
    tm, tn = _tile(m, tm, 16), _tile(n, tn)
    per_tile = tn // HEAD_DIM

    def body(a_ref, b_ref, o_ref):
        res = jnp.dot(a_ref[...], b_ref[...], preferred_element_type=F32)
        for hh in range(per_tile):
            o_ref[hh] = res[:, hh * HEAD_DIM:(hh + 1) * HEAD_DIM].astype(o_ref.dtype)

    return pl.pallas_call(
        body, name=name, grid=(m // tm, n // tn),
        in_specs=[pl.BlockSpec((tm, kk), lambda i, j: (i, 0)), pl.BlockSpec((kk, tn), lambda i, j: (0, j))],
        out_specs=pl.BlockSpec((per_tile, tm, HEAD_DIM), lambda i, j: (j, i, 0)),
        out_shape=jax.ShapeDtypeStruct((n // HEAD_DIM, m, HEAD_DIM), BF16),
        compiler_params=_cparams(("parallel", "parallel")),
    )(a, b)


def _mm_up(h, w_up, tm=2048, tn=256):
    s = h.shape[0]
    tm = _tile(s, tm, 8)
    nh = D_FF // tn
    return _matmul(h, w_up, mode="nn", grid=(s // tm, 2 * nh, 1),
                   a_block=(tm, D_MODEL), a_map=lambda i, j, k: (i, 0),
                   b_block=(D_MODEL, tn), b_map=lambda i, j, k: (0, j),
                   o_block=(None, tm, tn), o_map=lambda i, j, k: (j // nh, i, j % nh),
                   out_shape=jax.ShapeDtypeStruct((2, s, D_FF), F32), name="up_proj")


def _mm_dup_nt(dup, w_up, tm=1024, tk=1408):
    s = dup.shape[1]
    tm = _tile(s, tm, 8)
    nh = D_FF // tk
    return _matmul(dup, w_up, mode="nt", grid=(s // tm, 1, 2 * nh),
                   a_block=(None, tm, tk), a_map=lambda i, j, k: (k // nh, i, k % nh),
                   b_block=(D_MODEL, tk), b_map=lambda i, j, k: (0, k),
                   o_block=(tm, D_MODEL), o_map=lambda i, j, k: (i, 0),
                   out_shape=jax.ShapeDtypeStruct((s, D_MODEL), F32), name="d_h2")


def _mm_dwup_tn(h, dup, tn=1408, tk=1024):
    s = h.shape[0]
    tk = _tile(s, tk, 8)
    nh = D_FF // tn
    return _matmul(h, dup, mode="tn", grid=(1, 2 * nh, s // tk),
                   a_block=(tk, D_MODEL), a_map=lambda i, j, k: (k, 0),
                   b_block=(None, tk, tn), b_map=lambda i, j, k: (j // nh, k, j % nh),
                   o_block=(D_MODEL, tn), o_map=lambda i, j, k: (0, j),
                   out_shape=jax.ShapeDtypeStruct((D_MODEL, 2 * D_FF), F32), name="d_w_up")


def _rms_fwd(x, g, tr=256):
    s, d = x.shape
    tr = _tile(s, tr, 8)

    def body(x_ref, g_ref, o_ref):
        xv = x_ref[...]
        r = lax.rsqrt(jnp.mean(xv * xv, axis=-1, keepdims=True) + EPS)
        o_ref[...] = (xv * r * g_ref[...]).astype(o_ref.dtype)

    return pl.pallas_call(
        body, name="rms_fwd", grid=(s // tr,),
        in_specs=[pl.BlockSpec((tr, d), lambda i: (i, 0)), pl.BlockSpec((1, d), lambda i: (0, 0))],
        out_specs=pl.BlockSpec((tr, d), lambda i: (i, 0)),
        out_shape=jax.ShapeDtypeStruct((s, d), BF16),
        compiler_params=_cparams(("parallel",)),
    )(x, g)


def _group_rms_fwd(o_fox, o_sb, g_fox, g_sb, tr=256):
    nh, s, dh = o_fox.shape
    tr = _tile(s, tr, 8)

    def body(a_ref, b_ref, ga_ref, gb_ref, o_ref):
        for src, g_ref, lo in ((a_ref, ga_ref, 0), (b_ref, gb_ref, nh * dh)):
            heads = [src[hh] for hh in range(nh)]
            ss = heads[0] * heads[0]
            for xv in heads[1:]:
                ss = ss + xv * xv
            r = lax.rsqrt(jnp.sum(ss, axis=-1, keepdims=True) * (1.0 / (nh * dh)) + EPS)
            for hh, xv in enumerate(heads):
                o_ref[:, lo + hh * dh:lo + (hh + 1) * dh] = (xv * r * g_ref[hh]).astype(o_ref.dtype)

    heads_blk = pl.BlockSpec((nh, tr, dh), lambda i: (0, i, 0))
    gain = pl.BlockSpec((nh, 1, dh), lambda i: (0, 0, 0))
    return pl.pallas_call(
        body, name="group_rms_fwd", grid=(s // tr,),
        in_specs=[heads_blk, heads_blk, gain, gain],
        out_specs=pl.BlockSpec((tr, 2 * nh * dh), lambda i: (i, 0)),
        out_shape=jax.ShapeDtypeStruct((s, 2 * nh * dh), BF16),
        compiler_params=_cparams(("parallel",)),
    )(o_fox, o_sb, g_fox, g_sb)


def _group_rms_bwd(x, dy, g, *, dy_col, name, tr=256):
    nh, s, dh = x.shape
    tr = _tile(s, tr, 8)
    d = nh * dh

    def body(x_ref, dy_ref, g_ref, dx_ref, dg_ref):
        @pl.when(pl.program_id(0) == 0)
        def _():
            dg_ref[...] = jnp.zeros_like(dg_ref)

        dyv = dy_ref[...]
        xs_ = [x_ref[hh] for hh in range(nh)]
        dys = [dyv[:, hh * dh:(hh + 1) * dh] for hh in range(nh)]
        ss = xs_[0] * xs_[0]
        for xv in xs_[1:]:
            ss = ss + xv * xv
        r = lax.rsqrt(jnp.sum(ss, axis=-1, keepdims=True) * (1.0 / d) + EPS)
        xh = [xv * r for xv in xs_]
        gy = [dys[hh] * g_ref[hh] for hh in range(nh)]
        dot = xh[0] * gy[0]
        for hh in range(1, nh):
            dot = dot + xh[hh] * gy[hh]
        mean_dot = jnp.sum(dot, axis=-1, keepdims=True) * (1.0 / d)
        for hh in range(nh):
            dx_ref[hh] = r * (gy[hh] - xh[hh] * mean_dot)
            dg_ref[hh] += jnp.sum(dys[hh] * xh[hh], axis=0, keepdims=True)

    heads_blk = pl.BlockSpec((nh, tr, dh), lambda i: (0, i, 0))
    gain = pl.BlockSpec((nh, 1, dh), lambda i: (0, 0, 0))
    return pl.pallas_call(
        body, name=name, grid=(s // tr,),
        in_specs=[heads_blk, pl.BlockSpec((tr, d), lambda i: (i, dy_col)), gain],
        out_specs=[heads_blk, gain],
        out_shape=[jax.ShapeDtypeStruct((nh, s, dh), F32), jax.ShapeDtypeStruct((nh, 1, dh), F32)],
        compiler_params=_cparams(("arbitrary",)),
    )(x, dy, g)


def _merge_dproj(parts_fox, d_gate, parts_sb, tr=256):
    nh, s, dh = parts_fox[0].shape
    tr = _tile(s, tr, 16)

    def body(*refs):
        o_ref = refs[-1]
        gate_ref = refs[3]
        col = 0
        for ref in refs[:3]:
            for hh in range(nh):
                o_ref[:, col:col + dh] = ref[hh].astype(o_ref.dtype)
                col += dh
        o_ref[:, col:col + GATE_PAD] = jnp.zeros((tr, GATE_PAD), o_ref.dtype)
        o_ref[:, col:col + N_GROUP_HEADS] = gate_ref[...].astype(o_ref.dtype)
        col += GATE_PAD
        for ref in refs[4:7]:
            for hh in range(nh):
                o_ref[:, col:col + dh] = ref[hh].astype(o_ref.dtype)
                col += dh

    heads_blk = pl.BlockSpec((nh, tr, dh), lambda i: (0, i, 0))
    return pl.pallas_call(
        body, name="merge_d_proj", grid=(s // tr,),
        in_specs=[heads_blk] * 3 + [pl.BlockSpec((tr, N_GROUP_HEADS), lambda i: (i, 0))] + [heads_blk] * 3,
        out_specs=pl.BlockSpec((tr, IN_COLS_PAD), lambda i: (i, 0)),
        out_shape=jax.ShapeDtypeStruct((s, IN_COLS_PAD), BF16),
        compiler_params=_cparams(("parallel",)),
    )(*parts_fox, d_gate, *parts_sb)


def _rms_bwd(x, dy, g, resid, *, dy_col, name, want_bf16, tr=256):
    s, d = x.shape
    tr = _tile(s, tr, 8)
    has_resid = resid is not None

    def body(*refs):
        refs = list(refs)
        x_ref, dy_ref, g_ref = refs[:3]
        r_ref = refs[3] if has_resid else None
        outs = refs[4:] if has_resid else refs[3:]
        dx_ref = outs[0]
        dxb_ref = outs[1] if want_bf16 else None
        dg_ref = outs[-1]

        @pl.when(pl.program_id(0) == 0)
        def _():
            dg_ref[...] = jnp.zeros_like(dg_ref)

        xv = x_ref[...]
        dyv = dy_ref[...]
        r = lax.rsqrt(jnp.mean(xv * xv, axis=-1, keepdims=True) + EPS)
        xh = xv * r
        gy = dyv * g_ref[...]
        dx = r * (gy - xh * jnp.mean(xh * gy, axis=-1, keepdims=True))
        if r_ref is not None:
            dx = r_ref[...] + dx
        dx_ref[...] = dx
        if dxb_ref is not None:
            dxb_ref[...] = dx.astype(BF16)
        dg_ref[...] += jnp.sum(dyv * xh, axis=0, keepdims=True)

    row = pl.BlockSpec((tr, d), lambda i: (i, 0))
    in_specs = [row, pl.BlockSpec((tr, d), lambda i: (i, dy_col)), pl.BlockSpec((1, d), lambda i: (0, 0))]
    args = [x, dy, g]
    if has_resid:
        in_specs.append(row)
        args.append(resid)
    out_specs = [row]
    out_shape = [jax.ShapeDtypeStruct((s, d), F32)]
    if want_bf16:
        out_specs.append(row)
        out_shape.append(jax.ShapeDtypeStruct((s, d), BF16))
    out_specs.append(pl.BlockSpec((1, d), lambda i: (0, 0)))
    out_shape.append(jax.ShapeDtypeStruct((1, d), F32))
    return pl.pallas_call(
        body, name=name, grid=(s // tr,), in_specs=in_specs, out_specs=out_specs, out_shape=out_shape,
        compiler_params=_cparams(("arbitrary",)),
    )(*args)


def _loss_head(x2, target, g, tr=256):
    s, d = x2.shape
    tr = _tile(s, tr, 8)

    def body(x_ref, t_ref, g_ref, dx_ref, dxb_ref, dg_ref, loss_ref):
        @pl.when(pl.program_id(0) == 0)
        def _():
            dg_ref[...] = jnp.zeros_like(dg_ref)
            loss_ref[...] = jnp.zeros_like(loss_ref)

        xv = x_ref[...]
        gv = g_ref[...]
        r = lax.rsqrt(jnp.mean(xv * xv, axis=-1, keepdims=True) + EPS)
        xh = xv * r
        err = xh * gv - t_ref[...]
        loss_ref[...] += jnp.sum(jnp.mean(err * err, axis=-1, keepdims=True), axis=0, keepdims=True) * 0.5
        dyv = err * (1.0 / d)
        gy = dyv * gv
        dx = r * (gy - xh * jnp.mean(xh * gy, axis=-1, keepdims=True))
        dx_ref[...] = dx
        dxb_ref[...] = dx.astype(BF16)
        dg_ref[...] += jnp.sum(dyv * xh, axis=0, keepdims=True)

    row = pl.BlockSpec((tr, d), lambda i: (i, 0))
    return pl.pallas_call(
        body, name="loss_head", grid=(s // tr,),
        in_specs=[row, row, pl.BlockSpec((1, d), lambda i: (0, 0))],
        out_specs=[row, row, pl.BlockSpec((1, d), lambda i: (0, 0)), pl.BlockSpec((1, LANES), lambda i: (0, 0))],
        out_shape=[jax.ShapeDtypeStruct((s, d), F32), jax.ShapeDtypeStruct((s, d), BF16),
                   jax.ShapeDtypeStruct((1, d), F32), jax.ShapeDtypeStruct((1, LANES), F32)],
        compiler_params=_cparams(("arbitrary",)),
    )(x2, target, g)


def _conv_taps(cur, prev8, w, b, first):
    prev8 = jnp.where(first, 0.0, prev8)
    ext = jnp.concatenate([prev8, cur], axis=0)
    x1 = pltpu.roll(ext, 1, 0)[8:]
    x2 = pltpu.roll(ext, 2, 0)[8:]
    u = b + w[0:1] * x2
    u = u + w[1:2] * x1
    u = u + w[2:3] * cur
    return u, x1, x2


def _conv_gate_fwd(up, conv_w, conv_b, tm=1024, tn=256):
    s = up.shape[1]
    tm = _tile(s, tm, 8)
    nrb = s // tm
    rb8 = tm // 8

    def body(g_ref, v_ref, gp_ref, vp_ref, wg_ref, wv_ref, bg_ref, bv_ref, o_ref):
        first = pl.program_id(1) == 0
        ug, _, _ = _conv_taps(g_ref[...], gp_ref[...], wg_ref[...], bg_ref[...], first)
        uv, _, _ = _conv_taps(v_ref[...], vp_ref[...], wv_ref[...], bv_ref[...], first)
        sg = 1.0 / (1.0 + jnp.exp(-ug))
        o_ref[...] = (ug * sg * uv).astype(o_ref.dtype)

    def cur(h):
        return pl.BlockSpec((None, tm, tn), lambda j, i: (h, i, j))

    def prev(h):
        return pl.BlockSpec((None, 8, tn), lambda j, i: (h, jnp.maximum(i * rb8 - 1, 0), j))

    def par(h, r):
        return pl.BlockSpec((None, r, tn), lambda j, i: (h, 0, j))

    return pl.pallas_call(
        body, name="conv_gate_fwd", grid=(D_FF // tn, nrb),
        in_specs=[cur(0), cur(1), prev(0), prev(1), par(0, 3), par(1, 3), par(0, 1), par(1, 1)],
        out_specs=pl.BlockSpec((tm, tn), lambda j, i: (i, j)),
        out_shape=jax.ShapeDtypeStruct((s, D_FF), BF16),
        compiler_params=_cparams(("parallel", "parallel")),
    )(up, up, up, up, conv_w, conv_w, conv_b, conv_b)


def _conv_gate_bwd(up, dact, conv_w, conv_b, tm=512, tn=256):
    s = up.shape[1]
    tm = _tile(s, tm, 8)
    nrb = s // tm
    rb8 = tm // 8

    def body(g_ref, v_ref, gp_ref, vp_ref, da_ref, wg_ref, wv_ref, bg_ref, bv_ref,
             dup_ref, dcw_ref, dcb_ref, carry_ref):
        i = pl.program_id(1)
        first = i == nrb - 1

        @pl.when(i == 0)
        def _():
            carry_ref[...] = jnp.zeros_like(carry_ref)
            dcw_ref[...] = jnp.zeros_like(dcw_ref)
            dcb_ref[...] = jnp.zeros_like(dcb_ref)

        curs = (g_ref[...], v_ref[...])
        ws = (wg_ref[...], wv_ref[...])
        ug, g1, g2 = _conv_taps(curs[0], gp_ref[...], ws[0], bg_ref[...], first)
        uv, v1, v2 = _conv_taps(curs[1], vp_ref[...], ws[1], bv_ref[...], first)
        sg = 1.0 / (1.0 + jnp.exp(-ug))
        da = da_ref[...].astype(F32)
        d_v = da * (ug * sg)
        d_g = da * uv * (sg * (1.0 + ug * (1.0 - sg)))
        for h, (du, x0, x1, x2) in enumerate(((d_g, curs[0], g1, g2), (d_v, curs[1], v1, v2))):
            dcb_ref[h] += jnp.sum(du, axis=0, keepdims=True)
            dcw_ref[h, 0:1, :] += jnp.sum(du * x2, axis=0, keepdims=True)
            dcw_ref[h, 1:2, :] += jnp.sum(du * x1, axis=0, keepdims=True)
            dcw_ref[h, 2:3, :] += jnp.sum(du * x0, axis=0, keepdims=True)
            ext = jnp.concatenate([du, carry_ref[h]], axis=0)
            n1 = pltpu.roll(ext, tm + 7, 0)[:tm]
            n2 = pltpu.roll(ext, tm + 6, 0)[:tm]
            w = ws[h]
            dup_ref[h] = (w[2:3] * du + w[1:2] * n1 + w[0:1] * n2).astype(dup_ref.dtype)
            carry_ref[h] = du[:8]

    def cur(h):
        return pl.BlockSpec((None, tm, tn), lambda j, i: (h, nrb - 1 - i, j))

    def prev(h):
        return pl.BlockSpec((None, 8, tn), lambda j, i: (h, jnp.maximum((nrb - 1 - i) * rb8 - 1, 0), j))

    def par(h, r):
        return pl.BlockSpec((None, r, tn), lambda j, i: (h, 0, j))

    return pl.pallas_call(
        body, name="conv_gate_bwd", grid=(D_FF // tn, nrb),
        in_specs=[cur(0), cur(1), prev(0), prev(1),
                  pl.BlockSpec((tm, tn), lambda j, i: (nrb - 1 - i, j)),
                  par(0, 3), par(1, 3), par(0, 1), par(1, 1)],
        out_specs=[pl.BlockSpec((2, tm, tn), lambda j, i: (0, nrb - 1 - i, j)),
                   pl.BlockSpec((2, 3, tn), lambda j, i: (0, 0, j)),
                   pl.BlockSpec((2, 1, tn), lambda j, i: (0, 0, j))],
        out_shape=[jax.ShapeDtypeStruct((2, s, D_FF), BF16),
                   jax.ShapeDtypeStruct((2, 3, D_FF), F32),
                   jax.ShapeDtypeStruct((2, 1, D_FF), F32)],
        scratch_shapes=[pltpu.VMEM((2, 8, tn), F32)],
        compiler_params=_cparams(("parallel", "arbitrary")),
    )(up, up, up, up, dact, conv_w, conv_w, conv_b, conv_b)


def _split_dot(x, tri, terms):
    piece = x.astype(BF16)
    out = jnp.dot(piece, tri, preferred_element_type=F32)
    rest = x
    for _ in range(terms - 1):
        rest = rest - piece.astype(F32)
        piece = rest.astype(BF16)
        out = out + jnp.dot(piece, tri, preferred_element_type=F32)
    return out


def _split_dot_rhs(tri, x, terms):
    piece = x.astype(BF16)
    out = jnp.dot(tri, piece, preferred_element_type=F32)
    rest = x
    for _ in range(terms - 1):
        rest = rest - piece.astype(F32)
        piece = rest.astype(BF16)
        out = out + jnp.dot(tri, piece, preferred_element_type=F32)
    return out


def _tri(n, kind):
    r = lax.broadcasted_iota(jnp.int32, (n, n), 0)
    c = lax.broadcasted_iota(jnp.int32, (n, n), 1)
    cond = {"le": r <= c, "ge": r >= c, "lt": r < c, "gt": r > c}[kind]
    return jnp.where(cond, 1.0, 0.0).astype(BF16)


def _log_sigmoid(x):
    return jnp.minimum(x, 0.0) - jnp.log(1.0 + jnp.exp(-jnp.abs(x)))


def _forget_fwd(f_logit, bias):
    h, r, _ = f_logit.shape

    def body(x_ref, b_ref, o_ref):
        lf = _log_sigmoid(x_ref[...] + b_ref[...])
        within = _split_dot(lf, _tri(LANES, "le"), 3)
        row_tot = jnp.broadcast_to(within[:, LANES - 1:LANES], (r, LANES))
        before = _split_dot_rhs(_tri(r, "gt"), row_tot, 3)
        o_ref[...] = within + before

    blk = pl.BlockSpec((None, r, LANES), lambda i: (i, 0, 0))
    return pl.pallas_call(
        body, name="forget_cumsum_fwd", grid=(h,),
        in_specs=[blk, pl.BlockSpec((None, 1, LANES), lambda i: (i, 0, 0))],
        out_specs=blk, out_shape=jax.ShapeDtypeStruct((h, r, LANES), F32),
        compiler_params=_cparams(("parallel",)),
    )(f_logit, bias)


def _forget_bwd(f_logit, bias, ksum, qsum):
    h, r, _ = f_logit.shape

    def body(x_ref, b_ref, k_ref, q_ref, dx_ref, db_ref):
        d_f = q_ref[...] - k_ref[...]
        within = _split_dot(d_f, _tri(LANES, "ge"), 3)
        row_tot = jnp.broadcast_to(within[:, 0:1], (r, LANES))
        after = _split_dot_rhs(_tri(r, "lt"), row_tot, 3)
        xv = x_ref[...] + b_ref[...]
        dx = (within + after) * jnp.exp(_log_sigmoid(-xv))
        dx_ref[...] = dx
        db_ref[...] = jnp.broadcast_to(jnp.sum(dx), (1, LANES))

    blk = pl.BlockSpec((None, r, LANES), lambda i: (i, 0, 0))
    one = pl.BlockSpec((None, 1, LANES), lambda i: (i, 0, 0))
    return pl.pallas_call(
        body, name="forget_cumsum_bwd", grid=(h,),
        in_specs=[blk, one, blk, blk], out_specs=[blk, one],
        out_shape=[jax.ShapeDtypeStruct((h, r, LANES), F32), jax.ShapeDtypeStruct((h, 1, LANES), F32)],
        compiler_params=_cparams(("parallel",)),
    )(f_logit, bias, ksum, qsum)


def _head_specs(s, tq):
    qblk = pl.BlockSpec((None, tq, HEAD_DIM), lambda h, i: (h, i, 0))
    full = pl.BlockSpec((None, s, HEAD_DIM), lambda h, i: (h, 0, 0))
    col = pl.BlockSpec((None, tq, 1), lambda h, i: (h, i, 0))
    return qblk, full, col


def _qkv_specs(s, tq, offs):
    q_off, k_off, v_off = offs
    return (pl.BlockSpec((None, tq, HEAD_DIM), lambda h, i: (h + q_off, i, 0)),
            pl.BlockSpec((None, s, HEAD_DIM), lambda h, i: (h + k_off, 0, 0)),
            pl.BlockSpec((None, s, HEAD_DIM), lambda h, i: (h + v_off, 0, 0)))


def _scaled(q_ref):
    return (q_ref[...].astype(F32) * Q_SCALE).astype(BF16)


_NT = (((1,), (1,)), ((), ()))
_TN = (((0,), (0,)), ((), ()))


def _cols_minus_rows(rows, cols):
    return lax.broadcasted_iota(jnp.int32, (rows, cols), 1) - lax.broadcasted_iota(jnp.int32, (rows, cols), 0)


def _fox_fwd(qkv, offs, v_ones, f_col, f_row, tq, tk):
    h, s = N_GROUP_HEADS, qkv.shape[1]
    nk = s // tk
    assert tq == tk

    def body(q_ref, k_ref, v_ref, fc_ref, fr_ref, o_ref, lse_ref, m_ref, acc_ref, z0, z1):
        i = pl.program_id(1)
        qs = _scaled(q_ref)
        fq = fc_ref[...]
        m_ref[...] = jnp.full_like(m_ref, NEG_BIG)
        acc_ref[...] = jnp.zeros_like(acc_ref)

        ahead = _cols_minus_rows(tq, tk)

        def block_of(j):
            return jnp.minimum(j, nk - 1)

        def keys_of(j):
            return pl.ds(pl.multiple_of(block_of(j) * tk, tk), tk)

        def logits(j):
            return lax.dot_general(qs, k_ref[keys_of(j), :], _NT, preferred_element_type=F32)

        def soft(j, raw, masked):
            sc = raw + fq - fr_ref[block_of(j)]
            if masked:
                sc = jnp.where(ahead <= (i - j) * tk, sc, NEG_BIG)
            m_old = m_ref[...]
            m_new = jnp.maximum(m_old, jnp.max(sc, axis=-1, keepdims=True))
            p = jnp.exp(sc - m_new)
            acc_ref[...] = jnp.exp(m_old - m_new) * acc_ref[...] + jnp.dot(
                p.astype(BF16), v_ref[keys_of(j), :], preferred_element_type=F32)
            m_ref[...] = m_new

        z0[...] = logits(0)

        def trip(p, masked):
            j = 2 * p
            z1[...] = logits(j + 1)
            soft(j, z0[...], masked)
            z0[...] = logits(j + 2)
            soft(j + 1, z1[...], masked)

        def step(p, carry):
            trip(p, False)
            return carry

        lax.fori_loop(0, i // 2, step, 0)
        trip(i // 2, True)
        l = acc_ref[:, HEAD_DIM:HEAD_DIM + 1]
        o_ref[...] = acc_ref[:, :HEAD_DIM] / l
        lse_ref[...] = m_ref[...] + jnp.log(l)

    qblk, full, colspec = _head_specs(s, tq)
    q_in, k_in, _ = _qkv_specs(s, tq, offs)
    return pl.pallas_call(
        body, name="fox_fwd", grid=(h, s // tq),
        in_specs=[q_in, k_in, pl.BlockSpec((None, s, 2 * HEAD_DIM), lambda hh, i: (hh, 0, 0)), colspec,
                  pl.BlockSpec((None, nk, 1, tk), lambda hh, i: (hh, 0, 0, 0))],
        out_specs=[qblk, colspec],
        out_shape=[jax.ShapeDtypeStruct((h, s, HEAD_DIM), F32), jax.ShapeDtypeStruct((h, s, 1), F32)],
        scratch_shapes=[pltpu.VMEM((tq, 1), F32), pltpu.VMEM((tq, 2 * HEAD_DIM), F32),
                        pltpu.VMEM((tq, tk), F32), pltpu.VMEM((tq, tk), F32)],
        compiler_params=_cparams(("parallel", "parallel")),
    )(qkv, qkv, v_ones, f_col, f_row)


def _fox_bwd(qkv, offs, f_col, f_row, o, lse, d_o, tq, tk):
    h, s = N_GROUP_HEADS, qkv.shape[1]
    nk = s // tk
    assert tq == tk

    def body(q_ref, k_ref, v_ref, fc_ref, fr_ref, o_ref, lse_ref, do_ref,
             dq_ref, dk_ref, dv_ref, ks_ref, qs_ref, dq_acc, qsum_acc, z0, z1, p0, p1):
        i = pl.program_id(1)

        @pl.when(i == 0)
        def _():
            dk_ref[...] = jnp.zeros_like(dk_ref)
            dv_ref[...] = jnp.zeros_like(dv_ref)
            ks_ref[...] = jnp.zeros_like(ks_ref)

        qs = _scaled(q_ref)
        fq = fc_ref[...]
        lse_v = lse_ref[...]
        dob = do_ref[...].astype(BF16)
        delta = jnp.sum(dob.astype(F32) * o_ref[...], axis=-1, keepdims=True)
        dq_acc[...] = jnp.zeros_like(dq_acc)
        qsum_acc[...] = jnp.zeros_like(qsum_acc)

        ahead = _cols_minus_rows(tq, tk)

        def block_of(j):
            return jnp.minimum(j, nk - 1)

        def keys_of(j):
            return pl.ds(pl.multiple_of(block_of(j) * tk, tk), tk)

        def products(j):
            at = keys_of(j)
            return (lax.dot_general(qs, k_ref[at, :], _NT, preferred_element_type=F32),
                    lax.dot_general(dob, v_ref[at, :], _NT, preferred_element_type=F32))

        def grads(j, raw, dp, masked):
            at = keys_of(j)
            sc = raw + fq - fr_ref[block_of(j)]
            if masked:
                sc = jnp.where(ahead <= (i - j) * tk, sc, NEG_BIG)
            p = jnp.exp(sc - lse_v)
            ds = p * (dp - delta)
            dsb = ds.astype(BF16)
            dq_acc[...] += jnp.dot(dsb, k_ref[at, :], preferred_element_type=F32)
            dk_ref[at, :] += lax.dot_general(dsb, qs, _TN, preferred_element_type=F32)
            dv_ref[at, :] += lax.dot_general(p.astype(BF16), dob, _TN, preferred_element_type=F32)
            ks_ref[block_of(j)] += jnp.sum(ds.reshape(tq // 8, 8, tk), axis=0)
            qsum_acc[...] += jnp.sum(ds, axis=-1, keepdims=True)

        z0[...], p0[...] = products(0)

        def trip(pp, masked):
            j = 2 * pp
            z1[...], p1[...] = products(j + 1)
            grads(j, z0[...], p0[...], masked)
            z0[...], p0[...] = products(j + 2)
            grads(j + 1, z1[...], p1[...], masked)

        def step(pp, carry):
            trip(pp, False)
            return carry

        lax.fori_loop(0, i // 2, step, 0)
        trip(i // 2, True)
        dq_ref[...] = dq_acc[...] * Q_SCALE
        qs_ref[...] = qsum_acc[...]

    qblk, full, colspec = _head_specs(s, tq)
    frow = pl.BlockSpec((None, nk, 1, tk), lambda hh, i: (hh, 0, 0, 0))
    big = pltpu.VMEM((tq, tk), F32)
    return pl.pallas_call(
        body, name="fox_bwd", grid=(h, s // tq),
        in_specs=[*_qkv_specs(s, tq, offs), colspec, frow, qblk, colspec, qblk],
        out_specs=[qblk, full, full, pl.BlockSpec((None, nk, 8, tk), lambda hh, i: (hh, 0, 0, 0)), colspec],
        out_shape=[jax.ShapeDtypeStruct((h, s, HEAD_DIM), F32)] * 3
        + [jax.ShapeDtypeStruct((h, nk, 8, tk), F32), jax.ShapeDtypeStruct((h, s, 1), F32)],
        scratch_shapes=[pltpu.VMEM((tq, HEAD_DIM), F32), pltpu.VMEM((tq, 1), F32), big, big, big, big],
        compiler_params=_cparams(("parallel", "arbitrary")),
    )(qkv, qkv, qkv, f_col, f_row, o, lse, d_o)


SB_TERMS = 2
G_TERMS = 1
LOG2E = 1.4426950408889634
LN2 = 0.6931471805599453


def _softplus2(z2):
    return jnp.maximum(z2, 0.0) + jnp.log2(1.0 + jnp.exp2(-jnp.abs(z2)))


def _sb_fwd(qkv, offs, tq, tk):
    h, s = N_GROUP_HEADS, qkv.shape[1]

    assert tq == 2 * tk

    def body(q_ref, k_ref, v_ref, o_ref, tot_ref, acc_ref, run_ref, z0, z1, d0, d1, t0, t1):
        z_refs, d_refs, t_refs = (z0, z1), (d0, d1), (t0, t1)
        i = pl.program_id(1)
        qs = _scaled(q_ref)
        tri = _tri(tk, "ge")
        acc_ref[...] = jnp.zeros_like(acc_ref)
        run_ref[...] = jnp.zeros_like(run_ref)
        nb = (i + 1) * (tq // tk)
        ahead = _cols_minus_rows(tq, tk)

        def keys_of(b):
            j = nb - 1 - jnp.minimum(b, nb - 1)
            return pl.ds(pl.multiple_of(j * tk, tk), tk)

        def visible(b):
            return ahead < i * tq - (nb - 1 - b) * tk

        def logits(b, slot):
            z_refs[slot][...] = lax.dot_general(qs, k_ref[keys_of(b), :], _NT,
                                                preferred_element_type=F32) * LOG2E

        def sums(b, slot, masked):
            z2 = z_refs[slot][...]
            sp = _softplus2(z2)
            if masked:
                sp = jnp.where(visible(b), sp, 0.0)
            inc = _split_dot(sp, tri, SB_TERMS)
            d_refs[slot][...] = z2 - inc
            t_refs[slot][...] = inc[:, 0:1]

        def weigh(b, slot, masked):
            w = jnp.exp2(d_refs[slot][...] - run_ref[...])
            if masked:
                w = jnp.where(visible(b), w, 0.0)
            acc_ref[...] += jnp.dot(w.astype(BF16), v_ref[keys_of(b), :], preferred_element_type=F32)
            run_ref[...] += t_refs[slot][...]

        def trip(p, masked):
            b = 2 * p
            logits(b + 2, 0)
            sums(b + 1, 1, masked)
            weigh(b, 0, masked)
            logits(b + 3, 1)
            sums(b + 2, 0, masked)
            weigh(b + 1, 1, masked)

        logits(0, 0)
        logits(1, 1)
        sums(0, 0, True)
        trip(0, True)

        def step(p, carry):
            trip(p, False)
            return carry

        lax.fori_loop(1, nb // 2, step, 0)
        o_ref[...] = acc_ref[...]
        tot_ref[...] = run_ref[...] * (-LN2)

    qblk, full, colspec = _head_specs(s, tq)
    return pl.pallas_call(
        body, name="sb_fwd", grid=(h, s // tq),
        in_specs=[*_qkv_specs(s, tq, offs)], out_specs=[qblk, colspec],
        out_shape=[jax.ShapeDtypeStruct((h, s, HEAD_DIM), F32), jax.ShapeDtypeStruct((h, s, 1), F32)],
        scratch_shapes=[pltpu.VMEM((tq, HEAD_DIM), F32), pltpu.VMEM((tq, 1), F32),
                        pltpu.VMEM((tq, tk), F32), pltpu.VMEM((tq, tk), F32),
                        pltpu.VMEM((tq, tk), F32), pltpu.VMEM((tq, tk), F32),
                        pltpu.VMEM((tq, 1), F32), pltpu.VMEM((tq, 1), F32)],
        compiler_params=_cparams(("parallel", "parallel")),
    )(qkv, qkv, qkv)


def _sb_bwd(qkv, offs, tot, d_o, tq, tk):
    h, s = N_GROUP_HEADS, qkv.shape[1]

    assert tq == 2 * tk

    def body(q_ref, k_ref, v_ref, tot_ref, do_ref, dq_ref, dk_ref, dv_ref, dq_acc, off_ref, grun_ref,
             z0, z1, p0, p1, u0, u1, b0, b1, t0, t1):
        z_refs, p_refs, u_refs, b_refs, t_refs = (z0, z1), (p0, p1), (u0, u1), (b0, b1), (t0, t1)
        i = pl.program_id(1)

        @pl.when(i == 0)
        def _():
            dk_ref[...] = jnp.zeros_like(dk_ref)
            dv_ref[...] = jnp.zeros_like(dv_ref)

        qs = _scaled(q_ref)
        dob = do_ref[...].astype(BF16)
        tri = _tri(tk, "le")
        dq_acc[...] = jnp.zeros_like(dq_acc)
        off_ref[...] = tot_ref[...] * LOG2E
        grun_ref[...] = jnp.zeros_like(grun_ref)
        nb = (i + 1) * (tq // tk)
        ahead = _cols_minus_rows(tq, tk)

        def keys_of(b):
            return pl.ds(pl.multiple_of(jnp.minimum(b, nb - 1) * tk, tk), tk)

        def visible(b):
            return ahead < i * tq - b * tk

        def logits(b, slot):
            z_refs[slot][...] = lax.dot_general(qs, k_ref[keys_of(b), :], _NT,
                                                preferred_element_type=F32) * LOG2E

        def sums(b, slot, masked):
            z2 = z_refs[slot][...]
            sp = _softplus2(z2)
            lb2 = z2 - sp
            linc = _split_dot(jnp.where(visible(b), sp, 0.0) if masked else sp, tri, SB_TERMS)
            p_refs[slot][...] = lax.dot_general(dob, v_ref[keys_of(b), :], _NT, preferred_element_type=F32)
            u_refs[slot][...] = lb2 + linc
            b_refs[slot][...] = jnp.exp2(lb2)
            t_refs[slot][...] = linc[:, tk - 1:tk]

        def weigh(b, slot, masked):
            w = jnp.exp2(u_refs[slot][...] + off_ref[...])
            if masked:
                w = jnp.where(visible(b), w, 0.0)
            g = w * p_refs[slot][...]
            return w, g, _split_dot(g, tri, G_TERMS)

        def finish(b, slot, w, g, ginc, masked):
            at = keys_of(b)
            dz = g - b_refs[slot][...] * (grun_ref[...] + ginc)
            if masked:
                dz = jnp.where(visible(b), dz, 0.0)
            dzb = dz.astype(BF16)
            dq_acc[...] += jnp.dot(dzb, k_ref[at, :], preferred_element_type=F32)
            dk_ref[at, :] += lax.dot_general(dzb, qs, _TN, preferred_element_type=F32)
            dv_ref[at, :] += lax.dot_general(w.astype(BF16), dob, _TN, preferred_element_type=F32)
            off_ref[...] += t_refs[slot][...]
            grun_ref[...] += ginc[:, tk - 1:tk]

        def trip(p, masked):
            for slot in (0, 1):
                b = 2 * p + slot
                w, g, ginc = weigh(b, slot, masked)
                logits(b + 2, slot)
                sums(b + 1, 1 - slot, masked)
                finish(b, slot, w, g, ginc, masked)

        logits(0, 0)
        logits(1, 1)
        sums(0, 0, True)
        n_plain = jnp.maximum(nb // 2 - 2, 0)

        def plain(p, carry):
            trip(p, False)
            return carry

        def guarded(p, carry):
            trip(p, True)
            return carry

        lax.fori_loop(0, n_plain, plain, 0)
        lax.fori_loop(n_plain, nb // 2, guarded, 0)
        dq_ref[...] = dq_acc[...] * Q_SCALE

    qblk, full, colspec = _head_specs(s, tq)
    big = pltpu.VMEM((tq, tk), F32)
    return pl.pallas_call(
        body, name="sb_bwd", grid=(h, s // tq),
        in_specs=[*_qkv_specs(s, tq, offs), colspec, qblk], out_specs=[qblk, full, full],
        out_shape=[jax.ShapeDtypeStruct((h, s, HEAD_DIM), F32)] * 3,
        scratch_shapes=[pltpu.VMEM((tq, HEAD_DIM), F32), pltpu.VMEM((tq, 1), F32), pltpu.VMEM((tq, 1), F32)]
        + [big] * 8 + [pltpu.VMEM((tq, 1), F32)] * 2,
        compiler_params=_cparams(("parallel", "arbitrary")),
    )(qkv, qkv, qkv, tot, d_o)


def _sum_adamw(parts, w, m, v, name, tr=256):
    _, rows, lanes = parts.shape
    tr = _tile(rows, tr, 16)
    c_m = 1.0 - ADAM_B1 ** ADAM_STEP
    c_v = 1.0 - ADAM_B2 ** ADAM_STEP

    def body(p_ref, w_ref, m_ref, v_ref, g_ref, d_ref, nm_ref, nv_ref):
        g = p_ref[0].astype(F32)
        for j in range(1, N_DEV):
            g = g + p_ref[j].astype(F32)
        nm = ADAM_B1 * m_ref[...] + (1.0 - ADAM_B1) * g
        nv = ADAM_B2 * v_ref[...] + (1.0 - ADAM_B2) * (g * g)
        m_hat = nm / c_m
        v_hat = nv / c_v
        g_ref[...] = g
        d_ref[...] = -ADAM_LR * (m_hat / (jnp.sqrt(v_hat) + ADAM_EPS) + ADAM_WD * w_ref[...])
        nm_ref[...] = nm
        nv_ref[...] = nv

    blk = pl.BlockSpec((tr, lanes), lambda i: (i, 0))
    return pl.pallas_call(
        body, name=name, grid=(rows // tr,),
        in_specs=[pl.BlockSpec((N_DEV, tr, lanes), lambda i: (0, i, 0)), blk, blk, blk],
        out_specs=[blk] * 4, out_shape=[jax.ShapeDtypeStruct((rows, lanes), F32)] * 4,
        compiler_params=_cparams(("parallel",)),
    )(parts, w, m, v)


def kernel(x, attn_norm_g, w_in, forget_bias, fox_out_g, sb_out_g, w_out, ffn_norm_g, w_up, conv_w, conv_b, w_down, final_norm_g, loss_target, m_attn_norm_g, m_w_in, m_forget_bias, m_fox_out_g, m_sb_out_g, m_w_out, m_ffn_norm_g, m_w_up, m_conv_w, m_conv_b, m_w_down, m_final_norm_g, v_attn_norm_g, v_w_in, v_forget_bias, v_fox_out_g, v_sb_out_g, v_w_out, v_ffn_norm_g, v_w_up, v_conv_w, v_conv_b, v_w_down, v_final_norm_g):
    s = x.shape[1]
    xs = x[0]
    tq = min(ATTN_TQ, s)
    tk_fox = min(FOX_TK, s)
    tk_sb = min(SB_TK, s)
    in_shard, up_shard, out_shard, down_shard = IN_COLS // N_DEV, 2 * D_FF // N_DEV, D_MODEL // N_DEV, D_FF // N_DEV

    cw = conv_w[0]
    cw_hi = cw.astype(BF16)
    cw_lo = (cw - cw_hi.astype(F32)).astype(BF16)
    (g_in,) = _all_gather([w_in[0].astype(BF16)])
    rest = _exchange_start([w_out[0].astype(BF16), w_up[0].astype(BF16), w_down[0].astype(BF16),
                            jnp.stack([cw_hi, cw_lo])], False, "weights_rest_start")
    n_gate = QKV_W + N_GROUP_HEADS
    in_windows = _col_windows(N_DEV, in_shard, gap_at=n_gate, gap=GATE_PAD - N_GROUP_HEADS)
    up_windows = _col_windows(N_DEV, up_shard)
    w_in_p = _assemble_cols(g_in, IN_COLS_PAD, in_windows, "assemble_w_in")
    conv_b2 = conv_b.reshape(2, 1, D_FF)

    h1 = _rms_fwd(xs, attn_norm_g + rest[-1][0:1, 0:1])
    proj_h = _mm_heads(h1, w_in_p, "in_proj")
    fox_offs = (0, N_GROUP_HEADS, 2 * N_GROUP_HEADS)
    sb_first = 3 * N_GROUP_HEADS + GATE_PAD // HEAD_DIM
    sb_offs = (sb_first, sb_first + N_GROUP_HEADS, sb_first + 2 * N_GROUP_HEADS)
    f_logit = _mm_nn(h1, w_in_p[:, QKV_W:QKV_W + GATE_PAD], F32, "gate_proj")[:, :N_GROUP_HEADS]
    fv = proj_h[2 * N_GROUP_HEADS:3 * N_GROUP_HEADS]

    f_logit_h = f_logit.T.reshape(N_GROUP_HEADS, s // LANES, LANES)
    bias_h = jnp.broadcast_to(forget_bias.reshape(N_GROUP_HEADS, 1, 1), (N_GROUP_HEADS, 1, LANES))
    big_f = _forget_fwd(f_logit_h, bias_h)
    f_col = big_f.reshape(N_GROUP_HEADS, s, 1)
    f_row = big_f.reshape(N_GROUP_HEADS, s // tk_fox, 1, tk_fox)

    fv_ones = jnp.concatenate([fv, jnp.ones_like(fv)], axis=-1)
    o_fox_h, lse = _fox_fwd(proj_h, fox_offs, fv_ones, f_col, f_row, tq, tk_fox)
    o_sb_h, sb_tot = _sb_fwd(proj_h, sb_offs, tq, tk_sb)
    g_fox_h = fox_out_g.reshape(N_GROUP_HEADS, 1, HEAD_DIM)
    g_sb_h = sb_out_g.reshape(N_GROUP_HEADS, 1, HEAD_DIM)
    o_n = _group_rms_fwd(o_fox_h, o_sb_h, g_fox_h, g_sb_h)
    g_out, g_up, g_down, g_conv = _exchange_wait(rest, False, o_n, "weights_rest_wait")
    w_out_f = g_out.reshape(D_MODEL, D_MODEL)
    w_up_f = _assemble_cols(g_up, 2 * D_FF, up_windows, "assemble_w_up")
    w_down_f = g_down.reshape(D_FF, D_MODEL)
    conv_w_f = (g_conv[:, 0].astype(F32) + g_conv[:, 1].astype(F32)).transpose(1, 0, 2).reshape(3, 2 * D_FF)
    conv_w2 = conv_w_f.reshape(3, 2, D_FF).transpose(1, 0, 2)
    x1 = _mm_nn(o_n, w_out_f, F32, "out_proj", resid=xs)
    h2 = _rms_fwd(x1, ffn_norm_g)
    up = _mm_up(h2, w_up_f)
    act = _conv_gate_fwd(up, conv_w2, conv_b2)
    x2 = _mm_nn(act, w_down_f, F32, "down_proj", resid=x1, tk=1408)

    d_x2, d_x2b, dg_final, loss_part = _loss_head(x2, loss_target[0], final_norm_g.reshape(1, D_MODEL))
    d_act = _mm_nt(d_x2b, w_down_f, BF16, "d_act", tn=1408)
    dw_down = _mm_tn(act, d_x2b, "d_w_down", tm=1408)
    d_up, dcw2, dcb2 = _conv_gate_bwd(up, d_act, conv_w2, conv_b2)
    d_h2 = _mm_dup_nt(d_up, w_up_f)
    dw_up = _mm_dwup_tn(h2, d_up)
    d_x1, d_x1b, dg_ffn = _rms_bwd(x1, d_h2, ffn_norm_g, d_x2, dy_col=0, name="ffn_norm_bwd", want_bf16=True)
    d_on = _mm_nt(d_x1b, w_out_f, F32, "d_o_normed")
    dw_out = _mm_tn(o_n, d_x1b, "d_w_out")
    early = _exchange_start(
        [dw_out.astype(BF16).reshape(N_DEV, out_shard, D_MODEL),
         _split_cols(dw_up, N_DEV, up_shard, up_windows, "split_d_w_up"),
         dw_down.astype(BF16).reshape(N_DEV, down_shard, D_MODEL)], True, "grads_early_start")
    g_fox_t = g_fox_h + early[-1][0:1, 0:1]
    d_o_fox_h, dg_fox = _group_rms_bwd(o_fox_h, d_on, g_fox_t, dy_col=0, name="fox_norm_bwd")
    d_o_sb_h, dg_sb = _group_rms_bwd(o_sb_h, d_on, g_sb_h, dy_col=1, name="sb_norm_bwd")

    dfq, dfk, dfv, ksum8, qsum = _fox_bwd(proj_h, fox_offs, f_col, f_row, o_fox_h, lse, d_o_fox_h, tq, tk_fox)
    dsq, dsk, dsv = _sb_bwd(proj_h, sb_offs, sb_tot, d_o_sb_h, tq, tk_sb)
    ksum = jnp.sum(ksum8, axis=2).reshape(N_GROUP_HEADS, s // LANES, LANES)
    d_f_logit_h, d_bias_h = _forget_bwd(f_logit_h, bias_h, ksum,
                                        qsum.reshape(N_GROUP_HEADS, s // LANES, LANES))
    d_f_logit = d_f_logit_h.reshape(N_GROUP_HEADS, s).T

    d_proj = _merge_dproj((dfq, dfk, dfv), d_f_logit, (dsq, dsk, dsv))
    dw_in_p = _mm_tn(h1, d_proj, "d_w_in", tn=640)
    dconv_w = dcw2.transpose(1, 0, 2).reshape(3, 2 * D_FF)
    dconv_b = dcb2.reshape(1, 2 * D_FF)
    late = _exchange_start(
        [_split_cols(dw_in_p, N_DEV, in_shard, in_windows, "split_d_w_in"),
         dconv_w.astype(BF16).reshape(3, N_DEV, up_shard).transpose(1, 0, 2)],
        True, "grads_late_start")
    d_h1 = _mm_nt(d_proj, w_in_p + late[-1][0:1, 0:1].astype(BF16), F32, "d_h1", tk=640)
    grad_x, dg_attn = _rms_bwd(xs, d_h1, attn_norm_g, d_x1, dy_col=0, name="attn_norm_bwd", want_bf16=False)

    small_shapes = [(1, D_MODEL), (1, N_GROUP_HEADS), (1, GROUP_W), (1, GROUP_W), (1, D_MODEL),
                    (1, 2 * D_FF), (D_MODEL,), (1,)]
    spack = _pack([dg_attn, d_bias_h[:, 0, 0], dg_fox, dg_sb, dg_ffn, dconv_b, dg_final, loss_part[0, 0:1]],
                  SMALL_ROWS, F32)
    (srecv,) = _grad_exchange([], spack)
    r_out, r_up, r_down = _exchange_wait(early, True, srecv, "grads_early_wait")
    r_in, r_conv = _exchange_wait(late, True, r_out, "grads_late_wait")

    big = [_sum_adamw(g, w_[0], m_[0], v_[0], "adamw_" + tag)
           for g, w_, m_, v_, tag in zip(
               (r_in, r_out, r_up, r_down, r_conv), (w_in, w_out, w_up, w_down, conv_w), (m_w_in, m_w_out, m_w_up, m_w_down, m_conv_w),
               (v_w_in, v_w_out, v_w_up, v_w_down, v_conv_w), ("w_in", "w_out", "w_up", "w_down", "conv_w"))]

    def small_pack(a_attn, a_bias, a_fox, a_sb, a_ffn, a_cb, a_fin):
        return _pack([a_attn, a_bias, a_fox, a_sb, a_ffn, a_cb, a_fin, jnp.zeros((1,), F32)], SMALL_ROWS, F32)

    small = _sum_adamw(srecv, small_pack(attn_norm_g, forget_bias, fox_out_g, sb_out_g, ffn_norm_g, conv_b, final_norm_g),
                       small_pack(m_attn_norm_g, m_forget_bias, m_fox_out_g, m_sb_out_g, m_ffn_norm_g, m_conv_b, m_final_norm_g),
                       small_pack(v_attn_norm_g, v_forget_bias, v_fox_out_g, v_sb_out_g, v_ffn_norm_g, v_conv_b, v_final_norm_g),
                       "adamw_replicated", tr=SMALL_ROWS)

    outs = []
    loss = None
    for kind in range(4):
        b_in, b_out, b_up, b_down, b_conv = (res[kind] for res in big)
        s_attn, s_bias, s_fox, s_sb, s_ffn, s_cb, s_fin, s_loss = _unpack(small[kind], small_shapes)
        if kind == 0:
            loss = s_loss[0]
        outs += [s_attn, b_in[None], s_bias, s_fox, s_sb, b_out[None], s_ffn, b_up[None], b_conv[None], s_cb,
                 b_down[None], s_fin]
    return (loss, grad_x[None], *outs)
```

```python
import jax
import jax.numpy as jnp
from jax import lax
from jax.experimental import pallas as pl
from jax.experimental.pallas import tpu as pltpu

F32 = jnp.float32
BF16 = jnp.bfloat16

D_MODEL = 1024
HEAD_DIM = 64
N_GROUP_HEADS = 8
GROUP_W = N_GROUP_HEADS * HEAD_DIM
QKV_W = 3 * GROUP_W
IN_COLS = 2 * QKV_W + N_GROUP_HEADS
GATE_PAD = 128
IN_COLS_PAD = 2 * QKV_W + GATE_PAD
D_FF = 2816
N_DEV = 8
EPS = 1e-6
Q_SCALE = HEAD_DIM ** -0.5

ADAM_LR = 0.001
ADAM_B1 = 0.9
ADAM_B2 = 0.999
ADAM_EPS = 1e-08
ADAM_WD = 0.01
ADAM_STEP = 10

LANES = 128
SMALL_ROWS = 80
VMEM_LIMIT = 56 * 1024 * 1024
NEG_BIG = -1e30
ATTN_TQ = 512
SB_TQ = 512
FOX_TK = 512
SB_TK = 256
MESH = pl.DeviceIdType.MESH


def _cparams(sem=None, **kw):
    return pltpu.CompilerParams(dimension_semantics=sem, vmem_limit_bytes=VMEM_LIMIT, **kw)


def _tile(n, target, mult=LANES):
    if n <= target:
        return n
    t = (target // mult) * mult
    while t >= mult:
        if n % t == 0:
            return t
        t -= mult
    return n


def _seg_len(shape):
    n = 1
    for s in shape:
        n *= s
    return -(-n // LANES) * LANES


def _pack(arrs, rows, dtype):
    parts = []
    for a in arrs:
        f = a.reshape(-1).astype(dtype)
        parts.append(jnp.pad(f, (0, _seg_len(a.shape) - f.shape[0])))
    flat = jnp.concatenate(parts)
    flat = jnp.pad(flat, (0, rows * LANES - flat.shape[0]))
    return flat.reshape(rows, LANES)


def _unpack(p, shapes, lead=()):
    flat = p.reshape(lead + (-1,))
    out, off = [], 0
    for shp in shapes:
        n = 1
        for s in shp:
            n *= s
        out.append(flat[..., off:off + n].reshape(lead + tuple(shp)))
        off += _seg_len(shp)
    return out


def _my_pos():
    return lax.axis_index("x"), lax.axis_index("y"), lax.axis_index("c")


def _all_gather(blocks):
    n = len(blocks)

    def body(*refs):
        x_refs, out_refs = refs[:n], refs[n:2 * n]
        send_sems, recv_sems, local_sems = refs[2 * n:]
        x, y, c = _my_pos()
        me, sibling = (x, y, c), (x, y, 1 - c)
        chips = [(1 - x, y), (x, 1 - y), (1 - x, 1 - y)]

        def copy(a, k, blk, to, own=False):
            px, py, pc = blk
            slot = out_refs[a].at[4 * px + 2 * py + pc]
            return pltpu.make_async_remote_copy(
                src_ref=x_refs[a] if own else slot, dst_ref=slot,
                send_sem=send_sems.at[a, k], recv_sem=recv_sems.at[a, k],
                device_id=to, device_id_type=MESH)

        mine = [pltpu.make_async_copy(x_refs[a], out_refs[a].at[4 * x + 2 * y + c], local_sems.at[a])
                for a in range(n)]
        for cp in mine:
            cp.start()
        first = []
        for a in range(n):
            first.append(copy(a, 0, me, sibling, own=True))
            first += [copy(a, 1 + j, me, (*chip, c), own=True) for j, chip in enumerate(chips)]
        for cp in first:
            cp.start()
        passed = []
        for j, chip in enumerate(chips):
            for a in range(n):
                copy(a, 1 + j, (*chip, c), me).wait_recv()
                passed.append(copy(a, 4 + j, (*chip, c), sibling))
                passed[-1].start()
        for a in range(n):
            copy(a, 0, sibling, me).wait_recv()
            for j, chip in enumerate(chips):
                copy(a, 4 + j, (*chip, 1 - c), me).wait_recv()
        for cp in first + passed:
            cp.wait_send()
        for cp in mine:
            cp.wait()

    hbm = pl.BlockSpec(memory_space=pl.ANY)
    return pl.pallas_call(
        body, name="weights_all_gather",
        out_shape=[jax.ShapeDtypeStruct((N_DEV,) + b.shape, b.dtype) for b in blocks],
        in_specs=[hbm] * n, out_specs=[hbm] * n,
        scratch_shapes=[pltpu.SemaphoreType.DMA((n, 7)), pltpu.SemaphoreType.DMA((n, 7)),
                        pltpu.SemaphoreType.DMA((n,))],
    )(*blocks)


def _grad_exchange(slabs, spack):
    n = len(slabs) + 1

    def body(*refs):
        in_refs, out_refs = refs[:n], refs[n:2 * n]
        send_sems, recv_sems, local_sems = refs[2 * n:]
        x, y, c = _my_pos()
        my_id = 4 * x + 2 * y + c

        def src_of(a, dev):
            return in_refs[a] if a == n - 1 else in_refs[a].at[dev]

        own = [pltpu.make_async_copy(src_of(a, my_id), out_refs[a].at[my_id], local_sems.at[a])
               for a in range(n)]
        for cp in own:
            cp.start()
        sends, arrivals = [], []
        for k in range(1, N_DEV):
            px, py, pc = x ^ (k >> 2), y ^ ((k >> 1) & 1), c ^ (k & 1)
            peer_id = 4 * px + 2 * py + pc
            for a in range(n):
                for dst_slot, bucket in ((my_id, sends), (peer_id, arrivals)):
                    bucket.append(pltpu.make_async_remote_copy(
                        src_ref=src_of(a, peer_id), dst_ref=out_refs[a].at[dst_slot],
                        send_sem=send_sems.at[a, k - 1], recv_sem=recv_sems.at[a, k - 1],
                        device_id=(px, py, pc), device_id_type=MESH))
        for cp in sends:
            cp.start()
        for cp in arrivals:
            cp.wait_recv()
        for cp in sends:
            cp.wait_send()
        for cp in own:
            cp.wait()

    hbm = pl.BlockSpec(memory_space=pl.ANY)
    return pl.pallas_call(
        body, name="grad_exchange",
        out_shape=[jax.ShapeDtypeStruct(g.shape, g.dtype) for g in slabs]
        + [jax.ShapeDtypeStruct((N_DEV,) + spack.shape, spack.dtype)],
        in_specs=[hbm] * n, out_specs=[hbm] * n,
        scratch_shapes=[pltpu.SemaphoreType.DMA((n, 7)), pltpu.SemaphoreType.DMA((n, 7)),
                        pltpu.SemaphoreType.DMA((n,))],
    )(*slabs, spack)


_HBM = pl.BlockSpec(memory_space=pltpu.HBM)
_SEM = pl.BlockSpec(memory_space=pltpu.SEMAPHORE)
_EFFECT = pltpu.SideEffectType.DATAFLOW_SIDE_EFFECTING


def _my_id():
    x, y, c = _my_pos()
    return 4 * x + 2 * y + c


def _peer_copies(src_refs, land_refs, send_sems, recv_sems, per_peer):
    x, y, c = _my_pos()
    my_id = 4 * x + 2 * y + c
    copies = []
    for k in range(1, N_DEV):
        px, py, pc = x ^ (k >> 2), y ^ ((k >> 1) & 1), c ^ (k & 1)
        for a, (src, land) in enumerate(zip(src_refs, land_refs)):
            copies.append(pltpu.make_async_remote_copy(
                src_ref=src.at[4 * px + 2 * py + pc] if per_peer else src, dst_ref=land.at[my_id],
                send_sem=send_sems.at[a * (N_DEV - 1) + k - 1], recv_sem=recv_sems.at[a * (N_DEV - 1) + k - 1],
                device_id=(px, py, pc), device_id_type=MESH))
    return copies


def _exchange_start(srcs, per_peer, name):
    n = len(srcs)
    lands = [lax.empty(s.shape if per_peer else (N_DEV,) + s.shape, s.dtype) for s in srcs]

    def body(*refs):
        src_refs, land_refs = refs[:n], refs[n:2 * n]
        send_sems, recv_sems = refs[2 * n], refs[2 * n + 1]
        token = refs[-1]
        for cp in _peer_copies(src_refs, land_refs, send_sems, recv_sems, per_peer):
            cp.start()
        token[...] = jnp.zeros_like(token)

    outs = pl.pallas_call(
        body, name=name,
        out_shape=(pltpu.SemaphoreType.DMA((n * (N_DEV - 1),)), pltpu.SemaphoreType.DMA((n * (N_DEV - 1),)),
                   *[pltpu.HBM(a.shape, a.dtype) for a in srcs + lands],
                   jax.ShapeDtypeStruct((8, LANES), F32)),
        in_specs=[_HBM] * (2 * n),
        out_specs=(_SEM, _SEM, *[_HBM] * (2 * n), pl.BlockSpec(memory_space=pltpu.VMEM)),
        input_output_aliases={a: 2 + a for a in range(2 * n)},
        compiler_params=pltpu.CompilerParams(has_side_effects=_EFFECT),
    )(*[pltpu.with_memory_space_constraint(a, pltpu.HBM) for a in srcs + lands])
    return outs[0], outs[1], list(outs[2:2 + n]), list(outs[2 + n:2 + 2 * n]), outs[-1]


def _exchange_wait(handles, per_peer, after, name):
    send_sems, recv_sems, srcs, lands, _ = handles
    n = len(srcs)

    def body(*refs):
        src_refs, land_refs = refs[:n], refs[n:2 * n]
        for cp in _peer_copies(src_refs, land_refs, refs[2 * n], refs[2 * n + 1], per_peer):
            cp.wait_send()
            cp.wait_recv()

    outs = pl.pallas_call(
        body, name=name,
        out_shape=tuple(pltpu.HBM(a.shape, a.dtype) for a in srcs + lands),
        in_specs=[_HBM] * (2 * n) + [_SEM, _SEM, pl.BlockSpec(memory_space=pl.ANY)],
        out_specs=tuple([_HBM] * (2 * n)),
        input_output_aliases={a: a for a in range(2 * n)},
        compiler_params=pltpu.CompilerParams(has_side_effects=_EFFECT),
    )(*srcs, *lands, send_sems, recv_sems, after)
    me = _my_id()
    filled = []
    for src, land in zip(outs[:n], outs[n:]):
        own = lax.dynamic_index_in_dim(src, me, 0, keepdims=True) if per_peer else src[None]
        filled.append(lax.dynamic_update_slice_in_dim(land, own, me, 0))
    return filled


def _col_windows(n_shards, width, gap_at=None, gap=0):
    out = []
    for j in range(n_shards):
        g0, g1 = j * width, (j + 1) * width
        cuts = [g0, g1] if gap_at is None or not g0 < gap_at < g1 else [g0, gap_at, g1]
        for a, b in zip(cuts[:-1], cuts[1:]):
            out.append((j, a - g0, b - g0, a + (gap if gap_at is not None and a >= gap_at else 0)))
    return out


def _assemble_cols(parts, total, windows, name, tr=256):
    n, rows, w = parts.shape
    tr = _tile(rows, tr, 16)

    def body(p_ref, o_ref):
        o_ref[...] = jnp.zeros_like(o_ref)
        for j, lo, hi, dst in windows:
            o_ref[:, dst:dst + hi - lo] = p_ref[j, :, lo:hi]

    return pl.pallas_call(
        body, name=name, grid=(rows // tr,),
        in_specs=[pl.BlockSpec((n, tr, w), lambda i: (0, i, 0))],
        out_specs=pl.BlockSpec((tr, total), lambda i: (i, 0)),
        out_shape=jax.ShapeDtypeStruct((rows, total), parts.dtype),
        compiler_params=_cparams(("parallel",)),
    )(parts)


def _split_cols(full, n, w, windows, name, tr=256):
    rows, total = full.shape
    tr = _tile(rows, tr, 16)

    def body(f_ref, o_ref):
        for j, lo, hi, dst in windows:
            o_ref[j, :, lo:hi] = f_ref[:, dst:dst + hi - lo].astype(o_ref.dtype)

    return pl.pallas_call(
        body, name=name, grid=(rows // tr,),
        in_specs=[pl.BlockSpec((tr, total), lambda i: (i, 0))],
        out_specs=pl.BlockSpec((n, tr, w), lambda i: (0, i, 0)),
        out_shape=jax.ShapeDtypeStruct((n, rows, w), BF16),
        compiler_params=_cparams(("parallel",)),
    )(full)


_DIMS = {"nn": (((1,), (0,)), ((), ())), "nt": (((1,), (1,)), ((), ())), "tn": (((0,), (0,)), ((), ()))}


def _matmul(a, b, *, mode, grid, a_block, a_map, b_block, b_map, o_block, o_map, out_shape, name,
            resid=None):
    nk = grid[2]
    dims = _DIMS[mode]

    def body(*refs):
        if resid is None:
            a_ref, b_ref, o_ref, acc_ref = refs
            r_ref = None
        else:
            a_ref, b_ref, r_ref, o_ref, acc_ref = refs
        k = pl.program_id(2)

        @pl.when(k == 0)
        def _():
            acc_ref[...] = jnp.zeros_like(acc_ref)

        acc_ref[...] += lax.dot_general(a_ref[...], b_ref[...], dims, preferred_element_type=F32)

        @pl.when(k == nk - 1)
        def _():
            res = acc_ref[...]
            if r_ref is not None:
                res = r_ref[...] + res
            o_ref[...] = res.astype(o_ref.dtype)

    in_specs = [pl.BlockSpec(a_block, a_map), pl.BlockSpec(b_block, b_map)]
    args = [a, b]
    if resid is not None:
        in_specs.append(pl.BlockSpec(o_block, o_map))
        args.append(resid)
    acc_shape = tuple(d for d in o_block if d is not None)
    return pl.pallas_call(
        body, name=name, grid=grid, in_specs=in_specs,
        out_specs=pl.BlockSpec(o_block, o_map), out_shape=out_shape,
        scratch_shapes=[pltpu.VMEM(acc_shape, F32)],
        compiler_params=_cparams(("parallel", "parallel", "arbitrary")),
    )(*args)


def _mm_nn(a, b, out_dtype, name, resid=None, tm=1024, tn=1024, tk=1024):
    m, kk = a.shape
    n = b.shape[1]
    tm, tn, tk = _tile(m, tm, 8), _tile(n, tn), _tile(kk, tk)
    return _matmul(a, b, mode="nn", grid=(m // tm, n // tn, kk // tk),
                   a_block=(tm, tk), a_map=lambda i, j, k: (i, k),
                   b_block=(tk, tn), b_map=lambda i, j, k: (k, j),
                   o_block=(tm, tn), o_map=lambda i, j, k: (i, j),
                   out_shape=jax.ShapeDtypeStruct((m, n), out_dtype), name=name, resid=resid)


def _mm_nt(a, b, out_dtype, name, tm=1024, tn=1024, tk=1024):
    m, kk = a.shape
    n = b.shape[0]
    tm, tn, tk = _tile(m, tm, 8), _tile(n, tn), _tile(kk, tk)
    return _matmul(a, b, mode="nt", grid=(m // tm, n // tn, kk // tk),
                   a_block=(tm, tk), a_map=lambda i, j, k: (i, k),
                   b_block=(tn, tk), b_map=lambda i, j, k: (j, k),
                   o_block=(tm, tn), o_map=lambda i, j, k: (i, j),
                   out_shape=jax.ShapeDtypeStruct((m, n), out_dtype), name=name)


def _mm_tn(a, b, name, tm=1024, tn=1024, tk=1024):
    kk, m = a.shape
    n = b.shape[1]
    tm, tn, tk = _tile(m, tm), _tile(n, tn), _tile(kk, tk, 8)
    return _matmul(a, b, mode="tn", grid=(m // tm, n // tn, kk // tk),
                   a_block=(tk, tm), a_map=lambda i, j, k: (k, i),
                   b_block=(tk, tn), b_map=lambda i, j, k: (k, j),
                   o_block=(tm, tn), o_map=lambda i, j, k: (i, j),
                   out_shape=jax.ShapeDtypeStruct((m, n), F32), name=name)


def _mm_heads(a, b, name, tm=1024, tn=640):
    m, kk = a.shape
    n = b.shape[1]
    tm, tn = _tile(m, tm, 16), _tile(n, tn)
    per_tile = tn // HEAD_DIM

    def body(a_ref, b_ref, o_ref):
        res = jnp.dot(a_ref[...], b_ref[...], preferred_element_type=F32)
        for hh in range(per_tile):
            o_ref[hh] = res[:, hh * HEAD_DIM:(hh + 1) * HEAD_DIM].astype(o_ref.dtype)

    return pl.pallas_call(
        body, name=name, grid=(m // tm, n // tn),
        in_specs=[pl.BlockSpec((tm, kk), lambda i, j: (i, 0)), pl.BlockSpec((kk, tn), lambda i, j: (0, j))],
        out_specs=pl.BlockSpec((per_tile, tm, HEAD_DIM), lambda i, j: (j, i, 0)),
        out_shape=jax.ShapeDtypeStruct((n // HEAD_DIM, m, HEAD_DIM), BF16),
        compiler_params=_cparams(("parallel", "parallel")),
    )(a, b)


def _mm_up(h, w_up, tm=2048, tn=256):
    s = h.shape[0]
    tm = _tile(s, tm, 8)
    nh = D_FF // tn
    return _matmul(h, w_up, mode="nn", grid=(s // tm, 2 * nh, 1),
                   a_block=(tm, D_MODEL), a_map=lambda i, j, k: (i, 0),
                   b_block=(D_MODEL, tn), b_map=lambda i, j, k: (0, j),
                   o_block=(None, tm, tn), o_map=lambda i, j, k: (j // nh, i, j % nh),
                   out_shape=jax.ShapeDtypeStruct((2, s, D_FF), F32), name="up_proj")


def _mm_dup_nt(dup, w_up, tm=1024, tk=1408):
    s = dup.shape[1]
    tm = _tile(s, tm, 8)
    nh = D_FF // tk
    return _matmul(dup, w_up, mode="nt", grid=(s // tm, 1, 2 * nh),
                   a_block=(None, tm, tk), a_map=lambda i, j, k: (k // nh, i, k % nh),
                   b_block=(D_MODEL, tk), b_map=lambda i, j, k: (0, k),
                   o_block=(tm, D_MODEL), o_map=lambda i, j, k: (i, 0),
                   out_shape=jax.ShapeDtypeStruct((s, D_MODEL), F32), name="d_h2")


def _mm_dwup_tn(h, dup, tn=1408, tk=1024):
    s = h.shape[0]
    tk = _tile(s, tk, 8)
    nh = D_FF // tn
    return _matmul(h, dup, mode="tn", grid=(1, 2 * nh, s // tk),
                   a_block=(tk, D_MODEL), a_map=lambda i, j, k: (k, 0),
                   b_block=(None, tk, tn), b_map=lambda i, j, k: (j // nh, k, j % nh),
                   o_block=(D_MODEL, tn), o_map=lambda i, j, k: (0, j),
                   out_shape=jax.ShapeDtypeStruct((D_MODEL, 2 * D_FF), F32), name="d_w_up")


def _rms_fwd(x, g, tr=256):
    s, d = x.shape
    tr = _tile(s, tr, 8)

    def body(x_ref, g_ref, o_ref):
        xv = x_ref[...]
        r = lax.rsqrt(jnp.mean(xv * xv, axis=-1, keepdims=True) + EPS)
        o_ref[...] = (xv * r * g_ref[...]).astype(o_ref.dtype)

    return pl.pallas_call(
        body, name="rms_fwd", grid=(s // tr,),
        in_specs=[pl.BlockSpec((tr, d), lambda i: (i, 0)), pl.BlockSpec((1, d), lambda i: (0, 0))],
        out_specs=pl.BlockSpec((tr, d), lambda i: (i, 0)),
        out_shape=jax.ShapeDtypeStruct((s, d), BF16),
        compiler_params=_cparams(("parallel",)),
    )(x, g)


def _group_rms_fwd(o_fox, o_sb, g_fox, g_sb, tr=256):
    nh, s, dh = o_fox.shape
    tr = _tile(s, tr, 8)

    def body(a_ref, b_ref, ga_ref, gb_ref, o_ref):
        for src, g_ref, lo in ((a_ref, ga_ref, 0), (b_ref, gb_ref, nh * dh)):
            heads = [src[hh] for hh in range(nh)]
            ss = heads[0] * heads[0]
            for xv in heads[1:]:
                ss = ss + xv * xv
            r = lax.rsqrt(jnp.sum(ss, axis=-1, keepdims=True) * (1.0 / (nh * dh)) + EPS)
            for hh, xv in enumerate(heads):
                o_ref[:, lo + hh * dh:lo + (hh + 1) * dh] = (xv * r * g_ref[hh]).astype(o_ref.dtype)

    heads_blk = pl.BlockSpec((nh, tr, dh), lambda i: (0, i, 0))
    gain = pl.BlockSpec((nh, 1, dh), lambda i: (0, 0, 0))
    return pl.pallas_call(
        body, name="group_rms_fwd", grid=(s // tr,),
        in_specs=[heads_blk, heads_blk, gain, gain],
        out_specs=pl.BlockSpec((tr, 2 * nh * dh), lambda i: (i, 0)),
        out_shape=jax.ShapeDtypeStruct((s, 2 * nh * dh), BF16),
        compiler_params=_cparams(("parallel",)),
    )(o_fox, o_sb, g_fox, g_sb)


def _group_rms_bwd(x, dy, g, *, dy_col, name, tr=256):
    nh, s, dh = x.shape
    tr = _tile(s, tr, 8)
    d = nh * dh

    def body(x_ref, dy_ref, g_ref, dx_ref, dg_ref):
        @pl.when(pl.program_id(0) == 0)
        def _():
            dg_ref[...] = jnp.zeros_like(dg_ref)

        dyv = dy_ref[...]
        xs_ = [x_ref[hh] for hh in range(nh)]
        dys = [dyv[:, hh * dh:(hh + 1) * dh] for hh in range(nh)]
        ss = xs_[0] * xs_[0]
        for xv in xs_[1:]:
            ss = ss + xv * xv
        r = lax.rsqrt(jnp.sum(ss, axis=-1, keepdims=True) * (1.0 / d) + EPS)
        xh = [xv * r for xv in xs_]
        gy = [dys[hh] * g_ref[hh] for hh in range(nh)]
        dot = xh[0] * gy[0]
        for hh in range(1, nh):
            dot = dot + xh[hh] * gy[hh]
        mean_dot = jnp.sum(dot, axis=-1, keepdims=True) * (1.0 / d)
        for hh in range(nh):
            dx_ref[hh] = r * (gy[hh] - xh[hh] * mean_dot)
            dg_ref[hh] += jnp.sum(dys[hh] * xh[hh], axis=0, keepdims=True)

    heads_blk = pl.BlockSpec((nh, tr, dh), lambda i: (0, i, 0))
    gain = pl.BlockSpec((nh, 1, dh), lambda i: (0, 0, 0))
    return pl.pallas_call(
        body, name=name, grid=(s // tr,),
        in_specs=[heads_blk, pl.BlockSpec((tr, d), lambda i: (i, dy_col)), gain],
        out_specs=[heads_blk, gain],
        out_shape=[jax.ShapeDtypeStruct((nh, s, dh), F32), jax.ShapeDtypeStruct((nh, 1, dh), F32)],
        compiler_params=_cparams(("arbitrary",)),
    )(x, dy, g)


def _merge_dproj(parts_fox, d_gate, parts_sb, tr=256):
    nh, s, dh = parts_fox[0].shape
    tr = _tile(s, tr, 16)

    def body(*refs):
        o_ref = refs[-1]
        gate_ref = refs[3]
        col = 0
        for ref in refs[:3]:
            for hh in range(nh):
                o_ref[:, col:col + dh] = ref[hh].astype(o_ref.dtype)
                col += dh
        o_ref[:, col:col + GATE_PAD] = jnp.zeros((tr, GATE_PAD), o_ref.dtype)
        o_ref[:, col:col + N_GROUP_HEADS] = gate_ref[...].astype(o_ref.dtype)
        col += GATE_PAD
        for ref in refs[4:7]:
            for hh in range(nh):
                o_ref[:, col:col + dh] = ref[hh].astype(o_ref.dtype)
                col += dh

    heads_blk = pl.BlockSpec((nh, tr, dh), lambda i: (0, i, 0))
    return pl.pallas_call(
        body, name="merge_d_proj", grid=(s // tr,),
        in_specs=[heads_blk] * 3 + [pl.BlockSpec((tr, N_GROUP_HEADS), lambda i: (i, 0))] + [heads_blk] * 3,
        out_specs=pl.BlockSpec((tr, IN_COLS_PAD), lambda i: (i, 0)),
        out_shape=jax.ShapeDtypeStruct((s, IN_COLS_PAD), BF16),
        compiler_params=_cparams(("parallel",)),
    )(*parts_fox, d_gate, *parts_sb)


def _rms_bwd(x, dy, g, resid, *, dy_col, name, want_bf16, tr=256):
    s, d = x.shape
    tr = _tile(s, tr, 8)
    has_resid = resid is not None

    def body(*refs):
        refs = list(refs)
        x_ref, dy_ref, g_ref = refs[:3]
        r_ref = refs[3] if has_resid else None
        outs = refs[4:] if has_resid else refs[3:]
        dx_ref = outs[0]
        dxb_ref = outs[1] if want_bf16 else None
        dg_ref = outs[-1]

        @pl.when(pl.program_id(0) == 0)
        def _():
            dg_ref[...] = jnp.zeros_like(dg_ref)

        xv = x_ref[...]
        dyv = dy_ref[...]
        r = lax.rsqrt(jnp.mean(xv * xv, axis=-1, keepdims=True) + EPS)
        xh = xv * r
        gy = dyv * g_ref[...]
        dx = r * (gy - xh * jnp.mean(xh * gy, axis=-1, keepdims=True))
        if r_ref is not None:
            dx = r_ref[...] + dx
        dx_ref[...] = dx
        if dxb_ref is not None:
            dxb_ref[...] = dx.astype(BF16)
        dg_ref[...] += jnp.sum(dyv * xh, axis=0, keepdims=True)

    row = pl.BlockSpec((tr, d), lambda i: (i, 0))
    in_specs = [row, pl.BlockSpec((tr, d), lambda i: (i, dy_col)), pl.BlockSpec((1, d), lambda i: (0, 0))]
    args = [x, dy, g]
    if has_resid:
        in_specs.append(row)
        args.append(resid)
    out_specs = [row]
    out_shape = [jax.ShapeDtypeStruct((s, d), F32)]
    if want_bf16:
        out_specs.append(row)
        out_shape.append(jax.ShapeDtypeStruct((s, d), BF16))
    out_specs.append(pl.BlockSpec((1, d), lambda i: (0, 0)))
    out_shape.append(jax.ShapeDtypeStruct((1, d), F32))
    return pl.pallas_call(
        body, name=name, grid=(s // tr,), in_specs=in_specs, out_specs=out_specs, out_shape=out_shape,
        compiler_params=_cparams(("arbitrary",)),
    )(*args)


def _loss_head(x2, target, g, tr=256):
    s, d = x2.shape
    tr = _tile(s, tr, 8)

    def body(x_ref, t_ref, g_ref, dx_ref, dxb_ref, dg_ref, loss_ref):
        @pl.when(pl.program_id(0) == 0)
        def _():
            dg_ref[...] = jnp.zeros_like(dg_ref)
            loss_ref[...] = jnp.zeros_like(loss_ref)

        xv = x_ref[...]
        gv = g_ref[...]
        r = lax.rsqrt(jnp.mean(xv * xv, axis=-1, keepdims=True) + EPS)
        xh = xv * r
        err = xh * gv - t_ref[...]
        loss_ref[...] += jnp.sum(jnp.mean(err * err, axis=-1, keepdims=True), axis=0, keepdims=True) * 0.5
        dyv = err * (1.0 / d)
        gy = dyv * gv
        dx = r * (gy - xh * jnp.mean(xh * gy, axis=-1, keepdims=True))
        dx_ref[...] = dx
        dxb_ref[...] = dx.astype(BF16)
        dg_ref[...] += jnp.sum(dyv * xh, axis=0, keepdims=True)

    row = pl.BlockSpec((tr, d), lambda i: (i, 0))
    return pl.pallas_call(
        body, name="loss_head", grid=(s // tr,),
        in_specs=[row, row, pl.BlockSpec((1, d), lambda i: (0, 0))],
        out_specs=[row, row, pl.BlockSpec((1, d), lambda i: (0, 0)), pl.BlockSpec((1, LANES), lambda i: (0, 0))],
        out_shape=[jax.ShapeDtypeStruct((s, d), F32), jax.ShapeDtypeStruct((s, d), BF16),
                   jax.ShapeDtypeStruct((1, d), F32), jax.ShapeDtypeStruct((1, LANES), F32)],
        compiler_params=_cparams(("arbitrary",)),
    )(x2, target, g)


def _conv_taps(cur, prev8, w, b, first):
    prev8 = jnp.where(first, 0.0, prev8)
    ext = jnp.concatenate([prev8, cur], axis=0)
    x1 = pltpu.roll(ext, 1, 0)[8:]
    x2 = pltpu.roll(ext, 2, 0)[8:]
    u = b + w[0:1] * x2
    u = u + w[1:2] * x1
    u = u + w[2:3] * cur
    return u, x1, x2


def _conv_gate_fwd(up, conv_w, conv_b, tm=1024, tn=256):
    s = up.shape[1]
    tm = _tile(s, tm, 8)
    nrb = s // tm
    rb8 = tm // 8

    def body(g_ref, v_ref, gp_ref, vp_ref, wg_ref, wv_ref, bg_ref, bv_ref, o_ref):
        first = pl.program_id(1) == 0
        ug, _, _ = _conv_taps(g_ref[...], gp_ref[...], wg_ref[...], bg_ref[...], first)
        uv, _, _ = _conv_taps(v_ref[...], vp_ref[...], wv_ref[...], bv_ref[...], first)
        sg = 1.0 / (1.0 + jnp.exp(-ug))
        o_ref[...] = (ug * sg * uv).astype(o_ref.dtype)

    def cur(h):
        return pl.BlockSpec((None, tm, tn), lambda j, i: (h, i, j))

    def prev(h):
        return pl.BlockSpec((None, 8, tn), lambda j, i: (h, jnp.maximum(i * rb8 - 1, 0), j))

    def par(h, r):
        return pl.BlockSpec((None, r, tn), lambda j, i: (h, 0, j))

    return pl.pallas_call(
        body, name="conv_gate_fwd", grid=(D_FF // tn, nrb),
        in_specs=[cur(0), cur(1), prev(0), prev(1), par(0, 3), par(1, 3), par(0, 1), par(1, 1)],
        out_specs=pl.BlockSpec((tm, tn), lambda j, i: (i, j)),
        out_shape=jax.ShapeDtypeStruct((s, D_FF), BF16),
        compiler_params=_cparams(("parallel", "parallel")),
    )(up, up, up, up, conv_w, conv_w, conv_b, conv_b)


def _conv_gate_bwd(up, dact, conv_w, conv_b, tm=512, tn=256):
    s = up.shape[1]
    tm = _tile(s, tm, 8)
    nrb = s // tm
    rb8 = tm // 8

    def body(g_ref, v_ref, gp_ref, vp_ref, da_ref, wg_ref, wv_ref, bg_ref, bv_ref,
             dup_ref, dcw_ref, dcb_ref, carry_ref):
        i = pl.program_id(1)
        first = i == nrb - 1

        @pl.when(i == 0)
        def _():
            carry_ref[...] = jnp.zeros_like(carry_ref)
            dcw_ref[...] = jnp.zeros_like(dcw_ref)
            dcb_ref[...] = jnp.zeros_like(dcb_ref)

        curs = (g_ref[...], v_ref[...])
        ws = (wg_ref[...], wv_ref[...])
        ug, g1, g2 = _conv_taps(curs[0], gp_ref[...], ws[0], bg_ref[...], first)
        uv, v1, v2 = _conv_taps(curs[1], vp_ref[...], ws[1], bv_ref[...], first)
        sg = 1.0 / (1.0 + jnp.exp(-ug))
        da = da_ref[...].astype(F32)
        d_v = da * (ug * sg)
        d_g = da * uv * (sg * (1.0 + ug * (1.0 - sg)))
        for h, (du, x0, x1, x2) in enumerate(((d_g, curs[0], g1, g2), (d_v, curs[1], v1, v2))):
            dcb_ref[h] += jnp.sum(du, axis=0, keepdims=True)
            dcw_ref[h, 0:1, :] += jnp.sum(du * x2, axis=0, keepdims=True)
            dcw_ref[h, 1:2, :] += jnp.sum(du * x1, axis=0, keepdims=True)
            dcw_ref[h, 2:3, :] += jnp.sum(du * x0, axis=0, keepdims=True)
            ext = jnp.concatenate([du, carry_ref[h]], axis=0)
            n1 = pltpu.roll(ext, tm + 7, 0)[:tm]
            n2 = pltpu.roll(ext, tm + 6, 0)[:tm]
            w = ws[h]
            dup_ref[h] = (w[2:3] * du + w[1:2] * n1 + w[0:1] * n2).astype(dup_ref.dtype)
            carry_ref[h] = du[:8]

    def cur(h):
        return pl.BlockSpec((None, tm, tn), lambda j, i: (h, nrb - 1 - i, j))

    def prev(h):
        return pl.BlockSpec((None, 8, tn), lambda j, i: (h, jnp.maximum((nrb - 1 - i) * rb8 - 1, 0), j))

    def par(h, r):
        return pl.BlockSpec((None, r, tn), lambda j, i: (h, 0, j))

    return pl.pallas_call(
        body, name="conv_gate_bwd", grid=(D_FF // tn, nrb),
        in_specs=[cur(0), cur(1), prev(0), prev(1),
                  pl.BlockSpec((tm, tn), lambda j, i: (nrb - 1 - i, j)),
                  par(0, 3), par(1, 3), par(0, 1), par(1, 1)],
        out_specs=[pl.BlockSpec((2, tm, tn), lambda j, i: (0, nrb - 1 - i, j)),
                   pl.BlockSpec((2, 3, tn), lambda j, i: (0, 0, j)),
                   pl.BlockSpec((2, 1, tn), lambda j, i: (0, 0, j))],
        out_shape=[jax.ShapeDtypeStruct((2, s, D_FF), BF16),
                   jax.ShapeDtypeStruct((2, 3, D_FF), F32),
                   jax.ShapeDtypeStruct((2, 1, D_FF), F32)],
        scratch_shapes=[pltpu.VMEM((2, 8, tn), F32)],
        compiler_params=_cparams(("parallel", "arbitrary")),
    )(up, up, up, up, dact, conv_w, conv_w, conv_b, conv_b)


def _split_dot(x, tri, terms):
    piece = x.astype(BF16)
    out = jnp.dot(piece, tri, preferred_element_type=F32)
    rest = x
    for _ in range(terms - 1):
        rest = rest - piece.astype(F32)
        piece = rest.astype(BF16)
        out = out + jnp.dot(piece, tri, preferred_element_type=F32)
    return out


def _split_dot_rhs(tri, x, terms):
    piece = x.astype(BF16)
    out = jnp.dot(tri, piece, preferred_element_type=F32)
    rest = x
    for _ in range(terms - 1):
        rest = rest - piece.astype(F32)
        piece = rest.astype(BF16)
        out = out + jnp.dot(tri, piece, preferred_element_type=F32)
    return out


def _tri(n, kind):
    r = lax.broadcasted_iota(jnp.int32, (n, n), 0)
    c = lax.broadcasted_iota(jnp.int32, (n, n), 1)
    cond = {"le": r <= c, "ge": r >= c, "lt": r < c, "gt": r > c}[kind]
    return jnp.where(cond, 1.0, 0.0).astype(BF16)


def _log_sigmoid(x):
    return jnp.minimum(x, 0.0) - jnp.log(1.0 + jnp.exp(-jnp.abs(x)))


def _forget_fwd(f_logit, bias):
    h, r, _ = f_logit.shape

    def body(x_ref, b_ref, o_ref):
        lf = _log_sigmoid(x_ref[...] + b_ref[...])
        within = _split_dot(lf, _tri(LANES, "le"), 3)
        row_tot = jnp.broadcast_to(within[:, LANES - 1:LANES], (r, LANES))
        before = _split_dot_rhs(_tri(r, "gt"), row_tot, 3)
        o_ref[...] = within + before

    blk = pl.BlockSpec((None, r, LANES), lambda i: (i, 0, 0))
    return pl.pallas_call(
        body, name="forget_cumsum_fwd", grid=(h,),
        in_specs=[blk, pl.BlockSpec((None, 1, LANES), lambda i: (i, 0, 0))],
        out_specs=blk, out_shape=jax.ShapeDtypeStruct((h, r, LANES), F32),
        compiler_params=_cparams(("parallel",)),
    )(f_logit, bias)


def _forget_bwd(f_logit, bias, ksum, qsum):
    h, r, _ = f_logit.shape

    def body(x_ref, b_ref, k_ref, q_ref, dx_ref, db_ref):
        d_f = q_ref[...] - k_ref[...]
        within = _split_dot(d_f, _tri(LANES, "ge"), 3)
        row_tot = jnp.broadcast_to(within[:, 0:1], (r, LANES))
        after = _split_dot_rhs(_tri(r, "lt"), row_tot, 3)
        xv = x_ref[...] + b_ref[...]
        dx = (within + after) * jnp.exp(_log_sigmoid(-xv))
        dx_ref[...] = dx
        db_ref[...] = jnp.broadcast_to(jnp.sum(dx), (1, LANES))

    blk = pl.BlockSpec((None, r, LANES), lambda i: (i, 0, 0))
    one = pl.BlockSpec((None, 1, LANES), lambda i: (i, 0, 0))
    return pl.pallas_call(
        body, name="forget_cumsum_bwd", grid=(h,),
        in_specs=[blk, one, blk, blk], out_specs=[blk, one],
        out_shape=[jax.ShapeDtypeStruct((h, r, LANES), F32), jax.ShapeDtypeStruct((h, 1, LANES), F32)],
        compiler_params=_cparams(("parallel",)),
    )(f_logit, bias, ksum, qsum)


def _head_specs(s, tq):
    qblk = pl.BlockSpec((None, tq, HEAD_DIM), lambda h, i: (h, i, 0))
    full = pl.BlockSpec((None, s, HEAD_DIM), lambda h, i: (h, 0, 0))
    col = pl.BlockSpec((None, tq, 1), lambda h, i: (h, i, 0))
    return qblk, full, col


def _qkv_specs(s, tq, offs):
    q_off, k_off, v_off = offs
    return (pl.BlockSpec((None, tq, HEAD_DIM), lambda h, i: (h + q_off, i, 0)),
            pl.BlockSpec((None, s, HEAD_DIM), lambda h, i: (h + k_off, 0, 0)),
            pl.BlockSpec((None, s, HEAD_DIM), lambda h, i: (h + v_off, 0, 0)))


def _scaled(q_ref):
    return (q_ref[...].astype(F32) * Q_SCALE).astype(BF16)


_NT = (((1,), (1,)), ((), ()))
_TN = (((0,), (0,)), ((), ()))


def _cols_minus_rows(rows, cols):
    return lax.broadcasted_iota(jnp.int32, (rows, cols), 1) - lax.broadcasted_iota(jnp.int32, (rows, cols), 0)


def _fox_fwd(qkv, offs, v_ones, f_col, f_row, tq, tk):
    h, s = N_GROUP_HEADS, qkv.shape[1]
    nk = s // tk
    assert tq == tk

    def body(q_ref, k_ref, v_ref, fc_ref, fr_ref, o_ref, lse_ref, m_ref, acc_ref, z0, z1):
        i = pl.program_id(1)
        qs = _scaled(q_ref)
        fq = fc_ref[...]
        m_ref[...] = jnp.full_like(m_ref, NEG_BIG)
        acc_ref[...] = jnp.zeros_like(acc_ref)

        ahead = _cols_minus_rows(tq, tk)

        def block_of(j):
            return jnp.minimum(j, nk - 1)

        def keys_of(j):
            return pl.ds(pl.multiple_of(block_of(j) * tk, tk), tk)

        def logits(j):
            return lax.dot_general(qs, k_ref[keys_of(j), :], _NT, preferred_element_type=F32)

        def soft(j, raw, masked):
            sc = raw + fq - fr_ref[block_of(j)]
            if masked:
                sc = jnp.where(ahead <= (i - j) * tk, sc, NEG_BIG)
            m_old = m_ref[...]
            m_new = jnp.maximum(m_old, jnp.max(sc, axis=-1, keepdims=True))
            p = jnp.exp(sc - m_new)
            acc_ref[...] = jnp.exp(m_old - m_new) * acc_ref[...] + jnp.dot(
                p.astype(BF16), v_ref[keys_of(j), :], preferred_element_type=F32)
            m_ref[...] = m_new

        z0[...] = logits(0)

        def trip(p, masked):
            j = 2 * p
            z1[...] = logits(j + 1)
            soft(j, z0[...], masked)
            z0[...] = logits(j + 2)
            soft(j + 1, z1[...], masked)

        def step(p, carry):
            trip(p, False)
            return carry

        lax.fori_loop(0, i // 2, step, 0)
        trip(i // 2, True)
        l = acc_ref[:, HEAD_DIM:HEAD_DIM + 1]
        o_ref[...] = acc_ref[:, :HEAD_DIM] / l
        lse_ref[...] = m_ref[...] + jnp.log(l)

    qblk, full, colspec = _head_specs(s, tq)
    q_in, k_in, _ = _qkv_specs(s, tq, offs)
    return pl.pallas_call(
        body, name="fox_fwd", grid=(h, s // tq),
        in_specs=[q_in, k_in, pl.BlockSpec((None, s, 2 * HEAD_DIM), lambda hh, i: (hh, 0, 0)), colspec,
                  pl.BlockSpec((None, nk, 1, tk), lambda hh, i: (hh, 0, 0, 0))],
        out_specs=[qblk, colspec],
        out_shape=[jax.ShapeDtypeStruct((h, s, HEAD_DIM), F32), jax.ShapeDtypeStruct((h, s, 1), F32)],
        scratch_shapes=[pltpu.VMEM((tq, 1), F32), pltpu.VMEM((tq, 2 * HEAD_DIM), F32),
                        pltpu.VMEM((tq, tk), F32), pltpu.VMEM((tq, tk), F32)],
        compiler_params=_cparams(("parallel", "parallel")),
    )(qkv, qkv, v_ones, f_col, f_row)


def _fox_bwd(qkv, offs, f_col, f_row, o, lse, d_o, tq, tk):
    h, s = N_GROUP_HEADS, qkv.shape[1]
    nk = s // tk
    assert tq == tk

    def body(q_ref, k_ref, v_ref, fc_ref, fr_ref, o_ref, lse_ref, do_ref,
             dq_ref, dk_ref, dv_ref, ks_ref, qs_ref, dq_acc, qsum_acc, z0, z1, p0, p1):
        i = pl.program_id(1)

        @pl.when(i == 0)
        def _():
            dk_ref[...] = jnp.zeros_like(dk_ref)
            dv_ref[...] = jnp.zeros_like(dv_ref)
            ks_ref[...] = jnp.zeros_like(ks_ref)

        qs = _scaled(q_ref)
        fq = fc_ref[...]
        lse_v = lse_ref[...]
        dob = do_ref[...].astype(BF16)
        delta = jnp.sum(dob.astype(F32) * o_ref[...], axis=-1, keepdims=True)
        dq_acc[...] = jnp.zeros_like(dq_acc)
        qsum_acc[...] = jnp.zeros_like(qsum_acc)

        ahead = _cols_minus_rows(tq, tk)

        def block_of(j):
            return jnp.minimum(j, nk - 1)

        def keys_of(j):
            return pl.ds(pl.multiple_of(block_of(j) * tk, tk), tk)

        def products(j):
            at = keys_of(j)
            return (lax.dot_general(qs, k_ref[at, :], _NT, preferred_element_type=F32),
                    lax.dot_general(dob, v_ref[at, :], _NT, preferred_element_type=F32))

        def grads(j, raw, dp, masked):
            at = keys_of(j)
            sc = raw + fq - fr_ref[block_of(j)]
            if masked:
                sc = jnp.where(ahead <= (i - j) * tk, sc, NEG_BIG)
            p = jnp.exp(sc - lse_v)
            ds = p * (dp - delta)
            dsb = ds.astype(BF16)
            dq_acc[...] += jnp.dot(dsb, k_ref[at, :], preferred_element_type=F32)
            dk_ref[at, :] += lax.dot_general(dsb, qs, _TN, preferred_element_type=F32)
            dv_ref[at, :] += lax.dot_general(p.astype(BF16), dob, _TN, preferred_element_type=F32)
            ks_ref[block_of(j)] += jnp.sum(ds.reshape(tq // 8, 8, tk), axis=0)
            qsum_acc[...] += jnp.sum(ds, axis=-1, keepdims=True)

        z0[...], p0[...] = products(0)

        def trip(pp, masked):
            j = 2 * pp
            z1[...], p1[...] = products(j + 1)
            grads(j, z0[...], p0[...], masked)
            z0[...], p0[...] = products(j + 2)
            grads(j + 1, z1[...], p1[...], masked)

        def step(pp, carry):
            trip(pp, False)
            return carry

        lax.fori_loop(0, i // 2, step, 0)
        trip(i // 2, True)
        dq_ref[...] = dq_acc[...] * Q_SCALE
        qs_ref[...] = qsum_acc[...]

    qblk, full, colspec = _head_specs(s, tq)
    frow = pl.BlockSpec((None, nk, 1, tk), lambda hh, i: (hh, 0, 0, 0))
    big = pltpu.VMEM((tq, tk), F32)
    return pl.pallas_call(
        body, name="fox_bwd", grid=(h, s // tq),
        in_specs=[*_qkv_specs(s, tq, offs), colspec, frow, qblk, colspec, qblk],
        out_specs=[qblk, full, full, pl.BlockSpec((None, nk, 8, tk), lambda hh, i: (hh, 0, 0, 0)), colspec],
        out_shape=[jax.ShapeDtypeStruct((h, s, HEAD_DIM), F32)] * 3
        + [jax.ShapeDtypeStruct((h, nk, 8, tk), F32), jax.ShapeDtypeStruct((h, s, 1), F32)],
        scratch_shapes=[pltpu.VMEM((tq, HEAD_DIM), F32), pltpu.VMEM((tq, 1), F32), big, big, big, big],
        compiler_params=_cparams(("parallel", "arbitrary")),
    )(qkv, qkv, qkv, f_col, f_row, o, lse, d_o)


SB_TERMS = 2
G_TERMS = 1
LOG2E = 1.4426950408889634
LN2 = 0.6931471805599453


def _softplus2(z2):
    return jnp.maximum(z2, 0.0) + jnp.log2(1.0 + jnp.exp2(-jnp.abs(z2)))


def _sb_fwd(qkv, offs, tq, tk):
    h, s = N_GROUP_HEADS, qkv.shape[1]

    assert tq % (2 * tk) == 0

    def body(q_ref, k_ref, v_ref, o_ref, w_hbm, acc_ref, run_ref, z0, z1, d0, d1, t0, t1, w_stage, wsem):
        z_refs, d_refs, t_refs = (z0, z1), (d0, d1), (t0, t1)
        hh = pl.program_id(0)
        i = pl.program_id(1)
        qs = _scaled(q_ref)
        tri = _tri(tk, "ge")
        acc_ref[...] = jnp.zeros_like(acc_ref)
        run_ref[...] = jnp.zeros_like(run_ref)
        nb = (i + 1) * (tq // tk)
        ahead = _cols_minus_rows(tq, tk)

        def keys_of(b):
            j = nb - 1 - jnp.minimum(b, nb - 1)
            return pl.ds(pl.multiple_of(j * tk, tk), tk)

        def visible(b):
            return ahead < i * tq - (nb - 1 - b) * tk

        def logits(b, slot):
            z_refs[slot][...] = lax.dot_general(qs, k_ref[keys_of(b), :], _NT,
                                                preferred_element_type=F32) * LOG2E

        def sums(b, slot, masked):
            z2 = z_refs[slot][...]
            sp = _softplus2(z2)
            if masked:
                sp = jnp.where(visible(b), sp, 0.0)
            inc = _split_dot(sp, tri, SB_TERMS)
            d_refs[slot][...] = z2 - inc
            t_refs[slot][...] = inc[:, 0:1]

        def put(p, slot):
            st = (p % 2) * 2 + slot
            return pltpu.make_async_copy(w_stage.at[st], w_hbm.at[hh, i, nb - 1 - (2 * p + slot)], wsem.at[st])

        def weigh(p, slot, masked):
            b = 2 * p + slot
            w = jnp.exp2(d_refs[slot][...] - run_ref[...])
            if masked:
                w = jnp.where(visible(b), w, 0.0)
            wb = w.astype(BF16)
            w_stage[(p % 2) * 2 + slot] = wb
            acc_ref[...] += jnp.dot(wb, v_ref[keys_of(b), :], preferred_element_type=F32)
            run_ref[...] += t_refs[slot][...]

        def trip(p, masked):
            @pl.when(p >= 2)
            def _():
                put(p - 2, 0).wait()
                put(p - 2, 1).wait()

            b = 2 * p
            logits(b + 2, 0)
            sums(b + 1, 1, masked)
            weigh(p, 0, masked)
            logits(b + 3, 1)
            sums(b + 2, 0, masked)
            weigh(p, 1, masked)
            put(p, 0).start()
            put(p, 1).start()

        logits(0, 0)
        logits(1, 1)
        sums(0, 0, True)

        def guarded(p, carry):
            trip(p, True)
            return carry

        def plain(p, carry):
            trip(p, False)
            return carry

        lax.fori_loop(0, tq // tk // 2, guarded, 0)
        lax.fori_loop(tq // tk // 2, nb // 2, plain, 0)
        trips = nb // 2

        @pl.when(trips >= 2)
        def _():
            put(trips - 2, 0).wait()
            put(trips - 2, 1).wait()

        put(trips - 1, 0).wait()
        put(trips - 1, 1).wait()
        o_ref[...] = acc_ref[...]

    qblk, full, colspec = _head_specs(s, tq)
    return pl.pallas_call(
        body, name="sb_fwd", grid=(h, s // tq),
        in_specs=[*_qkv_specs(s, tq, offs)], out_specs=[qblk, pl.BlockSpec(memory_space=pl.ANY)],
        out_shape=[jax.ShapeDtypeStruct((h, s, HEAD_DIM), F32),
                   jax.ShapeDtypeStruct((h, s // tq, s // tk, tq, tk), BF16)],
        scratch_shapes=[pltpu.VMEM((tq, HEAD_DIM), F32), pltpu.VMEM((tq, 1), F32),
                        pltpu.VMEM((tq, tk), F32), pltpu.VMEM((tq, tk), F32),
                        pltpu.VMEM((tq, tk), F32), pltpu.VMEM((tq, tk), F32),
                        pltpu.VMEM((tq, 1), F32), pltpu.VMEM((tq, 1), F32),
                        pltpu.VMEM((4, tq, tk), BF16), pltpu.SemaphoreType.DMA((4,))],
        compiler_params=_cparams(("parallel", "parallel")),
    )(qkv, qkv, qkv)


def _sb_bwd(qkv, offs, w_saved, d_o, tq, tk):
    h, s = N_GROUP_HEADS, qkv.shape[1]

    assert tq % (2 * tk) == 0

    def body(q_ref, k_ref, v_ref, do_ref, w_hbm, dq_ref, dk_ref, dv_ref, dq_acc, grun_ref,
             z0, z1, p0, p1, w_bufs, wsem):
        z_refs, p_refs = (z0, z1), (p0, p1)
        hh = pl.program_id(0)
        i = pl.program_id(1)

        @pl.when(i == 0)
        def _():
            dk_ref[...] = jnp.zeros_like(dk_ref)
            dv_ref[...] = jnp.zeros_like(dv_ref)

        qs = _scaled(q_ref)
        dob = do_ref[...].astype(BF16)
        tri = _tri(tk, "le")
        dq_acc[...] = jnp.zeros_like(dq_acc)
        grun_ref[...] = jnp.zeros_like(grun_ref)
        nb = (i + 1) * (tq // tk)
        ahead = _cols_minus_rows(tq, tk)

        def block_of(b):
            return jnp.minimum(b, nb - 1)

        def keys_of(b):
            return pl.ds(pl.multiple_of(block_of(b) * tk, tk), tk)

        def visible(b):
            return ahead < i * tq - b * tk

        def fetch(p, slot):
            st = (p % 2) * 2 + slot
            return pltpu.make_async_copy(w_hbm.at[hh, i, block_of(2 * p + slot)], w_bufs.at[st], wsem.at[st])

        def products(b, slot):
            at = keys_of(b)
            z_refs[slot][...] = lax.dot_general(qs, k_ref[at, :], _NT, preferred_element_type=F32) * LOG2E
            p_refs[slot][...] = lax.dot_general(dob, v_ref[at, :], _NT, preferred_element_type=F32)

        def grads(p, slot, masked):
            b = 2 * p + slot
            at = keys_of(b)
            wb = w_bufs[(p % 2) * 2 + slot]
            g = wb.astype(F32) * p_refs[slot][...]
            ginc = _split_dot(g, tri, G_TERMS)
            beta = 1.0 / (1.0 + jnp.exp2(-z_refs[slot][...]))
            dz = g - beta * (grun_ref[...] + ginc)
            if masked:
                dz = jnp.where(visible(b), dz, 0.0)
            dzb = dz.astype(BF16)
            dq_acc[...] += jnp.dot(dzb, k_ref[at, :], preferred_element_type=F32)
            dk_ref[at, :] += lax.dot_general(dzb, qs, _TN, preferred_element_type=F32)
            dv_ref[at, :] += lax.dot_general(wb, dob, _TN, preferred_element_type=F32)
            grun_ref[...] += ginc[:, tk - 1:tk]

        def trip(p, masked):
            for slot in (0, 1):
                fetch(p + 1, slot).start()
            for slot in (0, 1):
                fetch(p, slot).wait()
            for slot in (0, 1):
                products(2 * p + slot + 1, 1 - slot)
                grads(p, slot, masked)

        for slot in (0, 1):
            fetch(0, slot).start()
        products(0, 0)
        n_plain = (nb - tq // tk) // 2

        def plain(p, carry):
            trip(p, False)
            return carry

        def guarded(p, carry):
            trip(p, True)
            return carry

        lax.fori_loop(0, n_plain, plain, 0)
        lax.fori_loop(n_plain, nb // 2, guarded, 0)
        for slot in (0, 1):
            fetch(nb // 2, slot).wait()
        dq_ref[...] = dq_acc[...] * Q_SCALE

    qblk, full, colspec = _head_specs(s, tq)
    big = pltpu.VMEM((tq, tk), F32)
    return pl.pallas_call(
        body, name="sb_bwd", grid=(h, s // tq),
        in_specs=[*_qkv_specs(s, tq, offs), qblk, pl.BlockSpec(memory_space=pl.ANY)], out_specs=[qblk, full, full],
        out_shape=[jax.ShapeDtypeStruct((h, s, HEAD_DIM), F32)] * 3,
        scratch_shapes=[pltpu.VMEM((tq, HEAD_DIM), F32), pltpu.VMEM((tq, 1), F32)]
        + [big] * 4 + [pltpu.VMEM((4, tq, tk), BF16), pltpu.SemaphoreType.DMA((4,))],
        compiler_params=_cparams(("parallel", "arbitrary")),
    )(qkv, qkv, qkv, d_o, w_saved)


def _sum_adamw(parts, w, m, v, name, tr=256):
    _, rows, lanes = parts.shape
    tr = _tile(rows, tr, 16)
    c_m = 1.0 - ADAM_B1 ** ADAM_STEP
    c_v = 1.0 - ADAM_B2 ** ADAM_STEP

    def body(p_ref, w_ref, m_ref, v_ref, g_ref, d_ref, nm_ref, nv_ref):
        g = p_ref[0].astype(F32)
        for j in range(1, N_DEV):
            g = g + p_ref[j].astype(F32)
        nm = ADAM_B1 * m_ref[...] + (1.0 - ADAM_B1) * g
        nv = ADAM_B2 * v_ref[...] + (1.0 - ADAM_B2) * (g * g)
        m_hat = nm / c_m
        v_hat = nv / c_v
        g_ref[...] = g
        d_ref[...] = -ADAM_LR * (m_hat / (jnp.sqrt(v_hat) + ADAM_EPS) + ADAM_WD * w_ref[...])
        nm_ref[...] = nm
        nv_ref[...] = nv

    blk = pl.BlockSpec((tr, lanes), lambda i: (i, 0))
    return pl.pallas_call(
        body, name=name, grid=(rows // tr,),
        in_specs=[pl.BlockSpec((N_DEV, tr, lanes), lambda i: (0, i, 0)), blk, blk, blk],
        out_specs=[blk] * 4, out_shape=[jax.ShapeDtypeStruct((rows, lanes), F32)] * 4,
        compiler_params=_cparams(("parallel",)),
    )(parts, w, m, v)


def kernel(x, attn_norm_g, w_in, forget_bias, fox_out_g, sb_out_g, w_out, ffn_norm_g, w_up, conv_w, conv_b, w_down, final_norm_g, loss_target, m_attn_norm_g, m_w_in, m_forget_bias, m_fox_out_g, m_sb_out_g, m_w_out, m_ffn_norm_g, m_w_up, m_conv_w, m_conv_b, m_w_down, m_final_norm_g, v_attn_norm_g, v_w_in, v_forget_bias, v_fox_out_g, v_sb_out_g, v_w_out, v_ffn_norm_g, v_w_up, v_conv_w, v_conv_b, v_w_down, v_final_norm_g):
    s = x.shape[1]
    xs = x[0]
    tq = min(ATTN_TQ, s)
    tk_fox = min(FOX_TK, s)
    tk_sb = min(SB_TK, s)
    in_shard, up_shard, out_shard, down_shard = IN_COLS // N_DEV, 2 * D_FF // N_DEV, D_MODEL // N_DEV, D_FF // N_DEV

    cw = conv_w[0]
    cw_hi = cw.astype(BF16)
    cw_lo = (cw - cw_hi.astype(F32)).astype(BF16)
    (g_in,) = _all_gather([w_in[0].astype(BF16)])
    rest = _exchange_start([w_out[0].astype(BF16), w_up[0].astype(BF16), w_down[0].astype(BF16),
                            jnp.stack([cw_hi, cw_lo])], False, "weights_rest_start")
    n_gate = QKV_W + N_GROUP_HEADS
    in_windows = _col_windows(N_DEV, in_shard, gap_at=n_gate, gap=GATE_PAD - N_GROUP_HEADS)
    up_windows = _col_windows(N_DEV, up_shard)
    w_in_p = _assemble_cols(g_in, IN_COLS_PAD, in_windows, "assemble_w_in")
    conv_b2 = conv_b.reshape(2, 1, D_FF)

    h1 = _rms_fwd(xs, attn_norm_g + rest[-1][0:1, 0:1])
    proj_h = _mm_heads(h1, w_in_p, "in_proj")
    fox_offs = (0, N_GROUP_HEADS, 2 * N_GROUP_HEADS)
    sb_first = 3 * N_GROUP_HEADS + GATE_PAD // HEAD_DIM
    sb_offs = (sb_first, sb_first + N_GROUP_HEADS, sb_first + 2 * N_GROUP_HEADS)
    f_logit = _mm_nn(h1, w_in_p[:, QKV_W:QKV_W + GATE_PAD], F32, "gate_proj")[:, :N_GROUP_HEADS]
    fv = proj_h[2 * N_GROUP_HEADS:3 * N_GROUP_HEADS]

    f_logit_h = f_logit.T.reshape(N_GROUP_HEADS, s // LANES, LANES)
    bias_h = jnp.broadcast_to(forget_bias.reshape(N_GROUP_HEADS, 1, 1), (N_GROUP_HEADS, 1, LANES))
    big_f = _forget_fwd(f_logit_h, bias_h)
    f_col = big_f.reshape(N_GROUP_HEADS, s, 1)
    f_row = big_f.reshape(N_GROUP_HEADS, s // tk_fox, 1, tk_fox)

    fv_ones = jnp.concatenate([fv, jnp.ones_like(fv)], axis=-1)
    o_fox_h, lse = _fox_fwd(proj_h, fox_offs, fv_ones, f_col, f_row, tq, tk_fox)
    o_sb_h, sb_w = _sb_fwd(proj_h, sb_offs, min(SB_TQ, s), tk_sb)
    g_fox_h = fox_out_g.reshape(N_GROUP_HEADS, 1, HEAD_DIM)
    g_sb_h = sb_out_g.reshape(N_GROUP_HEADS, 1, HEAD_DIM)
    o_n = _group_rms_fwd(o_fox_h, o_sb_h, g_fox_h, g_sb_h)
    g_out, g_up, g_down, g_conv = _exchange_wait(rest, False, o_n, "weights_rest_wait")
    w_out_f = g_out.reshape(D_MODEL, D_MODEL)
    w_up_f = _assemble_cols(g_up, 2 * D_FF, up_windows, "assemble_w_up")
    w_down_f = g_down.reshape(D_FF, D_MODEL)
    conv_w_f = (g_conv[:, 0].astype(F32) + g_conv[:, 1].astype(F32)).transpose(1, 0, 2).reshape(3, 2 * D_FF)
    conv_w2 = conv_w_f.reshape(3, 2, D_FF).transpose(1, 0, 2)
    x1 = _mm_nn(o_n, w_out_f, F32, "out_proj", resid=xs)
    h2 = _rms_fwd(x1, ffn_norm_g)
    up = _mm_up(h2, w_up_f)
    act = _conv_gate_fwd(up, conv_w2, conv_b2)
    x2 = _mm_nn(act, w_down_f, F32, "down_proj", resid=x1, tk=1408)

    d_x2, d_x2b, dg_final, loss_part = _loss_head(x2, loss_target[0], final_norm_g.reshape(1, D_MODEL))
    d_act = _mm_nt(d_x2b, w_down_f, BF16, "d_act", tn=1408)
    dw_down = _mm_tn(act, d_x2b, "d_w_down", tm=1408)
    d_up, dcw2, dcb2 = _conv_gate_bwd(up, d_act, conv_w2, conv_b2)
    d_h2 = _mm_dup_nt(d_up, w_up_f)
    dw_up = _mm_dwup_tn(h2, d_up)
    d_x1, d_x1b, dg_ffn = _rms_bwd(x1, d_h2, ffn_norm_g, d_x2, dy_col=0, name="ffn_norm_bwd", want_bf16=True)
    d_on = _mm_nt(d_x1b, w_out_f, F32, "d_o_normed")
    dw_out = _mm_tn(o_n, d_x1b, "d_w_out")
    early = _exchange_start(
        [dw_out.astype(BF16).reshape(N_DEV, out_shard, D_MODEL),
         _split_cols(dw_up, N_DEV, up_shard, up_windows, "split_d_w_up"),
         dw_down.astype(BF16).reshape(N_DEV, down_shard, D_MODEL)], True, "grads_early_start")
    g_fox_t = g_fox_h + early[-1][0:1, 0:1]
    d_o_fox_h, dg_fox = _group_rms_bwd(o_fox_h, d_on, g_fox_t, dy_col=0, name="fox_norm_bwd")
    d_o_sb_h, dg_sb = _group_rms_bwd(o_sb_h, d_on, g_sb_h, dy_col=1, name="sb_norm_bwd")

    dfq, dfk, dfv, ksum8, qsum = _fox_bwd(proj_h, fox_offs, f_col, f_row, o_fox_h, lse, d_o_fox_h, tq, tk_fox)
    dsq, dsk, dsv = _sb_bwd(proj_h, sb_offs, sb_w, d_o_sb_h, min(SB_TQ, s), tk_sb)
    ksum = jnp.sum(ksum8, axis=2).reshape(N_GROUP_HEADS, s // LANES, LANES)
    d_f_logit_h, d_bias_h = _forget_bwd(f_logit_h, bias_h, ksum,
                                        qsum.reshape(N_GROUP_HEADS, s // LANES, LANES))
    d_f_logit = d_f_logit_h.reshape(N_GROUP_HEADS, s).T

    d_proj = _merge_dproj((dfq, dfk, dfv), d_f_logit, (dsq, dsk, dsv))
    dw_in_p = _mm_tn(h1, d_proj, "d_w_in", tn=640)
    dconv_w = dcw2.transpose(1, 0, 2).reshape(3, 2 * D_FF)
    dconv_b = dcb2.reshape(1, 2 * D_FF)
    late = _exchange_start(
        [_split_cols(dw_in_p, N_DEV, in_shard, in_windows, "split_d_w_in"),
         dconv_w.astype(BF16).reshape(3, N_DEV, up_shard).transpose(1, 0, 2)],
        True, "grads_late_start")
    d_h1 = _mm_nt(d_proj, w_in_p + late[-1][0:1, 0:1].astype(BF16), F32, "d_h1", tk=640)
    grad_x, dg_attn = _rms_bwd(xs, d_h1, attn_norm_g, d_x1, dy_col=0, name="attn_norm_bwd", want_bf16=False)

    small_shapes = [(1, D_MODEL), (1, N_GROUP_HEADS), (1, GROUP_W), (1, GROUP_W), (1, D_MODEL),
                    (1, 2 * D_FF), (D_MODEL,), (1,)]
    spack = _pack([dg_attn, d_bias_h[:, 0, 0], dg_fox, dg_sb, dg_ffn, dconv_b, dg_final, loss_part[0, 0:1]],
                  SMALL_ROWS, F32)
    (srecv,) = _grad_exchange([], spack)
    r_out, r_up, r_down = _exchange_wait(early, True, srecv, "grads_early_wait")
    r_in, r_conv = _exchange_wait(late, True, r_out, "grads_late_wait")

    big = [_sum_adamw(g, w_[0], m_[0], v_[0], "adamw_" + tag)
           for g, w_, m_, v_, tag in zip(
               (r_in, r_out, r_up, r_down, r_conv), (w_in, w_out, w_up, w_down, conv_w), (m_w_in, m_w_out, m_w_up, m_w_down, m_conv_w),
               (v_w_in, v_w_out, v_w_up, v_w_down, v_conv_w), ("w_in", "w_out", "w_up", "w_down", "conv_w"))]

    def small_pack(a_attn, a_bias, a_fox, a_sb, a_ffn, a_cb, a_fin):
        return _pack([a_attn, a_bias, a_fox, a_sb, a_ffn, a_cb, a_fin, jnp.zeros((1,), F32)], SMALL_ROWS, F32)

    small = _sum_adamw(srecv, small_pack(attn_norm_g, forget_bias, fox_out_g, sb_out_g, ffn_norm_g, conv_b, final_norm_g),
                       small_pack(m_attn_norm_g, m_forget_bias, m_fox_out_g, m_sb_out_g, m_ffn_norm_g, m_conv_b, m_final_norm_g),
                       small_pack(v_attn_norm_g, v_forget_bias, v_fox_out_g, v_sb_out_g, v_ffn_norm_g, v_conv_b, v_final_norm_g),
                       "adamw_replicated", tr=SMALL_ROWS)

    outs = []
    loss = None
    for kind in range(4):
        b_in, b_out, b_up, b_down, b_conv = (res[kind] for res in big)
        s_attn, s_bias, s_fox, s_sb, s_ffn, s_cb, s_fin, s_loss = _unpack(small[kind], small_shapes)
        if kind == 0:
            loss = s_loss[0]
        outs += [s_attn, b_in[None], s_bias, s_fox, s_sb, b_out[None], s_ffn, b_up[None], b_conv[None], s_cb,
                 b_down[None], s_fin]
    return (loss, grad_x[None], *outs)
```

```python
import jax
import jax.numpy as jnp
from jax import lax
from jax.experimental import pallas as pl
from jax.experimental.pallas import tpu as pltpu

F32 = jnp.float32
BF16 = jnp.bfloat16

D_MODEL = 1024
HEAD_DIM = 64
N_GROUP_HEADS = 8
GROUP_W = N_GROUP_HEADS * HEAD_DIM
QKV_W = 3 * GROUP_W
IN_COLS = 2 * QKV_W + N_GROUP_HEADS
GATE_PAD = 128
IN_COLS_PAD = 2 * QKV_W + GATE_PAD
D_FF = 2816
N_DEV = 8
EPS = 1e-6
Q_SCALE = HEAD_DIM ** -0.5

ADAM_LR = 0.001
ADAM_B1 = 0.9
ADAM_B2 = 0.999
ADAM_EPS = 1e-08
ADAM_WD = 0.01
ADAM_STEP = 10

LANES = 128
SMALL_ROWS = 80
VMEM_LIMIT = 56 * 1024 * 1024
NEG_BIG = -1e30
ATTN_TQ = 512
SB_TQ = 512
FOX_TK = 512
SB_TK = 256
MESH = pl.DeviceIdType.MESH


def _cparams(sem=None, **kw):
    return pltpu.CompilerParams(dimension_semantics=sem, vmem_limit_bytes=VMEM_LIMIT, **kw)


def _tile(n, target, mult=LANES):
    if n <= target:
        return n
    t = (target // mult) * mult
    while t >= mult:
        if n % t == 0:
            return t
        t -= mult
    return n


def _seg_len(shape):
    n = 1
    for s in shape:
        n *= s
    return -(-n // LANES) * LANES


def _pack(arrs, rows, dtype):
    parts = []
    for a in arrs:
        f = a.reshape(-1).astype(dtype)
        parts.append(jnp.pad(f, (0, _seg_len(a.shape) - f.shape[0])))
    flat = jnp.concatenate(parts)
    flat = jnp.pad(flat, (0, rows * LANES - flat.shape[0]))
    return flat.reshape(rows, LANES)


def _unpack(p, shapes, lead=()):
    flat = p.reshape(lead + (-1,))
    out, off = [], 0
    for shp in shapes:
        n = 1
        for s in shp:
            n *= s
        out.append(flat[..., off:off + n].reshape(lead + tuple(shp)))
        off += _seg_len(shp)
    return out


def _my_pos():
    return lax.axis_index("x"), lax.axis_index("y"), lax.axis_index("c")


def _all_gather(blocks):
    n = len(blocks)

    def body(*refs):
        x_refs, out_refs = refs[:n], refs[n:2 * n]
        send_sems, recv_sems, local_sems = refs[2 * n:]
        x, y, c = _my_pos()
        me, sibling = (x, y, c), (x, y, 1 - c)
        chips = [(1 - x, y), (x, 1 - y), (1 - x, 1 - y)]

        def copy(a, k, blk, to, own=False):
            px, py, pc = blk
            slot = out_refs[a].at[4 * px + 2 * py + pc]
            return pltpu.make_async_remote_copy(
                src_ref=x_refs[a] if own else slot, dst_ref=slot,
                send_sem=send_sems.at[a, k], recv_sem=recv_sems.at[a, k],
                device_id=to, device_id_type=MESH)

        mine = [pltpu.make_async_copy(x_refs[a], out_refs[a].at[4 * x + 2 * y + c], local_sems.at[a])
                for a in range(n)]
        for cp in mine:
            cp.start()
        first = []
        for a in range(n):
            first.append(copy(a, 0, me, sibling, own=True))
            first += [copy(a, 1 + j, me, (*chip, c), own=True) for j, chip in enumerate(chips)]
        for cp in first:
            cp.start()
        passed = []
        for j, chip in enumerate(chips):
            for a in range(n):
                copy(a, 1 + j, (*chip, c), me).wait_recv()
                passed.append(copy(a, 4 + j, (*chip, c), sibling))
                passed[-1].start()
        for a in range(n):
            copy(a, 0, sibling, me).wait_recv()
            for j, chip in enumerate(chips):
                copy(a, 4 + j, (*chip, 1 - c), me).wait_recv()
        for cp in first + passed:
            cp.wait_send()
        for cp in mine:
            cp.wait()

    hbm = pl.BlockSpec(memory_space=pl.ANY)
    return pl.pallas_call(
        body, name="weights_all_gather",
        out_shape=[jax.ShapeDtypeStruct((N_DEV,) + b.shape, b.dtype) for b in blocks],
        in_specs=[hbm] * n, out_specs=[hbm] * n,
        scratch_shapes=[pltpu.SemaphoreType.DMA((n, 7)), pltpu.SemaphoreType.DMA((n, 7)),
                        pltpu.SemaphoreType.DMA((n,))],
    )(*blocks)


def _grad_exchange(slabs, spack):
    n = len(slabs) + 1

    def body(*refs):
        in_refs, out_refs = refs[:n], refs[n:2 * n]
        send_sems, recv_sems, local_sems = refs[2 * n:]
        x, y, c = _my_pos()
        my_id = 4 * x + 2 * y + c

        def src_of(a, dev):
            return in_refs[a] if a == n - 1 else in_refs[a].at[dev]

        own = [pltpu.make_async_copy(src_of(a, my_id), out_refs[a].at[my_id], local_sems.at[a])
               for a in range(n)]
        for cp in own:
            cp.start()
        sends, arrivals = [], []
        for k in range(1, N_DEV):
            px, py, pc = x ^ (k >> 2), y ^ ((k >> 1) & 1), c ^ (k & 1)
            peer_id = 4 * px + 2 * py + pc
            for a in range(n):
                for dst_slot, bucket in ((my_id, sends), (peer_id, arrivals)):
                    bucket.append(pltpu.make_async_remote_copy(
                        src_ref=src_of(a, peer_id), dst_ref=out_refs[a].at[dst_slot],
                        send_sem=send_sems.at[a, k - 1], recv_sem=recv_sems.at[a, k - 1],
                        device_id=(px, py, pc), device_id_type=MESH))
        for cp in sends:
            cp.start()
        for cp in arrivals:
            cp.wait_recv()
        for cp in sends:
            cp.wait_send()
        for cp in own:
            cp.wait()

    hbm = pl.BlockSpec(memory_space=pl.ANY)
    return pl.pallas_call(
        body, name="grad_exchange",
        out_shape=[jax.ShapeDtypeStruct(g.shape, g.dtype) for g in slabs]
        + [jax.ShapeDtypeStruct((N_DEV,) + spack.shape, spack.dtype)],
        in_specs=[hbm] * n, out_specs=[hbm] * n,
        scratch_shapes=[pltpu.SemaphoreType.DMA((n, 7)), pltpu.SemaphoreType.DMA((n, 7)),
                        pltpu.SemaphoreType.DMA((n,))],
    )(*slabs, spack)


_HBM = pl.BlockSpec(memory_space=pltpu.HBM)
_SEM = pl.BlockSpec(memory_space=pltpu.SEMAPHORE)
_EFFECT = pltpu.SideEffectType.DATAFLOW_SIDE_EFFECTING


def _my_id():
    x, y, c = _my_pos()
    return 4 * x + 2 * y + c


def _peer_copies(src_refs, land_refs, send_sems, recv_sems, per_peer):
    x, y, c = _my_pos()
    my_id = 4 * x + 2 * y + c
    copies = []
    for k in range(1, N_DEV):
        px, py, pc = x ^ (k >> 2), y ^ ((k >> 1) & 1), c ^ (k & 1)
        for a, (src, land) in enumerate(zip(src_refs, land_refs)):
            copies.append(pltpu.make_async_remote_copy(
                src_ref=src.at[4 * px + 2 * py + pc] if per_peer else src, dst_ref=land.at[my_id],
                send_sem=send_sems.at[a * (N_DEV - 1) + k - 1], recv_sem=recv_sems.at[a * (N_DEV - 1) + k - 1],
                device_id=(px, py, pc), device_id_type=MESH))
    return copies


def _exchange_start(srcs, per_peer, name):
    n = len(srcs)
    lands = [lax.empty(s.shape if per_peer else (N_DEV,) + s.shape, s.dtype) for s in srcs]

    def body(*refs):
        src_refs, land_refs = refs[:n], refs[n:2 * n]
        send_sems, recv_sems = refs[2 * n], refs[2 * n + 1]
        token = refs[-1]
        for cp in _peer_copies(src_refs, land_refs, send_sems, recv_sems, per_peer):
            cp.start()
        token[...] = jnp.zeros_like(token)

    outs = pl.pallas_call(
        body, name=name,
        out_shape=(pltpu.SemaphoreType.DMA((n * (N_DEV - 1),)), pltpu.SemaphoreType.DMA((n * (N_DEV - 1),)),
                   *[pltpu.HBM(a.shape, a.dtype) for a in srcs + lands],
                   jax.ShapeDtypeStruct((8, LANES), F32)),
        in_specs=[_HBM] * (2 * n),
        out_specs=(_SEM, _SEM, *[_HBM] * (2 * n), pl.BlockSpec(memory_space=pltpu.VMEM)),
        input_output_aliases={a: 2 + a for a in range(2 * n)},
        compiler_params=pltpu.CompilerParams(has_side_effects=_EFFECT),
    )(*[pltpu.with_memory_space_constraint(a, pltpu.HBM) for a in srcs + lands])
    return outs[0], outs[1], list(outs[2:2 + n]), list(outs[2 + n:2 + 2 * n]), outs[-1]


def _exchange_wait(handles, per_peer, after, name):
    send_sems, recv_sems, srcs, lands, _ = handles
    n = len(srcs)

    def body(*refs):
        src_refs, land_refs = refs[:n], refs[n:2 * n]
        for cp in _peer_copies(src_refs, land_refs, refs[2 * n], refs[2 * n + 1], per_peer):
            cp.wait_send()
            cp.wait_recv()

    outs = pl.pallas_call(
        body, name=name,
        out_shape=tuple(pltpu.HBM(a.shape, a.dtype) for a in srcs + lands),
        in_specs=[_HBM] * (2 * n) + [_SEM, _SEM, pl.BlockSpec(memory_space=pl.ANY)],
        out_specs=tuple([_HBM] * (2 * n)),
        input_output_aliases={a: a for a in range(2 * n)},
        compiler_params=pltpu.CompilerParams(has_side_effects=_EFFECT),
    )(*srcs, *lands, send_sems, recv_sems, after)
    me = _my_id()
    filled = []
    for src, land in zip(outs[:n], outs[n:]):
        own = lax.dynamic_index_in_dim(src, me, 0, keepdims=True) if per_peer else src[None]
        filled.append(lax.dynamic_update_slice_in_dim(land, own, me, 0))
    return filled


def _col_windows(n_shards, width, gap_at=None, gap=0):
    out = []
    for j in range(n_shards):
        g0, g1 = j * width, (j + 1) * width
        cuts = [g0, g1] if gap_at is None or not g0 < gap_at < g1 else [g0, gap_at, g1]
        for a, b in zip(cuts[:-1], cuts[1:]):
            out.append((j, a - g0, b - g0, a + (gap if gap_at is not None and a >= gap_at else 0)))
    return out


def _assemble_cols(parts, total, windows, name, tr=256):
    n, rows, w = parts.shape
    tr = _tile(rows, tr, 16)

    def body(p_ref, o_ref):
        o_ref[...] = jnp.zeros_like(o_ref)
        for j, lo, hi, dst in windows:
            o_ref[:, dst:dst + hi - lo] = p_ref[j, :, lo:hi]

    return pl.pallas_call(
        body, name=name, grid=(rows // tr,),
        in_specs=[pl.BlockSpec((n, tr, w), lambda i: (0, i, 0))],
        out_specs=pl.BlockSpec((tr, total), lambda i: (i, 0)),
        out_shape=jax.ShapeDtypeStruct((rows, total), parts.dtype),
        compiler_params=_cparams(("parallel",)),
    )(parts)


def _split_cols(full, n, w, windows, name, tr=256):
    rows, total = full.shape
    tr = _tile(rows, tr, 16)

    def body(f_ref, o_ref):
        for j, lo, hi, dst in windows:
            o_ref[j, :, lo:hi] = f_ref[:, dst:dst + hi - lo].astype(o_ref.dtype)

    return pl.pallas_call(
        body, name=name, grid=(rows // tr,),
        in_specs=[pl.BlockSpec((tr, total), lambda i: (i, 0))],
        out_specs=pl.BlockSpec((n, tr, w), lambda i: (0, i, 0)),
        out_shape=jax.ShapeDtypeStruct((n, rows, w), BF16),
        compiler_params=_cparams(("parallel",)),
    )(full)


_DIMS = {"nn": (((1,), (0,)), ((), ())), "nt": (((1,), (1,)), ((), ())), "tn": (((0,), (0,)), ((), ()))}


def _matmul(a, b, *, mode, grid, a_block, a_map, b_block, b_map, o_block, o_map, out_shape, name,
            resid=None):
    nk = grid[2]
    dims = _DIMS[mode]

    def body(*refs):
        if resid is None:
            a_ref, b_ref, o_ref, acc_ref = refs
            r_ref = None
        else:
            a_ref, b_ref, r_ref, o_ref, acc_ref = refs
        k = pl.program_id(2)

        @pl.when(k == 0)
        def _():
            acc_ref[...] = jnp.zeros_like(acc_ref)

        acc_ref[...] += lax.dot_general(a_ref[...], b_ref[...], dims, preferred_element_type=F32)

        @pl.when(k == nk - 1)
        def _():
            res = acc_ref[...]
            if r_ref is not None:
                res = r_ref[...] + res
            o_ref[...] = res.astype(o_ref.dtype)

    in_specs = [pl.BlockSpec(a_block, a_map), pl.BlockSpec(b_block, b_map)]
    args = [a, b]
    if resid is not None:
        in_specs.append(pl.BlockSpec(o_block, o_map))
        args.append(resid)
    acc_shape = tuple(d for d in o_block if d is not None)
    return pl.pallas_call(
        body, name=name, grid=grid, in_specs=in_specs,
        out_specs=pl.BlockSpec(o_block, o_map), out_shape=out_shape,
        scratch_shapes=[pltpu.VMEM(acc_shape, F32)],
        compiler_params=_cparams(("parallel", "parallel", "arbitrary")),
    )(*args)


def _mm_nn(a, b, out_dtype, name, resid=None, tm=1024, tn=1024, tk=1024):
    m, kk = a.shape
    n = b.shape[1]
    tm, tn, tk = _tile(m, tm, 8), _tile(n, tn), _tile(kk, tk)
    return _matmul(a, b, mode="nn", grid=(m // tm, n // tn, kk // tk),
                   a_block=(tm, tk), a_map=lambda i, j, k: (i, k),
                   b_block=(tk, tn), b_map=lambda i, j, k: (k, j),
                   o_block=(tm, tn), o_map=lambda i, j, k: (i, j),
                   out_shape=jax.ShapeDtypeStruct((m, n), out_dtype), name=name, resid=resid)


def _mm_nt(a, b, out_dtype, name, tm=1024, tn=1024, tk=1024):
    m, kk = a.shape
    n = b.shape[0]
    tm, tn, tk = _tile(m, tm, 8), _tile(n, tn), _tile(kk, tk)
    return _matmul(a, b, mode="nt", grid=(m // tm, n // tn, kk // tk),
                   a_block=(tm, tk), a_map=lambda i, j, k: (i, k),
                   b_block=(tn, tk), b_map=lambda i, j, k: (j, k),
                   o_block=(tm, tn), o_map=lambda i, j, k: (i, j),
                   out_shape=jax.ShapeDtypeStruct((m, n), out_dtype), name=name)


def _mm_tn(a, b, name, tm=1024, tn=1024, tk=1024):
    kk, m = a.shape
    n = b.shape[1]
    tm, tn, tk = _tile(m, tm), _tile(n, tn), _tile(kk, tk, 8)
    return _matmul(a, b, mode="tn", grid=(m // tm, n // tn, kk // tk),
                   a_block=(tk, tm), a_map=lambda i, j, k: (k, i),
                   b_block=(tk, tn), b_map=lambda i, j, k: (k, j),
                   o_block=(tm, tn), o_map=lambda i, j, k: (i, j),
                   out_shape=jax.ShapeDtypeStruct((m, n), F32), name=name)


def _mm_heads(a, b, name, tm=1024, tn=640):
    m, kk = a.shape
    n = b.shape[1]
    tm, tn = _tile(m, tm, 16), _tile(n, tn)
    per_tile = tn // HEAD_DIM

    def body(a_ref, b_ref, o_ref):
        res = jnp.dot(a_ref[...], b_ref[...], preferred_element_type=F32)
        for hh in range(per_tile):
            o_ref[hh] = res[:, hh * HEAD_DIM:(hh + 1) * HEAD_DIM].astype(o_ref.dtype)

    return pl.pallas_call(
        body, name=name, grid=(m // tm, n // tn),
        in_specs=[pl.BlockSpec((tm, kk), lambda i, j: (i, 0)), pl.BlockSpec((kk, tn), lambda i, j: (0, j))],
        out_specs=pl.BlockSpec((per_tile, tm, HEAD_DIM), lambda i, j: (j, i, 0)),
        out_shape=jax.ShapeDtypeStruct((n // HEAD_DIM, m, HEAD_DIM), BF16),
        compiler_params=_cparams(("parallel", "parallel")),
    )(a, b)


def _mm_up(h, w_up, tm=2048, tn=256):
    s = h.shape[0]
    tm = _tile(s, tm, 8)
    nh = D_FF // tn
    return _matmul(h, w_up, mode="nn", grid=(s // tm, 2 * nh, 1),
                   a_block=(tm, D_MODEL), a_map=lambda i, j, k: (i, 0),
                   b_block=(D_MODEL, tn), b_map=lambda i, j, k: (0, j),
                   o_block=(None, tm, tn), o_map=lambda i, j, k: (j // nh, i, j % nh),
                   out_shape=jax.ShapeDtypeStruct((2, s, D_FF), F32), name="up_proj")


def _mm_dup_nt(dup, w_up, tm=1024, tk=1408):
    s = dup.shape[1]
    tm = _tile(s, tm, 8)
    nh = D_FF // tk
    return _matmul(dup, w_up, mode="nt", grid=(s // tm, 1, 2 * nh),
                   a_block=(None, tm, tk), a_map=lambda i, j, k: (k // nh, i, k % nh),
                   b_block=(D_MODEL, tk), b_map=lambda i, j, k: (0, k),
                   o_block=(tm, D_MODEL), o_map=lambda i, j, k: (i, 0),
                   out_shape=jax.ShapeDtypeStruct((s, D_MODEL), F32), name="d_h2")


def _mm_dwup_tn(h, dup, tn=1408, tk=1024):
    s = h.shape[0]
    tk = _tile(s, tk, 8)
    nh = D_FF // tn
    return _matmul(h, dup, mode="tn", grid=(1, 2 * nh, s // tk),
                   a_block=(tk, D_MODEL), a_map=lambda i, j, k: (k, 0),
                   b_block=(None, tk, tn), b_map=lambda i, j, k: (j // nh, k, j % nh),
                   o_block=(D_MODEL, tn), o_map=lambda i, j, k: (0, j),
                   out_shape=jax.ShapeDtypeStruct((D_MODEL, 2 * D_FF), F32), name="d_w_up")


def _rms_fwd(x, g, tr=256):
    s, d = x.shape
    tr = _tile(s, tr, 8)

    def body(x_ref, g_ref, o_ref):
        xv = x_ref[...]
        r = lax.rsqrt(jnp.mean(xv * xv, axis=-1, keepdims=True) + EPS)
        o_ref[...] = (xv * r * g_ref[...]).astype(o_ref.dtype)

    return pl.pallas_call(
        body, name="rms_fwd", grid=(s // tr,),
        in_specs=[pl.BlockSpec((tr, d), lambda i: (i, 0)), pl.BlockSpec((1, d), lambda i: (0, 0))],
        out_specs=pl.BlockSpec((tr, d), lambda i: (i, 0)),
        out_shape=jax.ShapeDtypeStruct((s, d), BF16),
        compiler_params=_cparams(("parallel",)),
    )(x, g)


def _group_rms_fwd(o_fox, o_sb, g_fox, g_sb, tr=256):
    nh, s, dh = o_fox.shape
    tr = _tile(s, tr, 8)

    def body(a_ref, b_ref, ga_ref, gb_ref, o_ref):
        for src, g_ref, lo in ((a_ref, ga_ref, 0), (b_ref, gb_ref, nh * dh)):
            heads = [src[hh] for hh in range(nh)]
            ss = heads[0] * heads[0]
            for xv in heads[1:]:
                ss = ss + xv * xv
            r = lax.rsqrt(jnp.sum(ss, axis=-1, keepdims=True) * (1.0 / (nh * dh)) + EPS)
            for hh, xv in enumerate(heads):
                o_ref[:, lo + hh * dh:lo + (hh + 1) * dh] = (xv * r * g_ref[hh]).astype(o_ref.dtype)

    heads_blk = pl.BlockSpec((nh, tr, dh), lambda i: (0, i, 0))
    gain = pl.BlockSpec((nh, 1, dh), lambda i: (0, 0, 0))
    return pl.pallas_call(
        body, name="group_rms_fwd", grid=(s // tr,),
        in_specs=[heads_blk, heads_blk, gain, gain],
        out_specs=pl.BlockSpec((tr, 2 * nh * dh), lambda i: (i, 0)),
        out_shape=jax.ShapeDtypeStruct((s, 2 * nh * dh), BF16),
        compiler_params=_cparams(("parallel",)),
    )(o_fox, o_sb, g_fox, g_sb)


def _group_rms_bwd(x, dy, g, *, dy_col, name, tr=256):
    nh, s, dh = x.shape
    tr = _tile(s, tr, 8)
    d = nh * dh

    def body(x_ref, dy_ref, g_ref, dx_ref, dg_ref):
        @pl.when(pl.program_id(0) == 0)
        def _():
            dg_ref[...] = jnp.zeros_like(dg_ref)

        dyv = dy_ref[...]
        xs_ = [x_ref[hh] for hh in range(nh)]
        dys = [dyv[:, hh * dh:(hh + 1) * dh] for hh in range(nh)]
        ss = xs_[0] * xs_[0]
        for xv in xs_[1:]:
            ss = ss + xv * xv
        r = lax.rsqrt(jnp.sum(ss, axis=-1, keepdims=True) * (1.0 / d) + EPS)
        xh = [xv * r for xv in xs_]
        gy = [dys[hh] * g_ref[hh] for hh in range(nh)]
        dot = xh[0] * gy[0]
        for hh in range(1, nh):
            dot = dot + xh[hh] * gy[hh]
        mean_dot = jnp.sum(dot, axis=-1, keepdims=True) * (1.0 / d)
        for hh in range(nh):
            dx_ref[hh] = r * (gy[hh] - xh[hh] * mean_dot)
            dg_ref[hh] += jnp.sum(dys[hh] * xh[hh], axis=0, keepdims=True)

    heads_blk = pl.BlockSpec((nh, tr, dh), lambda i: (0, i, 0))
    gain = pl.BlockSpec((nh, 1, dh), lambda i: (0, 0, 0))
    return pl.pallas_call(
        body, name=name, grid=(s // tr,),
        in_specs=[heads_blk, pl.BlockSpec((tr, d), lambda i: (i, dy_col)), gain],
        out_specs=[heads_blk, gain],
        out_shape=[jax.ShapeDtypeStruct((nh, s, dh), F32), jax.ShapeDtypeStruct((nh, 1, dh), F32)],
        compiler_params=_cparams(("arbitrary",)),
    )(x, dy, g)


def _merge_dproj(parts_fox, d_gate, parts_sb, tr=256):
    nh, s, dh = parts_fox[0].shape
    tr = _tile(s, tr, 16)

    def body(*refs):
        o_ref = refs[-1]
        gate_ref = refs[3]
        col = 0
        for ref in refs[:3]:
            for hh in range(nh):
                o_ref[:, col:col + dh] = ref[hh].astype(o_ref.dtype)
                col += dh
        o_ref[:, col:col + GATE_PAD] = jnp.zeros((tr, GATE_PAD), o_ref.dtype)
        o_ref[:, col:col + N_GROUP_HEADS] = gate_ref[...].astype(o_ref.dtype)
        col += GATE_PAD
        for ref in refs[4:7]:
            for hh in range(nh):
                o_ref[:, col:col + dh] = ref[hh].astype(o_ref.dtype)
                col += dh

    heads_blk = pl.BlockSpec((nh, tr, dh), lambda i: (0, i, 0))
    return pl.pallas_call(
        body, name="merge_d_proj", grid=(s // tr,),
        in_specs=[heads_blk] * 3 + [pl.BlockSpec((tr, N_GROUP_HEADS), lambda i: (i, 0))] + [heads_blk] * 3,
        out_specs=pl.BlockSpec((tr, IN_COLS_PAD), lambda i: (i, 0)),
        out_shape=jax.ShapeDtypeStruct((s, IN_COLS_PAD), BF16),
        compiler_params=_cparams(("parallel",)),
    )(*parts_fox, d_gate, *parts_sb)


def _rms_bwd(x, dy, g, resid, *, dy_col, name, want_bf16, tr=256):
    s, d = x.shape
    tr = _tile(s, tr, 8)
    has_resid = resid is not None

    def body(*refs):
        refs = list(refs)
        x_ref, dy_ref, g_ref = refs[:3]
        r_ref = refs[3] if has_resid else None
        outs = refs[4:] if has_resid else refs[3:]
        dx_ref = outs[0]
        dxb_ref = outs[1] if want_bf16 else None
        dg_ref = outs[-1]

        @pl.when(pl.program_id(0) == 0)
        def _():
            dg_ref[...] = jnp.zeros_like(dg_ref)

        xv = x_ref[...]
        dyv = dy_ref[...]
        r = lax.rsqrt(jnp.mean(xv * xv, axis=-1, keepdims=True) + EPS)
        xh = xv * r
        gy = dyv * g_ref[...]
        dx = r * (gy - xh * jnp.mean(xh * gy, axis=-1, keepdims=True))
        if r_ref is not None:
            dx = r_ref[...] + dx
        dx_ref[...] = dx
        if dxb_ref is not None:
            dxb_ref[...] = dx.astype(BF16)
        dg_ref[...] += jnp.sum(dyv * xh, axis=0, keepdims=True)

    row = pl.BlockSpec((tr, d), lambda i: (i, 0))
    in_specs = [row, pl.BlockSpec((tr, d), lambda i: (i, dy_col)), pl.BlockSpec((1, d), lambda i: (0, 0))]
    args = [x, dy, g]
    if has_resid:
        in_specs.append(row)
        args.append(resid)
    out_specs = [row]
    out_shape = [jax.ShapeDtypeStruct((s, d), F32)]
    if want_bf16:
        out_specs.append(row)
        out_shape.append(jax.ShapeDtypeStruct((s, d), BF16))
    out_specs.append(pl.BlockSpec((1, d), lambda i: (0, 0)))
    out_shape.append(jax.ShapeDtypeStruct((1, d), F32))
    return pl.pallas_call(
        body, name=name, grid=(s // tr,), in_specs=in_specs, out_specs=out_specs, out_shape=out_shape,
        compiler_params=_cparams(("arbitrary",)),
    )(*args)


def _loss_head(x2, target, g, tr=256):
    s, d = x2.shape
    tr = _tile(s, tr, 8)

    def body(x_ref, t_ref, g_ref, dx_ref, dxb_ref, dg_ref, loss_ref):
        @pl.when(pl.program_id(0) == 0)
        def _():
            dg_ref[...] = jnp.zeros_like(dg_ref)
            loss_ref[...] = jnp.zeros_like(loss_ref)

        xv = x_ref[...]
        gv = g_ref[...]
        r = lax.rsqrt(jnp.mean(xv * xv, axis=-1, keepdims=True) + EPS)
        xh = xv * r
        err = xh * gv - t_ref[...]
        loss_ref[...] += jnp.sum(jnp.mean(err * err, axis=-1, keepdims=True), axis=0, keepdims=True) * 0.5
        dyv = err * (1.0 / d)
        gy = dyv * gv
        dx = r * (gy - xh * jnp.mean(xh * gy, axis=-1, keepdims=True))
        dx_ref[...] = dx
        dxb_ref[...] = dx.astype(BF16)
        dg_ref[...] += jnp.sum(dyv * xh, axis=0, keepdims=True)

    row = pl.BlockSpec((tr, d), lambda i: (i, 0))
    return pl.pallas_call(
        body, name="loss_head", grid=(s // tr,),
        in_specs=[row, row, pl.BlockSpec((1, d), lambda i: (0, 0))],
        out_specs=[row, row, pl.BlockSpec((1, d), lambda i: (0, 0)), pl.BlockSpec((1, LANES), lambda i: (0, 0))],
        out_shape=[jax.ShapeDtypeStruct((s, d), F32), jax.ShapeDtypeStruct((s, d), BF16),
                   jax.ShapeDtypeStruct((1, d), F32), jax.ShapeDtypeStruct((1, LANES), F32)],
        compiler_params=_cparams(("arbitrary",)),
    )(x2, target, g)


def _conv_taps(cur, prev8, w, b, first):
    prev8 = jnp.where(first, 0.0, prev8)
    ext = jnp.concatenate([prev8, cur], axis=0)
    x1 = pltpu.roll(ext, 1, 0)[8:]
    x2 = pltpu.roll(ext, 2, 0)[8:]
    u = b + w[0:1] * x2
    u = u + w[1:2] * x1
    u = u + w[2:3] * cur
    return u, x1, x2


def _conv_gate_fwd(up, conv_w, conv_b, tm=1024, tn=256):
    s = up.shape[1]
    tm = _tile(s, tm, 8)
    nrb = s // tm
    rb8 = tm // 8

    def body(g_ref, v_ref, gp_ref, vp_ref, wg_ref, wv_ref, bg_ref, bv_ref, o_ref):
        first = pl.program_id(1) == 0
        ug, _, _ = _conv_taps(g_ref[...], gp_ref[...], wg_ref[...], bg_ref[...], first)
        uv, _, _ = _conv_taps(v_ref[...], vp_ref[...], wv_ref[...], bv_ref[...], first)
        sg = 1.0 / (1.0 + jnp.exp(-ug))
        o_ref[...] = (ug * sg * uv).astype(o_ref.dtype)

    def cur(h):
        return pl.BlockSpec((None, tm, tn), lambda j, i: (h, i, j))

    def prev(h):
        return pl.BlockSpec((None, 8, tn), lambda j, i: (h, jnp.maximum(i * rb8 - 1, 0), j))

    def par(h, r):
        return pl.BlockSpec((None, r, tn), lambda j, i: (h, 0, j))

    return pl.pallas_call(
        body, name="conv_gate_fwd", grid=(D_FF // tn, nrb),
        in_specs=[cur(0), cur(1), prev(0), prev(1), par(0, 3), par(1, 3), par(0, 1), par(1, 1)],
        out_specs=pl.BlockSpec((tm, tn), lambda j, i: (i, j)),
        out_shape=jax.ShapeDtypeStruct((s, D_FF), BF16),
        compiler_params=_cparams(("parallel", "parallel")),
    )(up, up, up, up, conv_w, conv_w, conv_b, conv_b)


def _conv_gate_bwd(up, dact, conv_w, conv_b, tm=512, tn=256):
    s = up.shape[1]
    tm = _tile(s, tm, 8)
    nrb = s // tm
    rb8 = tm // 8

    def body(g_ref, v_ref, gp_ref, vp_ref, da_ref, wg_ref, wv_ref, bg_ref, bv_ref,
             dup_ref, dcw_ref, dcb_ref, carry_ref):
        i = pl.program_id(1)
        first = i == nrb - 1

        @pl.when(i == 0)
        def _():
            carry_ref[...] = jnp.zeros_like(carry_ref)
            dcw_ref[...] = jnp.zeros_like(dcw_ref)
            dcb_ref[...] = jnp.zeros_like(dcb_ref)

        curs = (g_ref[...], v_ref[...])
        ws = (wg_ref[...], wv_ref[...])
        ug, g1, g2 = _conv_taps(curs[0], gp_ref[...], ws[0], bg_ref[...], first)
        uv, v1, v2 = _conv_taps(curs[1], vp_ref[...], ws[1], bv_ref[...], first)
        sg = 1.0 / (1.0 + jnp.exp(-ug))
        da = da_ref[...].astype(F32)
        d_v = da * (ug * sg)
        d_g = da * uv * (sg * (1.0 + ug * (1.0 - sg)))
        for h, (du, x0, x1, x2) in enumerate(((d_g, curs[0], g1, g2), (d_v, curs[1], v1, v2))):
            dcb_ref[h] += jnp.sum(du, axis=0, keepdims=True)
            dcw_ref[h, 0:1, :] += jnp.sum(du * x2, axis=0, keepdims=True)
            dcw_ref[h, 1:2, :] += jnp.sum(du * x1, axis=0, keepdims=True)
            dcw_ref[h, 2:3, :] += jnp.sum(du * x0, axis=0, keepdims=True)
            ext = jnp.concatenate([du, carry_ref[h]], axis=0)
            n1 = pltpu.roll(ext, tm + 7, 0)[:tm]
            n2 = pltpu.roll(ext, tm + 6, 0)[:tm]
            w = ws[h]
            dup_ref[h] = (w[2:3] * du + w[1:2] * n1 + w[0:1] * n2).astype(dup_ref.dtype)
            carry_ref[h] = du[:8]

    def cur(h):
        return pl.BlockSpec((None, tm, tn), lambda j, i: (h, nrb - 1 - i, j))

    def prev(h):
        return pl.BlockSpec((None, 8, tn), lambda j, i: (h, jnp.maximum((nrb - 1 - i) * rb8 - 1, 0), j))

    def par(h, r):
        return pl.BlockSpec((None, r, tn), lambda j, i: (h, 0, j))

    return pl.pallas_call(
        body, name="conv_gate_bwd", grid=(D_FF // tn, nrb),
        in_specs=[cur(0), cur(1), prev(0), prev(1),
                  pl.BlockSpec((tm, tn), lambda j, i: (nrb - 1 - i, j)),
                  par(0, 3), par(1, 3), par(0, 1), par(1, 1)],
        out_specs=[pl.BlockSpec((2, tm, tn), lambda j, i: (0, nrb - 1 - i, j)),
                   pl.BlockSpec((2, 3, tn), lambda j, i: (0, 0, j)),
                   pl.BlockSpec((2, 1, tn), lambda j, i: (0, 0, j))],
        out_shape=[jax.ShapeDtypeStruct((2, s, D_FF), BF16),
                   jax.ShapeDtypeStruct((2, 3, D_FF), F32),
                   jax.ShapeDtypeStruct((2, 1, D_FF), F32)],
        scratch_shapes=[pltpu.VMEM((2, 8, tn), F32)],
        compiler_params=_cparams(("parallel", "arbitrary")),
    )(up, up, up, up, dact, conv_w, conv_w, conv_b, conv_b)


def _split_dot(x, tri, terms):
    piece = x.astype(BF16)
    out = jnp.dot(piece, tri, preferred_element_type=F32)
    rest = x
    for _ in range(terms - 1):
        rest = rest - piece.astype(F32)
        piece = rest.astype(BF16)
        out = out + jnp.dot(piece, tri, preferred_element_type=F32)
    return out


def _split_dot_rhs(tri, x, terms):
    piece = x.astype(BF16)
    out = jnp.dot(tri, piece, preferred_element_type=F32)
    rest = x
    for _ in range(terms - 1):
        rest = rest - piece.astype(F32)
        piece = rest.astype(BF16)
        out = out + jnp.dot(tri, piece, preferred_element_type=F32)
    return out


def _tri(n, kind):
    r = lax.broadcasted_iota(jnp.int32, (n, n), 0)
    c = lax.broadcasted_iota(jnp.int32, (n, n), 1)
    cond = {"le": r <= c, "ge": r >= c, "lt": r < c, "gt": r > c}[kind]
    return jnp.where(cond, 1.0, 0.0).astype(BF16)


def _log_sigmoid(x):
    return jnp.minimum(x, 0.0) - jnp.log(1.0 + jnp.exp(-jnp.abs(x)))


def _forget_fwd(f_logit, bias):
    h, r, _ = f_logit.shape

    def body(x_ref, b_ref, o_ref):
        lf = _log_sigmoid(x_ref[...] + b_ref[...])
        within = _split_dot(lf, _tri(LANES, "le"), 3)
        row_tot = jnp.broadcast_to(within[:, LANES - 1:LANES], (r, LANES))
        before = _split_dot_rhs(_tri(r, "gt"), row_tot, 3)
        o_ref[...] = within + before

    blk = pl.BlockSpec((None, r, LANES), lambda i: (i, 0, 0))
    return pl.pallas_call(
        body, name="forget_cumsum_fwd", grid=(h,),
        in_specs=[blk, pl.BlockSpec((None, 1, LANES), lambda i: (i, 0, 0))],
        out_specs=blk, out_shape=jax.ShapeDtypeStruct((h, r, LANES), F32),
        compiler_params=_cparams(("parallel",)),
    )(f_logit, bias)


def _forget_bwd(f_logit, bias, ksum, qsum):
    h, r, _ = f_logit.shape

    def body(x_ref, b_ref, k_ref, q_ref, dx_ref, db_ref):
        d_f = q_ref[...] - k_ref[...]
        within = _split_dot(d_f, _tri(LANES, "ge"), 3)
        row_tot = jnp.broadcast_to(within[:, 0:1], (r, LANES))
        after = _split_dot_rhs(_tri(r, "lt"), row_tot, 3)
        xv = x_ref[...] + b_ref[...]
        dx = (within + after) * jnp.exp(_log_sigmoid(-xv))
        dx_ref[...] = dx
        db_ref[...] = jnp.broadcast_to(jnp.sum(dx), (1, LANES))

    blk = pl.BlockSpec((None, r, LANES), lambda i: (i, 0, 0))
    one = pl.BlockSpec((None, 1, LANES), lambda i: (i, 0, 0))
    return pl.pallas_call(
        body, name="forget_cumsum_bwd", grid=(h,),
        in_specs=[blk, one, blk, blk], out_specs=[blk, one],
        out_shape=[jax.ShapeDtypeStruct((h, r, LANES), F32), jax.ShapeDtypeStruct((h, 1, LANES), F32)],
        compiler_params=_cparams(("parallel",)),
    )(f_logit, bias, ksum, qsum)


def _head_specs(s, tq):
    qblk = pl.BlockSpec((None, tq, HEAD_DIM), lambda h, i: (h, i, 0))
    full = pl.BlockSpec((None, s, HEAD_DIM), lambda h, i: (h, 0, 0))
    col = pl.BlockSpec((None, tq, 1), lambda h, i: (h, i, 0))
    return qblk, full, col


def _qkv_specs(s, tq, offs):
    q_off, k_off, v_off = offs
    return (pl.BlockSpec((None, tq, HEAD_DIM), lambda h, i: (h + q_off, i, 0)),
            pl.BlockSpec((None, s, HEAD_DIM), lambda h, i: (h + k_off, 0, 0)),
            pl.BlockSpec((None, s, HEAD_DIM), lambda h, i: (h + v_off, 0, 0)))


def _scaled(q_ref):
    return (q_ref[...].astype(F32) * Q_SCALE).astype(BF16)


_NT = (((1,), (1,)), ((), ()))
_TN = (((0,), (0,)), ((), ()))


def _cols_minus_rows(rows, cols):
    return lax.broadcasted_iota(jnp.int32, (rows, cols), 1) - lax.broadcasted_iota(jnp.int32, (rows, cols), 0)


def _fox_fwd(qkv, offs, v_ones, f_col, f_row, tq, tk):
    h, s = N_GROUP_HEADS, qkv.shape[1]
    nk = s // tk
    assert tq == tk

    assert nk <= LANES

    def body(q_ref, k_ref, v_ref, fc_ref, fr_ref, o_ref, lse_ref, p_hbm, mb_ref, m_ref, acc_ref, z0, z1,
             p_stage, psem):
        hh = pl.program_id(0)
        i = pl.program_id(1)
        qs = _scaled(q_ref)
        fq = fc_ref[...]
        m_ref[...] = jnp.full_like(m_ref, NEG_BIG)
        acc_ref[...] = jnp.zeros_like(acc_ref)
        mb_ref[...] = jnp.zeros_like(mb_ref)
        lane = lax.broadcasted_iota(jnp.int32, (tq, LANES), 1)

        ahead = _cols_minus_rows(tq, tk)

        def block_of(j):
            return jnp.minimum(j, nk - 1)

        def keys_of(j):
            return pl.ds(pl.multiple_of(block_of(j) * tk, tk), tk)

        def logits(j):
            return lax.dot_general(qs, k_ref[keys_of(j), :], _NT, preferred_element_type=F32)

        def put(p, slot):
            st = (p % STAGE_DEPTH) * 2 + slot
            return pltpu.make_async_copy(p_stage.at[st], p_hbm.at[hh, i, block_of(2 * p + slot)], psem.at[st])

        def soft(p, slot, raw, masked):
            j = 2 * p + slot
            sc = raw + fq - fr_ref[block_of(j)]
            if masked:
                sc = jnp.where(ahead <= (i - j) * tk, sc, NEG_BIG)
            m_old = m_ref[...]
            m_new = jnp.maximum(m_old, jnp.max(sc, axis=-1, keepdims=True))
            pb = jnp.exp(sc - m_new).astype(BF16)
            p_stage[(p % STAGE_DEPTH) * 2 + slot] = pb
            mb_ref[...] = jnp.where(lane == j, m_new, mb_ref[...])
            acc_ref[...] = jnp.exp(m_old - m_new) * acc_ref[...] + jnp.dot(
                pb, v_ref[keys_of(j), :], preferred_element_type=F32)
            m_ref[...] = m_new

        z0[...] = logits(0)

        def trip(p, masked):
            @pl.when(p >= STAGE_DEPTH)
            def _():
                put(p - STAGE_DEPTH, 0).wait()
                put(p - STAGE_DEPTH, 1).wait()

            j = 2 * p
            z1[...] = logits(j + 1)
            soft(p, 0, z0[...], masked)
            z0[...] = logits(j + 2)
            soft(p, 1, z1[...], masked)
            put(p, 0).start()
            put(p, 1).start()

        def step(p, carry):
            trip(p, False)
            return carry

        lax.fori_loop(0, i // 2, step, 0)
        trip(i // 2, True)

        for back in range(STAGE_DEPTH, 0, -1):
            @pl.when(i // 2 + 1 >= back)
            def _():
                put(i // 2 + 1 - back, 0).wait()
                put(i // 2 + 1 - back, 1).wait()
        l = acc_ref[:, HEAD_DIM:HEAD_DIM + 1]
        o_ref[...] = acc_ref[:, :HEAD_DIM] / l
        lse_ref[...] = m_ref[...] + jnp.log(l)

    qblk, full, colspec = _head_specs(s, tq)
    q_in, k_in, _ = _qkv_specs(s, tq, offs)
    return pl.pallas_call(
        body, name="fox_fwd", grid=(h, s // tq),
        in_specs=[q_in, k_in, pl.BlockSpec((None, s, 2 * HEAD_DIM), lambda hh, i: (hh, 0, 0)), colspec,
                  pl.BlockSpec((None, nk, 1, tk), lambda hh, i: (hh, 0, 0, 0))],
        out_specs=[qblk, colspec, pl.BlockSpec(memory_space=pl.ANY),
                   pl.BlockSpec((None, tq, LANES), lambda hh, i: (hh, i, 0))],
        out_shape=[jax.ShapeDtypeStruct((h, s, HEAD_DIM), F32), jax.ShapeDtypeStruct((h, s, 1), F32),
                   jax.ShapeDtypeStruct((h, s // tq, nk, tq, tk), BF16),
                   jax.ShapeDtypeStruct((h, s, LANES), F32)],
        scratch_shapes=[pltpu.VMEM((tq, 1), F32), pltpu.VMEM((tq, 2 * HEAD_DIM), F32),
                        pltpu.VMEM((tq, tk), F32), pltpu.VMEM((tq, tk), F32),
                        pltpu.VMEM((2 * STAGE_DEPTH, tq, tk), BF16), pltpu.SemaphoreType.DMA((2 * STAGE_DEPTH,))],
        compiler_params=_cparams(("parallel", "parallel")),
    )(qkv, qkv, v_ones, f_col, f_row)


def _fox_bwd(qkv, offs, p_saved, m_blocks, o, lse, d_o, tq, tk):
    h, s = N_GROUP_HEADS, qkv.shape[1]
    nk = s // tk
    assert tq == tk

    def body(q_ref, k_ref, v_ref, p_hbm, mb_ref, o_ref, lse_ref, do_ref,
             dq_ref, dk_ref, dv_ref, ks_ref, qs_ref, dq_acc, qsum_acc, d0, d1, p_bufs, psem):
        dp_refs = (d0, d1)
        hh = pl.program_id(0)
        i = pl.program_id(1)

        @pl.when(i == 0)
        def _():
            dk_ref[...] = jnp.zeros_like(dk_ref)
            dv_ref[...] = jnp.zeros_like(dv_ref)
            ks_ref[...] = jnp.zeros_like(ks_ref)

        qs = _scaled(q_ref)
        lse_v = lse_ref[...]
        dob = do_ref[...].astype(BF16)
        delta = jnp.sum(dob.astype(F32) * o_ref[...], axis=-1, keepdims=True)
        dq_acc[...] = jnp.zeros_like(dq_acc)
        qsum_acc[...] = jnp.zeros_like(qsum_acc)
        lane = lax.broadcasted_iota(jnp.int32, (tq, LANES), 1)

        def block_of(j):
            return jnp.minimum(j, nk - 1)

        def keys_of(j):
            return pl.ds(pl.multiple_of(block_of(j) * tk, tk), tk)

        def fetch(pp, slot):
            st = (pp % STAGE_DEPTH) * 2 + slot
            tile = jnp.minimum(2 * pp + slot, 2 * (i // 2) + 1)
            return pltpu.make_async_copy(p_hbm.at[hh, i, tile], p_bufs.at[st], psem.at[st])

        def product(j, slot):
            dp_refs[slot][...] = lax.dot_general(dob, v_ref[keys_of(j), :], _NT, preferred_element_type=F32)

        def grads(pp, slot):
            j = 2 * pp + slot
            at = keys_of(j)
            m_j = jnp.sum(jnp.where(lane == j, mb_ref[...], 0.0), axis=-1, keepdims=True)
            p = p_bufs[(pp % STAGE_DEPTH) * 2 + slot].astype(F32) * jnp.exp(m_j - lse_v)
            ds = p * (dp_refs[slot][...] - delta)
            dsb = ds.astype(BF16)
            dq_acc[...] += jnp.dot(dsb, k_ref[at, :], preferred_element_type=F32)
            dk_ref[at, :] += lax.dot_general(dsb, qs, _TN, preferred_element_type=F32)
            dv_ref[at, :] += lax.dot_general(p.astype(BF16), dob, _TN, preferred_element_type=F32)
            ks_ref[block_of(j)] += jnp.sum(ds.reshape(tq // 8, 8, tk), axis=0)
            qsum_acc[...] += jnp.sum(ds, axis=-1, keepdims=True)

        def trip(pp):
            for slot in (0, 1):
                fetch(pp + STAGE_DEPTH - 1, slot).start()
            for slot in (0, 1):
                fetch(pp, slot).wait()
            for slot in (0, 1):
                product(2 * pp + slot + 1, 1 - slot)
                grads(pp, slot)

        def step(pp, carry):
            trip(pp)
            return carry

        for lead in range(STAGE_DEPTH - 1):
            for slot in (0, 1):
                fetch(lead, slot).start()
        product(0, 0)
        lax.fori_loop(0, i // 2 + 1, step, 0)
        for lead in range(STAGE_DEPTH - 1):
            for slot in (0, 1):
                fetch(i // 2 + 1 + lead, slot).wait()
        dq_ref[...] = dq_acc[...] * Q_SCALE
        qs_ref[...] = qsum_acc[...]

    qblk, full, colspec = _head_specs(s, tq)
    big = pltpu.VMEM((tq, tk), F32)
    return pl.pallas_call(
        body, name="fox_bwd", grid=(h, s // tq),
        in_specs=[*_qkv_specs(s, tq, offs), pl.BlockSpec(memory_space=pl.ANY),
                  pl.BlockSpec((None, tq, LANES), lambda hh, i: (hh, i, 0)), qblk, colspec, qblk],
        out_specs=[qblk, full, full, pl.BlockSpec((None, nk, 8, tk), lambda hh, i: (hh, 0, 0, 0)), colspec],
        out_shape=[jax.ShapeDtypeStruct((h, s, HEAD_DIM), F32)] * 3
        + [jax.ShapeDtypeStruct((h, nk, 8, tk), F32), jax.ShapeDtypeStruct((h, s, 1), F32)],
        scratch_shapes=[pltpu.VMEM((tq, HEAD_DIM), F32), pltpu.VMEM((tq, 1), F32), big, big,
                        pltpu.VMEM((2 * STAGE_DEPTH, tq, tk), BF16), pltpu.SemaphoreType.DMA((2 * STAGE_DEPTH,))],
        compiler_params=_cparams(("parallel", "arbitrary")),
    )(qkv, qkv, qkv, p_saved, m_blocks, o, lse, d_o)


STAGE_DEPTH = 4
SB_TERMS = 2
G_TERMS = 1
LOG2E = 1.4426950408889634
LN2 = 0.6931471805599453


def _softplus2(z2):
    return jnp.maximum(z2, 0.0) + jnp.log2(1.0 + jnp.exp2(-jnp.abs(z2)))


def _sb_fwd(qkv, offs, tq, tk):
    h, s = N_GROUP_HEADS, qkv.shape[1]

    assert tq % (2 * tk) == 0

    def body(q_ref, k_ref, v_ref, o_ref, w_hbm, acc_ref, run_ref, z0, z1, d0, d1, t0, t1, w_stage, wsem):
        z_refs, d_refs, t_refs = (z0, z1), (d0, d1), (t0, t1)
        hh = pl.program_id(0)
        i = pl.program_id(1)
        qs = _scaled(q_ref)
        tri = _tri(tk, "ge")
        acc_ref[...] = jnp.zeros_like(acc_ref)
        run_ref[...] = jnp.zeros_like(run_ref)
        nb = (i + 1) * (tq // tk)
        ahead = _cols_minus_rows(tq, tk)

        def keys_of(b):
            j = nb - 1 - jnp.minimum(b, nb - 1)
            return pl.ds(pl.multiple_of(j * tk, tk), tk)

        def visible(b):
            return ahead < i * tq - (nb - 1 - b) * tk

        def logits(b, slot):
            z_refs[slot][...] = lax.dot_general(qs, k_ref[keys_of(b), :], _NT,
                                                preferred_element_type=F32) * LOG2E

        def sums(b, slot, masked):
            z2 = z_refs[slot][...]
            sp = _softplus2(z2)
            if masked:
                sp = jnp.where(visible(b), sp, 0.0)
            inc = _split_dot(sp, tri, SB_TERMS)
            d_refs[slot][...] = z2 - inc
            t_refs[slot][...] = inc[:, 0:1]

        def put(p, slot):
            st = (p % STAGE_DEPTH) * 2 + slot
            return pltpu.make_async_copy(w_stage.at[st], w_hbm.at[hh, i, nb - 1 - (2 * p + slot)], wsem.at[st])

        def weigh(p, slot, masked):
            b = 2 * p + slot
            w = jnp.exp2(d_refs[slot][...] - run_ref[...])
            if masked:
                w = jnp.where(visible(b), w, 0.0)
            wb = w.astype(BF16)
            w_stage[(p % STAGE_DEPTH) * 2 + slot] = wb
            acc_ref[...] += jnp.dot(wb, v_ref[keys_of(b), :], preferred_element_type=F32)
            run_ref[...] += t_refs[slot][...]

        def trip(p, masked):
            @pl.when(p >= STAGE_DEPTH)
            def _():
                put(p - STAGE_DEPTH, 0).wait()
                put(p - STAGE_DEPTH, 1).wait()

            b = 2 * p
            logits(b + 2, 0)
            sums(b + 1, 1, masked)
            weigh(p, 0, masked)
            logits(b + 3, 1)
            sums(b + 2, 0, masked)
            weigh(p, 1, masked)
            put(p, 0).start()
            put(p, 1).start()

        logits(0, 0)
        logits(1, 1)
        sums(0, 0, True)

        def guarded(p, carry):
            trip(p, True)
            return carry

        def plain(p, carry):
            trip(p, False)
            return carry

        lax.fori_loop(0, tq // tk // 2, guarded, 0)
        lax.fori_loop(tq // tk // 2, nb // 2, plain, 0)
        trips = nb // 2

        for back in range(STAGE_DEPTH, 0, -1):
            @pl.when(trips >= back)
            def _():
                put(trips - back, 0).wait()
                put(trips - back, 1).wait()
        o_ref[...] = acc_ref[...]

    qblk, full, colspec = _head_specs(s, tq)
    return pl.pallas_call(
        body, name="sb_fwd", grid=(h, s // tq),
        in_specs=[*_qkv_specs(s, tq, offs)], out_specs=[qblk, pl.BlockSpec(memory_space=pl.ANY)],
        out_shape=[jax.ShapeDtypeStruct((h, s, HEAD_DIM), F32),
                   jax.ShapeDtypeStruct((h, s // tq, s // tk, tq, tk), BF16)],
        scratch_shapes=[pltpu.VMEM((tq, HEAD_DIM), F32), pltpu.VMEM((tq, 1), F32),
                        pltpu.VMEM((tq, tk), F32), pltpu.VMEM((tq, tk), F32),
                        pltpu.VMEM((tq, tk), F32), pltpu.VMEM((tq, tk), F32),
                        pltpu.VMEM((tq, 1), F32), pltpu.VMEM((tq, 1), F32),
                        pltpu.VMEM((2 * STAGE_DEPTH, tq, tk), BF16), pltpu.SemaphoreType.DMA((2 * STAGE_DEPTH,))],
        compiler_params=_cparams(("parallel", "parallel")),
    )(qkv, qkv, qkv)


def _sb_bwd(qkv, offs, w_saved, d_o, tq, tk):
    h, s = N_GROUP_HEADS, qkv.shape[1]

    assert tq % (2 * tk) == 0

    def body(q_ref, k_ref, v_ref, do_ref, w_hbm, dq_ref, dk_ref, dv_ref, dq_acc, grun_ref,
             z0, z1, p0, p1, w_bufs, wsem):
        z_refs, p_refs = (z0, z1), (p0, p1)
        hh = pl.program_id(0)
        i = pl.program_id(1)

        @pl.when(i == 0)
        def _():
            dk_ref[...] = jnp.zeros_like(dk_ref)
            dv_ref[...] = jnp.zeros_like(dv_ref)

        qs = _scaled(q_ref)
        dob = do_ref[...].astype(BF16)
        tri = _tri(tk, "le")
        dq_acc[...] = jnp.zeros_like(dq_acc)
        grun_ref[...] = jnp.zeros_like(grun_ref)
        nb = (i + 1) * (tq // tk)
        ahead = _cols_minus_rows(tq, tk)

        def block_of(b):
            return jnp.minimum(b, nb - 1)

        def keys_of(b):
            return pl.ds(pl.multiple_of(block_of(b) * tk, tk), tk)

        def visible(b):
            return ahead < i * tq - b * tk

        def fetch(p, slot):
            st = (p % STAGE_DEPTH) * 2 + slot
            return pltpu.make_async_copy(w_hbm.at[hh, i, block_of(2 * p + slot)], w_bufs.at[st], wsem.at[st])

        def products(b, slot):
            at = keys_of(b)
            z_refs[slot][...] = lax.dot_general(qs, k_ref[at, :], _NT, preferred_element_type=F32) * LOG2E
            p_refs[slot][...] = lax.dot_general(dob, v_ref[at, :], _NT, preferred_element_type=F32)

        def grads(p, slot, masked):
            b = 2 * p + slot
            at = keys_of(b)
            wb = w_bufs[(p % STAGE_DEPTH) * 2 + slot]
            g = wb.astype(F32) * p_refs[slot][...]
            ginc = _split_dot(g, tri, G_TERMS)
            beta = 1.0 / (1.0 + jnp.exp2(-z_refs[slot][...]))
            dz = g - beta * (grun_ref[...] + ginc)
            if masked:
                dz = jnp.where(visible(b), dz, 0.0)
            dzb = dz.astype(BF16)
            dq_acc[...] += jnp.dot(dzb, k_ref[at, :], preferred_element_type=F32)
            dk_ref[at, :] += lax.dot_general(dzb, qs, _TN, preferred_element_type=F32)
            dv_ref[at, :] += lax.dot_general(wb, dob, _TN, preferred_element_type=F32)
            grun_ref[...] += ginc[:, tk - 1:tk]

        def trip(p, masked):
            for slot in (0, 1):
                fetch(p + STAGE_DEPTH - 1, slot).start()
            for slot in (0, 1):
                fetch(p, slot).wait()
            for slot in (0, 1):
                products(2 * p + slot + 1, 1 - slot)
                grads(p, slot, masked)

        for lead in range(STAGE_DEPTH - 1):
            for slot in (0, 1):
                fetch(lead, slot).start()
        products(0, 0)
        n_plain = (nb - tq // tk) // 2

        def plain(p, carry):
            trip(p, False)
            return carry

        def guarded(p, carry):
            trip(p, True)
            return carry

        lax.fori_loop(0, n_plain, plain, 0)
        lax.fori_loop(n_plain, nb // 2, guarded, 0)
        for lead in range(STAGE_DEPTH - 1):
            for slot in (0, 1):
                fetch(nb // 2 + lead, slot).wait()
        dq_ref[...] = dq_acc[...] * Q_SCALE

    qblk, full, colspec = _head_specs(s, tq)
    big = pltpu.VMEM((tq, tk), F32)
    return pl.pallas_call(
        body, name="sb_bwd", grid=(h, s // tq),
        in_specs=[*_qkv_specs(s, tq, offs), qblk, pl.BlockSpec(memory_space=pl.ANY)], out_specs=[qblk, full, full],
        out_shape=[jax.ShapeDtypeStruct((h, s, HEAD_DIM), F32)] * 3,
        scratch_shapes=[pltpu.VMEM((tq, HEAD_DIM), F32), pltpu.VMEM((tq, 1), F32)]
        + [big] * 4 + [pltpu.VMEM((2 * STAGE_DEPTH, tq, tk), BF16), pltpu.SemaphoreType.DMA((2 * STAGE_DEPTH,))],
        compiler_params=_cparams(("parallel", "arbitrary")),
    )(qkv, qkv, qkv, d_o, w_saved)


def _sum_adamw(parts, w, m, v, name, tr=256):
    _, rows, lanes = parts.shape
    tr = _tile(rows, tr, 16)
    c_m = 1.0 - ADAM_B1 ** ADAM_STEP
    c_v = 1.0 - ADAM_B2 ** ADAM_STEP

    def body(p_ref, w_ref, m_ref, v_ref, g_ref, d_ref, nm_ref, nv_ref):
        g = p_ref[0].astype(F32)
        for j in range(1, N_DEV):
            g = g + p_ref[j].astype(F32)
        nm = ADAM_B1 * m_ref[...] + (1.0 - ADAM_B1) * g
        nv = ADAM_B2 * v_ref[...] + (1.0 - ADAM_B2) * (g * g)
        m_hat = nm / c_m
        v_hat = nv / c_v
        g_ref[...] = g
        d_ref[...] = -ADAM_LR * (m_hat / (jnp.sqrt(v_hat) + ADAM_EPS) + ADAM_WD * w_ref[...])
        nm_ref[...] = nm
        nv_ref[...] = nv

    blk = pl.BlockSpec((tr, lanes), lambda i: (i, 0))
    return pl.pallas_call(
        body, name=name, grid=(rows // tr,),
        in_specs=[pl.BlockSpec((N_DEV, tr, lanes), lambda i: (0, i, 0)), blk, blk, blk],
        out_specs=[blk] * 4, out_shape=[jax.ShapeDtypeStruct((rows, lanes), F32)] * 4,
        compiler_params=_cparams(("parallel",)),
    )(parts, w, m, v)


def kernel(x, attn_norm_g, w_in, forget_bias, fox_out_g, sb_out_g, w_out, ffn_norm_g, w_up, conv_w, conv_b, w_down, final_norm_g, loss_target, m_attn_norm_g, m_w_in, m_forget_bias, m_fox_out_g, m_sb_out_g, m_w_out, m_ffn_norm_g, m_w_up, m_conv_w, m_conv_b, m_w_down, m_final_norm_g, v_attn_norm_g, v_w_in, v_forget_bias, v_fox_out_g, v_sb_out_g, v_w_out, v_ffn_norm_g, v_w_up, v_conv_w, v_conv_b, v_w_down, v_final_norm_g):
    s = x.shape[1]
    xs = x[0]
    tq = min(ATTN_TQ, s)
    tk_fox = min(FOX_TK, s)
    tk_sb = min(SB_TK, s)
    in_shard, up_shard, out_shard, down_shard = IN_COLS // N_DEV, 2 * D_FF // N_DEV, D_MODEL // N_DEV, D_FF // N_DEV

    cw = conv_w[0]
    cw_hi = cw.astype(BF16)
    cw_lo = (cw - cw_hi.astype(F32)).astype(BF16)
    (g_in,) = _all_gather([w_in[0].astype(BF16)])
    rest = _exchange_start([w_out[0].astype(BF16), w_up[0].astype(BF16), w_down[0].astype(BF16),
                            jnp.stack([cw_hi, cw_lo])], False, "weights_rest_start")
    n_gate = QKV_W + N_GROUP_HEADS
    in_windows = _col_windows(N_DEV, in_shard, gap_at=n_gate, gap=GATE_PAD - N_GROUP_HEADS)
    up_windows = _col_windows(N_DEV, up_shard)
    w_in_p = _assemble_cols(g_in, IN_COLS_PAD, in_windows, "assemble_w_in")
    conv_b2 = conv_b.reshape(2, 1, D_FF)

    h1 = _rms_fwd(xs, attn_norm_g + rest[-1][0:1, 0:1])
    proj_h = _mm_heads(h1, w_in_p, "in_proj")
    fox_offs = (0, N_GROUP_HEADS, 2 * N_GROUP_HEADS)
    sb_first = 3 * N_GROUP_HEADS + GATE_PAD // HEAD_DIM
    sb_offs = (sb_first, sb_first + N_GROUP_HEADS, sb_first + 2 * N_GROUP_HEADS)
    f_logit = _mm_nn(h1, w_in_p[:, QKV_W:QKV_W + GATE_PAD], F32, "gate_proj")[:, :N_GROUP_HEADS]
    fv = proj_h[2 * N_GROUP_HEADS:3 * N_GROUP_HEADS]

    f_logit_h = f_logit.T.reshape(N_GROUP_HEADS, s // LANES, LANES)
    bias_h = jnp.broadcast_to(forget_bias.reshape(N_GROUP_HEADS, 1, 1), (N_GROUP_HEADS, 1, LANES))
    big_f = _forget_fwd(f_logit_h, bias_h)
    f_col = big_f.reshape(N_GROUP_HEADS, s, 1)
    f_row = big_f.reshape(N_GROUP_HEADS, s // tk_fox, 1, tk_fox)

    fv_ones = jnp.concatenate([fv, jnp.ones_like(fv)], axis=-1)
    o_fox_h, lse, fox_p, fox_m = _fox_fwd(proj_h, fox_offs, fv_ones, f_col, f_row, tq, tk_fox)
    o_sb_h, sb_w = _sb_fwd(proj_h, sb_offs, min(SB_TQ, s), tk_sb)
    g_fox_h = fox_out_g.reshape(N_GROUP_HEADS, 1, HEAD_DIM)
    g_sb_h = sb_out_g.reshape(N_GROUP_HEADS, 1, HEAD_DIM)
    o_n = _group_rms_fwd(o_fox_h, o_sb_h, g_fox_h, g_sb_h)
    g_out, g_up, g_down, g_conv = _exchange_wait(rest, False, o_n, "weights_rest_wait")
    w_out_f = g_out.reshape(D_MODEL, D_MODEL)
    w_up_f = _assemble_cols(g_up, 2 * D_FF, up_windows, "assemble_w_up")
    w_down_f = g_down.reshape(D_FF, D_MODEL)
    conv_w_f = (g_conv[:, 0].astype(F32) + g_conv[:, 1].astype(F32)).transpose(1, 0, 2).reshape(3, 2 * D_FF)
    conv_w2 = conv_w_f.reshape(3, 2, D_FF).transpose(1, 0, 2)
    x1 = _mm_nn(o_n, w_out_f, F32, "out_proj", resid=xs)
    h2 = _rms_fwd(x1, ffn_norm_g)
    up = _mm_up(h2, w_up_f)
    act = _conv_gate_fwd(up, conv_w2, conv_b2)
    x2 = _mm_nn(act, w_down_f, F32, "down_proj", resid=x1, tk=1408)

    d_x2, d_x2b, dg_final, loss_part = _loss_head(x2, loss_target[0], final_norm_g.reshape(1, D_MODEL))
    d_act = _mm_nt(d_x2b, w_down_f, BF16, "d_act", tn=1408)
    dw_down = _mm_tn(act, d_x2b, "d_w_down", tm=1408)
    d_up, dcw2, dcb2 = _conv_gate_bwd(up, d_act, conv_w2, conv_b2)
    d_h2 = _mm_dup_nt(d_up, w_up_f)
    dw_up = _mm_dwup_tn(h2, d_up)
    d_x1, d_x1b, dg_ffn = _rms_bwd(x1, d_h2, ffn_norm_g, d_x2, dy_col=0, name="ffn_norm_bwd", want_bf16=True)
    d_on = _mm_nt(d_x1b, w_out_f, F32, "d_o_normed")
    dw_out = _mm_tn(o_n, d_x1b, "d_w_out")
    early = _exchange_start(
        [dw_out.astype(BF16).reshape(N_DEV, out_shard, D_MODEL),
         _split_cols(dw_up, N_DEV, up_shard, up_windows, "split_d_w_up"),
         dw_down.astype(BF16).reshape(N_DEV, down_shard, D_MODEL)], True, "grads_early_start")
    g_fox_t = g_fox_h + early[-1][0:1, 0:1]
    d_o_fox_h, dg_fox = _group_rms_bwd(o_fox_h, d_on, g_fox_t, dy_col=0, name="fox_norm_bwd")
    d_o_sb_h, dg_sb = _group_rms_bwd(o_sb_h, d_on, g_sb_h, dy_col=1, name="sb_norm_bwd")

    dfq, dfk, dfv, ksum8, qsum = _fox_bwd(proj_h, fox_offs, fox_p, fox_m, o_fox_h, lse, d_o_fox_h, tq, tk_fox)
    dsq, dsk, dsv = _sb_bwd(proj_h, sb_offs, sb_w, d_o_sb_h, min(SB_TQ, s), tk_sb)
    ksum = jnp.sum(ksum8, axis=2).reshape(N_GROUP_HEADS, s // LANES, LANES)
    d_f_logit_h, d_bias_h = _forget_bwd(f_logit_h, bias_h, ksum,
                                        qsum.reshape(N_GROUP_HEADS, s // LANES, LANES))
    d_f_logit = d_f_logit_h.reshape(N_GROUP_HEADS, s).T

    d_proj = _merge_dproj((dfq, dfk, dfv), d_f_logit, (dsq, dsk, dsv))
    dw_in_p = _mm_tn(h1, d_proj, "d_w_in", tn=640)
    dconv_w = dcw2.transpose(1, 0, 2).reshape(3, 2 * D_FF)
    dconv_b = dcb2.reshape(1, 2 * D_FF)
    late = _exchange_start(
        [_split_cols(dw_in_p, N_DEV, in_shard, in_windows, "split_d_w_in"),
         dconv_w.astype(BF16).reshape(3, N_DEV, up_shard).transpose(1, 0, 2)],
        True, "grads_late_start")
    d_h1 = _mm_nt(d_proj, w_in_p + late[-1][0:1, 0:1].astype(BF16), F32, "d_h1", tk=640)
    grad_x, dg_attn = _rms_bwd(xs, d_h1, attn_norm_g, d_x1, dy_col=0, name="attn_norm_bwd", want_bf16=False)

    small_shapes = [(1, D_MODEL), (1, N_GROUP_HEADS), (1, GROUP_W), (1, GROUP_W), (1, D_MODEL),
                    (1, 2 * D_FF), (D_MODEL,), (1,)]
    spack = _pack([dg_attn, d_bias_h[:, 0, 0], dg_fox, dg_sb, dg_ffn, dconv_b, dg_final, loss_part[0, 0:1]],
                  SMALL_ROWS, F32)
    (srecv,) = _grad_exchange([], spack)
    r_out, r_up, r_down = _exchange_wait(early, True, srecv, "grads_early_wait")
    r_in, r_conv = _exchange_wait(late, True, r_out, "grads_late_wait")

    big = [_sum_adamw(g, w_[0], m_[0], v_[0], "adamw_" + tag)
           for g, w_, m_, v_, tag in zip(
               (r_in, r_out, r_up, r_down, r_conv), (w_in, w_out, w_up, w_down, conv_w), (m_w_in, m_w_out, m_w_up, m_w_down, m_conv_w),
               (v_w_in, v_w_out, v_w_up, v_w_down, v_conv_w), ("w_in", "w_out", "w_up", "w_down", "conv_w"))]

    def small_pack(a_attn, a_bias, a_fox, a_sb, a_ffn, a_cb, a_fin):
        return _pack([a_attn, a_bias, a_fox, a_sb, a_ffn, a_cb, a_fin, jnp.zeros((1,), F32)], SMALL_ROWS, F32)

    small = _sum_adamw(srecv, small_pack(attn_norm_g, forget_bias, fox_out_g, sb_out_g, ffn_norm_g, conv_b, final_norm_g),
                       small_pack(m_attn_norm_g, m_forget_bias, m_fox_out_g, m_sb_out_g, m_ffn_norm_g, m_conv_b, m_final_norm_g),
                       small_pack(v_attn_norm_g, v_forget_bias, v_fox_out_g, v_sb_out_g, v_ffn_norm_g, v_conv_b, v_final_norm_g),
                       "adamw_replicated", tr=SMALL_ROWS)

    outs = []
    loss = None
    for kind in range(4):
        b_in, b_out, b_up, b_down, b_conv = (res[kind] for res in big)
        s_attn, s_bias, s_fox, s_sb, s_ffn, s_cb, s_fin, s_loss = _unpack(small[kind], small_shapes)
        if kind == 0:
            loss = s_loss[0]
        outs += [s_attn, b_in[None], s_bias, s_fox, s_sb, b_out[None], s_ffn, b_up[None], b_conv[None], s_cb,
                 b_down[None], s_fin]
    return (loss, grad_x[None], *outs)
```

```python
import jax
import jax.numpy as jnp
from jax import lax
from jax.experimental import pallas as pl
from jax.experimental.pallas import tpu as pltpu

F32 = jnp.float32
BF16 = jnp.bfloat16

D_MODEL = 1024
HEAD_DIM = 64
N_GROUP_HEADS = 8
GROUP_W = N_GROUP_HEADS * HEAD_DIM
QKV_W = 3 * GROUP_W
IN_COLS = 2 * QKV_W + N_GROUP_HEADS
GATE_PAD = 128
IN_COLS_PAD = 2 * QKV_W + GATE_PAD
D_FF = 2816
N_DEV = 8
EPS = 1e-6
Q_SCALE = HEAD_DIM ** -0.5

ADAM_LR = 0.001
ADAM_B1 = 0.9
ADAM_B2 = 0.999
ADAM_EPS = 1e-08
ADAM_WD = 0.01
ADAM_STEP = 10

LANES = 128
SMALL_ROWS = 80
VMEM_LIMIT = 56 * 1024 * 1024
NEG_BIG = -1e30
ATTN_TQ = 512
SB_TQ = 512
FOX_TK = 512
SB_TK = 256
MESH = pl.DeviceIdType.MESH


def _cparams(sem=None, **kw):
    return pltpu.CompilerParams(dimension_semantics=sem, vmem_limit_bytes=VMEM_LIMIT, **kw)


def _tile(n, target, mult=LANES):
    if n <= target:
        return n
    t = (target // mult) * mult
    while t >= mult:
        if n % t == 0:
            return t
        t -= mult
    return n


def _seg_len(shape):
    n = 1
    for s in shape:
        n *= s
    return -(-n // LANES) * LANES


def _pack(arrs, rows, dtype):
    parts = []
    for a in arrs:
        f = a.reshape(-1).astype(dtype)
        parts.append(jnp.pad(f, (0, _seg_len(a.shape) - f.shape[0])))
    flat = jnp.concatenate(parts)
    flat = jnp.pad(flat, (0, rows * LANES - flat.shape[0]))
    return flat.reshape(rows, LANES)


def _unpack(p, shapes, lead=()):
    flat = p.reshape(lead + (-1,))
    out, off = [], 0
    for shp in shapes:
        n = 1
        for s in shp:
            n *= s
        out.append(flat[..., off:off + n].reshape(lead + tuple(shp)))
        off += _seg_len(shp)
    return out


def _my_pos():
    return lax.axis_index("x"), lax.axis_index("y"), lax.axis_index("c")


def _all_gather(blocks):
    n = len(blocks)

    def body(*refs):
        x_refs, out_refs = refs[:n], refs[n:2 * n]
        send_sems, recv_sems, local_sems = refs[2 * n:]
        x, y, c = _my_pos()
        me, sibling = (x, y, c), (x, y, 1 - c)
        chips = [(1 - x, y), (x, 1 - y), (1 - x, 1 - y)]

        def copy(a, k, blk, to, own=False):
            px, py, pc = blk
            slot = out_refs[a].at[4 * px + 2 * py + pc]
            return pltpu.make_async_remote_copy(
                src_ref=x_refs[a] if own else slot, dst_ref=slot,
                send_sem=send_sems.at[a, k], recv_sem=recv_sems.at[a, k],
                device_id=to, device_id_type=MESH)

        mine = [pltpu.make_async_copy(x_refs[a], out_refs[a].at[4 * x + 2 * y + c], local_sems.at[a])
                for a in range(n)]
        for cp in mine:
            cp.start()
        first = []
        for a in range(n):
            first.append(copy(a, 0, me, sibling, own=True))
            first += [copy(a, 1 + j, me, (*chip, c), own=True) for j, chip in enumerate(chips)]
        for cp in first:
            cp.start()
        passed = []
        for j, chip in enumerate(chips):
            for a in range(n):
                copy(a, 1 + j, (*chip, c), me).wait_recv()
                passed.append(copy(a, 4 + j, (*chip, c), sibling))
                passed[-1].start()
        for a in range(n):
            copy(a, 0, sibling, me).wait_recv()
            for j, chip in enumerate(chips):
                copy(a, 4 + j, (*chip, 1 - c), me).wait_recv()
        for cp in first + passed:
            cp.wait_send()
        for cp in mine:
            cp.wait()

    hbm = pl.BlockSpec(memory_space=pl.ANY)
    return pl.pallas_call(
        body, name="weights_all_gather",
        out_shape=[jax.ShapeDtypeStruct((N_DEV,) + b.shape, b.dtype) for b in blocks],
        in_specs=[hbm] * n, out_specs=[hbm] * n,
        scratch_shapes=[pltpu.SemaphoreType.DMA((n, 7)), pltpu.SemaphoreType.DMA((n, 7)),
                        pltpu.SemaphoreType.DMA((n,))],
    )(*blocks)


def _grad_exchange(slabs, spack):
    n = len(slabs) + 1

    def body(*refs):
        in_refs, out_refs = refs[:n], refs[n:2 * n]
        send_sems, recv_sems, local_sems = refs[2 * n:]
        x, y, c = _my_pos()
        my_id = 4 * x + 2 * y + c

        def src_of(a, dev):
            return in_refs[a] if a == n - 1 else in_refs[a].at[dev]

        own = [pltpu.make_async_copy(src_of(a, my_id), out_refs[a].at[my_id], local_sems.at[a])
               for a in range(n)]
        for cp in own:
            cp.start()
        sends, arrivals = [], []
        for k in range(1, N_DEV):
            px, py, pc = x ^ (k >> 2), y ^ ((k >> 1) & 1), c ^ (k & 1)
            peer_id = 4 * px + 2 * py + pc
            for a in range(n):
                for dst_slot, bucket in ((my_id, sends), (peer_id, arrivals)):
                    bucket.append(pltpu.make_async_remote_copy(
                        src_ref=src_of(a, peer_id), dst_ref=out_refs[a].at[dst_slot],
                        send_sem=send_sems.at[a, k - 1], recv_sem=recv_sems.at[a, k - 1],
                        device_id=(px, py, pc), device_id_type=MESH))
        for cp in sends:
            cp.start()
        for cp in arrivals:
            cp.wait_recv()
        for cp in sends:
            cp.wait_send()
        for cp in own:
            cp.wait()

    hbm = pl.BlockSpec(memory_space=pl.ANY)
    return pl.pallas_call(
        body, name="grad_exchange",
        out_shape=[jax.ShapeDtypeStruct(g.shape, g.dtype) for g in slabs]
        + [jax.ShapeDtypeStruct((N_DEV,) + spack.shape, spack.dtype)],
        in_specs=[hbm] * n, out_specs=[hbm] * n,
        scratch_shapes=[pltpu.SemaphoreType.DMA((n, 7)), pltpu.SemaphoreType.DMA((n, 7)),
                        pltpu.SemaphoreType.DMA((n,))],
    )(*slabs, spack)


_HBM = pl.BlockSpec(memory_space=pltpu.HBM)
_SEM = pl.BlockSpec(memory_space=pltpu.SEMAPHORE)
_EFFECT = pltpu.SideEffectType.DATAFLOW_SIDE_EFFECTING


def _my_id():
    x, y, c = _my_pos()
    return 4 * x + 2 * y + c


def _peer_copies(src_refs, land_refs, send_sems, recv_sems, per_peer):
    x, y, c = _my_pos()
    my_id = 4 * x + 2 * y + c
    copies = []
    for k in range(1, N_DEV):
        px, py, pc = x ^ (k >> 2), y ^ ((k >> 1) & 1), c ^ (k & 1)
        for a, (src, land) in enumerate(zip(src_refs, land_refs)):
            copies.append(pltpu.make_async_remote_copy(
                src_ref=src.at[4 * px + 2 * py + pc] if per_peer else src, dst_ref=land.at[my_id],
                send_sem=send_sems.at[a * (N_DEV - 1) + k - 1], recv_sem=recv_sems.at[a * (N_DEV - 1) + k - 1],
                device_id=(px, py, pc), device_id_type=MESH))
    return copies


def _exchange_start(srcs, per_peer, name):
    n = len(srcs)
    lands = [lax.empty(s.shape if per_peer else (N_DEV,) + s.shape, s.dtype) for s in srcs]

    def body(*refs):
        src_refs, land_refs = refs[:n], refs[n:2 * n]
        send_sems, recv_sems = refs[2 * n], refs[2 * n + 1]
        token = refs[-1]
        for cp in _peer_copies(src_refs, land_refs, send_sems, recv_sems, per_peer):
            cp.start()
        token[...] = jnp.zeros_like(token)

    outs = pl.pallas_call(
        body, name=name,
        out_shape=(pltpu.SemaphoreType.DMA((n * (N_DEV - 1),)), pltpu.SemaphoreType.DMA((n * (N_DEV - 1),)),
                   *[pltpu.HBM(a.shape, a.dtype) for a in srcs + lands],
                   jax.ShapeDtypeStruct((8, LANES), F32)),
        in_specs=[_HBM] * (2 * n),
        out_specs=(_SEM, _SEM, *[_HBM] * (2 * n), pl.BlockSpec(memory_space=pltpu.VMEM)),
        input_output_aliases={a: 2 + a for a in range(2 * n)},
        compiler_params=pltpu.CompilerParams(has_side_effects=_EFFECT),
    )(*[pltpu.with_memory_space_constraint(a, pltpu.HBM) for a in srcs + lands])
    return outs[0], outs[1], list(outs[2:2 + n]), list(outs[2 + n:2 + 2 * n]), outs[-1]


def _exchange_wait(handles, per_peer, after, name):
    send_sems, recv_sems, srcs, lands, _ = handles
    n = len(srcs)

    def body(*refs):
        src_refs, land_refs = refs[:n], refs[n:2 * n]
        for cp in _peer_copies(src_refs, land_refs, refs[2 * n], refs[2 * n + 1], per_peer):
            cp.wait_send()
            cp.wait_recv()

    outs = pl.pallas_call(
        body, name=name,
        out_shape=tuple(pltpu.HBM(a.shape, a.dtype) for a in srcs + lands),
        in_specs=[_HBM] * (2 * n) + [_SEM, _SEM, pl.BlockSpec(memory_space=pl.ANY)],
        out_specs=tuple([_HBM] * (2 * n)),
        input_output_aliases={a: a for a in range(2 * n)},
        compiler_params=pltpu.CompilerParams(has_side_effects=_EFFECT),
    )(*srcs, *lands, send_sems, recv_sems, after)
    me = _my_id()
    filled = []
    for src, land in zip(outs[:n], outs[n:]):
        own = lax.dynamic_index_in_dim(src, me, 0, keepdims=True) if per_peer else src[None]
        filled.append(lax.dynamic_update_slice_in_dim(land, own, me, 0))
    return filled


def _col_windows(n_shards, width, gap_at=None, gap=0):
    out = []
    for j in range(n_shards):
        g0, g1 = j * width, (j + 1) * width
        cuts = [g0, g1] if gap_at is None or not g0 < gap_at < g1 else [g0, gap_at, g1]
        for a, b in zip(cuts[:-1], cuts[1:]):
            out.append((j, a - g0, b - g0, a + (gap if gap_at is not None and a >= gap_at else 0)))
    return out


def _assemble_cols(parts, total, windows, name, tr=256):
    n, rows, w = parts.shape
    tr = _tile(rows, tr, 16)

    def body(p_ref, o_ref):
        o_ref[...] = jnp.zeros_like(o_ref)
        for j, lo, hi, dst in windows:
            o_ref[:, dst:dst + hi - lo] = p_ref[j, :, lo:hi]

    return pl.pallas_call(
        body, name=name, grid=(rows // tr,),
        in_specs=[pl.BlockSpec((n, tr, w), lambda i: (0, i, 0))],
        out_specs=pl.BlockSpec((tr, total), lambda i: (i, 0)),
        out_shape=jax.ShapeDtypeStruct((rows, total), parts.dtype),
        compiler_params=_cparams(("parallel",)),
    )(parts)


def _split_cols(full, n, w, windows, name, tr=256):
    rows, total = full.shape
    tr = _tile(rows, tr, 16)

    def body(f_ref, o_ref):
        for j, lo, hi, dst in windows:
            o_ref[j, :, lo:hi] = f_ref[:, dst:dst + hi - lo].astype(o_ref.dtype)

    return pl.pallas_call(
        body, name=name, grid=(rows // tr,),
        in_specs=[pl.BlockSpec((tr, total), lambda i: (i, 0))],
        out_specs=pl.BlockSpec((n, tr, w), lambda i: (0, i, 0)),
        out_shape=jax.ShapeDtypeStruct((n, rows, w), BF16),
        compiler_params=_cparams(("parallel",)),
    )(full)


_DIMS = {"nn": (((1,), (0,)), ((), ())), "nt": (((1,), (1,)), ((), ())), "tn": (((0,), (0,)), ((), ()))}


def _matmul(a, b, *, mode, grid, a_block, a_map, b_block, b_map, o_block, o_map, out_shape, name,
            resid=None):
    nk = grid[2]
    dims = _DIMS[mode]

    def body(*refs):
        if resid is None:
            a_ref, b_ref, o_ref, acc_ref = refs
            r_ref = None
        else:
            a_ref, b_ref, r_ref, o_ref, acc_ref = refs
        k = pl.program_id(2)

        @pl.when(k == 0)
        def _():
            acc_ref[...] = jnp.zeros_like(acc_ref)

        acc_ref[...] += lax.dot_general(a_ref[...], b_ref[...], dims, preferred_element_type=F32)

        @pl.when(k == nk - 1)
        def _():
            res = acc_ref[...]
            if r_ref is not None:
                res = r_ref[...] + res
            o_ref[...] = res.astype(o_ref.dtype)

    in_specs = [pl.BlockSpec(a_block, a_map), pl.BlockSpec(b_block, b_map)]
    args = [a, b]
    if resid is not None:
        in_specs.append(pl.BlockSpec(o_block, o_map))
        args.append(resid)
    acc_shape = tuple(d for d in o_block if d is not None)
    return pl.pallas_call(
        body, name=name, grid=grid, in_specs=in_specs,
        out_specs=pl.BlockSpec(o_block, o_map), out_shape=out_shape,
        scratch_shapes=[pltpu.VMEM(acc_shape, F32)],
        compiler_params=_cparams(("parallel", "parallel", "arbitrary")),
    )(*args)


def _mm_nn(a, b, out_dtype, name, resid=None, tm=1024, tn=1024, tk=1024):
    m, kk = a.shape
    n = b.shape[1]
    tm, tn, tk = _tile(m, tm, 8), _tile(n, tn), _tile(kk, tk)
    return _matmul(a, b, mode="nn", grid=(m // tm, n // tn, kk // tk),
                   a_block=(tm, tk), a_map=lambda i, j, k: (i, k),
                   b_block=(tk, tn), b_map=lambda i, j, k: (k, j),
                   o_block=(tm, tn), o_map=lambda i, j, k: (i, j),
                   out_shape=jax.ShapeDtypeStruct((m, n), out_dtype), name=name, resid=resid)


def _mm_nt(a, b, out_dtype, name, tm=1024, tn=1024, tk=1024):
    m, kk = a.shape
    n = b.shape[0]
    tm, tn, tk = _tile(m, tm, 8), _tile(n, tn), _tile(kk, tk)
    return _matmul(a, b, mode="nt", grid=(m // tm, n // tn, kk // tk),
                   a_block=(tm, tk), a_map=lambda i, j, k: (i, k),
                   b_block=(tn, tk), b_map=lambda i, j, k: (j, k),
                   o_block=(tm, tn), o_map=lambda i, j, k: (i, j),
                   out_shape=jax.ShapeDtypeStruct((m, n), out_dtype), name=name)


def _mm_tn(a, b, name, tm=1024, tn=1024, tk=1024):
    kk, m = a.shape
    n = b.shape[1]
    tm, tn, tk = _tile(m, tm), _tile(n, tn), _tile(kk, tk, 8)
    return _matmul(a, b, mode="tn", grid=(m // tm, n // tn, kk // tk),
                   a_block=(tk, tm), a_map=lambda i, j, k: (k, i),
                   b_block=(tk, tn), b_map=lambda i, j, k: (k, j),
                   o_block=(tm, tn), o_map=lambda i, j, k: (i, j),
                   out_shape=jax.ShapeDtypeStruct((m, n), F32), name=name)


def _mm_heads(a, b, name, tm=1024, tn=640):
    m, kk = a.shape
    n = b.shape[1]
    tm, tn = _tile(m, tm, 16), _tile(n, tn)
    per_tile = tn // HEAD_DIM

    def body(a_ref, b_ref, o_ref):
        res = jnp.dot(a_ref[...], b_ref[...], preferred_element_type=F32)
        for hh in range(per_tile):
            o_ref[hh] = res[:, hh * HEAD_DIM:(hh + 1) * HEAD_DIM].astype(o_ref.dtype)

    return pl.pallas_call(
        body, name=name, grid=(m // tm, n // tn),
        in_specs=[pl.BlockSpec((tm, kk), lambda i, j: (i, 0)), pl.BlockSpec((kk, tn), lambda i, j: (0, j))],
        out_specs=pl.BlockSpec((per_tile, tm, HEAD_DIM), lambda i, j: (j, i, 0)),
        out_shape=jax.ShapeDtypeStruct((n // HEAD_DIM, m, HEAD_DIM), BF16),
        compiler_params=_cparams(("parallel", "parallel")),
    )(a, b)


def _mm_up(h, w_up, tm=2048, tn=256):
    s = h.shape[0]
    tm = _tile(s, tm, 8)
    nh = D_FF // tn
    return _matmul(h, w_up, mode="nn", grid=(s // tm, 2 * nh, 1),
                   a_block=(tm, D_MODEL), a_map=lambda i, j, k: (i, 0),
                   b_block=(D_MODEL, tn), b_map=lambda i, j, k: (0, j),
                   o_block=(None, tm, tn), o_map=lambda i, j, k: (j // nh, i, j % nh),
                   out_shape=jax.ShapeDtypeStruct((2, s, D_FF), F32), name="up_proj")


def _mm_dup_nt(dup, w_up, tm=1024, tk=1408):
    s = dup.shape[1]
    tm = _tile(s, tm, 8)
    nh = D_FF // tk
    return _matmul(dup, w_up, mode="nt", grid=(s // tm, 1, 2 * nh),
                   a_block=(None, tm, tk), a_map=lambda i, j, k: (k // nh, i, k % nh),
                   b_block=(D_MODEL, tk), b_map=lambda i, j, k: (0, k),
                   o_block=(tm, D_MODEL), o_map=lambda i, j, k: (i, 0),
                   out_shape=jax.ShapeDtypeStruct((s, D_MODEL), F32), name="d_h2")


def _mm_dwup_tn(h, dup, tn=1408, tk=1024):
    s = h.shape[0]
    tk = _tile(s, tk, 8)
    nh = D_FF // tn
    return _matmul(h, dup, mode="tn", grid=(1, 2 * nh, s // tk),
                   a_block=(tk, D_MODEL), a_map=lambda i, j, k: (k, 0),
                   b_block=(None, tk, tn), b_map=lambda i, j, k: (j // nh, k, j % nh),
                   o_block=(D_MODEL, tn), o_map=lambda i, j, k: (0, j),
                   out_shape=jax.ShapeDtypeStruct((D_MODEL, 2 * D_FF), F32), name="d_w_up")


def _rms_fwd(x, g, tr=512):
    s, d = x.shape
    tr = _tile(s, tr, 8)

    def body(x_ref, g_ref, o_ref):
        xv = x_ref[...]
        r = lax.rsqrt(jnp.mean(xv * xv, axis=-1, keepdims=True) + EPS)
        o_ref[...] = (xv * r * g_ref[...]).astype(o_ref.dtype)

    return pl.pallas_call(
        body, name="rms_fwd", grid=(s // tr,),
        in_specs=[pl.BlockSpec((tr, d), lambda i: (i, 0)), pl.BlockSpec((1, d), lambda i: (0, 0))],
        out_specs=pl.BlockSpec((tr, d), lambda i: (i, 0)),
        out_shape=jax.ShapeDtypeStruct((s, d), BF16),
        compiler_params=_cparams(("parallel",)),
    )(x, g)


def _group_rms_fwd(o_fox, o_sb, g_fox, g_sb, tr=512):
    nh, s, dh = o_fox.shape
    tr = _tile(s, tr, 8)

    def body(a_ref, b_ref, ga_ref, gb_ref, o_ref):
        for src, g_ref, lo in ((a_ref, ga_ref, 0), (b_ref, gb_ref, nh * dh)):
            heads = [src[hh] for hh in range(nh)]
            ss = heads[0] * heads[0]
            for xv in heads[1:]:
                ss = ss + xv * xv
            r = lax.rsqrt(jnp.sum(ss, axis=-1, keepdims=True) * (1.0 / (nh * dh)) + EPS)
            for hh, xv in enumerate(heads):
                o_ref[:, lo + hh * dh:lo + (hh + 1) * dh] = (xv * r * g_ref[hh]).astype(o_ref.dtype)

    heads_blk = pl.BlockSpec((nh, tr, dh), lambda i: (0, i, 0))
    gain = pl.BlockSpec((nh, 1, dh), lambda i: (0, 0, 0))
    return pl.pallas_call(
        body, name="group_rms_fwd", grid=(s // tr,),
        in_specs=[heads_blk, heads_blk, gain, gain],
        out_specs=pl.BlockSpec((tr, 2 * nh * dh), lambda i: (i, 0)),
        out_shape=jax.ShapeDtypeStruct((s, 2 * nh * dh), BF16),
        compiler_params=_cparams(("parallel",)),
    )(o_fox, o_sb, g_fox, g_sb)


def _group_rms_bwd(x, dy, g, *, dy_col, name, tr=512):
    nh, s, dh = x.shape
    tr = _tile(s, tr, 8)
    d = nh * dh

    def body(x_ref, dy_ref, g_ref, dx_ref, dg_ref):
        @pl.when(pl.program_id(0) == 0)
        def _():
            dg_ref[...] = jnp.zeros_like(dg_ref)

        dyv = dy_ref[...]
        xs_ = [x_ref[hh] for hh in range(nh)]
        dys = [dyv[:, hh * dh:(hh + 1) * dh] for hh in range(nh)]
        ss = xs_[0] * xs_[0]
        for xv in xs_[1:]:
            ss = ss + xv * xv
        r = lax.rsqrt(jnp.sum(ss, axis=-1, keepdims=True) * (1.0 / d) + EPS)
        xh = [xv * r for xv in xs_]
        gy = [dys[hh] * g_ref[hh] for hh in range(nh)]
        dot = xh[0] * gy[0]
        for hh in range(1, nh):
            dot = dot + xh[hh] * gy[hh]
        mean_dot = jnp.sum(dot, axis=-1, keepdims=True) * (1.0 / d)
        for hh in range(nh):
            dx_ref[hh] = r * (gy[hh] - xh[hh] * mean_dot)
            dg_ref[hh] += jnp.sum(dys[hh] * xh[hh], axis=0, keepdims=True)

    heads_blk = pl.BlockSpec((nh, tr, dh), lambda i: (0, i, 0))
    gain = pl.BlockSpec((nh, 1, dh), lambda i: (0, 0, 0))
    return pl.pallas_call(
        body, name=name, grid=(s // tr,),
        in_specs=[heads_blk, pl.BlockSpec((tr, d), lambda i: (i, dy_col)), gain],
        out_specs=[heads_blk, gain],
        out_shape=[jax.ShapeDtypeStruct((nh, s, dh), F32), jax.ShapeDtypeStruct((nh, 1, dh), F32)],
        compiler_params=_cparams(("arbitrary",)),
    )(x, dy, g)


def _merge_dproj(parts_fox, d_gate, parts_sb, tr=256):
    nh, s, dh = parts_fox[0].shape
    tr = _tile(s, tr, 16)

    def body(*refs):
        o_ref = refs[-1]
        gate_ref = refs[3]
        col = 0
        for ref in refs[:3]:
            for hh in range(nh):
                o_ref[:, col:col + dh] = ref[hh].astype(o_ref.dtype)
                col += dh
        o_ref[:, col:col + GATE_PAD] = jnp.zeros((tr, GATE_PAD), o_ref.dtype)
        o_ref[:, col:col + N_GROUP_HEADS] = gate_ref[...].astype(o_ref.dtype)
        col += GATE_PAD
        for ref in refs[4:7]:
            for hh in range(nh):
                o_ref[:, col:col + dh] = ref[hh].astype(o_ref.dtype)
                col += dh

    heads_blk = pl.BlockSpec((nh, tr, dh), lambda i: (0, i, 0))
    return pl.pallas_call(
        body, name="merge_d_proj", grid=(s // tr,),
        in_specs=[heads_blk] * 3 + [pl.BlockSpec((tr, N_GROUP_HEADS), lambda i: (i, 0))] + [heads_blk] * 3,
        out_specs=pl.BlockSpec((tr, IN_COLS_PAD), lambda i: (i, 0)),
        out_shape=jax.ShapeDtypeStruct((s, IN_COLS_PAD), BF16),
        compiler_params=_cparams(("parallel",)),
    )(*parts_fox, d_gate, *parts_sb)


def _rms_bwd(x, dy, g, resid, *, dy_col, name, want_bf16, tr=512):
    s, d = x.shape
    tr = _tile(s, tr, 8)
    has_resid = resid is not None

    def body(*refs):
        refs = list(refs)
        x_ref, dy_ref, g_ref = refs[:3]
        r_ref = refs[3] if has_resid else None
        outs = refs[4:] if has_resid else refs[3:]
        dx_ref = outs[0]
        dxb_ref = outs[1] if want_bf16 else None
        dg_ref = outs[-1]

        @pl.when(pl.program_id(0) == 0)
        def _():
            dg_ref[...] = jnp.zeros_like(dg_ref)

        xv = x_ref[...]
        dyv = dy_ref[...]
        r = lax.rsqrt(jnp.mean(xv * xv, axis=-1, keepdims=True) + EPS)
        xh = xv * r
        gy = dyv * g_ref[...]
        dx = r * (gy - xh * jnp.mean(xh * gy, axis=-1, keepdims=True))
        if r_ref is not None:
            dx = r_ref[...] + dx
        dx_ref[...] = dx
        if dxb_ref is not None:
            dxb_ref[...] = dx.astype(BF16)
        dg_ref[...] += jnp.sum(dyv * xh, axis=0, keepdims=True)

    row = pl.BlockSpec((tr, d), lambda i: (i, 0))
    in_specs = [row, pl.BlockSpec((tr, d), lambda i: (i, dy_col)), pl.BlockSpec((1, d), lambda i: (0, 0))]
    args = [x, dy, g]
    if has_resid:
        in_specs.append(row)
        args.append(resid)
    out_specs = [row]
    out_shape = [jax.ShapeDtypeStruct((s, d), F32)]
    if want_bf16:
        out_specs.append(row)
        out_shape.append(jax.ShapeDtypeStruct((s, d), BF16))
    out_specs.append(pl.BlockSpec((1, d), lambda i: (0, 0)))
    out_shape.append(jax.ShapeDtypeStruct((1, d), F32))
    return pl.pallas_call(
        body, name=name, grid=(s // tr,), in_specs=in_specs, out_specs=out_specs, out_shape=out_shape,
        compiler_params=_cparams(("arbitrary",)),
    )(*args)


def _loss_head(x2, target, g, tr=512):
    s, d = x2.shape
    tr = _tile(s, tr, 8)

    def body(x_ref, t_ref, g_ref, dx_ref, dxb_ref, dg_ref, loss_ref):
        @pl.when(pl.program_id(0) == 0)
        def _():
            dg_ref[...] = jnp.zeros_like(dg_ref)
            loss_ref[...] = jnp.zeros_like(loss_ref)

        xv = x_ref[...]
        gv = g_ref[...]
        r = lax.rsqrt(jnp.mean(xv * xv, axis=-1, keepdims=True) + EPS)
        xh = xv * r
        err = xh * gv - t_ref[...]
        loss_ref[...] += jnp.sum(jnp.mean(err * err, axis=-1, keepdims=True), axis=0, keepdims=True) * 0.5
        dyv = err * (1.0 / d)
        gy = dyv * gv
        dx = r * (gy - xh * jnp.mean(xh * gy, axis=-1, keepdims=True))
        dx_ref[...] = dx
        dxb_ref[...] = dx.astype(BF16)
        dg_ref[...] += jnp.sum(dyv * xh, axis=0, keepdims=True)

    row = pl.BlockSpec((tr, d), lambda i: (i, 0))
    return pl.pallas_call(
        body, name="loss_head", grid=(s // tr,),
        in_specs=[row, row, pl.BlockSpec((1, d), lambda i: (0, 0))],
        out_specs=[row, row, pl.BlockSpec((1, d), lambda i: (0, 0)), pl.BlockSpec((1, LANES), lambda i: (0, 0))],
        out_shape=[jax.ShapeDtypeStruct((s, d), F32), jax.ShapeDtypeStruct((s, d), BF16),
                   jax.ShapeDtypeStruct((1, d), F32), jax.ShapeDtypeStruct((1, LANES), F32)],
        compiler_params=_cparams(("arbitrary",)),
    )(x2, target, g)


def _conv_taps(cur, prev8, w, b, first):
    prev8 = jnp.where(first, 0.0, prev8)
    ext = jnp.concatenate([prev8, cur], axis=0)
    x1 = pltpu.roll(ext, 1, 0)[8:]
    x2 = pltpu.roll(ext, 2, 0)[8:]
    u = b + w[0:1] * x2
    u = u + w[1:2] * x1
    u = u + w[2:3] * cur
    return u, x1, x2


def _conv_gate_fwd(up, conv_w, conv_b, tm=1024, tn=256):
    s = up.shape[1]
    tm = _tile(s, tm, 8)
    nrb = s // tm
    rb8 = tm // 8

    def body(g_ref, v_ref, gp_ref, vp_ref, wg_ref, wv_ref, bg_ref, bv_ref, o_ref):
        first = pl.program_id(1) == 0
        ug, _, _ = _conv_taps(g_ref[...], gp_ref[...], wg_ref[...], bg_ref[...], first)
        uv, _, _ = _conv_taps(v_ref[...], vp_ref[...], wv_ref[...], bv_ref[...], first)
        sg = 1.0 / (1.0 + jnp.exp(-ug))
        o_ref[...] = (ug * sg * uv).astype(o_ref.dtype)

    def cur(h):
        return pl.BlockSpec((None, tm, tn), lambda j, i: (h, i, j))

    def prev(h):
        return pl.BlockSpec((None, 8, tn), lambda j, i: (h, jnp.maximum(i * rb8 - 1, 0), j))

    def par(h, r):
        return pl.BlockSpec((None, r, tn), lambda j, i: (h, 0, j))

    return pl.pallas_call(
        body, name="conv_gate_fwd", grid=(D_FF // tn, nrb),
        in_specs=[cur(0), cur(1), prev(0), prev(1), par(0, 3), par(1, 3), par(0, 1), par(1, 1)],
        out_specs=pl.BlockSpec((tm, tn), lambda j, i: (i, j)),
        out_shape=jax.ShapeDtypeStruct((s, D_FF), BF16),
        compiler_params=_cparams(("parallel", "parallel")),
    )(up, up, up, up, conv_w, conv_w, conv_b, conv_b)


def _conv_gate_bwd(up, dact, conv_w, conv_b, tm=512, tn=256):
    s = up.shape[1]
    tm = _tile(s, tm, 8)
    nrb = s // tm
    rb8 = tm // 8

    def body(g_ref, v_ref, gp_ref, vp_ref, da_ref, wg_ref, wv_ref, bg_ref, bv_ref,
             dup_ref, dcw_ref, dcb_ref, carry_ref):
        i = pl.program_id(1)
        first = i == nrb - 1

        @pl.when(i == 0)
        def _():
            carry_ref[...] = jnp.zeros_like(carry_ref)
            dcw_ref[...] = jnp.zeros_like(dcw_ref)
            dcb_ref[...] = jnp.zeros_like(dcb_ref)

        curs = (g_ref[...], v_ref[...])
        ws = (wg_ref[...], wv_ref[...])
        ug, g1, g2 = _conv_taps(curs[0], gp_ref[...], ws[0], bg_ref[...], first)
        uv, v1, v2 = _conv_taps(curs[1], vp_ref[...], ws[1], bv_ref[...], first)
        sg = 1.0 / (1.0 + jnp.exp(-ug))
        da = da_ref[...].astype(F32)
        d_v = da * (ug * sg)
        d_g = da * uv * (sg * (1.0 + ug * (1.0 - sg)))
        for h, (du, x0, x1, x2) in enumerate(((d_g, curs[0], g1, g2), (d_v, curs[1], v1, v2))):
            dcb_ref[h] += jnp.sum(du, axis=0, keepdims=True)
            dcw_ref[h, 0:1, :] += jnp.sum(du * x2, axis=0, keepdims=True)
            dcw_ref[h, 1:2, :] += jnp.sum(du * x1, axis=0, keepdims=True)
            dcw_ref[h, 2:3, :] += jnp.sum(du * x0, axis=0, keepdims=True)
            ext = jnp.concatenate([du, carry_ref[h]], axis=0)
            n1 = pltpu.roll(ext, tm + 7, 0)[:tm]
            n2 = pltpu.roll(ext, tm + 6, 0)[:tm]
            w = ws[h]
            dup_ref[h] = (w[2:3] * du + w[1:2] * n1 + w[0:1] * n2).astype(dup_ref.dtype)
            carry_ref[h] = du[:8]

    def cur(h):
        return pl.BlockSpec((None, tm, tn), lambda j, i: (h, nrb - 1 - i, j))

    def prev(h):
        return pl.BlockSpec((None, 8, tn), lambda j, i: (h, jnp.maximum((nrb - 1 - i) * rb8 - 1, 0), j))

    def par(h, r):
        return pl.BlockSpec((None, r, tn), lambda j, i: (h, 0, j))

    return pl.pallas_call(
        body, name="conv_gate_bwd", grid=(D_FF // tn, nrb),
        in_specs=[cur(0), cur(1), prev(0), prev(1),
                  pl.BlockSpec((tm, tn), lambda j, i: (nrb - 1 - i, j)),
                  par(0, 3), par(1, 3), par(0, 1), par(1, 1)],
        out_specs=[pl.BlockSpec((2, tm, tn), lambda j, i: (0, nrb - 1 - i, j)),
                   pl.BlockSpec((2, 3, tn), lambda j, i: (0, 0, j)),
                   pl.BlockSpec((2, 1, tn), lambda j, i: (0, 0, j))],
        out_shape=[jax.ShapeDtypeStruct((2, s, D_FF), BF16),
                   jax.ShapeDtypeStruct((2, 3, D_FF), F32),
                   jax.ShapeDtypeStruct((2, 1, D_FF), F32)],
        scratch_shapes=[pltpu.VMEM((2, 8, tn), F32)],
        compiler_params=_cparams(("parallel", "arbitrary")),
    )(up, up, up, up, dact, conv_w, conv_w, conv_b, conv_b)


def _split_dot(x, tri, terms):
    piece = x.astype(BF16)
    out = jnp.dot(piece, tri, preferred_element_type=F32)
    rest = x
    for _ in range(terms - 1):
        rest = rest - piece.astype(F32)
        piece = rest.astype(BF16)
        out = out + jnp.dot(piece, tri, preferred_element_type=F32)
    return out


def _split_dot_rhs(tri, x, terms):
    piece = x.astype(BF16)
    out = jnp.dot(tri, piece, preferred_element_type=F32)
    rest = x
    for _ in range(terms - 1):
        rest = rest - piece.astype(F32)
        piece = rest.astype(BF16)
        out = out + jnp.dot(tri, piece, preferred_element_type=F32)
    return out


def _tri(n, kind):
    r = lax.broadcasted_iota(jnp.int32, (n, n), 0)
    c = lax.broadcasted_iota(jnp.int32, (n, n), 1)
    cond = {"le": r <= c, "ge": r >= c, "lt": r < c, "gt": r > c}[kind]
    return jnp.where(cond, 1.0, 0.0).astype(BF16)


def _log_sigmoid(x):
    return jnp.minimum(x, 0.0) - jnp.log(1.0 + jnp.exp(-jnp.abs(x)))


def _forget_fwd(f_logit, bias):
    h, r, _ = f_logit.shape

    def body(x_ref, b_ref, o_ref):
        lf = _log_sigmoid(x_ref[...] + b_ref[...])
        within = _split_dot(lf, _tri(LANES, "le"), 3)
        row_tot = jnp.broadcast_to(within[:, LANES - 1:LANES], (r, LANES))
        before = _split_dot_rhs(_tri(r, "gt"), row_tot, 3)
        o_ref[...] = within + before

    blk = pl.BlockSpec((None, r, LANES), lambda i: (i, 0, 0))
    return pl.pallas_call(
        body, name="forget_cumsum_fwd", grid=(h,),
        in_specs=[blk, pl.BlockSpec((None, 1, LANES), lambda i: (i, 0, 0))],
        out_specs=blk, out_shape=jax.ShapeDtypeStruct((h, r, LANES), F32),
        compiler_params=_cparams(("parallel",)),
    )(f_logit, bias)


def _forget_bwd(f_logit, bias, ksum, qsum):
    h, r, _ = f_logit.shape

    def body(x_ref, b_ref, k_ref, q_ref, dx_ref, db_ref):
        d_f = q_ref[...] - k_ref[...]
        within = _split_dot(d_f, _tri(LANES, "ge"), 3)
        row_tot = jnp.broadcast_to(within[:, 0:1], (r, LANES))
        after = _split_dot_rhs(_tri(r, "lt"), row_tot, 3)
        xv = x_ref[...] + b_ref[...]
        dx = (within + after) * jnp.exp(_log_sigmoid(-xv))
        dx_ref[...] = dx
        db_ref[...] = jnp.broadcast_to(jnp.sum(dx), (1, LANES))

    blk = pl.BlockSpec((None, r, LANES), lambda i: (i, 0, 0))
    one = pl.BlockSpec((None, 1, LANES), lambda i: (i, 0, 0))
    return pl.pallas_call(
        body, name="forget_cumsum_bwd", grid=(h,),
        in_specs=[blk, one, blk, blk], out_specs=[blk, one],
        out_shape=[jax.ShapeDtypeStruct((h, r, LANES), F32), jax.ShapeDtypeStruct((h, 1, LANES), F32)],
        compiler_params=_cparams(("parallel",)),
    )(f_logit, bias, ksum, qsum)


def _head_specs(s, tq):
    qblk = pl.BlockSpec((None, tq, HEAD_DIM), lambda h, i: (h, i, 0))
    full = pl.BlockSpec((None, s, HEAD_DIM), lambda h, i: (h, 0, 0))
    col = pl.BlockSpec((None, tq, 1), lambda h, i: (h, i, 0))
    return qblk, full, col


def _qkv_specs(s, tq, offs):
    q_off, k_off, v_off = offs
    return (pl.BlockSpec((None, tq, HEAD_DIM), lambda h, i: (h + q_off, i, 0)),
            pl.BlockSpec((None, s, HEAD_DIM), lambda h, i: (h + k_off, 0, 0)),
            pl.BlockSpec((None, s, HEAD_DIM), lambda h, i: (h + v_off, 0, 0)))


def _scaled(q_ref):
    return (q_ref[...].astype(F32) * Q_SCALE).astype(BF16)


_NT = (((1,), (1,)), ((), ()))
_TN = (((0,), (0,)), ((), ()))


def _cols_minus_rows(rows, cols):
    return lax.broadcasted_iota(jnp.int32, (rows, cols), 1) - lax.broadcasted_iota(jnp.int32, (rows, cols), 0)


def _fox_fwd(qkv, offs, v_ones, f_row, tq, tk):
    h, s = N_GROUP_HEADS, qkv.shape[1]
    nk = s // tk
    assert tq == tk

    def body(q_ref, k_ref, v_ref, fr_ref, o_ref, lse_ref, m_ref, acc_ref, z0, z1):
        i = pl.program_id(1)
        qs = _scaled(q_ref)
        m_ref[...] = jnp.full_like(m_ref, NEG_BIG)
        acc_ref[...] = jnp.zeros_like(acc_ref)

        ahead = _cols_minus_rows(tq, tk)

        def block_of(j):
            return jnp.minimum(j, nk - 1)

        def keys_of(j):
            return pl.ds(pl.multiple_of(block_of(j) * tk, tk), tk)

        def logits(j):
            return lax.dot_general(qs, k_ref[keys_of(j), :], _NT, preferred_element_type=F32)

        def soft(j, raw, masked):
            sc = raw - fr_ref[block_of(j)]
            if masked:
                sc = jnp.where(ahead <= (i - j) * tk, sc, NEG_BIG)
            m_old = m_ref[...]
            m_new = jnp.maximum(m_old, jnp.max(sc, axis=-1, keepdims=True))
            p = jnp.exp(sc - m_new)
            acc_ref[...] = jnp.exp(m_old - m_new) * acc_ref[...] + jnp.dot(
                p.astype(BF16), v_ref[keys_of(j), :], preferred_element_type=F32)
            m_ref[...] = m_new

        z0[...] = logits(0)

        def trip(p, masked):
            j = 2 * p
            z1[...] = logits(j + 1)
            soft(j, z0[...], masked)
            z0[...] = logits(j + 2)
            soft(j + 1, z1[...], masked)

        def step(p, carry):
            trip(p, False)
            return carry

        lax.fori_loop(0, i // 2, step, 0)
        trip(i // 2, True)
        l = acc_ref[:, HEAD_DIM:HEAD_DIM + 1]
        o_ref[...] = acc_ref[:, :HEAD_DIM] / l
        lse_ref[...] = m_ref[...] + jnp.log(l)

    qblk, full, colspec = _head_specs(s, tq)
    q_in, k_in, _ = _qkv_specs(s, tq, offs)
    return pl.pallas_call(
        body, name="fox_fwd", grid=(h, s // tq),
        in_specs=[q_in, k_in, pl.BlockSpec((None, s, 2 * HEAD_DIM), lambda hh, i: (hh, 0, 0)),
                  pl.BlockSpec((None, nk, 1, tk), lambda hh, i: (hh, 0, 0, 0))],
        out_specs=[qblk, colspec],
        out_shape=[jax.ShapeDtypeStruct((h, s, HEAD_DIM), F32), jax.ShapeDtypeStruct((h, s, 1), F32)],
        scratch_shapes=[pltpu.VMEM((tq, 1), F32), pltpu.VMEM((tq, 2 * HEAD_DIM), F32),
                        pltpu.VMEM((tq, tk), F32), pltpu.VMEM((tq, tk), F32)],
        compiler_params=_cparams(("parallel", "parallel")),
    )(qkv, qkv, v_ones, f_row)


def _fox_bwd(qkv, offs, f_row, o, lse, d_o, tq, tk):
    h, s = N_GROUP_HEADS, qkv.shape[1]
    nk = s // tk
    assert tq == tk

    def body(q_ref, k_ref, v_ref, fr_ref, o_ref, lse_ref, do_ref,
             dq_ref, dk_ref, dv_ref, ks_ref, qs_ref, dq_acc, qsum_acc, z0, z1, p0, p1):
        i = pl.program_id(1)

        @pl.when(i == 0)
        def _():
            dk_ref[...] = jnp.zeros_like(dk_ref)
            dv_ref[...] = jnp.zeros_like(dv_ref)
            ks_ref[...] = jnp.zeros_like(ks_ref)

        qs = _scaled(q_ref)
        lse_v = lse_ref[...]
        dob = do_ref[...].astype(BF16)
        delta = jnp.sum(dob.astype(F32) * o_ref[...], axis=-1, keepdims=True)
        dq_acc[...] = jnp.zeros_like(dq_acc)
        qsum_acc[...] = jnp.zeros_like(qsum_acc)

        ahead = _cols_minus_rows(tq, tk)

        def block_of(j):
            return jnp.minimum(j, nk - 1)

        def keys_of(j):
            return pl.ds(pl.multiple_of(block_of(j) * tk, tk), tk)

        def products(j):
            at = keys_of(j)
            return (lax.dot_general(qs, k_ref[at, :], _NT, preferred_element_type=F32),
                    lax.dot_general(dob, v_ref[at, :], _NT, preferred_element_type=F32))

        def grads(j, raw, dp, masked):
            at = keys_of(j)
            sc = raw - fr_ref[block_of(j)]
            if masked:
                sc = jnp.where(ahead <= (i - j) * tk, sc, NEG_BIG)
            p = jnp.exp(sc - lse_v)
            ds = p * (dp - delta)
            dsb = ds.astype(BF16)
            dq_acc[...] += jnp.dot(dsb, k_ref[at, :], preferred_element_type=F32)
            dk_ref[at, :] += lax.dot_general(dsb, qs, _TN, preferred_element_type=F32)
            dv_ref[at, :] += lax.dot_general(p.astype(BF16), dob, _TN, preferred_element_type=F32)
            ks_ref[block_of(j)] += jnp.sum(ds.reshape(tq // 8, 8, tk), axis=0)
            qsum_acc[...] += jnp.sum(ds, axis=-1, keepdims=True)

        z0[...], p0[...] = products(0)

        def trip(pp, masked):
            j = 2 * pp
            z1[...], p1[...] = products(j + 1)
            grads(j, z0[...], p0[...], masked)
            z0[...], p0[...] = products(j + 2)
            grads(j + 1, z1[...], p1[...], masked)

        def step(pp, carry):
            trip(pp, False)
            return carry

        lax.fori_loop(0, i // 2, step, 0)
        trip(i // 2, True)
        dq_ref[...] = dq_acc[...] * Q_SCALE
        qs_ref[...] = qsum_acc[...]

    qblk, full, colspec = _head_specs(s, tq)
    frow = pl.BlockSpec((None, nk, 1, tk), lambda hh, i: (hh, 0, 0, 0))
    big = pltpu.VMEM((tq, tk), F32)
    return pl.pallas_call(
        body, name="fox_bwd", grid=(h, s // tq),
        in_specs=[*_qkv_specs(s, tq, offs), frow, qblk, colspec, qblk],
        out_specs=[qblk, full, full, pl.BlockSpec((None, nk, 8, tk), lambda hh, i: (hh, 0, 0, 0)), colspec],
        out_shape=[jax.ShapeDtypeStruct((h, s, HEAD_DIM), F32)] * 3
        + [jax.ShapeDtypeStruct((h, nk, 8, tk), F32), jax.ShapeDtypeStruct((h, s, 1), F32)],
        scratch_shapes=[pltpu.VMEM((tq, HEAD_DIM), F32), pltpu.VMEM((tq, 1), F32), big, big, big, big],
        compiler_params=_cparams(("parallel", "arbitrary")),
    )(qkv, qkv, qkv, f_row, o, lse, d_o)


SB_TERMS = 2
G_TERMS = 1
LOG2E = 1.4426950408889634
LN2 = 0.6931471805599453


def _softplus2(z2):
    return jnp.maximum(z2, 0.0) + jnp.log2(1.0 + jnp.exp2(-jnp.abs(z2)))


def _sb_fwd(qkv, offs, tq, tk):
    h, s = N_GROUP_HEADS, qkv.shape[1]

    assert tq % (2 * tk) == 0

    def body(q_ref, k_ref, v_ref, o_ref, w_hbm, acc_ref, run_ref, z0, z1, d0, d1, t0, t1, w_stage, wsem):
        z_refs, d_refs, t_refs = (z0, z1), (d0, d1), (t0, t1)
        hh = pl.program_id(0)
        i = pl.program_id(1)
        qs = _scaled(q_ref)
        tri = _tri(tk, "ge")
        acc_ref[...] = jnp.zeros_like(acc_ref)
        run_ref[...] = jnp.zeros_like(run_ref)
        nb = (i + 1) * (tq // tk)
        ahead = _cols_minus_rows(tq, tk)

        def keys_of(b):
            j = nb - 1 - jnp.minimum(b, nb - 1)
            return pl.ds(pl.multiple_of(j * tk, tk), tk)

        def visible(b):
            return ahead < i * tq - (nb - 1 - b) * tk

        def logits(b, slot):
            z_refs[slot][...] = lax.dot_general(qs, k_ref[keys_of(b), :], _NT,
                                                preferred_element_type=F32) * LOG2E

        def sums(b, slot, masked):
            z2 = z_refs[slot][...]
            sp = _softplus2(z2)
            if masked:
                sp = jnp.where(visible(b), sp, 0.0)
            inc = _split_dot(sp, tri, SB_TERMS)
            d_refs[slot][...] = z2 - inc
            t_refs[slot][...] = inc[:, 0:1]

        def put(p, slot):
            st = (p % 2) * 2 + slot
            return pltpu.make_async_copy(w_stage.at[st], w_hbm.at[hh, i, nb - 1 - (2 * p + slot)], wsem.at[st])

        def weigh(p, slot, masked):
            b = 2 * p + slot
            w = jnp.exp2(d_refs[slot][...] - run_ref[...])
            if masked:
                w = jnp.where(visible(b), w, 0.0)
            wb = w.astype(BF16)
            w_stage[(p % 2) * 2 + slot] = wb
            acc_ref[...] += jnp.dot(wb, v_ref[keys_of(b), :], preferred_element_type=F32)
            run_ref[...] += t_refs[slot][...]

        def trip(p, masked):
            @pl.when(p >= 2)
            def _():
                put(p - 2, 0).wait()
                put(p - 2, 1).wait()

            b = 2 * p
            logits(b + 2, 0)
            sums(b + 1, 1, masked)
            weigh(p, 0, masked)
            logits(b + 3, 1)
            sums(b + 2, 0, masked)
            weigh(p, 1, masked)
            put(p, 0).start()
            put(p, 1).start()

        logits(0, 0)
        logits(1, 1)
        sums(0, 0, True)

        def guarded(p, carry):
            trip(p, True)
            return carry

        def plain(p, carry):
            trip(p, False)
            return carry

        lax.fori_loop(0, tq // tk // 2, guarded, 0)
        lax.fori_loop(tq // tk // 2, nb // 2, plain, 0)
        trips = nb // 2

        @pl.when(trips >= 2)
        def _():
            put(trips - 2, 0).wait()
            put(trips - 2, 1).wait()

        put(trips - 1, 0).wait()
        put(trips - 1, 1).wait()
        o_ref[...] = acc_ref[...]

    qblk, full, colspec = _head_specs(s, tq)
    return pl.pallas_call(
        body, name="sb_fwd", grid=(h, s // tq),
        in_specs=[*_qkv_specs(s, tq, offs)], out_specs=[qblk, pl.BlockSpec(memory_space=pl.ANY)],
        out_shape=[jax.ShapeDtypeStruct((h, s, HEAD_DIM), F32),
                   jax.ShapeDtypeStruct((h, s // tq, s // tk, tq, tk), BF16)],
        scratch_shapes=[pltpu.VMEM((tq, HEAD_DIM), F32), pltpu.VMEM((tq, 1), F32),
                        pltpu.VMEM((tq, tk), F32), pltpu.VMEM((tq, tk), F32),
                        pltpu.VMEM((tq, tk), F32), pltpu.VMEM((tq, tk), F32),
                        pltpu.VMEM((tq, 1), F32), pltpu.VMEM((tq, 1), F32),
                        pltpu.VMEM((4, tq, tk), BF16), pltpu.SemaphoreType.DMA((4,))],
        compiler_params=_cparams(("parallel", "parallel")),
    )(qkv, qkv, qkv)


def _sb_bwd(qkv, offs, w_saved, d_o, tq, tk):
    h, s = N_GROUP_HEADS, qkv.shape[1]

    assert tq % (2 * tk) == 0

    def body(q_ref, k_ref, v_ref, do_ref, w_hbm, dq_ref, dk_ref, dv_ref, dq_acc, grun_ref,
             z0, z1, p0, p1, w_bufs, wsem):
        z_refs, p_refs = (z0, z1), (p0, p1)
        hh = pl.program_id(0)
        i = pl.program_id(1)

        @pl.when(i == 0)
        def _():
            dk_ref[...] = jnp.zeros_like(dk_ref)
            dv_ref[...] = jnp.zeros_like(dv_ref)

        qs = _scaled(q_ref)
        dob = do_ref[...].astype(BF16)
        tri = _tri(tk, "le")
        dq_acc[...] = jnp.zeros_like(dq_acc)
        grun_ref[...] = jnp.zeros_like(grun_ref)
        nb = (i + 1) * (tq // tk)
        ahead = _cols_minus_rows(tq, tk)

        def block_of(b):
            return jnp.minimum(b, nb - 1)

        def keys_of(b):
            return pl.ds(pl.multiple_of(block_of(b) * tk, tk), tk)

        def visible(b):
            return ahead < i * tq - b * tk

        def fetch(p, slot):
            st = (p % 2) * 2 + slot
            return pltpu.make_async_copy(w_hbm.at[hh, i, block_of(2 * p + slot)], w_bufs.at[st], wsem.at[st])

        def products(b, slot):
            at = keys_of(b)
            z_refs[slot][...] = lax.dot_general(qs, k_ref[at, :], _NT, preferred_element_type=F32) * LOG2E
            p_refs[slot][...] = lax.dot_general(dob, v_ref[at, :], _NT, preferred_element_type=F32)

        def grads(p, slot, masked):
            b = 2 * p + slot
            at = keys_of(b)
            wb = w_bufs[(p % 2) * 2 + slot]
            g = wb.astype(F32) * p_refs[slot][...]
            ginc = _split_dot(g, tri, G_TERMS)
            beta = 1.0 / (1.0 + jnp.exp2(-z_refs[slot][...]))
            dz = g - beta * (grun_ref[...] + ginc)
            if masked:
                dz = jnp.where(visible(b), dz, 0.0)
            dzb = dz.astype(BF16)
            dq_acc[...] += jnp.dot(dzb, k_ref[at, :], preferred_element_type=F32)
            dk_ref[at, :] += lax.dot_general(dzb, qs, _TN, preferred_element_type=F32)
            dv_ref[at, :] += lax.dot_general(wb, dob, _TN, preferred_element_type=F32)
            grun_ref[...] += ginc[:, tk - 1:tk]

        def trip(p, masked):
            for slot in (0, 1):
                fetch(p + 1, slot).start()
            for slot in (0, 1):
                fetch(p, slot).wait()
            for slot in (0, 1):
                products(2 * p + slot + 1, 1 - slot)
                grads(p, slot, masked)

        for slot in (0, 1):
            fetch(0, slot).start()
        products(0, 0)
        n_plain = (nb - tq // tk) // 2

        def plain(p, carry):
            trip(p, False)
            return carry

        def guarded(p, carry):
            trip(p, True)
            return carry

        lax.fori_loop(0, n_plain, plain, 0)
        lax.fori_loop(n_plain, nb // 2, guarded, 0)
        for slot in (0, 1):
            fetch(nb // 2, slot).wait()
        dq_ref[...] = dq_acc[...] * Q_SCALE

    qblk, full, colspec = _head_specs(s, tq)
    big = pltpu.VMEM((tq, tk), F32)
    return pl.pallas_call(
        body, name="sb_bwd", grid=(h, s // tq),
        in_specs=[*_qkv_specs(s, tq, offs), qblk, pl.BlockSpec(memory_space=pl.ANY)], out_specs=[qblk, full, full],
        out_shape=[jax.ShapeDtypeStruct((h, s, HEAD_DIM), F32)] * 3,
        scratch_shapes=[pltpu.VMEM((tq, HEAD_DIM), F32), pltpu.VMEM((tq, 1), F32)]
        + [big] * 4 + [pltpu.VMEM((4, tq, tk), BF16), pltpu.SemaphoreType.DMA((4,))],
        compiler_params=_cparams(("parallel", "arbitrary")),
    )(qkv, qkv, qkv, d_o, w_saved)


def _sum_adamw(parts, w, m, v, name, tr=256):
    _, rows, lanes = parts.shape
    tr = _tile(rows, tr, 16)
    c_m = 1.0 - ADAM_B1 ** ADAM_STEP
    c_v = 1.0 - ADAM_B2 ** ADAM_STEP

    def body(p_ref, w_ref, m_ref, v_ref, g_ref, d_ref, nm_ref, nv_ref):
        g = p_ref[0].astype(F32)
        for j in range(1, N_DEV):
            g = g + p_ref[j].astype(F32)
        nm = ADAM_B1 * m_ref[...] + (1.0 - ADAM_B1) * g
        nv = ADAM_B2 * v_ref[...] + (1.0 - ADAM_B2) * (g * g)
        m_hat = nm / c_m
        v_hat = nv / c_v
        g_ref[...] = g
        d_ref[...] = -ADAM_LR * (m_hat / (jnp.sqrt(v_hat) + ADAM_EPS) + ADAM_WD * w_ref[...])
        nm_ref[...] = nm
        nv_ref[...] = nv

    blk = pl.BlockSpec((tr, lanes), lambda i: (i, 0))
    return pl.pallas_call(
        body, name=name, grid=(rows // tr,),
        in_specs=[pl.BlockSpec((N_DEV, tr, lanes), lambda i: (0, i, 0)), blk, blk, blk],
        out_specs=[blk] * 4, out_shape=[jax.ShapeDtypeStruct((rows, lanes), F32)] * 4,
        compiler_params=_cparams(("parallel",)),
    )(parts, w, m, v)


def kernel(x, attn_norm_g, w_in, forget_bias, fox_out_g, sb_out_g, w_out, ffn_norm_g, w_up, conv_w, conv_b, w_down, final_norm_g, loss_target, m_attn_norm_g, m_w_in, m_forget_bias, m_fox_out_g, m_sb_out_g, m_w_out, m_ffn_norm_g, m_w_up, m_conv_w, m_conv_b, m_w_down, m_final_norm_g, v_attn_norm_g, v_w_in, v_forget_bias, v_fox_out_g, v_sb_out_g, v_w_out, v_ffn_norm_g, v_w_up, v_conv_w, v_conv_b, v_w_down, v_final_norm_g):
    s = x.shape[1]
    xs = x[0]
    tq = min(ATTN_TQ, s)
    tk_fox = min(FOX_TK, s)
    tk_sb = min(SB_TK, s)
    in_shard, up_shard, out_shard, down_shard = IN_COLS // N_DEV, 2 * D_FF // N_DEV, D_MODEL // N_DEV, D_FF // N_DEV

    cw = conv_w[0]
    cw_hi = cw.astype(BF16)
    cw_lo = (cw - cw_hi.astype(F32)).astype(BF16)
    (g_in,) = _all_gather([w_in[0].astype(BF16)])
    rest = _exchange_start([w_out[0].astype(BF16), w_up[0].astype(BF16), w_down[0].astype(BF16),
                            jnp.stack([cw_hi, cw_lo])], False, "weights_rest_start")
    n_gate = QKV_W + N_GROUP_HEADS
    in_windows = _col_windows(N_DEV, in_shard, gap_at=n_gate, gap=GATE_PAD - N_GROUP_HEADS)
    up_windows = _col_windows(N_DEV, up_shard)
    w_in_p = _assemble_cols(g_in, IN_COLS_PAD, in_windows, "assemble_w_in")
    conv_b2 = conv_b.reshape(2, 1, D_FF)

    h1 = _rms_fwd(xs, attn_norm_g + rest[-1][0:1, 0:1])
    proj_h = _mm_heads(h1, w_in_p, "in_proj")
    fox_offs = (0, N_GROUP_HEADS, 2 * N_GROUP_HEADS)
    sb_first = 3 * N_GROUP_HEADS + GATE_PAD // HEAD_DIM
    sb_offs = (sb_first, sb_first + N_GROUP_HEADS, sb_first + 2 * N_GROUP_HEADS)
    f_logit = _mm_nn(h1, w_in_p[:, QKV_W:QKV_W + GATE_PAD], F32, "gate_proj")[:, :N_GROUP_HEADS]
    fv = proj_h[2 * N_GROUP_HEADS:3 * N_GROUP_HEADS]

    f_logit_h = f_logit.T.reshape(N_GROUP_HEADS, s // LANES, LANES)
    bias_h = jnp.broadcast_to(forget_bias.reshape(N_GROUP_HEADS, 1, 1), (N_GROUP_HEADS, 1, LANES))
    big_f = _forget_fwd(f_logit_h, bias_h)
    f_row = big_f.reshape(N_GROUP_HEADS, s // tk_fox, 1, tk_fox)

    fv_ones = jnp.concatenate([fv, jnp.ones_like(fv)], axis=-1)
    o_fox_h, lse = _fox_fwd(proj_h, fox_offs, fv_ones, f_row, tq, tk_fox)
    o_sb_h, sb_w = _sb_fwd(proj_h, sb_offs, min(SB_TQ, s), tk_sb)
    g_fox_h = fox_out_g.reshape(N_GROUP_HEADS, 1, HEAD_DIM)
    g_sb_h = sb_out_g.reshape(N_GROUP_HEADS, 1, HEAD_DIM)
    o_n = _group_rms_fwd(o_fox_h, o_sb_h, g_fox_h, g_sb_h)
    g_out, g_up, g_down, g_conv = _exchange_wait(rest, False, o_n, "weights_rest_wait")
    w_out_f = g_out.reshape(D_MODEL, D_MODEL)
    w_up_f = _assemble_cols(g_up, 2 * D_FF, up_windows, "assemble_w_up")
    w_down_f = g_down.reshape(D_FF, D_MODEL)
    conv_w_f = (g_conv[:, 0].astype(F32) + g_conv[:, 1].astype(F32)).transpose(1, 0, 2).reshape(3, 2 * D_FF)
    conv_w2 = conv_w_f.reshape(3, 2, D_FF).transpose(1, 0, 2)
    x1 = _mm_nn(o_n, w_out_f, F32, "out_proj", resid=xs)
    h2 = _rms_fwd(x1, ffn_norm_g)
    up = _mm_up(h2, w_up_f)
    act = _conv_gate_fwd(up, conv_w2, conv_b2)
    x2 = _mm_nn(act, w_down_f, F32, "down_proj", resid=x1, tk=1408)

    d_x2, d_x2b, dg_final, loss_part = _loss_head(x2, loss_target[0], final_norm_g.reshape(1, D_MODEL))
    d_act = _mm_nt(d_x2b, w_down_f, BF16, "d_act", tn=1408)
    dw_down = _mm_tn(act, d_x2b, "d_w_down", tm=1408)
    d_up, dcw2, dcb2 = _conv_gate_bwd(up, d_act, conv_w2, conv_b2)
    d_h2 = _mm_dup_nt(d_up, w_up_f)
    dw_up = _mm_dwup_tn(h2, d_up)
    d_x1, d_x1b, dg_ffn = _rms_bwd(x1, d_h2, ffn_norm_g, d_x2, dy_col=0, name="ffn_norm_bwd", want_bf16=True)
    d_on = _mm_nt(d_x1b, w_out_f, F32, "d_o_normed")
    dw_out = _mm_tn(o_n, d_x1b, "d_w_out")
    early = _exchange_start(
        [dw_out.astype(BF16).reshape(N_DEV, out_shard, D_MODEL),
         _split_cols(dw_up, N_DEV, up_shard, up_windows, "split_d_w_up"),
         dw_down.astype(BF16).reshape(N_DEV, down_shard, D_MODEL)], True, "grads_early_start")
    g_fox_t = g_fox_h + early[-1][0:1, 0:1]
    d_o_fox_h, dg_fox = _group_rms_bwd(o_fox_h, d_on, g_fox_t, dy_col=0, name="fox_norm_bwd")
    d_o_sb_h, dg_sb = _group_rms_bwd(o_sb_h, d_on, g_sb_h, dy_col=1, name="sb_norm_bwd")

    dfq, dfk, dfv, ksum8, qsum = _fox_bwd(proj_h, fox_offs, f_row, o_fox_h, lse, d_o_fox_h, tq, tk_fox)
    dsq, dsk, dsv = _sb_bwd(proj_h, sb_offs, sb_w, d_o_sb_h, min(SB_TQ, s), tk_sb)
    ksum = jnp.sum(ksum8, axis=2).reshape(N_GROUP_HEADS, s // LANES, LANES)
    d_f_logit_h, d_bias_h = _forget_bwd(f_logit_h, bias_h, ksum,
                                        qsum.reshape(N_GROUP_HEADS, s // LANES, LANES))
    d_f_logit = d_f_logit_h.reshape(N_GROUP_HEADS, s).T

    d_proj = _merge_dproj((dfq, dfk, dfv), d_f_logit, (dsq, dsk, dsv))
    dw_in_p = _mm_tn(h1, d_proj, "d_w_in", tn=640)
    dconv_w = dcw2.transpose(1, 0, 2).reshape(3, 2 * D_FF)
    dconv_b = dcb2.reshape(1, 2 * D_FF)
    late = _exchange_start(
        [_split_cols(dw_in_p, N_DEV, in_shard, in_windows, "split_d_w_in"),
         dconv_w.astype(BF16).reshape(3, N_DEV, up_shard).transpose(1, 0, 2)],
        True, "grads_late_start")
    d_h1 = _mm_nt(d_proj, w_in_p + late[-1][0:1, 0:1].astype(BF16), F32, "d_h1", tk=640)
    grad_x, dg_attn = _rms_bwd(xs, d_h1, attn_norm_g, d_x1, dy_col=0, name="attn_norm_bwd", want_bf16=False)

    small_shapes = [(1, D_MODEL), (1, N_GROUP_HEADS), (1, GROUP_W), (1, GROUP_W), (1, D_MODEL),
                    (1, 2 * D_FF), (D_MODEL,), (1,)]
    spack = _pack([dg_attn, d_bias_h[:, 0, 0], dg_fox, dg_sb, dg_ffn, dconv_b, dg_final, loss_part[0, 0:1]],
                  SMALL_ROWS, F32)
    (srecv,) = _grad_exchange([], spack)
    r_out, r_up, r_down = _exchange_wait(early, True, srecv, "grads_early_wait")
    r_in, r_conv = _exchange_wait(late, True, r_out, "grads_late_wait")

    big = [_sum_adamw(g, w_[0], m_[0], v_[0], "adamw_" + tag)
           for g, w_, m_, v_, tag in zip(
               (r_in, r_out, r_up, r_down, r_conv), (w_in, w_out, w_up, w_down, conv_w), (m_w_in, m_w_out, m_w_up, m_w_down, m_conv_w),
               (v_w_in, v_w_out, v_w_up, v_w_down, v_conv_w), ("w_in", "w_out", "w_up", "w_down", "conv_w"))]

    def small_pack(a_attn, a_bias, a_fox, a_sb, a_ffn, a_cb, a_fin):
        return _pack([a_attn, a_bias, a_fox, a_sb, a_ffn, a_cb, a_fin, jnp.zeros((1,), F32)], SMALL_ROWS, F32)

    small = _sum_adamw(srecv, small_pack(attn_norm_g, forget_bias, fox_out_g, sb_out_g, ffn_norm_g, conv_b, final_norm_g),
                       small_pack(m_attn_norm_g, m_forget_bias, m_fox_out_g, m_sb_out_g, m_ffn_norm_g, m_conv_b, m_final_norm_g),
                       small_pack(v_attn_norm_g, v_forget_bias, v_fox_out_g, v_sb_out_g, v_ffn_norm_g, v_conv_b, v_final_norm_g),
                       "adamw_replicated", tr=SMALL_ROWS)

    outs = []
    loss = None
    for kind in range(4):
        b_in, b_out, b_up, b_down, b_conv = (res[kind] for res in big)
        s_attn, s_bias, s_fox, s_sb, s_ffn, s_cb, s_fin, s_loss = _unpack(small[kind], small_shapes)
        if kind == 0:
            loss = s_loss[0]
        outs += [s_attn, b_in[None], s_bias, s_fox, s_sb, b_out[None], s_ffn, b_up[None], b_conv[None], s_cb,
                 b_down[None], s_fin]
    return (loss, grad_x[None], *outs)
```

```python
import jax
import jax.numpy as jnp
from jax import lax
from jax.experimental import pallas as pl
from jax.experimental.pallas import tpu as pltpu

F32 = jnp.float32
BF16 = jnp.bfloat16

D_MODEL = 1024
HEAD_DIM = 64
N_GROUP_HEADS = 8
GROUP_W = N_GROUP_HEADS * HEAD_DIM
QKV_W = 3 * GROUP_W
IN_COLS = 2 * QKV_W + N_GROUP_HEADS
GATE_PAD = 128
IN_COLS_PAD = 2 * QKV_W + GATE_PAD
D_FF = 2816
N_DEV = 8
EPS = 1e-6
Q_SCALE = HEAD_DIM ** -0.5

ADAM_LR = 0.001
ADAM_B1 = 0.9
ADAM_B2 = 0.999
ADAM_EPS = 1e-08
ADAM_WD = 0.01
ADAM_STEP = 10

LANES = 128
SMALL_ROWS = 80
VMEM_LIMIT = 56 * 1024 * 1024
NEG_BIG = -1e30
ATTN_TQ = 512
SB_TQ = 512
FOX_TK = 512
SB_TK = 256
MESH = pl.DeviceIdType.MESH


def _cparams(sem=None, **kw):
    return pltpu.CompilerParams(dimension_semantics=sem, vmem_limit_bytes=VMEM_LIMIT, **kw)


def _tile(n, target, mult=LANES):
    if n <= target:
        return n
    t = (target // mult) * mult
    while t >= mult:
        if n % t == 0:
            return t
        t -= mult
    return n


def _seg_len(shape):
    n = 1
    for s in shape:
        n *= s
    return -(-n // LANES) * LANES


def _pack(arrs, rows, dtype):
    parts = []
    for a in arrs:
        f = a.reshape(-1).astype(dtype)
        parts.append(jnp.pad(f, (0, _seg_len(a.shape) - f.shape[0])))
    flat = jnp.concatenate(parts)
    flat = jnp.pad(flat, (0, rows * LANES - flat.shape[0]))
    return flat.reshape(rows, LANES)


def _unpack(p, shapes, lead=()):
    flat = p.reshape(lead + (-1,))
    out, off = [], 0
    for shp in shapes:
        n = 1
        for s in shp:
            n *= s
        out.append(flat[..., off:off + n].reshape(lead + tuple(shp)))
        off += _seg_len(shp)
    return out


def _my_pos():
    return lax.axis_index("x"), lax.axis_index("y"), lax.axis_index("c")


def _all_gather(blocks):
    n = len(blocks)

    def body(*refs):
        x_refs, out_refs = refs[:n], refs[n:2 * n]
        send_sems, recv_sems, local_sems = refs[2 * n:]
        x, y, c = _my_pos()
        me, sibling = (x, y, c), (x, y, 1 - c)
        chips = [(1 - x, y), (x, 1 - y), (1 - x, 1 - y)]

        def copy(a, k, blk, to, own=False):
            px, py, pc = blk
            slot = out_refs[a].at[4 * px + 2 * py + pc]
            return pltpu.make_async_remote_copy(
                src_ref=x_refs[a] if own else slot, dst_ref=slot,
                send_sem=send_sems.at[a, k], recv_sem=recv_sems.at[a, k],
                device_id=to, device_id_type=MESH)

        mine = [pltpu.make_async_copy(x_refs[a], out_refs[a].at[4 * x + 2 * y + c], local_sems.at[a])
                for a in range(n)]
        for cp in mine:
            cp.start()
        first = []
        for a in range(n):
            first.append(copy(a, 0, me, sibling, own=True))
            first += [copy(a, 1 + j, me, (*chip, c), own=True) for j, chip in enumerate(chips)]
        for cp in first:
            cp.start()
        passed = []
        for j, chip in enumerate(chips):
            for a in range(n):
                copy(a, 1 + j, (*chip, c), me).wait_recv()
                passed.append(copy(a, 4 + j, (*chip, c), sibling))
                passed[-1].start()
        for a in range(n):
            copy(a, 0, sibling, me).wait_recv()
            for j, chip in enumerate(chips):
                copy(a, 4 + j, (*chip, 1 - c), me).wait_recv()
        for cp in first + passed:
            cp.wait_send()
        for cp in mine:
            cp.wait()

    hbm = pl.BlockSpec(memory_space=pl.ANY)
    return pl.pallas_call(
        body, name="weights_all_gather",
        out_shape=[jax.ShapeDtypeStruct((N_DEV,) + b.shape, b.dtype) for b in blocks],
        in_specs=[hbm] * n, out_specs=[hbm] * n,
        scratch_shapes=[pltpu.SemaphoreType.DMA((n, 7)), pltpu.SemaphoreType.DMA((n, 7)),
                        pltpu.SemaphoreType.DMA((n,))],
    )(*blocks)


def _grad_exchange(slabs, spack):
    n = len(slabs) + 1

    def body(*refs):
        in_refs, out_refs = refs[:n], refs[n:2 * n]
        send_sems, recv_sems, local_sems = refs[2 * n:]
        x, y, c = _my_pos()
        my_id = 4 * x + 2 * y + c

        def src_of(a, dev):
            return in_refs[a] if a == n - 1 else in_refs[a].at[dev]

        own = [pltpu.make_async_copy(src_of(a, my_id), out_refs[a].at[my_id], local_sems.at[a])
               for a in range(n)]
        for cp in own:
            cp.start()
        sends, arrivals = [], []
        for k in range(1, N_DEV):
            px, py, pc = x ^ (k >> 2), y ^ ((k >> 1) & 1), c ^ (k & 1)
            peer_id = 4 * px + 2 * py + pc
            for a in range(n):
                for dst_slot, bucket in ((my_id, sends), (peer_id, arrivals)):
                    bucket.append(pltpu.make_async_remote_copy(
                        src_ref=src_of(a, peer_id), dst_ref=out_refs[a].at[dst_slot],
                        send_sem=send_sems.at[a, k - 1], recv_sem=recv_sems.at[a, k - 1],
                        device_id=(px, py, pc), device_id_type=MESH))
        for cp in sends:
            cp.start()
        for cp in arrivals:
            cp.wait_recv()
        for cp in sends:
            cp.wait_send()
        for cp in own:
            cp.wait()

    hbm = pl.BlockSpec(memory_space=pl.ANY)
    return pl.pallas_call(
        body, name="grad_exchange",
        out_shape=[jax.ShapeDtypeStruct(g.shape, g.dtype) for g in slabs]
        + [jax.ShapeDtypeStruct((N_DEV,) + spack.shape, spack.dtype)],
        in_specs=[hbm] * n, out_specs=[hbm] * n,
        scratch_shapes=[pltpu.SemaphoreType.DMA((n, 7)), pltpu.SemaphoreType.DMA((n, 7)),
                        pltpu.SemaphoreType.DMA((n,))],
    )(*slabs, spack)


_HBM = pl.BlockSpec(memory_space=pltpu.HBM)
_SEM = pl.BlockSpec(memory_space=pltpu.SEMAPHORE)
_EFFECT = pltpu.SideEffectType.DATAFLOW_SIDE_EFFECTING


def _my_id():
    x, y, c = _my_pos()
    return 4 * x + 2 * y + c


def _peer_copies(src_refs, land_refs, send_sems, recv_sems, per_peer):
    x, y, c = _my_pos()
    my_id = 4 * x + 2 * y + c
    copies = []
    for k in range(1, N_DEV):
        px, py, pc = x ^ (k >> 2), y ^ ((k >> 1) & 1), c ^ (k & 1)
        for a, (src, land) in enumerate(zip(src_refs, land_refs)):
            copies.append(pltpu.make_async_remote_copy(
                src_ref=src.at[4 * px + 2 * py + pc] if per_peer else src, dst_ref=land.at[my_id],
                send_sem=send_sems.at[a * (N_DEV - 1) + k - 1], recv_sem=recv_sems.at[a * (N_DEV - 1) + k - 1],
                device_id=(px, py, pc), device_id_type=MESH))
    return copies


def _exchange_start(srcs, per_peer, name):
    n = len(srcs)
    lands = [lax.empty(s.shape if per_peer else (N_DEV,) + s.shape, s.dtype) for s in srcs]

    def body(*refs):
        src_refs, land_refs = refs[:n], refs[n:2 * n]
        send_sems, recv_sems = refs[2 * n], refs[2 * n + 1]
        token = refs[-1]
        for cp in _peer_copies(src_refs, land_refs, send_sems, recv_sems, per_peer):
            cp.start()
        token[...] = jnp.zeros_like(token)

    outs = pl.pallas_call(
        body, name=name,
        out_shape=(pltpu.SemaphoreType.DMA((n * (N_DEV - 1),)), pltpu.SemaphoreType.DMA((n * (N_DEV - 1),)),
                   *[pltpu.HBM(a.shape, a.dtype) for a in srcs + lands],
                   jax.ShapeDtypeStruct((8, LANES), F32)),
        in_specs=[_HBM] * (2 * n),
        out_specs=(_SEM, _SEM, *[_HBM] * (2 * n), pl.BlockSpec(memory_space=pltpu.VMEM)),
        input_output_aliases={a: 2 + a for a in range(2 * n)},
        compiler_params=pltpu.CompilerParams(has_side_effects=_EFFECT),
    )(*[pltpu.with_memory_space_constraint(a, pltpu.HBM) for a in srcs + lands])
    return outs[0], outs[1], list(outs[2:2 + n]), list(outs[2 + n:2 + 2 * n]), outs[-1]


def _exchange_wait(handles, per_peer, after, name):
    send_sems, recv_sems, srcs, lands, _ = handles
    n = len(srcs)

    def body(*refs):
        src_refs, land_refs = refs[:n], refs[n:2 * n]
        for cp in _peer_copies(src_refs, land_refs, refs[2 * n], refs[2 * n + 1], per_peer):
            cp.wait_send()
            cp.wait_recv()

    outs = pl.pallas_call(
        body, name=name,
        out_shape=tuple(pltpu.HBM(a.shape, a.dtype) for a in srcs + lands),
        in_specs=[_HBM] * (2 * n) + [_SEM, _SEM, pl.BlockSpec(memory_space=pl.ANY)],
        out_specs=tuple([_HBM] * (2 * n)),
        input_output_aliases={a: a for a in range(2 * n)},
        compiler_params=pltpu.CompilerParams(has_side_effects=_EFFECT),
    )(*srcs, *lands, send_sems, recv_sems, after)
    me = _my_id()
    filled = []
    for src, land in zip(outs[:n], outs[n:]):
        own = lax.dynamic_index_in_dim(src, me, 0, keepdims=True) if per_peer else src[None]
        filled.append(lax.dynamic_update_slice_in_dim(land, own, me, 0))
    return filled


def _col_windows(n_shards, width, gap_at=None, gap=0):
    out = []
    for j in range(n_shards):
        g0, g1 = j * width, (j + 1) * width
        cuts = [g0, g1] if gap_at is None or not g0 < gap_at < g1 else [g0, gap_at, g1]
        for a, b in zip(cuts[:-1], cuts[1:]):
            out.append((j, a - g0, b - g0, a + (gap if gap_at is not None and a >= gap_at else 0)))
    return out


def _assemble_cols(parts, total, windows, name, tr=256):
    n, rows, w = parts.shape
    tr = _tile(rows, tr, 16)

    def body(p_ref, o_ref):
        o_ref[...] = jnp.zeros_like(o_ref)
        for j, lo, hi, dst in windows:
            o_ref[:, dst:dst + hi - lo] = p_ref[j, :, lo:hi]

    return pl.pallas_call(
        body, name=name, grid=(rows // tr,),
        in_specs=[pl.BlockSpec((n, tr, w), lambda i: (0, i, 0))],
        out_specs=pl.BlockSpec((tr, total), lambda i: (i, 0)),
        out_shape=jax.ShapeDtypeStruct((rows, total), parts.dtype),
        compiler_params=_cparams(("parallel",)),
    )(parts)


def _split_cols(full, n, w, windows, name, tr=256):
    rows, total = full.shape
    tr = _tile(rows, tr, 16)

    def body(f_ref, o_ref):
        for j, lo, hi, dst in windows:
            o_ref[j, :, lo:hi] = f_ref[:, dst:dst + hi - lo].astype(o_ref.dtype)

    return pl.pallas_call(
        body, name=name, grid=(rows // tr,),
        in_specs=[pl.BlockSpec((tr, total), lambda i: (i, 0))],
        out_specs=pl.BlockSpec((n, tr, w), lambda i: (0, i, 0)),
        out_shape=jax.ShapeDtypeStruct((n, rows, w), BF16),
        compiler_params=_cparams(("parallel",)),
    )(full)


_DIMS = {"nn": (((1,), (0,)), ((), ())), "nt": (((1,), (1,)), ((), ())), "tn": (((0,), (0,)), ((), ()))}


def _matmul(a, b, *, mode, grid, a_block, a_map, b_block, b_map, o_block, o_map, out_shape, name,
            resid=None):
    nk = grid[2]
    dims = _DIMS[mode]

    def body(*refs):
        if resid is None:
            a_ref, b_ref, o_ref, acc_ref = refs
            r_ref = None
        else:
            a_ref, b_ref, r_ref, o_ref, acc_ref = refs
        k = pl.program_id(2)

        @pl.when(k == 0)
        def _():
            acc_ref[...] = jnp.zeros_like(acc_ref)

        acc_ref[...] += lax.dot_general(a_ref[...], b_ref[...], dims, preferred_element_type=F32)

        @pl.when(k == nk - 1)
        def _():
            res = acc_ref[...]
            if r_ref is not None:
                res = r_ref[...] + res
            o_ref[...] = res.astype(o_ref.dtype)

    in_specs = [pl.BlockSpec(a_block, a_map), pl.BlockSpec(b_block, b_map)]
    args = [a, b]
    if resid is not None:
        in_specs.append(pl.BlockSpec(o_block, o_map))
        args.append(resid)
    acc_shape = tuple(d for d in o_block if d is not None)
    return pl.pallas_call(
        body, name=name, grid=grid, in_specs=in_specs,
        out_specs=pl.BlockSpec(o_block, o_map), out_shape=out_shape,
        scratch_shapes=[pltpu.VMEM(acc_shape, F32)],
        compiler_params=_cparams(("parallel", "parallel", "arbitrary")),
    )(*args)


def _mm_nn(a, b, out_dtype, name, resid=None, tm=1024, tn=1024, tk=1024):
    m, kk = a.shape
    n = b.shape[1]
    tm, tn, tk = _tile(m, tm, 8), _tile(n, tn), _tile(kk, tk)
    return _matmul(a, b, mode="nn", grid=(m // tm, n // tn, kk // tk),
                   a_block=(tm, tk), a_map=lambda i, j, k: (i, k),
                   b_block=(tk, tn), b_map=lambda i, j, k: (k, j),
                   o_block=(tm, tn), o_map=lambda i, j, k: (i, j),
                   out_shape=jax.ShapeDtypeStruct((m, n), out_dtype), name=name, resid=resid)


def _mm_nt(a, b, out_dtype, name, tm=1024, tn=1024, tk=1024):
    m, kk = a.shape
    n = b.shape[0]
    tm, tn, tk = _tile(m, tm, 8), _tile(n, tn), _tile(kk, tk)
    return _matmul(a, b, mode="nt", grid=(m // tm, n // tn, kk // tk),
                   a_block=(tm, tk), a_map=lambda i, j, k: (i, k),
                   b_block=(tn, tk), b_map=lambda i, j, k: (j, k),
                   o_block=(tm, tn), o_map=lambda i, j, k: (i, j),
                   out_shape=jax.ShapeDtypeStruct((m, n), out_dtype), name=name)


def _mm_tn(a, b, name, tm=1024, tn=1024, tk=1024):
    kk, m = a.shape
    n = b.shape[1]
    tm, tn, tk = _tile(m, tm), _tile(n, tn), _tile(kk, tk, 8)
    return _matmul(a, b, mode="tn", grid=(m // tm, n // tn, kk // tk),
                   a_block=(tk, tm), a_map=lambda i, j, k: (k, i),
                   b_block=(tk, tn), b_map=lambda i, j, k: (k, j),
                   o_block=(tm, tn), o_map=lambda i, j, k: (i, j),
                   out_shape=jax.ShapeDtypeStruct((m, n), F32), name=name)


def _mm_heads(a, b, name, tm=1024, tn=640):
    m, kk = a.shape
    n = b.shape[1]
    tm, tn = _tile(m, tm, 16), _tile(n, tn)
    per_tile = tn // HEAD_DIM

    def body(a_ref, b_ref, o_ref):
        res = jnp.dot(a_ref[...], b_ref[...], preferred_element_type=F32)
        for hh in range(per_tile):
            o_ref[hh] = res[:, hh * HEAD_DIM:(hh + 1) * HEAD_DIM].astype(o_ref.dtype)

    return pl.pallas_call(
        body, name=name, grid=(m // tm, n // tn),
        in_specs=[pl.BlockSpec((tm, kk), lambda i, j: (i, 0)), pl.BlockSpec((kk, tn), lambda i, j: (0, j))],
        out_specs=pl.BlockSpec((per_tile, tm, HEAD_DIM), lambda i, j: (j, i, 0)),
        out_shape=jax.ShapeDtypeStruct((n // HEAD_DIM, m, HEAD_DIM), BF16),
        compiler_params=_cparams(("parallel", "parallel")),
    )(a, b)


def _mm_up(h, w_up, tm=2048, tn=256):
    s = h.shape[0]
    tm = _tile(s, tm, 8)
    nh = D_FF // tn
    return _matmul(h, w_up, mode="nn", grid=(s // tm, 2 * nh, 1),
                   a_block=(tm, D_MODEL), a_map=lambda i, j, k: (i, 0),
                   b_block=(D_MODEL, tn), b_map=lambda i, j, k: (0, j),
                   o_block=(None, tm, tn), o_map=lambda i, j, k: (j // nh, i, j % nh),
                   out_shape=jax.ShapeDtypeStruct((2, s, D_FF), F32), name="up_proj")


def _mm_dup_nt(dup, w_up, tm=1024, tk=1408):
    s = dup.shape[1]
    tm = _tile(s, tm, 8)
    nh = D_FF // tk
    return _matmul(dup, w_up, mode="nt", grid=(s // tm, 1, 2 * nh),
                   a_block=(None, tm, tk), a_map=lambda i, j, k: (k // nh, i, k % nh),
                   b_block=(D_MODEL, tk), b_map=lambda i, j, k: (0, k),
                   o_block=(tm, D_MODEL), o_map=lambda i, j, k: (i, 0),
                   out_shape=jax.ShapeDtypeStruct((s, D_MODEL), F32), name="d_h2")


def _mm_dwup_tn(h, dup, tn=1408, tk=1024):
    s = h.shape[0]
    tk = _tile(s, tk, 8)
    nh = D_FF // tn
    return _matmul(h, dup, mode="tn", grid=(1, 2 * nh, s // tk),
                   a_block=(tk, D_MODEL), a_map=lambda i, j, k: (k, 0),
                   b_block=(None, tk, tn), b_map=lambda i, j, k: (j // nh, k, j % nh),
                   o_block=(D_MODEL, tn), o_map=lambda i, j, k: (0, j),
                   out_shape=jax.ShapeDtypeStruct((D_MODEL, 2 * D_FF), F32), name="d_w_up")


def _rms_fwd(x, g, tr=512):
    s, d = x.shape
    tr = _tile(s, tr, 8)

    def body(x_ref, g_ref, o_ref):
        xv = x_ref[...]
        r = lax.rsqrt(jnp.mean(xv * xv, axis=-1, keepdims=True) + EPS)
        o_ref[...] = (xv * r * g_ref[...]).astype(o_ref.dtype)

    return pl.pallas_call(
        body, name="rms_fwd", grid=(s // tr,),
        in_specs=[pl.BlockSpec((tr, d), lambda i: (i, 0)), pl.BlockSpec((1, d), lambda i: (0, 0))],
        out_specs=pl.BlockSpec((tr, d), lambda i: (i, 0)),
        out_shape=jax.ShapeDtypeStruct((s, d), BF16),
        compiler_params=_cparams(("parallel",)),
    )(x, g)


def _group_rms_fwd(o_fox, o_sb, g_fox, g_sb, tr=512):
    nh, s, dh = o_fox.shape
    tr = _tile(s, tr, 8)

    def body(a_ref, b_ref, ga_ref, gb_ref, o_ref):
        for src, g_ref, lo in ((a_ref, ga_ref, 0), (b_ref, gb_ref, nh * dh)):
            heads = [src[hh] for hh in range(nh)]
            ss = heads[0] * heads[0]
            for xv in heads[1:]:
                ss = ss + xv * xv
            r = lax.rsqrt(jnp.sum(ss, axis=-1, keepdims=True) * (1.0 / (nh * dh)) + EPS)
            for hh, xv in enumerate(heads):
                o_ref[:, lo + hh * dh:lo + (hh + 1) * dh] = (xv * r * g_ref[hh]).astype(o_ref.dtype)

    heads_blk = pl.BlockSpec((nh, tr, dh), lambda i: (0, i, 0))
    gain = pl.BlockSpec((nh, 1, dh), lambda i: (0, 0, 0))
    return pl.pallas_call(
        body, name="group_rms_fwd", grid=(s // tr,),
        in_specs=[heads_blk, heads_blk, gain, gain],
        out_specs=pl.BlockSpec((tr, 2 * nh * dh), lambda i: (i, 0)),
        out_shape=jax.ShapeDtypeStruct((s, 2 * nh * dh), BF16),
        compiler_params=_cparams(("parallel",)),
    )(o_fox, o_sb, g_fox, g_sb)


def _group_rms_bwd(x, dy, g, *, dy_col, name, tr=512):
    nh, s, dh = x.shape
    tr = _tile(s, tr, 8)
    d = nh * dh

    def body(x_ref, dy_ref, g_ref, dx_ref, dg_ref):
        @pl.when(pl.program_id(0) == 0)
        def _():
            dg_ref[...] = jnp.zeros_like(dg_ref)

        dyv = dy_ref[...]
        xs_ = [x_ref[hh] for hh in range(nh)]
        dys = [dyv[:, hh * dh:(hh + 1) * dh] for hh in range(nh)]
        ss = xs_[0] * xs_[0]
        for xv in xs_[1:]:
            ss = ss + xv * xv
        r = lax.rsqrt(jnp.sum(ss, axis=-1, keepdims=True) * (1.0 / d) + EPS)
        xh = [xv * r for xv in xs_]
        gy = [dys[hh] * g_ref[hh] for hh in range(nh)]
        dot = xh[0] * gy[0]
        for hh in range(1, nh):
            dot = dot + xh[hh] * gy[hh]
        mean_dot = jnp.sum(dot, axis=-1, keepdims=True) * (1.0 / d)
        for hh in range(nh):
            dx_ref[hh] = r * (gy[hh] - xh[hh] * mean_dot)
            dg_ref[hh] += jnp.sum(dys[hh] * xh[hh], axis=0, keepdims=True)

    heads_blk = pl.BlockSpec((nh, tr, dh), lambda i: (0, i, 0))
    gain = pl.BlockSpec((nh, 1, dh), lambda i: (0, 0, 0))
    return pl.pallas_call(
        body, name=name, grid=(s // tr,),
        in_specs=[heads_blk, pl.BlockSpec((tr, d), lambda i: (i, dy_col)), gain],
        out_specs=[heads_blk, gain],
        out_shape=[jax.ShapeDtypeStruct((nh, s, dh), F32), jax.ShapeDtypeStruct((nh, 1, dh), F32)],
        compiler_params=_cparams(("arbitrary",)),
    )(x, dy, g)


def _merge_dproj(parts_fox, d_gate, parts_sb, tr=256):
    nh, s, dh = parts_fox[0].shape
    tr = _tile(s, tr, 16)

    def body(*refs):
        o_ref = refs[-1]
        gate_ref = refs[3]
        col = 0
        for ref in refs[:3]:
            for hh in range(nh):
                o_ref[:, col:col + dh] = ref[hh].astype(o_ref.dtype)
                col += dh
        o_ref[:, col:col + GATE_PAD] = jnp.zeros((tr, GATE_PAD), o_ref.dtype)
        o_ref[:, col:col + N_GROUP_HEADS] = gate_ref[...].astype(o_ref.dtype)
        col += GATE_PAD
        for ref in refs[4:7]:
            for hh in range(nh):
                o_ref[:, col:col + dh] = ref[hh].astype(o_ref.dtype)
                col += dh

    heads_blk = pl.BlockSpec((nh, tr, dh), lambda i: (0, i, 0))
    return pl.pallas_call(
        body, name="merge_d_proj", grid=(s // tr,),
        in_specs=[heads_blk] * 3 + [pl.BlockSpec((tr, N_GROUP_HEADS), lambda i: (i, 0))] + [heads_blk] * 3,
        out_specs=pl.BlockSpec((tr, IN_COLS_PAD), lambda i: (i, 0)),
        out_shape=jax.ShapeDtypeStruct((s, IN_COLS_PAD), BF16),
        compiler_params=_cparams(("parallel",)),
    )(*parts_fox, d_gate, *parts_sb)


def _rms_bwd(x, dy, g, resid, *, dy_col, name, want_bf16, tr=512):
    s, d = x.shape
    tr = _tile(s, tr, 8)
    has_resid = resid is not None

    def body(*refs):
        refs = list(refs)
        x_ref, dy_ref, g_ref = refs[:3]
        r_ref = refs[3] if has_resid else None
        outs = refs[4:] if has_resid else refs[3:]
        dx_ref = outs[0]
        dxb_ref = outs[1] if want_bf16 else None
        dg_ref = outs[-1]

        @pl.when(pl.program_id(0) == 0)
        def _():
            dg_ref[...] = jnp.zeros_like(dg_ref)

        xv = x_ref[...]
        dyv = dy_ref[...]
        r = lax.rsqrt(jnp.mean(xv * xv, axis=-1, keepdims=True) + EPS)
        xh = xv * r
        gy = dyv * g_ref[...]
        dx = r * (gy - xh * jnp.mean(xh * gy, axis=-1, keepdims=True))
        if r_ref is not None:
            dx = r_ref[...] + dx
        dx_ref[...] = dx
        if dxb_ref is not None:
            dxb_ref[...] = dx.astype(BF16)
        dg_ref[...] += jnp.sum(dyv * xh, axis=0, keepdims=True)

    row = pl.BlockSpec((tr, d), lambda i: (i, 0))
    in_specs = [row, pl.BlockSpec((tr, d), lambda i: (i, dy_col)), pl.BlockSpec((1, d), lambda i: (0, 0))]
    args = [x, dy, g]
    if has_resid:
        in_specs.append(row)
        args.append(resid)
    out_specs = [row]
    out_shape = [jax.ShapeDtypeStruct((s, d), F32)]
    if want_bf16:
        out_specs.append(row)
        out_shape.append(jax.ShapeDtypeStruct((s, d), BF16))
    out_specs.append(pl.BlockSpec((1, d), lambda i: (0, 0)))
    out_shape.append(jax.ShapeDtypeStruct((1, d), F32))
    return pl.pallas_call(
        body, name=name, grid=(s // tr,), in_specs=in_specs, out_specs=out_specs, out_shape=out_shape,
        compiler_params=_cparams(("arbitrary",)),
    )(*args)


def _loss_head(x2, target, g, tr=512):
    s, d = x2.shape
    tr = _tile(s, tr, 8)

    def body(x_ref, t_ref, g_ref, dx_ref, dxb_ref, dg_ref, loss_ref):
        @pl.when(pl.program_id(0) == 0)
        def _():
            dg_ref[...] = jnp.zeros_like(dg_ref)
            loss_ref[...] = jnp.zeros_like(loss_ref)

        xv = x_ref[...]
        gv = g_ref[...]
        r = lax.rsqrt(jnp.mean(xv * xv, axis=-1, keepdims=True) + EPS)
        xh = xv * r
        err = xh * gv - t_ref[...]
        loss_ref[...] += jnp.sum(jnp.mean(err * err, axis=-1, keepdims=True), axis=0, keepdims=True) * 0.5
        dyv = err * (1.0 / d)
        gy = dyv * gv
        dx = r * (gy - xh * jnp.mean(xh * gy, axis=-1, keepdims=True))
        dx_ref[...] = dx
        dxb_ref[...] = dx.astype(BF16)
        dg_ref[...] += jnp.sum(dyv * xh, axis=0, keepdims=True)

    row = pl.BlockSpec((tr, d), lambda i: (i, 0))
    return pl.pallas_call(
        body, name="loss_head", grid=(s // tr,),
        in_specs=[row, row, pl.BlockSpec((1, d), lambda i: (0, 0))],
        out_specs=[row, row, pl.BlockSpec((1, d), lambda i: (0, 0)), pl.BlockSpec((1, LANES), lambda i: (0, 0))],
        out_shape=[jax.ShapeDtypeStruct((s, d), F32), jax.ShapeDtypeStruct((s, d), BF16),
                   jax.ShapeDtypeStruct((1, d), F32), jax.ShapeDtypeStruct((1, LANES), F32)],
        compiler_params=_cparams(("arbitrary",)),
    )(x2, target, g)


def _conv_taps(cur, prev8, w, b, first):
    prev8 = jnp.where(first, 0.0, prev8)
    ext = jnp.concatenate([prev8, cur], axis=0)
    x1 = pltpu.roll(ext, 1, 0)[8:]
    x2 = pltpu.roll(ext, 2, 0)[8:]
    u = b + w[0:1] * x2
    u = u + w[1:2] * x1
    u = u + w[2:3] * cur
    return u, x1, x2


def _conv_gate_fwd(up, conv_w, conv_b, tm=1024, tn=256):
    s = up.shape[1]
    tm = _tile(s, tm, 8)
    nrb = s // tm
    rb8 = tm // 8

    def body(g_ref, v_ref, gp_ref, vp_ref, wg_ref, wv_ref, bg_ref, bv_ref, o_ref):
        first = pl.program_id(1) == 0
        ug, _, _ = _conv_taps(g_ref[...], gp_ref[...], wg_ref[...], bg_ref[...], first)
        uv, _, _ = _conv_taps(v_ref[...], vp_ref[...], wv_ref[...], bv_ref[...], first)
        sg = 1.0 / (1.0 + jnp.exp(-ug))
        o_ref[...] = (ug * sg * uv).astype(o_ref.dtype)

    def cur(h):
        return pl.BlockSpec((None, tm, tn), lambda j, i: (h, i, j))

    def prev(h):
        return pl.BlockSpec((None, 8, tn), lambda j, i: (h, jnp.maximum(i * rb8 - 1, 0), j))

    def par(h, r):
        return pl.BlockSpec((None, r, tn), lambda j, i: (h, 0, j))

    return pl.pallas_call(
        body, name="conv_gate_fwd", grid=(D_FF // tn, nrb),
        in_specs=[cur(0), cur(1), prev(0), prev(1), par(0, 3), par(1, 3), par(0, 1), par(1, 1)],
        out_specs=pl.BlockSpec((tm, tn), lambda j, i: (i, j)),
        out_shape=jax.ShapeDtypeStruct((s, D_FF), BF16),
        compiler_params=_cparams(("parallel", "parallel")),
    )(up, up, up, up, conv_w, conv_w, conv_b, conv_b)


def _conv_gate_bwd(up, dact, conv_w, conv_b, tm=512, tn=256):
    s = up.shape[1]
    tm = _tile(s, tm, 8)
    nrb = s // tm
    rb8 = tm // 8

    def body(g_ref, v_ref, gp_ref, vp_ref, da_ref, wg_ref, wv_ref, bg_ref, bv_ref,
             dup_ref, dcw_ref, dcb_ref, carry_ref):
        i = pl.program_id(1)
        first = i == nrb - 1

        @pl.when(i == 0)
        def _():
            carry_ref[...] = jnp.zeros_like(carry_ref)
            dcw_ref[...] = jnp.zeros_like(dcw_ref)
            dcb_ref[...] = jnp.zeros_like(dcb_ref)

        curs = (g_ref[...], v_ref[...])
        ws = (wg_ref[...], wv_ref[...])
        ug, g1, g2 = _conv_taps(curs[0], gp_ref[...], ws[0], bg_ref[...], first)
        uv, v1, v2 = _conv_taps(curs[1], vp_ref[...], ws[1], bv_ref[...], first)
        sg = 1.0 / (1.0 + jnp.exp(-ug))
        da = da_ref[...].astype(F32)
        d_v = da * (ug * sg)
        d_g = da * uv * (sg * (1.0 + ug * (1.0 - sg)))
        for h, (du, x0, x1, x2) in enumerate(((d_g, curs[0], g1, g2), (d_v, curs[1], v1, v2))):
            dcb_ref[h] += jnp.sum(du, axis=0, keepdims=True)
            dcw_ref[h, 0:1, :] += jnp.sum(du * x2, axis=0, keepdims=True)
            dcw_ref[h, 1:2, :] += jnp.sum(du * x1, axis=0, keepdims=True)
            dcw_ref[h, 2:3, :] += jnp.sum(du * x0, axis=0, keepdims=True)
            ext = jnp.concatenate([du, carry_ref[h]], axis=0)
            n1 = pltpu.roll(ext, tm + 7, 0)[:tm]
            n2 = pltpu.roll(ext, tm + 6, 0)[:tm]
            w = ws[h]
            dup_ref[h] = (w[2:3] * du + w[1:2] * n1 + w[0:1] * n2).astype(dup_ref.dtype)
            carry_ref[h] = du[:8]

    def cur(h):
        return pl.BlockSpec((None, tm, tn), lambda j, i: (h, nrb - 1 - i, j))

    def prev(h):
        return pl.BlockSpec((None, 8, tn), lambda j, i: (h, jnp.maximum((nrb - 1 - i) * rb8 - 1, 0), j))

    def par(h, r):
        return pl.BlockSpec((None, r, tn), lambda j, i: (h, 0, j))

    return pl.pallas_call(
        body, name="conv_gate_bwd", grid=(D_FF // tn, nrb),
        in_specs=[cur(0), cur(1), prev(0), prev(1),
                  pl.BlockSpec((tm, tn), lambda j, i: (nrb - 1 - i, j)),
                  par(0, 3), par(1, 3), par(0, 1), par(1, 1)],
        out_specs=[pl.BlockSpec((2, tm, tn), lambda j, i: (0, nrb - 1 - i, j)),
                   pl.BlockSpec((2, 3, tn), lambda j, i: (0, 0, j)),
                   pl.BlockSpec((2, 1, tn), lambda j, i: (0, 0, j))],
        out_shape=[jax.ShapeDtypeStruct((2, s, D_FF), BF16),
                   jax.ShapeDtypeStruct((2, 3, D_FF), F32),
                   jax.ShapeDtypeStruct((2, 1, D_FF), F32)],
        scratch_shapes=[pltpu.VMEM((2, 8, tn), F32)],
        compiler_params=_cparams(("parallel", "arbitrary")),
    )(up, up, up, up, dact, conv_w, conv_w, conv_b, conv_b)


def _split_dot(x, tri, terms):
    piece = x.astype(BF16)
    out = jnp.dot(piece, tri, preferred_element_type=F32)
    rest = x
    for _ in range(terms - 1):
        rest = rest - piece.astype(F32)
        piece = rest.astype(BF16)
        out = out + jnp.dot(piece, tri, preferred_element_type=F32)
    return out


def _split_dot_rhs(tri, x, terms):
    piece = x.astype(BF16)
    out = jnp.dot(tri, piece, preferred_element_type=F32)
    rest = x
    for _ in range(terms - 1):
        rest = rest - piece.astype(F32)
        piece = rest.astype(BF16)
        out = out + jnp.dot(tri, piece, preferred_element_type=F32)
    return out


def _tri(n, kind):
    r = lax.broadcasted_iota(jnp.int32, (n, n), 0)
    c = lax.broadcasted_iota(jnp.int32, (n, n), 1)
    cond = {"le": r <= c, "ge": r >= c, "lt": r < c, "gt": r > c}[kind]
    return jnp.where(cond, 1.0, 0.0).astype(BF16)


def _log_sigmoid(x):
    return jnp.minimum(x, 0.0) - jnp.log(1.0 + jnp.exp(-jnp.abs(x)))


def _forget_fwd(f_logit, bias):
    h, r, _ = f_logit.shape

    def body(x_ref, b_ref, o_ref):
        lf = _log_sigmoid(x_ref[...] + b_ref[...])
        within = _split_dot(lf, _tri(LANES, "le"), 3)
        row_tot = jnp.broadcast_to(within[:, LANES - 1:LANES], (r, LANES))
        before = _split_dot_rhs(_tri(r, "gt"), row_tot, 3)
        o_ref[...] = within + before

    blk = pl.BlockSpec((None, r, LANES), lambda i: (i, 0, 0))
    return pl.pallas_call(
        body, name="forget_cumsum_fwd", grid=(h,),
        in_specs=[blk, pl.BlockSpec((None, 1, LANES), lambda i: (i, 0, 0))],
        out_specs=blk, out_shape=jax.ShapeDtypeStruct((h, r, LANES), F32),
        compiler_params=_cparams(("parallel",)),
    )(f_logit, bias)


def _forget_bwd(f_logit, bias, ksum, qsum):
    h, r, _ = f_logit.shape

    def body(x_ref, b_ref, k_ref, q_ref, dx_ref, db_ref):
        d_f = q_ref[...] - k_ref[...]
        within = _split_dot(d_f, _tri(LANES, "ge"), 3)
        row_tot = jnp.broadcast_to(within[:, 0:1], (r, LANES))
        after = _split_dot_rhs(_tri(r, "lt"), row_tot, 3)
        xv = x_ref[...] + b_ref[...]
        dx = (within + after) * jnp.exp(_log_sigmoid(-xv))
        dx_ref[...] = dx
        db_ref[...] = jnp.broadcast_to(jnp.sum(dx), (1, LANES))

    blk = pl.BlockSpec((None, r, LANES), lambda i: (i, 0, 0))
    one = pl.BlockSpec((None, 1, LANES), lambda i: (i, 0, 0))
    return pl.pallas_call(
        body, name="forget_cumsum_bwd", grid=(h,),
        in_specs=[blk, one, blk, blk], out_specs=[blk, one],
        out_shape=[jax.ShapeDtypeStruct((h, r, LANES), F32), jax.ShapeDtypeStruct((h, 1, LANES), F32)],
        compiler_params=_cparams(("parallel",)),
    )(f_logit, bias, ksum, qsum)


def _head_specs(s, tq):
    qblk = pl.BlockSpec((None, tq, HEAD_DIM), lambda h, i: (h, i, 0))
    full = pl.BlockSpec((None, s, HEAD_DIM), lambda h, i: (h, 0, 0))
    col = pl.BlockSpec((None, tq, 1), lambda h, i: (h, i, 0))
    return qblk, full, col


def _qkv_specs(s, tq, offs):
    q_off, k_off, v_off = offs
    return (pl.BlockSpec((None, tq, HEAD_DIM), lambda h, i: (h + q_off, i, 0)),
            pl.BlockSpec((None, s, HEAD_DIM), lambda h, i: (h + k_off, 0, 0)),
            pl.BlockSpec((None, s, HEAD_DIM), lambda h, i: (h + v_off, 0, 0)))


def _scaled(q_ref):
    return (q_ref[...].astype(F32) * Q_SCALE).astype(BF16)


_NT = (((1,), (1,)), ((), ()))
_TN = (((0,), (0,)), ((), ()))


def _cols_minus_rows(rows, cols):
    return lax.broadcasted_iota(jnp.int32, (rows, cols), 1) - lax.broadcasted_iota(jnp.int32, (rows, cols), 0)


def _fox_fwd(q_aug, k_aug, v_ones, tq, tk):
    h, s = N_GROUP_HEADS, q_aug.shape[1]
    nk = s // tk
    assert tq == tk

    def body(q_ref, k_ref, v_ref, o_ref, lse_ref, m_ref, acc_ref, z0, z1):
        i = pl.program_id(1)
        qs = _scaled(q_ref)
        m_ref[...] = jnp.full_like(m_ref, NEG_BIG)
        acc_ref[...] = jnp.zeros_like(acc_ref)

        ahead = _cols_minus_rows(tq, tk)

        def block_of(j):
            return jnp.minimum(j, nk - 1)

        def keys_of(j):
            return pl.ds(pl.multiple_of(block_of(j) * tk, tk), tk)

        def logits(j):
            return lax.dot_general(qs, k_ref[keys_of(j), :], _NT, preferred_element_type=F32)

        def soft(j, sc, masked):
            if masked:
                sc = jnp.where(ahead <= (i - j) * tk, sc, NEG_BIG)
            m_old = m_ref[...]
            m_new = jnp.maximum(m_old, jnp.max(sc, axis=-1, keepdims=True))
            p = jnp.exp(sc - m_new)
            acc_ref[...] = jnp.exp(m_old - m_new) * acc_ref[...] + jnp.dot(
                p.astype(BF16), v_ref[keys_of(j), :], preferred_element_type=F32)
            m_ref[...] = m_new

        z0[...] = logits(0)

        def trip(p, masked):
            j = 2 * p
            z1[...] = logits(j + 1)
            soft(j, z0[...], masked)
            z0[...] = logits(j + 2)
            soft(j + 1, z1[...], masked)

        def step(p, carry):
            trip(p, False)
            return carry

        lax.fori_loop(0, i // 2, step, 0)
        trip(i // 2, True)
        l = acc_ref[:, HEAD_DIM:HEAD_DIM + 1]
        o_ref[...] = acc_ref[:, :HEAD_DIM] / l
        lse_ref[...] = m_ref[...] + jnp.log(l)

    qblk, full, colspec = _head_specs(s, tq)
    wide = pl.BlockSpec((None, s, 2 * HEAD_DIM), lambda hh, i: (hh, 0, 0))
    return pl.pallas_call(
        body, name="fox_fwd", grid=(h, s // tq),
        in_specs=[pl.BlockSpec((None, tq, 2 * HEAD_DIM), lambda hh, i: (hh, i, 0)), wide, wide],
        out_specs=[qblk, colspec],
        out_shape=[jax.ShapeDtypeStruct((h, s, HEAD_DIM), F32), jax.ShapeDtypeStruct((h, s, 1), F32)],
        scratch_shapes=[pltpu.VMEM((tq, 1), F32), pltpu.VMEM((tq, 2 * HEAD_DIM), F32),
                        pltpu.VMEM((tq, tk), F32), pltpu.VMEM((tq, tk), F32)],
        compiler_params=_cparams(("parallel", "parallel")),
    )(q_aug, k_aug, v_ones)


def _fox_bwd(q_aug, k_aug, qkv, v_off, o, lse, d_o, tq, tk):
    h, s = N_GROUP_HEADS, qkv.shape[1]
    nk = s // tk
    assert tq == tk

    def body(q_ref, k_ref, v_ref, o_ref, lse_ref, do_ref,
             dq_ref, dk_ref, dv_ref, ks_ref, qs_ref, dq_acc, qsum_acc, z0, z1, p0, p1):
        i = pl.program_id(1)

        @pl.when(i == 0)
        def _():
            dk_ref[...] = jnp.zeros_like(dk_ref)
            dv_ref[...] = jnp.zeros_like(dv_ref)
            ks_ref[...] = jnp.zeros_like(ks_ref)

        qs = _scaled(q_ref)
        lse_v = lse_ref[...]
        dob = do_ref[...].astype(BF16)
        delta = jnp.sum(dob.astype(F32) * o_ref[...], axis=-1, keepdims=True)
        dq_acc[...] = jnp.zeros_like(dq_acc)
        qsum_acc[...] = jnp.zeros_like(qsum_acc)

        ahead = _cols_minus_rows(tq, tk)

        def block_of(j):
            return jnp.minimum(j, nk - 1)

        def keys_of(j):
            return pl.ds(pl.multiple_of(block_of(j) * tk, tk), tk)

        def products(j):
            at = keys_of(j)
            return (lax.dot_general(qs, k_ref[at, :], _NT, preferred_element_type=F32),
                    lax.dot_general(dob, v_ref[at, :], _NT, preferred_element_type=F32))

        def grads(j, sc, dp, masked):
            at = keys_of(j)
            if masked:
                sc = jnp.where(ahead <= (i - j) * tk, sc, NEG_BIG)
            p = jnp.exp(sc - lse_v)
            ds = p * (dp - delta)
            dsb = ds.astype(BF16)
            dq_acc[...] += jnp.dot(dsb, k_ref[at, :], preferred_element_type=F32)[:, :HEAD_DIM]
            dk_ref[at, :] += lax.dot_general(dsb, qs, _TN, preferred_element_type=F32)[:, :HEAD_DIM]
            dv_ref[at, :] += lax.dot_general(p.astype(BF16), dob, _TN, preferred_element_type=F32)
            ks_ref[block_of(j)] += jnp.sum(ds.reshape(tq // 8, 8, tk), axis=0)
            qsum_acc[...] += jnp.sum(ds, axis=-1, keepdims=True)

        z0[...], p0[...] = products(0)

        def trip(pp, masked):
            j = 2 * pp
            z1[...], p1[...] = products(j + 1)
            grads(j, z0[...], p0[...], masked)
            z0[...], p0[...] = products(j + 2)
            grads(j + 1, z1[...], p1[...], masked)

        def step(pp, carry):
            trip(pp, False)
            return carry

        lax.fori_loop(0, i // 2, step, 0)
        trip(i // 2, True)
        dq_ref[...] = dq_acc[...] * Q_SCALE
        qs_ref[...] = qsum_acc[...]

    qblk, full, colspec = _head_specs(s, tq)
    big = pltpu.VMEM((tq, tk), F32)
    return pl.pallas_call(
        body, name="fox_bwd", grid=(h, s // tq),
        in_specs=[pl.BlockSpec((None, tq, 2 * HEAD_DIM), lambda hh, i: (hh, i, 0)),
                  pl.BlockSpec((None, s, 2 * HEAD_DIM), lambda hh, i: (hh, 0, 0)),
                  pl.BlockSpec((None, s, HEAD_DIM), lambda hh, i: (hh + v_off, 0, 0)), qblk, colspec, qblk],
        out_specs=[qblk, full, full, pl.BlockSpec((None, nk, 8, tk), lambda hh, i: (hh, 0, 0, 0)), colspec],
        out_shape=[jax.ShapeDtypeStruct((h, s, HEAD_DIM), F32)] * 3
        + [jax.ShapeDtypeStruct((h, nk, 8, tk), F32), jax.ShapeDtypeStruct((h, s, 1), F32)],
        scratch_shapes=[pltpu.VMEM((tq, HEAD_DIM), F32), pltpu.VMEM((tq, 1), F32), big, big, big, big],
        compiler_params=_cparams(("parallel", "arbitrary")),
    )(q_aug, k_aug, qkv, o, lse, d_o)


SB_TERMS = 2
G_TERMS = 1
LOG2E = 1.4426950408889634
LN2 = 0.6931471805599453


def _softplus2(z2):
    return jnp.maximum(z2, 0.0) + jnp.log2(1.0 + jnp.exp2(-jnp.abs(z2)))


def _sb_fwd(qkv, offs, tq, tk):
    h, s = N_GROUP_HEADS, qkv.shape[1]

    assert tq % (2 * tk) == 0

    def body(q_ref, k_ref, v_ref, o_ref, w_hbm, acc_ref, run_ref, z0, z1, d0, d1, t0, t1, w_stage, wsem):
        z_refs, d_refs, t_refs = (z0, z1), (d0, d1), (t0, t1)
        hh = pl.program_id(0)
        i = pl.program_id(1)
        qs = _scaled(q_ref)
        tri = _tri(tk, "ge")
        acc_ref[...] = jnp.zeros_like(acc_ref)
        run_ref[...] = jnp.zeros_like(run_ref)
        nb = (i + 1) * (tq // tk)
        ahead = _cols_minus_rows(tq, tk)

        def keys_of(b):
            j = nb - 1 - jnp.minimum(b, nb - 1)
            return pl.ds(pl.multiple_of(j * tk, tk), tk)

        def visible(b):
            return ahead < i * tq - (nb - 1 - b) * tk

        def logits(b, slot):
            z_refs[slot][...] = lax.dot_general(qs, k_ref[keys_of(b), :], _NT,
                                                preferred_element_type=F32) * LOG2E

        def sums(b, slot, masked):
            z2 = z_refs[slot][...]
            sp = _softplus2(z2)
            if masked:
                sp = jnp.where(visible(b), sp, 0.0)
            inc = _split_dot(sp, tri, SB_TERMS)
            d_refs[slot][...] = z2 - inc
            t_refs[slot][...] = inc[:, 0:1]

        def put(p, slot):
            st = (p % 2) * 2 + slot
            return pltpu.make_async_copy(w_stage.at[st], w_hbm.at[hh, i, nb - 1 - (2 * p + slot)], wsem.at[st])

        def weigh(p, slot, masked):
            b = 2 * p + slot
            w = jnp.exp2(d_refs[slot][...] - run_ref[...])
            if masked:
                w = jnp.where(visible(b), w, 0.0)
            wb = w.astype(BF16)
            w_stage[(p % 2) * 2 + slot] = wb
            acc_ref[...] += jnp.dot(wb, v_ref[keys_of(b), :], preferred_element_type=F32)
            run_ref[...] += t_refs[slot][...]

        def trip(p, masked):
            @pl.when(p >= 2)
            def _():
                put(p - 2, 0).wait()
                put(p - 2, 1).wait()

            b = 2 * p
            logits(b + 2, 0)
            sums(b + 1, 1, masked)
            weigh(p, 0, masked)
            logits(b + 3, 1)
            sums(b + 2, 0, masked)
            weigh(p, 1, masked)
            put(p, 0).start()
            put(p, 1).start()

        logits(0, 0)
        logits(1, 1)
        sums(0, 0, True)

        def guarded(p, carry):
            trip(p, True)
            return carry

        def plain(p, carry):
            trip(p, False)
            return carry

        lax.fori_loop(0, tq // tk // 2, guarded, 0)
        lax.fori_loop(tq // tk // 2, nb // 2, plain, 0)
        trips = nb // 2

        @pl.when(trips >= 2)
        def _():
            put(trips - 2, 0).wait()
            put(trips - 2, 1).wait()

        put(trips - 1, 0).wait()
        put(trips - 1, 1).wait()
        o_ref[...] = acc_ref[...]

    qblk, full, colspec = _head_specs(s, tq)
    return pl.pallas_call(
        body, name="sb_fwd", grid=(h, s // tq),
        in_specs=[*_qkv_specs(s, tq, offs)], out_specs=[qblk, pl.BlockSpec(memory_space=pl.ANY)],
        out_shape=[jax.ShapeDtypeStruct((h, s, HEAD_DIM), F32),
                   jax.ShapeDtypeStruct((h, s // tq, s // tk, tq, tk), BF16)],
        scratch_shapes=[pltpu.VMEM((tq, HEAD_DIM), F32), pltpu.VMEM((tq, 1), F32),
                        pltpu.VMEM((tq, tk), F32), pltpu.VMEM((tq, tk), F32),
                        pltpu.VMEM((tq, tk), F32), pltpu.VMEM((tq, tk), F32),
                        pltpu.VMEM((tq, 1), F32), pltpu.VMEM((tq, 1), F32),
                        pltpu.VMEM((4, tq, tk), BF16), pltpu.SemaphoreType.DMA((4,))],
        compiler_params=_cparams(("parallel", "parallel")),
    )(qkv, qkv, qkv)


def _sb_bwd(qkv, offs, w_saved, d_o, tq, tk):
    h, s = N_GROUP_HEADS, qkv.shape[1]

    assert tq % (2 * tk) == 0

    def body(q_ref, k_ref, v_ref, do_ref, w_hbm, dq_ref, dk_ref, dv_ref, dq_acc, grun_ref,
             z0, z1, p0, p1, w_bufs, wsem):
        z_refs, p_refs = (z0, z1), (p0, p1)
        hh = pl.program_id(0)
        i = pl.program_id(1)

        @pl.when(i == 0)
        def _():
            dk_ref[...] = jnp.zeros_like(dk_ref)
            dv_ref[...] = jnp.zeros_like(dv_ref)

        qs = _scaled(q_ref)
        dob = do_ref[...].astype(BF16)
        tri = _tri(tk, "le")
        dq_acc[...] = jnp.zeros_like(dq_acc)
        grun_ref[...] = jnp.zeros_like(grun_ref)
        nb = (i + 1) * (tq // tk)
        ahead = _cols_minus_rows(tq, tk)

        def block_of(b):
            return jnp.minimum(b, nb - 1)

        def keys_of(b):
            return pl.ds(pl.multiple_of(block_of(b) * tk, tk), tk)

        def visible(b):
            return ahead < i * tq - b * tk

        def fetch(p, slot):
            st = (p % 2) * 2 + slot
            return pltpu.make_async_copy(w_hbm.at[hh, i, block_of(2 * p + slot)], w_bufs.at[st], wsem.at[st])

        def products(b, slot):
            at = keys_of(b)
            z_refs[slot][...] = lax.dot_general(qs, k_ref[at, :], _NT, preferred_element_type=F32) * LOG2E
            p_refs[slot][...] = lax.dot_general(dob, v_ref[at, :], _NT, preferred_element_type=F32)

        def grads(p, slot, masked):
            b = 2 * p + slot
            at = keys_of(b)
            wb = w_bufs[(p % 2) * 2 + slot]
            g = wb.astype(F32) * p_refs[slot][...]
            ginc = _split_dot(g, tri, G_TERMS)
            beta = 1.0 / (1.0 + jnp.exp2(-z_refs[slot][...]))
            dz = g - beta * (grun_ref[...] + ginc)
            if masked:
                dz = jnp.where(visible(b), dz, 0.0)
            dzb = dz.astype(BF16)
            dq_acc[...] += jnp.dot(dzb, k_ref[at, :], preferred_element_type=F32)
            dk_ref[at, :] += lax.dot_general(dzb, qs, _TN, preferred_element_type=F32)
            dv_ref[at, :] += lax.dot_general(wb, dob, _TN, preferred_element_type=F32)
            grun_ref[...] += ginc[:, tk - 1:tk]

        def trip(p, masked):
            for slot in (0, 1):
                fetch(p + 1, slot).start()
            for slot in (0, 1):
                fetch(p, slot).wait()
            for slot in (0, 1):
                products(2 * p + slot + 1, 1 - slot)
                grads(p, slot, masked)

        for slot in (0, 1):
            fetch(0, slot).start()
        products(0, 0)
        n_plain = (nb - tq // tk) // 2

        def plain(p, carry):
            trip(p, False)
            return carry

        def guarded(p, carry):
            trip(p, True)
            return carry

        lax.fori_loop(0, n_plain, plain, 0)
        lax.fori_loop(n_plain, nb // 2, guarded, 0)
        for slot in (0, 1):
            fetch(nb // 2, slot).wait()
        dq_ref[...] = dq_acc[...] * Q_SCALE

    qblk, full, colspec = _head_specs(s, tq)
    big = pltpu.VMEM((tq, tk), F32)
    return pl.pallas_call(
        body, name="sb_bwd", grid=(h, s // tq),
        in_specs=[*_qkv_specs(s, tq, offs), qblk, pl.BlockSpec(memory_space=pl.ANY)], out_specs=[qblk, full, full],
        out_shape=[jax.ShapeDtypeStruct((h, s, HEAD_DIM), F32)] * 3,
        scratch_shapes=[pltpu.VMEM((tq, HEAD_DIM), F32), pltpu.VMEM((tq, 1), F32)]
        + [big] * 4 + [pltpu.VMEM((4, tq, tk), BF16), pltpu.SemaphoreType.DMA((4,))],
        compiler_params=_cparams(("parallel", "arbitrary")),
    )(qkv, qkv, qkv, d_o, w_saved)


def _sum_adamw(parts, w, m, v, name, tr=256):
    _, rows, lanes = parts.shape
    tr = _tile(rows, tr, 16)
    c_m = 1.0 - ADAM_B1 ** ADAM_STEP
    c_v = 1.0 - ADAM_B2 ** ADAM_STEP

    def body(p_ref, w_ref, m_ref, v_ref, g_ref, d_ref, nm_ref, nv_ref):
        g = p_ref[0].astype(F32)
        for j in range(1, N_DEV):
            g = g + p_ref[j].astype(F32)
        nm = ADAM_B1 * m_ref[...] + (1.0 - ADAM_B1) * g
        nv = ADAM_B2 * v_ref[...] + (1.0 - ADAM_B2) * (g * g)
        m_hat = nm / c_m
        v_hat = nv / c_v
        g_ref[...] = g
        d_ref[...] = -ADAM_LR * (m_hat / (jnp.sqrt(v_hat) + ADAM_EPS) + ADAM_WD * w_ref[...])
        nm_ref[...] = nm
        nv_ref[...] = nv

    blk = pl.BlockSpec((tr, lanes), lambda i: (i, 0))
    return pl.pallas_call(
        body, name=name, grid=(rows // tr,),
        in_specs=[pl.BlockSpec((N_DEV, tr, lanes), lambda i: (0, i, 0)), blk, blk, blk],
        out_specs=[blk] * 4, out_shape=[jax.ShapeDtypeStruct((rows, lanes), F32)] * 4,
        compiler_params=_cparams(("parallel",)),
    )(parts, w, m, v)


def kernel(x, attn_norm_g, w_in, forget_bias, fox_out_g, sb_out_g, w_out, ffn_norm_g, w_up, conv_w, conv_b, w_down, final_norm_g, loss_target, m_attn_norm_g, m_w_in, m_forget_bias, m_fox_out_g, m_sb_out_g, m_w_out, m_ffn_norm_g, m_w_up, m_conv_w, m_conv_b, m_w_down, m_final_norm_g, v_attn_norm_g, v_w_in, v_forget_bias, v_fox_out_g, v_sb_out_g, v_w_out, v_ffn_norm_g, v_w_up, v_conv_w, v_conv_b, v_w_down, v_final_norm_g):
    s = x.shape[1]
    xs = x[0]
    tq = min(ATTN_TQ, s)
    tk_fox = min(FOX_TK, s)
    tk_sb = min(SB_TK, s)
    in_shard, up_shard, out_shard, down_shard = IN_COLS // N_DEV, 2 * D_FF // N_DEV, D_MODEL // N_DEV, D_FF // N_DEV

    cw = conv_w[0]
    cw_hi = cw.astype(BF16)
    cw_lo = (cw - cw_hi.astype(F32)).astype(BF16)
    (g_in,) = _all_gather([w_in[0].astype(BF16)])
    rest = _exchange_start([w_out[0].astype(BF16), w_up[0].astype(BF16), w_down[0].astype(BF16),
                            jnp.stack([cw_hi, cw_lo])], False, "weights_rest_start")
    n_gate = QKV_W + N_GROUP_HEADS
    in_windows = _col_windows(N_DEV, in_shard, gap_at=n_gate, gap=GATE_PAD - N_GROUP_HEADS)
    up_windows = _col_windows(N_DEV, up_shard)
    w_in_p = _assemble_cols(g_in, IN_COLS_PAD, in_windows, "assemble_w_in")
    conv_b2 = conv_b.reshape(2, 1, D_FF)

    h1 = _rms_fwd(xs, attn_norm_g + rest[-1][0:1, 0:1])
    proj_h = _mm_heads(h1, w_in_p, "in_proj")
    fox_offs = (0, N_GROUP_HEADS, 2 * N_GROUP_HEADS)
    sb_first = 3 * N_GROUP_HEADS + GATE_PAD // HEAD_DIM
    sb_offs = (sb_first, sb_first + N_GROUP_HEADS, sb_first + 2 * N_GROUP_HEADS)
    f_logit = _mm_nn(h1, w_in_p[:, QKV_W:QKV_W + GATE_PAD], F32, "gate_proj")[:, :N_GROUP_HEADS]
    fv = proj_h[2 * N_GROUP_HEADS:3 * N_GROUP_HEADS]

    f_logit_h = f_logit.T.reshape(N_GROUP_HEADS, s // LANES, LANES)
    bias_h = jnp.broadcast_to(forget_bias.reshape(N_GROUP_HEADS, 1, 1), (N_GROUP_HEADS, 1, LANES))
    big_f = _forget_fwd(f_logit_h, bias_h)
    f_keys = big_f.reshape(N_GROUP_HEADS, s, 1)
    f_hi = f_keys.astype(BF16)
    f_mid = (f_keys - f_hi.astype(F32)).astype(BF16)
    f_lo = (f_keys - f_hi.astype(F32) - f_mid.astype(F32)).astype(BF16)
    spare = jnp.zeros((N_GROUP_HEADS, s, HEAD_DIM - 3), BF16)
    fk_aug = jnp.concatenate([proj_h[N_GROUP_HEADS:2 * N_GROUP_HEADS], -f_hi, -f_mid, -f_lo, spare], axis=-1)
    fq_aug = jnp.concatenate([proj_h[:N_GROUP_HEADS], jnp.full((N_GROUP_HEADS, s, 3), 8.0, BF16), spare], axis=-1)

    fv_ones = jnp.concatenate([fv, jnp.ones_like(fv)], axis=-1)
    o_fox_h, lse = _fox_fwd(fq_aug, fk_aug, fv_ones, tq, tk_fox)
    o_sb_h, sb_w = _sb_fwd(proj_h, sb_offs, min(SB_TQ, s), tk_sb)
    g_fox_h = fox_out_g.reshape(N_GROUP_HEADS, 1, HEAD_DIM)
    g_sb_h = sb_out_g.reshape(N_GROUP_HEADS, 1, HEAD_DIM)
    o_n = _group_rms_fwd(o_fox_h, o_sb_h, g_fox_h, g_sb_h)
    g_out, g_up, g_down, g_conv = _exchange_wait(rest, False, o_n, "weights_rest_wait")
    w_out_f = g_out.reshape(D_MODEL, D_MODEL)
    w_up_f = _assemble_cols(g_up, 2 * D_FF, up_windows, "assemble_w_up")
    w_down_f = g_down.reshape(D_FF, D_MODEL)
    conv_w_f = (g_conv[:, 0].astype(F32) + g_conv[:, 1].astype(F32)).transpose(1, 0, 2).reshape(3, 2 * D_FF)
    conv_w2 = conv_w_f.reshape(3, 2, D_FF).transpose(1, 0, 2)
    x1 = _mm_nn(o_n, w_out_f, F32, "out_proj", resid=xs)
    h2 = _rms_fwd(x1, ffn_norm_g)
    up = _mm_up(h2, w_up_f)
    act = _conv_gate_fwd(up, conv_w2, conv_b2)
    x2 = _mm_nn(act, w_down_f, F32, "down_proj", resid=x1, tk=1408)

    d_x2, d_x2b, dg_final, loss_part = _loss_head(x2, loss_target[0], final_norm_g.reshape(1, D_MODEL))
    d_act = _mm_nt(d_x2b, w_down_f, BF16, "d_act", tn=1408)
    dw_down = _mm_tn(act, d_x2b, "d_w_down", tm=1408)
    d_up, dcw2, dcb2 = _conv_gate_bwd(up, d_act, conv_w2, conv_b2)
    d_h2 = _mm_dup_nt(d_up, w_up_f)
    dw_up = _mm_dwup_tn(h2, d_up)
    d_x1, d_x1b, dg_ffn = _rms_bwd(x1, d_h2, ffn_norm_g, d_x2, dy_col=0, name="ffn_norm_bwd", want_bf16=True)
    d_on = _mm_nt(d_x1b, w_out_f, F32, "d_o_normed")
    dw_out = _mm_tn(o_n, d_x1b, "d_w_out")
    early = _exchange_start(
        [dw_out.astype(BF16).reshape(N_DEV, out_shard, D_MODEL),
         _split_cols(dw_up, N_DEV, up_shard, up_windows, "split_d_w_up"),
         dw_down.astype(BF16).reshape(N_DEV, down_shard, D_MODEL)], True, "grads_early_start")
    g_fox_t = g_fox_h + early[-1][0:1, 0:1]
    d_o_fox_h, dg_fox = _group_rms_bwd(o_fox_h, d_on, g_fox_t, dy_col=0, name="fox_norm_bwd")
    d_o_sb_h, dg_sb = _group_rms_bwd(o_sb_h, d_on, g_sb_h, dy_col=1, name="sb_norm_bwd")

    dfq, dfk, dfv, ksum8, qsum = _fox_bwd(fq_aug, fk_aug, proj_h, fox_offs[2], o_fox_h, lse, d_o_fox_h, tq, tk_fox)
    dsq, dsk, dsv = _sb_bwd(proj_h, sb_offs, sb_w, d_o_sb_h, min(SB_TQ, s), tk_sb)
    ksum = jnp.sum(ksum8, axis=2).reshape(N_GROUP_HEADS, s // LANES, LANES)
    d_f_logit_h, d_bias_h = _forget_bwd(f_logit_h, bias_h, ksum,
                                        qsum.reshape(N_GROUP_HEADS, s // LANES, LANES))
    d_f_logit = d_f_logit_h.reshape(N_GROUP_HEADS, s).T

    d_proj = _merge_dproj((dfq, dfk, dfv), d_f_logit, (dsq, dsk, dsv))
    dw_in_p = _mm_tn(h1, d_proj, "d_w_in", tn=640)
    dconv_w = dcw2.transpose(1, 0, 2).reshape(3, 2 * D_FF)
    dconv_b = dcb2.reshape(1, 2 * D_FF)
    late = _exchange_start(
        [_split_cols(dw_in_p, N_DEV, in_shard, in_windows, "split_d_w_in"),
         dconv_w.astype(BF16).reshape(3, N_DEV, up_shard).transpose(1, 0, 2)],
        True, "grads_late_start")
    d_h1 = _mm_nt(d_proj, w_in_p + late[-1][0:1, 0:1].astype(BF16), F32, "d_h1", tk=640)
    grad_x, dg_attn = _rms_bwd(xs, d_h1, attn_norm_g, d_x1, dy_col=0, name="attn_norm_bwd", want_bf16=False)

    small_shapes = [(1, D_MODEL), (1, N_GROUP_HEADS), (1, GROUP_W), (1, GROUP_W), (1, D_MODEL),
                    (1, 2 * D_FF), (D_MODEL,), (1,)]
    spack = _pack([dg_attn, d_bias_h[:, 0, 0], dg_fox, dg_sb, dg_ffn, dconv_b, dg_final, loss_part[0, 0:1]],
                  SMALL_ROWS, F32)
    (srecv,) = _grad_exchange([], spack)
    r_out, r_up, r_down = _exchange_wait(early, True, srecv, "grads_early_wait")
    r_in, r_conv = _exchange_wait(late, True, r_out, "grads_late_wait")

    big = [_sum_adamw(g, w_[0], m_[0], v_[0], "adamw_" + tag)
           for g, w_, m_, v_, tag in zip(
               (r_in, r_out, r_up, r_down, r_conv), (w_in, w_out, w_up, w_down, conv_w), (m_w_in, m_w_out, m_w_up, m_w_down, m_conv_w),
               (v_w_in, v_w_out, v_w_up, v_w_down, v_conv_w), ("w_in", "w_out", "w_up", "w_down", "conv_w"))]

    def small_pack(a_attn, a_bias, a_fox, a_sb, a_ffn, a_cb, a_fin):
        return _pack([a_attn, a_bias, a_fox, a_sb, a_ffn, a_cb, a_fin, jnp.zeros((1,), F32)], SMALL_ROWS, F32)

    small = _sum_adamw(srecv, small_pack(attn_norm_g, forget_bias, fox_out_g, sb_out_g, ffn_norm_g, conv_b, final_norm_g),
                       small_pack(m_attn_norm_g, m_forget_bias, m_fox_out_g, m_sb_out_g, m_ffn_norm_g, m_conv_b, m_final_norm_g),
                       small_pack(v_attn_norm_g, v_forget_bias, v_fox_out_g, v_sb_out_g, v_ffn_norm_g, v_conv_b, v_final_norm_g),
                       "adamw_replicated", tr=SMALL_ROWS)

    outs = []
    loss = None
    for kind in range(4):
        b_in, b_out, b_up, b_down, b_conv = (res[kind] for res in big)
        s_attn, s_bias, s_fox, s_sb, s_ffn, s_cb, s_fin, s_loss = _unpack(small[kind], small_shapes)
        if kind == 0:
            loss = s_loss[0]
        outs += [s_attn, b_in[None], s_bias, s_fox, s_sb, b_out[None], s_ffn, b_up[None], b_conv[None], s_cb,
                 b_down[None], s_fin]
    return (loss, grad_x[None], *outs)
```

```python
import jax
import jax.numpy as jnp
from jax import lax
from jax.experimental import pallas as pl
from jax.experimental.pallas import tpu as pltpu

F32 = jnp.float32
BF16 = jnp.bfloat16

D_MODEL = 1024
HEAD_DIM = 64
N_GROUP_HEADS = 8
GROUP_W = N_GROUP_HEADS * HEAD_DIM
QKV_W = 3 * GROUP_W
IN_COLS = 2 * QKV_W + N_GROUP_HEADS
GATE_PAD = 128
IN_COLS_PAD = 2 * QKV_W + GATE_PAD
D_FF = 2816
N_DEV = 8
EPS = 1e-6
Q_SCALE = HEAD_DIM ** -0.5

ADAM_LR = 0.001
ADAM_B1 = 0.9
ADAM_B2 = 0.999
ADAM_EPS = 1e-08
ADAM_WD = 0.01
ADAM_STEP = 10

LANES = 128
SMALL_ROWS = 80
VMEM_LIMIT = 56 * 1024 * 1024
NEG_BIG = -1e30
ATTN_TQ = 512
SB_TQ = 512
FOX_TK = 512
SB_TK = 256
MESH = pl.DeviceIdType.MESH


def _cparams(sem=None, **kw):
    return pltpu.CompilerParams(dimension_semantics=sem, vmem_limit_bytes=VMEM_LIMIT, **kw)


def _tile(n, target, mult=LANES):
    if n <= target:
        return n
    t = (target // mult) * mult
    while t >= mult:
        if n % t == 0:
            return t
        t -= mult
    return n


def _seg_len(shape):
    n = 1
    for s in shape:
        n *= s
    return -(-n // LANES) * LANES


def _pack(arrs, rows, dtype):
    parts = []
    for a in arrs:
        f = a.reshape(-1).astype(dtype)
        parts.append(jnp.pad(f, (0, _seg_len(a.shape) - f.shape[0])))
    flat = jnp.concatenate(parts)
    flat = jnp.pad(flat, (0, rows * LANES - flat.shape[0]))
    return flat.reshape(rows, LANES)


def _unpack(p, shapes, lead=()):
    flat = p.reshape(lead + (-1,))
    out, off = [], 0
    for shp in shapes:
        n = 1
        for s in shp:
            n *= s
        out.append(flat[..., off:off + n].reshape(lead + tuple(shp)))
        off += _seg_len(shp)
    return out


def _my_pos():
    return lax.axis_index("x"), lax.axis_index("y"), lax.axis_index("c")


def _all_gather(blocks):
    n = len(blocks)

    def body(*refs):
        x_refs, out_refs = refs[:n], refs[n:2 * n]
        send_sems, recv_sems, local_sems = refs[2 * n:]
        x, y, c = _my_pos()
        me, sibling = (x, y, c), (x, y, 1 - c)
        chips = [(1 - x, y), (x, 1 - y), (1 - x, 1 - y)]

        def copy(a, k, blk, to, own=False):
            px, py, pc = blk
            slot = out_refs[a].at[4 * px + 2 * py + pc]
            return pltpu.make_async_remote_copy(
                src_ref=x_refs[a] if own else slot, dst_ref=slot,
                send_sem=send_sems.at[a, k], recv_sem=recv_sems.at[a, k],
                device_id=to, device_id_type=MESH)

        mine = [pltpu.make_async_copy(x_refs[a], out_refs[a].at[4 * x + 2 * y + c], local_sems.at[a])
                for a in range(n)]
        for cp in mine:
            cp.start()
        first = []
        for a in range(n):
            first.append(copy(a, 0, me, sibling, own=True))
            first += [copy(a, 1 + j, me, (*chip, c), own=True) for j, chip in enumerate(chips)]
        for cp in first:
            cp.start()
        passed = []
        for j, chip in enumerate(chips):
            for a in range(n):
                copy(a, 1 + j, (*chip, c), me).wait_recv()
                passed.append(copy(a, 4 + j, (*chip, c), sibling))
                passed[-1].start()
        for a in range(n):
            copy(a, 0, sibling, me).wait_recv()
            for j, chip in enumerate(chips):
                copy(a, 4 + j, (*chip, 1 - c), me).wait_recv()
        for cp in first + passed:
            cp.wait_send()
        for cp in mine:
            cp.wait()

    hbm = pl.BlockSpec(memory_space=pl.ANY)
    return pl.pallas_call(
        body, name="weights_all_gather",
        out_shape=[jax.ShapeDtypeStruct((N_DEV,) + b.shape, b.dtype) for b in blocks],
        in_specs=[hbm] * n, out_specs=[hbm] * n,
        scratch_shapes=[pltpu.SemaphoreType.DMA((n, 7)), pltpu.SemaphoreType.DMA((n, 7)),
                        pltpu.SemaphoreType.DMA((n,))],
    )(*blocks)


def _grad_exchange(slabs, spack):
    n = len(slabs) + 1

    def body(*refs):
        in_refs, out_refs = refs[:n], refs[n:2 * n]
        send_sems, recv_sems, local_sems = refs[2 * n:]
        x, y, c = _my_pos()
        my_id = 4 * x + 2 * y + c

        def src_of(a, dev):
            return in_refs[a] if a == n - 1 else in_refs[a].at[dev]

        own = [pltpu.make_async_copy(src_of(a, my_id), out_refs[a].at[my_id], local_sems.at[a])
               for a in range(n)]
        for cp in own:
            cp.start()
        sends, arrivals = [], []
        for k in range(1, N_DEV):
            px, py, pc = x ^ (k >> 2), y ^ ((k >> 1) & 1), c ^ (k & 1)
            peer_id = 4 * px + 2 * py + pc
            for a in range(n):
                for dst_slot, bucket in ((my_id, sends), (peer_id, arrivals)):
                    bucket.append(pltpu.make_async_remote_copy(
                        src_ref=src_of(a, peer_id), dst_ref=out_refs[a].at[dst_slot],
                        send_sem=send_sems.at[a, k - 1], recv_sem=recv_sems.at[a, k - 1],
                        device_id=(px, py, pc), device_id_type=MESH))
        for cp in sends:
            cp.start()
        for cp in arrivals:
            cp.wait_recv()
        for cp in sends:
            cp.wait_send()
        for cp in own:
            cp.wait()

    hbm = pl.BlockSpec(memory_space=pl.ANY)
    return pl.pallas_call(
        body, name="grad_exchange",
        out_shape=[jax.ShapeDtypeStruct(g.shape, g.dtype) for g in slabs]
        + [jax.ShapeDtypeStruct((N_DEV,) + spack.shape, spack.dtype)],
        in_specs=[hbm] * n, out_specs=[hbm] * n,
        scratch_shapes=[pltpu.SemaphoreType.DMA((n, 7)), pltpu.SemaphoreType.DMA((n, 7)),
                        pltpu.SemaphoreType.DMA((n,))],
    )(*slabs, spack)


_HBM = pl.BlockSpec(memory_space=pltpu.HBM)
_SEM = pl.BlockSpec(memory_space=pltpu.SEMAPHORE)
_EFFECT = pltpu.SideEffectType.DATAFLOW_SIDE_EFFECTING


def _my_id():
    x, y, c = _my_pos()
    return 4 * x + 2 * y + c


def _peer_copies(src_refs, land_refs, send_sems, recv_sems, per_peer):
    x, y, c = _my_pos()
    my_id = 4 * x + 2 * y + c
    copies = []
    for k in range(1, N_DEV):
        px, py, pc = x ^ (k >> 2), y ^ ((k >> 1) & 1), c ^ (k & 1)
        for a, (src, land) in enumerate(zip(src_refs, land_refs)):
            copies.append(pltpu.make_async_remote_copy(
                src_ref=src.at[4 * px + 2 * py + pc] if per_peer else src, dst_ref=land.at[my_id],
                send_sem=send_sems.at[a * (N_DEV - 1) + k - 1], recv_sem=recv_sems.at[a * (N_DEV - 1) + k - 1],
                device_id=(px, py, pc), device_id_type=MESH))
    return copies


def _exchange_start(srcs, per_peer, name):
    n = len(srcs)
    lands = [lax.empty(s.shape if per_peer else (N_DEV,) + s.shape, s.dtype) for s in srcs]

    def body(*refs):
        src_refs, land_refs = refs[:n], refs[n:2 * n]
        send_sems, recv_sems = refs[2 * n], refs[2 * n + 1]
        token = refs[-1]
        for cp in _peer_copies(src_refs, land_refs, send_sems, recv_sems, per_peer):
            cp.start()
        token[...] = jnp.zeros_like(token)

    outs = pl.pallas_call(
        body, name=name,
        out_shape=(pltpu.SemaphoreType.DMA((n * (N_DEV - 1),)), pltpu.SemaphoreType.DMA((n * (N_DEV - 1),)),
                   *[pltpu.HBM(a.shape, a.dtype) for a in srcs + lands],
                   jax.ShapeDtypeStruct((8, LANES), F32)),
        in_specs=[_HBM] * (2 * n),
        out_specs=(_SEM, _SEM, *[_HBM] * (2 * n), pl.BlockSpec(memory_space=pltpu.VMEM)),
        input_output_aliases={a: 2 + a for a in range(2 * n)},
        compiler_params=pltpu.CompilerParams(has_side_effects=_EFFECT),
    )(*[pltpu.with_memory_space_constraint(a, pltpu.HBM) for a in srcs + lands])
    return outs[0], outs[1], list(outs[2:2 + n]), list(outs[2 + n:2 + 2 * n]), outs[-1]


def _exchange_wait(handles, per_peer, after, name):
    send_sems, recv_sems, srcs, lands, _ = handles
    n = len(srcs)

    def body(*refs):
        src_refs, land_refs = refs[:n], refs[n:2 * n]
        for cp in _peer_copies(src_refs, land_refs, refs[2 * n], refs[2 * n + 1], per_peer):
            cp.wait_send()
            cp.wait_recv()

    outs = pl.pallas_call(
        body, name=name,
        out_shape=tuple(pltpu.HBM(a.shape, a.dtype) for a in srcs + lands),
        in_specs=[_HBM] * (2 * n) + [_SEM, _SEM, pl.BlockSpec(memory_space=pl.ANY)],
        out_specs=tuple([_HBM] * (2 * n)),
        input_output_aliases={a: a for a in range(2 * n)},
        compiler_params=pltpu.CompilerParams(has_side_effects=_EFFECT),
    )(*srcs, *lands, send_sems, recv_sems, after)
    me = _my_id()
    filled = []
    for src, land in zip(outs[:n], outs[n:]):
        own = lax.dynamic_index_in_dim(src, me, 0, keepdims=True) if per_peer else src[None]
        filled.append(lax.dynamic_update_slice_in_dim(land, own, me, 0))
    return filled


def _col_windows(n_shards, width, gap_at=None, gap=0):
    out = []
    for j in range(n_shards):
        g0, g1 = j * width, (j + 1) * width
        cuts = [g0, g1] if gap_at is None or not g0 < gap_at < g1 else [g0, gap_at, g1]
        for a, b in zip(cuts[:-1], cuts[1:]):
            out.append((j, a - g0, b - g0, a + (gap if gap_at is not None and a >= gap_at else 0)))
    return out


def _assemble_cols(parts, total, windows, name, tr=256):
    n, rows, w = parts.shape
    tr = _tile(rows, tr, 16)

    def body(p_ref, o_ref):
        o_ref[...] = jnp.zeros_like(o_ref)
        for j, lo, hi, dst in windows:
            o_ref[:, dst:dst + hi - lo] = p_ref[j, :, lo:hi]

    return pl.pallas_call(
        body, name=name, grid=(rows // tr,),
        in_specs=[pl.BlockSpec((n, tr, w), lambda i: (0, i, 0))],
        out_specs=pl.BlockSpec((tr, total), lambda i: (i, 0)),
        out_shape=jax.ShapeDtypeStruct((rows, total), parts.dtype),
        compiler_params=_cparams(("parallel",)),
    )(parts)


def _split_cols(full, n, w, windows, name, tr=256):
    rows, total = full.shape
    tr = _tile(rows, tr, 16)

    def body(f_ref, o_ref):
        for j, lo, hi, dst in windows:
            o_ref[j, :, lo:hi] = f_ref[:, dst:dst + hi - lo].astype(o_ref.dtype)

    return pl.pallas_call(
        body, name=name, grid=(rows // tr,),
        in_specs=[pl.BlockSpec((tr, total), lambda i: (i, 0))],
        out_specs=pl.BlockSpec((n, tr, w), lambda i: (0, i, 0)),
        out_shape=jax.ShapeDtypeStruct((n, rows, w), BF16),
        compiler_params=_cparams(("parallel",)),
    )(full)


_DIMS = {"nn": (((1,), (0,)), ((), ())), "nt": (((1,), (1,)), ((), ())), "tn": (((0,), (0,)), ((), ()))}


def _matmul(a, b, *, mode, grid, a_block, a_map, b_block, b_map, o_block, o_map, out_shape, name,
            resid=None):
    nk = grid[2]
    dims = _DIMS[mode]

    def body(*refs):
        if resid is None:
            a_ref, b_ref, o_ref, acc_ref = refs
            r_ref = None
        else:
            a_ref, b_ref, r_ref, o_ref, acc_ref = refs
        k = pl.program_id(2)

        @pl.when(k == 0)
        def _():
            acc_ref[...] = jnp.zeros_like(acc_ref)

        acc_ref[...] += lax.dot_general(a_ref[...], b_ref[...], dims, preferred_element_type=F32)

        @pl.when(k == nk - 1)
        def _():
            res = acc_ref[...]
            if r_ref is not None:
                res = r_ref[...] + res
            o_ref[...] = res.astype(o_ref.dtype)

    in_specs = [pl.BlockSpec(a_block, a_map), pl.BlockSpec(b_block, b_map)]
    args = [a, b]
    if resid is not None:
        in_specs.append(pl.BlockSpec(o_block, o_map))
        args.append(resid)
    acc_shape = tuple(d for d in o_block if d is not None)
    return pl.pallas_call(
        body, name=name, grid=grid, in_specs=in_specs,
        out_specs=pl.BlockSpec(o_block, o_map), out_shape=out_shape,
        scratch_shapes=[pltpu.VMEM(acc_shape, F32)],
        compiler_params=_cparams(("parallel", "parallel", "arbitrary")),
    )(*args)


def _mm_nn(a, b, out_dtype, name, resid=None, tm=1024, tn=1024, tk=1024):
    m, kk = a.shape
    n = b.shape[1]
    tm, tn, tk = _tile(m, tm, 8), _tile(n, tn), _tile(kk, tk)
    return _matmul(a, b, mode="nn", grid=(m // tm, n // tn, kk // tk),
                   a_block=(tm, tk), a_map=lambda i, j, k: (i, k),
                   b_block=(tk, tn), b_map=lambda i, j, k: (k, j),
                   o_block=(tm, tn), o_map=lambda i, j, k: (i, j),
                   out_shape=jax.ShapeDtypeStruct((m, n), out_dtype), name=name, resid=resid)


def _mm_nt(a, b, out_dtype, name, tm=1024, tn=1024, tk=1024):
    m, kk = a.shape
    n = b.shape[0]
    tm, tn, tk = _tile(m, tm, 8), _tile(n, tn), _tile(kk, tk)
    return _matmul(a, b, mode="nt", grid=(m // tm, n // tn, kk // tk),
                   a_block=(tm, tk), a_map=lambda i, j, k: (i, k),
                   b_block=(tn, tk), b_map=lambda i, j, k: (j, k),
                   o_block=(tm, tn), o_map=lambda i, j, k: (i, j),
                   out_shape=jax.ShapeDtypeStruct((m, n), out_dtype), name=name)


def _mm_tn(a, b, name, tm=1024, tn=1024, tk=1024):
    kk, m = a.shape
    n = b.shape[1]
    tm, tn, tk = _tile(m, tm), _tile(n, tn), _tile(kk, tk, 8)
    return _matmul(a, b, mode="tn", grid=(m // tm, n // tn, kk // tk),
                   a_block=(tk, tm), a_map=lambda i, j, k: (k, i),
                   b_block=(tk, tn), b_map=lambda i, j, k: (k, j),
                   o_block=(tm, tn), o_map=lambda i, j, k: (i, j),
                   out_shape=jax.ShapeDtypeStruct((m, n), F32), name=name)


def _mm_heads(a, b, name, tm=1024, tn=640):
    m, kk = a.shape
    n = b.shape[1]
    tm, tn = _tile(m, tm, 16), _tile(n, tn)
    per_tile = tn // HEAD_DIM

    def body(a_ref, b_ref, o_ref):
        res = jnp.dot(a_ref[...], b_ref[...], preferred_element_type=F32)
        for hh in range(per_tile):
            o_ref[hh] = res[:, hh * HEAD_DIM:(hh + 1) * HEAD_DIM].astype(o_ref.dtype)

    return pl.pallas_call(
        body, name=name, grid=(m // tm, n // tn),
        in_specs=[pl.BlockSpec((tm, kk), lambda i, j: (i, 0)), pl.BlockSpec((kk, tn), lambda i, j: (0, j))],
        out_specs=pl.BlockSpec((per_tile, tm, HEAD_DIM), lambda i, j: (j, i, 0)),
        out_shape=jax.ShapeDtypeStruct((n // HEAD_DIM, m, HEAD_DIM), BF16),
        compiler_params=_cparams(("parallel", "parallel")),
    )(a, b)


def _mm_up(h, w_up, tm=2048, tn=256):
    s = h.shape[0]
    tm = _tile(s, tm, 8)
    nh = D_FF // tn
    return _matmul(h, w_up, mode="nn", grid=(s // tm, 2 * nh, 1),
                   a_block=(tm, D_MODEL), a_map=lambda i, j, k: (i, 0),
                   b_block=(D_MODEL, tn), b_map=lambda i, j, k: (0, j),
                   o_block=(None, tm, tn), o_map=lambda i, j, k: (j // nh, i, j % nh),
                   out_shape=jax.ShapeDtypeStruct((2, s, D_FF), F32), name="up_proj")


def _mm_dup_nt(dup, w_up, tm=1024, tk=1408):
    s = dup.shape[1]
    tm = _tile(s, tm, 8)
    nh = D_FF // tk
    return _matmul(dup, w_up, mode="nt", grid=(s // tm, 1, 2 * nh),
                   a_block=(None, tm, tk), a_map=lambda i, j, k: (k // nh, i, k % nh),
                   b_block=(D_MODEL, tk), b_map=lambda i, j, k: (0, k),
                   o_block=(tm, D_MODEL), o_map=lambda i, j, k: (i, 0),
                   out_shape=jax.ShapeDtypeStruct((s, D_MODEL), F32), name="d_h2")


def _mm_dwup_tn(h, dup, tn=1408, tk=1024):
    s = h.shape[0]
    tk = _tile(s, tk, 8)
    nh = D_FF // tn
    return _matmul(h, dup, mode="tn", grid=(1, 2 * nh, s // tk),
                   a_block=(tk, D_MODEL), a_map=lambda i, j, k: (k, 0),
                   b_block=(None, tk, tn), b_map=lambda i, j, k: (j // nh, k, j % nh),
                   o_block=(D_MODEL, tn), o_map=lambda i, j, k: (0, j),
                   out_shape=jax.ShapeDtypeStruct((D_MODEL, 2 * D_FF), F32), name="d_w_up")


def _rms_fwd(x, g, tr=512):
    s, d = x.shape
    tr = _tile(s, tr, 8)

    def body(x_ref, g_ref, o_ref):
        xv = x_ref[...]
        r = lax.rsqrt(jnp.mean(xv * xv, axis=-1, keepdims=True) + EPS)
        o_ref[...] = (xv * r * g_ref[...]).astype(o_ref.dtype)

    return pl.pallas_call(
        body, name="rms_fwd", grid=(s // tr,),
        in_specs=[pl.BlockSpec((tr, d), lambda i: (i, 0)), pl.BlockSpec((1, d), lambda i: (0, 0))],
        out_specs=pl.BlockSpec((tr, d), lambda i: (i, 0)),
        out_shape=jax.ShapeDtypeStruct((s, d), BF16),
        compiler_params=_cparams(("parallel",)),
    )(x, g)


def _group_rms_fwd(o_fox, o_sb, g_fox, g_sb, tr=512):
    nh, s, dh = o_fox.shape
    tr = _tile(s, tr, 8)

    def body(a_ref, b_ref, ga_ref, gb_ref, o_ref):
        for src, g_ref, lo in ((a_ref, ga_ref, 0), (b_ref, gb_ref, nh * dh)):
            heads = [src[hh] for hh in range(nh)]
            ss = heads[0] * heads[0]
            for xv in heads[1:]:
                ss = ss + xv * xv
            r = lax.rsqrt(jnp.sum(ss, axis=-1, keepdims=True) * (1.0 / (nh * dh)) + EPS)
            for hh, xv in enumerate(heads):
                o_ref[:, lo + hh * dh:lo + (hh + 1) * dh] = (xv * r * g_ref[hh]).astype(o_ref.dtype)

    heads_blk = pl.BlockSpec((nh, tr, dh), lambda i: (0, i, 0))
    gain = pl.BlockSpec((nh, 1, dh), lambda i: (0, 0, 0))
    return pl.pallas_call(
        body, name="group_rms_fwd", grid=(s // tr,),
        in_specs=[heads_blk, heads_blk, gain, gain],
        out_specs=pl.BlockSpec((tr, 2 * nh * dh), lambda i: (i, 0)),
        out_shape=jax.ShapeDtypeStruct((s, 2 * nh * dh), BF16),
        compiler_params=_cparams(("parallel",)),
    )(o_fox, o_sb, g_fox, g_sb)


def _group_rms_bwd(x, dy, g, *, dy_col, name, tr=512):
    nh, s, dh = x.shape
    tr = _tile(s, tr, 8)
    d = nh * dh

    def body(x_ref, dy_ref, g_ref, dx_ref, dg_ref):
        @pl.when(pl.program_id(0) == 0)
        def _():
            dg_ref[...] = jnp.zeros_like(dg_ref)

        dyv = dy_ref[...]
        xs_ = [x_ref[hh] for hh in range(nh)]
        dys = [dyv[:, hh * dh:(hh + 1) * dh] for hh in range(nh)]
        ss = xs_[0] * xs_[0]
        for xv in xs_[1:]:
            ss = ss + xv * xv
        r = lax.rsqrt(jnp.sum(ss, axis=-1, keepdims=True) * (1.0 / d) + EPS)
        xh = [xv * r for xv in xs_]
        gy = [dys[hh] * g_ref[hh] for hh in range(nh)]
        dot = xh[0] * gy[0]
        for hh in range(1, nh):
            dot = dot + xh[hh] * gy[hh]
        mean_dot = jnp.sum(dot, axis=-1, keepdims=True) * (1.0 / d)
        for hh in range(nh):
            dx_ref[hh] = r * (gy[hh] - xh[hh] * mean_dot)
            dg_ref[hh] += jnp.sum(dys[hh] * xh[hh], axis=0, keepdims=True)

    heads_blk = pl.BlockSpec((nh, tr, dh), lambda i: (0, i, 0))
    gain = pl.BlockSpec((nh, 1, dh), lambda i: (0, 0, 0))
    return pl.pallas_call(
        body, name=name, grid=(s // tr,),
        in_specs=[heads_blk, pl.BlockSpec((tr, d), lambda i: (i, dy_col)), gain],
        out_specs=[heads_blk, gain],
        out_shape=[jax.ShapeDtypeStruct((nh, s, dh), F32), jax.ShapeDtypeStruct((nh, 1, dh), F32)],
        compiler_params=_cparams(("arbitrary",)),
    )(x, dy, g)


def _merge_dproj(parts_fox, d_gate, parts_sb, tr=256):
    nh, s, dh = parts_fox[0].shape
    tr = _tile(s, tr, 16)

    def body(*refs):
        o_ref = refs[-1]
        gate_ref = refs[3]
        col = 0
        for ref in refs[:3]:
            for hh in range(nh):
                o_ref[:, col:col + dh] = ref[hh].astype(o_ref.dtype)
                col += dh
        o_ref[:, col:col + GATE_PAD] = jnp.zeros((tr, GATE_PAD), o_ref.dtype)
        o_ref[:, col:col + N_GROUP_HEADS] = gate_ref[...].astype(o_ref.dtype)
        col += GATE_PAD
        for ref in refs[4:7]:
            for hh in range(nh):
                o_ref[:, col:col + dh] = ref[hh].astype(o_ref.dtype)
                col += dh

    heads_blk = pl.BlockSpec((nh, tr, dh), lambda i: (0, i, 0))
    return pl.pallas_call(
        body, name="merge_d_proj", grid=(s // tr,),
        in_specs=[heads_blk] * 3 + [pl.BlockSpec((tr, N_GROUP_HEADS), lambda i: (i, 0))] + [heads_blk] * 3,
        out_specs=pl.BlockSpec((tr, IN_COLS_PAD), lambda i: (i, 0)),
        out_shape=jax.ShapeDtypeStruct((s, IN_COLS_PAD), BF16),
        compiler_params=_cparams(("parallel",)),
    )(*parts_fox, d_gate, *parts_sb)


def _rms_bwd(x, dy, g, resid, *, dy_col, name, want_bf16, tr=512):
    s, d = x.shape
    tr = _tile(s, tr, 8)
    has_resid = resid is not None

    def body(*refs):
        refs = list(refs)
        x_ref, dy_ref, g_ref = refs[:3]
        r_ref = refs[3] if has_resid else None
        outs = refs[4:] if has_resid else refs[3:]
        dx_ref = outs[0]
        dxb_ref = outs[1] if want_bf16 else None
        dg_ref = outs[-1]

        @pl.when(pl.program_id(0) == 0)
        def _():
            dg_ref[...] = jnp.zeros_like(dg_ref)

        xv = x_ref[...]
        dyv = dy_ref[...]
        r = lax.rsqrt(jnp.mean(xv * xv, axis=-1, keepdims=True) + EPS)
        xh = xv * r
        gy = dyv * g_ref[...]
        dx = r * (gy - xh * jnp.mean(xh * gy, axis=-1, keepdims=True))
        if r_ref is not None:
            dx = r_ref[...] + dx
        dx_ref[...] = dx
        if dxb_ref is not None:
            dxb_ref[...] = dx.astype(BF16)
        dg_ref[...] += jnp.sum(dyv * xh, axis=0, keepdims=True)

    row = pl.BlockSpec((tr, d), lambda i: (i, 0))
    in_specs = [row, pl.BlockSpec((tr, d), lambda i: (i, dy_col)), pl.BlockSpec((1, d), lambda i: (0, 0))]
    args = [x, dy, g]
    if has_resid:
        in_specs.append(row)
        args.append(resid)
    out_specs = [row]
    out_shape = [jax.ShapeDtypeStruct((s, d), F32)]
    if want_bf16:
        out_specs.append(row)
        out_shape.append(jax.ShapeDtypeStruct((s, d), BF16))
    out_specs.append(pl.BlockSpec((1, d), lambda i: (0, 0)))
    out_shape.append(jax.ShapeDtypeStruct((1, d), F32))
    return pl.pallas_call(
        body, name=name, grid=(s // tr,), in_specs=in_specs, out_specs=out_specs, out_shape=out_shape,
        compiler_params=_cparams(("arbitrary",)),
    )(*args)


def _loss_head(x2, target, g, tr=512):
    s, d = x2.shape
    tr = _tile(s, tr, 8)

    def body(x_ref, t_ref, g_ref, dx_ref, dxb_ref, dg_ref, loss_ref):
        @pl.when(pl.program_id(0) == 0)
        def _():
            dg_ref[...] = jnp.zeros_like(dg_ref)
            loss_ref[...] = jnp.zeros_like(loss_ref)

        xv = x_ref[...]
        gv = g_ref[...]
        r = lax.rsqrt(jnp.mean(xv * xv, axis=-1, keepdims=True) + EPS)
        xh = xv * r
        err = xh * gv - t_ref[...]
        loss_ref[...] += jnp.sum(jnp.mean(err * err, axis=-1, keepdims=True), axis=0, keepdims=True) * 0.5
        dyv = err * (1.0 / d)
        gy = dyv * gv
        dx = r * (gy - xh * jnp.mean(xh * gy, axis=-1, keepdims=True))
        dx_ref[...] = dx
        dxb_ref[...] = dx.astype(BF16)
        dg_ref[...] += jnp.sum(dyv * xh, axis=0, keepdims=True)

    row = pl.BlockSpec((tr, d), lambda i: (i, 0))
    return pl.pallas_call(
        body, name="loss_head", grid=(s // tr,),
        in_specs=[row, row, pl.BlockSpec((1, d), lambda i: (0, 0))],
        out_specs=[row, row, pl.BlockSpec((1, d), lambda i: (0, 0)), pl.BlockSpec((1, LANES), lambda i: (0, 0))],
        out_shape=[jax.ShapeDtypeStruct((s, d), F32), jax.ShapeDtypeStruct((s, d), BF16),
                   jax.ShapeDtypeStruct((1, d), F32), jax.ShapeDtypeStruct((1, LANES), F32)],
        compiler_params=_cparams(("arbitrary",)),
    )(x2, target, g)


def _conv_taps(cur, prev8, w, b, first):
    prev8 = jnp.where(first, 0.0, prev8)
    ext = jnp.concatenate([prev8, cur], axis=0)
    x1 = pltpu.roll(ext, 1, 0)[8:]
    x2 = pltpu.roll(ext, 2, 0)[8:]
    u = b + w[0:1] * x2
    u = u + w[1:2] * x1
    u = u + w[2:3] * cur
    return u, x1, x2


def _conv_gate_fwd(up, conv_w, conv_b, tm=2048, tn=256):
    s = up.shape[1]
    tm = _tile(s, tm, 8)
    nrb = s // tm
    rb8 = tm // 8

    def body(g_ref, v_ref, gp_ref, vp_ref, wg_ref, wv_ref, bg_ref, bv_ref, o_ref):
        first = pl.program_id(1) == 0
        ug, _, _ = _conv_taps(g_ref[...], gp_ref[...], wg_ref[...], bg_ref[...], first)
        uv, _, _ = _conv_taps(v_ref[...], vp_ref[...], wv_ref[...], bv_ref[...], first)
        sg = 1.0 / (1.0 + jnp.exp(-ug))
        o_ref[...] = (ug * sg * uv).astype(o_ref.dtype)

    def cur(h):
        return pl.BlockSpec((None, tm, tn), lambda j, i: (h, i, j))

    def prev(h):
        return pl.BlockSpec((None, 8, tn), lambda j, i: (h, jnp.maximum(i * rb8 - 1, 0), j))

    def par(h, r):
        return pl.BlockSpec((None, r, tn), lambda j, i: (h, 0, j))

    return pl.pallas_call(
        body, name="conv_gate_fwd", grid=(D_FF // tn, nrb),
        in_specs=[cur(0), cur(1), prev(0), prev(1), par(0, 3), par(1, 3), par(0, 1), par(1, 1)],
        out_specs=pl.BlockSpec((tm, tn), lambda j, i: (i, j)),
        out_shape=jax.ShapeDtypeStruct((s, D_FF), BF16),
        compiler_params=_cparams(("parallel", "parallel")),
    )(up, up, up, up, conv_w, conv_w, conv_b, conv_b)


def _conv_gate_bwd(up, dact, conv_w, conv_b, tm=1024, tn=256):
    s = up.shape[1]
    tm = _tile(s, tm, 8)
    nrb = s // tm
    rb8 = tm // 8

    def body(g_ref, v_ref, gp_ref, vp_ref, da_ref, wg_ref, wv_ref, bg_ref, bv_ref,
             dup_ref, dcw_ref, dcb_ref, carry_ref):
        i = pl.program_id(1)
        first = i == nrb - 1

        @pl.when(i == 0)
        def _():
            carry_ref[...] = jnp.zeros_like(carry_ref)
            dcw_ref[...] = jnp.zeros_like(dcw_ref)
            dcb_ref[...] = jnp.zeros_like(dcb_ref)

        curs = (g_ref[...], v_ref[...])
        ws = (wg_ref[...], wv_ref[...])
        ug, g1, g2 = _conv_taps(curs[0], gp_ref[...], ws[0], bg_ref[...], first)
        uv, v1, v2 = _conv_taps(curs[1], vp_ref[...], ws[1], bv_ref[...], first)
        sg = 1.0 / (1.0 + jnp.exp(-ug))
        da = da_ref[...].astype(F32)
        d_v = da * (ug * sg)
        d_g = da * uv * (sg * (1.0 + ug * (1.0 - sg)))
        for h, (du, x0, x1, x2) in enumerate(((d_g, curs[0], g1, g2), (d_v, curs[1], v1, v2))):
            dcb_ref[h] += jnp.sum(du, axis=0, keepdims=True)
            dcw_ref[h, 0:1, :] += jnp.sum(du * x2, axis=0, keepdims=True)
            dcw_ref[h, 1:2, :] += jnp.sum(du * x1, axis=0, keepdims=True)
            dcw_ref[h, 2:3, :] += jnp.sum(du * x0, axis=0, keepdims=True)
            ext = jnp.concatenate([du, carry_ref[h]], axis=0)
            n1 = pltpu.roll(ext, tm + 7, 0)[:tm]
            n2 = pltpu.roll(ext, tm + 6, 0)[:tm]
            w = ws[h]
            dup_ref[h] = (w[2:3] * du + w[1:2] * n1 + w[0:1] * n2).astype(dup_ref.dtype)
            carry_ref[h] = du[:8]

    def cur(h):
        return pl.BlockSpec((None, tm, tn), lambda j, i: (h, nrb - 1 - i, j))

    def prev(h):
        return pl.BlockSpec((None, 8, tn), lambda j, i: (h, jnp.maximum((nrb - 1 - i) * rb8 - 1, 0), j))

    def par(h, r):
        return pl.BlockSpec((None, r, tn), lambda j, i: (h, 0, j))

    return pl.pallas_call(
        body, name="conv_gate_bwd", grid=(D_FF // tn, nrb),
        in_specs=[cur(0), cur(1), prev(0), prev(1),
                  pl.BlockSpec((tm, tn), lambda j, i: (nrb - 1 - i, j)),
                  par(0, 3), par(1, 3), par(0, 1), par(1, 1)],
        out_specs=[pl.BlockSpec((2, tm, tn), lambda j, i: (0, nrb - 1 - i, j)),
                   pl.BlockSpec((2, 3, tn), lambda j, i: (0, 0, j)),
                   pl.BlockSpec((2, 1, tn), lambda j, i: (0, 0, j))],
        out_shape=[jax.ShapeDtypeStruct((2, s, D_FF), BF16),
                   jax.ShapeDtypeStruct((2, 3, D_FF), F32),
                   jax.ShapeDtypeStruct((2, 1, D_FF), F32)],
        scratch_shapes=[pltpu.VMEM((2, 8, tn), F32)],
        compiler_params=_cparams(("parallel", "arbitrary")),
    )(up, up, up, up, dact, conv_w, conv_w, conv_b, conv_b)


def _split_dot(x, tri, terms):
    piece = x.astype(BF16)
    out = jnp.dot(piece, tri, preferred_element_type=F32)
    rest = x
    for _ in range(terms - 1):
        rest = rest - piece.astype(F32)
        piece = rest.astype(BF16)
        out = out + jnp.dot(piece, tri, preferred_element_type=F32)
    return out


def _split_dot_rhs(tri, x, terms):
    piece = x.astype(BF16)
    out = jnp.dot(tri, piece, preferred_element_type=F32)
    rest = x
    for _ in range(terms - 1):
        rest = rest - piece.astype(F32)
        piece = rest.astype(BF16)
        out = out + jnp.dot(tri, piece, preferred_element_type=F32)
    return out


def _tri(n, kind):
    r = lax.broadcasted_iota(jnp.int32, (n, n), 0)
    c = lax.broadcasted_iota(jnp.int32, (n, n), 1)
    cond = {"le": r <= c, "ge": r >= c, "lt": r < c, "gt": r > c}[kind]
    return jnp.where(cond, 1.0, 0.0).astype(BF16)


def _log_sigmoid(x):
    return jnp.minimum(x, 0.0) - jnp.log(1.0 + jnp.exp(-jnp.abs(x)))


def _forget_fwd(f_logit, bias):
    h, r, _ = f_logit.shape

    def body(x_ref, b_ref, o_ref):
        lf = _log_sigmoid(x_ref[...] + b_ref[...])
        within = _split_dot(lf, _tri(LANES, "le"), 3)
        row_tot = jnp.broadcast_to(within[:, LANES - 1:LANES], (r, LANES))
        before = _split_dot_rhs(_tri(r, "gt"), row_tot, 3)
        o_ref[...] = within + before

    blk = pl.BlockSpec((None, r, LANES), lambda i: (i, 0, 0))
    return pl.pallas_call(
        body, name="forget_cumsum_fwd", grid=(h,),
        in_specs=[blk, pl.BlockSpec((None, 1, LANES), lambda i: (i, 0, 0))],
        out_specs=blk, out_shape=jax.ShapeDtypeStruct((h, r, LANES), F32),
        compiler_params=_cparams(("parallel",)),
    )(f_logit, bias)


def _forget_bwd(f_logit, bias, ksum, qsum):
    h, r, _ = f_logit.shape

    def body(x_ref, b_ref, k_ref, q_ref, dx_ref, db_ref):
        d_f = q_ref[...] - k_ref[...]
        within = _split_dot(d_f, _tri(LANES, "ge"), 3)
        row_tot = jnp.broadcast_to(within[:, 0:1], (r, LANES))
        after = _split_dot_rhs(_tri(r, "lt"), row_tot, 3)
        xv = x_ref[...] + b_ref[...]
        dx = (within + after) * jnp.exp(_log_sigmoid(-xv))
        dx_ref[...] = dx
        db_ref[...] = jnp.broadcast_to(jnp.sum(dx), (1, LANES))

    blk = pl.BlockSpec((None, r, LANES), lambda i: (i, 0, 0))
    one = pl.BlockSpec((None, 1, LANES), lambda i: (i, 0, 0))
    return pl.pallas_call(
        body, name="forget_cumsum_bwd", grid=(h,),
        in_specs=[blk, one, blk, blk], out_specs=[blk, one],
        out_shape=[jax.ShapeDtypeStruct((h, r, LANES), F32), jax.ShapeDtypeStruct((h, 1, LANES), F32)],
        compiler_params=_cparams(("parallel",)),
    )(f_logit, bias, ksum, qsum)


def _head_specs(s, tq):
    qblk = pl.BlockSpec((None, tq, HEAD_DIM), lambda h, i: (h, i, 0))
    full = pl.BlockSpec((None, s, HEAD_DIM), lambda h, i: (h, 0, 0))
    col = pl.BlockSpec((None, tq, 1), lambda h, i: (h, i, 0))
    return qblk, full, col


def _qkv_specs(s, tq, offs):
    q_off, k_off, v_off = offs
    return (pl.BlockSpec((None, tq, HEAD_DIM), lambda h, i: (h + q_off, i, 0)),
            pl.BlockSpec((None, s, HEAD_DIM), lambda h, i: (h + k_off, 0, 0)),
            pl.BlockSpec((None, s, HEAD_DIM), lambda h, i: (h + v_off, 0, 0)))


def _scaled(q_ref):
    return (q_ref[...].astype(F32) * Q_SCALE).astype(BF16)


_NT = (((1,), (1,)), ((), ()))
_TN = (((0,), (0,)), ((), ()))


def _cols_minus_rows(rows, cols):
    return lax.broadcasted_iota(jnp.int32, (rows, cols), 1) - lax.broadcasted_iota(jnp.int32, (rows, cols), 0)


def _fox_fwd(qkv, offs, v_ones, f_row, tq, tk):
    h, s = N_GROUP_HEADS, qkv.shape[1]
    nk = s // tk
    assert tq == tk

    def body(q_ref, k_ref, v_ref, fr_ref, o_ref, lse_ref, m_ref, acc_ref, z0, z1):
        i = pl.program_id(1)
        qs = _scaled(q_ref)
        m_ref[...] = jnp.full_like(m_ref, NEG_BIG)
        acc_ref[...] = jnp.zeros_like(acc_ref)

        ahead = _cols_minus_rows(tq, tk)

        def block_of(j):
            return jnp.minimum(j, nk - 1)

        def keys_of(j):
            return pl.ds(pl.multiple_of(block_of(j) * tk, tk), tk)

        def logits(j):
            return lax.dot_general(qs, k_ref[keys_of(j), :], _NT, preferred_element_type=F32)

        def soft(j, raw, masked):
            sc = raw - fr_ref[block_of(j)]
            if masked:
                sc = jnp.where(ahead <= (i - j) * tk, sc, NEG_BIG)
            m_old = m_ref[...]
            m_new = jnp.maximum(m_old, jnp.max(sc, axis=-1, keepdims=True))
            p = jnp.exp(sc - m_new)
            acc_ref[...] = jnp.exp(m_old - m_new) * acc_ref[...] + jnp.dot(
                p.astype(BF16), v_ref[keys_of(j), :], preferred_element_type=F32)
            m_ref[...] = m_new

        z0[...] = logits(0)

        def trip(p, masked):
            j = 2 * p
            z1[...] = logits(j + 1)
            soft(j, z0[...], masked)
            z0[...] = logits(j + 2)
            soft(j + 1, z1[...], masked)

        def step(p, carry):
            trip(p, False)
            return carry

        lax.fori_loop(0, i // 2, step, 0)
        trip(i // 2, True)
        l = acc_ref[:, HEAD_DIM:HEAD_DIM + 1]
        o_ref[...] = acc_ref[:, :HEAD_DIM] / l
        lse_ref[...] = m_ref[...] + jnp.log(l)

    qblk, full, colspec = _head_specs(s, tq)
    q_in, k_in, _ = _qkv_specs(s, tq, offs)
    return pl.pallas_call(
        body, name="fox_fwd", grid=(h, s // tq),
        in_specs=[q_in, k_in, pl.BlockSpec((None, s, 2 * HEAD_DIM), lambda hh, i: (hh, 0, 0)),
                  pl.BlockSpec((None, nk, 1, tk), lambda hh, i: (hh, 0, 0, 0))],
        out_specs=[qblk, colspec],
        out_shape=[jax.ShapeDtypeStruct((h, s, HEAD_DIM), F32), jax.ShapeDtypeStruct((h, s, 1), F32)],
        scratch_shapes=[pltpu.VMEM((tq, 1), F32), pltpu.VMEM((tq, 2 * HEAD_DIM), F32),
                        pltpu.VMEM((tq, tk), F32), pltpu.VMEM((tq, tk), F32)],
        compiler_params=_cparams(("parallel", "parallel")),
    )(qkv, qkv, v_ones, f_row)


def _fox_bwd(qkv, offs, f_row, o, lse, d_o, tq, tk):
    h, s = N_GROUP_HEADS, qkv.shape[1]
    nk = s // tk
    assert tq == tk

    def body(q_ref, k_ref, v_ref, fr_ref, o_ref, lse_ref, do_ref,
             dq_ref, dk_ref, dv_ref, ks_ref, qs_ref, dq_acc, qsum_acc, z0, z1, p0, p1):
        i = pl.program_id(1)

        @pl.when(i == 0)
        def _():
            dk_ref[...] = jnp.zeros_like(dk_ref)
            dv_ref[...] = jnp.zeros_like(dv_ref)
            ks_ref[...] = jnp.zeros_like(ks_ref)

        qs = _scaled(q_ref)
        lse_v = lse_ref[...]
        dob = do_ref[...].astype(BF16)
        delta = jnp.sum(dob.astype(F32) * o_ref[...], axis=-1, keepdims=True)
        dq_acc[...] = jnp.zeros_like(dq_acc)
        qsum_acc[...] = jnp.zeros_like(qsum_acc)

        ahead = _cols_minus_rows(tq, tk)

        def block_of(j):
            return jnp.minimum(j, nk - 1)

        def keys_of(j):
            return pl.ds(pl.multiple_of(block_of(j) * tk, tk), tk)

        def products(j):
            at = keys_of(j)
            return (lax.dot_general(qs, k_ref[at, :], _NT, preferred_element_type=F32),
                    lax.dot_general(dob, v_ref[at, :], _NT, preferred_element_type=F32))

        def grads(j, raw, dp, masked):
            at = keys_of(j)
            sc = raw - fr_ref[block_of(j)]
            if masked:
                sc = jnp.where(ahead <= (i - j) * tk, sc, NEG_BIG)
            p = jnp.exp(sc - lse_v)
            ds = p * (dp - delta)
            dsb = ds.astype(BF16)
            dq_acc[...] += jnp.dot(dsb, k_ref[at, :], preferred_element_type=F32)
            dk_ref[at, :] += lax.dot_general(dsb, qs, _TN, preferred_element_type=F32)
            dv_ref[at, :] += lax.dot_general(p.astype(BF16), dob, _TN, preferred_element_type=F32)
            ks_ref[block_of(j)] += jnp.sum(ds.reshape(tq // 8, 8, tk), axis=0)
            qsum_acc[...] += jnp.sum(ds, axis=-1, keepdims=True)

        z0[...], p0[...] = products(0)

        def trip(pp, masked):
            j = 2 * pp
            z1[...], p1[...] = products(j + 1)
            grads(j, z0[...], p0[...], masked)
            z0[...], p0[...] = products(j + 2)
            grads(j + 1, z1[...], p1[...], masked)

        def step(pp, carry):
            trip(pp, False)
            return carry

        lax.fori_loop(0, i // 2, step, 0)
        trip(i // 2, True)
        dq_ref[...] = dq_acc[...] * Q_SCALE
        qs_ref[...] = qsum_acc[...]

    qblk, full, colspec = _head_specs(s, tq)
    frow = pl.BlockSpec((None, nk, 1, tk), lambda hh, i: (hh, 0, 0, 0))
    big = pltpu.VMEM((tq, tk), F32)
    return pl.pallas_call(
        body, name="fox_bwd", grid=(h, s // tq),
        in_specs=[*_qkv_specs(s, tq, offs), frow, qblk, colspec, qblk],
        out_specs=[qblk, full, full, pl.BlockSpec((None, nk, 8, tk), lambda hh, i: (hh, 0, 0, 0)), colspec],
        out_shape=[jax.ShapeDtypeStruct((h, s, HEAD_DIM), F32)] * 3
        + [jax.ShapeDtypeStruct((h, nk, 8, tk), F32), jax.ShapeDtypeStruct((h, s, 1), F32)],
        scratch_shapes=[pltpu.VMEM((tq, HEAD_DIM), F32), pltpu.VMEM((tq, 1), F32), big, big, big, big],
        compiler_params=_cparams(("parallel", "arbitrary")),
    )(qkv, qkv, qkv, f_row, o, lse, d_o)


SB_TERMS = 2
G_TERMS = 1
LOG2E = 1.4426950408889634
LN2 = 0.6931471805599453


def _softplus2(z2):
    return jnp.maximum(z2, 0.0) + jnp.log2(1.0 + jnp.exp2(-jnp.abs(z2)))


def _sb_fwd(qkv, offs, tq, tk):
    h, s = N_GROUP_HEADS, qkv.shape[1]

    assert tq % (2 * tk) == 0

    def body(q_ref, k_ref, v_ref, o_ref, w_hbm, acc_ref, run_ref, z0, z1, d0, d1, t0, t1, w_stage, wsem):
        z_refs, d_refs, t_refs = (z0, z1), (d0, d1), (t0, t1)
        hh = pl.program_id(0)
        i = pl.program_id(1)
        qs = _scaled(q_ref)
        tri = _tri(tk, "ge")
        acc_ref[...] = jnp.zeros_like(acc_ref)
        run_ref[...] = jnp.zeros_like(run_ref)
        nb = (i + 1) * (tq // tk)
        ahead = _cols_minus_rows(tq, tk)

        def keys_of(b):
            j = nb - 1 - jnp.minimum(b, nb - 1)
            return pl.ds(pl.multiple_of(j * tk, tk), tk)

        def visible(b):
            return ahead < i * tq - (nb - 1 - b) * tk

        def logits(b, slot):
            z_refs[slot][...] = lax.dot_general(qs, k_ref[keys_of(b), :], _NT,
                                                preferred_element_type=F32) * LOG2E

        def sums(b, slot, masked):
            z2 = z_refs[slot][...]
            sp = _softplus2(z2)
            if masked:
                sp = jnp.where(visible(b), sp, 0.0)
            inc = _split_dot(sp, tri, SB_TERMS)
            d_refs[slot][...] = z2 - inc
            t_refs[slot][...] = inc[:, 0:1]

        def put(p, slot):
            st = (p % 2) * 2 + slot
            return pltpu.make_async_copy(w_stage.at[st], w_hbm.at[hh, i, nb - 1 - (2 * p + slot)], wsem.at[st])

        def weigh(p, slot, masked):
            b = 2 * p + slot
            w = jnp.exp2(d_refs[slot][...] - run_ref[...])
            if masked:
                w = jnp.where(visible(b), w, 0.0)
            wb = w.astype(BF16)
            w_stage[(p % 2) * 2 + slot] = wb
            acc_ref[...] += jnp.dot(wb, v_ref[keys_of(b), :], preferred_element_type=F32)
            run_ref[...] += t_refs[slot][...]

        def trip(p, masked):
            @pl.when(p >= 2)
            def _():
                put(p - 2, 0).wait()
                put(p - 2, 1).wait()

            b = 2 * p
            logits(b + 2, 0)
            sums(b + 1, 1, masked)
            weigh(p, 0, masked)
            logits(b + 3, 1)
            sums(b + 2, 0, masked)
            weigh(p, 1, masked)
            put(p, 0).start()
            put(p, 1).start()

        logits(0, 0)
        logits(1, 1)
        sums(0, 0, True)

        def guarded(p, carry):
            trip(p, True)
            return carry

        def plain(p, carry):
            trip(p, False)
            return carry

        lax.fori_loop(0, tq // tk // 2, guarded, 0)
        lax.fori_loop(tq // tk // 2, nb // 2, plain, 0)
        trips = nb // 2

        @pl.when(trips >= 2)
        def _():
            put(trips - 2, 0).wait()
            put(trips - 2, 1).wait()

        put(trips - 1, 0).wait()
        put(trips - 1, 1).wait()
        o_ref[...] = acc_ref[...]

    qblk, full, colspec = _head_specs(s, tq)
    return pl.pallas_call(
        body, name="sb_fwd", grid=(h, s // tq),
        in_specs=[*_qkv_specs(s, tq, offs)], out_specs=[qblk, pl.BlockSpec(memory_space=pl.ANY)],
        out_shape=[jax.ShapeDtypeStruct((h, s, HEAD_DIM), F32),
                   jax.ShapeDtypeStruct((h, s // tq, s // tk, tq, tk), BF16)],
        scratch_shapes=[pltpu.VMEM((tq, HEAD_DIM), F32), pltpu.VMEM((tq, 1), F32),
                        pltpu.VMEM((tq, tk), F32), pltpu.VMEM((tq, tk), F32),
                        pltpu.VMEM((tq, tk), F32), pltpu.VMEM((tq, tk), F32),
                        pltpu.VMEM((tq, 1), F32), pltpu.VMEM((tq, 1), F32),
                        pltpu.VMEM((4, tq, tk), BF16), pltpu.SemaphoreType.DMA((4,))],
        compiler_params=_cparams(("parallel", "parallel")),
    )(qkv, qkv, qkv)


def _sb_bwd(qkv, offs, w_saved, d_o, tq, tk):
    h, s = N_GROUP_HEADS, qkv.shape[1]

    assert tq % (2 * tk) == 0

    def body(q_ref, k_ref, v_ref, do_ref, w_hbm, dq_ref, dk_ref, dv_ref, dq_acc, grun_ref,
             z0, z1, p0, p1, w_bufs, wsem):
        z_refs, p_refs = (z0, z1), (p0, p1)
        hh = pl.program_id(0)
        i = pl.program_id(1)

        @pl.when(i == 0)
        def _():
            dk_ref[...] = jnp.zeros_like(dk_ref)
            dv_ref[...] = jnp.zeros_like(dv_ref)

        qs = _scaled(q_ref)
        dob = do_ref[...].astype(BF16)
        tri = _tri(tk, "le")
        dq_acc[...] = jnp.zeros_like(dq_acc)
        grun_ref[...] = jnp.zeros_like(grun_ref)
        nb = (i + 1) * (tq // tk)
        ahead = _cols_minus_rows(tq, tk)

        def block_of(b):
            return jnp.minimum(b, nb - 1)

        def keys_of(b):
            return pl.ds(pl.multiple_of(block_of(b) * tk, tk), tk)

        def visible(b):
            return ahead < i * tq - b * tk

        def fetch(p, slot):
            st = (p % 2) * 2 + slot
            return pltpu.make_async_copy(w_hbm.at[hh, i, block_of(2 * p + slot)], w_bufs.at[st], wsem.at[st])

        def products(b, slot):
            at = keys_of(b)
            z_refs[slot][...] = lax.dot_general(qs, k_ref[at, :], _NT, preferred_element_type=F32) * LOG2E
            p_refs[slot][...] = lax.dot_general(dob, v_ref[at, :], _NT, preferred_element_type=F32)

        def grads(p, slot, masked):
            b = 2 * p + slot
            at = keys_of(b)
            wb = w_bufs[(p % 2) * 2 + slot]
            g = wb.astype(F32) * p_refs[slot][...]
            ginc = _split_dot(g, tri, G_TERMS)
            beta = 1.0 / (1.0 + jnp.exp2(-z_refs[slot][...]))
            dz = g - beta * (grun_ref[...] + ginc)
            if masked:
                dz = jnp.where(visible(b), dz, 0.0)
            dzb = dz.astype(BF16)
            dq_acc[...] += jnp.dot(dzb, k_ref[at, :], preferred_element_type=F32)
            dk_ref[at, :] += lax.dot_general(dzb, qs, _TN, preferred_element_type=F32)
            dv_ref[at, :] += lax.dot_general(wb, dob, _TN, preferred_element_type=F32)
            grun_ref[...] += ginc[:, tk - 1:tk]

        def trip(p, masked):
            for slot in (0, 1):
                fetch(p + 1, slot).start()
            for slot in (0, 1):
                fetch(p, slot).wait()
            for slot in (0, 1):
                products(2 * p + slot + 1, 1 - slot)
                grads(p, slot, masked)

        for slot in (0, 1):
            fetch(0, slot).start()
        products(0, 0)
        n_plain = (nb - tq // tk) // 2

        def plain(p, carry):
            trip(p, False)
            return carry

        def guarded(p, carry):
            trip(p, True)
            return carry

        lax.fori_loop(0, n_plain, plain, 0)
        lax.fori_loop(n_plain, nb // 2, guarded, 0)
        for slot in (0, 1):
            fetch(nb // 2, slot).wait()
        dq_ref[...] = dq_acc[...] * Q_SCALE

    qblk, full, colspec = _head_specs(s, tq)
    big = pltpu.VMEM((tq, tk), F32)
    return pl.pallas_call(
        body, name="sb_bwd", grid=(h, s // tq),
        in_specs=[*_qkv_specs(s, tq, offs), qblk, pl.BlockSpec(memory_space=pl.ANY)], out_specs=[qblk, full, full],
        out_shape=[jax.ShapeDtypeStruct((h, s, HEAD_DIM), F32)] * 3,
        scratch_shapes=[pltpu.VMEM((tq, HEAD_DIM), F32), pltpu.VMEM((tq, 1), F32)]
        + [big] * 4 + [pltpu.VMEM((4, tq, tk), BF16), pltpu.SemaphoreType.DMA((4,))],
        compiler_params=_cparams(("parallel", "arbitrary")),
    )(qkv, qkv, qkv, d_o, w_saved)


def _sum_adamw(parts, w, m, v, name, tr=256):
    _, rows, lanes = parts.shape
    tr = _tile(rows, tr, 16)
    c_m = 1.0 - ADAM_B1 ** ADAM_STEP
    c_v = 1.0 - ADAM_B2 ** ADAM_STEP

    def body(p_ref, w_ref, m_ref, v_ref, g_ref, d_ref, nm_ref, nv_ref):
        g = p_ref[0].astype(F32)
        for j in range(1, N_DEV):
            g = g + p_ref[j].astype(F32)
        nm = ADAM_B1 * m_ref[...] + (1.0 - ADAM_B1) * g
        nv = ADAM_B2 * v_ref[...] + (1.0 - ADAM_B2) * (g * g)
        m_hat = nm / c_m
        v_hat = nv / c_v
        g_ref[...] = g
        d_ref[...] = -ADAM_LR * (m_hat / (jnp.sqrt(v_hat) + ADAM_EPS) + ADAM_WD * w_ref[...])
        nm_ref[...] = nm
        nv_ref[...] = nv

    blk = pl.BlockSpec((tr, lanes), lambda i: (i, 0))
    return pl.pallas_call(
        body, name=name, grid=(rows // tr,),
        in_specs=[pl.BlockSpec((N_DEV, tr, lanes), lambda i: (0, i, 0)), blk, blk, blk],
        out_specs=[blk] * 4, out_shape=[jax.ShapeDtypeStruct((rows, lanes), F32)] * 4,
        compiler_params=_cparams(("parallel",)),
    )(parts, w, m, v)


def kernel(x, attn_norm_g, w_in, forget_bias, fox_out_g, sb_out_g, w_out, ffn_norm_g, w_up, conv_w, conv_b, w_down, final_norm_g, loss_target, m_attn_norm_g, m_w_in, m_forget_bias, m_fox_out_g, m_sb_out_g, m_w_out, m_ffn_norm_g, m_w_up, m_conv_w, m_conv_b, m_w_down, m_final_norm_g, v_attn_norm_g, v_w_in, v_forget_bias, v_fox_out_g, v_sb_out_g, v_w_out, v_ffn_norm_g, v_w_up, v_conv_w, v_conv_b, v_w_down, v_final_norm_g):
    s = x.shape[1]
    xs = x[0]
    tq = min(ATTN_TQ, s)
    tk_fox = min(FOX_TK, s)
    tk_sb = min(SB_TK, s)
    in_shard, up_shard, out_shard, down_shard = IN_COLS // N_DEV, 2 * D_FF // N_DEV, D_MODEL // N_DEV, D_FF // N_DEV

    cw = conv_w[0]
    cw_hi = cw.astype(BF16)
    cw_lo = (cw - cw_hi.astype(F32)).astype(BF16)
    (g_in,) = _all_gather([w_in[0].astype(BF16)])
    rest = _exchange_start([w_out[0].astype(BF16), w_up[0].astype(BF16), w_down[0].astype(BF16),
                            jnp.stack([cw_hi, cw_lo])], False, "weights_rest_start")
    n_gate = QKV_W + N_GROUP_HEADS
    in_windows = _col_windows(N_DEV, in_shard, gap_at=n_gate, gap=GATE_PAD - N_GROUP_HEADS)
    up_windows = _col_windows(N_DEV, up_shard)
    w_in_p = _assemble_cols(g_in, IN_COLS_PAD, in_windows, "assemble_w_in")
    conv_b2 = conv_b.reshape(2, 1, D_FF)

    h1 = _rms_fwd(xs, attn_norm_g + rest[-1][0:1, 0:1])
    proj_h = _mm_heads(h1, w_in_p, "in_proj")
    fox_offs = (0, N_GROUP_HEADS, 2 * N_GROUP_HEADS)
    sb_first = 3 * N_GROUP_HEADS + GATE_PAD // HEAD_DIM
    sb_offs = (sb_first, sb_first + N_GROUP_HEADS, sb_first + 2 * N_GROUP_HEADS)
    f_logit = _mm_nn(h1, w_in_p[:, QKV_W:QKV_W + GATE_PAD], F32, "gate_proj")[:, :N_GROUP_HEADS]
    fv = proj_h[2 * N_GROUP_HEADS:3 * N_GROUP_HEADS]

    f_logit_h = f_logit.T.reshape(N_GROUP_HEADS, s // LANES, LANES)
    bias_h = jnp.broadcast_to(forget_bias.reshape(N_GROUP_HEADS, 1, 1), (N_GROUP_HEADS, 1, LANES))
    big_f = _forget_fwd(f_logit_h, bias_h)
    f_row = big_f.reshape(N_GROUP_HEADS, s // tk_fox, 1, tk_fox)

    fv_ones = jnp.concatenate([fv, jnp.ones_like(fv)], axis=-1)
    o_fox_h, lse = _fox_fwd(proj_h, fox_offs, fv_ones, f_row, tq, tk_fox)
    o_sb_h, sb_w = _sb_fwd(proj_h, sb_offs, min(SB_TQ, s), tk_sb)
    g_fox_h = fox_out_g.reshape(N_GROUP_HEADS, 1, HEAD_DIM)
    g_sb_h = sb_out_g.reshape(N_GROUP_HEADS, 1, HEAD_DIM)
    o_n = _group_rms_fwd(o_fox_h, o_sb_h, g_fox_h, g_sb_h)
    g_out, g_up, g_down, g_conv = _exchange_wait(rest, False, o_n, "weights_rest_wait")
    w_out_f = g_out.reshape(D_MODEL, D_MODEL)
    w_up_f = _assemble_cols(g_up, 2 * D_FF, up_windows, "assemble_w_up")
    w_down_f = g_down.reshape(D_FF, D_MODEL)
    conv_w_f = (g_conv[:, 0].astype(F32) + g_conv[:, 1].astype(F32)).transpose(1, 0, 2).reshape(3, 2 * D_FF)
    conv_w2 = conv_w_f.reshape(3, 2, D_FF).transpose(1, 0, 2)
    x1 = _mm_nn(o_n, w_out_f, F32, "out_proj", resid=xs)
    h2 = _rms_fwd(x1, ffn_norm_g)
    up = _mm_up(h2, w_up_f)
    act = _conv_gate_fwd(up, conv_w2, conv_b2)
    x2 = _mm_nn(act, w_down_f, F32, "down_proj", resid=x1, tk=1408)

    d_x2, d_x2b, dg_final, loss_part = _loss_head(x2, loss_target[0], final_norm_g.reshape(1, D_MODEL))
    d_act = _mm_nt(d_x2b, w_down_f, BF16, "d_act", tn=1408)
    dw_down = _mm_tn(act, d_x2b, "d_w_down", tm=1408)
    d_up, dcw2, dcb2 = _conv_gate_bwd(up, d_act, conv_w2, conv_b2)
    d_h2 = _mm_dup_nt(d_up, w_up_f)
    dw_up = _mm_dwup_tn(h2, d_up)
    d_x1, d_x1b, dg_ffn = _rms_bwd(x1, d_h2, ffn_norm_g, d_x2, dy_col=0, name="ffn_norm_bwd", want_bf16=True)
    d_on = _mm_nt(d_x1b, w_out_f, F32, "d_o_normed")
    dw_out = _mm_tn(o_n, d_x1b, "d_w_out")
    early = _exchange_start(
        [dw_out.astype(BF16).reshape(N_DEV, out_shard, D_MODEL),
         _split_cols(dw_up, N_DEV, up_shard, up_windows, "split_d_w_up"),
         dw_down.astype(BF16).reshape(N_DEV, down_shard, D_MODEL)], True, "grads_early_start")
    g_fox_t = g_fox_h + early[-1][0:1, 0:1]
    d_o_fox_h, dg_fox = _group_rms_bwd(o_fox_h, d_on, g_fox_t, dy_col=0, name="fox_norm_bwd")
    d_o_sb_h, dg_sb = _group_rms_bwd(o_sb_h, d_on, g_sb_h, dy_col=1, name="sb_norm_bwd")

    dfq, dfk, dfv, ksum8, qsum = _fox_bwd(proj_h, fox_offs, f_row, o_fox_h, lse, d_o_fox_h, tq, tk_fox)
    dsq, dsk, dsv = _sb_bwd(proj_h, sb_offs, sb_w, d_o_sb_h, min(SB_TQ, s), tk_sb)
    ksum = jnp.sum(ksum8, axis=2).reshape(N_GROUP_HEADS, s // LANES, LANES)
    d_f_logit_h, d_bias_h = _forget_bwd(f_logit_h, bias_h, ksum,
                                        qsum.reshape(N_GROUP_HEADS, s // LANES, LANES))
    d_f_logit = d_f_logit_h.reshape(N_GROUP_HEADS, s).T

    d_proj = _merge_dproj((dfq, dfk, dfv), d_f_logit, (dsq, dsk, dsv))
    dw_in_p = _mm_tn(h1, d_proj, "d_w_in", tn=640)
    dconv_w = dcw2.transpose(1, 0, 2).reshape(3, 2 * D_FF)
    dconv_b = dcb2.reshape(1, 2 * D_FF)
    late = _exchange_start(
        [_split_cols(dw_in_p, N_DEV, in_shard, in_windows, "split_d_w_in"),
         dconv_w.astype(BF16).reshape(3, N_DEV, up_shard).transpose(1, 0, 2)],
        True, "grads_late_start")
    d_h1 = _mm_nt(d_proj, w_in_p + late[-1][0:1, 0:1].astype(BF16), F32, "d_h1", tk=640)
    grad_x, dg_attn = _rms_bwd(xs, d_h1, attn_norm_g, d_x1, dy_col=0, name="attn_norm_bwd", want_bf16=False)

    small_shapes = [(1, D_MODEL), (1, N_GROUP_HEADS), (1, GROUP_W), (1, GROUP_W), (1, D_MODEL),
                    (1, 2 * D_FF), (D_MODEL,), (1,)]
    spack = _pack([dg_attn, d_bias_h[:, 0, 0], dg_fox, dg_sb, dg_ffn, dconv_b, dg_final, loss_part[0, 0:1]],
                  SMALL_ROWS, F32)
    (srecv,) = _grad_exchange([], spack)
    r_out, r_up, r_down = _exchange_wait(early, True, srecv, "grads_early_wait")
    r_in, r_conv = _exchange_wait(late, True, r_out, "grads_late_wait")

    big = [_sum_adamw(g, w_[0], m_[0], v_[0], "adamw_" + tag)
           for g, w_, m_, v_, tag in zip(
               (r_in, r_out, r_up, r_down, r_conv), (w_in, w_out, w_up, w_down, conv_w), (m_w_in, m_w_out, m_w_up, m_w_down, m_conv_w),
               (v_w_in, v_w_out, v_w_up, v_w_down, v_conv_w), ("w_in", "w_out", "w_up", "w_down", "conv_w"))]

    def small_pack(a_attn, a_bias, a_fox, a_sb, a_ffn, a_cb, a_fin):
        return _pack([a_attn, a_bias, a_fox, a_sb, a_ffn, a_cb, a_fin, jnp.zeros((1,), F32)], SMALL_ROWS, F32)

    small = _sum_adamw(srecv, small_pack(attn_norm_g, forget_bias, fox_out_g, sb_out_g, ffn_norm_g, conv_b, final_norm_g),
                       small_pack(m_attn_norm_g, m_forget_bias, m_fox_out_g, m_sb_out_g, m_ffn_norm_g, m_conv_b, m_final_norm_g),
                       small_pack(v_attn_norm_g, v_forget_bias, v_fox_out_g, v_sb_out_g, v_ffn_norm_g, v_conv_b, v_final_norm_g),
                       "adamw_replicated", tr=SMALL_ROWS)

    outs = []
    loss = None
    for kind in range(4):
        b_in, b_out, b_up, b_down, b_conv = (res[kind] for res in big)
        s_attn, s_bias, s_fox, s_sb, s_ffn, s_cb, s_fin, s_loss = _unpack(small[kind], small_shapes)
        if kind == 0:
            loss = s_loss[0]
        outs += [s_attn, b_in[None], s_bias, s_fox, s_sb, b_out[None], s_ffn, b_up[None], b_conv[None], s_cb,
                 b_down[None], s_fin]
    return (loss, grad_x[None], *outs)
```

```python
import jax
import jax.numpy as jnp
from jax import lax
from jax.experimental import pallas as pl
from jax.experimental.pallas import tpu as pltpu

F32 = jnp.float32
BF16 = jnp.bfloat16

D_MODEL = 1024
HEAD_DIM = 64
N_GROUP_HEADS = 8
GROUP_W = N_GROUP_HEADS * HEAD_DIM
QKV_W = 3 * GROUP_W
IN_COLS = 2 * QKV_W + N_GROUP_HEADS
GATE_PAD = 128
IN_COLS_PAD = 2 * QKV_W + GATE_PAD
D_FF = 2816
N_DEV = 8
EPS = 1e-6
Q_SCALE = HEAD_DIM ** -0.5

ADAM_LR = 0.001
ADAM_B1 = 0.9
ADAM_B2 = 0.999
ADAM_EPS = 1e-08
ADAM_WD = 0.01
ADAM_STEP = 10

LANES = 128
SMALL_ROWS = 80
VMEM_LIMIT = 56 * 1024 * 1024
NEG_BIG = -1e30
ATTN_TQ = 512
SB_TQ = 512
FOX_TK = 512
SB_TK = 256
MESH = pl.DeviceIdType.MESH


def _cparams(sem=None, **kw):
    return pltpu.CompilerParams(dimension_semantics=sem, vmem_limit_bytes=VMEM_LIMIT, **kw)


def _tile(n, target, mult=LANES):
    if n <= target:
        return n
    t = (target // mult) * mult
    while t >= mult:
        if n % t == 0:
            return t
        t -= mult
    return n


def _seg_len(shape):
    n = 1
    for s in shape:
        n *= s
    return -(-n // LANES) * LANES


def _pack(arrs, rows, dtype):
    parts = []
    for a in arrs:
        f = a.reshape(-1).astype(dtype)
        parts.append(jnp.pad(f, (0, _seg_len(a.shape) - f.shape[0])))
    flat = jnp.concatenate(parts)
    flat = jnp.pad(flat, (0, rows * LANES - flat.shape[0]))
    return flat.reshape(rows, LANES)


def _unpack(p, shapes, lead=()):
    flat = p.reshape(lead + (-1,))
    out, off = [], 0
    for shp in shapes:
        n = 1
        for s in shp:
            n *= s
        out.append(flat[..., off:off + n].reshape(lead + tuple(shp)))
        off += _seg_len(shp)
    return out


def _my_pos():
    return lax.axis_index("x"), lax.axis_index("y"), lax.axis_index("c")


def _all_gather(blocks):
    n = len(blocks)

    def body(*refs):
        x_refs, out_refs = refs[:n], refs[n:2 * n]
        send_sems, recv_sems, local_sems = refs[2 * n:]
        x, y, c = _my_pos()
        me, sibling = (x, y, c), (x, y, 1 - c)
        chips = [(1 - x, y), (x, 1 - y), (1 - x, 1 - y)]

        def copy(a, k, blk, to, own=False):
            px, py, pc = blk
            slot = out_refs[a].at[4 * px + 2 * py + pc]
            return pltpu.make_async_remote_copy(
                src_ref=x_refs[a] if own else slot, dst_ref=slot,
                send_sem=send_sems.at[a, k], recv_sem=recv_sems.at[a, k],
                device_id=to, device_id_type=MESH)

        mine = [pltpu.make_async_copy(x_refs[a], out_refs[a].at[4 * x + 2 * y + c], local_sems.at[a])
                for a in range(n)]
        for cp in mine:
            cp.start()
        first = []
        for a in range(n):
            first.append(copy(a, 0, me, sibling, own=True))
            first += [copy(a, 1 + j, me, (*chip, c), own=True) for j, chip in enumerate(chips)]
        for cp in first:
            cp.start()
        passed = []
        for j, chip in enumerate(chips):
            for a in range(n):
                copy(a, 1 + j, (*chip, c), me).wait_recv()
                passed.append(copy(a, 4 + j, (*chip, c), sibling))
                passed[-1].start()
        for a in range(n):
            copy(a, 0, sibling, me).wait_recv()
            for j, chip in enumerate(chips):
                copy(a, 4 + j, (*chip, 1 - c), me).wait_recv()
        for cp in first + passed:
            cp.wait_send()
        for cp in mine:
            cp.wait()

    hbm = pl.BlockSpec(memory_space=pl.ANY)
    return pl.pallas_call(
        body, name="weights_all_gather",
        out_shape=[jax.ShapeDtypeStruct((N_DEV,) + b.shape, b.dtype) for b in blocks],
        in_specs=[hbm] * n, out_specs=[hbm] * n,
        scratch_shapes=[pltpu.SemaphoreType.DMA((n, 7)), pltpu.SemaphoreType.DMA((n, 7)),
                        pltpu.SemaphoreType.DMA((n,))],
    )(*blocks)


def _grad_exchange(slabs, spack):
    n = len(slabs) + 1

    def body(*refs):
        in_refs, out_refs = refs[:n], refs[n:2 * n]
        send_sems, recv_sems, local_sems = refs[2 * n:]
        x, y, c = _my_pos()
        my_id = 4 * x + 2 * y + c

        def src_of(a, dev):
            return in_refs[a] if a == n - 1 else in_refs[a].at[dev]

        own = [pltpu.make_async_copy(src_of(a, my_id), out_refs[a].at[my_id], local_sems.at[a])
               for a in range(n)]
        for cp in own:
            cp.start()
        sends, arrivals = [], []
        for k in range(1, N_DEV):
            px, py, pc = x ^ (k >> 2), y ^ ((k >> 1) & 1), c ^ (k & 1)
            peer_id = 4 * px + 2 * py + pc
            for a in range(n):
                for dst_slot, bucket in ((my_id, sends), (peer_id, arrivals)):
                    bucket.append(pltpu.make_async_remote_copy(
                        src_ref=src_of(a, peer_id), dst_ref=out_refs[a].at[dst_slot],
                        send_sem=send_sems.at[a, k - 1], recv_sem=recv_sems.at[a, k - 1],
                        device_id=(px, py, pc), device_id_type=MESH))
        for cp in sends:
            cp.start()
        for cp in arrivals:
            cp.wait_recv()
        for cp in sends:
            cp.wait_send()
        for cp in own:
            cp.wait()

    hbm = pl.BlockSpec(memory_space=pl.ANY)
    return pl.pallas_call(
        body, name="grad_exchange",
        out_shape=[jax.ShapeDtypeStruct(g.shape, g.dtype) for g in slabs]
        + [jax.ShapeDtypeStruct((N_DEV,) + spack.shape, spack.dtype)],
        in_specs=[hbm] * n, out_specs=[hbm] * n,
        scratch_shapes=[pltpu.SemaphoreType.DMA((n, 7)), pltpu.SemaphoreType.DMA((n, 7)),
                        pltpu.SemaphoreType.DMA((n,))],
    )(*slabs, spack)


_HBM = pl.BlockSpec(memory_space=pltpu.HBM)
_SEM = pl.BlockSpec(memory_space=pltpu.SEMAPHORE)
_EFFECT = pltpu.SideEffectType.DATAFLOW_SIDE_EFFECTING


def _my_id():
    x, y, c = _my_pos()
    return 4 * x + 2 * y + c


def _peer_copies(src_refs, land_refs, send_sems, recv_sems, per_peer):
    x, y, c = _my_pos()
    my_id = 4 * x + 2 * y + c
    copies = []
    for k in range(1, N_DEV):
        px, py, pc = x ^ (k >> 2), y ^ ((k >> 1) & 1), c ^ (k & 1)
        for a, (src, land) in enumerate(zip(src_refs, land_refs)):
            copies.append(pltpu.make_async_remote_copy(
                src_ref=src.at[4 * px + 2 * py + pc] if per_peer else src, dst_ref=land.at[my_id],
                send_sem=send_sems.at[a * (N_DEV - 1) + k - 1], recv_sem=recv_sems.at[a * (N_DEV - 1) + k - 1],
                device_id=(px, py, pc), device_id_type=MESH))
    return copies


def _exchange_start(srcs, per_peer, name):
    n = len(srcs)
    lands = [lax.empty(s.shape if per_peer else (N_DEV,) + s.shape, s.dtype) for s in srcs]

    def body(*refs):
        src_refs, land_refs = refs[:n], refs[n:2 * n]
        send_sems, recv_sems = refs[2 * n], refs[2 * n + 1]
        token = refs[-1]
        for cp in _peer_copies(src_refs, land_refs, send_sems, recv_sems, per_peer):
            cp.start()
        token[...] = jnp.zeros_like(token)

    outs = pl.pallas_call(
        body, name=name,
        out_shape=(pltpu.SemaphoreType.DMA((n * (N_DEV - 1),)), pltpu.SemaphoreType.DMA((n * (N_DEV - 1),)),
                   *[pltpu.HBM(a.shape, a.dtype) for a in srcs + lands],
                   jax.ShapeDtypeStruct((8, LANES), F32)),
        in_specs=[_HBM] * (2 * n),
        out_specs=(_SEM, _SEM, *[_HBM] * (2 * n), pl.BlockSpec(memory_space=pltpu.VMEM)),
        input_output_aliases={a: 2 + a for a in range(2 * n)},
        compiler_params=pltpu.CompilerParams(has_side_effects=_EFFECT),
    )(*[pltpu.with_memory_space_constraint(a, pltpu.HBM) for a in srcs + lands])
    return outs[0], outs[1], list(outs[2:2 + n]), list(outs[2 + n:2 + 2 * n]), outs[-1]


def _exchange_wait(handles, per_peer, after, name):
    send_sems, recv_sems, srcs, lands, _ = handles
    n = len(srcs)

    def body(*refs):
        src_refs, land_refs = refs[:n], refs[n:2 * n]
        for cp in _peer_copies(src_refs, land_refs, refs[2 * n], refs[2 * n + 1], per_peer):
            cp.wait_send()
            cp.wait_recv()

    outs = pl.pallas_call(
        body, name=name,
        out_shape=tuple(pltpu.HBM(a.shape, a.dtype) for a in srcs + lands),
        in_specs=[_HBM] * (2 * n) + [_SEM, _SEM, pl.BlockSpec(memory_space=pl.ANY)],
        out_specs=tuple([_HBM] * (2 * n)),
        input_output_aliases={a: a for a in range(2 * n)},
        compiler_params=pltpu.CompilerParams(has_side_effects=_EFFECT),
    )(*srcs, *lands, send_sems, recv_sems, after)
    me = _my_id()
    filled = []
    for src, land in zip(outs[:n], outs[n:]):
        own = lax.dynamic_index_in_dim(src, me, 0, keepdims=True) if per_peer else src[None]
        filled.append(lax.dynamic_update_slice_in_dim(land, own, me, 0))
    return filled


def _col_windows(n_shards, width, gap_at=None, gap=0):
    out = []
    for j in range(n_shards):
        g0, g1 = j * width, (j + 1) * width
        cuts = [g0, g1] if gap_at is None or not g0 < gap_at < g1 else [g0, gap_at, g1]
        for a, b in zip(cuts[:-1], cuts[1:]):
            out.append((j, a - g0, b - g0, a + (gap if gap_at is not None and a >= gap_at else 0)))
    return out


def _assemble_cols(parts, total, windows, name, tr=256):
    n, rows, w = parts.shape
    tr = _tile(rows, tr, 16)

    def body(p_ref, o_ref):
        o_ref[...] = jnp.zeros_like(o_ref)
        for j, lo, hi, dst in windows:
            o_ref[:, dst:dst + hi - lo] = p_ref[j, :, lo:hi]

    return pl.pallas_call(
        body, name=name, grid=(rows // tr,),
        in_specs=[pl.BlockSpec((n, tr, w), lambda i: (0, i, 0))],
        out_specs=pl.BlockSpec((tr, total), lambda i: (i, 0)),
        out_shape=jax.ShapeDtypeStruct((rows, total), parts.dtype),
        compiler_params=_cparams(("parallel",)),
    )(parts)


def _split_cols(full, n, w, windows, name, tr=256):
    rows, total = full.shape
    tr = _tile(rows, tr, 16)

    def body(f_ref, o_ref):
        for j, lo, hi, dst in windows:
            o_ref[j, :, lo:hi] = f_ref[:, dst:dst + hi - lo].astype(o_ref.dtype)

    return pl.pallas_call(
        body, name=name, grid=(rows // tr,),
        in_specs=[pl.BlockSpec((tr, total), lambda i: (i, 0))],
        out_specs=pl.BlockSpec((n, tr, w), lambda i: (0, i, 0)),
        out_shape=jax.ShapeDtypeStruct((n, rows, w), BF16),
        compiler_params=_cparams(("parallel",)),
    )(full)


_DIMS = {"nn": (((1,), (0,)), ((), ())), "nt": (((1,), (1,)), ((), ())), "tn": (((0,), (0,)), ((), ()))}


def _matmul(a, b, *, mode, grid, a_block, a_map, b_block, b_map, o_block, o_map, out_shape, name,
            resid=None):
    nk = grid[2]
    dims = _DIMS[mode]

    def body(*refs):
        if resid is None:
            a_ref, b_ref, o_ref, acc_ref = refs
            r_ref = None
        else:
            a_ref, b_ref, r_ref, o_ref, acc_ref = refs
        k = pl.program_id(2)

        @pl.when(k == 0)
        def _():
            acc_ref[...] = jnp.zeros_like(acc_ref)

        acc_ref[...] += lax.dot_general(a_ref[...], b_ref[...], dims, preferred_element_type=F32)

        @pl.when(k == nk - 1)
        def _():
            res = acc_ref[...]
            if r_ref is not None:
                res = r_ref[...] + res
            o_ref[...] = res.astype(o_ref.dtype)

    in_specs = [pl.BlockSpec(a_block, a_map), pl.BlockSpec(b_block, b_map)]
    args = [a, b]
    if resid is not None:
        in_specs.append(pl.BlockSpec(o_block, o_map))
        args.append(resid)
    acc_shape = tuple(d for d in o_block if d is not None)
    return pl.pallas_call(
        body, name=name, grid=grid, in_specs=in_specs,
        out_specs=pl.BlockSpec(o_block, o_map), out_shape=out_shape,
        scratch_shapes=[pltpu.VMEM(acc_shape, F32)],
        compiler_params=_cparams(("parallel", "parallel", "arbitrary")),
    )(*args)


def _mm_nn(a, b, out_dtype, name, resid=None, tm=1024, tn=1024, tk=1024):
    m, kk = a.shape
    n = b.shape[1]
    tm, tn, tk = _tile(m, tm, 8), _tile(n, tn), _tile(kk, tk)
    return _matmul(a, b, mode="nn", grid=(m // tm, n // tn, kk // tk),
                   a_block=(tm, tk), a_map=lambda i, j, k: (i, k),
                   b_block=(tk, tn), b_map=lambda i, j, k: (k, j),
                   o_block=(tm, tn), o_map=lambda i, j, k: (i, j),
                   out_shape=jax.ShapeDtypeStruct((m, n), out_dtype), name=name, resid=resid)


def _mm_nt(a, b, out_dtype, name, tm=1024, tn=1024, tk=1024):
    m, kk = a.shape
    n = b.shape[0]
    tm, tn, tk = _tile(m, tm, 8), _tile(n, tn), _tile(kk, tk)
    return _matmul(a, b, mode="nt", grid=(m // tm, n // tn, kk // tk),
                   a_block=(tm, tk), a_map=lambda i, j, k: (i, k),
                   b_block=(tn, tk), b_map=lambda i, j, k: (j, k),
                   o_block=(tm, tn), o_map=lambda i, j, k: (i, j),
                   out_shape=jax.ShapeDtypeStruct((m, n), out_dtype), name=name)


def _mm_tn(a, b, name, tm=1024, tn=1024, tk=1024):
    kk, m = a.shape
    n = b.shape[1]
    tm, tn, tk = _tile(m, tm), _tile(n, tn), _tile(kk, tk, 8)
    return _matmul(a, b, mode="tn", grid=(m // tm, n // tn, kk // tk),
                   a_block=(tk, tm), a_map=lambda i, j, k: (k, i),
                   b_block=(tk, tn), b_map=lambda i, j, k: (k, j),
                   o_block=(tm, tn), o_map=lambda i, j, k: (i, j),
                   out_shape=jax.ShapeDtypeStruct((m, n), F32), name=name)


def _mm_heads(a, b, name, tm=1024, tn=640):
    m, kk = a.shape
    n = b.shape[1]
    tm, tn = _tile(m, tm, 16), _tile(n, tn)
    per_tile = tn // HEAD_DIM

    def body(a_ref, b_ref, o_ref):
        res = jnp.dot(a_ref[...], b_ref[...], preferred_element_type=F32)
        for hh in range(per_tile):
            o_ref[hh] = res[:, hh * HEAD_DIM:(hh + 1) * HEAD_DIM].astype(o_ref.dtype)

    return pl.pallas_call(
        body, name=name, grid=(m // tm, n // tn),
        in_specs=[pl.BlockSpec((tm, kk), lambda i, j: (i, 0)), pl.BlockSpec((kk, tn), lambda i, j: (0, j))],
        out_specs=pl.BlockSpec((per_tile, tm, HEAD_DIM), lambda i, j: (j, i, 0)),
        out_shape=jax.ShapeDtypeStruct((n // HEAD_DIM, m, HEAD_DIM), BF16),
        compiler_params=_cparams(("parallel", "parallel")),
    )(a, b)


def _mm_up(h, w_up, tm=2048, tn=256):
    s = h.shape[0]
    tm = _tile(s, tm, 8)
    nh = D_FF // tn
    return _matmul(h, w_up, mode="nn", grid=(s // tm, 2 * nh, 1),
                   a_block=(tm, D_MODEL), a_map=lambda i, j, k: (i, 0),
                   b_block=(D_MODEL, tn), b_map=lambda i, j, k: (0, j),
                   o_block=(None, tm, tn), o_map=lambda i, j, k: (j // nh, i, j % nh),
                   out_shape=jax.ShapeDtypeStruct((2, s, D_FF), F32), name="up_proj")


def _mm_dup_nt(dup, w_up, tm=1024, tk=1408):
    s = dup.shape[1]
    tm = _tile(s, tm, 8)
    nh = D_FF // tk
    return _matmul(dup, w_up, mode="nt", grid=(s // tm, 1, 2 * nh),
                   a_block=(None, tm, tk), a_map=lambda i, j, k: (k // nh, i, k % nh),
                   b_block=(D_MODEL, tk), b_map=lambda i, j, k: (0, k),
                   o_block=(tm, D_MODEL), o_map=lambda i, j, k: (i, 0),
                   out_shape=jax.ShapeDtypeStruct((s, D_MODEL), F32), name="d_h2")


def _mm_dwup_tn(h, dup, tn=1408, tk=1024):
    s = h.shape[0]
    tk = _tile(s, tk, 8)
    nh = D_FF // tn
    return _matmul(h, dup, mode="tn", grid=(1, 2 * nh, s // tk),
                   a_block=(tk, D_MODEL), a_map=lambda i, j, k: (k, 0),
                   b_block=(None, tk, tn), b_map=lambda i, j, k: (j // nh, k, j % nh),
                   o_block=(D_MODEL, tn), o_map=lambda i, j, k: (0, j),
                   out_shape=jax.ShapeDtypeStruct((D_MODEL, 2 * D_FF), F32), name="d_w_up")


def _rms_fwd(x, g, tr=512):
    s, d = x.shape
    tr = _tile(s, tr, 8)

    def body(x_ref, g_ref, o_ref):
        xv = x_ref[...]
        r = lax.rsqrt(jnp.mean(xv * xv, axis=-1, keepdims=True) + EPS)
        o_ref[...] = (xv * r * g_ref[...]).astype(o_ref.dtype)

    return pl.pallas_call(
        body, name="rms_fwd", grid=(s // tr,),
        in_specs=[pl.BlockSpec((tr, d), lambda i: (i, 0)), pl.BlockSpec((1, d), lambda i: (0, 0))],
        out_specs=pl.BlockSpec((tr, d), lambda i: (i, 0)),
        out_shape=jax.ShapeDtypeStruct((s, d), BF16),
        compiler_params=_cparams(("parallel",)),
    )(x, g)


def _group_rms_fwd(o_fox, o_sb, g_fox, g_sb, tr=512):
    nh, s, dh = o_fox.shape
    tr = _tile(s, tr, 8)

    def body(a_ref, b_ref, ga_ref, gb_ref, o_ref):
        for src, g_ref, lo in ((a_ref, ga_ref, 0), (b_ref, gb_ref, nh * dh)):
            heads = [src[hh] for hh in range(nh)]
            ss = heads[0] * heads[0]
            for xv in heads[1:]:
                ss = ss + xv * xv
            r = lax.rsqrt(jnp.sum(ss, axis=-1, keepdims=True) * (1.0 / (nh * dh)) + EPS)
            for hh, xv in enumerate(heads):
                o_ref[:, lo + hh * dh:lo + (hh + 1) * dh] = (xv * r * g_ref[hh]).astype(o_ref.dtype)

    heads_blk = pl.BlockSpec((nh, tr, dh), lambda i: (0, i, 0))
    gain = pl.BlockSpec((nh, 1, dh), lambda i: (0, 0, 0))
    return pl.pallas_call(
        body, name="group_rms_fwd", grid=(s // tr,),
        in_specs=[heads_blk, heads_blk, gain, gain],
        out_specs=pl.BlockSpec((tr, 2 * nh * dh), lambda i: (i, 0)),
        out_shape=jax.ShapeDtypeStruct((s, 2 * nh * dh), BF16),
        compiler_params=_cparams(("parallel",)),
    )(o_fox, o_sb, g_fox, g_sb)


def _group_rms_bwd(x, dy, g, *, dy_col, name, tr=512):
    nh, s, dh = x.shape
    tr = _tile(s, tr, 8)
    d = nh * dh

    def body(x_ref, dy_ref, g_ref, dx_ref, dg_ref):
        @pl.when(pl.program_id(0) == 0)
        def _():
            dg_ref[...] = jnp.zeros_like(dg_ref)

        dyv = dy_ref[...]
        xs_ = [x_ref[hh] for hh in range(nh)]
        dys = [dyv[:, hh * dh:(hh + 1) * dh] for hh in range(nh)]
        ss = xs_[0] * xs_[0]
        for xv in xs_[1:]:
            ss = ss + xv * xv
        r = lax.rsqrt(jnp.sum(ss, axis=-1, keepdims=True) * (1.0 / d) + EPS)
        xh = [xv * r for xv in xs_]
        gy = [dys[hh] * g_ref[hh] for hh in range(nh)]
        dot = xh[0] * gy[0]
        for hh in range(1, nh):
            dot = dot + xh[hh] * gy[hh]
        mean_dot = jnp.sum(dot, axis=-1, keepdims=True) * (1.0 / d)
        for hh in range(nh):
            dx_ref[hh] = r * (gy[hh] - xh[hh] * mean_dot)
            dg_ref[hh] += jnp.sum(dys[hh] * xh[hh], axis=0, keepdims=True)

    heads_blk = pl.BlockSpec((nh, tr, dh), lambda i: (0, i, 0))
    gain = pl.BlockSpec((nh, 1, dh), lambda i: (0, 0, 0))
    return pl.pallas_call(
        body, name=name, grid=(s // tr,),
        in_specs=[heads_blk, pl.BlockSpec((tr, d), lambda i: (i, dy_col)), gain],
        out_specs=[heads_blk, gain],
        out_shape=[jax.ShapeDtypeStruct((nh, s, dh), F32), jax.ShapeDtypeStruct((nh, 1, dh), F32)],
        compiler_params=_cparams(("arbitrary",)),
    )(x, dy, g)


def _merge_dproj(parts_fox, d_gate, parts_sb, tr=256):
    nh, s, dh = parts_fox[0].shape
    tr = _tile(s, tr, 16)

    def body(*refs):
        o_ref = refs[-1]
        gate_ref = refs[3]
        col = 0
        for ref in refs[:3]:
            for hh in range(nh):
                o_ref[:, col:col + dh] = ref[hh].astype(o_ref.dtype)
                col += dh
        o_ref[:, col:col + GATE_PAD] = jnp.zeros((tr, GATE_PAD), o_ref.dtype)
        o_ref[:, col:col + N_GROUP_HEADS] = gate_ref[...].astype(o_ref.dtype)
        col += GATE_PAD
        for ref in refs[4:7]:
            for hh in range(nh):
                o_ref[:, col:col + dh] = ref[hh].astype(o_ref.dtype)
                col += dh

    heads_blk = pl.BlockSpec((nh, tr, dh), lambda i: (0, i, 0))
    return pl.pallas_call(
        body, name="merge_d_proj", grid=(s // tr,),
        in_specs=[heads_blk] * 3 + [pl.BlockSpec((tr, N_GROUP_HEADS), lambda i: (i, 0))] + [heads_blk] * 3,
        out_specs=pl.BlockSpec((tr, IN_COLS_PAD), lambda i: (i, 0)),
        out_shape=jax.ShapeDtypeStruct((s, IN_COLS_PAD), BF16),
        compiler_params=_cparams(("parallel",)),
    )(*parts_fox, d_gate, *parts_sb)


def _rms_bwd(x, dy, g, resid, *, dy_col, name, want_bf16, tr=512):
    s, d = x.shape
    tr = _tile(s, tr, 8)
    has_resid = resid is not None

    def body(*refs):
        refs = list(refs)
        x_ref, dy_ref, g_ref = refs[:3]
        r_ref = refs[3] if has_resid else None
        outs = refs[4:] if has_resid else refs[3:]
        dx_ref = outs[0]
        dxb_ref = outs[1] if want_bf16 else None
        dg_ref = outs[-1]

        @pl.when(pl.program_id(0) == 0)
        def _():
            dg_ref[...] = jnp.zeros_like(dg_ref)

        xv = x_ref[...]
        dyv = dy_ref[...]
        r = lax.rsqrt(jnp.mean(xv * xv, axis=-1, keepdims=True) + EPS)
        xh = xv * r
        gy = dyv * g_ref[...]
        dx = r * (gy - xh * jnp.mean(xh * gy, axis=-1, keepdims=True))
        if r_ref is not None:
            dx = r_ref[...] + dx
        dx_ref[...] = dx
        if dxb_ref is not None:
            dxb_ref[...] = dx.astype(BF16)
        dg_ref[...] += jnp.sum(dyv * xh, axis=0, keepdims=True)

    row = pl.BlockSpec((tr, d), lambda i: (i, 0))
    in_specs = [row, pl.BlockSpec((tr, d), lambda i: (i, dy_col)), pl.BlockSpec((1, d), lambda i: (0, 0))]
    args = [x, dy, g]
    if has_resid:
        in_specs.append(row)
        args.append(resid)
    out_specs = [row]
    out_shape = [jax.ShapeDtypeStruct((s, d), F32)]
    if want_bf16:
        out_specs.append(row)
        out_shape.append(jax.ShapeDtypeStruct((s, d), BF16))
    out_specs.append(pl.BlockSpec((1, d), lambda i: (0, 0)))
    out_shape.append(jax.ShapeDtypeStruct((1, d), F32))
    return pl.pallas_call(
        body, name=name, grid=(s // tr,), in_specs=in_specs, out_specs=out_specs, out_shape=out_shape,
        compiler_params=_cparams(("arbitrary",)),
    )(*args)


def _loss_head(x2, target, g, tr=512):
    s, d = x2.shape
    tr = _tile(s, tr, 8)

    def body(x_ref, t_ref, g_ref, dx_ref, dxb_ref, dg_ref, loss_ref):
        @pl.when(pl.program_id(0) == 0)
        def _():
            dg_ref[...] = jnp.zeros_like(dg_ref)
            loss_ref[...] = jnp.zeros_like(loss_ref)

        xv = x_ref[...]
        gv = g_ref[...]
        r = lax.rsqrt(jnp.mean(xv * xv, axis=-1, keepdims=True) + EPS)
        xh = xv * r
        err = xh * gv - t_ref[...]
        loss_ref[...] += jnp.sum(jnp.mean(err * err, axis=-1, keepdims=True), axis=0, keepdims=True) * 0.5
        dyv = err * (1.0 / d)
        gy = dyv * gv
        dx = r * (gy - xh * jnp.mean(xh * gy, axis=-1, keepdims=True))
        dx_ref[...] = dx
        dxb_ref[...] = dx.astype(BF16)
        dg_ref[...] += jnp.sum(dyv * xh, axis=0, keepdims=True)

    row = pl.BlockSpec((tr, d), lambda i: (i, 0))
    return pl.pallas_call(
        body, name="loss_head", grid=(s // tr,),
        in_specs=[row, row, pl.BlockSpec((1, d), lambda i: (0, 0))],
        out_specs=[row, row, pl.BlockSpec((1, d), lambda i: (0, 0)), pl.BlockSpec((1, LANES), lambda i: (0, 0))],
        out_shape=[jax.ShapeDtypeStruct((s, d), F32), jax.ShapeDtypeStruct((s, d), BF16),
                   jax.ShapeDtypeStruct((1, d), F32), jax.ShapeDtypeStruct((1, LANES), F32)],
        compiler_params=_cparams(("arbitrary",)),
    )(x2, target, g)


def _conv_taps(cur, prev8, w, b, first):
    prev8 = jnp.where(first, 0.0, prev8)
    ext = jnp.concatenate([prev8, cur], axis=0)
    x1 = pltpu.roll(ext, 1, 0)[8:]
    x2 = pltpu.roll(ext, 2, 0)[8:]
    u = b + w[0:1] * x2
    u = u + w[1:2] * x1
    u = u + w[2:3] * cur
    return u, x1, x2


def _conv_gate_fwd(up, conv_w, conv_b, tm=2048, tn=256):
    s = up.shape[1]
    tm = _tile(s, tm, 8)
    nrb = s // tm
    rb8 = tm // 8

    def body(g_ref, v_ref, gp_ref, vp_ref, wg_ref, wv_ref, bg_ref, bv_ref, o_ref):
        first = pl.program_id(1) == 0
        ug, _, _ = _conv_taps(g_ref[...], gp_ref[...], wg_ref[...], bg_ref[...], first)
        uv, _, _ = _conv_taps(v_ref[...], vp_ref[...], wv_ref[...], bv_ref[...], first)
        sg = 1.0 / (1.0 + jnp.exp(-ug))
        o_ref[...] = (ug * sg * uv).astype(o_ref.dtype)

    def cur(h):
        return pl.BlockSpec((None, tm, tn), lambda j, i: (h, i, j))

    def prev(h):
        return pl.BlockSpec((None, 8, tn), lambda j, i: (h, jnp.maximum(i * rb8 - 1, 0), j))

    def par(h, r):
        return pl.BlockSpec((None, r, tn), lambda j, i: (h, 0, j))

    return pl.pallas_call(
        body, name="conv_gate_fwd", grid=(D_FF // tn, nrb),
        in_specs=[cur(0), cur(1), prev(0), prev(1), par(0, 3), par(1, 3), par(0, 1), par(1, 1)],
        out_specs=pl.BlockSpec((tm, tn), lambda j, i: (i, j)),
        out_shape=jax.ShapeDtypeStruct((s, D_FF), BF16),
        compiler_params=_cparams(("parallel", "parallel")),
    )(up, up, up, up, conv_w, conv_w, conv_b, conv_b)


def _conv_gate_bwd(up, dact, conv_w, conv_b, tm=1024, tn=256):
    s = up.shape[1]
    tm = _tile(s, tm, 8)
    nrb = s // tm
    rb8 = tm // 8

    def body(g_ref, v_ref, gp_ref, vp_ref, da_ref, wg_ref, wv_ref, bg_ref, bv_ref,
             dup_ref, dcw_ref, dcb_ref, carry_ref):
        i = pl.program_id(1)
        first = i == nrb - 1

        @pl.when(i == 0)
        def _():
            carry_ref[...] = jnp.zeros_like(carry_ref)
            dcw_ref[...] = jnp.zeros_like(dcw_ref)
            dcb_ref[...] = jnp.zeros_like(dcb_ref)

        curs = (g_ref[...], v_ref[...])
        ws = (wg_ref[...], wv_ref[...])
        ug, g1, g2 = _conv_taps(curs[0], gp_ref[...], ws[0], bg_ref[...], first)
        uv, v1, v2 = _conv_taps(curs[1], vp_ref[...], ws[1], bv_ref[...], first)
        sg = 1.0 / (1.0 + jnp.exp(-ug))
        da = da_ref[...].astype(F32)
        d_v = da * (ug * sg)
        d_g = da * uv * (sg * (1.0 + ug * (1.0 - sg)))
        for h, (du, x0, x1, x2) in enumerate(((d_g, curs[0], g1, g2), (d_v, curs[1], v1, v2))):
            dcb_ref[h] += jnp.sum(du, axis=0, keepdims=True)
            dcw_ref[h, 0:1, :] += jnp.sum(du * x2, axis=0, keepdims=True)
            dcw_ref[h, 1:2, :] += jnp.sum(du * x1, axis=0, keepdims=True)
            dcw_ref[h, 2:3, :] += jnp.sum(du * x0, axis=0, keepdims=True)
            ext = jnp.concatenate([du, carry_ref[h]], axis=0)
            n1 = pltpu.roll(ext, tm + 7, 0)[:tm]
            n2 = pltpu.roll(ext, tm + 6, 0)[:tm]
            w = ws[h]
            dup_ref[h] = (w[2:3] * du + w[1:2] * n1 + w[0:1] * n2).astype(dup_ref.dtype)
            carry_ref[h] = du[:8]

    def cur(h):
        return pl.BlockSpec((None, tm, tn), lambda j, i: (h, nrb - 1 - i, j))

    def prev(h):
        return pl.BlockSpec((None, 8, tn), lambda j, i: (h, jnp.maximum((nrb - 1 - i) * rb8 - 1, 0), j))

    def par(h, r):
        return pl.BlockSpec((None, r, tn), lambda j, i: (h, 0, j))

    return pl.pallas_call(
        body, name="conv_gate_bwd", grid=(D_FF // tn, nrb),
        in_specs=[cur(0), cur(1), prev(0), prev(1),
                  pl.BlockSpec((tm, tn), lambda j, i: (nrb - 1 - i, j)),
                  par(0, 3), par(1, 3), par(0, 1), par(1, 1)],
        out_specs=[pl.BlockSpec((2, tm, tn), lambda j, i: (0, nrb - 1 - i, j)),
                   pl.BlockSpec((2, 3, tn), lambda j, i: (0, 0, j)),
                   pl.BlockSpec((2, 1, tn), lambda j, i: (0, 0, j))],
        out_shape=[jax.ShapeDtypeStruct((2, s, D_FF), BF16),
                   jax.ShapeDtypeStruct((2, 3, D_FF), F32),
                   jax.ShapeDtypeStruct((2, 1, D_FF), F32)],
        scratch_shapes=[pltpu.VMEM((2, 8, tn), F32)],
        compiler_params=_cparams(("parallel", "arbitrary")),
    )(up, up, up, up, dact, conv_w, conv_w, conv_b, conv_b)


def _split_dot(x, tri, terms):
    piece = x.astype(BF16)
    out = jnp.dot(piece, tri, preferred_element_type=F32)
    rest = x
    for _ in range(terms - 1):
        rest = rest - piece.astype(F32)
        piece = rest.astype(BF16)
        out = out + jnp.dot(piece, tri, preferred_element_type=F32)
    return out


def _split_dot_rhs(tri, x, terms):
    piece = x.astype(BF16)
    out = jnp.dot(tri, piece, preferred_element_type=F32)
    rest = x
    for _ in range(terms - 1):
        rest = rest - piece.astype(F32)
        piece = rest.astype(BF16)
        out = out + jnp.dot(tri, piece, preferred_element_type=F32)
    return out


def _tri(n, kind):
    r = lax.broadcasted_iota(jnp.int32, (n, n), 0)
    c = lax.broadcasted_iota(jnp.int32, (n, n), 1)
    cond = {"le": r <= c, "ge": r >= c, "lt": r < c, "gt": r > c}[kind]
    return jnp.where(cond, 1.0, 0.0).astype(BF16)


def _log_sigmoid(x):
    return jnp.minimum(x, 0.0) - jnp.log(1.0 + jnp.exp(-jnp.abs(x)))


def _forget_fwd(f_logit, bias):
    h, r, _ = f_logit.shape

    def body(x_ref, b_ref, o_ref):
        lf = _log_sigmoid(x_ref[...] + b_ref[...])
        within = _split_dot(lf, _tri(LANES, "le"), 3)
        row_tot = jnp.broadcast_to(within[:, LANES - 1:LANES], (r, LANES))
        before = _split_dot_rhs(_tri(r, "gt"), row_tot, 3)
        o_ref[...] = within + before

    blk = pl.BlockSpec((None, r, LANES), lambda i: (i, 0, 0))
    return pl.pallas_call(
        body, name="forget_cumsum_fwd", grid=(h,),
        in_specs=[blk, pl.BlockSpec((None, 1, LANES), lambda i: (i, 0, 0))],
        out_specs=blk, out_shape=jax.ShapeDtypeStruct((h, r, LANES), F32),
        compiler_params=_cparams(("parallel",)),
    )(f_logit, bias)


def _forget_bwd(f_logit, bias, ksum, qsum):
    h, r, _ = f_logit.shape

    def body(x_ref, b_ref, k_ref, q_ref, dx_ref, db_ref):
        d_f = q_ref[...] - k_ref[...]
        within = _split_dot(d_f, _tri(LANES, "ge"), 3)
        row_tot = jnp.broadcast_to(within[:, 0:1], (r, LANES))
        after = _split_dot_rhs(_tri(r, "lt"), row_tot, 3)
        xv = x_ref[...] + b_ref[...]
        dx = (within + after) * jnp.exp(_log_sigmoid(-xv))
        dx_ref[...] = dx
        db_ref[...] = jnp.broadcast_to(jnp.sum(dx), (1, LANES))

    blk = pl.BlockSpec((None, r, LANES), lambda i: (i, 0, 0))
    one = pl.BlockSpec((None, 1, LANES), lambda i: (i, 0, 0))
    return pl.pallas_call(
        body, name="forget_cumsum_bwd", grid=(h,),
        in_specs=[blk, one, blk, blk], out_specs=[blk, one],
        out_shape=[jax.ShapeDtypeStruct((h, r, LANES), F32), jax.ShapeDtypeStruct((h, 1, LANES), F32)],
        compiler_params=_cparams(("parallel",)),
    )(f_logit, bias, ksum, qsum)


def _head_specs(s, tq):
    qblk = pl.BlockSpec((None, tq, HEAD_DIM), lambda h, i: (h, i, 0))
    full = pl.BlockSpec((None, s, HEAD_DIM), lambda h, i: (h, 0, 0))
    col = pl.BlockSpec((None, tq, 1), lambda h, i: (h, i, 0))
    return qblk, full, col


def _qkv_specs(s, tq, offs):
    q_off, k_off, v_off = offs
    return (pl.BlockSpec((None, tq, HEAD_DIM), lambda h, i: (h + q_off, i, 0)),
            pl.BlockSpec((None, s, HEAD_DIM), lambda h, i: (h + k_off, 0, 0)),
            pl.BlockSpec((None, s, HEAD_DIM), lambda h, i: (h + v_off, 0, 0)))


def _scaled(q_ref):
    return (q_ref[...].astype(F32) * Q_SCALE).astype(BF16)


_NT = (((1,), (1,)), ((), ()))
_TN = (((0,), (0,)), ((), ()))


def _cols_minus_rows(rows, cols):
    return lax.broadcasted_iota(jnp.int32, (rows, cols), 1) - lax.broadcasted_iota(jnp.int32, (rows, cols), 0)


def _fox_fwd(qkv, offs, v_ones, f_row, tq, tk):
    h, s = N_GROUP_HEADS, qkv.shape[1]
    nk = s // tk
    assert tq == tk

    def body(q_ref, k_ref, v_ref, fr_ref, o_ref, lse_ref, m_ref, acc_ref, z0, z1):
        i = pl.program_id(1)
        qs = _scaled(q_ref)
        m_ref[...] = jnp.full_like(m_ref, NEG_BIG)
        acc_ref[...] = jnp.zeros_like(acc_ref)

        ahead = _cols_minus_rows(tq, tk)

        def block_of(j):
            return jnp.minimum(j, nk - 1)

        def keys_of(j):
            return pl.ds(pl.multiple_of(block_of(j) * tk, tk), tk)

        def logits(j):
            return lax.dot_general(qs, k_ref[keys_of(j), :], _NT, preferred_element_type=F32)

        def soft(j, raw, masked):
            sc = raw - fr_ref[block_of(j)]
            if masked:
                sc = jnp.where(ahead <= (i - j) * tk, sc, NEG_BIG)
            m_old = m_ref[...]
            m_new = jnp.maximum(m_old, jnp.max(sc, axis=-1, keepdims=True))
            p = jnp.exp(sc - m_new)
            acc_ref[...] = jnp.exp(m_old - m_new) * acc_ref[...] + jnp.dot(
                p.astype(BF16), v_ref[keys_of(j), :], preferred_element_type=F32)
            m_ref[...] = m_new

        z0[...] = logits(0)

        def trip(p, masked):
            j = 2 * p
            z1[...] = logits(j + 1)
            soft(j, z0[...], masked)
            z0[...] = logits(j + 2)
            soft(j + 1, z1[...], masked)

        def step(p, carry):
            trip(p, False)
            return carry

        lax.fori_loop(0, i // 2, step, 0)
        trip(i // 2, True)
        l = acc_ref[:, HEAD_DIM:HEAD_DIM + 1]
        o_ref[...] = acc_ref[:, :HEAD_DIM] / l
        lse_ref[...] = m_ref[...] + jnp.log(l)

    qblk, full, colspec = _head_specs(s, tq)
    q_in, k_in, _ = _qkv_specs(s, tq, offs)
    return pl.pallas_call(
        body, name="fox_fwd", grid=(h, s // tq),
        in_specs=[q_in, k_in, pl.BlockSpec((None, s, 2 * HEAD_DIM), lambda hh, i: (hh, 0, 0)),
                  pl.BlockSpec((None, nk, 1, tk), lambda hh, i: (hh, 0, 0, 0))],
        out_specs=[qblk, colspec],
        out_shape=[jax.ShapeDtypeStruct((h, s, HEAD_DIM), F32), jax.ShapeDtypeStruct((h, s, 1), F32)],
        scratch_shapes=[pltpu.VMEM((tq, 1), F32), pltpu.VMEM((tq, 2 * HEAD_DIM), F32),
                        pltpu.VMEM((tq, tk), F32), pltpu.VMEM((tq, tk), F32)],
        compiler_params=_cparams(("parallel", "parallel")),
    )(qkv, qkv, v_ones, f_row)


def _fox_bwd(qkv, offs, v_ones, f_row, o, lse, d_o, tq, tk):
    h, s = N_GROUP_HEADS, qkv.shape[1]
    nk = s // tk
    assert tq == tk

    def body(q_ref, k_ref, v_ref, fr_ref, o_ref, lse_ref, do_ref,
             dq_ref, dk_ref, dv_ref, ks_ref, qs_ref, dq_acc, qsum_acc, z0, z1, p0, p1, doa_ref):
        i = pl.program_id(1)

        @pl.when(i == 0)
        def _():
            dk_ref[...] = jnp.zeros_like(dk_ref)
            dv_ref[...] = jnp.zeros_like(dv_ref)
            ks_ref[...] = jnp.zeros_like(ks_ref)

        qs = _scaled(q_ref)
        lse_v = lse_ref[...]
        dob = do_ref[...].astype(BF16)
        delta = jnp.sum(dob.astype(F32) * o_ref[...], axis=-1, keepdims=True)
        dq_acc[...] = jnp.zeros_like(dq_acc)
        qsum_acc[...] = jnp.zeros_like(qsum_acc)
        d_hi = delta.astype(BF16).astype(F32)
        d_mid = (delta - d_hi).astype(BF16).astype(F32)
        d_lo = (delta - d_hi - d_mid).astype(BF16).astype(F32)
        spare = lax.broadcasted_iota(jnp.int32, (tq, HEAD_DIM), 1)
        doa_ref[:, :HEAD_DIM] = dob
        doa_ref[:, HEAD_DIM:] = jnp.where(spare == 0, -d_hi, jnp.where(
            spare == 1, -d_mid, jnp.where(spare == 2, -d_lo, 0.0))).astype(BF16)

        ahead = _cols_minus_rows(tq, tk)

        def block_of(j):
            return jnp.minimum(j, nk - 1)

        def keys_of(j):
            return pl.ds(pl.multiple_of(block_of(j) * tk, tk), tk)

        def products(j):
            at = keys_of(j)
            return (lax.dot_general(qs, k_ref[at, :], _NT, preferred_element_type=F32),
                    lax.dot_general(doa_ref[...], v_ref[at, :], _NT, preferred_element_type=F32))

        def grads(j, raw, dp, masked):
            at = keys_of(j)
            sc = raw - fr_ref[block_of(j)]
            if masked:
                sc = jnp.where(ahead <= (i - j) * tk, sc, NEG_BIG)
            p = jnp.exp(sc - lse_v)
            ds = p * dp
            dsb = ds.astype(BF16)
            dq_acc[...] += jnp.dot(dsb, k_ref[at, :], preferred_element_type=F32)
            dk_ref[at, :] += lax.dot_general(dsb, qs, _TN, preferred_element_type=F32)
            dv_ref[at, :] += lax.dot_general(p.astype(BF16), dob, _TN, preferred_element_type=F32)
            ks_ref[block_of(j)] += jnp.sum(ds.reshape(tq // 8, 8, tk), axis=0)
            qsum_acc[...] += jnp.sum(ds, axis=-1, keepdims=True)

        z0[...], p0[...] = products(0)

        def trip(pp, masked):
            j = 2 * pp
            z1[...], p1[...] = products(j + 1)
            grads(j, z0[...], p0[...], masked)
            z0[...], p0[...] = products(j + 2)
            grads(j + 1, z1[...], p1[...], masked)

        def step(pp, carry):
            trip(pp, False)
            return carry

        lax.fori_loop(0, i // 2, step, 0)
        trip(i // 2, True)
        dq_ref[...] = dq_acc[...] * Q_SCALE
        qs_ref[...] = qsum_acc[...]

    qblk, full, colspec = _head_specs(s, tq)
    frow = pl.BlockSpec((None, nk, 1, tk), lambda hh, i: (hh, 0, 0, 0))
    big = pltpu.VMEM((tq, tk), F32)
    return pl.pallas_call(
        body, name="fox_bwd", grid=(h, s // tq),
        in_specs=[*_qkv_specs(s, tq, offs)[:2], pl.BlockSpec((None, s, 2 * HEAD_DIM), lambda hh, i: (hh, 0, 0)),
                  frow, qblk, colspec, qblk],
        out_specs=[qblk, full, full, pl.BlockSpec((None, nk, 8, tk), lambda hh, i: (hh, 0, 0, 0)), colspec],
        out_shape=[jax.ShapeDtypeStruct((h, s, HEAD_DIM), F32)] * 3
        + [jax.ShapeDtypeStruct((h, nk, 8, tk), F32), jax.ShapeDtypeStruct((h, s, 1), F32)],
        scratch_shapes=[pltpu.VMEM((tq, HEAD_DIM), F32), pltpu.VMEM((tq, 1), F32), big, big, big, big,
                        pltpu.VMEM((tq, 2 * HEAD_DIM), BF16)],
        compiler_params=_cparams(("parallel", "arbitrary")),
    )(qkv, qkv, v_ones, f_row, o, lse, d_o)


SB_TERMS = 2
G_TERMS = 1
LOG2E = 1.4426950408889634
LN2 = 0.6931471805599453


def _softplus2(z2):
    return jnp.maximum(z2, 0.0) + jnp.log2(1.0 + jnp.exp2(-jnp.abs(z2)))


def _sb_fwd(qkv, offs, tq, tk):
    h, s = N_GROUP_HEADS, qkv.shape[1]

    assert tq % (2 * tk) == 0

    def body(q_ref, k_ref, v_ref, o_ref, w_hbm, acc_ref, run_ref, z0, z1, d0, d1, t0, t1, w_stage, wsem):
        z_refs, d_refs, t_refs = (z0, z1), (d0, d1), (t0, t1)
        hh = pl.program_id(0)
        i = pl.program_id(1)
        qs = _scaled(q_ref)
        tri = _tri(tk, "ge")
        acc_ref[...] = jnp.zeros_like(acc_ref)
        run_ref[...] = jnp.zeros_like(run_ref)
        nb = (i + 1) * (tq // tk)
        ahead = _cols_minus_rows(tq, tk)

        def keys_of(b):
            j = nb - 1 - jnp.minimum(b, nb - 1)
            return pl.ds(pl.multiple_of(j * tk, tk), tk)

        def visible(b):
            return ahead < i * tq - (nb - 1 - b) * tk

        def logits(b, slot):
            z_refs[slot][...] = lax.dot_general(qs, k_ref[keys_of(b), :], _NT,
                                                preferred_element_type=F32) * LOG2E

        def sums(b, slot, masked):
            z2 = z_refs[slot][...]
            sp = _softplus2(z2)
            if masked:
                sp = jnp.where(visible(b), sp, 0.0)
            inc = _split_dot(sp, tri, SB_TERMS)
            d_refs[slot][...] = z2 - inc
            t_refs[slot][...] = inc[:, 0:1]

        def put(p, slot):
            st = (p % 2) * 2 + slot
            return pltpu.make_async_copy(w_stage.at[st], w_hbm.at[hh, i, nb - 1 - (2 * p + slot)], wsem.at[st])

        def weigh(p, slot, masked):
            b = 2 * p + slot
            w = jnp.exp2(d_refs[slot][...] - run_ref[...])
            if masked:
                w = jnp.where(visible(b), w, 0.0)
            wb = w.astype(BF16)
            w_stage[(p % 2) * 2 + slot] = wb
            acc_ref[...] += jnp.dot(wb, v_ref[keys_of(b), :], preferred_element_type=F32)
            run_ref[...] += t_refs[slot][...]

        def trip(p, masked):
            @pl.when(p >= 2)
            def _():
                put(p - 2, 0).wait()
                put(p - 2, 1).wait()

            b = 2 * p
            logits(b + 2, 0)
            sums(b + 1, 1, masked)
            weigh(p, 0, masked)
            logits(b + 3, 1)
            sums(b + 2, 0, masked)
            weigh(p, 1, masked)
            put(p, 0).start()
            put(p, 1).start()

        logits(0, 0)
        logits(1, 1)
        sums(0, 0, True)

        def guarded(p, carry):
            trip(p, True)
            return carry

        def plain(p, carry):
            trip(p, False)
            return carry

        lax.fori_loop(0, tq // tk // 2, guarded, 0)
        lax.fori_loop(tq // tk // 2, nb // 2, plain, 0)
        trips = nb // 2

        @pl.when(trips >= 2)
        def _():
            put(trips - 2, 0).wait()
            put(trips - 2, 1).wait()

        put(trips - 1, 0).wait()
        put(trips - 1, 1).wait()
        o_ref[...] = acc_ref[...]

    qblk, full, colspec = _head_specs(s, tq)
    return pl.pallas_call(
        body, name="sb_fwd", grid=(h, s // tq),
        in_specs=[*_qkv_specs(s, tq, offs)], out_specs=[qblk, pl.BlockSpec(memory_space=pl.ANY)],
        out_shape=[jax.ShapeDtypeStruct((h, s, HEAD_DIM), F32),
                   jax.ShapeDtypeStruct((h, s // tq, s // tk, tq, tk), BF16)],
        scratch_shapes=[pltpu.VMEM((tq, HEAD_DIM), F32), pltpu.VMEM((tq, 1), F32),
                        pltpu.VMEM((tq, tk), F32), pltpu.VMEM((tq, tk), F32),
                        pltpu.VMEM((tq, tk), F32), pltpu.VMEM((tq, tk), F32),
                        pltpu.VMEM((tq, 1), F32), pltpu.VMEM((tq, 1), F32),
                        pltpu.VMEM((4, tq, tk), BF16), pltpu.SemaphoreType.DMA((4,))],
        compiler_params=_cparams(("parallel", "parallel")),
    )(qkv, qkv, qkv)


def _sb_bwd(qkv, offs, w_saved, d_o, tq, tk):
    h, s = N_GROUP_HEADS, qkv.shape[1]

    assert tq % (2 * tk) == 0

    def body(q_ref, k_ref, v_ref, do_ref, w_hbm, dq_ref, dk_ref, dv_ref, dq_acc, grun_ref,
             z0, z1, p0, p1, w_bufs, wsem):
        z_refs, p_refs = (z0, z1), (p0, p1)
        hh = pl.program_id(0)
        i = pl.program_id(1)

        @pl.when(i == 0)
        def _():
            dk_ref[...] = jnp.zeros_like(dk_ref)
            dv_ref[...] = jnp.zeros_like(dv_ref)

        qs = _scaled(q_ref)
        dob = do_ref[...].astype(BF16)
        tri = _tri(tk, "le")
        dq_acc[...] = jnp.zeros_like(dq_acc)
        grun_ref[...] = jnp.zeros_like(grun_ref)
        nb = (i + 1) * (tq // tk)
        ahead = _cols_minus_rows(tq, tk)

        def block_of(b):
            return jnp.minimum(b, nb - 1)

        def keys_of(b):
            return pl.ds(pl.multiple_of(block_of(b) * tk, tk), tk)

        def visible(b):
            return ahead < i * tq - b * tk

        def fetch(p, slot):
            st = (p % 2) * 2 + slot
            return pltpu.make_async_copy(w_hbm.at[hh, i, block_of(2 * p + slot)], w_bufs.at[st], wsem.at[st])

        def products(b, slot):
            at = keys_of(b)
            z_refs[slot][...] = lax.dot_general(qs, k_ref[at, :], _NT, preferred_element_type=F32) * LOG2E
            p_refs[slot][...] = lax.dot_general(dob, v_ref[at, :], _NT, preferred_element_type=F32)

        def grads(p, slot, masked):
            b = 2 * p + slot
            at = keys_of(b)
            wb = w_bufs[(p % 2) * 2 + slot]
            g = wb.astype(F32) * p_refs[slot][...]
            ginc = _split_dot(g, tri, G_TERMS)
            beta = 1.0 / (1.0 + jnp.exp2(-z_refs[slot][...]))
            dz = g - beta * (grun_ref[...] + ginc)
            if masked:
                dz = jnp.where(visible(b), dz, 0.0)
            dzb = dz.astype(BF16)
            dq_acc[...] += jnp.dot(dzb, k_ref[at, :], preferred_element_type=F32)
            dk_ref[at, :] += lax.dot_general(dzb, qs, _TN, preferred_element_type=F32)
            dv_ref[at, :] += lax.dot_general(wb, dob, _TN, preferred_element_type=F32)
            grun_ref[...] += ginc[:, tk - 1:tk]

        def trip(p, masked):
            for slot in (0, 1):
                fetch(p + 1, slot).start()
            for slot in (0, 1):
                fetch(p, slot).wait()
            for slot in (0, 1):
                products(2 * p + slot + 1, 1 - slot)
                grads(p, slot, masked)

        for slot in (0, 1):
            fetch(0, slot).start()
        products(0, 0)
        n_plain = (nb - tq // tk) // 2

        def plain(p, carry):
            trip(p, False)
            return carry

        def guarded(p, carry):
            trip(p, True)
            return carry

        lax.fori_loop(0, n_plain, plain, 0)
        lax.fori_loop(n_plain, nb // 2, guarded, 0)
        for slot in (0, 1):
            fetch(nb // 2, slot).wait()
        dq_ref[...] = dq_acc[...] * Q_SCALE

    qblk, full, colspec = _head_specs(s, tq)
    big = pltpu.VMEM((tq, tk), F32)
    return pl.pallas_call(
        body, name="sb_bwd", grid=(h, s // tq),
        in_specs=[*_qkv_specs(s, tq, offs), qblk, pl.BlockSpec(memory_space=pl.ANY)], out_specs=[qblk, full, full],
        out_shape=[jax.ShapeDtypeStruct((h, s, HEAD_DIM), F32)] * 3,
        scratch_shapes=[pltpu.VMEM((tq, HEAD_DIM), F32), pltpu.VMEM((tq, 1), F32)]
        + [big] * 4 + [pltpu.VMEM((4, tq, tk), BF16), pltpu.SemaphoreType.DMA((4,))],
        compiler_params=_cparams(("parallel", "arbitrary")),
    )(qkv, qkv, qkv, d_o, w_saved)


def _sum_adamw(parts, w, m, v, name, tr=256):
    _, rows, lanes = parts.shape
    tr = _tile(rows, tr, 16)
    c_m = 1.0 - ADAM_B1 ** ADAM_STEP
    c_v = 1.0 - ADAM_B2 ** ADAM_STEP

    def body(p_ref, w_ref, m_ref, v_ref, g_ref, d_ref, nm_ref, nv_ref):
        g = p_ref[0].astype(F32)
        for j in range(1, N_DEV):
            g = g + p_ref[j].astype(F32)
        nm = ADAM_B1 * m_ref[...] + (1.0 - ADAM_B1) * g
        nv = ADAM_B2 * v_ref[...] + (1.0 - ADAM_B2) * (g * g)
        m_hat = nm / c_m
        v_hat = nv / c_v
        g_ref[...] = g
        d_ref[...] = -ADAM_LR * (m_hat / (jnp.sqrt(v_hat) + ADAM_EPS) + ADAM_WD * w_ref[...])
        nm_ref[...] = nm
        nv_ref[...] = nv

    blk = pl.BlockSpec((tr, lanes), lambda i: (i, 0))
    return pl.pallas_call(
        body, name=name, grid=(rows // tr,),
        in_specs=[pl.BlockSpec((N_DEV, tr, lanes), lambda i: (0, i, 0)), blk, blk, blk],
        out_specs=[blk] * 4, out_shape=[jax.ShapeDtypeStruct((rows, lanes), F32)] * 4,
        compiler_params=_cparams(("parallel",)),
    )(parts, w, m, v)


def kernel(x, attn_norm_g, w_in, forget_bias, fox_out_g, sb_out_g, w_out, ffn_norm_g, w_up, conv_w, conv_b, w_down, final_norm_g, loss_target, m_attn_norm_g, m_w_in, m_forget_bias, m_fox_out_g, m_sb_out_g, m_w_out, m_ffn_norm_g, m_w_up, m_conv_w, m_conv_b, m_w_down, m_final_norm_g, v_attn_norm_g, v_w_in, v_forget_bias, v_fox_out_g, v_sb_out_g, v_w_out, v_ffn_norm_g, v_w_up, v_conv_w, v_conv_b, v_w_down, v_final_norm_g):
    s = x.shape[1]
    xs = x[0]
    tq = min(ATTN_TQ, s)
    tk_fox = min(FOX_TK, s)
    tk_sb = min(SB_TK, s)
    in_shard, up_shard, out_shard, down_shard = IN_COLS // N_DEV, 2 * D_FF // N_DEV, D_MODEL // N_DEV, D_FF // N_DEV

    cw = conv_w[0]
    cw_hi = cw.astype(BF16)
    cw_lo = (cw - cw_hi.astype(F32)).astype(BF16)
    (g_in,) = _all_gather([w_in[0].astype(BF16)])
    rest = _exchange_start([w_out[0].astype(BF16), w_up[0].astype(BF16), w_down[0].astype(BF16),
                            jnp.stack([cw_hi, cw_lo])], False, "weights_rest_start")
    n_gate = QKV_W + N_GROUP_HEADS
    in_windows = _col_windows(N_DEV, in_shard, gap_at=n_gate, gap=GATE_PAD - N_GROUP_HEADS)
    up_windows = _col_windows(N_DEV, up_shard)
    w_in_p = _assemble_cols(g_in, IN_COLS_PAD, in_windows, "assemble_w_in")
    conv_b2 = conv_b.reshape(2, 1, D_FF)

    h1 = _rms_fwd(xs, attn_norm_g + rest[-1][0:1, 0:1])
    proj_h = _mm_heads(h1, w_in_p, "in_proj")
    fox_offs = (0, N_GROUP_HEADS, 2 * N_GROUP_HEADS)
    sb_first = 3 * N_GROUP_HEADS + GATE_PAD // HEAD_DIM
    sb_offs = (sb_first, sb_first + N_GROUP_HEADS, sb_first + 2 * N_GROUP_HEADS)
    f_logit = _mm_nn(h1, w_in_p[:, QKV_W:QKV_W + GATE_PAD], F32, "gate_proj")[:, :N_GROUP_HEADS]
    fv = proj_h[2 * N_GROUP_HEADS:3 * N_GROUP_HEADS]

    f_logit_h = f_logit.T.reshape(N_GROUP_HEADS, s // LANES, LANES)
    bias_h = jnp.broadcast_to(forget_bias.reshape(N_GROUP_HEADS, 1, 1), (N_GROUP_HEADS, 1, LANES))
    big_f = _forget_fwd(f_logit_h, bias_h)
    f_row = big_f.reshape(N_GROUP_HEADS, s // tk_fox, 1, tk_fox)

    fv_ones = jnp.concatenate([fv, jnp.ones_like(fv)], axis=-1)
    o_fox_h, lse = _fox_fwd(proj_h, fox_offs, fv_ones, f_row, tq, tk_fox)
    o_sb_h, sb_w = _sb_fwd(proj_h, sb_offs, min(SB_TQ, s), tk_sb)
    g_fox_h = fox_out_g.reshape(N_GROUP_HEADS, 1, HEAD_DIM)
    g_sb_h = sb_out_g.reshape(N_GROUP_HEADS, 1, HEAD_DIM)
    o_n = _group_rms_fwd(o_fox_h, o_sb_h, g_fox_h, g_sb_h)
    g_out, g_up, g_down, g_conv = _exchange_wait(rest, False, o_n, "weights_rest_wait")
    w_out_f = g_out.reshape(D_MODEL, D_MODEL)
    w_up_f = _assemble_cols(g_up, 2 * D_FF, up_windows, "assemble_w_up")
    w_down_f = g_down.reshape(D_FF, D_MODEL)
    conv_w_f = (g_conv[:, 0].astype(F32) + g_conv[:, 1].astype(F32)).transpose(1, 0, 2).reshape(3, 2 * D_FF)
    conv_w2 = conv_w_f.reshape(3, 2, D_FF).transpose(1, 0, 2)
    x1 = _mm_nn(o_n, w_out_f, F32, "out_proj", resid=xs)
    h2 = _rms_fwd(x1, ffn_norm_g)
    up = _mm_up(h2, w_up_f)
    act = _conv_gate_fwd(up, conv_w2, conv_b2)
    x2 = _mm_nn(act, w_down_f, F32, "down_proj", resid=x1, tk=1408)

    d_x2, d_x2b, dg_final, loss_part = _loss_head(x2, loss_target[0], final_norm_g.reshape(1, D_MODEL))
    d_act = _mm_nt(d_x2b, w_down_f, BF16, "d_act", tn=1408)
    dw_down = _mm_tn(act, d_x2b, "d_w_down", tm=1408)
    d_up, dcw2, dcb2 = _conv_gate_bwd(up, d_act, conv_w2, conv_b2)
    d_h2 = _mm_dup_nt(d_up, w_up_f)
    dw_up = _mm_dwup_tn(h2, d_up)
    d_x1, d_x1b, dg_ffn = _rms_bwd(x1, d_h2, ffn_norm_g, d_x2, dy_col=0, name="ffn_norm_bwd", want_bf16=True)
    d_on = _mm_nt(d_x1b, w_out_f, F32, "d_o_normed")
    dw_out = _mm_tn(o_n, d_x1b, "d_w_out")
    early = _exchange_start(
        [dw_out.astype(BF16).reshape(N_DEV, out_shard, D_MODEL),
         _split_cols(dw_up, N_DEV, up_shard, up_windows, "split_d_w_up"),
         dw_down.astype(BF16).reshape(N_DEV, down_shard, D_MODEL)], True, "grads_early_start")
    g_fox_t = g_fox_h + early[-1][0:1, 0:1]
    d_o_fox_h, dg_fox = _group_rms_bwd(o_fox_h, d_on, g_fox_t, dy_col=0, name="fox_norm_bwd")
    d_o_sb_h, dg_sb = _group_rms_bwd(o_sb_h, d_on, g_sb_h, dy_col=1, name="sb_norm_bwd")

    dfq, dfk, dfv, ksum8, qsum = _fox_bwd(proj_h, fox_offs, fv_ones, f_row, o_fox_h, lse, d_o_fox_h, tq, tk_fox)
    dsq, dsk, dsv = _sb_bwd(proj_h, sb_offs, sb_w, d_o_sb_h, min(SB_TQ, s), tk_sb)
    ksum = jnp.sum(ksum8, axis=2).reshape(N_GROUP_HEADS, s // LANES, LANES)
    d_f_logit_h, d_bias_h = _forget_bwd(f_logit_h, bias_h, ksum,
                                        qsum.reshape(N_GROUP_HEADS, s // LANES, LANES))
    d_f_logit = d_f_logit_h.reshape(N_GROUP_HEADS, s).T

    d_proj = _merge_dproj((dfq, dfk, dfv), d_f_logit, (dsq, dsk, dsv))
    dw_in_p = _mm_tn(h1, d_proj, "d_w_in", tn=640)
    dconv_w = dcw2.transpose(1, 0, 2).reshape(3, 2 * D_FF)
    dconv_b = dcb2.reshape(1, 2 * D_FF)
    late = _exchange_start(
        [_split_cols(dw_in_p, N_DEV, in_shard, in_windows, "split_d_w_in"),
         dconv_w.astype(BF16).reshape(3, N_DEV, up_shard).transpose(1, 0, 2)],
        True, "grads_late_start")
    d_h1 = _mm_nt(d_proj, w_in_p + late[-1][0:1, 0:1].astype(BF16), F32, "d_h1", tk=640)
    grad_x, dg_attn = _rms_bwd(xs, d_h1, attn_norm_g, d_x1, dy_col=0, name="attn_norm_bwd", want_bf16=False)

    small_shapes = [(1, D_MODEL), (1, N_GROUP_HEADS), (1, GROUP_W), (1, GROUP_W), (1, D_MODEL),
                    (1, 2 * D_FF), (D_MODEL,), (1,)]
    spack = _pack([dg_attn, d_bias_h[:, 0, 0], dg_fox, dg_sb, dg_ffn, dconv_b, dg_final, loss_part[0, 0:1]],
                  SMALL_ROWS, F32)
    (srecv,) = _grad_exchange([], spack)
    r_out, r_up, r_down = _exchange_wait(early, True, srecv, "grads_early_wait")
    r_in, r_conv = _exchange_wait(late, True, r_out, "grads_late_wait")

    big = [_sum_adamw(g, w_[0], m_[0], v_[0], "adamw_" + tag)
           for g, w_, m_, v_, tag in zip(
               (r_in, r_out, r_up, r_down, r_conv), (w_in, w_out, w_up, w_down, conv_w), (m_w_in, m_w_out, m_w_up, m_w_down, m_conv_w),
               (v_w_in, v_w_out, v_w_up, v_w_down, v_conv_w), ("w_in", "w_out", "w_up", "w_down", "conv_w"))]

    def small_pack(a_attn, a_bias, a_fox, a_sb, a_ffn, a_cb, a_fin):
        return _pack([a_attn, a_bias, a_fox, a_sb, a_ffn, a_cb, a_fin, jnp.zeros((1,), F32)], SMALL_ROWS, F32)

    small = _sum_adamw(srecv, small_pack(attn_norm_g, forget_bias, fox_out_g, sb_out_g, ffn_norm_g, conv_b, final_norm_g),
                       small_pack(m_attn_norm_g, m_forget_bias, m_fox_out_g, m_sb_out_g, m_ffn_norm_g, m_conv_b, m_final_norm_g),
                       small_pack(v_attn_norm_g, v_forget_bias, v_fox_out_g, v_sb_out_g, v_ffn_norm_g, v_conv_b, v_final_norm_g),
                       "adamw_replicated", tr=SMALL_ROWS)

    outs = []
    loss = None
    for kind in range(4):
        b_in, b_out, b_up, b_down, b_conv = (res[kind] for res in big)
        s_attn, s_bias, s_fox, s_sb, s_ffn, s_cb, s_fin, s_loss = _unpack(small[kind], small_shapes)
        if kind == 0:
            loss = s_loss[0]
        outs += [s_attn, b_in[None], s_bias, s_fox, s_sb, b_out[None], s_ffn, b_up[None], b_conv[None], s_cb,
                 b_down[None], s_fin]
    return (loss, grad_x[None], *outs)
```

```python
import jax
import jax.numpy as jnp
from jax import lax
from jax.experimental import pallas as pl
from jax.experimental.pallas import tpu as pltpu

F32 = jnp.float32
BF16 = jnp.bfloat16

D_MODEL = 1024
HEAD_DIM = 64
N_GROUP_HEADS = 8
GROUP_W = N_GROUP_HEADS * HEAD_DIM
QKV_W = 3 * GROUP_W
IN_COLS = 2 * QKV_W + N_GROUP_HEADS
GATE_PAD = 128
IN_COLS_PAD = 2 * QKV_W + GATE_PAD
D_FF = 2816
N_DEV = 8
EPS = 1e-6
Q_SCALE = HEAD_DIM ** -0.5

ADAM_LR = 0.001
ADAM_B1 = 0.9
ADAM_B2 = 0.999
ADAM_EPS = 1e-08
ADAM_WD = 0.01
ADAM_STEP = 10

LANES = 128
SMALL_ROWS = 80
VMEM_LIMIT = 56 * 1024 * 1024
NEG_BIG = -1e30
ATTN_TQ = 512
SB_TQ = 512
FOX_TK = 512
SB_TK = 256
MESH = pl.DeviceIdType.MESH


def _cparams(sem=None, **kw):
    return pltpu.CompilerParams(dimension_semantics=sem, vmem_limit_bytes=VMEM_LIMIT, **kw)


def _tile(n, target, mult=LANES):
    if n <= target:
        return n
    t = (target // mult) * mult
    while t >= mult:
        if n % t == 0:
            return t
        t -= mult
    return n


def _seg_len(shape):
    n = 1
    for s in shape:
        n *= s
    return -(-n // LANES) * LANES


def _pack(arrs, rows, dtype):
    parts = []
    for a in arrs:
        f = a.reshape(-1).astype(dtype)
        parts.append(jnp.pad(f, (0, _seg_len(a.shape) - f.shape[0])))
    flat = jnp.concatenate(parts)
    flat = jnp.pad(flat, (0, rows * LANES - flat.shape[0]))
    return flat.reshape(rows, LANES)


def _unpack(p, shapes, lead=()):
    flat = p.reshape(lead + (-1,))
    out, off = [], 0
    for shp in shapes:
        n = 1
        for s in shp:
            n *= s
        out.append(flat[..., off:off + n].reshape(lead + tuple(shp)))
        off += _seg_len(shp)
    return out


def _my_pos():
    return lax.axis_index("x"), lax.axis_index("y"), lax.axis_index("c")


def _all_gather(blocks):
    n = len(blocks)

    def body(*refs):
        x_refs, out_refs = refs[:n], refs[n:2 * n]
        send_sems, recv_sems, local_sems = refs[2 * n:]
        x, y, c = _my_pos()
        me, sibling = (x, y, c), (x, y, 1 - c)
        chips = [(1 - x, y), (x, 1 - y), (1 - x, 1 - y)]

        def copy(a, k, blk, to, own=False):
            px, py, pc = blk
            slot = out_refs[a].at[4 * px + 2 * py + pc]
            return pltpu.make_async_remote_copy(
                src_ref=x_refs[a] if own else slot, dst_ref=slot,
                send_sem=send_sems.at[a, k], recv_sem=recv_sems.at[a, k],
                device_id=to, device_id_type=MESH)

        mine = [pltpu.make_async_copy(x_refs[a], out_refs[a].at[4 * x + 2 * y + c], local_sems.at[a])
                for a in range(n)]
        for cp in mine:
            cp.start()
        first = []
        for a in range(n):
            first.append(copy(a, 0, me, sibling, own=True))
            first += [copy(a, 1 + j, me, (*chip, c), own=True) for j, chip in enumerate(chips)]
        for cp in first:
            cp.start()
        passed = []
        for j, chip in enumerate(chips):
            for a in range(n):
                copy(a, 1 + j, (*chip, c), me).wait_recv()
                passed.append(copy(a, 4 + j, (*chip, c), sibling))
                passed[-1].start()
        for a in range(n):
            copy(a, 0, sibling, me).wait_recv()
            for j, chip in enumerate(chips):
                copy(a, 4 + j, (*chip, 1 - c), me).wait_recv()
        for cp in first + passed:
            cp.wait_send()
        for cp in mine:
            cp.wait()

    hbm = pl.BlockSpec(memory_space=pl.ANY)
    return pl.pallas_call(
        body, name="weights_all_gather",
        out_shape=[jax.ShapeDtypeStruct((N_DEV,) + b.shape, b.dtype) for b in blocks],
        in_specs=[hbm] * n, out_specs=[hbm] * n,
        scratch_shapes=[pltpu.SemaphoreType.DMA((n, 7)), pltpu.SemaphoreType.DMA((n, 7)),
                        pltpu.SemaphoreType.DMA((n,))],
    )(*blocks)


def _grad_exchange(slabs, spack):
    n = len(slabs) + 1

    def body(*refs):
        in_refs, out_refs = refs[:n], refs[n:2 * n]
        send_sems, recv_sems, local_sems = refs[2 * n:]
        x, y, c = _my_pos()
        my_id = 4 * x + 2 * y + c

        def src_of(a, dev):
            return in_refs[a] if a == n - 1 else in_refs[a].at[dev]

        own = [pltpu.make_async_copy(src_of(a, my_id), out_refs[a].at[my_id], local_sems.at[a])
               for a in range(n)]
        for cp in own:
            cp.start()
        sends, arrivals = [], []
        for k in range(1, N_DEV):
            px, py, pc = x ^ (k >> 2), y ^ ((k >> 1) & 1), c ^ (k & 1)
            peer_id = 4 * px + 2 * py + pc
            for a in range(n):
                for dst_slot, bucket in ((my_id, sends), (peer_id, arrivals)):
                    bucket.append(pltpu.make_async_remote_copy(
                        src_ref=src_of(a, peer_id), dst_ref=out_refs[a].at[dst_slot],
                        send_sem=send_sems.at[a, k - 1], recv_sem=recv_sems.at[a, k - 1],
                        device_id=(px, py, pc), device_id_type=MESH))
        for cp in sends:
            cp.start()
        for cp in arrivals:
            cp.wait_recv()
        for cp in sends:
            cp.wait_send()
        for cp in own:
            cp.wait()

    hbm = pl.BlockSpec(memory_space=pl.ANY)
    return pl.pallas_call(
        body, name="grad_exchange",
        out_shape=[jax.ShapeDtypeStruct(g.shape, g.dtype) for g in slabs]
        + [jax.ShapeDtypeStruct((N_DEV,) + spack.shape, spack.dtype)],
        in_specs=[hbm] * n, out_specs=[hbm] * n,
        scratch_shapes=[pltpu.SemaphoreType.DMA((n, 7)), pltpu.SemaphoreType.DMA((n, 7)),
                        pltpu.SemaphoreType.DMA((n,))],
    )(*slabs, spack)


_HBM = pl.BlockSpec(memory_space=pltpu.HBM)
_SEM = pl.BlockSpec(memory_space=pltpu.SEMAPHORE)
_EFFECT = pltpu.SideEffectType.DATAFLOW_SIDE_EFFECTING


def _my_id():
    x, y, c = _my_pos()
    return 4 * x + 2 * y + c


def _peer_copies(src_refs, land_refs, send_sems, recv_sems, per_peer):
    x, y, c = _my_pos()
    my_id = 4 * x + 2 * y + c
    copies = []
    for k in range(1, N_DEV):
        px, py, pc = x ^ (k >> 2), y ^ ((k >> 1) & 1), c ^ (k & 1)
        for a, (src, land) in enumerate(zip(src_refs, land_refs)):
            copies.append(pltpu.make_async_remote_copy(
                src_ref=src.at[4 * px + 2 * py + pc] if per_peer else src, dst_ref=land.at[my_id],
                send_sem=send_sems.at[a * (N_DEV - 1) + k - 1], recv_sem=recv_sems.at[a * (N_DEV - 1) + k - 1],
                device_id=(px, py, pc), device_id_type=MESH))
    return copies


def _exchange_start(srcs, per_peer, name):
    n = len(srcs)
    lands = [lax.empty(s.shape if per_peer else (N_DEV,) + s.shape, s.dtype) for s in srcs]

    def body(*refs):
        src_refs, land_refs = refs[:n], refs[n:2 * n]
        send_sems, recv_sems = refs[2 * n], refs[2 * n + 1]
        token = refs[-1]
        for cp in _peer_copies(src_refs, land_refs, send_sems, recv_sems, per_peer):
            cp.start()
        token[...] = jnp.zeros_like(token)

    outs = pl.pallas_call(
        body, name=name,
        out_shape=(pltpu.SemaphoreType.DMA((n * (N_DEV - 1),)), pltpu.SemaphoreType.DMA((n * (N_DEV - 1),)),
                   *[pltpu.HBM(a.shape, a.dtype) for a in srcs + lands],
                   jax.ShapeDtypeStruct((8, LANES), F32)),
        in_specs=[_HBM] * (2 * n),
        out_specs=(_SEM, _SEM, *[_HBM] * (2 * n), pl.BlockSpec(memory_space=pltpu.VMEM)),
        input_output_aliases={a: 2 + a for a in range(2 * n)},
        compiler_params=pltpu.CompilerParams(has_side_effects=_EFFECT),
    )(*[pltpu.with_memory_space_constraint(a, pltpu.HBM) for a in srcs + lands])
    return outs[0], outs[1], list(outs[2:2 + n]), list(outs[2 + n:2 + 2 * n]), outs[-1]


def _exchange_wait(handles, per_peer, after, name):
    send_sems, recv_sems, srcs, lands, _ = handles
    n = len(srcs)

    def body(*refs):
        src_refs, land_refs = refs[:n], refs[n:2 * n]
        for cp in _peer_copies(src_refs, land_refs, refs[2 * n], refs[2 * n + 1], per_peer):
            cp.wait_send()
            cp.wait_recv()

    outs = pl.pallas_call(
        body, name=name,
        out_shape=tuple(pltpu.HBM(a.shape, a.dtype) for a in srcs + lands),
        in_specs=[_HBM] * (2 * n) + [_SEM, _SEM, pl.BlockSpec(memory_space=pl.ANY)],
        out_specs=tuple([_HBM] * (2 * n)),
        input_output_aliases={a: a for a in range(2 * n)},
        compiler_params=pltpu.CompilerParams(has_side_effects=_EFFECT),
    )(*srcs, *lands, send_sems, recv_sems, after)
    me = _my_id()
    filled = []
    for src, land in zip(outs[:n], outs[n:]):
        own = lax.dynamic_index_in_dim(src, me, 0, keepdims=True) if per_peer else src[None]
        filled.append(lax.dynamic_update_slice_in_dim(land, own, me, 0))
    return filled


def _col_windows(n_shards, width, gap_at=None, gap=0):
    out = []
    for j in range(n_shards):
        g0, g1 = j * width, (j + 1) * width
        cuts = [g0, g1] if gap_at is None or not g0 < gap_at < g1 else [g0, gap_at, g1]
        for a, b in zip(cuts[:-1], cuts[1:]):
            out.append((j, a - g0, b - g0, a + (gap if gap_at is not None and a >= gap_at else 0)))
    return out


def _assemble_cols(parts, total, windows, name, tr=256):
    n, rows, w = parts.shape
    tr = _tile(rows, tr, 16)

    def body(p_ref, o_ref):
        o_ref[...] = jnp.zeros_like(o_ref)
        for j, lo, hi, dst in windows:
            o_ref[:, dst:dst + hi - lo] = p_ref[j, :, lo:hi]

    return pl.pallas_call(
        body, name=name, grid=(rows // tr,),
        in_specs=[pl.BlockSpec((n, tr, w), lambda i: (0, i, 0))],
        out_specs=pl.BlockSpec((tr, total), lambda i: (i, 0)),
        out_shape=jax.ShapeDtypeStruct((rows, total), parts.dtype),
        compiler_params=_cparams(("parallel",)),
    )(parts)


def _split_cols(full, n, w, windows, name, tr=256):
    rows, total = full.shape
    tr = _tile(rows, tr, 16)

    def body(f_ref, o_ref):
        for j, lo, hi, dst in windows:
            o_ref[j, :, lo:hi] = f_ref[:, dst:dst + hi - lo].astype(o_ref.dtype)

    return pl.pallas_call(
        body, name=name, grid=(rows // tr,),
        in_specs=[pl.BlockSpec((tr, total), lambda i: (i, 0))],
        out_specs=pl.BlockSpec((n, tr, w), lambda i: (0, i, 0)),
        out_shape=jax.ShapeDtypeStruct((n, rows, w), BF16),
        compiler_params=_cparams(("parallel",)),
    )(full)


_DIMS = {"nn": (((1,), (0,)), ((), ())), "nt": (((1,), (1,)), ((), ())), "tn": (((0,), (0,)), ((), ()))}


def _matmul(a, b, *, mode, grid, a_block, a_map, b_block, b_map, o_block, o_map, out_shape, name,
            resid=None):
    nk = grid[2]
    dims = _DIMS[mode]

    def body(*refs):
        if resid is None:
            a_ref, b_ref, o_ref, acc_ref = refs
            r_ref = None
        else:
            a_ref, b_ref, r_ref, o_ref, acc_ref = refs
        k = pl.program_id(2)

        @pl.when(k == 0)
        def _():
            acc_ref[...] = jnp.zeros_like(acc_ref)

        acc_ref[...] += lax.dot_general(a_ref[...], b_ref[...], dims, preferred_element_type=F32)

        @pl.when(k == nk - 1)
        def _():
            res = acc_ref[...]
            if r_ref is not None:
                res = r_ref[...] + res
            o_ref[...] = res.astype(o_ref.dtype)

    in_specs = [pl.BlockSpec(a_block, a_map), pl.BlockSpec(b_block, b_map)]
    args = [a, b]
    if resid is not None:
        in_specs.append(pl.BlockSpec(o_block, o_map))
        args.append(resid)
    acc_shape = tuple(d for d in o_block if d is not None)
    return pl.pallas_call(
        body, name=name, grid=grid, in_specs=in_specs,
        out_specs=pl.BlockSpec(o_block, o_map), out_shape=out_shape,
        scratch_shapes=[pltpu.VMEM(acc_shape, F32)],
        compiler_params=_cparams(("parallel", "parallel", "arbitrary")),
    )(*args)


def _mm_nn(a, b, out_dtype, name, resid=None, tm=1024, tn=1024, tk=1024):
    m, kk = a.shape
    n = b.shape[1]
    tm, tn, tk = _tile(m, tm, 8), _tile(n, tn), _tile(kk, tk)
    return _matmul(a, b, mode="nn", grid=(m // tm, n // tn, kk // tk),
                   a_block=(tm, tk), a_map=lambda i, j, k: (i, k),
                   b_block=(tk, tn), b_map=lambda i, j, k: (k, j),
                   o_block=(tm, tn), o_map=lambda i, j, k: (i, j),
                   out_shape=jax.ShapeDtypeStruct((m, n), out_dtype), name=name, resid=resid)


def _mm_nt(a, b, out_dtype, name, tm=1024, tn=1024, tk=1024):
    m, kk = a.shape
    n = b.shape[0]
    tm, tn, tk = _tile(m, tm, 8), _tile(n, tn), _tile(kk, tk)
    return _matmul(a, b, mode="nt", grid=(m // tm, n // tn, kk // tk),
                   a_block=(tm, tk), a_map=lambda i, j, k: (i, k),
                   b_block=(tn, tk), b_map=lambda i, j, k: (j, k),
                   o_block=(tm, tn), o_map=lambda i, j, k: (i, j),
                   out_shape=jax.ShapeDtypeStruct((m, n), out_dtype), name=name)


def _mm_tn(a, b, name, tm=1024, tn=1024, tk=1024):
    kk, m = a.shape
    n = b.shape[1]
    tm, tn, tk = _tile(m, tm), _tile(n, tn), _tile(kk, tk, 8)
    return _matmul(a, b, mode="tn", grid=(m // tm, n // tn, kk // tk),
                   a_block=(tk, tm), a_map=lambda i, j, k: (k, i),
                   b_block=(tk, tn), b_map=lambda i, j, k: (k, j),
                   o_block=(tm, tn), o_map=lambda i, j, k: (i, j),
                   out_shape=jax.ShapeDtypeStruct((m, n), F32), name=name)


def _mm_heads(a, b, name, tm=1024, tn=640):
    m, kk = a.shape
    n = b.shape[1]
    tm, tn = _tile(m, tm, 16), _tile(n, tn)
    per_tile = tn // HEAD_DIM

    def body(a_ref, b_ref, o_ref):
        res = jnp.dot(a_ref[...], b_ref[...], preferred_element_type=F32)
        for hh in range(per_tile):
            o_ref[hh] = res[:, hh * HEAD_DIM:(hh + 1) * HEAD_DIM].astype(o_ref.dtype)

    return pl.pallas_call(
        body, name=name, grid=(m // tm, n // tn),
        in_specs=[pl.BlockSpec((tm, kk), lambda i, j: (i, 0)), pl.BlockSpec((kk, tn), lambda i, j: (0, j))],
        out_specs=pl.BlockSpec((per_tile, tm, HEAD_DIM), lambda i, j: (j, i, 0)),
        out_shape=jax.ShapeDtypeStruct((n // HEAD_DIM, m, HEAD_DIM), BF16),
        compiler_params=_cparams(("parallel", "parallel")),
    )(a, b)


def _mm_up(h, w_up, tm=2048, tn=256):
    s = h.shape[0]
    tm = _tile(s, tm, 8)
    nh = D_FF // tn
    return _matmul(h, w_up, mode="nn", grid=(s // tm, 2 * nh, 1),
                   a_block=(tm, D_MODEL), a_map=lambda i, j, k: (i, 0),
                   b_block=(D_MODEL, tn), b_map=lambda i, j, k: (0, j),
                   o_block=(None, tm, tn), o_map=lambda i, j, k: (j // nh, i, j % nh),
                   out_shape=jax.ShapeDtypeStruct((2, s, D_FF), F32), name="up_proj")


def _mm_dup_nt(dup, w_up, tm=1024, tk=1408):
    s = dup.shape[1]
    tm = _tile(s, tm, 8)
    nh = D_FF // tk
    return _matmul(dup, w_up, mode="nt", grid=(s // tm, 1, 2 * nh),
                   a_block=(None, tm, tk), a_map=lambda i, j, k: (k // nh, i, k % nh),
                   b_block=(D_MODEL, tk), b_map=lambda i, j, k: (0, k),
                   o_block=(tm, D_MODEL), o_map=lambda i, j, k: (i, 0),
                   out_shape=jax.ShapeDtypeStruct((s, D_MODEL), F32), name="d_h2")


def _mm_dwup_tn(h, dup, tn=1408, tk=1024):
    s = h.shape[0]
    tk = _tile(s, tk, 8)
    nh = D_FF // tn
    return _matmul(h, dup, mode="tn", grid=(1, 2 * nh, s // tk),
                   a_block=(tk, D_MODEL), a_map=lambda i, j, k: (k, 0),
                   b_block=(None, tk, tn), b_map=lambda i, j, k: (j // nh, k, j % nh),
                   o_block=(D_MODEL, tn), o_map=lambda i, j, k: (0, j),
                   out_shape=jax.ShapeDtypeStruct((D_MODEL, 2 * D_FF), F32), name="d_w_up")


def _rms_fwd(x, g, tr=512):
    s, d = x.shape
    tr = _tile(s, tr, 8)

    def body(x_ref, g_ref, o_ref):
        xv = x_ref[...]
        r = lax.rsqrt(jnp.mean(xv * xv, axis=-1, keepdims=True) + EPS)
        o_ref[...] = (xv * r * g_ref[...]).astype(o_ref.dtype)

    return pl.pallas_call(
        body, name="rms_fwd", grid=(s // tr,),
        in_specs=[pl.BlockSpec((tr, d), lambda i: (i, 0)), pl.BlockSpec((1, d), lambda i: (0, 0))],
        out_specs=pl.BlockSpec((tr, d), lambda i: (i, 0)),
        out_shape=jax.ShapeDtypeStruct((s, d), BF16),
        compiler_params=_cparams(("parallel",)),
    )(x, g)


def _group_rms_fwd(o_fox, o_sb, g_fox, g_sb, tr=512):
    nh, s, dh = o_fox.shape
    tr = _tile(s, tr, 8)

    def body(a_ref, b_ref, ga_ref, gb_ref, o_ref):
        for src, g_ref, lo in ((a_ref, ga_ref, 0), (b_ref, gb_ref, nh * dh)):
            heads = [src[hh] for hh in range(nh)]
            ss = heads[0] * heads[0]
            for xv in heads[1:]:
                ss = ss + xv * xv
            r = lax.rsqrt(jnp.sum(ss, axis=-1, keepdims=True) * (1.0 / (nh * dh)) + EPS)
            for hh, xv in enumerate(heads):
                o_ref[:, lo + hh * dh:lo + (hh + 1) * dh] = (xv * r * g_ref[hh]).astype(o_ref.dtype)

    heads_blk = pl.BlockSpec((nh, tr, dh), lambda i: (0, i, 0))
    gain = pl.BlockSpec((nh, 1, dh), lambda i: (0, 0, 0))
    return pl.pallas_call(
        body, name="group_rms_fwd", grid=(s // tr,),
        in_specs=[heads_blk, heads_blk, gain, gain],
        out_specs=pl.BlockSpec((tr, 2 * nh * dh), lambda i: (i, 0)),
        out_shape=jax.ShapeDtypeStruct((s, 2 * nh * dh), BF16),
        compiler_params=_cparams(("parallel",)),
    )(o_fox, o_sb, g_fox, g_sb)


def _group_rms_bwd(x, dy, g, *, dy_col, name, tr=512):
    nh, s, dh = x.shape
    tr = _tile(s, tr, 8)
    d = nh * dh

    def body(x_ref, dy_ref, g_ref, dx_ref, dg_ref):
        @pl.when(pl.program_id(0) == 0)
        def _():
            dg_ref[...] = jnp.zeros_like(dg_ref)

        dyv = dy_ref[...]
        xs_ = [x_ref[hh] for hh in range(nh)]
        dys = [dyv[:, hh * dh:(hh + 1) * dh] for hh in range(nh)]
        ss = xs_[0] * xs_[0]
        for xv in xs_[1:]:
            ss = ss + xv * xv
        r = lax.rsqrt(jnp.sum(ss, axis=-1, keepdims=True) * (1.0 / d) + EPS)
        xh = [xv * r for xv in xs_]
        gy = [dys[hh] * g_ref[hh] for hh in range(nh)]
        dot = xh[0] * gy[0]
        for hh in range(1, nh):
            dot = dot + xh[hh] * gy[hh]
        mean_dot = jnp.sum(dot, axis=-1, keepdims=True) * (1.0 / d)
        for hh in range(nh):
            dx_ref[hh] = r * (gy[hh] - xh[hh] * mean_dot)
            dg_ref[hh] += jnp.sum(dys[hh] * xh[hh], axis=0, keepdims=True)

    heads_blk = pl.BlockSpec((nh, tr, dh), lambda i: (0, i, 0))
    gain = pl.BlockSpec((nh, 1, dh), lambda i: (0, 0, 0))
    return pl.pallas_call(
        body, name=name, grid=(s // tr,),
        in_specs=[heads_blk, pl.BlockSpec((tr, d), lambda i: (i, dy_col)), gain],
        out_specs=[heads_blk, gain],
        out_shape=[jax.ShapeDtypeStruct((nh, s, dh), F32), jax.ShapeDtypeStruct((nh, 1, dh), F32)],
        compiler_params=_cparams(("arbitrary",)),
    )(x, dy, g)


def _merge_dproj(parts_fox, d_gate, parts_sb, tr=256):
    nh, s, dh = parts_fox[0].shape
    tr = _tile(s, tr, 16)

    def body(*refs):
        o_ref = refs[-1]
        gate_ref = refs[3]
        col = 0
        for ref in refs[:3]:
            for hh in range(nh):
                o_ref[:, col:col + dh] = ref[hh].astype(o_ref.dtype)
                col += dh
        o_ref[:, col:col + GATE_PAD] = jnp.zeros((tr, GATE_PAD), o_ref.dtype)
        o_ref[:, col:col + N_GROUP_HEADS] = gate_ref[...].astype(o_ref.dtype)
        col += GATE_PAD
        for ref in refs[4:7]:
            for hh in range(nh):
                o_ref[:, col:col + dh] = ref[hh].astype(o_ref.dtype)
                col += dh

    heads_blk = pl.BlockSpec((nh, tr, dh), lambda i: (0, i, 0))
    return pl.pallas_call(
        body, name="merge_d_proj", grid=(s // tr,),
        in_specs=[heads_blk] * 3 + [pl.BlockSpec((tr, N_GROUP_HEADS), lambda i: (i, 0))] + [heads_blk] * 3,
        out_specs=pl.BlockSpec((tr, IN_COLS_PAD), lambda i: (i, 0)),
        out_shape=jax.ShapeDtypeStruct((s, IN_COLS_PAD), BF16),
        compiler_params=_cparams(("parallel",)),
    )(*parts_fox, d_gate, *parts_sb)


def _rms_bwd(x, dy, g, resid, *, dy_col, name, want_bf16, tr=512):
    s, d = x.shape
    tr = _tile(s, tr, 8)
    has_resid = resid is not None

    def body(*refs):
        refs = list(refs)
        x_ref, dy_ref, g_ref = refs[:3]
        r_ref = refs[3] if has_resid else None
        outs = refs[4:] if has_resid else refs[3:]
        dx_ref = outs[0]
        dxb_ref = outs[1] if want_bf16 else None
        dg_ref = outs[-1]

        @pl.when(pl.program_id(0) == 0)
        def _():
            dg_ref[...] = jnp.zeros_like(dg_ref)

        xv = x_ref[...]
        dyv = dy_ref[...]
        r = lax.rsqrt(jnp.mean(xv * xv, axis=-1, keepdims=True) + EPS)
        xh = xv * r
        gy = dyv * g_ref[...]
        dx = r * (gy - xh * jnp.mean(xh * gy, axis=-1, keepdims=True))
        if r_ref is not None:
            dx = r_ref[...] + dx
        dx_ref[...] = dx
        if dxb_ref is not None:
            dxb_ref[...] = dx.astype(BF16)
        dg_ref[...] += jnp.sum(dyv * xh, axis=0, keepdims=True)

    row = pl.BlockSpec((tr, d), lambda i: (i, 0))
    in_specs = [row, pl.BlockSpec((tr, d), lambda i: (i, dy_col)), pl.BlockSpec((1, d), lambda i: (0, 0))]
    args = [x, dy, g]
    if has_resid:
        in_specs.append(row)
        args.append(resid)
    out_specs = [row]
    out_shape = [jax.ShapeDtypeStruct((s, d), F32)]
    if want_bf16:
        out_specs.append(row)
        out_shape.append(jax.ShapeDtypeStruct((s, d), BF16))
    out_specs.append(pl.BlockSpec((1, d), lambda i: (0, 0)))
    out_shape.append(jax.ShapeDtypeStruct((1, d), F32))
    return pl.pallas_call(
        body, name=name, grid=(s // tr,), in_specs=in_specs, out_specs=out_specs, out_shape=out_shape,
        compiler_params=_cparams(("arbitrary",)),
    )(*args)


def _loss_head(x2, target, g, tr=512):
    s, d = x2.shape
    tr = _tile(s, tr, 8)

    def body(x_ref, t_ref, g_ref, dx_ref, dxb_ref, dg_ref, loss_ref):
        @pl.when(pl.program_id(0) == 0)
        def _():
            dg_ref[...] = jnp.zeros_like(dg_ref)
            loss_ref[...] = jnp.zeros_like(loss_ref)

        xv = x_ref[...]
        gv = g_ref[...]
        r = lax.rsqrt(jnp.mean(xv * xv, axis=-1, keepdims=True) + EPS)
        xh = xv * r
        err = xh * gv - t_ref[...]
        loss_ref[...] += jnp.sum(jnp.mean(err * err, axis=-1, keepdims=True), axis=0, keepdims=True) * 0.5
        dyv = err * (1.0 / d)
        gy = dyv * gv
        dx = r * (gy - xh * jnp.mean(xh * gy, axis=-1, keepdims=True))
        dx_ref[...] = dx
        dxb_ref[...] = dx.astype(BF16)
        dg_ref[...] += jnp.sum(dyv * xh, axis=0, keepdims=True)

    row = pl.BlockSpec((tr, d), lambda i: (i, 0))
    return pl.pallas_call(
        body, name="loss_head", grid=(s // tr,),
        in_specs=[row, row, pl.BlockSpec((1, d), lambda i: (0, 0))],
        out_specs=[row, row, pl.BlockSpec((1, d), lambda i: (0, 0)), pl.BlockSpec((1, LANES), lambda i: (0, 0))],
        out_shape=[jax.ShapeDtypeStruct((s, d), F32), jax.ShapeDtypeStruct((s, d), BF16),
                   jax.ShapeDtypeStruct((1, d), F32), jax.ShapeDtypeStruct((1, LANES), F32)],
        compiler_params=_cparams(("arbitrary",)),
    )(x2, target, g)


def _conv_taps(cur, prev8, w, b, first):
    prev8 = jnp.where(first, 0.0, prev8)
    ext = jnp.concatenate([prev8, cur], axis=0)
    x1 = pltpu.roll(ext, 1, 0)[8:]
    x2 = pltpu.roll(ext, 2, 0)[8:]
    u = b + w[0:1] * x2
    u = u + w[1:2] * x1
    u = u + w[2:3] * cur
    return u, x1, x2


def _conv_gate_fwd(up, conv_w, conv_b, tm=2048, tn=256):
    s = up.shape[1]
    tm = _tile(s, tm, 8)
    nrb = s // tm
    rb8 = tm // 8

    def body(g_ref, v_ref, gp_ref, vp_ref, wg_ref, wv_ref, bg_ref, bv_ref, o_ref):
        first = pl.program_id(1) == 0
        ug, _, _ = _conv_taps(g_ref[...], gp_ref[...], wg_ref[...], bg_ref[...], first)
        uv, _, _ = _conv_taps(v_ref[...], vp_ref[...], wv_ref[...], bv_ref[...], first)
        sg = 1.0 / (1.0 + jnp.exp(-ug))
        o_ref[...] = (ug * sg * uv).astype(o_ref.dtype)

    def cur(h):
        return pl.BlockSpec((None, tm, tn), lambda j, i: (h, i, j))

    def prev(h):
        return pl.BlockSpec((None, 8, tn), lambda j, i: (h, jnp.maximum(i * rb8 - 1, 0), j))

    def par(h, r):
        return pl.BlockSpec((None, r, tn), lambda j, i: (h, 0, j))

    return pl.pallas_call(
        body, name="conv_gate_fwd", grid=(D_FF // tn, nrb),
        in_specs=[cur(0), cur(1), prev(0), prev(1), par(0, 3), par(1, 3), par(0, 1), par(1, 1)],
        out_specs=pl.BlockSpec((tm, tn), lambda j, i: (i, j)),
        out_shape=jax.ShapeDtypeStruct((s, D_FF), BF16),
        compiler_params=_cparams(("parallel", "parallel")),
    )(up, up, up, up, conv_w, conv_w, conv_b, conv_b)


def _conv_gate_bwd(up, dact, conv_w, conv_b, tm=1024, tn=256):
    s = up.shape[1]
    tm = _tile(s, tm, 8)
    nrb = s // tm
    rb8 = tm // 8

    def body(g_ref, v_ref, gp_ref, vp_ref, da_ref, wg_ref, wv_ref, bg_ref, bv_ref,
             dup_ref, dcw_ref, dcb_ref, carry_ref):
        i = pl.program_id(1)
        first = i == nrb - 1

        @pl.when(i == 0)
        def _():
            carry_ref[...] = jnp.zeros_like(carry_ref)
            dcw_ref[...] = jnp.zeros_like(dcw_ref)
            dcb_ref[...] = jnp.zeros_like(dcb_ref)

        curs = (g_ref[...], v_ref[...])
        ws = (wg_ref[...], wv_ref[...])
        ug, g1, g2 = _conv_taps(curs[0], gp_ref[...], ws[0], bg_ref[...], first)
        uv, v1, v2 = _conv_taps(curs[1], vp_ref[...], ws[1], bv_ref[...], first)
        sg = 1.0 / (1.0 + jnp.exp(-ug))
        da = da_ref[...].astype(F32)
        d_v = da * (ug * sg)
        d_g = da * uv * (sg * (1.0 + ug * (1.0 - sg)))
        for h, (du, x0, x1, x2) in enumerate(((d_g, curs[0], g1, g2), (d_v, curs[1], v1, v2))):
            dcb_ref[h] += jnp.sum(du, axis=0, keepdims=True)
            dcw_ref[h, 0:1, :] += jnp.sum(du * x2, axis=0, keepdims=True)
            dcw_ref[h, 1:2, :] += jnp.sum(du * x1, axis=0, keepdims=True)
            dcw_ref[h, 2:3, :] += jnp.sum(du * x0, axis=0, keepdims=True)
            ext = jnp.concatenate([du, carry_ref[h]], axis=0)
            n1 = pltpu.roll(ext, tm + 7, 0)[:tm]
            n2 = pltpu.roll(ext, tm + 6, 0)[:tm]
            w = ws[h]
            dup_ref[h] = (w[2:3] * du + w[1:2] * n1 + w[0:1] * n2).astype(dup_ref.dtype)
            carry_ref[h] = du[:8]

    def cur(h):
        return pl.BlockSpec((None, tm, tn), lambda j, i: (h, nrb - 1 - i, j))

    def prev(h):
        return pl.BlockSpec((None, 8, tn), lambda j, i: (h, jnp.maximum((nrb - 1 - i) * rb8 - 1, 0), j))

    def par(h, r):
        return pl.BlockSpec((None, r, tn), lambda j, i: (h, 0, j))

    return pl.pallas_call(
        body, name="conv_gate_bwd", grid=(D_FF // tn, nrb),
        in_specs=[cur(0), cur(1), prev(0), prev(1),
                  pl.BlockSpec((tm, tn), lambda j, i: (nrb - 1 - i, j)),
                  par(0, 3), par(1, 3), par(0, 1), par(1, 1)],
        out_specs=[pl.BlockSpec((2, tm, tn), lambda j, i: (0, nrb - 1 - i, j)),
                   pl.BlockSpec((2, 3, tn), lambda j, i: (0, 0, j)),
                   pl.BlockSpec((2, 1, tn), lambda j, i: (0, 0, j))],
        out_shape=[jax.ShapeDtypeStruct((2, s, D_FF), BF16),
                   jax.ShapeDtypeStruct((2, 3, D_FF), F32),
                   jax.ShapeDtypeStruct((2, 1, D_FF), F32)],
        scratch_shapes=[pltpu.VMEM((2, 8, tn), F32)],
        compiler_params=_cparams(("parallel", "arbitrary")),
    )(up, up, up, up, dact, conv_w, conv_w, conv_b, conv_b)


def _split_dot(x, tri, terms):
    piece = x.astype(BF16)
    out = jnp.dot(piece, tri, preferred_element_type=F32)
    rest = x
    for _ in range(terms - 1):
        rest = rest - piece.astype(F32)
        piece = rest.astype(BF16)
        out = out + jnp.dot(piece, tri, preferred_element_type=F32)
    return out


def _split_dot_rhs(tri, x, terms):
    piece = x.astype(BF16)
    out = jnp.dot(tri, piece, preferred_element_type=F32)
    rest = x
    for _ in range(terms - 1):
        rest = rest - piece.astype(F32)
        piece = rest.astype(BF16)
        out = out + jnp.dot(tri, piece, preferred_element_type=F32)
    return out


def _tri(n, kind):
    r = lax.broadcasted_iota(jnp.int32, (n, n), 0)
    c = lax.broadcasted_iota(jnp.int32, (n, n), 1)
    cond = {"le": r <= c, "ge": r >= c, "lt": r < c, "gt": r > c}[kind]
    return jnp.where(cond, 1.0, 0.0).astype(BF16)


def _log_sigmoid(x):
    return jnp.minimum(x, 0.0) - jnp.log(1.0 + jnp.exp(-jnp.abs(x)))


def _forget_fwd(f_logit, bias):
    h, r, _ = f_logit.shape

    def body(x_ref, b_ref, o_ref):
        lf = _log_sigmoid(x_ref[...] + b_ref[...])
        within = _split_dot(lf, _tri(LANES, "le"), 3)
        row_tot = jnp.broadcast_to(within[:, LANES - 1:LANES], (r, LANES))
        before = _split_dot_rhs(_tri(r, "gt"), row_tot, 3)
        o_ref[...] = within + before

    blk = pl.BlockSpec((None, r, LANES), lambda i: (i, 0, 0))
    return pl.pallas_call(
        body, name="forget_cumsum_fwd", grid=(h,),
        in_specs=[blk, pl.BlockSpec((None, 1, LANES), lambda i: (i, 0, 0))],
        out_specs=blk, out_shape=jax.ShapeDtypeStruct((h, r, LANES), F32),
        compiler_params=_cparams(("parallel",)),
    )(f_logit, bias)


def _forget_bwd(f_logit, bias, ksum, qsum):
    h, r, _ = f_logit.shape

    def body(x_ref, b_ref, k_ref, q_ref, dx_ref, db_ref):
        d_f = q_ref[...] - k_ref[...]
        within = _split_dot(d_f, _tri(LANES, "ge"), 3)
        row_tot = jnp.broadcast_to(within[:, 0:1], (r, LANES))
        after = _split_dot_rhs(_tri(r, "lt"), row_tot, 3)
        xv = x_ref[...] + b_ref[...]
        dx = (within + after) * jnp.exp(_log_sigmoid(-xv))
        dx_ref[...] = dx
        db_ref[...] = jnp.broadcast_to(jnp.sum(dx), (1, LANES))

    blk = pl.BlockSpec((None, r, LANES), lambda i: (i, 0, 0))
    one = pl.BlockSpec((None, 1, LANES), lambda i: (i, 0, 0))
    return pl.pallas_call(
        body, name="forget_cumsum_bwd", grid=(h,),
        in_specs=[blk, one, blk, blk], out_specs=[blk, one],
        out_shape=[jax.ShapeDtypeStruct((h, r, LANES), F32), jax.ShapeDtypeStruct((h, 1, LANES), F32)],
        compiler_params=_cparams(("parallel",)),
    )(f_logit, bias, ksum, qsum)


def _head_specs(s, tq):
    qblk = pl.BlockSpec((None, tq, HEAD_DIM), lambda h, i: (h, i, 0))
    full = pl.BlockSpec((None, s, HEAD_DIM), lambda h, i: (h, 0, 0))
    col = pl.BlockSpec((None, tq, 1), lambda h, i: (h, i, 0))
    return qblk, full, col


def _qkv_specs(s, tq, offs):
    q_off, k_off, v_off = offs
    return (pl.BlockSpec((None, tq, HEAD_DIM), lambda h, i: (h + q_off, i, 0)),
            pl.BlockSpec((None, s, HEAD_DIM), lambda h, i: (h + k_off, 0, 0)),
            pl.BlockSpec((None, s, HEAD_DIM), lambda h, i: (h + v_off, 0, 0)))


def _scaled(q_ref):
    return (q_ref[...].astype(F32) * Q_SCALE).astype(BF16)


_NT = (((1,), (1,)), ((), ()))
_TN = (((0,), (0,)), ((), ()))


def _cols_minus_rows(rows, cols):
    return lax.broadcasted_iota(jnp.int32, (rows, cols), 1) - lax.broadcasted_iota(jnp.int32, (rows, cols), 0)


def _fox_fwd(qkv, offs, v_ones, f_row, tq, tk):
    h, s = N_GROUP_HEADS, qkv.shape[1]
    nk = s // tk
    assert tq == tk

    def body(q_ref, k_ref, v_ref, fr_ref, o_ref, lse_ref, m_ref, acc_ref, z0, z1):
        i = pl.program_id(1)
        qs = _scaled(q_ref)
        m_ref[...] = jnp.full_like(m_ref, NEG_BIG)
        acc_ref[...] = jnp.zeros_like(acc_ref)

        ahead = _cols_minus_rows(tq, tk)

        def block_of(j):
            return jnp.minimum(j, nk - 1)

        def keys_of(j):
            return pl.ds(pl.multiple_of(block_of(j) * tk, tk), tk)

        def logits(j):
            return lax.dot_general(qs, k_ref[keys_of(j), :], _NT, preferred_element_type=F32)

        def soft(j, raw, masked):
            sc = raw - fr_ref[block_of(j)]
            if masked:
                sc = jnp.where(ahead <= (i - j) * tk, sc, NEG_BIG)
            m_old = m_ref[...]
            m_new = jnp.maximum(m_old, jnp.max(sc, axis=-1, keepdims=True))
            p = jnp.exp(sc - m_new)
            acc_ref[...] = jnp.exp(m_old - m_new) * acc_ref[...] + jnp.dot(
                p.astype(BF16), v_ref[keys_of(j), :], preferred_element_type=F32)
            m_ref[...] = m_new

        z0[...] = logits(0)

        def trip(p, masked):
            j = 2 * p
            z1[...] = logits(j + 1)
            soft(j, z0[...], masked)
            z0[...] = logits(j + 2)
            soft(j + 1, z1[...], masked)

        def step(p, carry):
            trip(p, False)
            return carry

        lax.fori_loop(0, i // 2, step, 0)
        trip(i // 2, True)
        l = acc_ref[:, HEAD_DIM:HEAD_DIM + 1]
        o_ref[...] = acc_ref[:, :HEAD_DIM] / l
        lse_ref[...] = m_ref[...] + jnp.log(l)

    qblk, full, colspec = _head_specs(s, tq)
    q_in, k_in, _ = _qkv_specs(s, tq, offs)
    return pl.pallas_call(
        body, name="fox_fwd", grid=(h, s // tq),
        in_specs=[q_in, k_in, pl.BlockSpec((None, s, 2 * HEAD_DIM), lambda hh, i: (hh, 0, 0)),
                  pl.BlockSpec((None, nk, 1, tk), lambda hh, i: (hh, 0, 0, 0))],
        out_specs=[qblk, colspec],
        out_shape=[jax.ShapeDtypeStruct((h, s, HEAD_DIM), F32), jax.ShapeDtypeStruct((h, s, 1), F32)],
        scratch_shapes=[pltpu.VMEM((tq, 1), F32), pltpu.VMEM((tq, 2 * HEAD_DIM), F32),
                        pltpu.VMEM((tq, tk), F32), pltpu.VMEM((tq, tk), F32)],
        compiler_params=_cparams(("parallel", "parallel")),
    )(qkv, qkv, v_ones, f_row)


def _fox_bwd(qkv, offs, k_ones, v_ones, f_row, o, lse, d_o, tq, tk):
    h, s = N_GROUP_HEADS, qkv.shape[1]
    nk = s // tk
    assert tq == tk

    def body(q_ref, k_ref, v_ref, fr_ref, o_ref, lse_ref, do_ref,
             dq_ref, dk_ref, dv_ref, ks_ref, qs_ref, dq_acc, qsum_acc, z0, z1, p0, p1, doa_ref, qa_ref):
        i = pl.program_id(1)

        @pl.when(i == 0)
        def _():
            dk_ref[...] = jnp.zeros_like(dk_ref)
            dv_ref[...] = jnp.zeros_like(dv_ref)
            ks_ref[...] = jnp.zeros_like(ks_ref)

        qs = _scaled(q_ref)
        lse_v = lse_ref[...]
        dob = do_ref[...].astype(BF16)
        delta = jnp.sum(dob.astype(F32) * o_ref[...], axis=-1, keepdims=True)
        dq_acc[...] = jnp.zeros_like(dq_acc)
        qsum_acc[...] = jnp.zeros_like(qsum_acc)
        d_hi = delta.astype(BF16).astype(F32)
        d_mid = (delta - d_hi).astype(BF16).astype(F32)
        d_lo = (delta - d_hi - d_mid).astype(BF16).astype(F32)
        spare = lax.broadcasted_iota(jnp.int32, (tq, HEAD_DIM), 1)
        doa_ref[:, :HEAD_DIM] = dob
        doa_ref[:, HEAD_DIM:] = jnp.where(spare == 0, -d_hi, jnp.where(
            spare == 1, -d_mid, jnp.where(spare == 2, -d_lo, 0.0))).astype(BF16)
        l_hi = lse_v.astype(BF16).astype(F32)
        l_mid = (lse_v - l_hi).astype(BF16).astype(F32)
        l_lo = (lse_v - l_hi - l_mid).astype(BF16).astype(F32)
        qa_ref[:, :HEAD_DIM] = qs
        qa_ref[:, HEAD_DIM:] = jnp.where(spare == 0, -l_hi, jnp.where(
            spare == 1, -l_mid, jnp.where(spare == 2, -l_lo, 0.0))).astype(BF16)

        ahead = _cols_minus_rows(tq, tk)

        def block_of(j):
            return jnp.minimum(j, nk - 1)

        def keys_of(j):
            return pl.ds(pl.multiple_of(block_of(j) * tk, tk), tk)

        def products(j):
            at = keys_of(j)
            return (lax.dot_general(qa_ref[...], k_ref[at, :], _NT, preferred_element_type=F32),
                    lax.dot_general(doa_ref[...], v_ref[at, :], _NT, preferred_element_type=F32))

        def grads(j, raw, dp, masked):
            at = keys_of(j)
            sc = raw - fr_ref[block_of(j)]
            if masked:
                sc = jnp.where(ahead <= (i - j) * tk, sc, NEG_BIG)
            p = jnp.exp(sc)
            ds = p * dp
            dsb = ds.astype(BF16)
            dq_acc[...] += jnp.dot(dsb, k_ref[at, :], preferred_element_type=F32)[:, :HEAD_DIM]
            dk_ref[at, :] += lax.dot_general(dsb, qs, _TN, preferred_element_type=F32)
            dv_ref[at, :] += lax.dot_general(p.astype(BF16), dob, _TN, preferred_element_type=F32)
            ks_ref[block_of(j)] += jnp.sum(ds.reshape(tq // 8, 8, tk), axis=0)
            qsum_acc[...] += jnp.sum(ds, axis=-1, keepdims=True)

        z0[...], p0[...] = products(0)

        def trip(pp, masked):
            j = 2 * pp
            z1[...], p1[...] = products(j + 1)
            grads(j, z0[...], p0[...], masked)
            z0[...], p0[...] = products(j + 2)
            grads(j + 1, z1[...], p1[...], masked)

        def step(pp, carry):
            trip(pp, False)
            return carry

        lax.fori_loop(0, i // 2, step, 0)
        trip(i // 2, True)
        dq_ref[...] = dq_acc[...] * Q_SCALE
        qs_ref[...] = qsum_acc[...]

    qblk, full, colspec = _head_specs(s, tq)
    frow = pl.BlockSpec((None, nk, 1, tk), lambda hh, i: (hh, 0, 0, 0))
    big = pltpu.VMEM((tq, tk), F32)
    return pl.pallas_call(
        body, name="fox_bwd", grid=(h, s // tq),
        in_specs=[_qkv_specs(s, tq, offs)[0], pl.BlockSpec((None, s, 2 * HEAD_DIM), lambda hh, i: (hh, 0, 0)),
                  pl.BlockSpec((None, s, 2 * HEAD_DIM), lambda hh, i: (hh, 0, 0)),
                  frow, qblk, colspec, qblk],
        out_specs=[qblk, full, full, pl.BlockSpec((None, nk, 8, tk), lambda hh, i: (hh, 0, 0, 0)), colspec],
        out_shape=[jax.ShapeDtypeStruct((h, s, HEAD_DIM), F32)] * 3
        + [jax.ShapeDtypeStruct((h, nk, 8, tk), F32), jax.ShapeDtypeStruct((h, s, 1), F32)],
        scratch_shapes=[pltpu.VMEM((tq, HEAD_DIM), F32), pltpu.VMEM((tq, 1), F32), big, big, big, big,
                        pltpu.VMEM((tq, 2 * HEAD_DIM), BF16), pltpu.VMEM((tq, 2 * HEAD_DIM), BF16)],
        compiler_params=_cparams(("parallel", "arbitrary")),
    )(qkv, k_ones, v_ones, f_row, o, lse, d_o)


SB_TERMS = 2
G_TERMS = 1
LOG2E = 1.4426950408889634
LN2 = 0.6931471805599453


def _softplus2(z2):
    return jnp.maximum(z2, 0.0) + jnp.log2(1.0 + jnp.exp2(-jnp.abs(z2)))


def _sb_fwd(qkv, offs, tq, tk):
    h, s = N_GROUP_HEADS, qkv.shape[1]

    assert tq % (2 * tk) == 0

    def body(q_ref, k_ref, v_ref, o_ref, w_hbm, acc_ref, run_ref, z0, z1, d0, d1, t0, t1, w_stage, wsem):
        z_refs, d_refs, t_refs = (z0, z1), (d0, d1), (t0, t1)
        hh = pl.program_id(0)
        i = pl.program_id(1)
        qs = _scaled(q_ref)
        tri = _tri(tk, "ge")
        acc_ref[...] = jnp.zeros_like(acc_ref)
        run_ref[...] = jnp.zeros_like(run_ref)
        nb = (i + 1) * (tq // tk)
        ahead = _cols_minus_rows(tq, tk)

        def keys_of(b):
            j = nb - 1 - jnp.minimum(b, nb - 1)
            return pl.ds(pl.multiple_of(j * tk, tk), tk)

        def visible(b):
            return ahead < i * tq - (nb - 1 - b) * tk

        def logits(b, slot):
            z_refs[slot][...] = lax.dot_general(qs, k_ref[keys_of(b), :], _NT,
                                                preferred_element_type=F32) * LOG2E

        def sums(b, slot, masked):
            z2 = z_refs[slot][...]
            sp = _softplus2(z2)
            if masked:
                sp = jnp.where(visible(b), sp, 0.0)
            inc = _split_dot(sp, tri, SB_TERMS)
            d_refs[slot][...] = z2 - inc
            t_refs[slot][...] = inc[:, 0:1]

        def put(p, slot):
            st = (p % 2) * 2 + slot
            return pltpu.make_async_copy(w_stage.at[st], w_hbm.at[hh, i, nb - 1 - (2 * p + slot)], wsem.at[st])

        def weigh(p, slot, masked):
            b = 2 * p + slot
            w = jnp.exp2(d_refs[slot][...] - run_ref[...])
            if masked:
                w = jnp.where(visible(b), w, 0.0)
            wb = w.astype(BF16)
            w_stage[(p % 2) * 2 + slot] = wb
            acc_ref[...] += jnp.dot(wb, v_ref[keys_of(b), :], preferred_element_type=F32)
            run_ref[...] += t_refs[slot][...]

        def trip(p, masked):
            @pl.when(p >= 2)
            def _():
                put(p - 2, 0).wait()
                put(p - 2, 1).wait()

            b = 2 * p
            logits(b + 2, 0)
            sums(b + 1, 1, masked)
            weigh(p, 0, masked)
            logits(b + 3, 1)
            sums(b + 2, 0, masked)
            weigh(p, 1, masked)
            put(p, 0).start()
            put(p, 1).start()

        logits(0, 0)
        logits(1, 1)
        sums(0, 0, True)

        def guarded(p, carry):
            trip(p, True)
            return carry

        def plain(p, carry):
            trip(p, False)
            return carry

        lax.fori_loop(0, tq // tk // 2, guarded, 0)
        lax.fori_loop(tq // tk // 2, nb // 2, plain, 0)
        trips = nb // 2

        @pl.when(trips >= 2)
        def _():
            put(trips - 2, 0).wait()
            put(trips - 2, 1).wait()

        put(trips - 1, 0).wait()
        put(trips - 1, 1).wait()
        o_ref[...] = acc_ref[...]

    qblk, full, colspec = _head_specs(s, tq)
    return pl.pallas_call(
        body, name="sb_fwd", grid=(h, s // tq),
        in_specs=[*_qkv_specs(s, tq, offs)], out_specs=[qblk, pl.BlockSpec(memory_space=pl.ANY)],
        out_shape=[jax.ShapeDtypeStruct((h, s, HEAD_DIM), F32),
                   jax.ShapeDtypeStruct((h, s // tq, s // tk, tq, tk), BF16)],
        scratch_shapes=[pltpu.VMEM((tq, HEAD_DIM), F32), pltpu.VMEM((tq, 1), F32),
                        pltpu.VMEM((tq, tk), F32), pltpu.VMEM((tq, tk), F32),
                        pltpu.VMEM((tq, tk), F32), pltpu.VMEM((tq, tk), F32),
                        pltpu.VMEM((tq, 1), F32), pltpu.VMEM((tq, 1), F32),
                        pltpu.VMEM((4, tq, tk), BF16), pltpu.SemaphoreType.DMA((4,))],
        compiler_params=_cparams(("parallel", "parallel")),
    )(qkv, qkv, qkv)


def _sb_bwd(qkv, offs, w_saved, d_o, tq, tk):
    h, s = N_GROUP_HEADS, qkv.shape[1]

    assert tq % (2 * tk) == 0

    def body(q_ref, k_ref, v_ref, do_ref, w_hbm, dq_ref, dk_ref, dv_ref, dq_acc, grun_ref,
             z0, z1, p0, p1, w_bufs, wsem):
        z_refs, p_refs = (z0, z1), (p0, p1)
        hh = pl.program_id(0)
        i = pl.program_id(1)

        @pl.when(i == 0)
        def _():
            dk_ref[...] = jnp.zeros_like(dk_ref)
            dv_ref[...] = jnp.zeros_like(dv_ref)

        qs = _scaled(q_ref)
        dob = do_ref[...].astype(BF16)
        tri = _tri(tk, "le")
        dq_acc[...] = jnp.zeros_like(dq_acc)
        grun_ref[...] = jnp.zeros_like(grun_ref)
        nb = (i + 1) * (tq // tk)
        ahead = _cols_minus_rows(tq, tk)

        def block_of(b):
            return jnp.minimum(b, nb - 1)

        def keys_of(b):
            return pl.ds(pl.multiple_of(block_of(b) * tk, tk), tk)

        def visible(b):
            return ahead < i * tq - b * tk

        def fetch(p, slot):
            st = (p % 2) * 2 + slot
            return pltpu.make_async_copy(w_hbm.at[hh, i, block_of(2 * p + slot)], w_bufs.at[st], wsem.at[st])

        def products(b, slot):
            at = keys_of(b)
            z_refs[slot][...] = lax.dot_general(qs, k_ref[at, :], _NT, preferred_element_type=F32) * LOG2E
            p_refs[slot][...] = lax.dot_general(dob, v_ref[at, :], _NT, preferred_element_type=F32)

        def grads(p, slot, masked):
            b = 2 * p + slot
            at = keys_of(b)
            wb = w_bufs[(p % 2) * 2 + slot]
            g = wb.astype(F32) * p_refs[slot][...]
            ginc = _split_dot(g, tri, G_TERMS)
            beta = 1.0 / (1.0 + jnp.exp2(-z_refs[slot][...]))
            dz = g - beta * (grun_ref[...] + ginc)
            if masked:
                dz = jnp.where(visible(b), dz, 0.0)
            dzb = dz.astype(BF16)
            dq_acc[...] += jnp.dot(dzb, k_ref[at, :], preferred_element_type=F32)
            dk_ref[at, :] += lax.dot_general(dzb, qs, _TN, preferred_element_type=F32)
            dv_ref[at, :] += lax.dot_general(wb, dob, _TN, preferred_element_type=F32)
            grun_ref[...] += ginc[:, tk - 1:tk]

        def trip(p, masked):
            for slot in (0, 1):
                fetch(p + 1, slot).start()
            for slot in (0, 1):
                fetch(p, slot).wait()
            for slot in (0, 1):
                products(2 * p + slot + 1, 1 - slot)
                grads(p, slot, masked)

        for slot in (0, 1):
            fetch(0, slot).start()
        products(0, 0)
        n_plain = (nb - tq // tk) // 2

        def plain(p, carry):
            trip(p, False)
            return carry

        def guarded(p, carry):
            trip(p, True)
            return carry

        lax.fori_loop(0, n_plain, plain, 0)
        lax.fori_loop(n_plain, nb // 2, guarded, 0)
        for slot in (0, 1):
            fetch(nb // 2, slot).wait()
        dq_ref[...] = dq_acc[...] * Q_SCALE

    qblk, full, colspec = _head_specs(s, tq)
    big = pltpu.VMEM((tq, tk), F32)
    return pl.pallas_call(
        body, name="sb_bwd", grid=(h, s // tq),
        in_specs=[*_qkv_specs(s, tq, offs), qblk, pl.BlockSpec(memory_space=pl.ANY)], out_specs=[qblk, full, full],
        out_shape=[jax.ShapeDtypeStruct((h, s, HEAD_DIM), F32)] * 3,
        scratch_shapes=[pltpu.VMEM((tq, HEAD_DIM), F32), pltpu.VMEM((tq, 1), F32)]
        + [big] * 4 + [pltpu.VMEM((4, tq, tk), BF16), pltpu.SemaphoreType.DMA((4,))],
        compiler_params=_cparams(("parallel", "arbitrary")),
    )(qkv, qkv, qkv, d_o, w_saved)


def _sum_adamw(parts, w, m, v, name, tr=256):
    _, rows, lanes = parts.shape
    tr = _tile(rows, tr, 16)
    c_m = 1.0 - ADAM_B1 ** ADAM_STEP
    c_v = 1.0 - ADAM_B2 ** ADAM_STEP

    def body(p_ref, w_ref, m_ref, v_ref, g_ref, d_ref, nm_ref, nv_ref):
        g = p_ref[0].astype(F32)
        for j in range(1, N_DEV):
            g = g + p_ref[j].astype(F32)
        nm = ADAM_B1 * m_ref[...] + (1.0 - ADAM_B1) * g
        nv = ADAM_B2 * v_ref[...] + (1.0 - ADAM_B2) * (g * g)
        m_hat = nm / c_m
        v_hat = nv / c_v
        g_ref[...] = g
        d_ref[...] = -ADAM_LR * (m_hat / (jnp.sqrt(v_hat) + ADAM_EPS) + ADAM_WD * w_ref[...])
        nm_ref[...] = nm
        nv_ref[...] = nv

    blk = pl.BlockSpec((tr, lanes), lambda i: (i, 0))
    return pl.pallas_call(
        body, name=name, grid=(rows // tr,),
        in_specs=[pl.BlockSpec((N_DEV, tr, lanes), lambda i: (0, i, 0)), blk, blk, blk],
        out_specs=[blk] * 4, out_shape=[jax.ShapeDtypeStruct((rows, lanes), F32)] * 4,
        compiler_params=_cparams(("parallel",)),
    )(parts, w, m, v)


def kernel(x, attn_norm_g, w_in, forget_bias, fox_out_g, sb_out_g, w_out, ffn_norm_g, w_up, conv_w, conv_b, w_down, final_norm_g, loss_target, m_attn_norm_g, m_w_in, m_forget_bias, m_fox_out_g, m_sb_out_g, m_w_out, m_ffn_norm_g, m_w_up, m_conv_w, m_conv_b, m_w_down, m_final_norm_g, v_attn_norm_g, v_w_in, v_forget_bias, v_fox_out_g, v_sb_out_g, v_w_out, v_ffn_norm_g, v_w_up, v_conv_w, v_conv_b, v_w_down, v_final_norm_g):
    s = x.shape[1]
    xs = x[0]
    tq = min(ATTN_TQ, s)
    tk_fox = min(FOX_TK, s)
    tk_sb = min(SB_TK, s)
    in_shard, up_shard, out_shard, down_shard = IN_COLS // N_DEV, 2 * D_FF // N_DEV, D_MODEL // N_DEV, D_FF // N_DEV

    cw = conv_w[0]
    cw_hi = cw.astype(BF16)
    cw_lo = (cw - cw_hi.astype(F32)).astype(BF16)
    (g_in,) = _all_gather([w_in[0].astype(BF16)])
    rest = _exchange_start([w_out[0].astype(BF16), w_up[0].astype(BF16), w_down[0].astype(BF16),
                            jnp.stack([cw_hi, cw_lo])], False, "weights_rest_start")
    n_gate = QKV_W + N_GROUP_HEADS
    in_windows = _col_windows(N_DEV, in_shard, gap_at=n_gate, gap=GATE_PAD - N_GROUP_HEADS)
    up_windows = _col_windows(N_DEV, up_shard)
    w_in_p = _assemble_cols(g_in, IN_COLS_PAD, in_windows, "assemble_w_in")
    conv_b2 = conv_b.reshape(2, 1, D_FF)

    h1 = _rms_fwd(xs, attn_norm_g + rest[-1][0:1, 0:1])
    proj_h = _mm_heads(h1, w_in_p, "in_proj")
    fox_offs = (0, N_GROUP_HEADS, 2 * N_GROUP_HEADS)
    sb_first = 3 * N_GROUP_HEADS + GATE_PAD // HEAD_DIM
    sb_offs = (sb_first, sb_first + N_GROUP_HEADS, sb_first + 2 * N_GROUP_HEADS)
    f_logit = _mm_nn(h1, w_in_p[:, QKV_W:QKV_W + GATE_PAD], F32, "gate_proj")[:, :N_GROUP_HEADS]
    fv = proj_h[2 * N_GROUP_HEADS:3 * N_GROUP_HEADS]

    f_logit_h = f_logit.T.reshape(N_GROUP_HEADS, s // LANES, LANES)
    bias_h = jnp.broadcast_to(forget_bias.reshape(N_GROUP_HEADS, 1, 1), (N_GROUP_HEADS, 1, LANES))
    big_f = _forget_fwd(f_logit_h, bias_h)
    f_row = big_f.reshape(N_GROUP_HEADS, s // tk_fox, 1, tk_fox)

    fv_ones = jnp.concatenate([fv, jnp.ones_like(fv)], axis=-1)
    fk_ones = jnp.concatenate([proj_h[N_GROUP_HEADS:2 * N_GROUP_HEADS], jnp.ones_like(fv)], axis=-1)
    o_fox_h, lse = _fox_fwd(proj_h, fox_offs, fv_ones, f_row, tq, tk_fox)
    o_sb_h, sb_w = _sb_fwd(proj_h, sb_offs, min(SB_TQ, s), tk_sb)
    g_fox_h = fox_out_g.reshape(N_GROUP_HEADS, 1, HEAD_DIM)
    g_sb_h = sb_out_g.reshape(N_GROUP_HEADS, 1, HEAD_DIM)
    o_n = _group_rms_fwd(o_fox_h, o_sb_h, g_fox_h, g_sb_h)
    g_out, g_up, g_down, g_conv = _exchange_wait(rest, False, o_n, "weights_rest_wait")
    w_out_f = g_out.reshape(D_MODEL, D_MODEL)
    w_up_f = _assemble_cols(g_up, 2 * D_FF, up_windows, "assemble_w_up")
    w_down_f = g_down.reshape(D_FF, D_MODEL)
    conv_w_f = (g_conv[:, 0].astype(F32) + g_conv[:, 1].astype(F32)).transpose(1, 0, 2).reshape(3, 2 * D_FF)
    conv_w2 = conv_w_f.reshape(3, 2, D_FF).transpose(1, 0, 2)
    x1 = _mm_nn(o_n, w_out_f, F32, "out_proj", resid=xs)
    h2 = _rms_fwd(x1, ffn_norm_g)
    up = _mm_up(h2, w_up_f)
    act = _conv_gate_fwd(up, conv_w2, conv_b2)
    x2 = _mm_nn(act, w_down_f, F32, "down_proj", resid=x1, tk=1408)

    d_x2, d_x2b, dg_final, loss_part = _loss_head(x2, loss_target[0], final_norm_g.reshape(1, D_MODEL))
    d_act = _mm_nt(d_x2b, w_down_f, BF16, "d_act", tn=1408)
    dw_down = _mm_tn(act, d_x2b, "d_w_down", tm=1408)
    d_up, dcw2, dcb2 = _conv_gate_bwd(up, d_act, conv_w2, conv_b2)
    d_h2 = _mm_dup_nt(d_up, w_up_f)
    dw_up = _mm_dwup_tn(h2, d_up)
    d_x1, d_x1b, dg_ffn = _rms_bwd(x1, d_h2, ffn_norm_g, d_x2, dy_col=0, name="ffn_norm_bwd", want_bf16=True)
    d_on = _mm_nt(d_x1b, w_out_f, F32, "d_o_normed")
    dw_out = _mm_tn(o_n, d_x1b, "d_w_out")
    early = _exchange_start(
        [dw_out.astype(BF16).reshape(N_DEV, out_shard, D_MODEL),
         _split_cols(dw_up, N_DEV, up_shard, up_windows, "split_d_w_up"),
         dw_down.astype(BF16).reshape(N_DEV, down_shard, D_MODEL)], True, "grads_early_start")
    g_fox_t = g_fox_h + early[-1][0:1, 0:1]
    d_o_fox_h, dg_fox = _group_rms_bwd(o_fox_h, d_on, g_fox_t, dy_col=0, name="fox_norm_bwd")
    d_o_sb_h, dg_sb = _group_rms_bwd(o_sb_h, d_on, g_sb_h, dy_col=1, name="sb_norm_bwd")

    dfq, dfk, dfv, ksum8, qsum = _fox_bwd(proj_h, fox_offs, fk_ones, fv_ones, f_row, o_fox_h, lse, d_o_fox_h, tq, tk_fox)
    dsq, dsk, dsv = _sb_bwd(proj_h, sb_offs, sb_w, d_o_sb_h, min(SB_TQ, s), tk_sb)
    ksum = jnp.sum(ksum8, axis=2).reshape(N_GROUP_HEADS, s // LANES, LANES)
    d_f_logit_h, d_bias_h = _forget_bwd(f_logit_h, bias_h, ksum,
                                        qsum.reshape(N_GROUP_HEADS, s // LANES, LANES))
    d_f_logit = d_f_logit_h.reshape(N_GROUP_HEADS, s).T

    d_proj = _merge_dproj((dfq, dfk, dfv), d_f_logit, (dsq, dsk, dsv))
    dw_in_p = _mm_tn(h1, d_proj, "d_w_in", tn=640)
    dconv_w = dcw2.transpose(1, 0, 2).reshape(3, 2 * D_FF)
    dconv_b = dcb2.reshape(1, 2 * D_FF)
    late = _exchange_start(
        [_split_cols(dw_in_p, N_DEV, in_shard, in_windows, "split_d_w_in"),
         dconv_w.astype(BF16).reshape(3, N_DEV, up_shard).transpose(1, 0, 2)],
        True, "grads_late_start")
    d_h1 = _mm_nt(d_proj, w_in_p + late[-1][0:1, 0:1].astype(BF16), F32, "d_h1", tk=640)
    grad_x, dg_attn = _rms_bwd(xs, d_h1, attn_norm_g, d_x1, dy_col=0, name="attn_norm_bwd", want_bf16=False)

    small_shapes = [(1, D_MODEL), (1, N_GROUP_HEADS), (1, GROUP_W), (1, GROUP_W), (1, D_MODEL),
                    (1, 2 * D_FF), (D_MODEL,), (1,)]
    spack = _pack([dg_attn, d_bias_h[:, 0, 0], dg_fox, dg_sb, dg_ffn, dconv_b, dg_final, loss_part[0, 0:1]],
                  SMALL_ROWS, F32)
    (srecv,) = _grad_exchange([], spack)
    r_out, r_up, r_down = _exchange_wait(early, True, srecv, "grads_early_wait")
    r_in, r_conv = _exchange_wait(late, True, r_out, "grads_late_wait")

    big = [_sum_adamw(g, w_[0], m_[0], v_[0], "adamw_" + tag)
           for g, w_, m_, v_, tag in zip(
               (r_in, r_out, r_up, r_down, r_conv), (w_in, w_out, w_up, w_down, conv_w), (m_w_in, m_w_out, m_w_up, m_w_down, m_conv_w),
               (v_w_in, v_w_out, v_w_up, v_w_down, v_conv_w), ("w_in", "w_out", "w_up", "w_down", "conv_w"))]

    def small_pack(a_attn, a_bias, a_fox, a_sb, a_ffn, a_cb, a_fin):
        return _pack([a_attn, a_bias, a_fox, a_sb, a_ffn, a_cb, a_fin, jnp.zeros((1,), F32)], SMALL_ROWS, F32)

    small = _sum_adamw(srecv, small_pack(attn_norm_g, forget_bias, fox_out_g, sb_out_g, ffn_norm_g, conv_b, final_norm_g),
                       small_pack(m_attn_norm_g, m_forget_bias, m_fox_out_g, m_sb_out_g, m_ffn_norm_g, m_conv_b, m_final_norm_g),
                       small_pack(v_attn_norm_g, v_forget_bias, v_fox_out_g, v_sb_out_g, v_ffn_norm_g, v_conv_b, v_final_norm_g),
                       "adamw_replicated", tr=SMALL_ROWS)

    outs = []
    loss = None
    for kind in range(4):
        b_in, b_out, b_up, b_down, b_conv = (res[kind] for res in big)
        s_attn, s_bias, s_fox, s_sb, s_ffn, s_cb, s_fin, s_loss = _unpack(small[kind], small_shapes)
        if kind == 0:
            loss = s_loss[0]
        outs += [s_attn, b_in[None], s_bias, s_fox, s_sb, b_out[None], s_ffn, b_up[None], b_conv[None], s_cb,
                 b_down[None], s_fin]
    return (loss, grad_x[None], *outs)
```

```python
import jax
import jax.numpy as jnp
from jax import lax
from jax.experimental import pallas as pl
from jax.experimental.pallas import tpu as pltpu

F32 = jnp.float32
BF16 = jnp.bfloat16

D_MODEL = 1024
HEAD_DIM = 64
N_GROUP_HEADS = 8
GROUP_W = N_GROUP_HEADS * HEAD_DIM
QKV_W = 3 * GROUP_W
IN_COLS = 2 * QKV_W + N_GROUP_HEADS
GATE_PAD = 128
IN_COLS_PAD = 2 * QKV_W + GATE_PAD
D_FF = 2816
N_DEV = 8
EPS = 1e-6
Q_SCALE = HEAD_DIM ** -0.5

ADAM_LR = 0.001
ADAM_B1 = 0.9
ADAM_B2 = 0.999
ADAM_EPS = 1e-08
ADAM_WD = 0.01
ADAM_STEP = 10

LANES = 128
SMALL_ROWS = 80
VMEM_LIMIT = 56 * 1024 * 1024
NEG_BIG = -1e30
ATTN_TQ = 512
SB_TQ = 512
FOX_TK = 512
SB_TK = 256
MESH = pl.DeviceIdType.MESH


def _cparams(sem=None, **kw):
    return pltpu.CompilerParams(dimension_semantics=sem, vmem_limit_bytes=VMEM_LIMIT, **kw)


def _tile(n, target, mult=LANES):
    if n <= target:
        return n
    t = (target // mult) * mult
    while t >= mult:
        if n % t == 0:
            return t
        t -= mult
    return n


def _seg_len(shape):
    n = 1
    for s in shape:
        n *= s
    return -(-n // LANES) * LANES


def _pack(arrs, rows, dtype):
    parts = []
    for a in arrs:
        f = a.reshape(-1).astype(dtype)
        parts.append(jnp.pad(f, (0, _seg_len(a.shape) - f.shape[0])))
    flat = jnp.concatenate(parts)
    flat = jnp.pad(flat, (0, rows * LANES - flat.shape[0]))
    return flat.reshape(rows, LANES)


def _unpack(p, shapes, lead=()):
    flat = p.reshape(lead + (-1,))
    out, off = [], 0
    for shp in shapes:
        n = 1
        for s in shp:
            n *= s
        out.append(flat[..., off:off + n].reshape(lead + tuple(shp)))
        off += _seg_len(shp)
    return out


def _my_pos():
    return lax.axis_index("x"), lax.axis_index("y"), lax.axis_index("c")


def _all_gather(blocks):
    n = len(blocks)

    def body(*refs):
        x_refs, out_refs = refs[:n], refs[n:2 * n]
        send_sems, recv_sems, local_sems = refs[2 * n:]
        x, y, c = _my_pos()
        me, sibling = (x, y, c), (x, y, 1 - c)
        chips = [(1 - x, y), (x, 1 - y), (1 - x, 1 - y)]

        def copy(a, k, blk, to, own=False):
            px, py, pc = blk
            slot = out_refs[a].at[4 * px + 2 * py + pc]
            return pltpu.make_async_remote_copy(
                src_ref=x_refs[a] if own else slot, dst_ref=slot,
                send_sem=send_sems.at[a, k], recv_sem=recv_sems.at[a, k],
                device_id=to, device_id_type=MESH)

        mine = [pltpu.make_async_copy(x_refs[a], out_refs[a].at[4 * x + 2 * y + c], local_sems.at[a])
                for a in range(n)]
        for cp in mine:
            cp.start()
        first = []
        for a in range(n):
            first.append(copy(a, 0, me, sibling, own=True))
            first += [copy(a, 1 + j, me, (*chip, c), own=True) for j, chip in enumerate(chips)]
        for cp in first:
            cp.start()
        passed = []
        for j, chip in enumerate(chips):
            for a in range(n):
                copy(a, 1 + j, (*chip, c), me).wait_recv()
                passed.append(copy(a, 4 + j, (*chip, c), sibling))
                passed[-1].start()
        for a in range(n):
            copy(a, 0, sibling, me).wait_recv()
            for j, chip in enumerate(chips):
                copy(a, 4 + j, (*chip, 1 - c), me).wait_recv()
        for cp in first + passed:
            cp.wait_send()
        for cp in mine:
            cp.wait()

    hbm = pl.BlockSpec(memory_space=pl.ANY)
    return pl.pallas_call(
        body, name="weights_all_gather",
        out_shape=[jax.ShapeDtypeStruct((N_DEV,) + b.shape, b.dtype) for b in blocks],
        in_specs=[hbm] * n, out_specs=[hbm] * n,
        scratch_shapes=[pltpu.SemaphoreType.DMA((n, 7)), pltpu.SemaphoreType.DMA((n, 7)),
                        pltpu.SemaphoreType.DMA((n,))],
    )(*blocks)


def _grad_exchange(slabs, spack):
    n = len(slabs) + 1

    def body(*refs):
        in_refs, out_refs = refs[:n], refs[n:2 * n]
        send_sems, recv_sems, local_sems = refs[2 * n:]
        x, y, c = _my_pos()
        my_id = 4 * x + 2 * y + c

        def src_of(a, dev):
            return in_refs[a] if a == n - 1 else in_refs[a].at[dev]

        own = [pltpu.make_async_copy(src_of(a, my_id), out_refs[a].at[my_id], local_sems.at[a])
               for a in range(n)]
        for cp in own:
            cp.start()
        sends, arrivals = [], []
        for k in range(1, N_DEV):
            px, py, pc = x ^ (k >> 2), y ^ ((k >> 1) & 1), c ^ (k & 1)
            peer_id = 4 * px + 2 * py + pc
            for a in range(n):
                for dst_slot, bucket in ((my_id, sends), (peer_id, arrivals)):
                    bucket.append(pltpu.make_async_remote_copy(
                        src_ref=src_of(a, peer_id), dst_ref=out_refs[a].at[dst_slot],
                        send_sem=send_sems.at[a, k - 1], recv_sem=recv_sems.at[a, k - 1],
                        device_id=(px, py, pc), device_id_type=MESH))
        for cp in sends:
            cp.start()
        for cp in arrivals:
            cp.wait_recv()
        for cp in sends:
            cp.wait_send()
        for cp in own:
            cp.wait()

    hbm = pl.BlockSpec(memory_space=pl.ANY)
    return pl.pallas_call(
        body, name="grad_exchange",
        out_shape=[jax.ShapeDtypeStruct(g.shape, g.dtype) for g in slabs]
        + [jax.ShapeDtypeStruct((N_DEV,) + spack.shape, spack.dtype)],
        in_specs=[hbm] * n, out_specs=[hbm] * n,
        scratch_shapes=[pltpu.SemaphoreType.DMA((n, 7)), pltpu.SemaphoreType.DMA((n, 7)),
                        pltpu.SemaphoreType.DMA((n,))],
    )(*slabs, spack)


_HBM = pl.BlockSpec(memory_space=pltpu.HBM)
_SEM = pl.BlockSpec(memory_space=pltpu.SEMAPHORE)
_EFFECT = pltpu.SideEffectType.DATAFLOW_SIDE_EFFECTING


def _my_id():
    x, y, c = _my_pos()
    return 4 * x + 2 * y + c


def _peer_copies(src_refs, land_refs, send_sems, recv_sems, per_peer):
    x, y, c = _my_pos()
    my_id = 4 * x + 2 * y + c
    copies = []
    for k in range(1, N_DEV):
        px, py, pc = x ^ (k >> 2), y ^ ((k >> 1) & 1), c ^ (k & 1)
        for a, (src, land) in enumerate(zip(src_refs, land_refs)):
            copies.append(pltpu.make_async_remote_copy(
                src_ref=src.at[4 * px + 2 * py + pc] if per_peer else src, dst_ref=land.at[my_id],
                send_sem=send_sems.at[a * (N_DEV - 1) + k - 1], recv_sem=recv_sems.at[a * (N_DEV - 1) + k - 1],
                device_id=(px, py, pc), device_id_type=MESH))
    return copies


def _exchange_start(srcs, per_peer, name):
    n = len(srcs)
    lands = [lax.empty(s.shape if per_peer else (N_DEV,) + s.shape, s.dtype) for s in srcs]

    def body(*refs):
        src_refs, land_refs = refs[:n], refs[n:2 * n]
        send_sems, recv_sems = refs[2 * n], refs[2 * n + 1]
        token = refs[-1]
        for cp in _peer_copies(src_refs, land_refs, send_sems, recv_sems, per_peer):
            cp.start()
        token[...] = jnp.zeros_like(token)

    outs = pl.pallas_call(
        body, name=name,
        out_shape=(pltpu.SemaphoreType.DMA((n * (N_DEV - 1),)), pltpu.SemaphoreType.DMA((n * (N_DEV - 1),)),
                   *[pltpu.HBM(a.shape, a.dtype) for a in srcs + lands],
                   jax.ShapeDtypeStruct((8, LANES), F32)),
        in_specs=[_HBM] * (2 * n),
        out_specs=(_SEM, _SEM, *[_HBM] * (2 * n), pl.BlockSpec(memory_space=pltpu.VMEM)),
        input_output_aliases={a: 2 + a for a in range(2 * n)},
        compiler_params=pltpu.CompilerParams(has_side_effects=_EFFECT),
    )(*[pltpu.with_memory_space_constraint(a, pltpu.HBM) for a in srcs + lands])
    return outs[0], outs[1], list(outs[2:2 + n]), list(outs[2 + n:2 + 2 * n]), outs[-1]


def _exchange_wait(handles, per_peer, after, name):
    send_sems, recv_sems, srcs, lands, _ = handles
    n = len(srcs)

    def body(*refs):
        src_refs, land_refs = refs[:n], refs[n:2 * n]
        for cp in _peer_copies(src_refs, land_refs, refs[2 * n], refs[2 * n + 1], per_peer):
            cp.wait_send()
            cp.wait_recv()

    outs = pl.pallas_call(
        body, name=name,
        out_shape=tuple(pltpu.HBM(a.shape, a.dtype) for a in srcs + lands),
        in_specs=[_HBM] * (2 * n) + [_SEM, _SEM, pl.BlockSpec(memory_space=pl.ANY)],
        out_specs=tuple([_HBM] * (2 * n)),
        input_output_aliases={a: a for a in range(2 * n)},
        compiler_params=pltpu.CompilerParams(has_side_effects=_EFFECT),
    )(*srcs, *lands, send_sems, recv_sems, after)
    me = _my_id()
    filled = []
    for src, land in zip(outs[:n], outs[n:]):
        own = lax.dynamic_index_in_dim(src, me, 0, keepdims=True) if per_peer else src[None]
        filled.append(lax.dynamic_update_slice_in_dim(land, own, me, 0))
    return filled


def _col_windows(n_shards, width, gap_at=None, gap=0):
    out = []
    for j in range(n_shards):
        g0, g1 = j * width, (j + 1) * width
        cuts = [g0, g1] if gap_at is None or not g0 < gap_at < g1 else [g0, gap_at, g1]
        for a, b in zip(cuts[:-1], cuts[1:]):
            out.append((j, a - g0, b - g0, a + (gap if gap_at is not None and a >= gap_at else 0)))
    return out


def _assemble_cols(parts, total, windows, name, tr=256):
    n, rows, w = parts.shape
    tr = _tile(rows, tr, 16)

    def body(p_ref, o_ref):
        o_ref[...] = jnp.zeros_like(o_ref)
        for j, lo, hi, dst in windows:
            o_ref[:, dst:dst + hi - lo] = p_ref[j, :, lo:hi]

    return pl.pallas_call(
        body, name=name, grid=(rows // tr,),
        in_specs=[pl.BlockSpec((n, tr, w), lambda i: (0, i, 0))],
        out_specs=pl.BlockSpec((tr, total), lambda i: (i, 0)),
        out_shape=jax.ShapeDtypeStruct((rows, total), parts.dtype),
        compiler_params=_cparams(("parallel",)),
    )(parts)


def _split_cols(full, n, w, windows, name, tr=256):
    rows, total = full.shape
    tr = _tile(rows, tr, 16)

    def body(f_ref, o_ref):
        for j, lo, hi, dst in windows:
            o_ref[j, :, lo:hi] = f_ref[:, dst:dst + hi - lo].astype(o_ref.dtype)

    return pl.pallas_call(
        body, name=name, grid=(rows // tr,),
        in_specs=[pl.BlockSpec((tr, total), lambda i: (i, 0))],
        out_specs=pl.BlockSpec((n, tr, w), lambda i: (0, i, 0)),
        out_shape=jax.ShapeDtypeStruct((n, rows, w), BF16),
        compiler_params=_cparams(("parallel",)),
    )(full)


_DIMS = {"nn": (((1,), (0,)), ((), ())), "nt": (((1,), (1,)), ((), ())), "tn": (((0,), (0,)), ((), ()))}


def _matmul(a, b, *, mode, grid, a_block, a_map, b_block, b_map, o_block, o_map, out_shape, name,
            resid=None):
    nk = grid[2]
    dims = _DIMS[mode]

    def body(*refs):
        if resid is None:
            a_ref, b_ref, o_ref, acc_ref = refs
            r_ref = None
        else:
            a_ref, b_ref, r_ref, o_ref, acc_ref = refs
        k = pl.program_id(2)

        @pl.when(k == 0)
        def _():
            acc_ref[...] = jnp.zeros_like(acc_ref)

        acc_ref[...] += lax.dot_general(a_ref[...], b_ref[...], dims, preferred_element_type=F32)

        @pl.when(k == nk - 1)
        def _():
            res = acc_ref[...]
            if r_ref is not None:
                res = r_ref[...] + res
            o_ref[...] = res.astype(o_ref.dtype)

    in_specs = [pl.BlockSpec(a_block, a_map), pl.BlockSpec(b_block, b_map)]
    args = [a, b]
    if resid is not None:
        in_specs.append(pl.BlockSpec(o_block, o_map))
        args.append(resid)
    acc_shape = tuple(d for d in o_block if d is not None)
    return pl.pallas_call(
        body, name=name, grid=grid, in_specs=in_specs,
        out_specs=pl.BlockSpec(o_block, o_map), out_shape=out_shape,
        scratch_shapes=[pltpu.VMEM(acc_shape, F32)],
        compiler_params=_cparams(("parallel", "parallel", "arbitrary")),
    )(*args)


def _mm_nn(a, b, out_dtype, name, resid=None, tm=1024, tn=1024, tk=1024):
    m, kk = a.shape
    n = b.shape[1]
    tm, tn, tk = _tile(m, tm, 8), _tile(n, tn), _tile(kk, tk)
    return _matmul(a, b, mode="nn", grid=(m // tm, n // tn, kk // tk),
                   a_block=(tm, tk), a_map=lambda i, j, k: (i, k),
                   b_block=(tk, tn), b_map=lambda i, j, k: (k, j),
                   o_block=(tm, tn), o_map=lambda i, j, k: (i, j),
                   out_shape=jax.ShapeDtypeStruct((m, n), out_dtype), name=name, resid=resid)


def _mm_nt(a, b, out_dtype, name, tm=1024, tn=1024, tk=1024):
    m, kk = a.shape
    n = b.shape[0]
    tm, tn, tk = _tile(m, tm, 8), _tile(n, tn), _tile(kk, tk)
    return _matmul(a, b, mode="nt", grid=(m // tm, n // tn, kk // tk),
                   a_block=(tm, tk), a_map=lambda i, j, k: (i, k),
                   b_block=(tn, tk), b_map=lambda i, j, k: (j, k),
                   o_block=(tm, tn), o_map=lambda i, j, k: (i, j),
                   out_shape=jax.ShapeDtypeStruct((m, n), out_dtype), name=name)


def _mm_tn(a, b, name, tm=1024, tn=1024, tk=1024):
    kk, m = a.shape
    n = b.shape[1]
    tm, tn, tk = _tile(m, tm), _tile(n, tn), _tile(kk, tk, 8)
    return _matmul(a, b, mode="tn", grid=(m // tm, n // tn, kk // tk),
                   a_block=(tk, tm), a_map=lambda i, j, k: (k, i),
                   b_block=(tk, tn), b_map=lambda i, j, k: (k, j),
                   o_block=(tm, tn), o_map=lambda i, j, k: (i, j),
                   out_shape=jax.ShapeDtypeStruct((m, n), F32), name=name)


def _mm_heads(a, b, name, tm=2048, tn=640):
    m, kk = a.shape
    n = b.shape[1]
    tm, tn = _tile(m, tm, 16), _tile(n, tn)
    per_tile = tn // HEAD_DIM

    def body(a_ref, b_ref, o_ref):
        res = jnp.dot(a_ref[...], b_ref[...], preferred_element_type=F32)
        for hh in range(per_tile):
            o_ref[hh] = res[:, hh * HEAD_DIM:(hh + 1) * HEAD_DIM].astype(o_ref.dtype)

    return pl.pallas_call(
        body, name=name, grid=(m // tm, n // tn),
        in_specs=[pl.BlockSpec((tm, kk), lambda i, j: (i, 0)), pl.BlockSpec((kk, tn), lambda i, j: (0, j))],
        out_specs=pl.BlockSpec((per_tile, tm, HEAD_DIM), lambda i, j: (j, i, 0)),
        out_shape=jax.ShapeDtypeStruct((n // HEAD_DIM, m, HEAD_DIM), BF16),
        compiler_params=_cparams(("parallel", "parallel")),
    )(a, b)


def _mm_up(h, w_up, tm=2048, tn=256):
    s = h.shape[0]
    tm = _tile(s, tm, 8)
    nh = D_FF // tn
    return _matmul(h, w_up, mode="nn", grid=(s // tm, 2 * nh, 1),
                   a_block=(tm, D_MODEL), a_map=lambda i, j, k: (i, 0),
                   b_block=(D_MODEL, tn), b_map=lambda i, j, k: (0, j),
                   o_block=(None, tm, tn), o_map=lambda i, j, k: (j // nh, i, j % nh),
                   out_shape=jax.ShapeDtypeStruct((2, s, D_FF), F32), name="up_proj")


def _mm_dup_nt(dup, w_up, tm=1024, tk=1408):
    s = dup.shape[1]
    tm = _tile(s, tm, 8)
    nh = D_FF // tk
    return _matmul(dup, w_up, mode="nt", grid=(s // tm, 1, 2 * nh),
                   a_block=(None, tm, tk), a_map=lambda i, j, k: (k // nh, i, k % nh),
                   b_block=(D_MODEL, tk), b_map=lambda i, j, k: (0, k),
                   o_block=(tm, D_MODEL), o_map=lambda i, j, k: (i, 0),
                   out_shape=jax.ShapeDtypeStruct((s, D_MODEL), F32), name="d_h2")


def _mm_dwup_tn(h, dup, tn=1408, tk=1024):
    s = h.shape[0]
    tk = _tile(s, tk, 8)
    nh = D_FF // tn
    return _matmul(h, dup, mode="tn", grid=(1, 2 * nh, s // tk),
                   a_block=(tk, D_MODEL), a_map=lambda i, j, k: (k, 0),
                   b_block=(None, tk, tn), b_map=lambda i, j, k: (j // nh, k, j % nh),
                   o_block=(D_MODEL, tn), o_map=lambda i, j, k: (0, j),
                   out_shape=jax.ShapeDtypeStruct((D_MODEL, 2 * D_FF), F32), name="d_w_up")


def _rms_fwd(x, g, tr=512):
    s, d = x.shape
    tr = _tile(s, tr, 8)

    def body(x_ref, g_ref, o_ref):
        xv = x_ref[...]
        r = lax.rsqrt(jnp.mean(xv * xv, axis=-1, keepdims=True) + EPS)
        o_ref[...] = (xv * r * g_ref[...]).astype(o_ref.dtype)

    return pl.pallas_call(
        body, name="rms_fwd", grid=(s // tr,),
        in_specs=[pl.BlockSpec((tr, d), lambda i: (i, 0)), pl.BlockSpec((1, d), lambda i: (0, 0))],
        out_specs=pl.BlockSpec((tr, d), lambda i: (i, 0)),
        out_shape=jax.ShapeDtypeStruct((s, d), BF16),
        compiler_params=_cparams(("parallel",)),
    )(x, g)


def _group_rms_fwd(o_fox, o_sb, g_fox, g_sb, tr=512):
    nh, s, dh = o_fox.shape
    tr = _tile(s, tr, 8)

    def body(a_ref, b_ref, ga_ref, gb_ref, o_ref):
        for src, g_ref, lo in ((a_ref, ga_ref, 0), (b_ref, gb_ref, nh * dh)):
            heads = [src[hh] for hh in range(nh)]
            ss = heads[0] * heads[0]
            for xv in heads[1:]:
                ss = ss + xv * xv
            r = lax.rsqrt(jnp.sum(ss, axis=-1, keepdims=True) * (1.0 / (nh * dh)) + EPS)
            for hh, xv in enumerate(heads):
                o_ref[:, lo + hh * dh:lo + (hh + 1) * dh] = (xv * r * g_ref[hh]).astype(o_ref.dtype)

    heads_blk = pl.BlockSpec((nh, tr, dh), lambda i: (0, i, 0))
    gain = pl.BlockSpec((nh, 1, dh), lambda i: (0, 0, 0))
    return pl.pallas_call(
        body, name="group_rms_fwd", grid=(s // tr,),
        in_specs=[heads_blk, heads_blk, gain, gain],
        out_specs=pl.BlockSpec((tr, 2 * nh * dh), lambda i: (i, 0)),
        out_shape=jax.ShapeDtypeStruct((s, 2 * nh * dh), BF16),
        compiler_params=_cparams(("parallel",)),
    )(o_fox, o_sb, g_fox, g_sb)


def _group_rms_bwd(x, dy, g, *, dy_col, name, tr=512):
    nh, s, dh = x.shape
    tr = _tile(s, tr, 8)
    d = nh * dh

    def body(x_ref, dy_ref, g_ref, dx_ref, dg_ref):
        @pl.when(pl.program_id(0) == 0)
        def _():
            dg_ref[...] = jnp.zeros_like(dg_ref)

        dyv = dy_ref[...]
        xs_ = [x_ref[hh] for hh in range(nh)]
        dys = [dyv[:, hh * dh:(hh + 1) * dh] for hh in range(nh)]
        ss = xs_[0] * xs_[0]
        for xv in xs_[1:]:
            ss = ss + xv * xv
        r = lax.rsqrt(jnp.sum(ss, axis=-1, keepdims=True) * (1.0 / d) + EPS)
        xh = [xv * r for xv in xs_]
        gy = [dys[hh] * g_ref[hh] for hh in range(nh)]
        dot = xh[0] * gy[0]
        for hh in range(1, nh):
            dot = dot + xh[hh] * gy[hh]
        mean_dot = jnp.sum(dot, axis=-1, keepdims=True) * (1.0 / d)
        for hh in range(nh):
            dx_ref[hh] = r * (gy[hh] - xh[hh] * mean_dot)
            dg_ref[hh] += jnp.sum(dys[hh] * xh[hh], axis=0, keepdims=True)

    heads_blk = pl.BlockSpec((nh, tr, dh), lambda i: (0, i, 0))
    gain = pl.BlockSpec((nh, 1, dh), lambda i: (0, 0, 0))
    return pl.pallas_call(
        body, name=name, grid=(s // tr,),
        in_specs=[heads_blk, pl.BlockSpec((tr, d), lambda i: (i, dy_col)), gain],
        out_specs=[heads_blk, gain],
        out_shape=[jax.ShapeDtypeStruct((nh, s, dh), F32), jax.ShapeDtypeStruct((nh, 1, dh), F32)],
        compiler_params=_cparams(("arbitrary",)),
    )(x, dy, g)


def _merge_dproj(parts_fox, d_gate, parts_sb, tr=512):
    nh, s, dh = parts_fox[0].shape
    tr = _tile(s, tr, 16)

    def body(*refs):
        o_ref = refs[-1]
        gate_ref = refs[3]
        col = 0
        for ref in refs[:3]:
            for hh in range(nh):
                o_ref[:, col:col + dh] = ref[hh].astype(o_ref.dtype)
                col += dh
        o_ref[:, col:col + GATE_PAD] = jnp.zeros((tr, GATE_PAD), o_ref.dtype)
        o_ref[:, col:col + N_GROUP_HEADS] = gate_ref[...].astype(o_ref.dtype)
        col += GATE_PAD
        for ref in refs[4:7]:
            for hh in range(nh):
                o_ref[:, col:col + dh] = ref[hh].astype(o_ref.dtype)
                col += dh

    heads_blk = pl.BlockSpec((nh, tr, dh), lambda i: (0, i, 0))
    return pl.pallas_call(
        body, name="merge_d_proj", grid=(s // tr,),
        in_specs=[heads_blk] * 3 + [pl.BlockSpec((tr, N_GROUP_HEADS), lambda i: (i, 0))] + [heads_blk] * 3,
        out_specs=pl.BlockSpec((tr, IN_COLS_PAD), lambda i: (i, 0)),
        out_shape=jax.ShapeDtypeStruct((s, IN_COLS_PAD), BF16),
        compiler_params=_cparams(("parallel",)),
    )(*parts_fox, d_gate, *parts_sb)


def _rms_bwd(x, dy, g, resid, *, dy_col, name, want_bf16, tr=512):
    s, d = x.shape
    tr = _tile(s, tr, 8)
    has_resid = resid is not None

    def body(*refs):
        refs = list(refs)
        x_ref, dy_ref, g_ref = refs[:3]
        r_ref = refs[3] if has_resid else None
        outs = refs[4:] if has_resid else refs[3:]
        dx_ref = outs[0]
        dxb_ref = outs[1] if want_bf16 else None
        dg_ref = outs[-1]

        @pl.when(pl.program_id(0) == 0)
        def _():
            dg_ref[...] = jnp.zeros_like(dg_ref)

        xv = x_ref[...]
        dyv = dy_ref[...]
        r = lax.rsqrt(jnp.mean(xv * xv, axis=-1, keepdims=True) + EPS)
        xh = xv * r
        gy = dyv * g_ref[...]
        dx = r * (gy - xh * jnp.mean(xh * gy, axis=-1, keepdims=True))
        if r_ref is not None:
            dx = r_ref[...] + dx
        dx_ref[...] = dx
        if dxb_ref is not None:
            dxb_ref[...] = dx.astype(BF16)
        dg_ref[...] += jnp.sum(dyv * xh, axis=0, keepdims=True)

    row = pl.BlockSpec((tr, d), lambda i: (i, 0))
    in_specs = [row, pl.BlockSpec((tr, d), lambda i: (i, dy_col)), pl.BlockSpec((1, d), lambda i: (0, 0))]
    args = [x, dy, g]
    if has_resid:
        in_specs.append(row)
        args.append(resid)
    out_specs = [row]
    out_shape = [jax.ShapeDtypeStruct((s, d), F32)]
    if want_bf16:
        out_specs.append(row)
        out_shape.append(jax.ShapeDtypeStruct((s, d), BF16))
    out_specs.append(pl.BlockSpec((1, d), lambda i: (0, 0)))
    out_shape.append(jax.ShapeDtypeStruct((1, d), F32))
    return pl.pallas_call(
        body, name=name, grid=(s // tr,), in_specs=in_specs, out_specs=out_specs, out_shape=out_shape,
        compiler_params=_cparams(("arbitrary",)),
    )(*args)


def _loss_head(x2, target, g, tr=512):
    s, d = x2.shape
    tr = _tile(s, tr, 8)

    def body(x_ref, t_ref, g_ref, dx_ref, dxb_ref, dg_ref, loss_ref):
        @pl.when(pl.program_id(0) == 0)
        def _():
            dg_ref[...] = jnp.zeros_like(dg_ref)
            loss_ref[...] = jnp.zeros_like(loss_ref)

        xv = x_ref[...]
        gv = g_ref[...]
        r = lax.rsqrt(jnp.mean(xv * xv, axis=-1, keepdims=True) + EPS)
        xh = xv * r
        err = xh * gv - t_ref[...]
        loss_ref[...] += jnp.sum(jnp.mean(err * err, axis=-1, keepdims=True), axis=0, keepdims=True) * 0.5
        dyv = err * (1.0 / d)
        gy = dyv * gv
        dx = r * (gy - xh * jnp.mean(xh * gy, axis=-1, keepdims=True))
        dx_ref[...] = dx
        dxb_ref[...] = dx.astype(BF16)
        dg_ref[...] += jnp.sum(dyv * xh, axis=0, keepdims=True)

    row = pl.BlockSpec((tr, d), lambda i: (i, 0))
    return pl.pallas_call(
        body, name="loss_head", grid=(s // tr,),
        in_specs=[row, row, pl.BlockSpec((1, d), lambda i: (0, 0))],
        out_specs=[row, row, pl.BlockSpec((1, d), lambda i: (0, 0)), pl.BlockSpec((1, LANES), lambda i: (0, 0))],
        out_shape=[jax.ShapeDtypeStruct((s, d), F32), jax.ShapeDtypeStruct((s, d), BF16),
                   jax.ShapeDtypeStruct((1, d), F32), jax.ShapeDtypeStruct((1, LANES), F32)],
        compiler_params=_cparams(("arbitrary",)),
    )(x2, target, g)


def _conv_taps(cur, prev8, w, b, first):
    prev8 = jnp.where(first, 0.0, prev8)
    ext = jnp.concatenate([prev8, cur], axis=0)
    x1 = pltpu.roll(ext, 1, 0)[8:]
    x2 = pltpu.roll(ext, 2, 0)[8:]
    u = b + w[0:1] * x2
    u = u + w[1:2] * x1
    u = u + w[2:3] * cur
    return u, x1, x2


def _conv_gate_fwd(up, conv_w, conv_b, tm=2048, tn=256):
    s = up.shape[1]
    tm = _tile(s, tm, 8)
    nrb = s // tm
    rb8 = tm // 8

    def body(g_ref, v_ref, gp_ref, vp_ref, wg_ref, wv_ref, bg_ref, bv_ref, o_ref):
        first = pl.program_id(1) == 0
        ug, _, _ = _conv_taps(g_ref[...], gp_ref[...], wg_ref[...], bg_ref[...], first)
        uv, _, _ = _conv_taps(v_ref[...], vp_ref[...], wv_ref[...], bv_ref[...], first)
        sg = 1.0 / (1.0 + jnp.exp(-ug))
        o_ref[...] = (ug * sg * uv).astype(o_ref.dtype)

    def cur(h):
        return pl.BlockSpec((None, tm, tn), lambda j, i: (h, i, j))

    def prev(h):
        return pl.BlockSpec((None, 8, tn), lambda j, i: (h, jnp.maximum(i * rb8 - 1, 0), j))

    def par(h, r):
        return pl.BlockSpec((None, r, tn), lambda j, i: (h, 0, j))

    return pl.pallas_call(
        body, name="conv_gate_fwd", grid=(D_FF // tn, nrb),
        in_specs=[cur(0), cur(1), prev(0), prev(1), par(0, 3), par(1, 3), par(0, 1), par(1, 1)],
        out_specs=pl.BlockSpec((tm, tn), lambda j, i: (i, j)),
        out_shape=jax.ShapeDtypeStruct((s, D_FF), BF16),
        compiler_params=_cparams(("parallel", "parallel")),
    )(up, up, up, up, conv_w, conv_w, conv_b, conv_b)


def _conv_gate_bwd(up, dact, conv_w, conv_b, tm=1024, tn=256):
    s = up.shape[1]
    tm = _tile(s, tm, 8)
    nrb = s // tm
    rb8 = tm // 8

    def body(g_ref, v_ref, gp_ref, vp_ref, da_ref, wg_ref, wv_ref, bg_ref, bv_ref,
             dup_ref, dcw_ref, dcb_ref, carry_ref):
        i = pl.program_id(1)
        first = i == nrb - 1

        @pl.when(i == 0)
        def _():
            carry_ref[...] = jnp.zeros_like(carry_ref)
            dcw_ref[...] = jnp.zeros_like(dcw_ref)
            dcb_ref[...] = jnp.zeros_like(dcb_ref)

        curs = (g_ref[...], v_ref[...])
        ws = (wg_ref[...], wv_ref[...])
        ug, g1, g2 = _conv_taps(curs[0], gp_ref[...], ws[0], bg_ref[...], first)
        uv, v1, v2 = _conv_taps(curs[1], vp_ref[...], ws[1], bv_ref[...], first)
        sg = 1.0 / (1.0 + jnp.exp(-ug))
        da = da_ref[...].astype(F32)
        d_v = da * (ug * sg)
        d_g = da * uv * (sg * (1.0 + ug * (1.0 - sg)))
        for h, (du, x0, x1, x2) in enumerate(((d_g, curs[0], g1, g2), (d_v, curs[1], v1, v2))):
            dcb_ref[h] += jnp.sum(du, axis=0, keepdims=True)
            dcw_ref[h, 0:1, :] += jnp.sum(du * x2, axis=0, keepdims=True)
            dcw_ref[h, 1:2, :] += jnp.sum(du * x1, axis=0, keepdims=True)
            dcw_ref[h, 2:3, :] += jnp.sum(du * x0, axis=0, keepdims=True)
            ext = jnp.concatenate([du, carry_ref[h]], axis=0)
            n1 = pltpu.roll(ext, tm + 7, 0)[:tm]
            n2 = pltpu.roll(ext, tm + 6, 0)[:tm]
            w = ws[h]
            dup_ref[h] = (w[2:3] * du + w[1:2] * n1 + w[0:1] * n2).astype(dup_ref.dtype)
            carry_ref[h] = du[:8]

    def cur(h):
        return pl.BlockSpec((None, tm, tn), lambda j, i: (h, nrb - 1 - i, j))

    def prev(h):
        return pl.BlockSpec((None, 8, tn), lambda j, i: (h, jnp.maximum((nrb - 1 - i) * rb8 - 1, 0), j))

    def par(h, r):
        return pl.BlockSpec((None, r, tn), lambda j, i: (h, 0, j))

    return pl.pallas_call(
        body, name="conv_gate_bwd", grid=(D_FF // tn, nrb),
        in_specs=[cur(0), cur(1), prev(0), prev(1),
                  pl.BlockSpec((tm, tn), lambda j, i: (nrb - 1 - i, j)),
                  par(0, 3), par(1, 3), par(0, 1), par(1, 1)],
        out_specs=[pl.BlockSpec((2, tm, tn), lambda j, i: (0, nrb - 1 - i, j)),
                   pl.BlockSpec((2, 3, tn), lambda j, i: (0, 0, j)),
                   pl.BlockSpec((2, 1, tn), lambda j, i: (0, 0, j))],
        out_shape=[jax.ShapeDtypeStruct((2, s, D_FF), BF16),
                   jax.ShapeDtypeStruct((2, 3, D_FF), F32),
                   jax.ShapeDtypeStruct((2, 1, D_FF), F32)],
        scratch_shapes=[pltpu.VMEM((2, 8, tn), F32)],
        compiler_params=_cparams(("parallel", "arbitrary")),
    )(up, up, up, up, dact, conv_w, conv_w, conv_b, conv_b)


def _split_dot(x, tri, terms):
    piece = x.astype(BF16)
    out = jnp.dot(piece, tri, preferred_element_type=F32)
    rest = x
    for _ in range(terms - 1):
        rest = rest - piece.astype(F32)
        piece = rest.astype(BF16)
        out = out + jnp.dot(piece, tri, preferred_element_type=F32)
    return out


def _split_dot_rhs(tri, x, terms):
    piece = x.astype(BF16)
    out = jnp.dot(tri, piece, preferred_element_type=F32)
    rest = x
    for _ in range(terms - 1):
        rest = rest - piece.astype(F32)
        piece = rest.astype(BF16)
        out = out + jnp.dot(tri, piece, preferred_element_type=F32)
    return out


def _tri(n, kind):
    r = lax.broadcasted_iota(jnp.int32, (n, n), 0)
    c = lax.broadcasted_iota(jnp.int32, (n, n), 1)
    cond = {"le": r <= c, "ge": r >= c, "lt": r < c, "gt": r > c}[kind]
    return jnp.where(cond, 1.0, 0.0).astype(BF16)


def _log_sigmoid(x):
    return jnp.minimum(x, 0.0) - jnp.log(1.0 + jnp.exp(-jnp.abs(x)))


def _forget_fwd(f_logit, bias):
    h, r, _ = f_logit.shape

    def body(x_ref, b_ref, o_ref):
        lf = _log_sigmoid(x_ref[...] + b_ref[...])
        within = _split_dot(lf, _tri(LANES, "le"), 3)
        row_tot = jnp.broadcast_to(within[:, LANES - 1:LANES], (r, LANES))
        before = _split_dot_rhs(_tri(r, "gt"), row_tot, 3)
        o_ref[...] = within + before

    blk = pl.BlockSpec((None, r, LANES), lambda i: (i, 0, 0))
    return pl.pallas_call(
        body, name="forget_cumsum_fwd", grid=(h,),
        in_specs=[blk, pl.BlockSpec((None, 1, LANES), lambda i: (i, 0, 0))],
        out_specs=blk, out_shape=jax.ShapeDtypeStruct((h, r, LANES), F32),
        compiler_params=_cparams(("parallel",)),
    )(f_logit, bias)


def _forget_bwd(f_logit, bias, ksum, qsum):
    h, r, _ = f_logit.shape

    def body(x_ref, b_ref, k_ref, q_ref, dx_ref, db_ref):
        d_f = q_ref[...] - k_ref[...]
        within = _split_dot(d_f, _tri(LANES, "ge"), 3)
        row_tot = jnp.broadcast_to(within[:, 0:1], (r, LANES))
        after = _split_dot_rhs(_tri(r, "lt"), row_tot, 3)
        xv = x_ref[...] + b_ref[...]
        dx = (within + after) * jnp.exp(_log_sigmoid(-xv))
        dx_ref[...] = dx
        db_ref[...] = jnp.broadcast_to(jnp.sum(dx), (1, LANES))

    blk = pl.BlockSpec((None, r, LANES), lambda i: (i, 0, 0))
    one = pl.BlockSpec((None, 1, LANES), lambda i: (i, 0, 0))
    return pl.pallas_call(
        body, name="forget_cumsum_bwd", grid=(h,),
        in_specs=[blk, one, blk, blk], out_specs=[blk, one],
        out_shape=[jax.ShapeDtypeStruct((h, r, LANES), F32), jax.ShapeDtypeStruct((h, 1, LANES), F32)],
        compiler_params=_cparams(("parallel",)),
    )(f_logit, bias, ksum, qsum)


def _head_specs(s, tq):
    qblk = pl.BlockSpec((None, tq, HEAD_DIM), lambda h, i: (h, i, 0))
    full = pl.BlockSpec((None, s, HEAD_DIM), lambda h, i: (h, 0, 0))
    col = pl.BlockSpec((None, tq, 1), lambda h, i: (h, i, 0))
    return qblk, full, col


def _qkv_specs(s, tq, offs):
    q_off, k_off, v_off = offs
    return (pl.BlockSpec((None, tq, HEAD_DIM), lambda h, i: (h + q_off, i, 0)),
            pl.BlockSpec((None, s, HEAD_DIM), lambda h, i: (h + k_off, 0, 0)),
            pl.BlockSpec((None, s, HEAD_DIM), lambda h, i: (h + v_off, 0, 0)))


def _scaled(q_ref):
    return (q_ref[...].astype(F32) * Q_SCALE).astype(BF16)


_NT = (((1,), (1,)), ((), ()))
_TN = (((0,), (0,)), ((), ()))


def _cols_minus_rows(rows, cols):
    return lax.broadcasted_iota(jnp.int32, (rows, cols), 1) - lax.broadcasted_iota(jnp.int32, (rows, cols), 0)


def _fox_fwd(qkv, offs, v_ones, f_row, tq, tk):
    h, s = N_GROUP_HEADS, qkv.shape[1]
    nk = s // tk
    assert tq == tk

    def body(q_ref, k_ref, v_ref, fr_ref, o_ref, lse_ref, m_ref, acc_ref, z0, z1):
        i = pl.program_id(1)
        qs = _scaled(q_ref)
        m_ref[...] = jnp.full_like(m_ref, NEG_BIG)
        acc_ref[...] = jnp.zeros_like(acc_ref)

        ahead = _cols_minus_rows(tq, tk)

        def block_of(j):
            return jnp.minimum(j, nk - 1)

        def keys_of(j):
            return pl.ds(pl.multiple_of(block_of(j) * tk, tk), tk)

        def logits(j):
            return lax.dot_general(qs, k_ref[keys_of(j), :], _NT, preferred_element_type=F32)

        def soft(j, raw, masked):
            sc = raw - fr_ref[block_of(j)]
            if masked:
                sc = jnp.where(ahead <= (i - j) * tk, sc, NEG_BIG)
            m_old = m_ref[...]
            m_new = jnp.maximum(m_old, jnp.max(sc, axis=-1, keepdims=True))
            p = jnp.exp(sc - m_new)
            acc_ref[...] = jnp.exp(m_old - m_new) * acc_ref[...] + jnp.dot(
                p.astype(BF16), v_ref[keys_of(j), :], preferred_element_type=F32)
            m_ref[...] = m_new

        z0[...] = logits(0)

        def trip(p, masked):
            j = 2 * p
            z1[...] = logits(j + 1)
            soft(j, z0[...], masked)
            z0[...] = logits(j + 2)
            soft(j + 1, z1[...], masked)

        def step(p, carry):
            trip(p, False)
            return carry

        lax.fori_loop(0, i // 2, step, 0)
        trip(i // 2, True)
        l = acc_ref[:, HEAD_DIM:HEAD_DIM + 1]
        o_ref[...] = acc_ref[:, :HEAD_DIM] / l
        lse_ref[...] = m_ref[...] + jnp.log(l)

    qblk, full, colspec = _head_specs(s, tq)
    q_in, k_in, _ = _qkv_specs(s, tq, offs)
    return pl.pallas_call(
        body, name="fox_fwd", grid=(h, s // tq),
        in_specs=[q_in, k_in, pl.BlockSpec((None, s, 2 * HEAD_DIM), lambda hh, i: (hh, 0, 0)),
                  pl.BlockSpec((None, nk, 1, tk), lambda hh, i: (hh, 0, 0, 0))],
        out_specs=[qblk, colspec],
        out_shape=[jax.ShapeDtypeStruct((h, s, HEAD_DIM), F32), jax.ShapeDtypeStruct((h, s, 1), F32)],
        scratch_shapes=[pltpu.VMEM((tq, 1), F32), pltpu.VMEM((tq, 2 * HEAD_DIM), F32),
                        pltpu.VMEM((tq, tk), F32), pltpu.VMEM((tq, tk), F32)],
        compiler_params=_cparams(("parallel", "parallel")),
    )(qkv, qkv, v_ones, f_row)


def _fox_bwd(qkv, offs, k_ones, v_ones, f_row, o, lse, d_o, tq, tk):
    h, s = N_GROUP_HEADS, qkv.shape[1]
    nk = s // tk
    assert tq == tk

    def body(q_ref, k_ref, v_ref, fr_ref, o_ref, lse_ref, do_ref,
             dq_ref, dk_ref, dv_ref, ks_ref, qs_ref, dq_acc, qsum_acc, z0, z1, p0, p1, doa_ref, qa_ref):
        i = pl.program_id(1)

        @pl.when(i == 0)
        def _():
            dk_ref[...] = jnp.zeros_like(dk_ref)
            dv_ref[...] = jnp.zeros_like(dv_ref)
            ks_ref[...] = jnp.zeros_like(ks_ref)

        qs = _scaled(q_ref)
        lse_v = lse_ref[...]
        dob = do_ref[...].astype(BF16)
        delta = jnp.sum(dob.astype(F32) * o_ref[...], axis=-1, keepdims=True)
        dq_acc[...] = jnp.zeros_like(dq_acc)
        qsum_acc[...] = jnp.zeros_like(qsum_acc)
        d_hi = delta.astype(BF16).astype(F32)
        d_mid = (delta - d_hi).astype(BF16).astype(F32)
        d_lo = (delta - d_hi - d_mid).astype(BF16).astype(F32)
        spare = lax.broadcasted_iota(jnp.int32, (tq, HEAD_DIM), 1)
        doa_ref[:, :HEAD_DIM] = dob
        doa_ref[:, HEAD_DIM:] = jnp.where(spare == 0, -d_hi, jnp.where(
            spare == 1, -d_mid, jnp.where(spare == 2, -d_lo, 0.0))).astype(BF16)
        l_hi = lse_v.astype(BF16).astype(F32)
        l_mid = (lse_v - l_hi).astype(BF16).astype(F32)
        l_lo = (lse_v - l_hi - l_mid).astype(BF16).astype(F32)
        qa_ref[:, :HEAD_DIM] = qs
        qa_ref[:, HEAD_DIM:] = jnp.where(spare == 0, -l_hi, jnp.where(
            spare == 1, -l_mid, jnp.where(spare == 2, -l_lo, 0.0))).astype(BF16)

        ahead = _cols_minus_rows(tq, tk)

        def block_of(j):
            return jnp.minimum(j, nk - 1)

        def keys_of(j):
            return pl.ds(pl.multiple_of(block_of(j) * tk, tk), tk)

        def products(j):
            at = keys_of(j)
            return (lax.dot_general(qa_ref[...], k_ref[at, :], _NT, preferred_element_type=F32),
                    lax.dot_general(doa_ref[...], v_ref[at, :], _NT, preferred_element_type=F32))

        def grads(j, raw, dp, masked):
            at = keys_of(j)
            sc = raw - fr_ref[block_of(j)]
            if masked:
                sc = jnp.where(ahead <= (i - j) * tk, sc, NEG_BIG)
            p = jnp.exp(sc)
            ds = p * dp
            dsb = ds.astype(BF16)
            dq_acc[...] += jnp.dot(dsb, k_ref[at, :], preferred_element_type=F32)[:, :HEAD_DIM]
            dk_ref[at, :] += lax.dot_general(dsb, qs, _TN, preferred_element_type=F32)
            dv_ref[at, :] += lax.dot_general(p.astype(BF16), dob, _TN, preferred_element_type=F32)
            ks_ref[block_of(j)] += jnp.sum(ds.reshape(tq // 8, 8, tk), axis=0)
            qsum_acc[...] += jnp.sum(ds, axis=-1, keepdims=True)

        z0[...], p0[...] = products(0)

        def trip(pp, masked):
            j = 2 * pp
            z1[...], p1[...] = products(j + 1)
            grads(j, z0[...], p0[...], masked)
            z0[...], p0[...] = products(j + 2)
            grads(j + 1, z1[...], p1[...], masked)

        def step(pp, carry):
            trip(pp, False)
            return carry

        lax.fori_loop(0, i // 2, step, 0)
        trip(i // 2, True)
        dq_ref[...] = dq_acc[...] * Q_SCALE
        qs_ref[...] = qsum_acc[...]

    qblk, full, colspec = _head_specs(s, tq)
    frow = pl.BlockSpec((None, nk, 1, tk), lambda hh, i: (hh, 0, 0, 0))
    big = pltpu.VMEM((tq, tk), F32)
    return pl.pallas_call(
        body, name="fox_bwd", grid=(h, s // tq),
        in_specs=[_qkv_specs(s, tq, offs)[0], pl.BlockSpec((None, s, 2 * HEAD_DIM), lambda hh, i: (hh, 0, 0)),
                  pl.BlockSpec((None, s, 2 * HEAD_DIM), lambda hh, i: (hh, 0, 0)),
                  frow, qblk, colspec, qblk],
        out_specs=[qblk, full, full, pl.BlockSpec((None, nk, 8, tk), lambda hh, i: (hh, 0, 0, 0)), colspec],
        out_shape=[jax.ShapeDtypeStruct((h, s, HEAD_DIM), F32)] * 3
        + [jax.ShapeDtypeStruct((h, nk, 8, tk), F32), jax.ShapeDtypeStruct((h, s, 1), F32)],
        scratch_shapes=[pltpu.VMEM((tq, HEAD_DIM), F32), pltpu.VMEM((tq, 1), F32), big, big, big, big,
                        pltpu.VMEM((tq, 2 * HEAD_DIM), BF16), pltpu.VMEM((tq, 2 * HEAD_DIM), BF16)],
        compiler_params=_cparams(("parallel", "arbitrary")),
    )(qkv, k_ones, v_ones, f_row, o, lse, d_o)


SB_TERMS = 2
G_TERMS = 1
LOG2E = 1.4426950408889634
LN2 = 0.6931471805599453


def _softplus2(z2):
    return jnp.maximum(z2, 0.0) + jnp.log2(1.0 + jnp.exp2(-jnp.abs(z2)))


def _sb_fwd(qkv, offs, tq, tk):
    h, s = N_GROUP_HEADS, qkv.shape[1]

    assert tq % (2 * tk) == 0

    def body(q_ref, k_ref, v_ref, o_ref, w_hbm, acc_ref, run_ref, z0, z1, d0, d1, t0, t1, w_stage, wsem):
        z_refs, d_refs, t_refs = (z0, z1), (d0, d1), (t0, t1)
        hh = pl.program_id(0)
        i = pl.program_id(1)
        qs = _scaled(q_ref)
        tri = _tri(tk, "ge")
        acc_ref[...] = jnp.zeros_like(acc_ref)
        run_ref[...] = jnp.zeros_like(run_ref)
        nb = (i + 1) * (tq // tk)
        ahead = _cols_minus_rows(tq, tk)

        def keys_of(b):
            j = nb - 1 - jnp.minimum(b, nb - 1)
            return pl.ds(pl.multiple_of(j * tk, tk), tk)

        def visible(b):
            return ahead < i * tq - (nb - 1 - b) * tk

        def logits(b, slot):
            z_refs[slot][...] = lax.dot_general(qs, k_ref[keys_of(b), :], _NT,
                                                preferred_element_type=F32) * LOG2E

        def sums(b, slot, masked):
            z2 = z_refs[slot][...]
            sp = _softplus2(z2)
            if masked:
                sp = jnp.where(visible(b), sp, 0.0)
            inc = _split_dot(sp, tri, SB_TERMS)
            d_refs[slot][...] = z2 - inc
            t_refs[slot][...] = inc[:, 0:1]

        def put(p, slot):
            st = (p % 2) * 2 + slot
            return pltpu.make_async_copy(w_stage.at[st], w_hbm.at[hh, i, nb - 1 - (2 * p + slot)], wsem.at[st])

        def weigh(p, slot, masked):
            b = 2 * p + slot
            w = jnp.exp2(d_refs[slot][...] - run_ref[...])
            if masked:
                w = jnp.where(visible(b), w, 0.0)
            wb = w.astype(BF16)
            w_stage[(p % 2) * 2 + slot] = wb
            acc_ref[...] += jnp.dot(wb, v_ref[keys_of(b), :], preferred_element_type=F32)
            run_ref[...] += t_refs[slot][...]

        def trip(p, masked):
            @pl.when(p >= 2)
            def _():
                put(p - 2, 0).wait()
                put(p - 2, 1).wait()

            b = 2 * p
            logits(b + 2, 0)
            sums(b + 1, 1, masked)
            weigh(p, 0, masked)
            logits(b + 3, 1)
            sums(b + 2, 0, masked)
            weigh(p, 1, masked)
            put(p, 0).start()
            put(p, 1).start()

        logits(0, 0)
        logits(1, 1)
        sums(0, 0, True)

        def guarded(p, carry):
            trip(p, True)
            return carry

        def plain(p, carry):
            trip(p, False)
            return carry

        lax.fori_loop(0, tq // tk // 2, guarded, 0)
        lax.fori_loop(tq // tk // 2, nb // 2, plain, 0)
        trips = nb // 2

        @pl.when(trips >= 2)
        def _():
            put(trips - 2, 0).wait()
            put(trips - 2, 1).wait()

        put(trips - 1, 0).wait()
        put(trips - 1, 1).wait()
        o_ref[...] = acc_ref[...]

    qblk, full, colspec = _head_specs(s, tq)
    return pl.pallas_call(
        body, name="sb_fwd", grid=(h, s // tq),
        in_specs=[*_qkv_specs(s, tq, offs)], out_specs=[qblk, pl.BlockSpec(memory_space=pl.ANY)],
        out_shape=[jax.ShapeDtypeStruct((h, s, HEAD_DIM), F32),
                   jax.ShapeDtypeStruct((h, s // tq, s // tk, tq, tk), BF16)],
        scratch_shapes=[pltpu.VMEM((tq, HEAD_DIM), F32), pltpu.VMEM((tq, 1), F32),
                        pltpu.VMEM((tq, tk), F32), pltpu.VMEM((tq, tk), F32),
                        pltpu.VMEM((tq, tk), F32), pltpu.VMEM((tq, tk), F32),
                        pltpu.VMEM((tq, 1), F32), pltpu.VMEM((tq, 1), F32),
                        pltpu.VMEM((4, tq, tk), BF16), pltpu.SemaphoreType.DMA((4,))],
        compiler_params=_cparams(("parallel", "parallel")),
    )(qkv, qkv, qkv)


def _sb_bwd(qkv, offs, w_saved, d_o, tq, tk):
    h, s = N_GROUP_HEADS, qkv.shape[1]

    assert tq % (2 * tk) == 0

    def body(q_ref, k_ref, v_ref, do_ref, w_hbm, dq_ref, dk_ref, dv_ref, dq_acc, grun_ref,
             z0, z1, p0, p1, w_bufs, wsem):
        z_refs, p_refs = (z0, z1), (p0, p1)
        hh = pl.program_id(0)
        i = pl.program_id(1)

        @pl.when(i == 0)
        def _():
            dk_ref[...] = jnp.zeros_like(dk_ref)
            dv_ref[...] = jnp.zeros_like(dv_ref)

        qs = _scaled(q_ref)
        dob = do_ref[...].astype(BF16)
        tri = _tri(tk, "le")
        dq_acc[...] = jnp.zeros_like(dq_acc)
        grun_ref[...] = jnp.zeros_like(grun_ref)
        nb = (i + 1) * (tq // tk)
        ahead = _cols_minus_rows(tq, tk)

        def block_of(b):
            return jnp.minimum(b, nb - 1)

        def keys_of(b):
            return pl.ds(pl.multiple_of(block_of(b) * tk, tk), tk)

        def visible(b):
            return ahead < i * tq - b * tk

        def fetch(p, slot):
            st = (p % 2) * 2 + slot
            return pltpu.make_async_copy(w_hbm.at[hh, i, block_of(2 * p + slot)], w_bufs.at[st], wsem.at[st])

        def products(b, slot):
            at = keys_of(b)
            z_refs[slot][...] = lax.dot_general(qs, k_ref[at, :], _NT, preferred_element_type=F32) * LOG2E
            p_refs[slot][...] = lax.dot_general(dob, v_ref[at, :], _NT, preferred_element_type=F32)

        def grads(p, slot, masked):
            b = 2 * p + slot
            at = keys_of(b)
            wb = w_bufs[(p % 2) * 2 + slot]
            g = wb.astype(F32) * p_refs[slot][...]
            ginc = _split_dot(g, tri, G_TERMS)
            beta = 1.0 / (1.0 + jnp.exp2(-z_refs[slot][...]))
            dz = g - beta * (grun_ref[...] + ginc)
            if masked:
                dz = jnp.where(visible(b), dz, 0.0)
            dzb = dz.astype(BF16)
            dq_acc[...] += jnp.dot(dzb, k_ref[at, :], preferred_element_type=F32)
            dk_ref[at, :] += lax.dot_general(dzb, qs, _TN, preferred_element_type=F32)
            dv_ref[at, :] += lax.dot_general(wb, dob, _TN, preferred_element_type=F32)
            grun_ref[...] += ginc[:, tk - 1:tk]

        def trip(p, masked):
            for slot in (0, 1):
                fetch(p + 1, slot).start()
            for slot in (0, 1):
                fetch(p, slot).wait()
            for slot in (0, 1):
                products(2 * p + slot + 1, 1 - slot)
                grads(p, slot, masked)

        for slot in (0, 1):
            fetch(0, slot).start()
        products(0, 0)
        n_plain = (nb - tq // tk) // 2

        def plain(p, carry):
            trip(p, False)
            return carry

        def guarded(p, carry):
            trip(p, True)
            return carry

        lax.fori_loop(0, n_plain, plain, 0)
        lax.fori_loop(n_plain, nb // 2, guarded, 0)
        for slot in (0, 1):
            fetch(nb // 2, slot).wait()
        dq_ref[...] = dq_acc[...] * Q_SCALE

    qblk, full, colspec = _head_specs(s, tq)
    big = pltpu.VMEM((tq, tk), F32)
    return pl.pallas_call(
        body, name="sb_bwd", grid=(h, s // tq),
        in_specs=[*_qkv_specs(s, tq, offs), qblk, pl.BlockSpec(memory_space=pl.ANY)], out_specs=[qblk, full, full],
        out_shape=[jax.ShapeDtypeStruct((h, s, HEAD_DIM), F32)] * 3,
        scratch_shapes=[pltpu.VMEM((tq, HEAD_DIM), F32), pltpu.VMEM((tq, 1), F32)]
        + [big] * 4 + [pltpu.VMEM((4, tq, tk), BF16), pltpu.SemaphoreType.DMA((4,))],
        compiler_params=_cparams(("parallel", "arbitrary")),
    )(qkv, qkv, qkv, d_o, w_saved)


def _sum_adamw(parts, w, m, v, name, tr=256):
    _, rows, lanes = parts.shape
    tr = _tile(rows, tr, 16)
    c_m = 1.0 - ADAM_B1 ** ADAM_STEP
    c_v = 1.0 - ADAM_B2 ** ADAM_STEP

    def body(p_ref, w_ref, m_ref, v_ref, g_ref, d_ref, nm_ref, nv_ref):
        g = p_ref[0].astype(F32)
        for j in range(1, N_DEV):
            g = g + p_ref[j].astype(F32)
        nm = ADAM_B1 * m_ref[...] + (1.0 - ADAM_B1) * g
        nv = ADAM_B2 * v_ref[...] + (1.0 - ADAM_B2) * (g * g)
        m_hat = nm / c_m
        v_hat = nv / c_v
        g_ref[...] = g
        d_ref[...] = -ADAM_LR * (m_hat / (jnp.sqrt(v_hat) + ADAM_EPS) + ADAM_WD * w_ref[...])
        nm_ref[...] = nm
        nv_ref[...] = nv

    blk = pl.BlockSpec((tr, lanes), lambda i: (i, 0))
    return pl.pallas_call(
        body, name=name, grid=(rows // tr,),
        in_specs=[pl.BlockSpec((N_DEV, tr, lanes), lambda i: (0, i, 0)), blk, blk, blk],
        out_specs=[blk] * 4, out_shape=[jax.ShapeDtypeStruct((rows, lanes), F32)] * 4,
        compiler_params=_cparams(("parallel",)),
    )(parts, w, m, v)


def kernel(x, attn_norm_g, w_in, forget_bias, fox_out_g, sb_out_g, w_out, ffn_norm_g, w_up, conv_w, conv_b, w_down, final_norm_g, loss_target, m_attn_norm_g, m_w_in, m_forget_bias, m_fox_out_g, m_sb_out_g, m_w_out, m_ffn_norm_g, m_w_up, m_conv_w, m_conv_b, m_w_down, m_final_norm_g, v_attn_norm_g, v_w_in, v_forget_bias, v_fox_out_g, v_sb_out_g, v_w_out, v_ffn_norm_g, v_w_up, v_conv_w, v_conv_b, v_w_down, v_final_norm_g):
    s = x.shape[1]
    xs = x[0]
    tq = min(ATTN_TQ, s)
    tk_fox = min(FOX_TK, s)
    tk_sb = min(SB_TK, s)
    in_shard, up_shard, out_shard, down_shard = IN_COLS // N_DEV, 2 * D_FF // N_DEV, D_MODEL // N_DEV, D_FF // N_DEV

    cw = conv_w[0]
    cw_hi = cw.astype(BF16)
    cw_lo = (cw - cw_hi.astype(F32)).astype(BF16)
    (g_in,) = _all_gather([w_in[0].astype(BF16)])
    rest = _exchange_start([w_out[0].astype(BF16), w_up[0].astype(BF16), w_down[0].astype(BF16),
                            jnp.stack([cw_hi, cw_lo])], False, "weights_rest_start")
    n_gate = QKV_W + N_GROUP_HEADS
    in_windows = _col_windows(N_DEV, in_shard, gap_at=n_gate, gap=GATE_PAD - N_GROUP_HEADS)
    up_windows = _col_windows(N_DEV, up_shard)
    w_in_p = _assemble_cols(g_in, IN_COLS_PAD, in_windows, "assemble_w_in")
    conv_b2 = conv_b.reshape(2, 1, D_FF)

    h1 = _rms_fwd(xs, attn_norm_g + rest[-1][0:1, 0:1])
    proj_h = _mm_heads(h1, w_in_p, "in_proj")
    fox_offs = (0, N_GROUP_HEADS, 2 * N_GROUP_HEADS)
    sb_first = 3 * N_GROUP_HEADS + GATE_PAD // HEAD_DIM
    sb_offs = (sb_first, sb_first + N_GROUP_HEADS, sb_first + 2 * N_GROUP_HEADS)
    f_logit = _mm_nn(h1, w_in_p[:, QKV_W:QKV_W + GATE_PAD], F32, "gate_proj")[:, :N_GROUP_HEADS]
    fv = proj_h[2 * N_GROUP_HEADS:3 * N_GROUP_HEADS]

    f_logit_h = f_logit.T.reshape(N_GROUP_HEADS, s // LANES, LANES)
    bias_h = jnp.broadcast_to(forget_bias.reshape(N_GROUP_HEADS, 1, 1), (N_GROUP_HEADS, 1, LANES))
    big_f = _forget_fwd(f_logit_h, bias_h)
    f_row = big_f.reshape(N_GROUP_HEADS, s // tk_fox, 1, tk_fox)

    fv_ones = jnp.concatenate([fv, jnp.ones_like(fv)], axis=-1)
    fk_ones = jnp.concatenate([proj_h[N_GROUP_HEADS:2 * N_GROUP_HEADS], jnp.ones_like(fv)], axis=-1)
    o_fox_h, lse = _fox_fwd(proj_h, fox_offs, fv_ones, f_row, tq, tk_fox)
    o_sb_h, sb_w = _sb_fwd(proj_h, sb_offs, min(SB_TQ, s), tk_sb)
    g_fox_h = fox_out_g.reshape(N_GROUP_HEADS, 1, HEAD_DIM)
    g_sb_h = sb_out_g.reshape(N_GROUP_HEADS, 1, HEAD_DIM)
    o_n = _group_rms_fwd(o_fox_h, o_sb_h, g_fox_h, g_sb_h)
    g_out, g_up, g_down, g_conv = _exchange_wait(rest, False, o_n, "weights_rest_wait")
    w_out_f = g_out.reshape(D_MODEL, D_MODEL)
    w_up_f = _assemble_cols(g_up, 2 * D_FF, up_windows, "assemble_w_up")
    w_down_f = g_down.reshape(D_FF, D_MODEL)
    conv_w_f = (g_conv[:, 0].astype(F32) + g_conv[:, 1].astype(F32)).transpose(1, 0, 2).reshape(3, 2 * D_FF)
    conv_w2 = conv_w_f.reshape(3, 2, D_FF).transpose(1, 0, 2)
    x1 = _mm_nn(o_n, w_out_f, F32, "out_proj", resid=xs)
    h2 = _rms_fwd(x1, ffn_norm_g)
    up = _mm_up(h2, w_up_f)
    act = _conv_gate_fwd(up, conv_w2, conv_b2)
    x2 = _mm_nn(act, w_down_f, F32, "down_proj", resid=x1, tk=1408)

    d_x2, d_x2b, dg_final, loss_part = _loss_head(x2, loss_target[0], final_norm_g.reshape(1, D_MODEL))
    d_act = _mm_nt(d_x2b, w_down_f, BF16, "d_act", tn=1408)
    dw_down = _mm_tn(act, d_x2b, "d_w_down", tm=1408)
    d_up, dcw2, dcb2 = _conv_gate_bwd(up, d_act, conv_w2, conv_b2)
    d_h2 = _mm_dup_nt(d_up, w_up_f)
    dw_up = _mm_dwup_tn(h2, d_up)
    d_x1, d_x1b, dg_ffn = _rms_bwd(x1, d_h2, ffn_norm_g, d_x2, dy_col=0, name="ffn_norm_bwd", want_bf16=True)
    d_on = _mm_nt(d_x1b, w_out_f, F32, "d_o_normed")
    dw_out = _mm_tn(o_n, d_x1b, "d_w_out")
    early = _exchange_start(
        [dw_out.astype(BF16).reshape(N_DEV, out_shard, D_MODEL),
         _split_cols(dw_up, N_DEV, up_shard, up_windows, "split_d_w_up"),
         dw_down.astype(BF16).reshape(N_DEV, down_shard, D_MODEL)], True, "grads_early_start")
    g_fox_t = g_fox_h + early[-1][0:1, 0:1]
    d_o_fox_h, dg_fox = _group_rms_bwd(o_fox_h, d_on, g_fox_t, dy_col=0, name="fox_norm_bwd")
    d_o_sb_h, dg_sb = _group_rms_bwd(o_sb_h, d_on, g_sb_h, dy_col=1, name="sb_norm_bwd")

    dfq, dfk, dfv, ksum8, qsum = _fox_bwd(proj_h, fox_offs, fk_ones, fv_ones, f_row, o_fox_h, lse, d_o_fox_h, tq, tk_fox)
    dsq, dsk, dsv = _sb_bwd(proj_h, sb_offs, sb_w, d_o_sb_h, min(SB_TQ, s), tk_sb)
    ksum = jnp.sum(ksum8, axis=2).reshape(N_GROUP_HEADS, s // LANES, LANES)
    d_f_logit_h, d_bias_h = _forget_bwd(f_logit_h, bias_h, ksum,
                                        qsum.reshape(N_GROUP_HEADS, s // LANES, LANES))
    d_f_logit = d_f_logit_h.reshape(N_GROUP_HEADS, s).T

    d_proj = _merge_dproj((dfq, dfk, dfv), d_f_logit, (dsq, dsk, dsv))
    dw_in_p = _mm_tn(h1, d_proj, "d_w_in", tn=640)
    dconv_w = dcw2.transpose(1, 0, 2).reshape(3, 2 * D_FF)
    dconv_b = dcb2.reshape(1, 2 * D_FF)
    late = _exchange_start(
        [_split_cols(dw_in_p, N_DEV, in_shard, in_windows, "split_d_w_in"),
         dconv_w.astype(BF16).reshape(3, N_DEV, up_shard).transpose(1, 0, 2)],
        True, "grads_late_start")
    d_h1 = _mm_nt(d_proj, w_in_p + late[-1][0:1, 0:1].astype(BF16), F32, "d_h1", tk=640)
    grad_x, dg_attn = _rms_bwd(xs, d_h1, attn_norm_g, d_x1, dy_col=0, name="attn_norm_bwd", want_bf16=False)

    small_shapes = [(1, D_MODEL), (1, N_GROUP_HEADS), (1, GROUP_W), (1, GROUP_W), (1, D_MODEL),
                    (1, 2 * D_FF), (D_MODEL,), (1,)]
    spack = _pack([dg_attn, d_bias_h[:, 0, 0], dg_fox, dg_sb, dg_ffn, dconv_b, dg_final, loss_part[0, 0:1]],
                  SMALL_ROWS, F32)
    (srecv,) = _grad_exchange([], spack)
    r_out, r_up, r_down = _exchange_wait(early, True, srecv, "grads_early_wait")
    r_in, r_conv = _exchange_wait(late, True, r_out, "grads_late_wait")

    big = [_sum_adamw(g, w_[0], m_[0], v_[0], "adamw_" + tag)
           for g, w_, m_, v_, tag in zip(
               (r_in, r_out, r_up, r_down, r_conv), (w_in, w_out, w_up, w_down, conv_w), (m_w_in, m_w_out, m_w_up, m_w_down, m_conv_w),
               (v_w_in, v_w_out, v_w_up, v_w_down, v_conv_w), ("w_in", "w_out", "w_up", "w_down", "conv_w"))]

    def small_pack(a_attn, a_bias, a_fox, a_sb, a_ffn, a_cb, a_fin):
        return _pack([a_attn, a_bias, a_fox, a_sb, a_ffn, a_cb, a_fin, jnp.zeros((1,), F32)], SMALL_ROWS, F32)

    small = _sum_adamw(srecv, small_pack(attn_norm_g, forget_bias, fox_out_g, sb_out_g, ffn_norm_g, conv_b, final_norm_g),
                       small_pack(m_attn_norm_g, m_forget_bias, m_fox_out_g, m_sb_out_g, m_ffn_norm_g, m_conv_b, m_final_norm_g),
                       small_pack(v_attn_norm_g, v_forget_bias, v_fox_out_g, v_sb_out_g, v_ffn_norm_g, v_conv_b, v_final_norm_g),
                       "adamw_replicated", tr=SMALL_ROWS)

    outs = []
    loss = None
    for kind in range(4):
        b_in, b_out, b_up, b_down, b_conv = (res[kind] for res in big)
        s_attn, s_bias, s_fox, s_sb, s_ffn, s_cb, s_fin, s_loss = _unpack(small[kind], small_shapes)
        if kind == 0:
            loss = s_loss[0]
        outs += [s_attn, b_in[None], s_bias, s_fox, s_sb, b_out[None], s_ffn, b_up[None], b_conv[None], s_cb,
                 b_down[None], s_fin]
    return (loss, grad_x[None], *outs)
```

```python
import jax
import jax.numpy as jnp
from jax import lax
from jax.experimental import pallas as pl
from jax.experimental.pallas import tpu as pltpu

F32 = jnp.float32
BF16 = jnp.bfloat16

D_MODEL = 1024
HEAD_DIM = 64
N_GROUP_HEADS = 8
GROUP_W = N_GROUP_HEADS * HEAD_DIM
QKV_W = 3 * GROUP_W
IN_COLS = 2 * QKV_W + N_GROUP_HEADS
GATE_PAD = 128
IN_COLS_PAD = 2 * QKV_W + GATE_PAD
D_FF = 2816
N_DEV = 8
EPS = 1e-6
Q_SCALE = HEAD_DIM ** -0.5

ADAM_LR = 0.001
ADAM_B1 = 0.9
ADAM_B2 = 0.999
ADAM_EPS = 1e-08
ADAM_WD = 0.01
ADAM_STEP = 10

LANES = 128
SMALL_ROWS = 80
VMEM_LIMIT = 56 * 1024 * 1024
NEG_BIG = -1e30
ATTN_TQ = 512
SB_TQ = 512
FOX_TK = 512
SB_TK = 256
MESH = pl.DeviceIdType.MESH


def _cparams(sem=None, **kw):
    return pltpu.CompilerParams(dimension_semantics=sem, vmem_limit_bytes=VMEM_LIMIT, **kw)


def _tile(n, target, mult=LANES):
    if n <= target:
        return n
    t = (target // mult) * mult
    while t >= mult:
        if n % t == 0:
            return t
        t -= mult
    return n


def _seg_len(shape):
    n = 1
    for s in shape:
        n *= s
    return -(-n // LANES) * LANES


def _pack(arrs, rows, dtype):
    parts = []
    for a in arrs:
        f = a.reshape(-1).astype(dtype)
        parts.append(jnp.pad(f, (0, _seg_len(a.shape) - f.shape[0])))
    flat = jnp.concatenate(parts)
    flat = jnp.pad(flat, (0, rows * LANES - flat.shape[0]))
    return flat.reshape(rows, LANES)


def _unpack(p, shapes, lead=()):
    flat = p.reshape(lead + (-1,))
    out, off = [], 0
    for shp in shapes:
        n = 1
        for s in shp:
            n *= s
        out.append(flat[..., off:off + n].reshape(lead + tuple(shp)))
        off += _seg_len(shp)
    return out


def _my_pos():
    return lax.axis_index("x"), lax.axis_index("y"), lax.axis_index("c")


def _all_gather(blocks):
    n = len(blocks)

    def body(*refs):
        x_refs, out_refs = refs[:n], refs[n:2 * n]
        send_sems, recv_sems, local_sems = refs[2 * n:]
        x, y, c = _my_pos()
        me, sibling = (x, y, c), (x, y, 1 - c)
        chips = [(1 - x, y), (x, 1 - y), (1 - x, 1 - y)]

        def copy(a, k, blk, to, own=False):
            px, py, pc = blk
            slot = out_refs[a].at[4 * px + 2 * py + pc]
            return pltpu.make_async_remote_copy(
                src_ref=x_refs[a] if own else slot, dst_ref=slot,
                send_sem=send_sems.at[a, k], recv_sem=recv_sems.at[a, k],
                device_id=to, device_id_type=MESH)

        mine = [pltpu.make_async_copy(x_refs[a], out_refs[a].at[4 * x + 2 * y + c], local_sems.at[a])
                for a in range(n)]
        for cp in mine:
            cp.start()
        first = []
        for a in range(n):
            first.append(copy(a, 0, me, sibling, own=True))
            first += [copy(a, 1 + j, me, (*chip, c), own=True) for j, chip in enumerate(chips)]
        for cp in first:
            cp.start()
        passed = []
        for j, chip in enumerate(chips):
            for a in range(n):
                copy(a, 1 + j, (*chip, c), me).wait_recv()
                passed.append(copy(a, 4 + j, (*chip, c), sibling))
                passed[-1].start()
        for a in range(n):
            copy(a, 0, sibling, me).wait_recv()
            for j, chip in enumerate(chips):
                copy(a, 4 + j, (*chip, 1 - c), me).wait_recv()
        for cp in first + passed:
            cp.wait_send()
        for cp in mine:
            cp.wait()

    hbm = pl.BlockSpec(memory_space=pl.ANY)
    return pl.pallas_call(
        body, name="weights_all_gather",
        out_shape=[jax.ShapeDtypeStruct((N_DEV,) + b.shape, b.dtype) for b in blocks],
        in_specs=[hbm] * n, out_specs=[hbm] * n,
        scratch_shapes=[pltpu.SemaphoreType.DMA((n, 7)), pltpu.SemaphoreType.DMA((n, 7)),
                        pltpu.SemaphoreType.DMA((n,))],
    )(*blocks)


def _grad_exchange(slabs, spack):
    n = len(slabs) + 1

    def body(*refs):
        in_refs, out_refs = refs[:n], refs[n:2 * n]
        send_sems, recv_sems, local_sems = refs[2 * n:]
        x, y, c = _my_pos()
        my_id = 4 * x + 2 * y + c

        def src_of(a, dev):
            return in_refs[a] if a == n - 1 else in_refs[a].at[dev]

        own = [pltpu.make_async_copy(src_of(a, my_id), out_refs[a].at[my_id], local_sems.at[a])
               for a in range(n)]
        for cp in own:
            cp.start()
        sends, arrivals = [], []
        for k in range(1, N_DEV):
            px, py, pc = x ^ (k >> 2), y ^ ((k >> 1) & 1), c ^ (k & 1)
            peer_id = 4 * px + 2 * py + pc
            for a in range(n):
                for dst_slot, bucket in ((my_id, sends), (peer_id, arrivals)):
                    bucket.append(pltpu.make_async_remote_copy(
                        src_ref=src_of(a, peer_id), dst_ref=out_refs[a].at[dst_slot],
                        send_sem=send_sems.at[a, k - 1], recv_sem=recv_sems.at[a, k - 1],
                        device_id=(px, py, pc), device_id_type=MESH))
        for cp in sends:
            cp.start()
        for cp in arrivals:
            cp.wait_recv()
        for cp in sends:
            cp.wait_send()
        for cp in own:
            cp.wait()

    hbm = pl.BlockSpec(memory_space=pl.ANY)
    return pl.pallas_call(
        body, name="grad_exchange",
        out_shape=[jax.ShapeDtypeStruct(g.shape, g.dtype) for g in slabs]
        + [jax.ShapeDtypeStruct((N_DEV,) + spack.shape, spack.dtype)],
        in_specs=[hbm] * n, out_specs=[hbm] * n,
        scratch_shapes=[pltpu.SemaphoreType.DMA((n, 7)), pltpu.SemaphoreType.DMA((n, 7)),
                        pltpu.SemaphoreType.DMA((n,))],
    )(*slabs, spack)


_HBM = pl.BlockSpec(memory_space=pltpu.HBM)
_SEM = pl.BlockSpec(memory_space=pltpu.SEMAPHORE)
_EFFECT = pltpu.SideEffectType.DATAFLOW_SIDE_EFFECTING


def _my_id():
    x, y, c = _my_pos()
    return 4 * x + 2 * y + c


def _peer_copies(src_refs, land_refs, send_sems, recv_sems, per_peer):
    x, y, c = _my_pos()
    my_id = 4 * x + 2 * y + c
    copies = []
    for k in range(1, N_DEV):
        px, py, pc = x ^ (k >> 2), y ^ ((k >> 1) & 1), c ^ (k & 1)
        for a, (src, land) in enumerate(zip(src_refs, land_refs)):
            copies.append(pltpu.make_async_remote_copy(
                src_ref=src.at[4 * px + 2 * py + pc] if per_peer else src, dst_ref=land.at[my_id],
                send_sem=send_sems.at[a * (N_DEV - 1) + k - 1], recv_sem=recv_sems.at[a * (N_DEV - 1) + k - 1],
                device_id=(px, py, pc), device_id_type=MESH))
    return copies


def _exchange_start(srcs, per_peer, name):
    n = len(srcs)
    lands = [lax.empty(s.shape if per_peer else (N_DEV,) + s.shape, s.dtype) for s in srcs]

    def body(*refs):
        src_refs, land_refs = refs[:n], refs[n:2 * n]
        send_sems, recv_sems = refs[2 * n], refs[2 * n + 1]
        token = refs[-1]
        for cp in _peer_copies(src_refs, land_refs, send_sems, recv_sems, per_peer):
            cp.start()
        token[...] = jnp.zeros_like(token)

    outs = pl.pallas_call(
        body, name=name,
        out_shape=(pltpu.SemaphoreType.DMA((n * (N_DEV - 1),)), pltpu.SemaphoreType.DMA((n * (N_DEV - 1),)),
                   *[pltpu.HBM(a.shape, a.dtype) for a in srcs + lands],
                   jax.ShapeDtypeStruct((8, LANES), F32)),
        in_specs=[_HBM] * (2 * n),
        out_specs=(_SEM, _SEM, *[_HBM] * (2 * n), pl.BlockSpec(memory_space=pltpu.VMEM)),
        input_output_aliases={a: 2 + a for a in range(2 * n)},
        compiler_params=pltpu.CompilerParams(has_side_effects=_EFFECT),
    )(*[pltpu.with_memory_space_constraint(a, pltpu.HBM) for a in srcs + lands])
    return outs[0], outs[1], list(outs[2:2 + n]), list(outs[2 + n:2 + 2 * n]), outs[-1]


def _exchange_wait(handles, per_peer, after, name):
    send_sems, recv_sems, srcs, lands, _ = handles
    n = len(srcs)

    def body(*refs):
        src_refs, land_refs = refs[:n], refs[n:2 * n]
        for cp in _peer_copies(src_refs, land_refs, refs[2 * n], refs[2 * n + 1], per_peer):
            cp.wait_send()
            cp.wait_recv()

    outs = pl.pallas_call(
        body, name=name,
        out_shape=tuple(pltpu.HBM(a.shape, a.dtype) for a in srcs + lands),
        in_specs=[_HBM] * (2 * n) + [_SEM, _SEM, pl.BlockSpec(memory_space=pl.ANY)],
        out_specs=tuple([_HBM] * (2 * n)),
        input_output_aliases={a: a for a in range(2 * n)},
        compiler_params=pltpu.CompilerParams(has_side_effects=_EFFECT),
    )(*srcs, *lands, send_sems, recv_sems, after)
    me = _my_id()
    filled = []
    for src, land in zip(outs[:n], outs[n:]):
        own = lax.dynamic_index_in_dim(src, me, 0, keepdims=True) if per_peer else src[None]
        filled.append(lax.dynamic_update_slice_in_dim(land, own, me, 0))
    return filled


def _col_windows(n_shards, width, gap_at=None, gap=0):
    out = []
    for j in range(n_shards):
        g0, g1 = j * width, (j + 1) * width
        cuts = [g0, g1] if gap_at is None or not g0 < gap_at < g1 else [g0, gap_at, g1]
        for a, b in zip(cuts[:-1], cuts[1:]):
            out.append((j, a - g0, b - g0, a + (gap if gap_at is not None and a >= gap_at else 0)))
    return out


def _assemble_cols(parts, total, windows, name, tr=256):
    n, rows, w = parts.shape
    tr = _tile(rows, tr, 16)

    def body(p_ref, o_ref):
        o_ref[...] = jnp.zeros_like(o_ref)
        for j, lo, hi, dst in windows:
            o_ref[:, dst:dst + hi - lo] = p_ref[j, :, lo:hi]

    return pl.pallas_call(
        body, name=name, grid=(rows // tr,),
        in_specs=[pl.BlockSpec((n, tr, w), lambda i: (0, i, 0))],
        out_specs=pl.BlockSpec((tr, total), lambda i: (i, 0)),
        out_shape=jax.ShapeDtypeStruct((rows, total), parts.dtype),
        compiler_params=_cparams(("parallel",)),
    )(parts)


def _split_cols(full, n, w, windows, name, tr=256):
    rows, total = full.shape
    tr = _tile(rows, tr, 16)

    def body(f_ref, o_ref):
        for j, lo, hi, dst in windows:
            o_ref[j, :, lo:hi] = f_ref[:, dst:dst + hi - lo].astype(o_ref.dtype)

    return pl.pallas_call(
        body, name=name, grid=(rows // tr,),
        in_specs=[pl.BlockSpec((tr, total), lambda i: (i, 0))],
        out_specs=pl.BlockSpec((n, tr, w), lambda i: (0, i, 0)),
        out_shape=jax.ShapeDtypeStruct((n, rows, w), BF16),
        compiler_params=_cparams(("parallel",)),
    )(full)


_DIMS = {"nn": (((1,), (0,)), ((), ())), "nt": (((1,), (1,)), ((), ())), "tn": (((0,), (0,)), ((), ()))}


def _matmul(a, b, *, mode, grid, a_block, a_map, b_block, b_map, o_block, o_map, out_shape, name,
            resid=None):
    nk = grid[2]
    dims = _DIMS[mode]

    def body(*refs):
        if resid is None:
            a_ref, b_ref, o_ref, acc_ref = refs
            r_ref = None
        else:
            a_ref, b_ref, r_ref, o_ref, acc_ref = refs
        k = pl.program_id(2)

        @pl.when(k == 0)
        def _():
            acc_ref[...] = jnp.zeros_like(acc_ref)

        acc_ref[...] += lax.dot_general(a_ref[...], b_ref[...], dims, preferred_element_type=F32)

        @pl.when(k == nk - 1)
        def _():
            res = acc_ref[...]
            if r_ref is not None:
                res = r_ref[...] + res
            o_ref[...] = res.astype(o_ref.dtype)

    in_specs = [pl.BlockSpec(a_block, a_map), pl.BlockSpec(b_block, b_map)]
    args = [a, b]
    if resid is not None:
        in_specs.append(pl.BlockSpec(o_block, o_map))
        args.append(resid)
    acc_shape = tuple(d for d in o_block if d is not None)
    return pl.pallas_call(
        body, name=name, grid=grid, in_specs=in_specs,
        out_specs=pl.BlockSpec(o_block, o_map), out_shape=out_shape,
        scratch_shapes=[pltpu.VMEM(acc_shape, F32)],
        compiler_params=_cparams(("parallel", "parallel", "arbitrary")),
    )(*args)


def _mm_nn(a, b, out_dtype, name, resid=None, tm=1024, tn=1024, tk=1024):
    m, kk = a.shape
    n = b.shape[1]
    tm, tn, tk = _tile(m, tm, 8), _tile(n, tn), _tile(kk, tk)
    return _matmul(a, b, mode="nn", grid=(m // tm, n // tn, kk // tk),
                   a_block=(tm, tk), a_map=lambda i, j, k: (i, k),
                   b_block=(tk, tn), b_map=lambda i, j, k: (k, j),
                   o_block=(tm, tn), o_map=lambda i, j, k: (i, j),
                   out_shape=jax.ShapeDtypeStruct((m, n), out_dtype), name=name, resid=resid)


def _mm_nt(a, b, out_dtype, name, tm=1024, tn=1024, tk=1024):
    m, kk = a.shape
    n = b.shape[0]
    tm, tn, tk = _tile(m, tm, 8), _tile(n, tn), _tile(kk, tk)
    return _matmul(a, b, mode="nt", grid=(m // tm, n // tn, kk // tk),
                   a_block=(tm, tk), a_map=lambda i, j, k: (i, k),
                   b_block=(tn, tk), b_map=lambda i, j, k: (j, k),
                   o_block=(tm, tn), o_map=lambda i, j, k: (i, j),
                   out_shape=jax.ShapeDtypeStruct((m, n), out_dtype), name=name)


def _mm_tn(a, b, name, tm=1024, tn=1024, tk=1024):
    kk, m = a.shape
    n = b.shape[1]
    tm, tn, tk = _tile(m, tm), _tile(n, tn), _tile(kk, tk, 8)
    return _matmul(a, b, mode="tn", grid=(m // tm, n // tn, kk // tk),
                   a_block=(tk, tm), a_map=lambda i, j, k: (k, i),
                   b_block=(tk, tn), b_map=lambda i, j, k: (k, j),
                   o_block=(tm, tn), o_map=lambda i, j, k: (i, j),
                   out_shape=jax.ShapeDtypeStruct((m, n), F32), name=name)


def _mm_heads(a, b, name, tm=2048, tn=640):
    m, kk = a.shape
    n = b.shape[1]
    tm, tn = _tile(m, tm, 16), _tile(n, tn)
    per_tile = tn // HEAD_DIM

    def body(a_ref, b_ref, o_ref):
        res = jnp.dot(a_ref[...], b_ref[...], preferred_element_type=F32)
        for hh in range(per_tile):
            o_ref[hh] = res[:, hh * HEAD_DIM:(hh + 1) * HEAD_DIM].astype(o_ref.dtype)

    return pl.pallas_call(
        body, name=name, grid=(m // tm, n // tn),
        in_specs=[pl.BlockSpec((tm, kk), lambda i, j: (i, 0)), pl.BlockSpec((kk, tn), lambda i, j: (0, j))],
        out_specs=pl.BlockSpec((per_tile, tm, HEAD_DIM), lambda i, j: (j, i, 0)),
        out_shape=jax.ShapeDtypeStruct((n // HEAD_DIM, m, HEAD_DIM), BF16),
        compiler_params=_cparams(("parallel", "parallel")),
    )(a, b)


def _mm_up(h, w_up, tm=2048, tn=256):
    s = h.shape[0]
    tm = _tile(s, tm, 8)
    nh = D_FF // tn
    return _matmul(h, w_up, mode="nn", grid=(s // tm, 2 * nh, 1),
                   a_block=(tm, D_MODEL), a_map=lambda i, j, k: (i, 0),
                   b_block=(D_MODEL, tn), b_map=lambda i, j, k: (0, j),
                   o_block=(None, tm, tn), o_map=lambda i, j, k: (j // nh, i, j % nh),
                   out_shape=jax.ShapeDtypeStruct((2, s, D_FF), F32), name="up_proj")


def _mm_dup_nt(dup, w_up, tm=1024, tk=1408):
    s = dup.shape[1]
    tm = _tile(s, tm, 8)
    nh = D_FF // tk
    return _matmul(dup, w_up, mode="nt", grid=(s // tm, 1, 2 * nh),
                   a_block=(None, tm, tk), a_map=lambda i, j, k: (k // nh, i, k % nh),
                   b_block=(D_MODEL, tk), b_map=lambda i, j, k: (0, k),
                   o_block=(tm, D_MODEL), o_map=lambda i, j, k: (i, 0),
                   out_shape=jax.ShapeDtypeStruct((s, D_MODEL), F32), name="d_h2")


def _mm_dwup_tn(h, dup, tn=1408, tk=1024):
    s = h.shape[0]
    tk = _tile(s, tk, 8)
    nh = D_FF // tn
    return _matmul(h, dup, mode="tn", grid=(1, 2 * nh, s // tk),
                   a_block=(tk, D_MODEL), a_map=lambda i, j, k: (k, 0),
                   b_block=(None, tk, tn), b_map=lambda i, j, k: (j // nh, k, j % nh),
                   o_block=(D_MODEL, tn), o_map=lambda i, j, k: (0, j),
                   out_shape=jax.ShapeDtypeStruct((D_MODEL, 2 * D_FF), F32), name="d_w_up")


def _rms_fwd(x, g, tr=1024):
    s, d = x.shape
    tr = _tile(s, tr, 8)

    def body(x_ref, g_ref, o_ref):
        xv = x_ref[...]
        r = lax.rsqrt(jnp.mean(xv * xv, axis=-1, keepdims=True) + EPS)
        o_ref[...] = (xv * r * g_ref[...]).astype(o_ref.dtype)

    return pl.pallas_call(
        body, name="rms_fwd", grid=(s // tr,),
        in_specs=[pl.BlockSpec((tr, d), lambda i: (i, 0)), pl.BlockSpec((1, d), lambda i: (0, 0))],
        out_specs=pl.BlockSpec((tr, d), lambda i: (i, 0)),
        out_shape=jax.ShapeDtypeStruct((s, d), BF16),
        compiler_params=_cparams(("parallel",)),
    )(x, g)


def _group_rms_fwd(o_fox, o_sb, g_fox, g_sb, tr=512):
    nh, s, dh = o_fox.shape
    tr = _tile(s, tr, 8)

    def body(a_ref, b_ref, ga_ref, gb_ref, o_ref):
        for src, g_ref, lo in ((a_ref, ga_ref, 0), (b_ref, gb_ref, nh * dh)):
            heads = [src[hh] for hh in range(nh)]
            ss = heads[0] * heads[0]
            for xv in heads[1:]:
                ss = ss + xv * xv
            r = lax.rsqrt(jnp.sum(ss, axis=-1, keepdims=True) * (1.0 / (nh * dh)) + EPS)
            for hh, xv in enumerate(heads):
                o_ref[:, lo + hh * dh:lo + (hh + 1) * dh] = (xv * r * g_ref[hh]).astype(o_ref.dtype)

    heads_blk = pl.BlockSpec((nh, tr, dh), lambda i: (0, i, 0))
    gain = pl.BlockSpec((nh, 1, dh), lambda i: (0, 0, 0))
    return pl.pallas_call(
        body, name="group_rms_fwd", grid=(s // tr,),
        in_specs=[heads_blk, heads_blk, gain, gain],
        out_specs=pl.BlockSpec((tr, 2 * nh * dh), lambda i: (i, 0)),
        out_shape=jax.ShapeDtypeStruct((s, 2 * nh * dh), BF16),
        compiler_params=_cparams(("parallel",)),
    )(o_fox, o_sb, g_fox, g_sb)


def _group_rms_bwd(x, dy, g, *, dy_col, name, tr=512):
    nh, s, dh = x.shape
    tr = _tile(s, tr, 8)
    d = nh * dh

    def body(x_ref, dy_ref, g_ref, dx_ref, dg_ref):
        @pl.when(pl.program_id(0) == 0)
        def _():
            dg_ref[...] = jnp.zeros_like(dg_ref)

        dyv = dy_ref[...]
        xs_ = [x_ref[hh] for hh in range(nh)]
        dys = [dyv[:, hh * dh:(hh + 1) * dh] for hh in range(nh)]
        ss = xs_[0] * xs_[0]
        for xv in xs_[1:]:
            ss = ss + xv * xv
        r = lax.rsqrt(jnp.sum(ss, axis=-1, keepdims=True) * (1.0 / d) + EPS)
        xh = [xv * r for xv in xs_]
        gy = [dys[hh] * g_ref[hh] for hh in range(nh)]
        dot = xh[0] * gy[0]
        for hh in range(1, nh):
            dot = dot + xh[hh] * gy[hh]
        mean_dot = jnp.sum(dot, axis=-1, keepdims=True) * (1.0 / d)
        for hh in range(nh):
            dx_ref[hh] = r * (gy[hh] - xh[hh] * mean_dot)
            dg_ref[hh] += jnp.sum(dys[hh] * xh[hh], axis=0, keepdims=True)

    heads_blk = pl.BlockSpec((nh, tr, dh), lambda i: (0, i, 0))
    gain = pl.BlockSpec((nh, 1, dh), lambda i: (0, 0, 0))
    return pl.pallas_call(
        body, name=name, grid=(s // tr,),
        in_specs=[heads_blk, pl.BlockSpec((tr, d), lambda i: (i, dy_col)), gain],
        out_specs=[heads_blk, gain],
        out_shape=[jax.ShapeDtypeStruct((nh, s, dh), F32), jax.ShapeDtypeStruct((nh, 1, dh), F32)],
        compiler_params=_cparams(("arbitrary",)),
    )(x, dy, g)


def _merge_dproj(parts_fox, d_gate, parts_sb, tr=512):
    nh, s, dh = parts_fox[0].shape
    tr = _tile(s, tr, 16)

    def body(*refs):
        o_ref = refs[-1]
        gate_ref = refs[3]
        col = 0
        for ref in refs[:3]:
            for hh in range(nh):
                o_ref[:, col:col + dh] = ref[hh].astype(o_ref.dtype)
                col += dh
        o_ref[:, col:col + GATE_PAD] = jnp.zeros((tr, GATE_PAD), o_ref.dtype)
        o_ref[:, col:col + N_GROUP_HEADS] = gate_ref[...].astype(o_ref.dtype)
        col += GATE_PAD
        for ref in refs[4:7]:
            for hh in range(nh):
                o_ref[:, col:col + dh] = ref[hh].astype(o_ref.dtype)
                col += dh

    heads_blk = pl.BlockSpec((nh, tr, dh), lambda i: (0, i, 0))
    return pl.pallas_call(
        body, name="merge_d_proj", grid=(s // tr,),
        in_specs=[heads_blk] * 3 + [pl.BlockSpec((tr, N_GROUP_HEADS), lambda i: (i, 0))] + [heads_blk] * 3,
        out_specs=pl.BlockSpec((tr, IN_COLS_PAD), lambda i: (i, 0)),
        out_shape=jax.ShapeDtypeStruct((s, IN_COLS_PAD), BF16),
        compiler_params=_cparams(("parallel",)),
    )(*parts_fox, d_gate, *parts_sb)


def _rms_bwd(x, dy, g, resid, *, dy_col, name, want_bf16, tr=1024):
    s, d = x.shape
    tr = _tile(s, tr, 8)
    has_resid = resid is not None

    def body(*refs):
        refs = list(refs)
        x_ref, dy_ref, g_ref = refs[:3]
        r_ref = refs[3] if has_resid else None
        outs = refs[4:] if has_resid else refs[3:]
        dx_ref = outs[0]
        dxb_ref = outs[1] if want_bf16 else None
        dg_ref = outs[-1]

        @pl.when(pl.program_id(0) == 0)
        def _():
            dg_ref[...] = jnp.zeros_like(dg_ref)

        xv = x_ref[...]
        dyv = dy_ref[...]
        r = lax.rsqrt(jnp.mean(xv * xv, axis=-1, keepdims=True) + EPS)
        xh = xv * r
        gy = dyv * g_ref[...]
        dx = r * (gy - xh * jnp.mean(xh * gy, axis=-1, keepdims=True))
        if r_ref is not None:
            dx = r_ref[...] + dx
        dx_ref[...] = dx
        if dxb_ref is not None:
            dxb_ref[...] = dx.astype(BF16)
        dg_ref[...] += jnp.sum(dyv * xh, axis=0, keepdims=True)

    row = pl.BlockSpec((tr, d), lambda i: (i, 0))
    in_specs = [row, pl.BlockSpec((tr, d), lambda i: (i, dy_col)), pl.BlockSpec((1, d), lambda i: (0, 0))]
    args = [x, dy, g]
    if has_resid:
        in_specs.append(row)
        args.append(resid)
    out_specs = [row]
    out_shape = [jax.ShapeDtypeStruct((s, d), F32)]
    if want_bf16:
        out_specs.append(row)
        out_shape.append(jax.ShapeDtypeStruct((s, d), BF16))
    out_specs.append(pl.BlockSpec((1, d), lambda i: (0, 0)))
    out_shape.append(jax.ShapeDtypeStruct((1, d), F32))
    return pl.pallas_call(
        body, name=name, grid=(s // tr,), in_specs=in_specs, out_specs=out_specs, out_shape=out_shape,
        compiler_params=_cparams(("arbitrary",)),
    )(*args)


def _loss_head(x2, target, g, tr=1024):
    s, d = x2.shape
    tr = _tile(s, tr, 8)

    def body(x_ref, t_ref, g_ref, dx_ref, dxb_ref, dg_ref, loss_ref):
        @pl.when(pl.program_id(0) == 0)
        def _():
            dg_ref[...] = jnp.zeros_like(dg_ref)
            loss_ref[...] = jnp.zeros_like(loss_ref)

        xv = x_ref[...]
        gv = g_ref[...]
        r = lax.rsqrt(jnp.mean(xv * xv, axis=-1, keepdims=True) + EPS)
        xh = xv * r
        err = xh * gv - t_ref[...]
        loss_ref[...] += jnp.sum(jnp.mean(err * err, axis=-1, keepdims=True), axis=0, keepdims=True) * 0.5
        dyv = err * (1.0 / d)
        gy = dyv * gv
        dx = r * (gy - xh * jnp.mean(xh * gy, axis=-1, keepdims=True))
        dx_ref[...] = dx
        dxb_ref[...] = dx.astype(BF16)
        dg_ref[...] += jnp.sum(dyv * xh, axis=0, keepdims=True)

    row = pl.BlockSpec((tr, d), lambda i: (i, 0))
    return pl.pallas_call(
        body, name="loss_head", grid=(s // tr,),
        in_specs=[row, row, pl.BlockSpec((1, d), lambda i: (0, 0))],
        out_specs=[row, row, pl.BlockSpec((1, d), lambda i: (0, 0)), pl.BlockSpec((1, LANES), lambda i: (0, 0))],
        out_shape=[jax.ShapeDtypeStruct((s, d), F32), jax.ShapeDtypeStruct((s, d), BF16),
                   jax.ShapeDtypeStruct((1, d), F32), jax.ShapeDtypeStruct((1, LANES), F32)],
        compiler_params=_cparams(("arbitrary",)),
    )(x2, target, g)


def _conv_taps(cur, prev8, w, b, first):
    prev8 = jnp.where(first, 0.0, prev8)
    ext = jnp.concatenate([prev8, cur], axis=0)
    x1 = pltpu.roll(ext, 1, 0)[8:]
    x2 = pltpu.roll(ext, 2, 0)[8:]
    u = b + w[0:1] * x2
    u = u + w[1:2] * x1
    u = u + w[2:3] * cur
    return u, x1, x2


def _conv_gate_fwd(up, conv_w, conv_b, tm=2048, tn=256):
    s = up.shape[1]
    tm = _tile(s, tm, 8)
    nrb = s // tm
    rb8 = tm // 8

    def body(g_ref, v_ref, gp_ref, vp_ref, wg_ref, wv_ref, bg_ref, bv_ref, o_ref):
        first = pl.program_id(1) == 0
        ug, _, _ = _conv_taps(g_ref[...], gp_ref[...], wg_ref[...], bg_ref[...], first)
        uv, _, _ = _conv_taps(v_ref[...], vp_ref[...], wv_ref[...], bv_ref[...], first)
        sg = 1.0 / (1.0 + jnp.exp(-ug))
        o_ref[...] = (ug * sg * uv).astype(o_ref.dtype)

    def cur(h):
        return pl.BlockSpec((None, tm, tn), lambda j, i: (h, i, j))

    def prev(h):
        return pl.BlockSpec((None, 8, tn), lambda j, i: (h, jnp.maximum(i * rb8 - 1, 0), j))

    def par(h, r):
        return pl.BlockSpec((None, r, tn), lambda j, i: (h, 0, j))

    return pl.pallas_call(
        body, name="conv_gate_fwd", grid=(D_FF // tn, nrb),
        in_specs=[cur(0), cur(1), prev(0), prev(1), par(0, 3), par(1, 3), par(0, 1), par(1, 1)],
        out_specs=pl.BlockSpec((tm, tn), lambda j, i: (i, j)),
        out_shape=jax.ShapeDtypeStruct((s, D_FF), BF16),
        compiler_params=_cparams(("parallel", "parallel")),
    )(up, up, up, up, conv_w, conv_w, conv_b, conv_b)


def _conv_gate_bwd(up, dact, conv_w, conv_b, tm=1024, tn=256):
    s = up.shape[1]
    tm = _tile(s, tm, 8)
    nrb = s // tm
    rb8 = tm // 8

    def body(g_ref, v_ref, gp_ref, vp_ref, da_ref, wg_ref, wv_ref, bg_ref, bv_ref,
             dup_ref, dcw_ref, dcb_ref, carry_ref):
        i = pl.program_id(1)
        first = i == nrb - 1

        @pl.when(i == 0)
        def _():
            carry_ref[...] = jnp.zeros_like(carry_ref)
            dcw_ref[...] = jnp.zeros_like(dcw_ref)
            dcb_ref[...] = jnp.zeros_like(dcb_ref)

        curs = (g_ref[...], v_ref[...])
        ws = (wg_ref[...], wv_ref[...])
        ug, g1, g2 = _conv_taps(curs[0], gp_ref[...], ws[0], bg_ref[...], first)
        uv, v1, v2 = _conv_taps(curs[1], vp_ref[...], ws[1], bv_ref[...], first)
        sg = 1.0 / (1.0 + jnp.exp(-ug))
        da = da_ref[...].astype(F32)
        d_v = da * (ug * sg)
        d_g = da * uv * (sg * (1.0 + ug * (1.0 - sg)))
        for h, (du, x0, x1, x2) in enumerate(((d_g, curs[0], g1, g2), (d_v, curs[1], v1, v2))):
            dcb_ref[h] += jnp.sum(du, axis=0, keepdims=True)
            dcw_ref[h, 0:1, :] += jnp.sum(du * x2, axis=0, keepdims=True)
            dcw_ref[h, 1:2, :] += jnp.sum(du * x1, axis=0, keepdims=True)
            dcw_ref[h, 2:3, :] += jnp.sum(du * x0, axis=0, keepdims=True)
            ext = jnp.concatenate([du, carry_ref[h]], axis=0)
            n1 = pltpu.roll(ext, tm + 7, 0)[:tm]
            n2 = pltpu.roll(ext, tm + 6, 0)[:tm]
            w = ws[h]
            dup_ref[h] = (w[2:3] * du + w[1:2] * n1 + w[0:1] * n2).astype(dup_ref.dtype)
            carry_ref[h] = du[:8]

    def cur(h):
        return pl.BlockSpec((None, tm, tn), lambda j, i: (h, nrb - 1 - i, j))

    def prev(h):
        return pl.BlockSpec((None, 8, tn), lambda j, i: (h, jnp.maximum((nrb - 1 - i) * rb8 - 1, 0), j))

    def par(h, r):
        return pl.BlockSpec((None, r, tn), lambda j, i: (h, 0, j))

    return pl.pallas_call(
        body, name="conv_gate_bwd", grid=(D_FF // tn, nrb),
        in_specs=[cur(0), cur(1), prev(0), prev(1),
                  pl.BlockSpec((tm, tn), lambda j, i: (nrb - 1 - i, j)),
                  par(0, 3), par(1, 3), par(0, 1), par(1, 1)],
        out_specs=[pl.BlockSpec((2, tm, tn), lambda j, i: (0, nrb - 1 - i, j)),
                   pl.BlockSpec((2, 3, tn), lambda j, i: (0, 0, j)),
                   pl.BlockSpec((2, 1, tn), lambda j, i: (0, 0, j))],
        out_shape=[jax.ShapeDtypeStruct((2, s, D_FF), BF16),
                   jax.ShapeDtypeStruct((2, 3, D_FF), F32),
                   jax.ShapeDtypeStruct((2, 1, D_FF), F32)],
        scratch_shapes=[pltpu.VMEM((2, 8, tn), F32)],
        compiler_params=_cparams(("parallel", "arbitrary")),
    )(up, up, up, up, dact, conv_w, conv_w, conv_b, conv_b)


def _split_dot(x, tri, terms):
    piece = x.astype(BF16)
    out = jnp.dot(piece, tri, preferred_element_type=F32)
    rest = x
    for _ in range(terms - 1):
        rest = rest - piece.astype(F32)
        piece = rest.astype(BF16)
        out = out + jnp.dot(piece, tri, preferred_element_type=F32)
    return out


def _split_dot_rhs(tri, x, terms):
    piece = x.astype(BF16)
    out = jnp.dot(tri, piece, preferred_element_type=F32)
    rest = x
    for _ in range(terms - 1):
        rest = rest - piece.astype(F32)
        piece = rest.astype(BF16)
        out = out + jnp.dot(tri, piece, preferred_element_type=F32)
    return out


def _tri(n, kind):
    r = lax.broadcasted_iota(jnp.int32, (n, n), 0)
    c = lax.broadcasted_iota(jnp.int32, (n, n), 1)
    cond = {"le": r <= c, "ge": r >= c, "lt": r < c, "gt": r > c}[kind]
    return jnp.where(cond, 1.0, 0.0).astype(BF16)


def _log_sigmoid(x):
    return jnp.minimum(x, 0.0) - jnp.log(1.0 + jnp.exp(-jnp.abs(x)))


def _forget_fwd(f_logit, bias):
    h, r, _ = f_logit.shape

    def body(x_ref, b_ref, o_ref):
        lf = _log_sigmoid(x_ref[...] + b_ref[...])
        within = _split_dot(lf, _tri(LANES, "le"), 3)
        row_tot = jnp.broadcast_to(within[:, LANES - 1:LANES], (r, LANES))
        before = _split_dot_rhs(_tri(r, "gt"), row_tot, 3)
        o_ref[...] = within + before

    blk = pl.BlockSpec((None, r, LANES), lambda i: (i, 0, 0))
    return pl.pallas_call(
        body, name="forget_cumsum_fwd", grid=(h,),
        in_specs=[blk, pl.BlockSpec((None, 1, LANES), lambda i: (i, 0, 0))],
        out_specs=blk, out_shape=jax.ShapeDtypeStruct((h, r, LANES), F32),
        compiler_params=_cparams(("parallel",)),
    )(f_logit, bias)


def _forget_bwd(f_logit, bias, ksum, qsum):
    h, r, _ = f_logit.shape

    def body(x_ref, b_ref, k_ref, q_ref, dx_ref, db_ref):
        d_f = q_ref[...] - k_ref[...]
        within = _split_dot(d_f, _tri(LANES, "ge"), 3)
        row_tot = jnp.broadcast_to(within[:, 0:1], (r, LANES))
        after = _split_dot_rhs(_tri(r, "lt"), row_tot, 3)
        xv = x_ref[...] + b_ref[...]
        dx = (within + after) * jnp.exp(_log_sigmoid(-xv))
        dx_ref[...] = dx
        db_ref[...] = jnp.broadcast_to(jnp.sum(dx), (1, LANES))

    blk = pl.BlockSpec((None, r, LANES), lambda i: (i, 0, 0))
    one = pl.BlockSpec((None, 1, LANES), lambda i: (i, 0, 0))
    return pl.pallas_call(
        body, name="forget_cumsum_bwd", grid=(h,),
        in_specs=[blk, one, blk, blk], out_specs=[blk, one],
        out_shape=[jax.ShapeDtypeStruct((h, r, LANES), F32), jax.ShapeDtypeStruct((h, 1, LANES), F32)],
        compiler_params=_cparams(("parallel",)),
    )(f_logit, bias, ksum, qsum)


def _head_specs(s, tq):
    qblk = pl.BlockSpec((None, tq, HEAD_DIM), lambda h, i: (h, i, 0))
    full = pl.BlockSpec((None, s, HEAD_DIM), lambda h, i: (h, 0, 0))
    col = pl.BlockSpec((None, tq, 1), lambda h, i: (h, i, 0))
    return qblk, full, col


def _qkv_specs(s, tq, offs):
    q_off, k_off, v_off = offs
    return (pl.BlockSpec((None, tq, HEAD_DIM), lambda h, i: (h + q_off, i, 0)),
            pl.BlockSpec((None, s, HEAD_DIM), lambda h, i: (h + k_off, 0, 0)),
            pl.BlockSpec((None, s, HEAD_DIM), lambda h, i: (h + v_off, 0, 0)))


def _scaled(q_ref):
    return (q_ref[...].astype(F32) * Q_SCALE).astype(BF16)


_NT = (((1,), (1,)), ((), ()))
_TN = (((0,), (0,)), ((), ()))


def _cols_minus_rows(rows, cols):
    return lax.broadcasted_iota(jnp.int32, (rows, cols), 1) - lax.broadcasted_iota(jnp.int32, (rows, cols), 0)


def _fox_fwd(qkv, offs, v_ones, f_row, tq, tk):
    h, s = N_GROUP_HEADS, qkv.shape[1]
    nk = s // tk
    assert tq == tk

    def body(q_ref, k_ref, v_ref, fr_ref, o_ref, lse_ref, m_ref, acc_ref, z0, z1):
        i = pl.program_id(1)
        qs = _scaled(q_ref)
        m_ref[...] = jnp.full_like(m_ref, NEG_BIG)
        acc_ref[...] = jnp.zeros_like(acc_ref)

        ahead = _cols_minus_rows(tq, tk)

        def block_of(j):
            return jnp.minimum(j, nk - 1)

        def keys_of(j):
            return pl.ds(pl.multiple_of(block_of(j) * tk, tk), tk)

        def logits(j):
            return lax.dot_general(qs, k_ref[keys_of(j), :], _NT, preferred_element_type=F32)

        def soft(j, raw, masked):
            sc = raw - fr_ref[block_of(j)]
            if masked:
                sc = jnp.where(ahead <= (i - j) * tk, sc, NEG_BIG)
            m_old = m_ref[...]
            m_new = jnp.maximum(m_old, jnp.max(sc, axis=-1, keepdims=True))
            p = jnp.exp(sc - m_new)
            acc_ref[...] = jnp.exp(m_old - m_new) * acc_ref[...] + jnp.dot(
                p.astype(BF16), v_ref[keys_of(j), :], preferred_element_type=F32)
            m_ref[...] = m_new

        z0[...] = logits(0)

        def trip(p, masked):
            j = 2 * p
            z1[...] = logits(j + 1)
            soft(j, z0[...], masked)
            z0[...] = logits(j + 2)
            soft(j + 1, z1[...], masked)

        def step(p, carry):
            trip(p, False)
            return carry

        lax.fori_loop(0, i // 2, step, 0)
        trip(i // 2, True)
        l = acc_ref[:, HEAD_DIM:HEAD_DIM + 1]
        o_ref[...] = acc_ref[:, :HEAD_DIM] / l
        lse_ref[...] = m_ref[...] + jnp.log(l)

    qblk, full, colspec = _head_specs(s, tq)
    q_in, k_in, _ = _qkv_specs(s, tq, offs)
    return pl.pallas_call(
        body, name="fox_fwd", grid=(h, s // tq),
        in_specs=[q_in, k_in, pl.BlockSpec((None, s, 2 * HEAD_DIM), lambda hh, i: (hh, 0, 0)),
                  pl.BlockSpec((None, nk, 1, tk), lambda hh, i: (hh, 0, 0, 0))],
        out_specs=[qblk, colspec],
        out_shape=[jax.ShapeDtypeStruct((h, s, HEAD_DIM), F32), jax.ShapeDtypeStruct((h, s, 1), F32)],
        scratch_shapes=[pltpu.VMEM((tq, 1), F32), pltpu.VMEM((tq, 2 * HEAD_DIM), F32),
                        pltpu.VMEM((tq, tk), F32), pltpu.VMEM((tq, tk), F32)],
        compiler_params=_cparams(("parallel", "parallel")),
    )(qkv, qkv, v_ones, f_row)


def _fox_bwd(qkv, offs, k_ones, v_ones, f_row, o, lse, d_o, tq, tk):
    h, s = N_GROUP_HEADS, qkv.shape[1]
    nk = s // tk
    assert tq == tk

    def body(q_ref, k_ref, v_ref, fr_ref, o_ref, lse_ref, do_ref,
             dq_ref, dk_ref, dv_ref, ks_ref, qs_ref, dq_acc, qsum_acc, z0, z1, p0, p1, doa_ref, qa_ref):
        i = pl.program_id(1)

        @pl.when(i == 0)
        def _():
            dk_ref[...] = jnp.zeros_like(dk_ref)
            dv_ref[...] = jnp.zeros_like(dv_ref)
            ks_ref[...] = jnp.zeros_like(ks_ref)

        qs = _scaled(q_ref)
        lse_v = lse_ref[...]
        dob = do_ref[...].astype(BF16)
        delta = jnp.sum(dob.astype(F32) * o_ref[...], axis=-1, keepdims=True)
        dq_acc[...] = jnp.zeros_like(dq_acc)
        qsum_acc[...] = jnp.zeros_like(qsum_acc)
        d_hi = delta.astype(BF16).astype(F32)
        d_mid = (delta - d_hi).astype(BF16).astype(F32)
        d_lo = (delta - d_hi - d_mid).astype(BF16).astype(F32)
        spare = lax.broadcasted_iota(jnp.int32, (tq, HEAD_DIM), 1)
        doa_ref[:, :HEAD_DIM] = dob
        doa_ref[:, HEAD_DIM:] = jnp.where(spare == 0, -d_hi, jnp.where(
            spare == 1, -d_mid, jnp.where(spare == 2, -d_lo, 0.0))).astype(BF16)
        l_hi = lse_v.astype(BF16).astype(F32)
        l_mid = (lse_v - l_hi).astype(BF16).astype(F32)
        l_lo = (lse_v - l_hi - l_mid).astype(BF16).astype(F32)
        qa_ref[:, :HEAD_DIM] = qs
        qa_ref[:, HEAD_DIM:] = jnp.where(spare == 0, -l_hi, jnp.where(
            spare == 1, -l_mid, jnp.where(spare == 2, -l_lo, 0.0))).astype(BF16)

        ahead = _cols_minus_rows(tq, tk)

        def block_of(j):
            return jnp.minimum(j, nk - 1)

        def keys_of(j):
            return pl.ds(pl.multiple_of(block_of(j) * tk, tk), tk)

        def products(j):
            at = keys_of(j)
            return (lax.dot_general(qa_ref[...], k_ref[at, :], _NT, preferred_element_type=F32),
                    lax.dot_general(doa_ref[...], v_ref[at, :], _NT, preferred_element_type=F32))

        def grads(j, raw, dp, masked):
            at = keys_of(j)
            sc = raw - fr_ref[block_of(j)]
            if masked:
                sc = jnp.where(ahead <= (i - j) * tk, sc, NEG_BIG)
            p = jnp.exp(sc)
            ds = p * dp
            dsb = ds.astype(BF16)
            dq_acc[...] += jnp.dot(dsb, k_ref[at, :], preferred_element_type=F32)[:, :HEAD_DIM]
            dk_ref[at, :] += lax.dot_general(dsb, qs, _TN, preferred_element_type=F32)
            dv_ref[at, :] += lax.dot_general(p.astype(BF16), dob, _TN, preferred_element_type=F32)
            ks_ref[block_of(j)] += jnp.sum(ds.reshape(tq // 8, 8, tk), axis=0)
            qsum_acc[...] += jnp.sum(ds, axis=-1, keepdims=True)

        z0[...], p0[...] = products(0)

        def trip(pp, masked):
            j = 2 * pp
            z1[...], p1[...] = products(j + 1)
            grads(j, z0[...], p0[...], masked)
            z0[...], p0[...] = products(j + 2)
            grads(j + 1, z1[...], p1[...], masked)

        def step(pp, carry):
            trip(pp, False)
            return carry

        lax.fori_loop(0, i // 2, step, 0)
        trip(i // 2, True)
        dq_ref[...] = dq_acc[...] * Q_SCALE
        qs_ref[...] = qsum_acc[...]

    qblk, full, colspec = _head_specs(s, tq)
    frow = pl.BlockSpec((None, nk, 1, tk), lambda hh, i: (hh, 0, 0, 0))
    big = pltpu.VMEM((tq, tk), F32)
    return pl.pallas_call(
        body, name="fox_bwd", grid=(h, s // tq),
        in_specs=[_qkv_specs(s, tq, offs)[0], pl.BlockSpec((None, s, 2 * HEAD_DIM), lambda hh, i: (hh, 0, 0)),
                  pl.BlockSpec((None, s, 2 * HEAD_DIM), lambda hh, i: (hh, 0, 0)),
                  frow, qblk, colspec, qblk],
        out_specs=[qblk, full, full, pl.BlockSpec((None, nk, 8, tk), lambda hh, i: (hh, 0, 0, 0)), colspec],
        out_shape=[jax.ShapeDtypeStruct((h, s, HEAD_DIM), F32)] * 3
        + [jax.ShapeDtypeStruct((h, nk, 8, tk), F32), jax.ShapeDtypeStruct((h, s, 1), F32)],
        scratch_shapes=[pltpu.VMEM((tq, HEAD_DIM), F32), pltpu.VMEM((tq, 1), F32), big, big, big, big,
                        pltpu.VMEM((tq, 2 * HEAD_DIM), BF16), pltpu.VMEM((tq, 2 * HEAD_DIM), BF16)],
        compiler_params=_cparams(("parallel", "arbitrary")),
    )(qkv, k_ones, v_ones, f_row, o, lse, d_o)


SB_TERMS = 2
G_TERMS = 1
LOG2E = 1.4426950408889634
LN2 = 0.6931471805599453


def _softplus2(z2):
    return jnp.maximum(z2, 0.0) + jnp.log2(1.0 + jnp.exp2(-jnp.abs(z2)))


def _sb_fwd(qkv, offs, tq, tk):
    h, s = N_GROUP_HEADS, qkv.shape[1]

    assert tq % (2 * tk) == 0

    def body(q_ref, k_ref, v_ref, o_ref, w_hbm, acc_ref, run_ref, z0, z1, d0, d1, t0, t1, w_stage, wsem):
        z_refs, d_refs, t_refs = (z0, z1), (d0, d1), (t0, t1)
        hh = pl.program_id(0)
        i = pl.program_id(1)
        qs = _scaled(q_ref)
        tri = _tri(tk, "ge")
        acc_ref[...] = jnp.zeros_like(acc_ref)
        run_ref[...] = jnp.zeros_like(run_ref)
        nb = (i + 1) * (tq // tk)
        ahead = _cols_minus_rows(tq, tk)

        def keys_of(b):
            j = nb - 1 - jnp.minimum(b, nb - 1)
            return pl.ds(pl.multiple_of(j * tk, tk), tk)

        def visible(b):
            return ahead < i * tq - (nb - 1 - b) * tk

        def logits(b, slot):
            z_refs[slot][...] = lax.dot_general(qs, k_ref[keys_of(b), :], _NT,
                                                preferred_element_type=F32) * LOG2E

        def sums(b, slot, masked):
            z2 = z_refs[slot][...]
            sp = _softplus2(z2)
            if masked:
                sp = jnp.where(visible(b), sp, 0.0)
            inc = _split_dot(sp, tri, SB_TERMS)
            d_refs[slot][...] = z2 - inc
            t_refs[slot][...] = inc[:, 0:1]

        def put(p, slot):
            st = (p % 2) * 2 + slot
            return pltpu.make_async_copy(w_stage.at[st], w_hbm.at[hh, i, nb - 1 - (2 * p + slot)], wsem.at[st])

        def weigh(p, slot, masked):
            b = 2 * p + slot
            w = jnp.exp2(d_refs[slot][...] - run_ref[...])
            if masked:
                w = jnp.where(visible(b), w, 0.0)
            wb = w.astype(BF16)
            w_stage[(p % 2) * 2 + slot] = wb
            acc_ref[...] += jnp.dot(wb, v_ref[keys_of(b), :], preferred_element_type=F32)
            run_ref[...] += t_refs[slot][...]

        def trip(p, masked):
            @pl.when(p >= 2)
            def _():
                put(p - 2, 0).wait()
                put(p - 2, 1).wait()

            b = 2 * p
            logits(b + 2, 0)
            sums(b + 1, 1, masked)
            weigh(p, 0, masked)
            logits(b + 3, 1)
            sums(b + 2, 0, masked)
            weigh(p, 1, masked)
            put(p, 0).start()
            put(p, 1).start()

        logits(0, 0)
        logits(1, 1)
        sums(0, 0, True)

        def guarded(p, carry):
            trip(p, True)
            return carry

        def plain(p, carry):
            trip(p, False)
            return carry

        lax.fori_loop(0, tq // tk // 2, guarded, 0)
        lax.fori_loop(tq // tk // 2, nb // 2, plain, 0)
        trips = nb // 2

        @pl.when(trips >= 2)
        def _():
            put(trips - 2, 0).wait()
            put(trips - 2, 1).wait()

        put(trips - 1, 0).wait()
        put(trips - 1, 1).wait()
        o_ref[...] = acc_ref[...]

    qblk, full, colspec = _head_specs(s, tq)
    return pl.pallas_call(
        body, name="sb_fwd", grid=(h, s // tq),
        in_specs=[*_qkv_specs(s, tq, offs)], out_specs=[qblk, pl.BlockSpec(memory_space=pl.ANY)],
        out_shape=[jax.ShapeDtypeStruct((h, s, HEAD_DIM), F32),
                   jax.ShapeDtypeStruct((h, s // tq, s // tk, tq, tk), BF16)],
        scratch_shapes=[pltpu.VMEM((tq, HEAD_DIM), F32), pltpu.VMEM((tq, 1), F32),
                        pltpu.VMEM((tq, tk), F32), pltpu.VMEM((tq, tk), F32),
                        pltpu.VMEM((tq, tk), F32), pltpu.VMEM((tq, tk), F32),
                        pltpu.VMEM((tq, 1), F32), pltpu.VMEM((tq, 1), F32),
                        pltpu.VMEM((4, tq, tk), BF16), pltpu.SemaphoreType.DMA((4,))],
        compiler_params=_cparams(("parallel", "parallel")),
    )(qkv, qkv, qkv)


def _sb_bwd(qkv, offs, w_saved, d_o, tq, tk):
    h, s = N_GROUP_HEADS, qkv.shape[1]

    assert tq % (2 * tk) == 0

    def body(q_ref, k_ref, v_ref, do_ref, w_hbm, dq_ref, dk_ref, dv_ref, dq_acc, grun_ref,
             z0, z1, p0, p1, w_bufs, wsem):
        z_refs, p_refs = (z0, z1), (p0, p1)
        hh = pl.program_id(0)
        i = pl.program_id(1)

        @pl.when(i == 0)
        def _():
            dk_ref[...] = jnp.zeros_like(dk_ref)
            dv_ref[...] = jnp.zeros_like(dv_ref)

        qs = _scaled(q_ref)
        dob = do_ref[...].astype(BF16)
        tri = _tri(tk, "le")
        dq_acc[...] = jnp.zeros_like(dq_acc)
        grun_ref[...] = jnp.zeros_like(grun_ref)
        nb = (i + 1) * (tq // tk)
        ahead = _cols_minus_rows(tq, tk)

        def block_of(b):
            return jnp.minimum(b, nb - 1)

        def keys_of(b):
            return pl.ds(pl.multiple_of(block_of(b) * tk, tk), tk)

        def visible(b):
            return ahead < i * tq - b * tk

        def fetch(p, slot):
            st = (p % 2) * 2 + slot
            return pltpu.make_async_copy(w_hbm.at[hh, i, block_of(2 * p + slot)], w_bufs.at[st], wsem.at[st])

        def products(b, slot):
            at = keys_of(b)
            z_refs[slot][...] = lax.dot_general(qs, k_ref[at, :], _NT, preferred_element_type=F32) * LOG2E
            p_refs[slot][...] = lax.dot_general(dob, v_ref[at, :], _NT, preferred_element_type=F32)

        def grads(p, slot, masked):
            b = 2 * p + slot
            at = keys_of(b)
            wb = w_bufs[(p % 2) * 2 + slot]
            g = wb.astype(F32) * p_refs[slot][...]
            ginc = _split_dot(g, tri, G_TERMS)
            beta = 1.0 / (1.0 + jnp.exp2(-z_refs[slot][...]))
            dz = g - beta * (grun_ref[...] + ginc)
            if masked:
                dz = jnp.where(visible(b), dz, 0.0)
            dzb = dz.astype(BF16)
            dq_acc[...] += jnp.dot(dzb, k_ref[at, :], preferred_element_type=F32)
            dk_ref[at, :] += lax.dot_general(dzb, qs, _TN, preferred_element_type=F32)
            dv_ref[at, :] += lax.dot_general(wb, dob, _TN, preferred_element_type=F32)
            grun_ref[...] += ginc[:, tk - 1:tk]

        def trip(p, masked):
            for slot in (0, 1):
                fetch(p + 1, slot).start()
            for slot in (0, 1):
                fetch(p, slot).wait()
            for slot in (0, 1):
                products(2 * p + slot + 1, 1 - slot)
                grads(p, slot, masked)

        for slot in (0, 1):
            fetch(0, slot).start()
        products(0, 0)
        n_plain = (nb - tq // tk) // 2

        def plain(p, carry):
            trip(p, False)
            return carry

        def guarded(p, carry):
            trip(p, True)
            return carry

        lax.fori_loop(0, n_plain, plain, 0)
        lax.fori_loop(n_plain, nb // 2, guarded, 0)
        for slot in (0, 1):
            fetch(nb // 2, slot).wait()
        dq_ref[...] = dq_acc[...] * Q_SCALE

    qblk, full, colspec = _head_specs(s, tq)
    big = pltpu.VMEM((tq, tk), F32)
    return pl.pallas_call(
        body, name="sb_bwd", grid=(h, s // tq),
        in_specs=[*_qkv_specs(s, tq, offs), qblk, pl.BlockSpec(memory_space=pl.ANY)], out_specs=[qblk, full, full],
        out_shape=[jax.ShapeDtypeStruct((h, s, HEAD_DIM), F32)] * 3,
        scratch_shapes=[pltpu.VMEM((tq, HEAD_DIM), F32), pltpu.VMEM((tq, 1), F32)]
        + [big] * 4 + [pltpu.VMEM((4, tq, tk), BF16), pltpu.SemaphoreType.DMA((4,))],
        compiler_params=_cparams(("parallel", "arbitrary")),
    )(qkv, qkv, qkv, d_o, w_saved)


def _sum_adamw(parts, w, m, v, name, tr=256):
    _, rows, lanes = parts.shape
    tr = _tile(rows, tr, 16)
    c_m = 1.0 - ADAM_B1 ** ADAM_STEP
    c_v = 1.0 - ADAM_B2 ** ADAM_STEP

    def body(p_ref, w_ref, m_ref, v_ref, g_ref, d_ref, nm_ref, nv_ref):
        g = p_ref[0].astype(F32)
        for j in range(1, N_DEV):
            g = g + p_ref[j].astype(F32)
        nm = ADAM_B1 * m_ref[...] + (1.0 - ADAM_B1) * g
        nv = ADAM_B2 * v_ref[...] + (1.0 - ADAM_B2) * (g * g)
        m_hat = nm / c_m
        v_hat = nv / c_v
        g_ref[...] = g
        d_ref[...] = -ADAM_LR * (m_hat / (jnp.sqrt(v_hat) + ADAM_EPS) + ADAM_WD * w_ref[...])
        nm_ref[...] = nm
        nv_ref[...] = nv

    blk = pl.BlockSpec((tr, lanes), lambda i: (i, 0))
    return pl.pallas_call(
        body, name=name, grid=(rows // tr,),
        in_specs=[pl.BlockSpec((N_DEV, tr, lanes), lambda i: (0, i, 0)), blk, blk, blk],
        out_specs=[blk] * 4, out_shape=[jax.ShapeDtypeStruct((rows, lanes), F32)] * 4,
        compiler_params=_cparams(("parallel",)),
    )(parts, w, m, v)


def kernel(x, attn_norm_g, w_in, forget_bias, fox_out_g, sb_out_g, w_out, ffn_norm_g, w_up, conv_w, conv_b, w_down, final_norm_g, loss_target, m_attn_norm_g, m_w_in, m_forget_bias, m_fox_out_g, m_sb_out_g, m_w_out, m_ffn_norm_g, m_w_up, m_conv_w, m_conv_b, m_w_down, m_final_norm_g, v_attn_norm_g, v_w_in, v_forget_bias, v_fox_out_g, v_sb_out_g, v_w_out, v_ffn_norm_g, v_w_up, v_conv_w, v_conv_b, v_w_down, v_final_norm_g):
    s = x.shape[1]
    xs = x[0]
    tq = min(ATTN_TQ, s)
    tk_fox = min(FOX_TK, s)
    tk_sb = min(SB_TK, s)
    in_shard, up_shard, out_shard, down_shard = IN_COLS // N_DEV, 2 * D_FF // N_DEV, D_MODEL // N_DEV, D_FF // N_DEV

    cw = conv_w[0]
    cw_hi = cw.astype(BF16)
    cw_lo = (cw - cw_hi.astype(F32)).astype(BF16)
    (g_in,) = _all_gather([w_in[0].astype(BF16)])
    rest = _exchange_start([w_out[0].astype(BF16), w_up[0].astype(BF16), w_down[0].astype(BF16),
                            jnp.stack([cw_hi, cw_lo])], False, "weights_rest_start")
    n_gate = QKV_W + N_GROUP_HEADS
    in_windows = _col_windows(N_DEV, in_shard, gap_at=n_gate, gap=GATE_PAD - N_GROUP_HEADS)
    up_windows = _col_windows(N_DEV, up_shard)
    w_in_p = _assemble_cols(g_in, IN_COLS_PAD, in_windows, "assemble_w_in")
    conv_b2 = conv_b.reshape(2, 1, D_FF)

    h1 = _rms_fwd(xs, attn_norm_g + rest[-1][0:1, 0:1])
    proj_h = _mm_heads(h1, w_in_p, "in_proj")
    fox_offs = (0, N_GROUP_HEADS, 2 * N_GROUP_HEADS)
    sb_first = 3 * N_GROUP_HEADS + GATE_PAD // HEAD_DIM
    sb_offs = (sb_first, sb_first + N_GROUP_HEADS, sb_first + 2 * N_GROUP_HEADS)
    f_logit = _mm_nn(h1, w_in_p[:, QKV_W:QKV_W + GATE_PAD], F32, "gate_proj")[:, :N_GROUP_HEADS]
    fv = proj_h[2 * N_GROUP_HEADS:3 * N_GROUP_HEADS]

    f_logit_h = f_logit.T.reshape(N_GROUP_HEADS, s // LANES, LANES)
    bias_h = jnp.broadcast_to(forget_bias.reshape(N_GROUP_HEADS, 1, 1), (N_GROUP_HEADS, 1, LANES))
    big_f = _forget_fwd(f_logit_h, bias_h)
    f_row = big_f.reshape(N_GROUP_HEADS, s // tk_fox, 1, tk_fox)

    fv_ones = jnp.concatenate([fv, jnp.ones_like(fv)], axis=-1)
    fk_ones = jnp.concatenate([proj_h[N_GROUP_HEADS:2 * N_GROUP_HEADS], jnp.ones_like(fv)], axis=-1)
    o_fox_h, lse = _fox_fwd(proj_h, fox_offs, fv_ones, f_row, tq, tk_fox)
    o_sb_h, sb_w = _sb_fwd(proj_h, sb_offs, min(SB_TQ, s), tk_sb)
    g_fox_h = fox_out_g.reshape(N_GROUP_HEADS, 1, HEAD_DIM)
    g_sb_h = sb_out_g.reshape(N_GROUP_HEADS, 1, HEAD_DIM)
    o_n = _group_rms_fwd(o_fox_h, o_sb_h, g_fox_h, g_sb_h)
    g_out, g_up, g_down, g_conv = _exchange_wait(rest, False, o_n, "weights_rest_wait")
    w_out_f = g_out.reshape(D_MODEL, D_MODEL)
    w_up_f = _assemble_cols(g_up, 2 * D_FF, up_windows, "assemble_w_up")
    w_down_f = g_down.reshape(D_FF, D_MODEL)
    conv_w_f = (g_conv[:, 0].astype(F32) + g_conv[:, 1].astype(F32)).transpose(1, 0, 2).reshape(3, 2 * D_FF)
    conv_w2 = conv_w_f.reshape(3, 2, D_FF).transpose(1, 0, 2)
    x1 = _mm_nn(o_n, w_out_f, F32, "out_proj", resid=xs)
    h2 = _rms_fwd(x1, ffn_norm_g)
    up = _mm_up(h2, w_up_f)
    act = _conv_gate_fwd(up, conv_w2, conv_b2)
    x2 = _mm_nn(act, w_down_f, F32, "down_proj", resid=x1, tk=1408)

    d_x2, d_x2b, dg_final, loss_part = _loss_head(x2, loss_target[0], final_norm_g.reshape(1, D_MODEL))
    d_act = _mm_nt(d_x2b, w_down_f, BF16, "d_act", tn=1408)
    dw_down = _mm_tn(act, d_x2b, "d_w_down", tm=1408)
    d_up, dcw2, dcb2 = _conv_gate_bwd(up, d_act, conv_w2, conv_b2)
    d_h2 = _mm_dup_nt(d_up, w_up_f)
    dw_up = _mm_dwup_tn(h2, d_up)
    d_x1, d_x1b, dg_ffn = _rms_bwd(x1, d_h2, ffn_norm_g, d_x2, dy_col=0, name="ffn_norm_bwd", want_bf16=True)
    d_on = _mm_nt(d_x1b, w_out_f, F32, "d_o_normed")
    dw_out = _mm_tn(o_n, d_x1b, "d_w_out")
    early = _exchange_start(
        [dw_out.astype(BF16).reshape(N_DEV, out_shard, D_MODEL),
         _split_cols(dw_up, N_DEV, up_shard, up_windows, "split_d_w_up"),
         dw_down.astype(BF16).reshape(N_DEV, down_shard, D_MODEL)], True, "grads_early_start")
    g_fox_t = g_fox_h + early[-1][0:1, 0:1]
    d_o_fox_h, dg_fox = _group_rms_bwd(o_fox_h, d_on, g_fox_t, dy_col=0, name="fox_norm_bwd")
    d_o_sb_h, dg_sb = _group_rms_bwd(o_sb_h, d_on, g_sb_h, dy_col=1, name="sb_norm_bwd")

    dfq, dfk, dfv, ksum8, qsum = _fox_bwd(proj_h, fox_offs, fk_ones, fv_ones, f_row, o_fox_h, lse, d_o_fox_h, tq, tk_fox)
    dsq, dsk, dsv = _sb_bwd(proj_h, sb_offs, sb_w, d_o_sb_h, min(SB_TQ, s), tk_sb)
    ksum = jnp.sum(ksum8, axis=2).reshape(N_GROUP_HEADS, s // LANES, LANES)
    d_f_logit_h, d_bias_h = _forget_bwd(f_logit_h, bias_h, ksum,
                                        qsum.reshape(N_GROUP_HEADS, s // LANES, LANES))
    d_f_logit = d_f_logit_h.reshape(N_GROUP_HEADS, s).T

    d_proj = _merge_dproj((dfq, dfk, dfv), d_f_logit, (dsq, dsk, dsv))
    dw_in_p = _mm_tn(h1, d_proj, "d_w_in", tn=640)
    dconv_w = dcw2.transpose(1, 0, 2).reshape(3, 2 * D_FF)
    dconv_b = dcb2.reshape(1, 2 * D_FF)
    late = _exchange_start(
        [_split_cols(dw_in_p, N_DEV, in_shard, in_windows, "split_d_w_in"),
         dconv_w.astype(BF16).reshape(3, N_DEV, up_shard).transpose(1, 0, 2)],
        True, "grads_late_start")
    d_h1 = _mm_nt(d_proj, w_in_p + late[-1][0:1, 0:1].astype(BF16), F32, "d_h1", tk=640)
    grad_x, dg_attn = _rms_bwd(xs, d_h1, attn_norm_g, d_x1, dy_col=0, name="attn_norm_bwd", want_bf16=False)

    small_shapes = [(1, D_MODEL), (1, N_GROUP_HEADS), (1, GROUP_W), (1, GROUP_W), (1, D_MODEL),
                    (1, 2 * D_FF), (D_MODEL,), (1,)]
    spack = _pack([dg_attn, d_bias_h[:, 0, 0], dg_fox, dg_sb, dg_ffn, dconv_b, dg_final, loss_part[0, 0:1]],
                  SMALL_ROWS, F32)
    (srecv,) = _grad_exchange([], spack)
    r_out, r_up, r_down = _exchange_wait(early, True, srecv, "grads_early_wait")
    r_in, r_conv = _exchange_wait(late, True, r_out, "grads_late_wait")

    big = [_sum_adamw(g, w_[0], m_[0], v_[0], "adamw_" + tag)
           for g, w_, m_, v_, tag in zip(
               (r_in, r_out, r_up, r_down, r_conv), (w_in, w_out, w_up, w_down, conv_w), (m_w_in, m_w_out, m_w_up, m_w_down, m_conv_w),
               (v_w_in, v_w_out, v_w_up, v_w_down, v_conv_w), ("w_in", "w_out", "w_up", "w_down", "conv_w"))]

    def small_pack(a_attn, a_bias, a_fox, a_sb, a_ffn, a_cb, a_fin):
        return _pack([a_attn, a_bias, a_fox, a_sb, a_ffn, a_cb, a_fin, jnp.zeros((1,), F32)], SMALL_ROWS, F32)

    small = _sum_adamw(srecv, small_pack(attn_norm_g, forget_bias, fox_out_g, sb_out_g, ffn_norm_g, conv_b, final_norm_g),
                       small_pack(m_attn_norm_g, m_forget_bias, m_fox_out_g, m_sb_out_g, m_ffn_norm_g, m_conv_b, m_final_norm_g),
                       small_pack(v_attn_norm_g, v_forget_bias, v_fox_out_g, v_sb_out_g, v_ffn_norm_g, v_conv_b, v_final_norm_g),
                       "adamw_replicated", tr=SMALL_ROWS)

    outs = []
    loss = None
    for kind in range(4):
        b_in, b_out, b_up, b_down, b_conv = (res[kind] for res in big)
        s_attn, s_bias, s_fox, s_sb, s_ffn, s_cb, s_fin, s_loss = _unpack(small[kind], small_shapes)
        if kind == 0:
            loss = s_loss[0]
        outs += [s_attn, b_in[None], s_bias, s_fox, s_sb, b_out[None], s_ffn, b_up[None], b_conv[None], s_cb,
                 b_down[None], s_fin]
    return (loss, grad_x[None], *outs)
```
